```python
import math
import jax
import jax.numpy as jnp
from jax import lax
import numpy as np

D_MODEL = 1024
BATCH = 16
SEQ = 2048
DEPTH = 4

LN_EPS = 1e-5
DN_ALPHA = (2.0 * DEPTH) ** 0.25
DN_BETA = (8.0 * DEPTH) ** -0.25
FFN_RES = 0.5
D_FF = ((8 * D_MODEL // 3 + 127) // 128) * 128
POOL_WIDTH = 3 * D_MODEL // 4
POOL_WINDOWS = (2, 4, 8, 16)
POOL_GROUPS = len(POOL_WINDOWS)
POOL_GDIM = POOL_WIDTH // POOL_GROUPS
SSD_INNER = D_MODEL
SSD_HEADDIM = 64
SSD_HEADS = SSD_INNER // SSD_HEADDIM
SSD_GROUPS = 4
SSD_HPG = SSD_HEADS // SSD_GROUPS
SSD_STATE = 128
SSD_CONV = 4
SSD_CHUNK = 128
SSD_CONV_CH = SSD_INNER + 2 * SSD_GROUPS * SSD_STATE
SSD_EPS = 1e-5
ATTN_CONFIGS = ((128, 1), (512, 4), (2048, 16))
ATTN_HEAD_DIM = 64
ATTN_GROUP_HEADS = 4
ATTN_HEADS = ATTN_GROUP_HEADS * len(ATTN_CONFIGS)
ATTN_WIDTH = ATTN_HEADS * ATTN_HEAD_DIM
ATTN_OUT = ATTN_GROUP_HEADS * ATTN_HEAD_DIM
REL_BUCKETS = 32
REL_MAX_DIST = 2048
N_BRANCH = 3
IN_SIZES = (POOL_WIDTH, SSD_INNER, SSD_CONV_CH, SSD_HEADS, ATTN_WIDTH, ATTN_WIDTH, ATTN_WIDTH, N_BRANCH * D_MODEL)
IN_SPLITS = tuple(sum(IN_SIZES[:i + 1]) for i in range(len(IN_SIZES) - 1))
W_IN_COLS = sum(IN_SIZES)
DT_OFFSET = POOL_WIDTH + SSD_INNER + SSD_CONV_CH

kernel_name = 'hybrid_pool_ssd_dilated_attn_block'


def layer_norm(x, g, b):
    xf = x.astype(jnp.float32)
    mu = xf.mean(-1, keepdims=True)
    var = jnp.square(xf - mu).mean(-1, keepdims=True)
    return ((xf - mu) * lax.rsqrt(var + LN_EPS) * g + b).astype(x.dtype)


def swiglu(x, w13, w2):
    a, g = jnp.split(x @ w13, 2, axis=-1)
    return (jax.nn.silu(a) * g) @ w2


def pool_mixer(u, pool_w, pool_b, pool_scale):
    b, s, _ = u.shape
    uf = u.astype(jnp.float32).reshape(b, s, POOL_GROUPS, POOL_GDIM)
    csum = jnp.pad(jnp.cumsum(uf, axis=1), ((0, 0), (1, 0), (0, 0), (0, 0)))
    hi = jnp.arange(1, s + 1)[:, None]
    lo = jnp.maximum(hi - jnp.array(POOL_WINDOWS)[None, :], 0)
    lower = csum[:, lo, jnp.arange(POOL_GROUPS)[None, :], :]
    mean = (csum[:, 1:] - lower) / (hi - lo).astype(jnp.float32)[None, :, :, None]
    pooled = (mean - uf).astype(u.dtype)
    y = jnp.einsum('bsgc,gcd->bsgd', pooled, pool_w) + pool_b
    return y.reshape(b, s, POOL_WIDTH) * pool_scale


def causal_dwconv(x, w, bias):
    y = lax.conv_general_dilated(x, w[:, None, :], window_strides=(1,), padding=[(SSD_CONV - 1, 0)],
                                 dimension_numbers=('NWC', 'WIO', 'NWC'), feature_group_count=x.shape[-1])
    return y + bias


def segsum(a):
    cs = jnp.cumsum(a, axis=-1)
    diff = cs[..., :, None] - cs[..., None, :]
    n = a.shape[-1]
    mask = jnp.tril(jnp.ones((n, n), dtype=bool))
    return jnp.where(mask, diff, -jnp.inf)


def ssd_scan(xh, dt, a, bm, cm):
    b, s, g, e, p = xh.shape
    n = bm.shape[-1]
    nc, l = s // SSD_CHUNK, SSD_CHUNK
    xdt = (xh * dt[..., None]).reshape(b, nc, l, g, e, p)
    adt = (dt * a).reshape(b, nc, l, g, e).transpose(0, 1, 3, 4, 2)
    bm = bm.reshape(b, nc, l, g, n)
    cm = cm.reshape(b, nc, l, g, n)
    a_cum = jnp.cumsum(adt, axis=-1)
    decay = jnp.exp(segsum(adt))
    cb = jnp.einsum('bclgn,bcsgn->bcgls', cm, bm)
    y_diag = jnp.einsum('bcgls,bcgels,bcsgep->bclgep', cb, decay, xdt)
    decay_states = jnp.exp(a_cum[..., -1:] - a_cum)
    states = jnp.einsum('bclgn,bcgel,bclgep->bcgepn', bm, decay_states, xdt)
    chunk_decay = jnp.exp(a_cum[..., -1])

    def step(h, inp):
        st, dec = inp
        return h * dec[..., None, None] + st, h

    h0 = jnp.zeros((b, g, e, p, n), jnp.float32)
    _, prev = lax.scan(step, h0, (states.transpose(1, 0, 2, 3, 4, 5), chunk_decay.transpose(1, 0, 2, 3)))
    prev = prev.transpose(1, 0, 2, 3, 4, 5)
    y_off = jnp.einsum('bclgn,bcgepn,bcgel->bclgep', cm, prev, jnp.exp(a_cum))
    return (y_diag + y_off).reshape(b, s, g, e, p)


def ssd_mixer(z, xbc, dt_raw, conv_w, conv_b, dt_bias, a_log, d_skip, norm_w):
    b, s, _ = z.shape
    xbc = jax.nn.silu(causal_dwconv(xbc, conv_w, conv_b)).astype(jnp.float32)
    xs, bm, cm = jnp.split(xbc, [SSD_INNER, SSD_INNER + SSD_GROUPS * SSD_STATE], axis=-1)
    dt = jax.nn.softplus(dt_raw.astype(jnp.float32) + dt_bias.astype(jnp.float32))
    a = -jnp.exp(a_log.astype(jnp.float32))
    xh = xs.reshape(b, s, SSD_GROUPS, SSD_HPG, SSD_HEADDIM)
    y = ssd_scan(xh, dt.reshape(b, s, SSD_GROUPS, SSD_HPG), a.reshape(SSD_GROUPS, SSD_HPG),
                 bm.reshape(b, s, SSD_GROUPS, SSD_STATE), cm.reshape(b, s, SSD_GROUPS, SSD_STATE))
    y = y + d_skip.astype(jnp.float32).reshape(SSD_GROUPS, SSD_HPG)[..., None] * xh
    y = y.reshape(b, s, SSD_INNER) * jax.nn.silu(z.astype(jnp.float32))
    yg = y.reshape(b, s, SSD_GROUPS, SSD_INNER // SSD_GROUPS)
    yg = yg * lax.rsqrt(jnp.mean(jnp.square(yg), axis=-1, keepdims=True) + SSD_EPS)
    return (yg.reshape(b, s, SSD_INNER) * norm_w).astype(z.dtype)


def t5_bucket_np(dist):
    dist = np.maximum(dist, 0)
    max_exact = REL_BUCKETS // 2
    large = max_exact + (np.log(np.maximum(dist, 1) / max_exact) / np.log(REL_MAX_DIST / max_exact)
                         * (REL_BUCKETS - max_exact)).astype(np.int32)
    large = np.minimum(large, REL_BUCKETS - 1)
    return np.where(dist < max_exact, dist, large).astype(np.int32)


def dilated_group_attention(q, k, v, bias_tab, window, dilation):
    b, s, h, dh = q.shape
    span = window // dilation
    blk = span
    sub_len = s // dilation
    n_blk = -(-sub_len // blk)
    lp = n_blk * blk

    def to_sub(t):
        t = t.reshape(b, sub_len, dilation, h, dh).transpose(0, 2, 1, 3, 4)
        return jnp.pad(t, ((0, 0), (0, 0), (0, lp - sub_len), (0, 0), (0, 0)))

    def band(t):
        tb = t.reshape(b, dilation, n_blk, blk, h, dh)
        prev = jnp.pad(tb, ((0, 0), (0, 0), (1, 0), (0, 0), (0, 0), (0, 0)))[:, :, :-1]
        return jnp.concatenate([prev, tb], axis=3)

    qb = to_sub(q).reshape(b, dilation, n_blk, blk, h, dh)
    kb = band(to_sub(k))
    vb = band(to_sub(v))
    qi = np.arange(blk)[:, None]
    kj = np.arange(2 * blk)[None, :]
    delta = qi - kj + blk
    in_band = (delta >= 0) & (delta <= span)
    has_prev = (np.arange(n_blk)[:, None, None] > 0) | (kj >= blk)[None]
    valid = jnp.asarray(in_band[None] & has_prev)
    bias = jnp.take(bias_tab, jnp.asarray(t5_bucket_np(delta * dilation)), axis=0)
    bias = bias.astype(jnp.float32).transpose(2, 0, 1)
    logits = jnp.einsum('brnqhd,brnkhd->brnhqk', qb, kb).astype(jnp.float32) * (dh ** -0.5) + bias
    logits = jnp.where(valid[None, None, :, None], logits, -jnp.inf)
    m = jnp.max(logits, axis=-1, keepdims=True)
    pexp = jnp.exp(logits - m)
    den = jnp.sum(pexp, axis=-1, keepdims=True)
    out = jnp.einsum('brnhqk,brnkhd->brnqhd', (pexp / den).astype(v.dtype), vb)
    lse = (m + jnp.log(den))[..., 0]
    out = out.reshape(b, dilation, lp, h, dh)[:, :, :sub_len].transpose(0, 2, 1, 3, 4).reshape(b, s, h, dh)
    lse = lse.transpose(0, 1, 2, 4, 3).reshape(b, dilation, lp, h)[:, :, :sub_len]
    lse = lse.transpose(0, 2, 1, 3).reshape(b, s, h)
    return out, lse


def dilated_attention(q, k, v, rel_bias):
    b, s = q.shape[:2]
    outs, lses = [], []
    for gi, (window, dilation) in enumerate(ATTN_CONFIGS):
        sl = slice(gi * ATTN_GROUP_HEADS, (gi + 1) * ATTN_GROUP_HEADS)
        o, l = dilated_group_attention(q[:, :, sl], k[:, :, sl], v[:, :, sl], rel_bias[:, sl], window, dilation)
        outs.append(o)
        lses.append(l)
    wgt = jax.nn.softmax(jnp.stack(lses, axis=0), axis=0)
    y = jnp.einsum('gbsh,gbshd->bshd', wgt, jnp.stack(outs, axis=0).astype(jnp.float32))
    return y.reshape(b, s, ATTN_OUT).astype(q.dtype)


def hybrid_mixer(x, w_in, gate_b, pool_w, pool_b, pool_scale, conv_w, conv_b, dt_bias, a_log, d_skip,
                 ssd_norm, rel_bias, p_pool, p_ssd, p_attn, w_out):
    b, s, _ = x.shape
    hcat = x @ w_in
    u_pool, z, xbc, dt_raw, q, k, v, gates = jnp.split(hcat, IN_SPLITS, axis=-1)
    y_a = pool_mixer(u_pool, pool_w, pool_b, pool_scale) @ p_pool
    y_b = ssd_mixer(z, xbc, dt_raw, conv_w, conv_b, dt_bias, a_log, d_skip, ssd_norm) @ p_ssd
    hs = (b, s, ATTN_HEADS, ATTN_HEAD_DIM)
    y_c = dilated_attention(q.reshape(hs), k.reshape(hs), v.reshape(hs), rel_bias) @ p_attn
    g = jax.nn.sigmoid(gates.reshape(b, s, N_BRANCH, D_MODEL) + gate_b)
    merged = g[:, :, 0] * y_a + g[:, :, 1] * y_b + g[:, :, 2] * y_c
    return merged @ w_out


def _fwd_setup_inputs(seed: int = 0) -> dict:
    key = jax.random.key(seed)
    ks = jax.random.split(key, 32)
    f32 = jnp.float32

    def nrm(k, shape, scale):
        return jax.random.normal(k, shape, f32) * scale

    x = jax.random.normal(ks[0], (BATCH, SEQ, D_MODEL), f32)
    col_scale = jnp.ones((W_IN_COLS,), f32).at[DT_OFFSET:DT_OFFSET + SSD_HEADS].set(0.1)
    w_in = nrm(ks[1], (DEPTH, D_MODEL, W_IN_COLS), D_MODEL ** -0.5) * col_scale
    dt0 = jnp.exp(jax.random.uniform(ks[2], (DEPTH, SSD_HEADS), f32, math.log(1e-3), math.log(1e-1)))
    return {
        'x': x,
        'ffn1_w13': nrm(ks[3], (DEPTH, D_MODEL, 2 * D_FF), D_MODEL ** -0.5),
        'ffn1_w2': nrm(ks[4], (DEPTH, D_FF, D_MODEL), D_FF ** -0.5 * DN_BETA),
        'ln1_g': 1.0 + nrm(ks[5], (DEPTH, D_MODEL), 0.02),
        'ln1_b': nrm(ks[6], (DEPTH, D_MODEL), 0.02),
        'w_in': w_in,
        'gate_b': nrm(ks[7], (DEPTH, N_BRANCH, D_MODEL), 0.1),
        'pool_w': nrm(ks[8], (DEPTH, POOL_GROUPS, POOL_GDIM, POOL_GDIM), POOL_GDIM ** -0.5),
        'pool_b': nrm(ks[9], (DEPTH, POOL_GROUPS, POOL_GDIM), 0.02),
        'pool_scale': 1.0 + nrm(ks[10], (DEPTH, POOL_WIDTH), 0.02),
        'conv_w': nrm(ks[11], (DEPTH, SSD_CONV, SSD_CONV_CH), SSD_CONV ** -0.5),
        'conv_b': nrm(ks[12], (DEPTH, SSD_CONV_CH), 0.02),
        'dt_bias': dt0 + jnp.log(-jnp.expm1(-dt0)),
        'a_log': jnp.log(jax.random.uniform(ks[13], (DEPTH, SSD_HEADS), f32, 1.0, 16.0)),
        'd_skip': 1.0 + nrm(ks[14], (DEPTH, SSD_HEADS), 0.02),
        'ssd_norm': 1.0 + nrm(ks[15], (DEPTH, SSD_INNER), 0.02),
        'rel_bias': nrm(ks[16], (REL_BUCKETS, ATTN_HEADS), 0.2),
        'p_pool': nrm(ks[17], (DEPTH, POOL_WIDTH, D_MODEL), POOL_WIDTH ** -0.5),
        'p_ssd': nrm(ks[18], (DEPTH, SSD_INNER, D_MODEL), SSD_INNER ** -0.5),
        'p_attn': nrm(ks[19], (DEPTH, ATTN_OUT, D_MODEL), ATTN_OUT ** -0.5),
        'w_out': nrm(ks[20], (DEPTH, D_MODEL, D_MODEL), D_MODEL ** -0.5 * DN_BETA),
        'ln2_g': 1.0 + nrm(ks[21], (DEPTH, D_MODEL), 0.02),
        'ln2_b': nrm(ks[22], (DEPTH, D_MODEL), 0.02),
        'ffn2_w13': nrm(ks[23], (DEPTH, D_MODEL, 2 * D_FF), D_MODEL ** -0.5),
        'ffn2_w2': nrm(ks[24], (DEPTH, D_FF, D_MODEL), D_FF ** -0.5 * DN_BETA),
        'ln3_g': 1.0 + nrm(ks[25], (DEPTH, D_MODEL), 0.02),
        'ln3_b': nrm(ks[26], (DEPTH, D_MODEL), 0.02),
    }


def _fwd_reference(x, ffn1_w13, ffn1_w2, ln1_g, ln1_b, w_in, gate_b, pool_w, pool_b, pool_scale, conv_w, conv_b,
              dt_bias, a_log, d_skip, ssd_norm, rel_bias, p_pool, p_ssd, p_attn, w_out, ln2_g, ln2_b,
              ffn2_w13, ffn2_w2, ln3_g, ln3_b):
    for i in range(DEPTH):
        x = layer_norm(DN_ALPHA * x + FFN_RES * swiglu(x, ffn1_w13[i], ffn1_w2[i]), ln1_g[i], ln1_b[i])
        mix = hybrid_mixer(x, w_in[i], gate_b[i], pool_w[i], pool_b[i], pool_scale[i], conv_w[i], conv_b[i],
                           dt_bias[i], a_log[i], d_skip[i], ssd_norm[i], rel_bias, p_pool[i], p_ssd[i],
                           p_attn[i], w_out[i])
        x = layer_norm(DN_ALPHA * x + mix, ln2_g[i], ln2_b[i])
        x = layer_norm(DN_ALPHA * x + FFN_RES * swiglu(x, ffn2_w13[i], ffn2_w2[i]), ln3_g[i], ln3_b[i])
    return x


import jax as _jax
import jax.numpy as _jnp

TWIN_FORMAT = 'train_step'
FWD_PARAMS = ['x', 'ffn1_w13', 'ffn1_w2', 'ln1_g', 'ln1_b', 'w_in', 'gate_b', 'pool_w', 'pool_b', 'pool_scale', 'conv_w', 'conv_b', 'dt_bias', 'a_log', 'd_skip', 'ssd_norm', 'rel_bias', 'p_pool', 'p_ssd', 'p_attn', 'w_out', 'ln2_g', 'ln2_b', 'ffn2_w13', 'ffn2_w2', 'ln3_g', 'ln3_b']
TWIN_WEIGHTS = ['ffn1_w13', 'ffn1_w2', 'ln1_g', 'ln1_b', 'w_in', 'gate_b', 'pool_w', 'pool_b', 'pool_scale', 'conv_w', 'conv_b', 'dt_bias', 'a_log', 'd_skip', 'ssd_norm', 'rel_bias', 'p_pool', 'p_ssd', 'p_attn', 'w_out', 'ln2_g', 'ln2_b', 'ffn2_w13', 'ffn2_w2', 'ln3_g', 'ln3_b']
TWIN_DIFF_INPUT = 'x'
TWIN_INPUTS = ['x', 'ffn1_w13', 'ffn1_w2', 'ln1_g', 'ln1_b', 'w_in', 'gate_b', 'pool_w', 'pool_b', 'pool_scale', 'conv_w', 'conv_b', 'dt_bias', 'a_log', 'd_skip', 'ssd_norm', 'rel_bias', 'p_pool', 'p_ssd', 'p_attn', 'w_out', 'ln2_g', 'ln2_b', 'ffn2_w13', 'ffn2_w2', 'ln3_g', 'ln3_b', 'loss_target', 'm_ffn1_w13', 'm_ffn1_w2', 'm_ln1_g', 'm_ln1_b', 'm_w_in', 'm_gate_b', 'm_pool_w', 'm_pool_b', 'm_pool_scale', 'm_conv_w', 'm_conv_b', 'm_dt_bias', 'm_a_log', 'm_d_skip', 'm_ssd_norm', 'm_rel_bias', 'm_p_pool', 'm_p_ssd', 'm_p_attn', 'm_w_out', 'm_ln2_g', 'm_ln2_b', 'm_ffn2_w13', 'm_ffn2_w2', 'm_ln3_g', 'm_ln3_b', 'v_ffn1_w13', 'v_ffn1_w2', 'v_ln1_g', 'v_ln1_b', 'v_w_in', 'v_gate_b', 'v_pool_w', 'v_pool_b', 'v_pool_scale', 'v_conv_w', 'v_conv_b', 'v_dt_bias', 'v_a_log', 'v_d_skip', 'v_ssd_norm', 'v_rel_bias', 'v_p_pool', 'v_p_ssd', 'v_p_attn', 'v_w_out', 'v_ln2_g', 'v_ln2_b', 'v_ffn2_w13', 'v_ffn2_w2', 'v_ln3_g', 'v_ln3_b']
TWIN_OUTPUTS = ['loss', 'grad_x', 'grad_ffn1_w13', 'grad_ffn1_w2', 'grad_ln1_g', 'grad_ln1_b', 'grad_w_in', 'grad_gate_b', 'grad_pool_w', 'grad_pool_b', 'grad_pool_scale', 'grad_conv_w', 'grad_conv_b', 'grad_dt_bias', 'grad_a_log', 'grad_d_skip', 'grad_ssd_norm', 'grad_rel_bias', 'grad_p_pool', 'grad_p_ssd', 'grad_p_attn', 'grad_w_out', 'grad_ln2_g', 'grad_ln2_b', 'grad_ffn2_w13', 'grad_ffn2_w2', 'grad_ln3_g', 'grad_ln3_b', 'delta_ffn1_w13', 'delta_ffn1_w2', 'delta_ln1_g', 'delta_ln1_b', 'delta_w_in', 'delta_gate_b', 'delta_pool_w', 'delta_pool_b', 'delta_pool_scale', 'delta_conv_w', 'delta_conv_b', 'delta_dt_bias', 'delta_a_log', 'delta_d_skip', 'delta_ssd_norm', 'delta_rel_bias', 'delta_p_pool', 'delta_p_ssd', 'delta_p_attn', 'delta_w_out', 'delta_ln2_g', 'delta_ln2_b', 'delta_ffn2_w13', 'delta_ffn2_w2', 'delta_ln3_g', 'delta_ln3_b', 'new_m_ffn1_w13', 'new_m_ffn1_w2', 'new_m_ln1_g', 'new_m_ln1_b', 'new_m_w_in', 'new_m_gate_b', 'new_m_pool_w', 'new_m_pool_b', 'new_m_pool_scale', 'new_m_conv_w', 'new_m_conv_b', 'new_m_dt_bias', 'new_m_a_log', 'new_m_d_skip', 'new_m_ssd_norm', 'new_m_rel_bias', 'new_m_p_pool', 'new_m_p_ssd', 'new_m_p_attn', 'new_m_w_out', 'new_m_ln2_g', 'new_m_ln2_b', 'new_m_ffn2_w13', 'new_m_ffn2_w2', 'new_m_ln3_g', 'new_m_ln3_b', 'new_v_ffn1_w13', 'new_v_ffn1_w2', 'new_v_ln1_g', 'new_v_ln1_b', 'new_v_w_in', 'new_v_gate_b', 'new_v_pool_w', 'new_v_pool_b', 'new_v_pool_scale', 'new_v_conv_w', 'new_v_conv_b', 'new_v_dt_bias', 'new_v_a_log', 'new_v_d_skip', 'new_v_ssd_norm', 'new_v_rel_bias', 'new_v_p_pool', 'new_v_p_ssd', 'new_v_p_attn', 'new_v_w_out', 'new_v_ln2_g', 'new_v_ln2_b', 'new_v_ffn2_w13', 'new_v_ffn2_w2', 'new_v_ln3_g', 'new_v_ln3_b']
TWIN_LEAF_KINDS = {'loss': 'loss', 'grad_x': 'grad_x', 'grad_ffn1_w13': 'grad_w', 'grad_ffn1_w2': 'grad_w', 'grad_ln1_g': 'grad_w', 'grad_ln1_b': 'grad_w', 'grad_w_in': 'grad_w', 'grad_gate_b': 'grad_w', 'grad_pool_w': 'grad_w', 'grad_pool_b': 'grad_w', 'grad_pool_scale': 'grad_w', 'grad_conv_w': 'grad_w', 'grad_conv_b': 'grad_w', 'grad_dt_bias': 'grad_w', 'grad_a_log': 'grad_w', 'grad_d_skip': 'grad_w', 'grad_ssd_norm': 'grad_w', 'grad_rel_bias': 'grad_w', 'grad_p_pool': 'grad_w', 'grad_p_ssd': 'grad_w', 'grad_p_attn': 'grad_w', 'grad_w_out': 'grad_w', 'grad_ln2_g': 'grad_w', 'grad_ln2_b': 'grad_w', 'grad_ffn2_w13': 'grad_w', 'grad_ffn2_w2': 'grad_w', 'grad_ln3_g': 'grad_w', 'grad_ln3_b': 'grad_w', 'delta_ffn1_w13': 'delta_w', 'delta_ffn1_w2': 'delta_w', 'delta_ln1_g': 'delta_w', 'delta_ln1_b': 'delta_w', 'delta_w_in': 'delta_w', 'delta_gate_b': 'delta_w', 'delta_pool_w': 'delta_w', 'delta_pool_b': 'delta_w', 'delta_pool_scale': 'delta_w', 'delta_conv_w': 'delta_w', 'delta_conv_b': 'delta_w', 'delta_dt_bias': 'delta_w', 'delta_a_log': 'delta_w', 'delta_d_skip': 'delta_w', 'delta_ssd_norm': 'delta_w', 'delta_rel_bias': 'delta_w', 'delta_p_pool': 'delta_w', 'delta_p_ssd': 'delta_w', 'delta_p_attn': 'delta_w', 'delta_w_out': 'delta_w', 'delta_ln2_g': 'delta_w', 'delta_ln2_b': 'delta_w', 'delta_ffn2_w13': 'delta_w', 'delta_ffn2_w2': 'delta_w', 'delta_ln3_g': 'delta_w', 'delta_ln3_b': 'delta_w', 'new_m_ffn1_w13': 'new_m', 'new_m_ffn1_w2': 'new_m', 'new_m_ln1_g': 'new_m', 'new_m_ln1_b': 'new_m', 'new_m_w_in': 'new_m', 'new_m_gate_b': 'new_m', 'new_m_pool_w': 'new_m', 'new_m_pool_b': 'new_m', 'new_m_pool_scale': 'new_m', 'new_m_conv_w': 'new_m', 'new_m_conv_b': 'new_m', 'new_m_dt_bias': 'new_m', 'new_m_a_log': 'new_m', 'new_m_d_skip': 'new_m', 'new_m_ssd_norm': 'new_m', 'new_m_rel_bias': 'new_m', 'new_m_p_pool': 'new_m', 'new_m_p_ssd': 'new_m', 'new_m_p_attn': 'new_m', 'new_m_w_out': 'new_m', 'new_m_ln2_g': 'new_m', 'new_m_ln2_b': 'new_m', 'new_m_ffn2_w13': 'new_m', 'new_m_ffn2_w2': 'new_m', 'new_m_ln3_g': 'new_m', 'new_m_ln3_b': 'new_m', 'new_v_ffn1_w13': 'new_v', 'new_v_ffn1_w2': 'new_v', 'new_v_ln1_g': 'new_v', 'new_v_ln1_b': 'new_v', 'new_v_w_in': 'new_v', 'new_v_gate_b': 'new_v', 'new_v_pool_w': 'new_v', 'new_v_pool_b': 'new_v', 'new_v_pool_scale': 'new_v', 'new_v_conv_w': 'new_v', 'new_v_conv_b': 'new_v', 'new_v_dt_bias': 'new_v', 'new_v_a_log': 'new_v', 'new_v_d_skip': 'new_v', 'new_v_ssd_norm': 'new_v', 'new_v_rel_bias': 'new_v', 'new_v_p_pool': 'new_v', 'new_v_p_ssd': 'new_v', 'new_v_p_attn': 'new_v', 'new_v_w_out': 'new_v', 'new_v_ln2_g': 'new_v', 'new_v_ln2_b': 'new_v', 'new_v_ffn2_w13': 'new_v', 'new_v_ffn2_w2': 'new_v', 'new_v_ln3_g': 'new_v', 'new_v_ln3_b': 'new_v'}


def _forward(args):
    return _fwd_reference(*[args[k] for k in FWD_PARAMS])


def _output_shape():
    out = _jax.eval_shape(lambda: _forward(_fwd_setup_inputs(0)))
    return out.shape, out.dtype

N_MICROBATCH = 1
ADAM_LR = 0.001
ADAM_B1 = 0.9
ADAM_B2 = 0.999
ADAM_EPS = 1e-08
ADAM_WD = 0.01
ADAM_STEP = 10
PER_EXAMPLE_BATCH_AXIS = {'x': 0, 'loss_target': 0}
SHARED_INPUTS = []
_WEIGHT_DTYPES = {'ffn1_w13': _jnp.float32, 'ffn1_w2': _jnp.float32, 'ln1_g': _jnp.float32, 'ln1_b': _jnp.float32, 'w_in': _jnp.float32, 'gate_b': _jnp.float32, 'pool_w': _jnp.float32, 'pool_b': _jnp.float32, 'pool_scale': _jnp.float32, 'conv_w': _jnp.float32, 'conv_b': _jnp.float32, 'dt_bias': _jnp.float32, 'a_log': _jnp.float32, 'd_skip': _jnp.float32, 'ssd_norm': _jnp.float32, 'rel_bias': _jnp.float32, 'p_pool': _jnp.float32, 'p_ssd': _jnp.float32, 'p_attn': _jnp.float32, 'w_out': _jnp.float32, 'ln2_g': _jnp.float32, 'ln2_b': _jnp.float32, 'ffn2_w13': _jnp.float32, 'ffn2_w2': _jnp.float32, 'ln3_g': _jnp.float32, 'ln3_b': _jnp.float32}
MOMENT_SCALE = {'ffn1_w13': 8.405787e-03, 'ffn1_w2': 3.263708e-02, 'ln1_g': 9.420522e-01, 'ln1_b': 4.346470e-01, 'w_in': 1.464375e-02, 'gate_b': 7.300460e-03, 'pool_w': 2.462570e-02, 'pool_b': 6.679474e-02, 'pool_scale': 2.426917e-02, 'conv_w': 1.819264e-02, 'conv_b': 3.017046e-02, 'dt_bias': 3.128584e-02, 'a_log': 6.950629e-02, 'd_skip': 1.507418e-01, 'ssd_norm': 2.508257e-02, 'rel_bias': 1.305263e-02, 'p_pool': 2.128188e-02, 'p_ssd': 2.546453e-02, 'p_attn': 4.838244e-03, 'w_out': 7.890810e-02, 'ln2_g': 9.861120e-01, 'ln2_b': 4.417359e-01, 'ffn2_w13': 8.282770e-03, 'ffn2_w2': 3.217082e-02, 'ln3_g': 1.612320e+01, 'ln3_b': 1.221810e+00}


def _to_microbatches(a, axis):
    t = _jnp.moveaxis(a, axis, 0)
    t = t.reshape((N_MICROBATCH, t.shape[0] // N_MICROBATCH) + t.shape[1:])
    return _jnp.moveaxis(t, 1, axis + 1)


def setup_inputs(seed: int = 0) -> dict:
    inp = _fwd_setup_inputs(seed)
    key = _jax.random.fold_in(_jax.random.key(seed), 7919)
    shape, _ = _output_shape()
    out = dict(inp)
    out["loss_target"] = _jax.random.normal(_jax.random.fold_in(key, 0), shape, _jnp.float32)
    for i, name in enumerate(TWIN_WEIGHTS):
        w = inp[name].astype(_jnp.float32)
        if MOMENT_SCALE is None:
            s = _jnp.sqrt(_jnp.mean(_jnp.square(w)) + 1e-30)
        else:
            s = MOMENT_SCALE[name]
        km, kv = _jax.random.split(_jax.random.fold_in(key, i + 1))
        out[name] = w
        out["m_" + name] = s * _jax.random.normal(km, w.shape, _jnp.float32)
        out["v_" + name] = (s * s) * _jax.random.uniform(kv, w.shape, _jnp.float32, 0.5, 1.5)
    if N_MICROBATCH > 1:
        for name, axis in PER_EXAMPLE_BATCH_AXIS.items():
            out[name] = _to_microbatches(out[name], axis)
    return {'x': out['x'], 'ffn1_w13': out['ffn1_w13'], 'ffn1_w2': out['ffn1_w2'], 'ln1_g': out['ln1_g'], 'ln1_b': out['ln1_b'], 'w_in': out['w_in'], 'gate_b': out['gate_b'], 'pool_w': out['pool_w'], 'pool_b': out['pool_b'], 'pool_scale': out['pool_scale'], 'conv_w': out['conv_w'], 'conv_b': out['conv_b'], 'dt_bias': out['dt_bias'], 'a_log': out['a_log'], 'd_skip': out['d_skip'], 'ssd_norm': out['ssd_norm'], 'rel_bias': out['rel_bias'], 'p_pool': out['p_pool'], 'p_ssd': out['p_ssd'], 'p_attn': out['p_attn'], 'w_out': out['w_out'], 'ln2_g': out['ln2_g'], 'ln2_b': out['ln2_b'], 'ffn2_w13': out['ffn2_w13'], 'ffn2_w2': out['ffn2_w2'], 'ln3_g': out['ln3_g'], 'ln3_b': out['ln3_b'], 'loss_target': out['loss_target'], 'm_ffn1_w13': out['m_ffn1_w13'], 'm_ffn1_w2': out['m_ffn1_w2'], 'm_ln1_g': out['m_ln1_g'], 'm_ln1_b': out['m_ln1_b'], 'm_w_in': out['m_w_in'], 'm_gate_b': out['m_gate_b'], 'm_pool_w': out['m_pool_w'], 'm_pool_b': out['m_pool_b'], 'm_pool_scale': out['m_pool_scale'], 'm_conv_w': out['m_conv_w'], 'm_conv_b': out['m_conv_b'], 'm_dt_bias': out['m_dt_bias'], 'm_a_log': out['m_a_log'], 'm_d_skip': out['m_d_skip'], 'm_ssd_norm': out['m_ssd_norm'], 'm_rel_bias': out['m_rel_bias'], 'm_p_pool': out['m_p_pool'], 'm_p_ssd': out['m_p_ssd'], 'm_p_attn': out['m_p_attn'], 'm_w_out': out['m_w_out'], 'm_ln2_g': out['m_ln2_g'], 'm_ln2_b': out['m_ln2_b'], 'm_ffn2_w13': out['m_ffn2_w13'], 'm_ffn2_w2': out['m_ffn2_w2'], 'm_ln3_g': out['m_ln3_g'], 'm_ln3_b': out['m_ln3_b'], 'v_ffn1_w13': out['v_ffn1_w13'], 'v_ffn1_w2': out['v_ffn1_w2'], 'v_ln1_g': out['v_ln1_g'], 'v_ln1_b': out['v_ln1_b'], 'v_w_in': out['v_w_in'], 'v_gate_b': out['v_gate_b'], 'v_pool_w': out['v_pool_w'], 'v_pool_b': out['v_pool_b'], 'v_pool_scale': out['v_pool_scale'], 'v_conv_w': out['v_conv_w'], 'v_conv_b': out['v_conv_b'], 'v_dt_bias': out['v_dt_bias'], 'v_a_log': out['v_a_log'], 'v_d_skip': out['v_d_skip'], 'v_ssd_norm': out['v_ssd_norm'], 'v_rel_bias': out['v_rel_bias'], 'v_p_pool': out['v_p_pool'], 'v_p_ssd': out['v_p_ssd'], 'v_p_attn': out['v_p_attn'], 'v_w_out': out['v_w_out'], 'v_ln2_g': out['v_ln2_g'], 'v_ln2_b': out['v_ln2_b'], 'v_ffn2_w13': out['v_ffn2_w13'], 'v_ffn2_w2': out['v_ffn2_w2'], 'v_ln3_g': out['v_ln3_g'], 'v_ln3_b': out['v_ln3_b']}


def _loss(weights, diff, rest, loss_target):
    with _jax.named_scope("forward"):
        args = {**rest, TWIN_DIFF_INPUT: diff, **{k: w.astype(_WEIGHT_DTYPES[k]) for k, w in weights.items()}}
        y = _forward(args)
    with _jax.named_scope("loss_head"):
        err = _jnp.square(y.astype(_jnp.float32) - loss_target)
        return 0.5 * _jnp.sum(_jnp.mean(err, axis=-1)) if err.ndim else 0.5 * err


def _adamw(w, g, m, v):
    m = ADAM_B1 * m + (1.0 - ADAM_B1) * g
    v = ADAM_B2 * v + (1.0 - ADAM_B2) * _jnp.square(g)
    m_hat = m / (1.0 - ADAM_B1 ** ADAM_STEP)
    v_hat = v / (1.0 - ADAM_B2 ** ADAM_STEP)
    delta = -ADAM_LR * (m_hat / (_jnp.sqrt(v_hat) + ADAM_EPS) + ADAM_WD * w)
    return delta, m, v


def reference(x, ffn1_w13, ffn1_w2, ln1_g, ln1_b, w_in, gate_b, pool_w, pool_b, pool_scale, conv_w, conv_b, dt_bias, a_log, d_skip, ssd_norm, rel_bias, p_pool, p_ssd, p_attn, w_out, ln2_g, ln2_b, ffn2_w13, ffn2_w2, ln3_g, ln3_b, loss_target, m_ffn1_w13, m_ffn1_w2, m_ln1_g, m_ln1_b, m_w_in, m_gate_b, m_pool_w, m_pool_b, m_pool_scale, m_conv_w, m_conv_b, m_dt_bias, m_a_log, m_d_skip, m_ssd_norm, m_rel_bias, m_p_pool, m_p_ssd, m_p_attn, m_w_out, m_ln2_g, m_ln2_b, m_ffn2_w13, m_ffn2_w2, m_ln3_g, m_ln3_b, v_ffn1_w13, v_ffn1_w2, v_ln1_g, v_ln1_b, v_w_in, v_gate_b, v_pool_w, v_pool_b, v_pool_scale, v_conv_w, v_conv_b, v_dt_bias, v_a_log, v_d_skip, v_ssd_norm, v_rel_bias, v_p_pool, v_p_ssd, v_p_attn, v_w_out, v_ln2_g, v_ln2_b, v_ffn2_w13, v_ffn2_w2, v_ln3_g, v_ln3_b):
    given = dict(x=x, ffn1_w13=ffn1_w13, ffn1_w2=ffn1_w2, ln1_g=ln1_g, ln1_b=ln1_b, w_in=w_in, gate_b=gate_b, pool_w=pool_w, pool_b=pool_b, pool_scale=pool_scale, conv_w=conv_w, conv_b=conv_b, dt_bias=dt_bias, a_log=a_log, d_skip=d_skip, ssd_norm=ssd_norm, rel_bias=rel_bias, p_pool=p_pool, p_ssd=p_ssd, p_attn=p_attn, w_out=w_out, ln2_g=ln2_g, ln2_b=ln2_b, ffn2_w13=ffn2_w13, ffn2_w2=ffn2_w2, ln3_g=ln3_g, ln3_b=ln3_b, loss_target=loss_target, m_ffn1_w13=m_ffn1_w13, m_ffn1_w2=m_ffn1_w2, m_ln1_g=m_ln1_g, m_ln1_b=m_ln1_b, m_w_in=m_w_in, m_gate_b=m_gate_b, m_pool_w=m_pool_w, m_pool_b=m_pool_b, m_pool_scale=m_pool_scale, m_conv_w=m_conv_w, m_conv_b=m_conv_b, m_dt_bias=m_dt_bias, m_a_log=m_a_log, m_d_skip=m_d_skip, m_ssd_norm=m_ssd_norm, m_rel_bias=m_rel_bias, m_p_pool=m_p_pool, m_p_ssd=m_p_ssd, m_p_attn=m_p_attn, m_w_out=m_w_out, m_ln2_g=m_ln2_g, m_ln2_b=m_ln2_b, m_ffn2_w13=m_ffn2_w13, m_ffn2_w2=m_ffn2_w2, m_ln3_g=m_ln3_g, m_ln3_b=m_ln3_b, v_ffn1_w13=v_ffn1_w13, v_ffn1_w2=v_ffn1_w2, v_ln1_g=v_ln1_g, v_ln1_b=v_ln1_b, v_w_in=v_w_in, v_gate_b=v_gate_b, v_pool_w=v_pool_w, v_pool_b=v_pool_b, v_pool_scale=v_pool_scale, v_conv_w=v_conv_w, v_conv_b=v_conv_b, v_dt_bias=v_dt_bias, v_a_log=v_a_log, v_d_skip=v_d_skip, v_ssd_norm=v_ssd_norm, v_rel_bias=v_rel_bias, v_p_pool=v_p_pool, v_p_ssd=v_p_ssd, v_p_attn=v_p_attn, v_w_out=v_w_out, v_ln2_g=v_ln2_g, v_ln2_b=v_ln2_b, v_ffn2_w13=v_ffn2_w13, v_ffn2_w2=v_ffn2_w2, v_ln3_g=v_ln3_g, v_ln3_b=v_ln3_b)
    weights = {n: given[n] for n in TWIN_WEIGHTS}
    shared = {n: given[n] for n in SHARED_INPUTS}
    per_example = {n: given[n] for n in ['x']}
    grad_fn = _jax.value_and_grad(_loss, argnums=(0, 1))

    def one_microbatch(ex, loss_target):
        ex = dict(ex)
        diff = ex.pop(TWIN_DIFF_INPUT)
        return grad_fn(weights, diff, {**shared, **ex}, loss_target)

    if N_MICROBATCH == 1:
        loss, (grad_w, grad_x) = one_microbatch(per_example, given["loss_target"])
    else:
        def body(carry, xs):
            loss_sum, grad_sum = carry
            l_k, (gw_k, gx_k) = one_microbatch(xs[0], xs[1])
            with _jax.named_scope("update"):
                return (loss_sum + l_k, _jax.tree.map(_jnp.add, grad_sum, gw_k)), gx_k

        init = (_jnp.zeros((), _jnp.float32), _jax.tree.map(_jnp.zeros_like, weights))
        (loss, grad_w), grad_x = _jax.lax.scan(body, init, (per_example, given["loss_target"]))
    with _jax.named_scope("update"):
        delta_w, new_m, new_v = {}, {}, {}
        for n in TWIN_WEIGHTS:
            delta_w[n], new_m[n], new_v[n] = _adamw(weights[n], grad_w[n], given["m_" + n], given["v_" + n])
    return (loss, grad_x, *[grad_w[n] for n in TWIN_WEIGHTS], *[delta_w[n] for n in TWIN_WEIGHTS],
            *[new_m[n] for n in TWIN_WEIGHTS], *[new_v[n] for n in TWIN_WEIGHTS])
```

```python
import functools

import numpy as np
import jax
import jax.numpy as jnp
from jax import lax
from jax.experimental import pallas as pl
from jax.experimental.pallas import tpu as pltpu

F32 = jnp.float32
BF16 = jnp.bfloat16
_MXU = jnp.bfloat16
_VMEM_LIMIT = 56 * 1024 * 1024

S = 2048
D = 1024
NL = 4
DFF = 2816
LN_EPS = 1e-5
SSD_EPS = 1e-5
ALPHA = (2.0 * NL) ** 0.25
POOLW = 768
POOL_WINDOWS = (2, 4, 8, 16)
POOL_GDIM = 192
CH = 128
ATTN_DILS = (1, 4, 16)
HC = 9728
O_U, O_Z, O_XBC, O_Q, O_K, O_V, O_G, O_DT = 0, 768, 1792, 3840, 4608, 5376, 6144, 9216

ADAM_LR, ADAM_B1, ADAM_B2, ADAM_EPS, ADAM_WD, ADAM_STEP = 0.001, 0.9, 0.999, 1e-08, 0.01, 10

WEIGHTS = ['ffn1_w13', 'ffn1_w2', 'ln1_g', 'ln1_b', 'w_in', 'gate_b', 'pool_w', 'pool_b', 'pool_scale', 'conv_w',
           'conv_b', 'dt_bias', 'a_log', 'd_skip', 'ssd_norm', 'rel_bias', 'p_pool', 'p_ssd', 'p_attn', 'w_out',
           'ln2_g', 'ln2_b', 'ffn2_w13', 'ffn2_w2', 'ln3_g', 'ln3_b']
BIG = ['ffn1_w13', 'ffn1_w2', 'w_in', 'p_pool', 'p_ssd', 'p_attn', 'w_out', 'ffn2_w13', 'ffn2_w2']
COL_SHARDED = {'ffn1_w13', 'ffn2_w13', 'w_in', 'p_pool', 'p_attn'}
SMALL = [n for n in WEIGHTS if n not in BIG]


def _pcall(body, **kw):
    return pl.pallas_call(body, **kw)


def _cp(sem=None):
    return pltpu.CompilerParams(dimension_semantics=sem, vmem_limit_bytes=_VMEM_LIMIT)


def _pick(n, cands):
    for c in cands:
        if n % c == 0:
            return c
    raise ValueError(f"no tile for {n}")


def _mm(a, b, *, ta=False, tb=False, add=None, out_dtype=F32, name):
    if ta:
        K, M = a.shape
    else:
        M, K = a.shape
    if tb:
        N, K2 = b.shape
    else:
        K2, N = b.shape
    assert K == K2, (a.shape, b.shape, ta, tb)
    tm = _pick(M, (1024, 512, 256, 128))
    tn = _pick(N, (1024, 512, 256, 128))
    tk = _pick(K, (512, 256, 128))
    nk = K // tk
    a_spec = pl.BlockSpec((tk, tm), lambda i, j, k: (k, i)) if ta else pl.BlockSpec((tm, tk), lambda i, j, k: (i, k))
    b_spec = pl.BlockSpec((tn, tk), lambda i, j, k: (j, k)) if tb else pl.BlockSpec((tk, tn), lambda i, j, k: (k, j))
    dims = (((0 if ta else 1,), (1 if tb else 0,)), ((), ()))
    has_add = add is not None

    def body(*refs):
        if has_add:
            a_ref, b_ref, add_ref, o_ref, acc = refs
        else:
            a_ref, b_ref, o_ref, acc = refs
        k = pl.program_id(2)

        @pl.when(k == 0)
        def _():
            acc[...] = jnp.zeros_like(acc)

        acc[...] += lax.dot_general(a_ref[...].astype(_MXU), b_ref[...].astype(_MXU), dims,
                                    preferred_element_type=F32)

        @pl.when(k == nk - 1)
        def _():
            r = acc[...]
            if has_add:
                r = r + add_ref[...]
            o_ref[...] = r.astype(out_dtype)

    in_specs = [a_spec, b_spec]
    args = [a, b]
    if has_add:
        in_specs.append(pl.BlockSpec((tm, tn), lambda i, j, k: (i, j)))
        args.append(add)
    return _pcall(
        body, name=name, grid=(M // tm, N // tn, nk), in_specs=in_specs,
        out_specs=pl.BlockSpec((tm, tn), lambda i, j, k: (i, j)),
        out_shape=jax.ShapeDtypeStruct((M, N), out_dtype),
        scratch_shapes=[pltpu.VMEM((tm, tn), F32)],
        compiler_params=_cp(("parallel", "parallel", "arbitrary")),
    )(*args)


def _store(ref, val):
    if isinstance(val, (list, tuple)):
        off = 0
        for p in val:
            w = p.shape[1]
            ref[:, off:off + w] = p.astype(ref.dtype)
            off += w
    else:
        ref[...] = val.astype(ref.dtype)


def _acc_store(ref, val, first):
    pieces = val if isinstance(val, (list, tuple)) else [val]
    off = 0
    for p in pieces:
        w = p.shape[1]

        @pl.when(first)
        def _(p=p, off=off, w=w):
            ref[:, off:off + w] = p

        @pl.when(jnp.logical_not(first))
        def _(p=p, off=off, w=w):
            ref[:, off:off + w] += p

        off += w


def _rowwise(fn, tiled, full, out_tiled, out_acc, *, name, tm=256):
    arrs, specs = [], []
    for t in tiled:
        arr, w, cb = t if isinstance(t, tuple) else (t, t.shape[1], 0)
        arrs.append(arr)
        specs.append(pl.BlockSpec((tm, w), functools.partial(lambda i, cb: (i, cb), cb=cb)))
    R = arrs[0].shape[0]
    assert R % tm == 0
    for f in full:
        arrs.append(f)
        specs.append(pl.BlockSpec(f.shape, functools.partial(lambda i, nd: (0,) * nd, nd=f.ndim)))
    nt, nf, no = len(tiled), len(full), len(out_tiled)

    def body(*refs):
        tv = [r[...] for r in refs[:nt]]
        fv = [r[...] for r in refs[nt:nt + nf]]
        ot, oa = fn(tv, fv)
        for r, v in zip(refs[nt + nf:nt + nf + no], ot):
            _store(r, v)
        first = pl.program_id(0) == 0
        for r, v in zip(refs[nt + nf + no:], oa):
            _acc_store(r, v, first)

    out_shape = [jax.ShapeDtypeStruct((R, c), dt) for c, dt in out_tiled]
    out_specs = [pl.BlockSpec((tm, c), lambda i: (i, 0)) for c, _ in out_tiled]
    for shp in out_acc:
        out_shape.append(jax.ShapeDtypeStruct(shp, F32))
        out_specs.append(pl.BlockSpec(shp, lambda i: (0, 0)))
    return _pcall(body, name=name, grid=(R // tm,), in_specs=specs, out_specs=out_specs, out_shape=out_shape,
                  compiler_params=_cp(("arbitrary",)))(*arrs)


def _sumslots(x, out_dtype, name):
    n, R, C = x.shape
    tr = _pick(R, (256, 128, 64, 32, 16, 8))

    def body(x_ref, o_ref):
        acc = x_ref[0].astype(F32)
        for s in range(1, n):
            acc = acc + x_ref[s].astype(F32)
        o_ref[...] = acc.astype(out_dtype)

    return _pcall(body, name=name, grid=(R // tr,), in_specs=[pl.BlockSpec((n, tr, C), lambda i: (0, i, 0))],
                  out_specs=pl.BlockSpec((tr, C), lambda i: (i, 0)), out_shape=jax.ShapeDtypeStruct((R, C), out_dtype),
                  compiler_params=_cp(("parallel",)))(x)


def _group(group):
    x, y, c = lax.axis_index("x"), lax.axis_index("y"), lax.axis_index("c")
    if group == "chips":
        me = 2 * x + y
        peers = [((1 - x, y, c), 2 * (1 - x) + y), ((x, 1 - y, c), 2 * x + 1 - y), ((1 - x, 1 - y, c), 2 * (1 - x) + 1 - y)]
    else:
        me = c
        peers = [((x, y, 1 - c), 1 - c)]
    return me, peers


def _exchange(arrs, group, scatter, name):
    n = 4 if group == "chips" else 2
    k = len(arrs)
    npeer = n - 1

    def body(*refs):
        ins, outs = refs[:k], refs[k:2 * k]
        send_sems, recv_sems, loc_sems = refs[2 * k:]
        me, peers = _group(group)
        started = []
        for i in range(k):
            src_me = ins[i].at[me] if scatter else ins[i]
            loc = pltpu.make_async_copy(src_me, outs[i].at[me], loc_sems.at[i])
            loc.start()
            started.append(loc)
        remote = []
        for i in range(k):
            for p, (dev, slot) in enumerate(peers):
                src = ins[i].at[slot] if scatter else ins[i]
                cp = pltpu.make_async_remote_copy(src_ref=src, dst_ref=outs[i].at[me], send_sem=send_sems.at[i, p],
                                                  recv_sem=recv_sems.at[i, p], device_id=dev,
                                                  device_id_type=pl.DeviceIdType.MESH)
                cp.start()
                remote.append(cp)
        for cp in remote:
            cp.wait_recv()
        for cp in remote:
            cp.wait_send()
        for loc in started:
            loc.wait()

    any_spec = pl.BlockSpec(memory_space=pl.ANY)
    out_shape = [jax.ShapeDtypeStruct(a.shape if scatter else (n,) + a.shape, a.dtype) for a in arrs]
    return _pcall(body, name=name, in_specs=[any_spec] * k, out_specs=[any_spec] * k, out_shape=out_shape,
                  scratch_shapes=[pltpu.SemaphoreType.DMA((k, npeer)), pltpu.SemaphoreType.DMA((k, npeer)),
                                  pltpu.SemaphoreType.DMA((k,))])(*arrs)


def _silu(x):
    return x * jax.nn.sigmoid(x)


def _ln(r, g, b):
    mu = jnp.mean(r, -1, keepdims=True)
    xc = r - mu
    var = jnp.mean(xc * xc, -1, keepdims=True)
    return xc * lax.rsqrt(var + LN_EPS) * g + b


def _softplus(x):
    return jnp.maximum(x, 0.0) + jnp.log1p(jnp.exp(-jnp.abs(x)))


def _res_ln_fwd(x, y, g, b, res, name):
    def fn(tv, fv):
        r = ALPHA * tv[0] + res * tv[1]
        return [r, _ln(r, fv[0], fv[1])], []
    return _rowwise(fn, [x, y], [g, b], [(D, F32), (D, F32)], [], name=name)


def _ln_bwd(r, g, b, dout, res, name):
    def fn(tv, fv):
        _, vjp = jax.vjp(_ln, tv[0], fv[0], fv[1])
        dr, dg, db = vjp(tv[1])
        return [ALPHA * dr, res * dr], [dg, db]
    return _rowwise(fn, [r, dout], [g, b], [(D, F32), (D, F32)], [(1, D), (1, D)], name=name)


def _swiglu_act(h, name):
    def fn(tv, fv):
        return [_silu(tv[0]) * tv[1]], []
    return _rowwise(fn, [(h, DFF, 0), (h, DFF, 1)], [], [(DFF, F32)], [], name=name)[0]


def _swiglu_act_bwd(h, ds, name):
    def fn(tv, fv):
        s, vjp = jax.vjp(lambda a, g: _silu(a) * g, tv[0], tv[1])
        da, dg = vjp(tv[2])
        return [[da, dg], s], []
    return _rowwise(fn, [(h, DFF, 0), (h, DFF, 1), ds], [], [(2 * DFF, F32), (DFF, F32)], [], name=name)


def _loss_fwd_bwd(y, tgt, name):
    def fn(tv, fv):
        e = tv[0] - tv[1]
        row = jnp.sum(e * e, axis=1, keepdims=True)
        tot = jnp.sum(row, axis=0, keepdims=True) * (0.5 / D)
        return [e * (1.0 / D)], [jnp.broadcast_to(tot, (1, 128))]
    return _rowwise(fn, [y, tgt], [], [(D, F32)], [(1, 128)], name=name)


def _shift_down(x, k, row):
    return jnp.where(row >= k, pltpu.roll(x, k, axis=0), 0.0)


def _shift_up(x, k, row):
    n = x.shape[0]
    return jnp.where(row < n - k, pltpu.roll(x, n - k, axis=0), 0.0)


def _pool_window_masks(j):
    lane = lax.broadcasted_iota(jnp.int32, (1, 128), 1) + j * 128
    grp = lane // POOL_GDIM
    return [grp == g for g in range(4)]


def _pool_mean(u, bwd, name):
    T = u.shape[0]
    B = T // S

    def body(u_ref, o_ref):
        j = pl.program_id(1)
        x = u_ref[...]
        row = lax.broadcasted_iota(jnp.int32, (S, 1), 0)
        masks = _pool_window_masks(j)
        inv = [1.0 / jnp.minimum(row + 1, w).astype(F32) for w in POOL_WINDOWS]
        if not bwd:
            s2 = x + _shift_down(x, 1, row)
            s4 = s2 + _shift_down(s2, 2, row)
            s8 = s4 + _shift_down(s4, 4, row)
            s16 = s8 + _shift_down(s8, 8, row)
            mean = jnp.where(masks[0], s2 * inv[0], jnp.where(masks[1], s4 * inv[1],
                             jnp.where(masks[2], s8 * inv[2], s16 * inv[3])))
            o_ref[...] = mean - x
        else:
            g = [jnp.where(masks[i], x * inv[i], 0.0) for i in range(4)]
            t = g[3]
            t = t + _shift_up(t, 8, row) + g[2]
            t = t + _shift_up(t, 4, row) + g[1]
            t = t + _shift_up(t, 2, row) + g[0]
            t = t + _shift_up(t, 1, row)
            o_ref[...] = t - x

    spec = pl.BlockSpec((S, 128), lambda b, j: (b, j))
    return _pcall(body, name=name, grid=(B, POOLW // 128), in_specs=[spec], out_specs=spec,
                  out_shape=jax.ShapeDtypeStruct((T, POOLW), F32), compiler_params=_cp(("parallel", "parallel")))(u)


def _conv_silu(xbc, w, b, name):
    T, C = xbc.shape
    B = T // S

    def body(x_ref, w_ref, b_ref, o_ref):
        x = x_ref[...]
        row = lax.broadcasted_iota(jnp.int32, (S, 1), 0)
        c = b_ref[...] + w_ref[3:4, :] * x
        for s in range(1, 4):
            c = c + w_ref[3 - s:4 - s, :] * _shift_down(x, s, row)
        o_ref[...] = _silu(c)

    return _pcall(body, name=name, grid=(B, C // 128),
                  in_specs=[pl.BlockSpec((S, 128), lambda b, j: (b, j)), pl.BlockSpec((4, 128), lambda b, j: (0, j)),
                            pl.BlockSpec((1, 128), lambda b, j: (0, j))],
                  out_specs=pl.BlockSpec((S, 128), lambda b, j: (b, j)),
                  out_shape=jax.ShapeDtypeStruct((T, C), F32), compiler_params=_cp(("parallel", "parallel")))(xbc, w, b)


def _conv_silu_bwd(xbc, w, b, dact, name):
    T, C = xbc.shape
    B = T // S

    def body(x_ref, w_ref, b_ref, d_ref, dx_ref, dw_ref, db_ref):
        bi = pl.program_id(1)
        x = x_ref[...]
        row = lax.broadcasted_iota(jnp.int32, (S, 1), 0)
        xs = [x] + [_shift_down(x, s, row) for s in range(1, 4)]
        c = b_ref[...]
        for s in range(4):
            c = c + w_ref[3 - s:4 - s, :] * xs[s]
        _, vjp = jax.vjp(_silu, c)
        dc = vjp(d_ref[...])[0]
        dx = w_ref[3:4, :] * dc
        for s in range(1, 4):
            dx = dx + w_ref[3 - s:4 - s, :] * _shift_up(dc, s, row)
        dx_ref[...] = dx
        first = bi == 0
        for s in range(4):
            _acc_rows(dw_ref, 3 - s, jnp.sum(dc * xs[s], axis=0, keepdims=True), first)
        _acc_rows(db_ref, 0, jnp.sum(dc, axis=0, keepdims=True), first)

    blk = pl.BlockSpec((S, 128), lambda j, b: (b, j))
    return _pcall(body, name=name, grid=(C // 128, B),
                  in_specs=[blk, pl.BlockSpec((4, 128), lambda j, b: (0, j)), pl.BlockSpec((1, 128), lambda j, b: (0, j)), blk],
                  out_specs=[blk, pl.BlockSpec((4, 128), lambda j, b: (0, j)), pl.BlockSpec((1, 128), lambda j, b: (0, j))],
                  out_shape=[jax.ShapeDtypeStruct((T, C), F32), jax.ShapeDtypeStruct((4, C), F32),
                             jax.ShapeDtypeStruct((1, C), F32)],
                  compiler_params=_cp(("parallel", "arbitrary")))(xbc, w, b, dact)


def _acc_rows(ref, r, val, first):
    @pl.when(first)
    def _():
        ref[r:r + 1, :] = val

    @pl.when(jnp.logical_not(first))
    def _():
        ref[r:r + 1, :] += val


def _tri_consts():
    i = lax.broadcasted_iota(jnp.int32, (CH, CH), 0)
    j = lax.broadcasted_iota(jnp.int32, (CH, CH), 1)
    return (i == j).astype(F32), (j <= i).astype(F32), (i <= j).astype(F32), i >= j


def _ssd_chunk(h, x, dt, Bm, Cm, a, dsk, consts):
    eye, tril, triu, lower = consts
    Bb = Bm.astype(_MXU)
    Cb = Cm.astype(_MXU)
    cb = lax.dot_general(Cb, Bb, (((1,), (1,)), ((), ())), preferred_element_type=F32)
    ys, hn = [], []
    for e in range(4):
        adt = dt[e] * a[e]
        adt_row = jnp.sum(adt * eye, axis=0, keepdims=True)
        cs_col = jnp.sum(adt_row * tril, axis=1, keepdims=True)
        cs_row = jnp.sum(adt * triu, axis=0, keepdims=True)
        cs_last = jnp.sum(adt, axis=0, keepdims=True)
        decay = jnp.exp(jnp.where(lower, cs_col - cs_row, -jnp.inf))
        xb = (x[e] * dt[e]).astype(_MXU)
        y_diag = jnp.dot((cb * decay).astype(_MXU), xb, preferred_element_type=F32)
        bdec = (Bm * jnp.exp(cs_last - cs_col)).astype(_MXU)
        st = lax.dot_general(bdec, xb, (((0,), (0,)), ((), ())), preferred_element_type=F32)
        hn.append(h[e] * jnp.exp(cs_last) + st)
        y_off = jnp.exp(cs_col) * jnp.dot(Cb, h[e].astype(_MXU), preferred_element_type=F32)
        ys.append(y_diag + y_off + dsk[e] * x[e])
    return ys, hn


def _ssd_specs(order):
    def im(f):
        return lambda p, q: f(*order(p, q))
    xs = pl.BlockSpec((S, 256), im(lambda b, g: (b, g)))
    dt = pl.BlockSpec((None, S, 4), im(lambda b, g: (g, b, 0)))
    bc = pl.BlockSpec((S, 128), im(lambda b, g: (b, g)))
    hd = pl.BlockSpec((None, 1, 4), im(lambda b, g: (g, 0, 0)))
    hs = pl.BlockSpec((None, None, S // CH, 4, 128, 64), im(lambda b, g: (b, g, 0, 0, 0, 0)))
    return xs, dt, bc, hd, hs


def _ssd_fwd(xs, dtg, bm, cm, a, dsk, name):
    T = xs.shape[0]
    B = T // S
    nc = S // CH

    def body(x_ref, dt_ref, b_ref, c_ref, a_ref, k_ref, y_ref, hs_ref, h_ref):
        consts = _tri_consts()
        h_ref[...] = jnp.zeros_like(h_ref)
        al = [a_ref[:, e:e + 1] for e in range(4)]
        kl = [k_ref[:, e:e + 1] for e in range(4)]

        def step(c, carry):
            r0 = pl.multiple_of(c * CH, CH)
            rows = pl.ds(r0, CH)
            h = [h_ref[e] for e in range(4)]
            for e in range(4):
                hs_ref[c, e] = h[e]
            x = [x_ref[rows, 64 * e:64 * e + 64] for e in range(4)]
            dt = [dt_ref[rows, e:e + 1] for e in range(4)]
            ys, hn = _ssd_chunk(h, x, dt, b_ref[rows, :], c_ref[rows, :], al, kl, consts)
            for e in range(4):
                y_ref[rows, 64 * e:64 * e + 64] = ys[e]
                h_ref[e] = hn[e]
            return carry

        lax.fori_loop(0, nc, step, 0)

    sx, sdt, sbc, shd, shs = _ssd_specs(lambda b, g: (b, g))
    return _pcall(body, name=name, grid=(B, 4), in_specs=[sx, sdt, sbc, sbc, shd, shd], out_specs=[sx, shs],
                  out_shape=[jax.ShapeDtypeStruct((T, 1024), F32), jax.ShapeDtypeStruct((B, 4, nc, 4, 128, 64), F32)],
                  scratch_shapes=[pltpu.VMEM((4, 128, 64), F32)],
                  compiler_params=_cp(("parallel", "parallel")))(xs, dtg, bm, cm, a, dsk)


def _lane_place(vals, width):
    lane = lax.broadcasted_iota(jnp.int32, (1, width), 1)
    out = jnp.zeros((1, width), F32)
    for e, v in enumerate(vals):
        out = out + jnp.where(lane == e, v, 0.0)
    return out


def _ssd_bwd(xs, dtg, bm, cm, a, dsk, hs, dy, name):
    T = xs.shape[0]
    B = T // S
    nc = S // CH

    def body(x_ref, dt_ref, b_ref, c_ref, a_ref, k_ref, hs_ref, dy_ref,
             dx_ref, ddt_ref, db_ref, dc_ref, dak_ref, dh_ref, sc_ref):
        bi = pl.program_id(1)
        consts = _tri_consts()
        dh_ref[...] = jnp.zeros_like(dh_ref)
        sc_ref[...] = jnp.zeros_like(sc_ref)
        al = [a_ref[:, e:e + 1] for e in range(4)]
        kl = [k_ref[:, e:e + 1] for e in range(4)]

        def step(i, carry):
            c = nc - 1 - i
            r0 = pl.multiple_of(c * CH, CH)
            rows = pl.ds(r0, CH)
            h = [hs_ref[c, e] for e in range(4)]
            x = [x_ref[rows, 64 * e:64 * e + 64] for e in range(4)]
            dt = [dt_ref[rows, e:e + 1] for e in range(4)]
            f = functools.partial(_ssd_chunk, consts=consts)
            _, vjp = jax.vjp(f, h, x, dt, b_ref[rows, :], c_ref[rows, :], al, kl)
            dys = [dy_ref[rows, 64 * e:64 * e + 64] for e in range(4)]
            dhn = [dh_ref[e] for e in range(4)]
            dh, dx, ddt, dB, dC, da, dk = vjp((dys, dhn))
            for e in range(4):
                dh_ref[e] = dh[e]
                dx_ref[rows, 64 * e:64 * e + 64] = dx[e]
                ddt_ref[rows, e:e + 1] = ddt[e]
            db_ref[rows, :] = dB
            dc_ref[rows, :] = dC
            sc_ref[0:1, :] += _lane_place(da, 128)
            sc_ref[1:2, :] += _lane_place(dk, 128)
            return carry

        lax.fori_loop(0, nc, step, 0)
        first = bi == 0

        @pl.when(first)
        def _():
            dak_ref[...] = sc_ref[...]

        @pl.when(jnp.logical_not(first))
        def _():
            dak_ref[...] += sc_ref[...]

    sx, sdt, sbc, shd, shs = _ssd_specs(lambda g, b: (b, g))
    return _pcall(body, name=name, grid=(4, B), in_specs=[sx, sdt, sbc, sbc, shd, shd, shs, sx],
                  out_specs=[sx, sdt, sbc, sbc, pl.BlockSpec((None, 8, 128), lambda g, b: (g, 0, 0))],
                  out_shape=[jax.ShapeDtypeStruct((T, 1024), F32), jax.ShapeDtypeStruct((4, T, 4), F32),
                             jax.ShapeDtypeStruct((T, 512), F32), jax.ShapeDtypeStruct((T, 512), F32),
                             jax.ShapeDtypeStruct((4, 8, 128), F32)],
                  scratch_shapes=[pltpu.VMEM((4, 128, 64), F32), pltpu.VMEM((8, 128), F32)],
                  compiler_params=_cp(("parallel", "arbitrary")))(xs, dtg, bm, cm, a, dsk, hs, dy)


def _gate_norm(y, z, nw):
    t = y * _silu(z)
    return t * lax.rsqrt(jnp.mean(t * t, axis=-1, keepdims=True) + SSD_EPS) * nw


def _ssd_gate_norm(y, z, nw, name):
    def fn(tv, fv):
        return [[_gate_norm(tv[g], tv[4 + g], fv[0][:, 256 * g:256 * g + 256]) for g in range(4)]], []
    tiled = [(y, 256, g) for g in range(4)] + [(z, 256, g) for g in range(4)]
    return _rowwise(fn, tiled, [nw], [(1024, F32)], [], name=name)[0]


def _ssd_gate_norm_bwd(y, z, nw, dout, name):
    def fn(tv, fv):
        dys, dzs, dns = [], [], []
        for g in range(4):
            _, vjp = jax.vjp(_gate_norm, tv[g], tv[4 + g], fv[0][:, 256 * g:256 * g + 256])
            a, b, c = vjp(tv[8 + g])
            dys.append(a)
            dzs.append(b)
            dns.append(c)
        return [dys, dzs], [dns]
    tiled = [(y, 256, g) for g in range(4)] + [(z, 256, g) for g in range(4)] + [(dout, 256, g) for g in range(4)]
    return _rowwise(fn, tiled, [nw], [(1024, F32), (1024, F32)], [(1, 1024)], name=name)


def _t5_bucket_np(dist):
    dist = np.maximum(dist, 0)
    max_exact = 16
    large = max_exact + (np.log(np.maximum(dist, 1) / max_exact) / np.log(2048 / max_exact) * (32 - max_exact)).astype(np.int32)
    large = np.minimum(large, 31)
    return np.where(dist < max_exact, dist, large).astype(np.int32)


def _bucket_maps():
    qi = np.arange(128)[:, None]
    kj = np.arange(256)[None, :]
    return np.stack([_t5_bucket_np((qi - kj + 128) * dil) for dil in ATTN_DILS]).astype(np.int32)


def _bias_build(rel_bias, maps, name):
    def body(tab_ref, map_ref, o_ref):
        hh = pl.program_id(0)
        m = map_ref[...]
        acc = jnp.zeros((128, 256), F32)
        for b in range(32):
            acc = jnp.where(m == b, tab_ref[b, hh], acc)
        o_ref[...] = acc

    return _pcall(body, name=name, grid=(12,),
                  in_specs=[pl.BlockSpec(memory_space=pltpu.SMEM), pl.BlockSpec((None, 128, 256), lambda h: (h // 4, 0, 0))],
                  out_specs=pl.BlockSpec((None, 128, 256), lambda h: (h, 0, 0)),
                  out_shape=jax.ShapeDtypeStruct((12, 128, 256), F32), compiler_params=_cp(("parallel",)))(rel_bias, maps)


def _bias_reduce(dbias, maps, name):
    nl = dbias.shape[0]

    def body(d_ref, map_ref, o_ref):
        m = map_ref[...]
        d = d_ref[0]
        for i in range(1, nl):
            d = d + d_ref[i]
        lane = lax.broadcasted_iota(jnp.int32, (1, 128), 1)
        out = jnp.zeros((1, 128), F32)
        for b in range(32):
            s = jnp.sum(jnp.sum(jnp.where(m == b, d, 0.0), axis=1, keepdims=True), axis=0, keepdims=True)
            out = out + jnp.where(lane == b, s, 0.0)
        o_ref[...] = out

    return _pcall(body, name=name, grid=(12,),
                  in_specs=[pl.BlockSpec((nl, None, 128, 256), lambda h: (0, h, 0, 0)),
                            pl.BlockSpec((None, 128, 256), lambda h: (h // 4, 0, 0))],
                  out_specs=pl.BlockSpec((None, 1, 128), lambda h: (h, 0, 0)),
                  out_shape=jax.ShapeDtypeStruct((12, 1, 128), F32), compiler_params=_cp(("parallel",)))(dbias, maps)


def _attn_block(q, kb, vb, bias, mask):
    s = lax.dot_general(q.astype(_MXU), kb.astype(_MXU), (((1,), (1,)), ((), ())), preferred_element_type=F32) * 0.125 + bias
    s = jnp.where(mask, s, -jnp.inf)
    m = lax.stop_gradient(jnp.max(s, axis=-1, keepdims=True))
    p = jnp.exp(s - m)
    den = jnp.sum(p, axis=-1, keepdims=True)
    out = jnp.dot((p / den).astype(_MXU), vb.astype(_MXU), preferred_element_type=F32)
    return out, m + jnp.log(den)


def _band_mask():
    qi = lax.broadcasted_iota(jnp.int32, (128, 256), 0)
    kj = lax.broadcasted_iota(jnp.int32, (128, 256), 1)
    return (kj >= qi) & (kj <= qi + 128)


def _attn_fwd(q, k, v, bias, name):
    B, _, dil, L, _ = q.shape
    nb = L // 128

    def body(q_ref, k_ref, v_ref, b_ref, o_ref, l_ref):
        mask = _band_mask()
        bias_v = b_ref[...]
        o, l = _attn_block(q_ref[0:128, :], k_ref[0:128, :], v_ref[0:128, :], bias_v[:, 128:], mask[:, 128:])
        o_ref[0:128, :] = o
        l_ref[0:128, :] = l
        if nb > 1:
            def step(n, carry):
                r0 = pl.multiple_of(n * 128, 128)
                p0 = pl.multiple_of(n * 128 - 128, 128)
                o, l = _attn_block(q_ref[pl.ds(r0, 128), :], k_ref[pl.ds(p0, 256), :], v_ref[pl.ds(p0, 256), :],
                                   bias_v, mask)
                o_ref[pl.ds(r0, 128), :] = o
                l_ref[pl.ds(r0, 128), :] = l
                return carry
            lax.fori_loop(1, nb, step, 0)

    blk = pl.BlockSpec((None, None, None, L, 64), lambda b, h, r: (b, h, r, 0, 0))
    lblk = pl.BlockSpec((None, None, None, L, 1), lambda b, h, r: (b, h, r, 0, 0))
    return _pcall(body, name=name, grid=(B, 4, dil),
                  in_specs=[blk, blk, blk, pl.BlockSpec((None, 128, 256), lambda b, h, r: (h, 0, 0))],
                  out_specs=[blk, lblk],
                  out_shape=[jax.ShapeDtypeStruct(q.shape, F32), jax.ShapeDtypeStruct(q.shape[:4] + (1,), F32)],
                  compiler_params=_cp(("parallel", "parallel", "parallel")))(q, k, v, bias)


def _attn_bwd(q, k, v, bias, do, dl, name):
    B, _, dil, L, _ = q.shape
    nb = L // 128

    def body(q_ref, k_ref, v_ref, b_ref, do_ref, dl_ref, dq_ref, dk_ref, dv_ref, db_ref, acc_ref):
        bi, ri = pl.program_id(1), pl.program_id(2)
        mask = _band_mask()
        bias_v = b_ref[...]
        dk_ref[...] = jnp.zeros_like(dk_ref)
        dv_ref[...] = jnp.zeros_like(dv_ref)
        f0 = functools.partial(_attn_block, mask=mask[:, 128:])
        _, vjp = jax.vjp(f0, q_ref[0:128, :], k_ref[0:128, :], v_ref[0:128, :], bias_v[:, 128:])
        dq, dkb, dvb, dbs = vjp((do_ref[0:128, :], dl_ref[0:128, :]))
        dq_ref[0:128, :] = dq
        dk_ref[0:128, :] += dkb
        dv_ref[0:128, :] += dvb
        acc_ref[:, 0:128] = jnp.zeros((128, 128), F32)
        acc_ref[:, 128:256] = dbs
        if nb > 1:
            f1 = functools.partial(_attn_block, mask=mask)

            def step(n, carry):
                r0 = pl.multiple_of(n * 128, 128)
                p0 = pl.multiple_of(n * 128 - 128, 128)
                _, vjp = jax.vjp(f1, q_ref[pl.ds(r0, 128), :], k_ref[pl.ds(p0, 256), :], v_ref[pl.ds(p0, 256), :], bias_v)
                dq, dkb, dvb, dbs = vjp((do_ref[pl.ds(r0, 128), :], dl_ref[pl.ds(r0, 128), :]))
                dq_ref[pl.ds(r0, 128), :] = dq
                dk_ref[pl.ds(p0, 256), :] += dkb
                dv_ref[pl.ds(p0, 256), :] += dvb
                acc_ref[...] += dbs
                return carry
            lax.fori_loop(1, nb, step, 0)
        first = (bi == 0) & (ri == 0)

        @pl.when(first)
        def _():
            db_ref[...] = acc_ref[...]

        @pl.when(jnp.logical_not(first))
        def _():
            db_ref[...] += acc_ref[...]

    blk = pl.BlockSpec((None, None, None, L, 64), lambda h, b, r: (b, h, r, 0, 0))
    lblk = pl.BlockSpec((None, None, None, L, 1), lambda h, b, r: (b, h, r, 0, 0))
    bblk = pl.BlockSpec((None, 128, 256), lambda h, b, r: (h, 0, 0))
    sds = jax.ShapeDtypeStruct(q.shape, F32)
    return _pcall(body, name=name, grid=(4, B, dil), in_specs=[blk, blk, blk, bblk, blk, lblk],
                  out_specs=[blk, blk, blk, bblk], out_shape=[sds, sds, sds, jax.ShapeDtypeStruct((4, 128, 256), F32)],
                  scratch_shapes=[pltpu.VMEM((128, 256), F32)],
                  compiler_params=_cp(("parallel", "arbitrary", "arbitrary")))(q, k, v, bias, do, dl)


def _lse_merge(o0, o1, o2, l0, l1, l2):
    m = lax.stop_gradient(jnp.maximum(jnp.maximum(l0, l1), l2))
    e0, e1, e2 = jnp.exp(l0 - m), jnp.exp(l1 - m), jnp.exp(l2 - m)
    den = e0 + e1 + e2
    return (e0 / den) * o0 + (e1 / den) * o1 + (e2 / den) * o2


def _attn_merge(outs, lses, dy, name):
    B = outs[0].shape[0]
    bwd = dy is not None

    def body(*refs):
        vals = [r[...] for r in refs[:6]]
        if not bwd:
            refs[6][...] = _lse_merge(*vals)
        else:
            _, vjp = jax.vjp(_lse_merge, *vals)
            for r, g in zip(refs[7:], vjp(refs[6][...])):
                r[...] = g

    blk = pl.BlockSpec((None, None, S, 64), lambda b, h: (b, h, 0, 0))
    lblk = pl.BlockSpec((None, None, S, 1), lambda b, h: (b, h, 0, 0))
    osd = jax.ShapeDtypeStruct(outs[0].shape, F32)
    lsd = jax.ShapeDtypeStruct(lses[0].shape, F32)
    if not bwd:
        return _pcall(body, name=name, grid=(B, 4), in_specs=[blk] * 3 + [lblk] * 3, out_specs=blk, out_shape=osd,
                      compiler_params=_cp(("parallel", "parallel")))(*outs, *lses)
    return _pcall(body, name=name, grid=(B, 4), in_specs=[blk] * 3 + [lblk] * 3 + [blk],
                  out_specs=[blk] * 3 + [lblk] * 3, out_shape=[osd] * 3 + [lsd] * 3,
                  compiler_params=_cp(("parallel", "parallel")))(*outs, *lses, dy)


def _to_dil(t, dil):
    B = t.shape[0] // S
    return t.reshape(B, S // dil, dil, 4, 64).transpose(0, 3, 2, 1, 4)


def _from_dil(t):
    B, _, dil, L, w = t.shape
    return t.transpose(0, 1, 3, 2, 4).reshape(B, 4, S, w)


def _hm_to_dil(t, dil):
    B, _, _, w = t.shape
    return t.reshape(B, 4, S // dil, dil, w).transpose(0, 1, 3, 2, 4)


def _dil_to_rows(t):
    B, _, dil, L, _ = t.shape
    return t.transpose(0, 3, 2, 1, 4).reshape(B * S, 256)


def _gmerge(g0, g1, g2, gb, ya, yb, yc):
    return (jax.nn.sigmoid(g0 + gb[:, 0:D]) * ya + jax.nn.sigmoid(g1 + gb[:, D:2 * D]) * yb
            + jax.nn.sigmoid(g2 + gb[:, 2 * D:3 * D]) * yc)


def _gated_merge(gates, gb, ya, yb, yc, name):
    def fn(tv, fv):
        return [_gmerge(tv[0], tv[1], tv[2], fv[0], tv[3], tv[4], tv[5])], []
    return _rowwise(fn, [(gates, D, 0), (gates, D, 1), (gates, D, 2), ya, yb, yc], [gb], [(D, F32)], [], name=name)[0]


def _gated_merge_bwd(gates, gb, ya, yb, yc, dm, name):
    def fn(tv, fv):
        _, vjp = jax.vjp(_gmerge, tv[0], tv[1], tv[2], fv[0], tv[3], tv[4], tv[5])
        d0, d1, d2, dgb, da, db, dc = vjp(tv[6])
        return [[d0, d1, d2], da, db, dc], [dgb]
    return _rowwise(fn, [(gates, D, 0), (gates, D, 1), (gates, D, 2), ya, yb, yc, dm], [gb],
                    [(3 * D, F32), (D, F32), (D, F32), (D, F32)], [(1, 3 * D)], name=name)


def _pool_affine(t1, pb, ps, dout, name):
    if dout is None:
        def fn(tv, fv):
            return [(tv[0] + fv[0]) * fv[1]], []
        return _rowwise(fn, [t1], [pb, ps], [(POOLW, F32)], [], name=name)[0]

    def fnb(tv, fv):
        t2, vjp = jax.vjp(lambda t, b, s: (t + b) * s, tv[0], fv[0], fv[1])
        dt, db, dsc = vjp(tv[1])
        return [dt, t2], [db, dsc]
    return _rowwise(fnb, [t1, dout], [pb, ps], [(POOLW, F32), (POOLW, F32)], [(1, POOLW), (1, POOLW)], name=name)


def _dt_softplus(dt_raw, dt_bias, ddt, name):
    f = lambda r, b: _softplus(r + b)
    if ddt is None:
        def fn(tv, fv):
            return [f(tv[0], fv[0])], []
        return _rowwise(fn, [dt_raw], [dt_bias], [(16, F32)], [], name=name, tm=1024)[0]

    def fnb(tv, fv):
        _, vjp = jax.vjp(f, tv[0], fv[0])
        dr, db = vjp(tv[1])
        return [dr], [db]
    return _rowwise(fnb, [dt_raw, ddt], [dt_bias], [(16, F32)], [(1, 16)], name=name, tm=1024)


def _adamw(w, g, m, v, name):
    R, C = w.shape
    tm = _pick(R, (256, 128, 64, 32, 16, 8))
    c1 = 1.0 / (1.0 - ADAM_B1 ** ADAM_STEP)
    c2 = 1.0 / (1.0 - ADAM_B2 ** ADAM_STEP)

    def fn(tv, fv):
        wv, gv, mv, vv = tv
        mn = ADAM_B1 * mv + (1.0 - ADAM_B1) * gv
        vn = ADAM_B2 * vv + (1.0 - ADAM_B2) * (gv * gv)
        delta = -ADAM_LR * ((mn * c1) / (jnp.sqrt(vn * c2) + ADAM_EPS) + ADAM_WD * wv)
        return [delta, mn, vn], []
    return _rowwise(fn, [w, g, m, v], [], [(C, F32)] * 3, [], name=name, tm=tm)


def _ffn_fwd(x, w13, w2, g, b, tag):
    h = _mm(x, w13, name=f"{tag}_h")
    s = _swiglu_act(h, name=f"{tag}_act")
    y = _mm(s, w2, name=f"{tag}_y")
    r, out = _res_ln_fwd(x, y, g, b, 0.5, name=f"{tag}_ln")
    return out, dict(x=x, h=h, r=r)


def _ffn_bwd(dout, sv, w13, w2, g, b, tag):
    dskip, dy, dg, db = _ln_bwd(sv['r'], g, b, dout, 0.5, name=f"{tag}_lnb")
    ds = _mm(dy, w2, tb=True, name=f"{tag}_ds")
    dh, s = _swiglu_act_bwd(sv['h'], ds, name=f"{tag}_actb")
    dw2 = _mm(s, dy, ta=True, name=f"{tag}_dw2")
    dw13 = _mm(sv['x'], dh, ta=True, name=f"{tag}_dw13")
    dx = _mm(dh, w13, tb=True, add=dskip, name=f"{tag}_dx")
    return dx, dict(w13=dw13, w2=dw2, g=dg, b=db)


def _mixer_fwd(x1, W, bias_all, tag):
    T = x1.shape[0]
    hcat = _mm(x1, W['w_in_r'], name=f"{tag}_hcat")
    u = hcat[:, O_U:O_Z]
    z = hcat[:, O_Z:O_XBC]
    xbc = hcat[:, O_XBC:O_Q]
    gates = hcat[:, O_G:O_DT]
    dt_raw = hcat[:, O_DT:O_DT + 16]
    pooled = _pool_mean(u, False, name=f"{tag}_pool")
    t1 = _mm(pooled, W['pool_wbd'], name=f"{tag}_pt1")
    t2 = _pool_affine(t1, W['pool_b'], W['pool_scale'], None, name=f"{tag}_paff")
    ya = _mm(t2, W['p_pool'], name=f"{tag}_ya")
    act = _conv_silu(xbc, W['conv_w'], W['conv_b'], name=f"{tag}_conv")
    xs, bm, cm = act[:, :1024], act[:, 1024:1536], act[:, 1536:2048]
    dt = _dt_softplus(dt_raw, W['dt_bias'], None, name=f"{tag}_dt")
    dtg = dt.reshape(T, 4, 4).transpose(1, 0, 2)
    yscan, hs = _ssd_fwd(xs, dtg, bm, cm, W['a_neg'], W['d_skip'], name=f"{tag}_ssd")
    ybn = _ssd_gate_norm(yscan, z, W['ssd_norm'], name=f"{tag}_gn")
    yb = _mm(ybn, W['p_ssd'], name=f"{tag}_yb")
    outs, lses, qkv = [], [], []
    for gi, dil in enumerate(ATTN_DILS):
        qd = _to_dil(hcat[:, O_Q + 256 * gi:O_Q + 256 * gi + 256], dil)
        kd = _to_dil(hcat[:, O_K + 256 * gi:O_K + 256 * gi + 256], dil)
        vd = _to_dil(hcat[:, O_V + 256 * gi:O_V + 256 * gi + 256], dil)
        o, l = _attn_fwd(qd, kd, vd, bias_all[4 * gi:4 * gi + 4], name=f"{tag}_attn{gi}")
        qkv.append((qd, kd, vd))
        outs.append(_from_dil(o))
        lses.append(_from_dil(l))
    ym = _attn_merge(outs, lses, None, name=f"{tag}_amerge")
    ycp = ym.transpose(0, 2, 1, 3).reshape(T, 256)
    yc = _mm(ycp, W['p_attn'], name=f"{tag}_yc")
    merged = _gated_merge(gates, W['gate_b'], ya, yb, yc, name=f"{tag}_gm")
    mix = _mm(merged, W['w_out'], name=f"{tag}_mix")
    r, out = _res_ln_fwd(x1, mix, W['ln2_g'], W['ln2_b'], 1.0, name=f"{tag}_ln")
    sv = dict(x1=x1, z=z, xbc=xbc, gates=gates, dt_raw=dt_raw, pooled=pooled, t1=t1, xs=xs, bm=bm, cm=cm, dtg=dtg,
              hs=hs, yscan=yscan, ybn=ybn, qkv=qkv, outs=outs, lses=lses, ycp=ycp, ya=ya, yb=yb, yc=yc,
              merged=merged, r=r)
    return out, sv


def _mixer_bwd(dout, sv, W, bias_all, tag):
    T = dout.shape[0]
    B = T // S
    gr = {}
    dx1a, dr, gr['ln2_g'], gr['ln2_b'] = _ln_bwd(sv['r'], W['ln2_g'], W['ln2_b'], dout, 1.0, name=f"{tag}_lnb")
    dmerged = _mm(dr, W['w_out'], tb=True, name=f"{tag}_dmerged")
    gr['w_out'] = _mm(sv['merged'], dr, ta=True, name=f"{tag}_dwout")
    dgates, dya, dyb, dyc, gr['gate_b'] = _gated_merge_bwd(sv['gates'], W['gate_b'], sv['ya'], sv['yb'], sv['yc'],
                                                           dmerged, name=f"{tag}_gmb")
    dycp = _mm(dyc, W['p_attn'], tb=True, name=f"{tag}_dycp")
    gr['p_attn'] = _mm(sv['ycp'], dyc, ta=True, name=f"{tag}_dpattn")
    dym = dycp.reshape(B, S, 4, 64).transpose(0, 2, 1, 3)
    dml = _attn_merge(sv['outs'], sv['lses'], dym, name=f"{tag}_amergeb")
    dq, dk, dv, dbias = [], [], [], []
    for gi, dil in enumerate(ATTN_DILS):
        qd, kd, vd = sv['qkv'][gi]
        a, b, c, d = _attn_bwd(qd, kd, vd, bias_all[4 * gi:4 * gi + 4], _hm_to_dil(dml[gi], dil),
                               _hm_to_dil(dml[3 + gi], dil), name=f"{tag}_attnb{gi}")
        dq.append(_dil_to_rows(a))
        dk.append(_dil_to_rows(b))
        dv.append(_dil_to_rows(c))
        dbias.append(d)
    dbias = jnp.concatenate(dbias, axis=0)
    dybn = _mm(dyb, W['p_ssd'], tb=True, name=f"{tag}_dybn")
    gr['p_ssd'] = _mm(sv['ybn'], dyb, ta=True, name=f"{tag}_dpssd")
    dyscan, dz, gr['ssd_norm'] = _ssd_gate_norm_bwd(sv['yscan'], sv['z'], W['ssd_norm'], dybn, name=f"{tag}_gnb")
    dxs, ddtg, dbm, dcm, dak = _ssd_bwd(sv['xs'], sv['dtg'], sv['bm'], sv['cm'], W['a_neg'], W['d_skip'], sv['hs'],
                                        dyscan, name=f"{tag}_ssdb")
    gr['a_neg'], gr['d_skip'] = dak[:, 0, 0:4], dak[:, 1, 0:4]
    ddt = ddtg.transpose(1, 0, 2).reshape(T, 16)
    ddt_raw, gr['dt_bias'] = _dt_softplus(sv['dt_raw'], W['dt_bias'], ddt, name=f"{tag}_dtb")
    dact = jnp.concatenate([dxs, dbm, dcm], axis=1)
    dxbc, gr['conv_w'], gr['conv_b'] = _conv_silu_bwd(sv['xbc'], W['conv_w'], W['conv_b'], dact, name=f"{tag}_convb")
    dt2 = _mm(dya, W['p_pool'], tb=True, name=f"{tag}_dt2")
    dt1, t2, gr['pool_b'], gr['pool_scale'] = _pool_affine(sv['t1'], W['pool_b'], W['pool_scale'], dt2, name=f"{tag}_paffb")
    gr['p_pool'] = _mm(t2, dya, ta=True, name=f"{tag}_dppool")
    dpooled = _mm(dt1, W['pool_wbd'], tb=True, name=f"{tag}_dpooled")
    gr['pool_wbd'] = _mm(sv['pooled'], dt1, ta=True, name=f"{tag}_dpoolw")
    du = _pool_mean(dpooled, True, name=f"{tag}_poolb")
    dhcat = jnp.concatenate([du, dz, dxbc] + dq + dk + dv + [dgates, ddt_raw, jnp.zeros((T, HC - O_DT - 16), F32)], axis=1)
    dx1 = _mm(dhcat, W['w_in_r'], tb=True, add=dx1a, name=f"{tag}_dx1")
    gr['w_in_r'] = _mm(sv['x1'], dhcat, ta=True, name=f"{tag}_dwin")
    return dx1, gr, dbias


def _prep_layer_weights(i, inp, G):
    W = {}
    nat = {}
    for n in BIG:
        g = G[n]
        nat[n] = g.transpose(1, 0, 2).reshape(g.shape[1], 4 * g.shape[2]) if n in COL_SHARDED else g.reshape(4 * g.shape[1], g.shape[2])
    wi = nat['w_in']
    W['w_in_r'] = jnp.concatenate([wi[:, 0:3840], wi[:, 3856:9232], wi[:, 3840:3856], jnp.zeros((D, HC - 9232), wi.dtype)], axis=1)
    for n in BIG:
        if n != 'w_in':
            W[n] = nat[n]
    pw = inp['pool_w'][i].astype(_MXU)
    wbd = jnp.zeros((POOLW, POOLW), _MXU)
    for g in range(4):
        wbd = lax.dynamic_update_slice(wbd, pw[g], (g * POOL_GDIM, g * POOL_GDIM))
    W['pool_wbd'] = wbd
    W['pool_b'] = inp['pool_b'][i].reshape(1, POOLW)
    W['pool_scale'] = inp['pool_scale'][i].reshape(1, POOLW)
    W['conv_w'] = G['conv_w'].transpose(1, 0, 2).reshape(4, 2048)
    W['conv_b'] = inp['conv_b'][i].reshape(1, 2048)
    W['dt_bias'] = inp['dt_bias'][i].reshape(1, 16)
    W['a_neg'] = (-jnp.exp(inp['a_log'][i])).reshape(4, 1, 4)
    W['d_skip'] = inp['d_skip'][i].reshape(4, 1, 4)
    W['ssd_norm'] = inp['ssd_norm'][i].reshape(1, D)
    W['gate_b'] = G['gate_b'].transpose(1, 0, 2).reshape(1, 3 * D)
    for n in ('ln1_g', 'ln1_b', 'ln2_g', 'ln2_b', 'ln3_g', 'ln3_b'):
        W[n] = inp[n][i].reshape(1, D)
    return W


def _gather_layer(i, inp):
    arrs = [inp[n][i].astype(BF16) for n in BIG] + [inp['gate_b'][i], inp['conv_w'][i]]
    outs = _exchange(arrs, "chips", False, name="gather_weights")
    G = dict(zip(BIG + ['gate_b', 'conv_w'], outs))
    return G


def _to_shard_major(n, g):
    if n in COL_SHARDED:
        r, c4 = g.shape
        return g.reshape(r, 4, c4 // 4).transpose(1, 0, 2)
    r4, c = g.shape
    return g.reshape(4, r4 // 4, c)


def _reduce_big(grads):
    names = list(grads)
    halves = []
    for n in names:
        g = grads[n]
        _, r, c = g.shape
        halves.append(g.reshape(4, 2, r // 2, c).transpose(1, 0, 2, 3).reshape(2, 4 * (r // 2), c))
    got = _exchange(halves, "cores", True, name="rs_cores")
    chip = [_sumslots(t, BF16, name="rs_sum2") for t in got]
    chip = [t.reshape(4, t.shape[0] // 4, t.shape[1]) for t in chip]
    got = _exchange(chip, "chips", True, name="rs_chips")
    red = [_sumslots(t, F32, name="rs_sum4") for t in got]
    full = _exchange(red, "cores", False, name="rs_share")
    return {n: t.reshape(2 * t.shape[1], t.shape[2]) for n, t in zip(names, full)}


def _allreduce_small(vec):
    a = _exchange([vec], "cores", False, name="ar_cores")[0]
    a = _sumslots(a, F32, name="ar_sum2")
    a = _exchange([a], "chips", False, name="ar_chips")[0]
    return _sumslots(a, F32, name="ar_sum4")


def _pack(arrs):
    flat = jnp.concatenate([a.reshape(-1) for a in arrs])
    n = flat.shape[0]
    pad = (-n) % (8 * 128)
    flat = jnp.concatenate([flat, jnp.zeros((pad,), F32)])
    return flat.reshape(-1, 128)


def _unpack(p, shapes):
    flat = p.reshape(-1)
    out, off = [], 0
    for s in shapes:
        sz = int(np.prod(s))
        out.append(flat[off:off + sz].reshape(s))
        off += sz
    return out


def _forward_backward(inp, gather, bias_all):
    x = inp['x'].reshape(-1, D)
    tgt = inp['loss_target'].reshape(-1, D)
    saved, Ws = [], []
    for i in range(NL):
        W = _prep_layer_weights(i, inp, gather(i))
        x1, s1 = _ffn_fwd(x, W['ffn1_w13'], W['ffn1_w2'], W['ln1_g'], W['ln1_b'], "f1")
        x2, s2 = _mixer_fwd(x1, W, bias_all, "mx")
        x3, s3 = _ffn_fwd(x2, W['ffn2_w13'], W['ffn2_w2'], W['ln3_g'], W['ln3_b'], "f2")
        saved.append((s1, s2, s3))
        Ws.append(W)
        x = x3
    dy, lpart = _loss_fwd_bwd(x, tgt, name="loss")
    grads, dbiases = [None] * NL, [None] * NL
    for i in reversed(range(NL)):
        W = Ws[i]
        s1, s2, s3 = saved[i]
        g = {}
        dx2, f = _ffn_bwd(dy, s3, W['ffn2_w13'], W['ffn2_w2'], W['ln3_g'], W['ln3_b'], "f2")
        g['ffn2_w13'], g['ffn2_w2'], g['ln3_g'], g['ln3_b'] = f['w13'], f['w2'], f['g'], f['b']
        dx1, gm, dbiases[i] = _mixer_bwd(dx2, s2, W, bias_all, "mx")
        g.update(gm)
        dy, f = _ffn_bwd(dx1, s1, W['ffn1_w13'], W['ffn1_w2'], W['ln1_g'], W['ln1_b'], "f1")
        g['ffn1_w13'], g['ffn1_w2'], g['ln1_g'], g['ln1_b'] = f['w13'], f['w2'], f['g'], f['b']
        grads[i] = g
    return lpart, dy, grads, dbiases


def _finish_layer_grads(i, g, inp):
    out = {}
    for n in ('ffn1_w13', 'ffn2_w13', 'p_pool', 'p_ssd', 'p_attn', 'w_out'):
        out[n] = g[n]
    d = g['w_in_r']
    out['w_in'] = jnp.concatenate([d[:, 0:3840], d[:, O_DT:O_DT + 16], d[:, 3840:O_DT]], axis=1)
    out['ffn1_w2'], out['ffn2_w2'] = g['ffn1_w2'], g['ffn2_w2']
    out['pool_w'] = jnp.stack([g['pool_wbd'][k * POOL_GDIM:(k + 1) * POOL_GDIM, k * POOL_GDIM:(k + 1) * POOL_GDIM] for k in range(4)])
    out['pool_b'] = g['pool_b'].reshape(4, POOL_GDIM)
    out['pool_scale'] = g['pool_scale'].reshape(POOLW)
    out['conv_w'] = g['conv_w']
    out['conv_b'] = g['conv_b'].reshape(2048)
    out['dt_bias'] = g['dt_bias'].reshape(16)
    out['a_log'] = (g['a_neg'].reshape(16)) * (-jnp.exp(inp['a_log'][i]))
    out['d_skip'] = g['d_skip'].reshape(16)
    out['ssd_norm'] = g['ssd_norm'].reshape(D)
    out['gate_b'] = g['gate_b'].reshape(3, D)
    for n in ('ln1_g', 'ln1_b', 'ln2_g', 'ln2_b', 'ln3_g', 'ln3_b'):
        out[n] = g[n].reshape(D)
    return out


def kernel(x, ffn1_w13, ffn1_w2, ln1_g, ln1_b, w_in, gate_b, pool_w, pool_b, pool_scale, conv_w, conv_b,
           dt_bias, a_log, d_skip, ssd_norm, rel_bias, p_pool, p_ssd, p_attn, w_out, ln2_g, ln2_b, ffn2_w13,
           ffn2_w2, ln3_g, ln3_b, loss_target, m_ffn1_w13, m_ffn1_w2, m_ln1_g, m_ln1_b, m_w_in, m_gate_b,
           m_pool_w, m_pool_b, m_pool_scale, m_conv_w, m_conv_b, m_dt_bias, m_a_log, m_d_skip, m_ssd_norm,
           m_rel_bias, m_p_pool, m_p_ssd, m_p_attn, m_w_out, m_ln2_g, m_ln2_b, m_ffn2_w13, m_ffn2_w2, m_ln3_g,
           m_ln3_b, v_ffn1_w13, v_ffn1_w2, v_ln1_g, v_ln1_b, v_w_in, v_gate_b, v_pool_w, v_pool_b,
           v_pool_scale, v_conv_w, v_conv_b, v_dt_bias, v_a_log, v_d_skip, v_ssd_norm, v_rel_bias, v_p_pool,
           v_p_ssd, v_p_attn, v_w_out, v_ln2_g, v_ln2_b, v_ffn2_w13, v_ffn2_w2, v_ln3_g, v_ln3_b):
    inp = dict(locals())
    maps = jnp.asarray(_bucket_maps())
    bias_all = _bias_build(rel_bias, maps, name="bias_build")
    lpart, gx, grads, dbiases = _forward_backward(inp, lambda i: _gather_layer(i, inp), bias_all)
    loss = lax.psum(lpart[0, 0], ("x", "y", "c"))
    fins = [_finish_layer_grads(i, grads[i], inp) for i in range(NL)]

    red = [_reduce_big({n: _to_shard_major(n, fins[i][n]) for n in BIG}) for i in range(NL)]
    gout = {n: jnp.stack([red[i][n] for i in range(NL)]) for n in BIG}

    small_l = [n for n in SMALL if n != 'rel_bias']
    drel = _bias_reduce(jnp.stack(dbiases), maps, name="bias_reduce")[:, 0, :32].T
    small_arrs = [jnp.stack([fins[i][n] for i in range(NL)]) for n in small_l] + [drel]
    packed = _allreduce_small(_pack(small_arrs))
    gsmall = dict(zip(small_l + ['rel_bias'], _unpack(packed, [a.shape for a in small_arrs])))
    shard = 2 * lax.axis_index("x") + lax.axis_index("y")
    gsmall['gate_b'] = lax.dynamic_slice_in_dim(gsmall['gate_b'], shard * 256, 256, axis=2)
    gsmall['conv_w'] = lax.dynamic_slice_in_dim(gsmall['conv_w'], shard * 512, 512, axis=2)
    gout.update(gsmall)

    delta, new_m, new_v = {}, {}, {}
    for n in BIG:
        shp = inp[n].shape
        two_d = lambda a: a.reshape(shp[0] * shp[1], shp[2])
        d, m, v = _adamw(two_d(inp[n]), two_d(gout[n]), two_d(inp['m_' + n]), two_d(inp['v_' + n]), name="adamw_big")
        delta[n], new_m[n], new_v[n] = d.reshape(shp), m.reshape(shp), v.reshape(shp)
    shapes = [inp[n].shape for n in SMALL]
    d, m, v = _adamw(_pack([inp[n] for n in SMALL]), _pack([gout[n] for n in SMALL]),
                     _pack([inp['m_' + n] for n in SMALL]), _pack([inp['v_' + n] for n in SMALL]), name="adamw_small")
    for n, dd, mm, vv in zip(SMALL, _unpack(d, shapes), _unpack(m, shapes), _unpack(v, shapes)):
        delta[n], new_m[n], new_v[n] = dd, mm, vv

    return (loss, gx.reshape(x.shape), *[gout[n] for n in WEIGHTS], *[delta[n] for n in WEIGHTS],
            *[new_m[n] for n in WEIGHTS], *[new_v[n] for n in WEIGHTS])
```

```python
import functools

import numpy as np
import jax
import jax.numpy as jnp
from jax import lax
from jax.experimental import pallas as pl
from jax.experimental.pallas import tpu as pltpu

F32 = jnp.float32
BF16 = jnp.bfloat16
_MXU = jnp.bfloat16
_VMEM_LIMIT = 56 * 1024 * 1024

S = 2048
D = 1024
NL = 4
DFF = 2816
LN_EPS = 1e-5
SSD_EPS = 1e-5
ALPHA = (2.0 * NL) ** 0.25
POOLW = 768
POOL_WINDOWS = (2, 4, 8, 16)
POOL_GDIM = 192
CH = 128
ATTN_DILS = (1, 4, 16)
HC = 9728
O_U, O_Z, O_XBC, O_Q, O_K, O_V, O_G, O_DT = 0, 768, 1792, 3840, 4608, 5376, 6144, 9216

ADAM_LR, ADAM_B1, ADAM_B2, ADAM_EPS, ADAM_WD, ADAM_STEP = 0.001, 0.9, 0.999, 1e-08, 0.01, 10

WEIGHTS = ['ffn1_w13', 'ffn1_w2', 'ln1_g', 'ln1_b', 'w_in', 'gate_b', 'pool_w', 'pool_b', 'pool_scale', 'conv_w',
           'conv_b', 'dt_bias', 'a_log', 'd_skip', 'ssd_norm', 'rel_bias', 'p_pool', 'p_ssd', 'p_attn', 'w_out',
           'ln2_g', 'ln2_b', 'ffn2_w13', 'ffn2_w2', 'ln3_g', 'ln3_b']
BIG = ['ffn1_w13', 'ffn1_w2', 'w_in', 'p_pool', 'p_ssd', 'p_attn', 'w_out', 'ffn2_w13', 'ffn2_w2']
COL_SHARDED = {'ffn1_w13', 'ffn2_w13', 'w_in', 'p_pool', 'p_attn'}
SMALL = [n for n in WEIGHTS if n not in BIG]


def _pcall(body, **kw):
    return pl.pallas_call(body, **kw)


def _cp(sem=None):
    return pltpu.CompilerParams(dimension_semantics=sem, vmem_limit_bytes=_VMEM_LIMIT)


def _pick(n, cands):
    for c in cands:
        if n % c == 0:
            return c
    raise ValueError(f"no tile for {n}")


def _mm(a, b, *, ta=False, tb=False, add=None, out_dtype=F32, name):
    if ta:
        K, M = a.shape
    else:
        M, K = a.shape
    if tb:
        N, K2 = b.shape
    else:
        K2, N = b.shape
    assert K == K2, (a.shape, b.shape, ta, tb)
    sa, sb, so = a.dtype.itemsize, b.dtype.itemsize, jnp.dtype(out_dtype).itemsize
    tm, tn, tk = _mm_tiles(M, N, K, sa, sb, so + (4 if add is not None else 0))
    nk = K // tk
    a_bytes, b_bytes = M * K * sa, K * N * sb
    j_outer = nk == 1 and (b_bytes + a_bytes * (N // tn) < a_bytes + b_bytes * (M // tm))
    ij = (lambda p, q: (q, p)) if j_outer else (lambda p, q: (p, q))

    def im(f):
        return lambda p, q, k: f(*ij(p, q), k)

    a_spec = pl.BlockSpec((tk, tm), im(lambda i, j, k: (k, i))) if ta else pl.BlockSpec((tm, tk), im(lambda i, j, k: (i, k)))
    b_spec = pl.BlockSpec((tn, tk), im(lambda i, j, k: (j, k))) if tb else pl.BlockSpec((tk, tn), im(lambda i, j, k: (k, j)))
    o_spec = pl.BlockSpec((tm, tn), im(lambda i, j, k: (i, j)))
    dims = (((0 if ta else 1,), (1 if tb else 0,)), ((), ()))
    has_add = add is not None

    def body(*refs):
        a_ref, b_ref = refs[0], refs[1]
        add_ref = refs[2] if has_add else None
        o_ref = refs[3] if has_add else refs[2]
        part = lax.dot_general(a_ref[...].astype(_MXU), b_ref[...].astype(_MXU), dims, preferred_element_type=F32)

        def finish(r):
            if has_add:
                r = r + add_ref[...]
            o_ref[...] = r.astype(out_dtype)

        if nk == 1:
            finish(part)
        else:
            acc = refs[-1]
            k = pl.program_id(2)

            @pl.when(k == 0)
            def _():
                acc[...] = part

            @pl.when(k > 0)
            def _():
                acc[...] += part

            @pl.when(k == nk - 1)
            def _():
                finish(acc[...])

    in_specs = [a_spec, b_spec]
    args = [a, b]
    if has_add:
        in_specs.append(o_spec)
        args.append(add)
    gm, gn = M // tm, N // tn
    return _pcall(
        body, name=name, grid=((gn, gm, nk) if j_outer else (gm, gn, nk)), in_specs=in_specs, out_specs=o_spec,
        out_shape=jax.ShapeDtypeStruct((M, N), out_dtype),
        scratch_shapes=([pltpu.VMEM((tm, tn), F32)] if nk > 1 else []),
        compiler_params=_cp(("parallel", "parallel", "arbitrary")),
    )(*args)


_MM_VMEM_BUDGET = 40 * 1024 * 1024


def _divisors128(n, cap):
    return [d for d in range(128, min(n, cap) + 1, 128) if n % d == 0][::-1]


def _mm_tiles(M, N, K, sa, sb, so):
    best = None
    for tm in _divisors128(M, 1024):
        for tn in _divisors128(N, 2560):
            for tk in ([K] if K <= 4096 else []) + _divisors128(K, 2048):
                nk = K // tk
                need = 2 * (tm * tk * sa + tk * tn * sb + tm * tn * so) + (tm * tn * 4 if nk > 1 else 0)
                need += tm * tk * 2 + tk * tn * 2 + tm * tn * 4
                if need > _MM_VMEM_BUDGET:
                    continue
                score = (tm * tn, tk)
                if best is None or score > best[0]:
                    best = (score, (tm, tn, tk))
                break
    assert best is not None, (M, N, K)
    return best[1]


def _store(ref, val):
    if isinstance(val, (list, tuple)):
        off = 0
        for p in val:
            w = p.shape[1]
            ref[:, off:off + w] = p.astype(ref.dtype)
            off += w
    else:
        ref[...] = val.astype(ref.dtype)


def _acc_store(ref, val, first):
    pieces = val if isinstance(val, (list, tuple)) else [val]
    off = 0
    for p in pieces:
        w = p.shape[1]

        @pl.when(first)
        def _(p=p, off=off, w=w):
            ref[:, off:off + w] = p

        @pl.when(jnp.logical_not(first))
        def _(p=p, off=off, w=w):
            ref[:, off:off + w] += p

        off += w


def _rowwise(fn, tiled, full, out_tiled, out_acc, *, name, tm=256):
    arrs, specs = [], []
    for t in tiled:
        arr, w, cb = t if isinstance(t, tuple) else (t, t.shape[1], 0)
        arrs.append(arr)
        specs.append(pl.BlockSpec((tm, w), functools.partial(lambda i, cb: (i, cb), cb=cb)))
    R = arrs[0].shape[0]
    assert R % tm == 0
    for f in full:
        arrs.append(f)
        specs.append(pl.BlockSpec(f.shape, functools.partial(lambda i, nd: (0,) * nd, nd=f.ndim)))
    nt, nf, no = len(tiled), len(full), len(out_tiled)

    def body(*refs):
        tv = [r[...] for r in refs[:nt]]
        fv = [r[...] for r in refs[nt:nt + nf]]
        ot, oa = fn(tv, fv)
        for r, v in zip(refs[nt + nf:nt + nf + no], ot):
            _store(r, v)
        first = pl.program_id(0) == 0
        for r, v in zip(refs[nt + nf + no:], oa):
            _acc_store(r, v, first)

    out_shape = [jax.ShapeDtypeStruct((R, c), dt) for c, dt in out_tiled]
    out_specs = [pl.BlockSpec((tm, c), lambda i: (i, 0)) for c, _ in out_tiled]
    for shp in out_acc:
        out_shape.append(jax.ShapeDtypeStruct(shp, F32))
        out_specs.append(pl.BlockSpec(shp, lambda i: (0, 0)))
    return _pcall(body, name=name, grid=(R // tm,), in_specs=specs, out_specs=out_specs, out_shape=out_shape,
                  compiler_params=_cp(("arbitrary",)))(*arrs)


def _group(group):
    x, y, c = lax.axis_index("x"), lax.axis_index("y"), lax.axis_index("c")
    if group == "chips":
        return 2 * x + y, [((x, 1 - y, c), 2 * x + 1 - y), ((1 - x, y, c), 2 * (1 - x) + y),
                           ((1 - x, 1 - y, c), 2 * (1 - x) + 1 - y)]
    if group == "cores":
        return c, [((x, y, 1 - c), 1 - c)]
    if group == "x":
        return x, [((1 - x, y, c), 1 - x)]
    return y, [((x, 1 - y, c), 1 - y)]


def _exchange(arrs, group, mode, name):
    chips = group == "chips"
    k = len(arrs)
    npeer = 3 if chips else 1

    def body(*refs):
        ins, outs = refs[:k], refs[k:2 * k]
        send_sems, recv_sems = refs[2 * k:]
        me, peers = _group(group)
        remote = []
        for i in range(k):
            for p, (dev, slot) in enumerate(peers):
                src = ins[i].at[slot] if mode == "scatter" else ins[i]
                if not chips:
                    dst = outs[i]
                else:
                    dst = outs[i].at[p] if mode == "scatter" else outs[i].at[me]
                cp = pltpu.make_async_remote_copy(src_ref=src, dst_ref=dst, send_sem=send_sems.at[i, p],
                                                  recv_sem=recv_sems.at[i, p], device_id=dev,
                                                  device_id_type=pl.DeviceIdType.MESH)
                cp.start()
                remote.append(cp)
        for cp in remote:
            cp.wait_recv()
        for cp in remote:
            cp.wait_send()

    def oshape(a):
        piece = a.shape[1:] if mode == "scatter" else a.shape
        if chips:
            piece = ((3,) if mode == "scatter" else (4,)) + piece
        return jax.ShapeDtypeStruct(piece, a.dtype)

    any_spec = pl.BlockSpec(memory_space=pl.ANY)
    return _pcall(body, name=name, in_specs=[any_spec] * k, out_specs=[any_spec] * k, out_shape=[oshape(a) for a in arrs],
                  scratch_shapes=[pltpu.SemaphoreType.DMA((k, npeer)), pltpu.SemaphoreType.DMA((k, npeer))])(*arrs)


def _sum_own_recv(own, recv, me, out_dtype, name):
    n, R, C = own.shape
    nr = 1 if recv.ndim == 2 else recv.shape[0]
    tr = _pick(R, (256, 128, 64, 32, 16, 8))

    def body(me_ref, own_ref, *refs):
        o_ref = refs[-1]
        acc = own_ref[...].astype(F32)
        for r in refs[:-1]:
            acc = acc + r[...].astype(F32)
        o_ref[...] = acc.astype(out_dtype)

    specs = [pl.BlockSpec((None, tr, C), lambda i, me_ref: (me_ref[0], i, 0))]
    args = [own]
    if recv.ndim == 2:
        specs.append(pl.BlockSpec((tr, C), lambda i, me_ref: (i, 0)))
        args.append(recv)
    else:
        for p in range(nr):
            specs.append(pl.BlockSpec((None, tr, C), functools.partial(lambda i, me_ref, p: (p, i, 0), p=p)))
            args.append(recv)
    gs = pltpu.PrefetchScalarGridSpec(num_scalar_prefetch=1, grid=(R // tr,), in_specs=specs,
                                      out_specs=pl.BlockSpec((tr, C), lambda i, me_ref: (i, 0)))
    return _pcall(body, name=name, grid_spec=gs, out_shape=jax.ShapeDtypeStruct((R, C), out_dtype),
                  compiler_params=_cp(("parallel",)))(me, *args)


def _silu(x):
    return x * jax.nn.sigmoid(x)


def _ln(r, g, b):
    mu = jnp.mean(r, -1, keepdims=True)
    xc = r - mu
    var = jnp.mean(xc * xc, -1, keepdims=True)
    return xc * lax.rsqrt(var + LN_EPS) * g + b


def _softplus(x):
    return jnp.maximum(x, 0.0) + jnp.log1p(jnp.exp(-jnp.abs(x)))


def _res_ln_fwd(x, y, g, b, res, name):
    def fn(tv, fv):
        r = ALPHA * tv[0] + res * tv[1]
        return [r, _ln(r, fv[0], fv[1])], []
    return _rowwise(fn, [x, y], [g, b], [(D, F32), (D, F32)], [], name=name)


def _ln_bwd(r, g, b, dout, res, name):
    def fn(tv, fv):
        _, vjp = jax.vjp(_ln, tv[0], fv[0], fv[1])
        dr, dg, db = vjp(tv[1])
        return [ALPHA * dr, res * dr], [dg, db]
    return _rowwise(fn, [r, dout], [g, b], [(D, F32), (D, F32)], [(1, D), (1, D)], name=name)


def _swiglu_act(h, name):
    def fn(tv, fv):
        return [_silu(tv[0]) * tv[1]], []
    return _rowwise(fn, [(h, DFF, 0), (h, DFF, 1)], [], [(DFF, F32)], [], name=name)[0]


def _swiglu_act_bwd(h, ds, name):
    def fn(tv, fv):
        s, vjp = jax.vjp(lambda a, g: _silu(a) * g, tv[0], tv[1])
        da, dg = vjp(tv[2])
        return [[da, dg], s], []
    return _rowwise(fn, [(h, DFF, 0), (h, DFF, 1), ds], [], [(2 * DFF, F32), (DFF, F32)], [], name=name)


def _loss_fwd_bwd(y, tgt, name):
    def fn(tv, fv):
        e = tv[0] - tv[1]
        row = jnp.sum(e * e, axis=1, keepdims=True)
        tot = jnp.sum(row, axis=0, keepdims=True) * (0.5 / D)
        return [e * (1.0 / D)], [jnp.broadcast_to(tot, (1, 128))]
    return _rowwise(fn, [y, tgt], [], [(D, F32)], [(1, 128)], name=name)


def _shift_down(x, k, row):
    return jnp.where(row >= k, pltpu.roll(x, k, axis=0), 0.0)


def _shift_up(x, k, row):
    n = x.shape[0]
    return jnp.where(row < n - k, pltpu.roll(x, n - k, axis=0), 0.0)


def _pool_window_masks(j):
    lane = lax.broadcasted_iota(jnp.int32, (1, 128), 1) + j * 128
    grp = lane // POOL_GDIM
    return [grp == g for g in range(4)]


def _pool_mean(u, bwd, name):
    T = u.shape[0]
    B = T // S

    def body(u_ref, o_ref):
        j = pl.program_id(1)
        x = u_ref[...]
        row = lax.broadcasted_iota(jnp.int32, (S, 1), 0)
        masks = _pool_window_masks(j)
        inv = [1.0 / jnp.minimum(row + 1, w).astype(F32) for w in POOL_WINDOWS]
        if not bwd:
            s2 = x + _shift_down(x, 1, row)
            s4 = s2 + _shift_down(s2, 2, row)
            s8 = s4 + _shift_down(s4, 4, row)
            s16 = s8 + _shift_down(s8, 8, row)
            mean = jnp.where(masks[0], s2 * inv[0], jnp.where(masks[1], s4 * inv[1],
                             jnp.where(masks[2], s8 * inv[2], s16 * inv[3])))
            o_ref[...] = mean - x
        else:
            g = [jnp.where(masks[i], x * inv[i], 0.0) for i in range(4)]
            t = g[3]
            t = t + _shift_up(t, 8, row) + g[2]
            t = t + _shift_up(t, 4, row) + g[1]
            t = t + _shift_up(t, 2, row) + g[0]
            t = t + _shift_up(t, 1, row)
            o_ref[...] = t - x

    spec = pl.BlockSpec((S, 128), lambda b, j: (b, j))
    return _pcall(body, name=name, grid=(B, POOLW // 128), in_specs=[spec], out_specs=spec,
                  out_shape=jax.ShapeDtypeStruct((T, POOLW), F32), compiler_params=_cp(("parallel", "parallel")))(u)


def _conv_silu(xbc, w, b, name):
    T, C = xbc.shape
    B = T // S

    def body(x_ref, w_ref, b_ref, o_ref):
        x = x_ref[...]
        row = lax.broadcasted_iota(jnp.int32, (S, 1), 0)
        c = b_ref[...] + w_ref[3:4, :] * x
        for s in range(1, 4):
            c = c + w_ref[3 - s:4 - s, :] * _shift_down(x, s, row)
        o_ref[...] = _silu(c)

    return _pcall(body, name=name, grid=(B, C // 128),
                  in_specs=[pl.BlockSpec((S, 128), lambda b, j: (b, j)), pl.BlockSpec((4, 128), lambda b, j: (0, j)),
                            pl.BlockSpec((1, 128), lambda b, j: (0, j))],
                  out_specs=pl.BlockSpec((S, 128), lambda b, j: (b, j)),
                  out_shape=jax.ShapeDtypeStruct((T, C), F32), compiler_params=_cp(("parallel", "parallel")))(xbc, w, b)


def _conv_silu_bwd(xbc, w, b, dact, name):
    T, C = xbc.shape
    B = T // S

    def body(x_ref, w_ref, b_ref, d_ref, dx_ref, dw_ref, db_ref):
        bi = pl.program_id(1)
        x = x_ref[...]
        row = lax.broadcasted_iota(jnp.int32, (S, 1), 0)
        xs = [x] + [_shift_down(x, s, row) for s in range(1, 4)]
        c = b_ref[...]
        for s in range(4):
            c = c + w_ref[3 - s:4 - s, :] * xs[s]
        _, vjp = jax.vjp(_silu, c)
        dc = vjp(d_ref[...])[0]
        dx = w_ref[3:4, :] * dc
        for s in range(1, 4):
            dx = dx + w_ref[3 - s:4 - s, :] * _shift_up(dc, s, row)
        dx_ref[...] = dx
        first = bi == 0
        for s in range(4):
            _acc_rows(dw_ref, 3 - s, jnp.sum(dc * xs[s], axis=0, keepdims=True), first)
        _acc_rows(db_ref, 0, jnp.sum(dc, axis=0, keepdims=True), first)

    blk = pl.BlockSpec((S, 128), lambda j, b: (b, j))
    return _pcall(body, name=name, grid=(C // 128, B),
                  in_specs=[blk, pl.BlockSpec((4, 128), lambda j, b: (0, j)), pl.BlockSpec((1, 128), lambda j, b: (0, j)), blk],
                  out_specs=[blk, pl.BlockSpec((4, 128), lambda j, b: (0, j)), pl.BlockSpec((1, 128), lambda j, b: (0, j))],
                  out_shape=[jax.ShapeDtypeStruct((T, C), F32), jax.ShapeDtypeStruct((4, C), F32),
                             jax.ShapeDtypeStruct((1, C), F32)],
                  compiler_params=_cp(("parallel", "arbitrary")))(xbc, w, b, dact)


def _acc_rows(ref, r, val, first):
    @pl.when(first)
    def _():
        ref[r:r + 1, :] = val

    @pl.when(jnp.logical_not(first))
    def _():
        ref[r:r + 1, :] += val


def _tri_consts():
    i = lax.broadcasted_iota(jnp.int32, (CH, CH), 0)
    j = lax.broadcasted_iota(jnp.int32, (CH, CH), 1)
    return (i == j).astype(F32), (j <= i).astype(F32), (i <= j).astype(F32), i >= j


def _ssd_chunk(h, x, dt, Bm, Cm, a, dsk, consts):
    eye, tril, triu, lower = consts
    Bb = Bm.astype(_MXU)
    Cb = Cm.astype(_MXU)
    cb = lax.dot_general(Cb, Bb, (((1,), (1,)), ((), ())), preferred_element_type=F32)
    ys, hn = [], []
    for e in range(4):
        adt = dt[e] * a[e]
        adt_row = jnp.sum(adt * eye, axis=0, keepdims=True)
        cs_col = jnp.sum(adt_row * tril, axis=1, keepdims=True)
        cs_row = jnp.sum(adt * triu, axis=0, keepdims=True)
        cs_last = jnp.sum(adt, axis=0, keepdims=True)
        decay = jnp.exp(jnp.where(lower, cs_col - cs_row, -jnp.inf))
        xb = (x[e] * dt[e]).astype(_MXU)
        y_diag = jnp.dot((cb * decay).astype(_MXU), xb, preferred_element_type=F32)
        bdec = (Bm * jnp.exp(cs_last - cs_col)).astype(_MXU)
        st = lax.dot_general(bdec, xb, (((0,), (0,)), ((), ())), preferred_element_type=F32)
        hn.append(h[e] * jnp.exp(cs_last) + st)
        y_off = jnp.exp(cs_col) * jnp.dot(Cb, h[e].astype(_MXU), preferred_element_type=F32)
        ys.append(y_diag + y_off + dsk[e] * x[e])
    return ys, hn


def _ssd_specs(order):
    def im(f):
        return lambda p, q: f(*order(p, q))
    xs = pl.BlockSpec((S, 256), im(lambda b, g: (b, g)))
    dt = pl.BlockSpec((None, S, 4), im(lambda b, g: (g, b, 0)))
    bc = pl.BlockSpec((S, 128), im(lambda b, g: (b, g)))
    hd = pl.BlockSpec((None, 1, 4), im(lambda b, g: (g, 0, 0)))
    hs = pl.BlockSpec((None, None, S // CH, 4, 128, 64), im(lambda b, g: (b, g, 0, 0, 0, 0)))
    return xs, dt, bc, hd, hs


def _ssd_fwd(xs, dtg, bm, cm, a, dsk, name):
    T = xs.shape[0]
    B = T // S
    nc = S // CH

    def body(x_ref, dt_ref, b_ref, c_ref, a_ref, k_ref, y_ref, hs_ref, h_ref):
        consts = _tri_consts()
        h_ref[...] = jnp.zeros_like(h_ref)
        al = [a_ref[:, e:e + 1] for e in range(4)]
        kl = [k_ref[:, e:e + 1] for e in range(4)]

        def step(c, carry):
            r0 = pl.multiple_of(c * CH, CH)
            rows = pl.ds(r0, CH)
            h = [h_ref[e] for e in range(4)]
            for e in range(4):
                hs_ref[c, e] = h[e]
            x = [x_ref[rows, 64 * e:64 * e + 64] for e in range(4)]
            dt = [dt_ref[rows, e:e + 1] for e in range(4)]
            ys, hn = _ssd_chunk(h, x, dt, b_ref[rows, :], c_ref[rows, :], al, kl, consts)
            for e in range(4):
                y_ref[rows, 64 * e:64 * e + 64] = ys[e]
                h_ref[e] = hn[e]
            return carry

        lax.fori_loop(0, nc, step, 0)

    sx, sdt, sbc, shd, shs = _ssd_specs(lambda b, g: (b, g))
    return _pcall(body, name=name, grid=(B, 4), in_specs=[sx, sdt, sbc, sbc, shd, shd], out_specs=[sx, shs],
                  out_shape=[jax.ShapeDtypeStruct((T, 1024), F32), jax.ShapeDtypeStruct((B, 4, nc, 4, 128, 64), F32)],
                  scratch_shapes=[pltpu.VMEM((4, 128, 64), F32)],
                  compiler_params=_cp(("parallel", "parallel")))(xs, dtg, bm, cm, a, dsk)


def _lane_place(vals, width):
    lane = lax.broadcasted_iota(jnp.int32, (1, width), 1)
    out = jnp.zeros((1, width), F32)
    for e, v in enumerate(vals):
        out = out + jnp.where(lane == e, v, 0.0)
    return out


def _ssd_bwd(xs, dtg, bm, cm, a, dsk, hs, dy, name):
    T = xs.shape[0]
    B = T // S
    nc = S // CH

    def body(x_ref, dt_ref, b_ref, c_ref, a_ref, k_ref, hs_ref, dy_ref,
             dx_ref, ddt_ref, db_ref, dc_ref, dak_ref, dh_ref, sc_ref):
        bi = pl.program_id(1)
        consts = _tri_consts()
        dh_ref[...] = jnp.zeros_like(dh_ref)
        sc_ref[...] = jnp.zeros_like(sc_ref)
        al = [a_ref[:, e:e + 1] for e in range(4)]
        kl = [k_ref[:, e:e + 1] for e in range(4)]

        def step(i, carry):
            c = nc - 1 - i
            r0 = pl.multiple_of(c * CH, CH)
            rows = pl.ds(r0, CH)
            h = [hs_ref[c, e] for e in range(4)]
            x = [x_ref[rows, 64 * e:64 * e + 64] for e in range(4)]
            dt = [dt_ref[rows, e:e + 1] for e in range(4)]
            f = functools.partial(_ssd_chunk, consts=consts)
            _, vjp = jax.vjp(f, h, x, dt, b_ref[rows, :], c_ref[rows, :], al, kl)
            dys = [dy_ref[rows, 64 * e:64 * e + 64] for e in range(4)]
            dhn = [dh_ref[e] for e in range(4)]
            dh, dx, ddt, dB, dC, da, dk = vjp((dys, dhn))
            for e in range(4):
                dh_ref[e] = dh[e]
                dx_ref[rows, 64 * e:64 * e + 64] = dx[e]
                ddt_ref[rows, e:e + 1] = ddt[e]
            db_ref[rows, :] = dB
            dc_ref[rows, :] = dC
            sc_ref[0:1, :] += _lane_place(da, 128)
            sc_ref[1:2, :] += _lane_place(dk, 128)
            return carry

        lax.fori_loop(0, nc, step, 0)
        first = bi == 0

        @pl.when(first)
        def _():
            dak_ref[...] = sc_ref[...]

        @pl.when(jnp.logical_not(first))
        def _():
            dak_ref[...] += sc_ref[...]

    sx, sdt, sbc, shd, shs = _ssd_specs(lambda g, b: (b, g))
    return _pcall(body, name=name, grid=(4, B), in_specs=[sx, sdt, sbc, sbc, shd, shd, shs, sx],
                  out_specs=[sx, sdt, sbc, sbc, pl.BlockSpec((None, 8, 128), lambda g, b: (g, 0, 0))],
                  out_shape=[jax.ShapeDtypeStruct((T, 1024), F32), jax.ShapeDtypeStruct((4, T, 4), F32),
                             jax.ShapeDtypeStruct((T, 512), F32), jax.ShapeDtypeStruct((T, 512), F32),
                             jax.ShapeDtypeStruct((4, 8, 128), F32)],
                  scratch_shapes=[pltpu.VMEM((4, 128, 64), F32), pltpu.VMEM((8, 128), F32)],
                  compiler_params=_cp(("parallel", "arbitrary")))(xs, dtg, bm, cm, a, dsk, hs, dy)


def _gate_norm(y, z, nw):
    t = y * _silu(z)
    return t * lax.rsqrt(jnp.mean(t * t, axis=-1, keepdims=True) + SSD_EPS) * nw


def _ssd_gate_norm(y, z, nw, name):
    def fn(tv, fv):
        return [[_gate_norm(tv[g], tv[4 + g], fv[0][:, 256 * g:256 * g + 256]) for g in range(4)]], []
    tiled = [(y, 256, g) for g in range(4)] + [(z, 256, g) for g in range(4)]
    return _rowwise(fn, tiled, [nw], [(1024, F32)], [], name=name)[0]


def _ssd_gate_norm_bwd(y, z, nw, dout, name):
    def fn(tv, fv):
        dys, dzs, dns = [], [], []
        for g in range(4):
            _, vjp = jax.vjp(_gate_norm, tv[g], tv[4 + g], fv[0][:, 256 * g:256 * g + 256])
            a, b, c = vjp(tv[8 + g])
            dys.append(a)
            dzs.append(b)
            dns.append(c)
        return [dys, dzs], [dns]
    tiled = [(y, 256, g) for g in range(4)] + [(z, 256, g) for g in range(4)] + [(dout, 256, g) for g in range(4)]
    return _rowwise(fn, tiled, [nw], [(1024, F32), (1024, F32)], [(1, 1024)], name=name)


def _t5_bucket_np(dist):
    dist = np.maximum(dist, 0)
    max_exact = 16
    large = max_exact + (np.log(np.maximum(dist, 1) / max_exact) / np.log(2048 / max_exact) * (32 - max_exact)).astype(np.int32)
    large = np.minimum(large, 31)
    return np.where(dist < max_exact, dist, large).astype(np.int32)


def _bucket_maps():
    qi = np.arange(128)[:, None]
    kj = np.arange(256)[None, :]
    return np.stack([_t5_bucket_np((qi - kj + 128) * dil) for dil in ATTN_DILS]).astype(np.int32)


def _bias_build(rel_bias, maps, name):
    def body(tab_ref, map_ref, o_ref):
        hh = pl.program_id(0)
        m = map_ref[...]
        acc = jnp.zeros((128, 256), F32)
        for b in range(32):
            acc = jnp.where(m == b, tab_ref[b, hh], acc)
        o_ref[...] = acc

    return _pcall(body, name=name, grid=(12,),
                  in_specs=[pl.BlockSpec(memory_space=pltpu.SMEM), pl.BlockSpec((None, 128, 256), lambda h: (h // 4, 0, 0))],
                  out_specs=pl.BlockSpec((None, 128, 256), lambda h: (h, 0, 0)),
                  out_shape=jax.ShapeDtypeStruct((12, 128, 256), F32), compiler_params=_cp(("parallel",)))(rel_bias, maps)


def _bias_reduce(dbias, maps, name):
    nl = dbias.shape[0]

    def body(d_ref, map_ref, o_ref):
        m = map_ref[...]
        d = d_ref[0]
        for i in range(1, nl):
            d = d + d_ref[i]
        lane = lax.broadcasted_iota(jnp.int32, (1, 128), 1)
        out = jnp.zeros((1, 128), F32)
        for b in range(32):
            s = jnp.sum(jnp.sum(jnp.where(m == b, d, 0.0), axis=1, keepdims=True), axis=0, keepdims=True)
            out = out + jnp.where(lane == b, s, 0.0)
        o_ref[...] = out

    return _pcall(body, name=name, grid=(12,),
                  in_specs=[pl.BlockSpec((nl, None, 128, 256), lambda h: (0, h, 0, 0)),
                            pl.BlockSpec((None, 128, 256), lambda h: (h // 4, 0, 0))],
                  out_specs=pl.BlockSpec((None, 1, 128), lambda h: (h, 0, 0)),
                  out_shape=jax.ShapeDtypeStruct((12, 1, 128), F32), compiler_params=_cp(("parallel",)))(dbias, maps)


def _attn_block(q, kb, vb, bias, mask):
    s = lax.dot_general(q.astype(_MXU), kb.astype(_MXU), (((1,), (1,)), ((), ())), preferred_element_type=F32) * 0.125 + bias
    s = jnp.where(mask, s, -jnp.inf)
    m = lax.stop_gradient(jnp.max(s, axis=-1, keepdims=True))
    p = jnp.exp(s - m)
    den = jnp.sum(p, axis=-1, keepdims=True)
    out = jnp.dot((p / den).astype(_MXU), vb.astype(_MXU), preferred_element_type=F32)
    return out, m + jnp.log(den)


def _band_mask():
    qi = lax.broadcasted_iota(jnp.int32, (128, 256), 0)
    kj = lax.broadcasted_iota(jnp.int32, (128, 256), 1)
    return (kj >= qi) & (kj <= qi + 128)


def _attn_fwd(q, k, v, bias, name):
    B, _, dil, L, _ = q.shape
    nb = L // 128

    def body(q_ref, k_ref, v_ref, b_ref, o_ref, l_ref):
        mask = _band_mask()
        bias_v = b_ref[...]
        o, l = _attn_block(q_ref[0:128, :], k_ref[0:128, :], v_ref[0:128, :], bias_v[:, 128:], mask[:, 128:])
        o_ref[0:128, :] = o
        l_ref[0:128, :] = l
        if nb > 1:
            def step(n, carry):
                r0 = pl.multiple_of(n * 128, 128)
                p0 = pl.multiple_of(n * 128 - 128, 128)
                o, l = _attn_block(q_ref[pl.ds(r0, 128), :], k_ref[pl.ds(p0, 256), :], v_ref[pl.ds(p0, 256), :],
                                   bias_v, mask)
                o_ref[pl.ds(r0, 128), :] = o
                l_ref[pl.ds(r0, 128), :] = l
                return carry
            lax.fori_loop(1, nb, step, 0)

    blk = pl.BlockSpec((None, None, None, L, 64), lambda b, h, r: (b, h, r, 0, 0))
    lblk = pl.BlockSpec((None, None, None, L, 1), lambda b, h, r: (b, h, r, 0, 0))
    return _pcall(body, name=name, grid=(B, 4, dil),
                  in_specs=[blk, blk, blk, pl.BlockSpec((None, 128, 256), lambda b, h, r: (h, 0, 0))],
                  out_specs=[blk, lblk],
                  out_shape=[jax.ShapeDtypeStruct(q.shape, F32), jax.ShapeDtypeStruct(q.shape[:4] + (1,), F32)],
                  compiler_params=_cp(("parallel", "parallel", "parallel")))(q, k, v, bias)


def _attn_bwd(q, k, v, bias, do, dl, name):
    B, _, dil, L, _ = q.shape
    nb = L // 128

    def body(q_ref, k_ref, v_ref, b_ref, do_ref, dl_ref, dq_ref, dk_ref, dv_ref, db_ref, acc_ref):
        bi, ri = pl.program_id(1), pl.program_id(2)
        mask = _band_mask()
        bias_v = b_ref[...]
        dk_ref[...] = jnp.zeros_like(dk_ref)
        dv_ref[...] = jnp.zeros_like(dv_ref)
        f0 = functools.partial(_attn_block, mask=mask[:, 128:])
        _, vjp = jax.vjp(f0, q_ref[0:128, :], k_ref[0:128, :], v_ref[0:128, :], bias_v[:, 128:])
        dq, dkb, dvb, dbs = vjp((do_ref[0:128, :], dl_ref[0:128, :]))
        dq_ref[0:128, :] = dq
        dk_ref[0:128, :] += dkb
        dv_ref[0:128, :] += dvb
        acc_ref[:, 0:128] = jnp.zeros((128, 128), F32)
        acc_ref[:, 128:256] = dbs
        if nb > 1:
            f1 = functools.partial(_attn_block, mask=mask)

            def step(n, carry):
                r0 = pl.multiple_of(n * 128, 128)
                p0 = pl.multiple_of(n * 128 - 128, 128)
                _, vjp = jax.vjp(f1, q_ref[pl.ds(r0, 128), :], k_ref[pl.ds(p0, 256), :], v_ref[pl.ds(p0, 256), :], bias_v)
                dq, dkb, dvb, dbs = vjp((do_ref[pl.ds(r0, 128), :], dl_ref[pl.ds(r0, 128), :]))
                dq_ref[pl.ds(r0, 128), :] = dq
                dk_ref[pl.ds(p0, 256), :] += dkb
                dv_ref[pl.ds(p0, 256), :] += dvb
                acc_ref[...] += dbs
                return carry
            lax.fori_loop(1, nb, step, 0)
        first = (bi == 0) & (ri == 0)

        @pl.when(first)
        def _():
            db_ref[...] = acc_ref[...]

        @pl.when(jnp.logical_not(first))
        def _():
            db_ref[...] += acc_ref[...]

    blk = pl.BlockSpec((None, None, None, L, 64), lambda h, b, r: (b, h, r, 0, 0))
    lblk = pl.BlockSpec((None, None, None, L, 1), lambda h, b, r: (b, h, r, 0, 0))
    bblk = pl.BlockSpec((None, 128, 256), lambda h, b, r: (h, 0, 0))
    sds = jax.ShapeDtypeStruct(q.shape, F32)
    return _pcall(body, name=name, grid=(4, B, dil), in_specs=[blk, blk, blk, bblk, blk, lblk],
                  out_specs=[blk, blk, blk, bblk], out_shape=[sds, sds, sds, jax.ShapeDtypeStruct((4, 128, 256), F32)],
                  scratch_shapes=[pltpu.VMEM((128, 256), F32)],
                  compiler_params=_cp(("parallel", "arbitrary", "arbitrary")))(q, k, v, bias, do, dl)


def _lse_merge(o0, o1, o2, l0, l1, l2):
    m = lax.stop_gradient(jnp.maximum(jnp.maximum(l0, l1), l2))
    e0, e1, e2 = jnp.exp(l0 - m), jnp.exp(l1 - m), jnp.exp(l2 - m)
    den = e0 + e1 + e2
    return (e0 / den) * o0 + (e1 / den) * o1 + (e2 / den) * o2


def _attn_merge(outs, lses, dy, name):
    B = outs[0].shape[0]
    bwd = dy is not None

    def body(*refs):
        vals = [r[...] for r in refs[:6]]
        if not bwd:
            refs[6][...] = _lse_merge(*vals)
        else:
            _, vjp = jax.vjp(_lse_merge, *vals)
            for r, g in zip(refs[7:], vjp(refs[6][...])):
                r[...] = g

    blk = pl.BlockSpec((None, None, S, 64), lambda b, h: (b, h, 0, 0))
    lblk = pl.BlockSpec((None, None, S, 1), lambda b, h: (b, h, 0, 0))
    osd = jax.ShapeDtypeStruct(outs[0].shape, F32)
    lsd = jax.ShapeDtypeStruct(lses[0].shape, F32)
    if not bwd:
        return _pcall(body, name=name, grid=(B, 4), in_specs=[blk] * 3 + [lblk] * 3, out_specs=blk, out_shape=osd,
                      compiler_params=_cp(("parallel", "parallel")))(*outs, *lses)
    return _pcall(body, name=name, grid=(B, 4), in_specs=[blk] * 3 + [lblk] * 3 + [blk],
                  out_specs=[blk] * 3 + [lblk] * 3, out_shape=[osd] * 3 + [lsd] * 3,
                  compiler_params=_cp(("parallel", "parallel")))(*outs, *lses, dy)


def _to_dil(t, dil):
    B = t.shape[0] // S
    return t.reshape(B, S // dil, dil, 4, 64).transpose(0, 3, 2, 1, 4)


def _from_dil(t):
    B, _, dil, L, w = t.shape
    return t.transpose(0, 1, 3, 2, 4).reshape(B, 4, S, w)


def _hm_to_dil(t, dil):
    B, _, _, w = t.shape
    return t.reshape(B, 4, S // dil, dil, w).transpose(0, 1, 3, 2, 4)


def _dil_to_rows(t):
    B, _, dil, L, _ = t.shape
    return t.transpose(0, 3, 2, 1, 4).reshape(B * S, 256)


def _gmerge(g0, g1, g2, gb, ya, yb, yc):
    return (jax.nn.sigmoid(g0 + gb[:, 0:D]) * ya + jax.nn.sigmoid(g1 + gb[:, D:2 * D]) * yb
            + jax.nn.sigmoid(g2 + gb[:, 2 * D:3 * D]) * yc)


def _gated_merge(gates, gb, ya, yb, yc, name):
    def fn(tv, fv):
        return [_gmerge(tv[0], tv[1], tv[2], fv[0], tv[3], tv[4], tv[5])], []
    return _rowwise(fn, [(gates, D, 0), (gates, D, 1), (gates, D, 2), ya, yb, yc], [gb], [(D, F32)], [], name=name)[0]


def _gated_merge_bwd(gates, gb, ya, yb, yc, dm, name):
    def fn(tv, fv):
        _, vjp = jax.vjp(_gmerge, tv[0], tv[1], tv[2], fv[0], tv[3], tv[4], tv[5])
        d0, d1, d2, dgb, da, db, dc = vjp(tv[6])
        return [[d0, d1, d2], da, db, dc], [dgb]
    return _rowwise(fn, [(gates, D, 0), (gates, D, 1), (gates, D, 2), ya, yb, yc, dm], [gb],
                    [(3 * D, F32), (D, F32), (D, F32), (D, F32)], [(1, 3 * D)], name=name)


def _pool_affine(t1, pb, ps, dout, name):
    if dout is None:
        def fn(tv, fv):
            return [(tv[0] + fv[0]) * fv[1]], []
        return _rowwise(fn, [t1], [pb, ps], [(POOLW, F32)], [], name=name)[0]

    def fnb(tv, fv):
        t2, vjp = jax.vjp(lambda t, b, s: (t + b) * s, tv[0], fv[0], fv[1])
        dt, db, dsc = vjp(tv[1])
        return [dt, t2], [db, dsc]
    return _rowwise(fnb, [t1, dout], [pb, ps], [(POOLW, F32), (POOLW, F32)], [(1, POOLW), (1, POOLW)], name=name)


def _dt_softplus(dt_raw, dt_bias, ddt, name):
    f = lambda r, b: _softplus(r + b)
    if ddt is None:
        def fn(tv, fv):
            return [f(tv[0], fv[0])], []
        return _rowwise(fn, [dt_raw], [dt_bias], [(16, F32)], [], name=name, tm=1024)[0]

    def fnb(tv, fv):
        _, vjp = jax.vjp(f, tv[0], fv[0])
        dr, db = vjp(tv[1])
        return [dr], [db]
    return _rowwise(fnb, [dt_raw, ddt], [dt_bias], [(16, F32)], [(1, 16)], name=name, tm=1024)


def _adamw(w, g, m, v, name):
    R, C = w.shape
    tm = _pick(R, (256, 128, 64, 32, 16, 8))
    c1 = 1.0 / (1.0 - ADAM_B1 ** ADAM_STEP)
    c2 = 1.0 / (1.0 - ADAM_B2 ** ADAM_STEP)

    def fn(tv, fv):
        wv, gv, mv, vv = tv
        mn = ADAM_B1 * mv + (1.0 - ADAM_B1) * gv
        vn = ADAM_B2 * vv + (1.0 - ADAM_B2) * (gv * gv)
        delta = -ADAM_LR * ((mn * c1) / (jnp.sqrt(vn * c2) + ADAM_EPS) + ADAM_WD * wv)
        return [delta, mn, vn], []
    return _rowwise(fn, [w, g, m, v], [], [(C, F32)] * 3, [], name=name, tm=tm)


def _ffn_fwd(x, w13, w2, g, b, tag):
    h = _mm(x, w13, name=f"{tag}_h")
    s = _swiglu_act(h, name=f"{tag}_act")
    y = _mm(s, w2, name=f"{tag}_y")
    r, out = _res_ln_fwd(x, y, g, b, 0.5, name=f"{tag}_ln")
    return out, dict(x=x, h=h, r=r)


def _ffn_bwd(dout, sv, w13, w2, g, b, tag):
    dskip, dy, dg, db = _ln_bwd(sv['r'], g, b, dout, 0.5, name=f"{tag}_lnb")
    ds = _mm(dy, w2, tb=True, name=f"{tag}_ds")
    dh, s = _swiglu_act_bwd(sv['h'], ds, name=f"{tag}_actb")
    dw2 = _mm(s, dy, ta=True, name=f"{tag}_dw2")
    dw13 = _mm(sv['x'], dh, ta=True, name=f"{tag}_dw13")
    dx = _mm(dh, w13, tb=True, add=dskip, name=f"{tag}_dx")
    return dx, dict(w13=dw13, w2=dw2, g=dg, b=db)


def _mixer_fwd(x1, W, bias_all, tag):
    T = x1.shape[0]
    hcat = _mm(x1, W['w_in_r'], name=f"{tag}_hcat")
    u = hcat[:, O_U:O_Z]
    z = hcat[:, O_Z:O_XBC]
    xbc = hcat[:, O_XBC:O_Q]
    gates = hcat[:, O_G:O_DT]
    dt_raw = hcat[:, O_DT:O_DT + 16]
    pooled = _pool_mean(u, False, name=f"{tag}_pool")
    t1 = _mm(pooled, W['pool_wbd'], name=f"{tag}_pt1")
    t2 = _pool_affine(t1, W['pool_b'], W['pool_scale'], None, name=f"{tag}_paff")
    ya = _mm(t2, W['p_pool'], name=f"{tag}_ya")
    act = _conv_silu(xbc, W['conv_w'], W['conv_b'], name=f"{tag}_conv")
    xs, bm, cm = act[:, :1024], act[:, 1024:1536], act[:, 1536:2048]
    dt = _dt_softplus(dt_raw, W['dt_bias'], None, name=f"{tag}_dt")
    dtg = dt.reshape(T, 4, 4).transpose(1, 0, 2)
    yscan, hs = _ssd_fwd(xs, dtg, bm, cm, W['a_neg'], W['d_skip'], name=f"{tag}_ssd")
    ybn = _ssd_gate_norm(yscan, z, W['ssd_norm'], name=f"{tag}_gn")
    yb = _mm(ybn, W['p_ssd'], name=f"{tag}_yb")
    outs, lses, qkv = [], [], []
    for gi, dil in enumerate(ATTN_DILS):
        qd = _to_dil(hcat[:, O_Q + 256 * gi:O_Q + 256 * gi + 256], dil)
        kd = _to_dil(hcat[:, O_K + 256 * gi:O_K + 256 * gi + 256], dil)
        vd = _to_dil(hcat[:, O_V + 256 * gi:O_V + 256 * gi + 256], dil)
        o, l = _attn_fwd(qd, kd, vd, bias_all[4 * gi:4 * gi + 4], name=f"{tag}_attn{gi}")
        qkv.append((qd, kd, vd))
        outs.append(_from_dil(o))
        lses.append(_from_dil(l))
    ym = _attn_merge(outs, lses, None, name=f"{tag}_amerge")
    ycp = ym.transpose(0, 2, 1, 3).reshape(T, 256)
    yc = _mm(ycp, W['p_attn'], name=f"{tag}_yc")
    merged = _gated_merge(gates, W['gate_b'], ya, yb, yc, name=f"{tag}_gm")
    mix = _mm(merged, W['w_out'], name=f"{tag}_mix")
    r, out = _res_ln_fwd(x1, mix, W['ln2_g'], W['ln2_b'], 1.0, name=f"{tag}_ln")
    sv = dict(x1=x1, z=z, xbc=xbc, gates=gates, dt_raw=dt_raw, pooled=pooled, t1=t1, xs=xs, bm=bm, cm=cm, dtg=dtg,
              hs=hs, yscan=yscan, ybn=ybn, qkv=qkv, outs=outs, lses=lses, ycp=ycp, ya=ya, yb=yb, yc=yc,
              merged=merged, r=r)
    return out, sv


def _mixer_bwd(dout, sv, W, bias_all, tag):
    T = dout.shape[0]
    B = T // S
    gr = {}
    dx1a, dr, gr['ln2_g'], gr['ln2_b'] = _ln_bwd(sv['r'], W['ln2_g'], W['ln2_b'], dout, 1.0, name=f"{tag}_lnb")
    dmerged = _mm(dr, W['w_out'], tb=True, name=f"{tag}_dmerged")
    gr['w_out'] = _mm(sv['merged'], dr, ta=True, name=f"{tag}_dwout")
    dgates, dya, dyb, dyc, gr['gate_b'] = _gated_merge_bwd(sv['gates'], W['gate_b'], sv['ya'], sv['yb'], sv['yc'],
                                                           dmerged, name=f"{tag}_gmb")
    dycp = _mm(dyc, W['p_attn'], tb=True, name=f"{tag}_dycp")
    gr['p_attn'] = _mm(sv['ycp'], dyc, ta=True, name=f"{tag}_dpattn")
    dym = dycp.reshape(B, S, 4, 64).transpose(0, 2, 1, 3)
    dml = _attn_merge(sv['outs'], sv['lses'], dym, name=f"{tag}_amergeb")
    dq, dk, dv, dbias = [], [], [], []
    for gi, dil in enumerate(ATTN_DILS):
        qd, kd, vd = sv['qkv'][gi]
        a, b, c, d = _attn_bwd(qd, kd, vd, bias_all[4 * gi:4 * gi + 4], _hm_to_dil(dml[gi], dil),
                               _hm_to_dil(dml[3 + gi], dil), name=f"{tag}_attnb{gi}")
        dq.append(_dil_to_rows(a))
        dk.append(_dil_to_rows(b))
        dv.append(_dil_to_rows(c))
        dbias.append(d)
    dbias = jnp.concatenate(dbias, axis=0)
    dybn = _mm(dyb, W['p_ssd'], tb=True, name=f"{tag}_dybn")
    gr['p_ssd'] = _mm(sv['ybn'], dyb, ta=True, name=f"{tag}_dpssd")
    dyscan, dz, gr['ssd_norm'] = _ssd_gate_norm_bwd(sv['yscan'], sv['z'], W['ssd_norm'], dybn, name=f"{tag}_gnb")
    dxs, ddtg, dbm, dcm, dak = _ssd_bwd(sv['xs'], sv['dtg'], sv['bm'], sv['cm'], W['a_neg'], W['d_skip'], sv['hs'],
                                        dyscan, name=f"{tag}_ssdb")
    gr['a_neg'], gr['d_skip'] = dak[:, 0, 0:4], dak[:, 1, 0:4]
    ddt = ddtg.transpose(1, 0, 2).reshape(T, 16)
    ddt_raw, gr['dt_bias'] = _dt_softplus(sv['dt_raw'], W['dt_bias'], ddt, name=f"{tag}_dtb")
    dact = jnp.concatenate([dxs, dbm, dcm], axis=1)
    dxbc, gr['conv_w'], gr['conv_b'] = _conv_silu_bwd(sv['xbc'], W['conv_w'], W['conv_b'], dact, name=f"{tag}_convb")
    dt2 = _mm(dya, W['p_pool'], tb=True, name=f"{tag}_dt2")
    dt1, t2, gr['pool_b'], gr['pool_scale'] = _pool_affine(sv['t1'], W['pool_b'], W['pool_scale'], dt2, name=f"{tag}_paffb")
    gr['p_pool'] = _mm(t2, dya, ta=True, name=f"{tag}_dppool")
    dpooled = _mm(dt1, W['pool_wbd'], tb=True, name=f"{tag}_dpooled")
    gr['pool_wbd'] = _mm(sv['pooled'], dt1, ta=True, name=f"{tag}_dpoolw")
    du = _pool_mean(dpooled, True, name=f"{tag}_poolb")
    dhcat = jnp.concatenate([du, dz, dxbc] + dq + dk + dv + [dgates, ddt_raw, jnp.zeros((T, HC - O_DT - 16), F32)], axis=1)
    dx1 = _mm(dhcat, W['w_in_r'], tb=True, add=dx1a, name=f"{tag}_dx1")
    gr['w_in_r'] = _mm(sv['x1'], dhcat, ta=True, name=f"{tag}_dwin")
    return dx1, gr, dbias


def _prep_layer_weights(i, inp, G):
    W = {}
    nat = {}
    for n in BIG:
        g = G[n]
        nat[n] = g.transpose(1, 0, 2).reshape(g.shape[1], 4 * g.shape[2]) if n in COL_SHARDED else g.reshape(4 * g.shape[1], g.shape[2])
    wi = nat['w_in']
    W['w_in_r'] = jnp.concatenate([wi[:, 0:3840], wi[:, 3856:9232], wi[:, 3840:3856], jnp.zeros((D, HC - 9232), wi.dtype)], axis=1)
    for n in BIG:
        if n != 'w_in':
            W[n] = nat[n]
    pw = inp['pool_w'][i].astype(_MXU)
    wbd = jnp.zeros((POOLW, POOLW), _MXU)
    for g in range(4):
        wbd = lax.dynamic_update_slice(wbd, pw[g], (g * POOL_GDIM, g * POOL_GDIM))
    W['pool_wbd'] = wbd
    W['pool_b'] = inp['pool_b'][i].reshape(1, POOLW)
    W['pool_scale'] = inp['pool_scale'][i].reshape(1, POOLW)
    W['conv_w'] = G['conv_w'].transpose(1, 0, 2).reshape(4, 2048)
    W['conv_b'] = inp['conv_b'][i].reshape(1, 2048)
    W['dt_bias'] = inp['dt_bias'][i].reshape(1, 16)
    W['a_neg'] = (-jnp.exp(inp['a_log'][i])).reshape(4, 1, 4)
    W['d_skip'] = inp['d_skip'][i].reshape(4, 1, 4)
    W['ssd_norm'] = inp['ssd_norm'][i].reshape(1, D)
    W['gate_b'] = G['gate_b'].transpose(1, 0, 2).reshape(1, 3 * D)
    for n in ('ln1_g', 'ln1_b', 'ln2_g', 'ln2_b', 'ln3_g', 'ln3_b'):
        W[n] = inp[n][i].reshape(1, D)
    return W


def _gather_layer(i, inp):
    names = BIG + ['gate_b', 'conv_w']
    arrs = [inp[n][i].astype(BF16) for n in BIG] + [inp['gate_b'][i], inp['conv_w'][i]]
    outs = _exchange(arrs, "chips", "gather", name="gather_weights")
    me = 2 * lax.axis_index("x") + lax.axis_index("y")
    return {n: lax.dynamic_update_slice(o, a[None], (me, 0, 0)) for n, o, a in zip(names, outs, arrs)}


def _to_shard_major(n, g):
    if n in COL_SHARDED:
        r, c4 = g.shape
        return g.reshape(r, 4, c4 // 4).transpose(1, 0, 2)
    r4, c = g.shape
    return g.reshape(4, r4 // 4, c)


def _reduce_big(grads):
    names = list(grads)
    halves = []
    for n in names:
        g = grads[n]
        _, r, c = g.shape
        halves.append(g.reshape(4, 2, r // 2, c).transpose(1, 0, 2, 3).reshape(2, 4 * (r // 2), c))
    core = lax.axis_index("c").reshape(1)
    chip_id = (2 * lax.axis_index("x") + lax.axis_index("y")).reshape(1)
    got = _exchange(halves, "cores", "scatter", name="rs_cores")
    chip = [_sum_own_recv(h, t, core, BF16, name="rs_sum2") for h, t in zip(halves, got)]
    chip = [t.reshape(4, t.shape[0] // 4, t.shape[1]) for t in chip]
    got = _exchange(chip, "chips", "scatter", name="rs_chips")
    red = [_sum_own_recv(h, t, chip_id, F32, name="rs_sum4") for h, t in zip(chip, got)]
    other = _exchange(red, "cores", "gather", name="rs_share")
    out = {}
    for n, mine, theirs in zip(names, red, other):
        both = jnp.where(lax.axis_index("c") == 0, jnp.concatenate([mine, theirs]), jnp.concatenate([theirs, mine]))
        out[n] = both
    return out


def _allreduce_small(vec):
    for group in ("cores", "x", "y"):
        recv = _exchange([vec], group, "gather", name=f"ar_{group}")[0]
        vec = _rowwise(lambda tv, fv: ([tv[0] + tv[1]], []), [vec, recv], [], [(128, F32)], [], name=f"ar_add_{group}")[0]
    return vec


def _pack(arrs):
    flat = jnp.concatenate([a.reshape(-1) for a in arrs])
    n = flat.shape[0]
    pad = (-n) % (256 * 128)
    flat = jnp.concatenate([flat, jnp.zeros((pad,), F32)])
    return flat.reshape(-1, 128)


def _unpack(p, shapes):
    flat = p.reshape(-1)
    out, off = [], 0
    for s in shapes:
        sz = int(np.prod(s))
        out.append(flat[off:off + sz].reshape(s))
        off += sz
    return out


def _forward_backward(inp, gather, bias_all):
    x = inp['x'].reshape(-1, D)
    tgt = inp['loss_target'].reshape(-1, D)
    saved, Ws = [], []
    for i in range(NL):
        W = _prep_layer_weights(i, inp, gather(i))
        x1, s1 = _ffn_fwd(x, W['ffn1_w13'], W['ffn1_w2'], W['ln1_g'], W['ln1_b'], "f1")
        x2, s2 = _mixer_fwd(x1, W, bias_all, "mx")
        x3, s3 = _ffn_fwd(x2, W['ffn2_w13'], W['ffn2_w2'], W['ln3_g'], W['ln3_b'], "f2")
        saved.append((s1, s2, s3))
        Ws.append(W)
        x = x3
    dy, lpart = _loss_fwd_bwd(x, tgt, name="loss")
    grads, dbiases = [None] * NL, [None] * NL
    for i in reversed(range(NL)):
        W = Ws[i]
        s1, s2, s3 = saved[i]
        g = {}
        dx2, f = _ffn_bwd(dy, s3, W['ffn2_w13'], W['ffn2_w2'], W['ln3_g'], W['ln3_b'], "f2")
        g['ffn2_w13'], g['ffn2_w2'], g['ln3_g'], g['ln3_b'] = f['w13'], f['w2'], f['g'], f['b']
        dx1, gm, dbiases[i] = _mixer_bwd(dx2, s2, W, bias_all, "mx")
        g.update(gm)
        dy, f = _ffn_bwd(dx1, s1, W['ffn1_w13'], W['ffn1_w2'], W['ln1_g'], W['ln1_b'], "f1")
        g['ffn1_w13'], g['ffn1_w2'], g['ln1_g'], g['ln1_b'] = f['w13'], f['w2'], f['g'], f['b']
        grads[i] = g
    return lpart, dy, grads, dbiases


def _finish_layer_grads(i, g, inp):
    out = {}
    for n in ('ffn1_w13', 'ffn2_w13', 'p_pool', 'p_ssd', 'p_attn', 'w_out'):
        out[n] = g[n]
    d = g['w_in_r']
    out['w_in'] = jnp.concatenate([d[:, 0:3840], d[:, O_DT:O_DT + 16], d[:, 3840:O_DT]], axis=1)
    out['ffn1_w2'], out['ffn2_w2'] = g['ffn1_w2'], g['ffn2_w2']
    out['pool_w'] = jnp.stack([g['pool_wbd'][k * POOL_GDIM:(k + 1) * POOL_GDIM, k * POOL_GDIM:(k + 1) * POOL_GDIM] for k in range(4)])
    out['pool_b'] = g['pool_b'].reshape(4, POOL_GDIM)
    out['pool_scale'] = g['pool_scale'].reshape(POOLW)
    out['conv_w'] = g['conv_w']
    out['conv_b'] = g['conv_b'].reshape(2048)
    out['dt_bias'] = g['dt_bias'].reshape(16)
    out['a_log'] = (g['a_neg'].reshape(16)) * (-jnp.exp(inp['a_log'][i]))
    out['d_skip'] = g['d_skip'].reshape(16)
    out['ssd_norm'] = g['ssd_norm'].reshape(D)
    out['gate_b'] = g['gate_b'].reshape(3, D)
    for n in ('ln1_g', 'ln1_b', 'ln2_g', 'ln2_b', 'ln3_g', 'ln3_b'):
        out[n] = g[n].reshape(D)
    return out


def kernel(x, ffn1_w13, ffn1_w2, ln1_g, ln1_b, w_in, gate_b, pool_w, pool_b, pool_scale, conv_w, conv_b,
           dt_bias, a_log, d_skip, ssd_norm, rel_bias, p_pool, p_ssd, p_attn, w_out, ln2_g, ln2_b, ffn2_w13,
           ffn2_w2, ln3_g, ln3_b, loss_target, m_ffn1_w13, m_ffn1_w2, m_ln1_g, m_ln1_b, m_w_in, m_gate_b,
           m_pool_w, m_pool_b, m_pool_scale, m_conv_w, m_conv_b, m_dt_bias, m_a_log, m_d_skip, m_ssd_norm,
           m_rel_bias, m_p_pool, m_p_ssd, m_p_attn, m_w_out, m_ln2_g, m_ln2_b, m_ffn2_w13, m_ffn2_w2, m_ln3_g,
           m_ln3_b, v_ffn1_w13, v_ffn1_w2, v_ln1_g, v_ln1_b, v_w_in, v_gate_b, v_pool_w, v_pool_b,
           v_pool_scale, v_conv_w, v_conv_b, v_dt_bias, v_a_log, v_d_skip, v_ssd_norm, v_rel_bias, v_p_pool,
           v_p_ssd, v_p_attn, v_w_out, v_ln2_g, v_ln2_b, v_ffn2_w13, v_ffn2_w2, v_ln3_g, v_ln3_b):
    inp = dict(locals())
    maps = jnp.asarray(_bucket_maps())
    bias_all = _bias_build(rel_bias, maps, name="bias_build")
    lpart, gx, grads, dbiases = _forward_backward(inp, lambda i: _gather_layer(i, inp), bias_all)
    loss = lax.psum(lpart[0, 0], ("x", "y", "c"))
    fins = [_finish_layer_grads(i, grads[i], inp) for i in range(NL)]

    red = [_reduce_big({n: _to_shard_major(n, fins[i][n]) for n in BIG}) for i in range(NL)]
    gout = {n: jnp.stack([red[i][n] for i in range(NL)]) for n in BIG}

    small_l = [n for n in SMALL if n != 'rel_bias']
    drel = _bias_reduce(jnp.stack(dbiases), maps, name="bias_reduce")[:, 0, :32].T
    small_arrs = [jnp.stack([fins[i][n] for i in range(NL)]) for n in small_l] + [drel]
    packed = _allreduce_small(_pack(small_arrs))
    gsmall = dict(zip(small_l + ['rel_bias'], _unpack(packed, [a.shape for a in small_arrs])))
    shard = 2 * lax.axis_index("x") + lax.axis_index("y")
    gsmall['gate_b'] = lax.dynamic_slice_in_dim(gsmall['gate_b'], shard * 256, 256, axis=2)
    gsmall['conv_w'] = lax.dynamic_slice_in_dim(gsmall['conv_w'], shard * 512, 512, axis=2)
    gout.update(gsmall)

    delta, new_m, new_v = {}, {}, {}
    for n in BIG:
        shp = inp[n].shape
        two_d = lambda a: a.reshape(shp[0] * shp[1], shp[2])
        d, m, v = _adamw(two_d(inp[n]), two_d(gout[n]), two_d(inp['m_' + n]), two_d(inp['v_' + n]), name="adamw_big")
        delta[n], new_m[n], new_v[n] = d.reshape(shp), m.reshape(shp), v.reshape(shp)
    shapes = [inp[n].shape for n in SMALL]
    d, m, v = _adamw(_pack([inp[n] for n in SMALL]), _pack([gout[n] for n in SMALL]),
                     _pack([inp['m_' + n] for n in SMALL]), _pack([inp['v_' + n] for n in SMALL]), name="adamw_small")
    for n, dd, mm, vv in zip(SMALL, _unpack(d, shapes), _unpack(m, shapes), _unpack(v, shapes)):
        delta[n], new_m[n], new_v[n] = dd, mm, vv

    return (loss, gx.reshape(x.shape), *[gout[n] for n in WEIGHTS], *[delta[n] for n in WEIGHTS],
            *[new_m[n] for n in WEIGHTS], *[new_v[n] for n in WEIGHTS])
```

```python
import functools

import numpy as np
import jax
import jax.numpy as jnp
from jax import lax
from jax.experimental import pallas as pl
from jax.experimental.pallas import tpu as pltpu

F32 = jnp.float32
BF16 = jnp.bfloat16
_MXU = jnp.bfloat16
_ACT = jnp.bfloat16
_VMEM_LIMIT = 56 * 1024 * 1024

S = 2048
D = 1024
NL = 4
DFF = 2816
LN_EPS = 1e-5
SSD_EPS = 1e-5
ALPHA = (2.0 * NL) ** 0.25
POOLW = 768
POOL_WINDOWS = (2, 4, 8, 16)
POOL_GDIM = 192
CH = 128
ATTN_DILS = (1, 4, 16)
HC = 9728
O_U, O_Z, O_XBC, O_Q, O_K, O_V, O_G, O_DT = 0, 768, 1792, 3840, 4608, 5376, 6144, 9216

ADAM_LR, ADAM_B1, ADAM_B2, ADAM_EPS, ADAM_WD, ADAM_STEP = 0.001, 0.9, 0.999, 1e-08, 0.01, 10

WEIGHTS = ['ffn1_w13', 'ffn1_w2', 'ln1_g', 'ln1_b', 'w_in', 'gate_b', 'pool_w', 'pool_b', 'pool_scale', 'conv_w',
           'conv_b', 'dt_bias', 'a_log', 'd_skip', 'ssd_norm', 'rel_bias', 'p_pool', 'p_ssd', 'p_attn', 'w_out',
           'ln2_g', 'ln2_b', 'ffn2_w13', 'ffn2_w2', 'ln3_g', 'ln3_b']
BIG = ['ffn1_w13', 'ffn1_w2', 'w_in', 'p_pool', 'p_ssd', 'p_attn', 'w_out', 'ffn2_w13', 'ffn2_w2']
COL_SHARDED = {'ffn1_w13', 'ffn2_w13', 'w_in', 'p_pool', 'p_attn'}
SMALL = [n for n in WEIGHTS if n not in BIG]


def _pcall(body, **kw):
    return pl.pallas_call(body, **kw)


def _cp(sem=None):
    return pltpu.CompilerParams(dimension_semantics=sem, vmem_limit_bytes=_VMEM_LIMIT)


def _pick(n, cands):
    for c in cands:
        if n % c == 0:
            return c
    raise ValueError(f"no tile for {n}")


def _mm(a, b, *, ta=False, tb=False, add=None, out_dtype=F32, name):
    if ta:
        K, M = a.shape
    else:
        M, K = a.shape
    if tb:
        N, K2 = b.shape
    else:
        K2, N = b.shape
    assert K == K2, (a.shape, b.shape, ta, tb)
    sa, sb, so = a.dtype.itemsize, b.dtype.itemsize, jnp.dtype(out_dtype).itemsize
    tm, tn, tk = _mm_tiles(M, N, K, sa, sb, so + (4 if add is not None else 0))
    nk = K // tk
    a_bytes, b_bytes = M * K * sa, K * N * sb
    j_outer = nk == 1 and (b_bytes + a_bytes * (N // tn) < a_bytes + b_bytes * (M // tm))
    ij = (lambda p, q: (q, p)) if j_outer else (lambda p, q: (p, q))

    def im(f):
        return lambda p, q, k: f(*ij(p, q), k)

    a_spec = pl.BlockSpec((tk, tm), im(lambda i, j, k: (k, i))) if ta else pl.BlockSpec((tm, tk), im(lambda i, j, k: (i, k)))
    b_spec = pl.BlockSpec((tn, tk), im(lambda i, j, k: (j, k))) if tb else pl.BlockSpec((tk, tn), im(lambda i, j, k: (k, j)))
    o_spec = pl.BlockSpec((tm, tn), im(lambda i, j, k: (i, j)))
    dims = (((0 if ta else 1,), (1 if tb else 0,)), ((), ()))
    has_add = add is not None

    def body(*refs):
        a_ref, b_ref = refs[0], refs[1]
        add_ref = refs[2] if has_add else None
        o_ref = refs[3] if has_add else refs[2]
        part = lax.dot_general(a_ref[...].astype(_MXU), b_ref[...].astype(_MXU), dims, preferred_element_type=F32)

        def finish(r):
            if has_add:
                r = r + add_ref[...]
            o_ref[...] = r.astype(out_dtype)

        if nk == 1:
            finish(part)
        else:
            acc = refs[-1]
            k = pl.program_id(2)

            @pl.when(k == 0)
            def _():
                acc[...] = part

            @pl.when(k > 0)
            def _():
                acc[...] += part

            @pl.when(k == nk - 1)
            def _():
                finish(acc[...])

    in_specs = [a_spec, b_spec]
    args = [a, b]
    if has_add:
        in_specs.append(o_spec)
        args.append(add)
    gm, gn = M // tm, N // tn
    return _pcall(
        body, name=name, grid=((gn, gm, nk) if j_outer else (gm, gn, nk)), in_specs=in_specs, out_specs=o_spec,
        out_shape=jax.ShapeDtypeStruct((M, N), out_dtype),
        scratch_shapes=([pltpu.VMEM((tm, tn), F32)] if nk > 1 else []),
        compiler_params=_cp(("parallel", "parallel", "arbitrary")),
    )(*args)


_MM_VMEM_BUDGET = 40 * 1024 * 1024


def _divisors128(n, cap):
    return [d for d in range(128, min(n, cap) + 1, 128) if n % d == 0][::-1]


def _mm_tiles(M, N, K, sa, sb, so):
    best = None
    for tm in _divisors128(M, 1024):
        for tn in _divisors128(N, 2560):
            for tk in ([K] if K <= 4096 else []) + _divisors128(K, 2048):
                nk = K // tk
                need = 2 * (tm * tk * sa + tk * tn * sb + tm * tn * so) + (tm * tn * 4 if nk > 1 else 0)
                need += tm * tk * 2 + tk * tn * 2 + tm * tn * 4
                if need > _MM_VMEM_BUDGET:
                    continue
                score = (tm * tn, tk)
                if best is None or score > best[0]:
                    best = (score, (tm, tn, tk))
                break
    assert best is not None, (M, N, K)
    return best[1]


def _store(ref, val):
    if isinstance(val, (list, tuple)):
        off = 0
        for p in val:
            w = p.shape[1]
            ref[:, off:off + w] = p.astype(ref.dtype)
            off += w
    else:
        ref[...] = val.astype(ref.dtype)


def _acc_store(ref, val, first):
    pieces = val if isinstance(val, (list, tuple)) else [val]
    off = 0
    for p in pieces:
        w = p.shape[1]

        @pl.when(first)
        def _(p=p, off=off, w=w):
            ref[:, off:off + w] = p

        @pl.when(jnp.logical_not(first))
        def _(p=p, off=off, w=w):
            ref[:, off:off + w] += p

        off += w


def _rowwise(fn, tiled, full, out_tiled, out_acc, *, name, tm=256):
    arrs, specs = [], []
    for t in tiled:
        arr, w, cb = t if isinstance(t, tuple) else (t, t.shape[1], 0)
        arrs.append(arr)
        specs.append(pl.BlockSpec((tm, w), functools.partial(lambda i, cb: (i, cb), cb=cb)))
    R = arrs[0].shape[0]
    assert R % tm == 0
    for f in full:
        arrs.append(f)
        specs.append(pl.BlockSpec(f.shape, functools.partial(lambda i, nd: (0,) * nd, nd=f.ndim)))
    nt, nf, no = len(tiled), len(full), len(out_tiled)

    def body(*refs):
        tv = [r[...] for r in refs[:nt]]
        fv = [r[...] for r in refs[nt:nt + nf]]
        ot, oa = fn(tv, fv)
        for r, v in zip(refs[nt + nf:nt + nf + no], ot):
            _store(r, v)
        first = pl.program_id(0) == 0
        for r, v in zip(refs[nt + nf + no:], oa):
            _acc_store(r, v, first)

    out_shape = [jax.ShapeDtypeStruct((R, c), dt) for c, dt in out_tiled]
    out_specs = [pl.BlockSpec((tm, c), lambda i: (i, 0)) for c, _ in out_tiled]
    for shp in out_acc:
        out_shape.append(jax.ShapeDtypeStruct(shp, F32))
        out_specs.append(pl.BlockSpec(shp, lambda i: (0, 0)))
    return _pcall(body, name=name, grid=(R // tm,), in_specs=specs, out_specs=out_specs, out_shape=out_shape,
                  compiler_params=_cp(("arbitrary",)))(*arrs)


def _group(group):
    x, y, c = lax.axis_index("x"), lax.axis_index("y"), lax.axis_index("c")
    if group == "chips":
        return 2 * x + y, [((x, 1 - y, c), 2 * x + 1 - y), ((1 - x, y, c), 2 * (1 - x) + y),
                           ((1 - x, 1 - y, c), 2 * (1 - x) + 1 - y)]
    if group == "cores":
        return c, [((x, y, 1 - c), 1 - c)]
    if group == "x":
        return x, [((1 - x, y, c), 1 - x)]
    return y, [((x, 1 - y, c), 1 - y)]


def _exchange(arrs, group, mode, name):
    chips = group == "chips"
    k = len(arrs)
    npeer = 3 if chips else 1

    def body(*refs):
        ins, outs = refs[:k], refs[k:2 * k]
        send_sems, recv_sems = refs[2 * k:]
        me, peers = _group(group)
        remote = []
        for i in range(k):
            for p, (dev, slot) in enumerate(peers):
                src = ins[i].at[slot] if mode == "scatter" else ins[i]
                if not chips:
                    dst = outs[i]
                else:
                    dst = outs[i].at[p] if mode == "scatter" else outs[i].at[me]
                cp = pltpu.make_async_remote_copy(src_ref=src, dst_ref=dst, send_sem=send_sems.at[i, p],
                                                  recv_sem=recv_sems.at[i, p], device_id=dev,
                                                  device_id_type=pl.DeviceIdType.MESH)
                cp.start()
                remote.append(cp)
        for cp in remote:
            cp.wait_recv()
        for cp in remote:
            cp.wait_send()

    def oshape(a):
        piece = a.shape[1:] if mode == "scatter" else a.shape
        if chips:
            piece = ((3,) if mode == "scatter" else (4,)) + piece
        return jax.ShapeDtypeStruct(piece, a.dtype)

    any_spec = pl.BlockSpec(memory_space=pl.ANY)
    return _pcall(body, name=name, in_specs=[any_spec] * k, out_specs=[any_spec] * k, out_shape=[oshape(a) for a in arrs],
                  scratch_shapes=[pltpu.SemaphoreType.DMA((k, npeer)), pltpu.SemaphoreType.DMA((k, npeer))])(*arrs)


def _sum_own_recv(own, recv, me, out_dtype, name):
    n, R, C = own.shape
    nr = 1 if recv.ndim == 2 else recv.shape[0]
    tr = _pick(R, (256, 128, 64, 32, 16, 8))

    def body(me_ref, own_ref, *refs):
        o_ref = refs[-1]
        acc = own_ref[...].astype(F32)
        for r in refs[:-1]:
            acc = acc + r[...].astype(F32)
        o_ref[...] = acc.astype(out_dtype)

    specs = [pl.BlockSpec((None, tr, C), lambda i, me_ref: (me_ref[0], i, 0))]
    args = [own]
    if recv.ndim == 2:
        specs.append(pl.BlockSpec((tr, C), lambda i, me_ref: (i, 0)))
        args.append(recv)
    else:
        for p in range(nr):
            specs.append(pl.BlockSpec((None, tr, C), functools.partial(lambda i, me_ref, p: (p, i, 0), p=p)))
            args.append(recv)
    gs = pltpu.PrefetchScalarGridSpec(num_scalar_prefetch=1, grid=(R // tr,), in_specs=specs,
                                      out_specs=pl.BlockSpec((tr, C), lambda i, me_ref: (i, 0)))
    return _pcall(body, name=name, grid_spec=gs, out_shape=jax.ShapeDtypeStruct((R, C), out_dtype),
                  compiler_params=_cp(("parallel",)))(me, *args)


def _silu(x):
    return x * jax.nn.sigmoid(x)


def _ln(r, g, b):
    mu = jnp.mean(r, -1, keepdims=True)
    xc = r - mu
    var = jnp.mean(xc * xc, -1, keepdims=True)
    return xc * lax.rsqrt(var + LN_EPS) * g + b


def _softplus(x):
    return jnp.maximum(x, 0.0) + jnp.log1p(jnp.exp(-jnp.abs(x)))


def _res_ln_fwd(x, y, g, b, res, name):
    def fn(tv, fv):
        r = ALPHA * tv[0] + res * tv[1]
        out = _ln(r, fv[0], fv[1])
        return [r, out, out], []
    return _rowwise(fn, [x, y], [g, b], [(D, F32), (D, F32), (D, _ACT)], [], name=name)


def _ln_bwd(r, g, b, dout, res, name):
    def fn(tv, fv):
        _, vjp = jax.vjp(_ln, tv[0], fv[0], fv[1])
        dr, dg, db = vjp(tv[1])
        return [ALPHA * dr, res * dr], [dg, db]
    return _rowwise(fn, [r, dout], [g, b], [(D, F32), (D, _ACT)], [(1, D), (1, D)], name=name)


def _swiglu_act(h, name):
    def fn(tv, fv):
        return [_silu(tv[0]) * tv[1]], []
    return _rowwise(fn, [(h, DFF, 0), (h, DFF, 1)], [], [(DFF, _ACT)], [], name=name)[0]


def _swiglu_act_bwd(h, ds, name):
    def fn(tv, fv):
        s, vjp = jax.vjp(lambda a, g: _silu(a) * g, tv[0], tv[1])
        da, dg = vjp(tv[2])
        return [[da, dg], s], []
    return _rowwise(fn, [(h, DFF, 0), (h, DFF, 1), ds], [], [(2 * DFF, _ACT), (DFF, _ACT)], [], name=name)


def _loss_fwd_bwd(y, tgt, name):
    def fn(tv, fv):
        e = tv[0] - tv[1]
        row = jnp.sum(e * e, axis=1, keepdims=True)
        tot = jnp.sum(row, axis=0, keepdims=True) * (0.5 / D)
        return [e * (1.0 / D)], [jnp.broadcast_to(tot, (1, 128))]
    return _rowwise(fn, [y, tgt], [], [(D, F32)], [(1, 128)], name=name)


def _shift_down(x, k, row):
    return jnp.where(row >= k, pltpu.roll(x, k, axis=0), 0.0)


def _shift_up(x, k, row):
    n = x.shape[0]
    return jnp.where(row < n - k, pltpu.roll(x, n - k, axis=0), 0.0)


def _pool_window_masks(j):
    lane = lax.broadcasted_iota(jnp.int32, (1, 128), 1) + j * 128
    grp = lane // POOL_GDIM
    return [grp == g for g in range(4)]


def _pool_mean(u, bwd, name):
    T = u.shape[0]
    B = T // S

    def body(u_ref, o_ref):
        j = pl.program_id(1)
        x = u_ref[...]
        row = lax.broadcasted_iota(jnp.int32, (S, 1), 0)
        masks = _pool_window_masks(j)
        inv = [1.0 / jnp.minimum(row + 1, w).astype(F32) for w in POOL_WINDOWS]
        if not bwd:
            s2 = x + _shift_down(x, 1, row)
            s4 = s2 + _shift_down(s2, 2, row)
            s8 = s4 + _shift_down(s4, 4, row)
            s16 = s8 + _shift_down(s8, 8, row)
            mean = jnp.where(masks[0], s2 * inv[0], jnp.where(masks[1], s4 * inv[1],
                             jnp.where(masks[2], s8 * inv[2], s16 * inv[3])))
            o_ref[...] = (mean - x).astype(o_ref.dtype)
        else:
            g = [jnp.where(masks[i], x * inv[i], 0.0) for i in range(4)]
            t = g[3]
            t = t + _shift_up(t, 8, row) + g[2]
            t = t + _shift_up(t, 4, row) + g[1]
            t = t + _shift_up(t, 2, row) + g[0]
            t = t + _shift_up(t, 1, row)
            o_ref[...] = (t - x).astype(o_ref.dtype)

    spec = pl.BlockSpec((S, 128), lambda b, j: (b, j))
    return _pcall(body, name=name, grid=(B, POOLW // 128), in_specs=[spec], out_specs=spec,
                  out_shape=jax.ShapeDtypeStruct((T, POOLW), _ACT), compiler_params=_cp(("parallel", "parallel")))(u)


def _conv_silu(xbc, w, b, name):
    T, C = xbc.shape
    B = T // S

    def body(x_ref, w_ref, b_ref, o_ref):
        x = x_ref[...]
        row = lax.broadcasted_iota(jnp.int32, (S, 1), 0)
        c = b_ref[...] + w_ref[3:4, :] * x
        for s in range(1, 4):
            c = c + w_ref[3 - s:4 - s, :] * _shift_down(x, s, row)
        o_ref[...] = _silu(c)

    return _pcall(body, name=name, grid=(B, C // 128),
                  in_specs=[pl.BlockSpec((S, 128), lambda b, j: (b, j)), pl.BlockSpec((4, 128), lambda b, j: (0, j)),
                            pl.BlockSpec((1, 128), lambda b, j: (0, j))],
                  out_specs=pl.BlockSpec((S, 128), lambda b, j: (b, j)),
                  out_shape=jax.ShapeDtypeStruct((T, C), F32), compiler_params=_cp(("parallel", "parallel")))(xbc, w, b)


def _conv_silu_bwd(xbc, w, b, dact, name):
    T, C = xbc.shape
    B = T // S

    def body(x_ref, w_ref, b_ref, d_ref, dx_ref, dw_ref, db_ref):
        bi = pl.program_id(1)
        x = x_ref[...]
        row = lax.broadcasted_iota(jnp.int32, (S, 1), 0)
        xs = [x] + [_shift_down(x, s, row) for s in range(1, 4)]
        c = b_ref[...]
        for s in range(4):
            c = c + w_ref[3 - s:4 - s, :] * xs[s]
        _, vjp = jax.vjp(_silu, c)
        dc = vjp(d_ref[...])[0]
        dx = w_ref[3:4, :] * dc
        for s in range(1, 4):
            dx = dx + w_ref[3 - s:4 - s, :] * _shift_up(dc, s, row)
        dx_ref[...] = dx.astype(dx_ref.dtype)
        first = bi == 0
        for s in range(4):
            _acc_rows(dw_ref, 3 - s, jnp.sum(dc * xs[s], axis=0, keepdims=True), first)
        _acc_rows(db_ref, 0, jnp.sum(dc, axis=0, keepdims=True), first)

    blk = pl.BlockSpec((S, 128), lambda j, b: (b, j))
    return _pcall(body, name=name, grid=(C // 128, B),
                  in_specs=[blk, pl.BlockSpec((4, 128), lambda j, b: (0, j)), pl.BlockSpec((1, 128), lambda j, b: (0, j)), blk],
                  out_specs=[blk, pl.BlockSpec((4, 128), lambda j, b: (0, j)), pl.BlockSpec((1, 128), lambda j, b: (0, j))],
                  out_shape=[jax.ShapeDtypeStruct((T, C), _ACT), jax.ShapeDtypeStruct((4, C), F32),
                             jax.ShapeDtypeStruct((1, C), F32)],
                  compiler_params=_cp(("parallel", "arbitrary")))(xbc, w, b, dact)


def _acc_rows(ref, r, val, first):
    @pl.when(first)
    def _():
        ref[r:r + 1, :] = val

    @pl.when(jnp.logical_not(first))
    def _():
        ref[r:r + 1, :] += val


def _tri_consts():
    i = lax.broadcasted_iota(jnp.int32, (CH, CH), 0)
    j = lax.broadcasted_iota(jnp.int32, (CH, CH), 1)
    return (i == j).astype(F32), (j <= i).astype(F32), (i <= j).astype(F32), i >= j


def _ssd_chunk(h, x, dt, Bm, Cm, a, dsk, consts):
    eye, tril, triu, lower = consts
    Bb = Bm.astype(_MXU)
    Cb = Cm.astype(_MXU)
    cb = lax.dot_general(Cb, Bb, (((1,), (1,)), ((), ())), preferred_element_type=F32)
    ys, hn = [], []
    for e in range(4):
        adt = dt[e] * a[e]
        adt_row = jnp.sum(adt * eye, axis=0, keepdims=True)
        cs_col = jnp.sum(adt_row * tril, axis=1, keepdims=True)
        cs_row = jnp.sum(adt * triu, axis=0, keepdims=True)
        cs_last = jnp.sum(adt, axis=0, keepdims=True)
        decay = jnp.exp(jnp.where(lower, cs_col - cs_row, -jnp.inf))
        xb = (x[e] * dt[e]).astype(_MXU)
        y_diag = jnp.dot((cb * decay).astype(_MXU), xb, preferred_element_type=F32)
        bdec = (Bm * jnp.exp(cs_last - cs_col)).astype(_MXU)
        st = lax.dot_general(bdec, xb, (((0,), (0,)), ((), ())), preferred_element_type=F32)
        hn.append(h[e] * jnp.exp(cs_last) + st)
        y_off = jnp.exp(cs_col) * jnp.dot(Cb, h[e].astype(_MXU), preferred_element_type=F32)
        ys.append(y_diag + y_off + dsk[e] * x[e])
    return ys, hn


def _ssd_specs(order):
    def im(f):
        return lambda p, q: f(*order(p, q))
    xs = pl.BlockSpec((S, 256), im(lambda b, g: (b, g)))
    dt = pl.BlockSpec((None, S, 4), im(lambda b, g: (g, b, 0)))
    bc = pl.BlockSpec((S, 128), im(lambda b, g: (b, g)))
    hd = pl.BlockSpec((None, 1, 4), im(lambda b, g: (g, 0, 0)))
    hs = pl.BlockSpec((None, None, S // CH, 4, 128, 64), im(lambda b, g: (b, g, 0, 0, 0, 0)))
    return xs, dt, bc, hd, hs


def _ssd_fwd(xs, dtg, bm, cm, a, dsk, name):
    T = xs.shape[0]
    B = T // S
    nc = S // CH

    def body(x_ref, dt_ref, b_ref, c_ref, a_ref, k_ref, y_ref, hs_ref, h_ref):
        consts = _tri_consts()
        h_ref[...] = jnp.zeros_like(h_ref)
        al = [a_ref[:, e:e + 1] for e in range(4)]
        kl = [k_ref[:, e:e + 1] for e in range(4)]

        def step(c, carry):
            r0 = pl.multiple_of(c * CH, CH)
            rows = pl.ds(r0, CH)
            h = [h_ref[e] for e in range(4)]
            for e in range(4):
                hs_ref[c, e] = h[e]
            x = [x_ref[rows, 64 * e:64 * e + 64] for e in range(4)]
            dt = [dt_ref[rows, e:e + 1] for e in range(4)]
            ys, hn = _ssd_chunk(h, x, dt, b_ref[rows, :], c_ref[rows, :], al, kl, consts)
            for e in range(4):
                y_ref[rows, 64 * e:64 * e + 64] = ys[e]
                h_ref[e] = hn[e]
            return carry

        lax.fori_loop(0, nc, step, 0)

    sx, sdt, sbc, shd, shs = _ssd_specs(lambda b, g: (b, g))
    return _pcall(body, name=name, grid=(B, 4), in_specs=[sx, sdt, sbc, sbc, shd, shd], out_specs=[sx, shs],
                  out_shape=[jax.ShapeDtypeStruct((T, 1024), F32), jax.ShapeDtypeStruct((B, 4, nc, 4, 128, 64), F32)],
                  scratch_shapes=[pltpu.VMEM((4, 128, 64), F32)],
                  compiler_params=_cp(("parallel", "parallel")))(xs, dtg, bm, cm, a, dsk)


def _lane_place(vals, width):
    lane = lax.broadcasted_iota(jnp.int32, (1, width), 1)
    out = jnp.zeros((1, width), F32)
    for e, v in enumerate(vals):
        out = out + jnp.where(lane == e, v, 0.0)
    return out


def _ssd_bwd(xs, dtg, bm, cm, a, dsk, hs, dy, name):
    T = xs.shape[0]
    B = T // S
    nc = S // CH

    def body(x_ref, dt_ref, b_ref, c_ref, a_ref, k_ref, hs_ref, dy_ref,
             dx_ref, ddt_ref, db_ref, dc_ref, dak_ref, dh_ref, sc_ref):
        bi = pl.program_id(1)
        consts = _tri_consts()
        dh_ref[...] = jnp.zeros_like(dh_ref)
        sc_ref[...] = jnp.zeros_like(sc_ref)
        al = [a_ref[:, e:e + 1] for e in range(4)]
        kl = [k_ref[:, e:e + 1] for e in range(4)]

        def step(i, carry):
            c = nc - 1 - i
            r0 = pl.multiple_of(c * CH, CH)
            rows = pl.ds(r0, CH)
            h = [hs_ref[c, e] for e in range(4)]
            x = [x_ref[rows, 64 * e:64 * e + 64] for e in range(4)]
            dt = [dt_ref[rows, e:e + 1] for e in range(4)]
            f = functools.partial(_ssd_chunk, consts=consts)
            _, vjp = jax.vjp(f, h, x, dt, b_ref[rows, :], c_ref[rows, :], al, kl)
            dys = [dy_ref[rows, 64 * e:64 * e + 64] for e in range(4)]
            dhn = [dh_ref[e] for e in range(4)]
            dh, dx, ddt, dB, dC, da, dk = vjp((dys, dhn))
            for e in range(4):
                dh_ref[e] = dh[e]
                dx_ref[rows, 64 * e:64 * e + 64] = dx[e]
                ddt_ref[rows, e:e + 1] = ddt[e]
            db_ref[rows, :] = dB
            dc_ref[rows, :] = dC
            sc_ref[0:1, :] += _lane_place(da, 128)
            sc_ref[1:2, :] += _lane_place(dk, 128)
            return carry

        lax.fori_loop(0, nc, step, 0)
        first = bi == 0

        @pl.when(first)
        def _():
            dak_ref[...] = sc_ref[...]

        @pl.when(jnp.logical_not(first))
        def _():
            dak_ref[...] += sc_ref[...]

    sx, sdt, sbc, shd, shs = _ssd_specs(lambda g, b: (b, g))
    return _pcall(body, name=name, grid=(4, B), in_specs=[sx, sdt, sbc, sbc, shd, shd, shs, sx],
                  out_specs=[sx, sdt, sbc, sbc, pl.BlockSpec((None, 8, 128), lambda g, b: (g, 0, 0))],
                  out_shape=[jax.ShapeDtypeStruct((T, 1024), F32), jax.ShapeDtypeStruct((4, T, 4), F32),
                             jax.ShapeDtypeStruct((T, 512), F32), jax.ShapeDtypeStruct((T, 512), F32),
                             jax.ShapeDtypeStruct((4, 8, 128), F32)],
                  scratch_shapes=[pltpu.VMEM((4, 128, 64), F32), pltpu.VMEM((8, 128), F32)],
                  compiler_params=_cp(("parallel", "arbitrary")))(xs, dtg, bm, cm, a, dsk, hs, dy)


def _gate_norm(y, z, nw):
    t = y * _silu(z)
    return t * lax.rsqrt(jnp.mean(t * t, axis=-1, keepdims=True) + SSD_EPS) * nw


def _ssd_gate_norm(y, z, nw, name):
    def fn(tv, fv):
        return [[_gate_norm(tv[g], tv[4 + g], fv[0][:, 256 * g:256 * g + 256]) for g in range(4)]], []
    tiled = [(y, 256, g) for g in range(4)] + [(z, 256, g) for g in range(4)]
    return _rowwise(fn, tiled, [nw], [(1024, _ACT)], [], name=name)[0]


def _ssd_gate_norm_bwd(y, z, nw, dout, name):
    def fn(tv, fv):
        dys, dzs, dns = [], [], []
        for g in range(4):
            _, vjp = jax.vjp(_gate_norm, tv[g], tv[4 + g], fv[0][:, 256 * g:256 * g + 256])
            a, b, c = vjp(tv[8 + g])
            dys.append(a)
            dzs.append(b)
            dns.append(c)
        return [dys, dzs], [dns]
    tiled = [(y, 256, g) for g in range(4)] + [(z, 256, g) for g in range(4)] + [(dout, 256, g) for g in range(4)]
    return _rowwise(fn, tiled, [nw], [(1024, F32), (1024, _ACT)], [(1, 1024)], name=name)


def _t5_bucket_np(dist):
    dist = np.maximum(dist, 0)
    max_exact = 16
    large = max_exact + (np.log(np.maximum(dist, 1) / max_exact) / np.log(2048 / max_exact) * (32 - max_exact)).astype(np.int32)
    large = np.minimum(large, 31)
    return np.where(dist < max_exact, dist, large).astype(np.int32)


def _bucket_maps():
    qi = np.arange(128)[:, None]
    kj = np.arange(256)[None, :]
    return np.stack([_t5_bucket_np((qi - kj + 128) * dil) for dil in ATTN_DILS]).astype(np.int32)


def _bias_build(rel_bias, maps, name):
    def body(tab_ref, map_ref, o_ref):
        hh = pl.program_id(0)
        m = map_ref[...]
        acc = jnp.zeros((128, 256), F32)
        for b in range(32):
            acc = jnp.where(m == b, tab_ref[b, hh], acc)
        o_ref[...] = acc

    return _pcall(body, name=name, grid=(12,),
                  in_specs=[pl.BlockSpec(memory_space=pltpu.SMEM), pl.BlockSpec((None, 128, 256), lambda h: (h // 4, 0, 0))],
                  out_specs=pl.BlockSpec((None, 128, 256), lambda h: (h, 0, 0)),
                  out_shape=jax.ShapeDtypeStruct((12, 128, 256), F32), compiler_params=_cp(("parallel",)))(rel_bias, maps)


def _bias_reduce(dbias, maps, name):
    nl = dbias.shape[0]

    def body(d_ref, map_ref, o_ref):
        m = map_ref[...]
        d = d_ref[0]
        for i in range(1, nl):
            d = d + d_ref[i]
        lane = lax.broadcasted_iota(jnp.int32, (1, 128), 1)
        out = jnp.zeros((1, 128), F32)
        for b in range(32):
            s = jnp.sum(jnp.sum(jnp.where(m == b, d, 0.0), axis=1, keepdims=True), axis=0, keepdims=True)
            out = out + jnp.where(lane == b, s, 0.0)
        o_ref[...] = out

    return _pcall(body, name=name, grid=(12,),
                  in_specs=[pl.BlockSpec((nl, None, 128, 256), lambda h: (0, h, 0, 0)),
                            pl.BlockSpec((None, 128, 256), lambda h: (h // 4, 0, 0))],
                  out_specs=pl.BlockSpec((None, 1, 128), lambda h: (h, 0, 0)),
                  out_shape=jax.ShapeDtypeStruct((12, 1, 128), F32), compiler_params=_cp(("parallel",)))(dbias, maps)


def _attn_block(q, kb, vb, bias, mask):
    s = lax.dot_general(q.astype(_MXU), kb.astype(_MXU), (((1,), (1,)), ((), ())), preferred_element_type=F32) * 0.125 + bias
    s = jnp.where(mask, s, -jnp.inf)
    m = lax.stop_gradient(jnp.max(s, axis=-1, keepdims=True))
    p = jnp.exp(s - m)
    den = jnp.sum(p, axis=-1, keepdims=True)
    out = jnp.dot((p / den).astype(_MXU), vb.astype(_MXU), preferred_element_type=F32)
    return out, m + jnp.log(den)


def _band_mask():
    qi = lax.broadcasted_iota(jnp.int32, (128, 256), 0)
    kj = lax.broadcasted_iota(jnp.int32, (128, 256), 1)
    return (kj >= qi) & (kj <= qi + 128)


def _attn_fwd(q, k, v, bias, name):
    B, _, dil, L, _ = q.shape
    nb = L // 128

    def body(q_ref, k_ref, v_ref, b_ref, o_ref, l_ref):
        mask = _band_mask()
        bias_v = b_ref[...]
        o, l = _attn_block(q_ref[0:128, :], k_ref[0:128, :], v_ref[0:128, :], bias_v[:, 128:], mask[:, 128:])
        o_ref[0:128, :] = o
        l_ref[0:128, :] = l
        if nb > 1:
            def step(n, carry):
                r0 = pl.multiple_of(n * 128, 128)
                p0 = pl.multiple_of(n * 128 - 128, 128)
                o, l = _attn_block(q_ref[pl.ds(r0, 128), :], k_ref[pl.ds(p0, 256), :], v_ref[pl.ds(p0, 256), :],
                                   bias_v, mask)
                o_ref[pl.ds(r0, 128), :] = o
                l_ref[pl.ds(r0, 128), :] = l
                return carry
            lax.fori_loop(1, nb, step, 0)

    blk = pl.BlockSpec((None, None, None, L, 64), lambda b, h, r: (b, h, r, 0, 0))
    lblk = pl.BlockSpec((None, None, None, L, 1), lambda b, h, r: (b, h, r, 0, 0))
    return _pcall(body, name=name, grid=(B, 4, dil),
                  in_specs=[blk, blk, blk, pl.BlockSpec((None, 128, 256), lambda b, h, r: (h, 0, 0))],
                  out_specs=[blk, lblk],
                  out_shape=[jax.ShapeDtypeStruct(q.shape, F32), jax.ShapeDtypeStruct(q.shape[:4] + (1,), F32)],
                  compiler_params=_cp(("parallel", "parallel", "parallel")))(q, k, v, bias)


def _attn_bwd(q, k, v, bias, do, dl, name):
    B, _, dil, L, _ = q.shape
    nb = L // 128

    def body(q_ref, k_ref, v_ref, b_ref, do_ref, dl_ref, dq_ref, dk_ref, dv_ref, db_ref, acc_ref):
        bi, ri = pl.program_id(1), pl.program_id(2)
        mask = _band_mask()
        bias_v = b_ref[...]
        dk_ref[...] = jnp.zeros_like(dk_ref)
        dv_ref[...] = jnp.zeros_like(dv_ref)
        f0 = functools.partial(_attn_block, mask=mask[:, 128:])
        _, vjp = jax.vjp(f0, q_ref[0:128, :], k_ref[0:128, :], v_ref[0:128, :], bias_v[:, 128:])
        dq, dkb, dvb, dbs = vjp((do_ref[0:128, :], dl_ref[0:128, :]))
        dq_ref[0:128, :] = dq
        dk_ref[0:128, :] += dkb
        dv_ref[0:128, :] += dvb
        acc_ref[:, 0:128] = jnp.zeros((128, 128), F32)
        acc_ref[:, 128:256] = dbs
        if nb > 1:
            f1 = functools.partial(_attn_block, mask=mask)

            def step(n, carry):
                r0 = pl.multiple_of(n * 128, 128)
                p0 = pl.multiple_of(n * 128 - 128, 128)
                _, vjp = jax.vjp(f1, q_ref[pl.ds(r0, 128), :], k_ref[pl.ds(p0, 256), :], v_ref[pl.ds(p0, 256), :], bias_v)
                dq, dkb, dvb, dbs = vjp((do_ref[pl.ds(r0, 128), :], dl_ref[pl.ds(r0, 128), :]))
                dq_ref[pl.ds(r0, 128), :] = dq
                dk_ref[pl.ds(p0, 256), :] += dkb
                dv_ref[pl.ds(p0, 256), :] += dvb
                acc_ref[...] += dbs
                return carry
            lax.fori_loop(1, nb, step, 0)
        first = (bi == 0) & (ri == 0)

        @pl.when(first)
        def _():
            db_ref[...] = acc_ref[...]

        @pl.when(jnp.logical_not(first))
        def _():
            db_ref[...] += acc_ref[...]

    blk = pl.BlockSpec((None, None, None, L, 64), lambda h, b, r: (b, h, r, 0, 0))
    lblk = pl.BlockSpec((None, None, None, L, 1), lambda h, b, r: (b, h, r, 0, 0))
    bblk = pl.BlockSpec((None, 128, 256), lambda h, b, r: (h, 0, 0))
    sds = jax.ShapeDtypeStruct(q.shape, F32)
    return _pcall(body, name=name, grid=(4, B, dil), in_specs=[blk, blk, blk, bblk, blk, lblk],
                  out_specs=[blk, blk, blk, bblk], out_shape=[sds, sds, sds, jax.ShapeDtypeStruct((4, 128, 256), F32)],
                  scratch_shapes=[pltpu.VMEM((128, 256), F32)],
                  compiler_params=_cp(("parallel", "arbitrary", "arbitrary")))(q, k, v, bias, do, dl)


def _lse_merge(o0, o1, o2, l0, l1, l2):
    m = lax.stop_gradient(jnp.maximum(jnp.maximum(l0, l1), l2))
    e0, e1, e2 = jnp.exp(l0 - m), jnp.exp(l1 - m), jnp.exp(l2 - m)
    den = e0 + e1 + e2
    return (e0 / den) * o0 + (e1 / den) * o1 + (e2 / den) * o2


def _attn_merge(outs, lses, dy, name):
    B = outs[0].shape[0]
    bwd = dy is not None

    def body(*refs):
        vals = [r[...] for r in refs[:6]]
        if not bwd:
            refs[6][...] = _lse_merge(*vals)
        else:
            _, vjp = jax.vjp(_lse_merge, *vals)
            for r, g in zip(refs[7:], vjp(refs[6][...])):
                r[...] = g

    blk = pl.BlockSpec((None, None, S, 64), lambda b, h: (b, h, 0, 0))
    lblk = pl.BlockSpec((None, None, S, 1), lambda b, h: (b, h, 0, 0))
    osd = jax.ShapeDtypeStruct(outs[0].shape, F32)
    lsd = jax.ShapeDtypeStruct(lses[0].shape, F32)
    if not bwd:
        return _pcall(body, name=name, grid=(B, 4), in_specs=[blk] * 3 + [lblk] * 3, out_specs=blk, out_shape=osd,
                      compiler_params=_cp(("parallel", "parallel")))(*outs, *lses)
    return _pcall(body, name=name, grid=(B, 4), in_specs=[blk] * 3 + [lblk] * 3 + [blk],
                  out_specs=[blk] * 3 + [lblk] * 3, out_shape=[osd] * 3 + [lsd] * 3,
                  compiler_params=_cp(("parallel", "parallel")))(*outs, *lses, dy)


def _to_dil(t, dil):
    B = t.shape[0] // S
    return t.reshape(B, S // dil, dil, 4, 64).transpose(0, 3, 2, 1, 4)


def _from_dil(t):
    B, _, dil, L, w = t.shape
    return t.transpose(0, 1, 3, 2, 4).reshape(B, 4, S, w)


def _hm_to_dil(t, dil):
    B, _, _, w = t.shape
    return t.reshape(B, 4, S // dil, dil, w).transpose(0, 1, 3, 2, 4)


def _dil_to_rows(t):
    B, _, dil, L, _ = t.shape
    return t.transpose(0, 3, 2, 1, 4).reshape(B * S, 256)


def _gmerge(g0, g1, g2, gb, ya, yb, yc):
    return (jax.nn.sigmoid(g0 + gb[:, 0:D]) * ya + jax.nn.sigmoid(g1 + gb[:, D:2 * D]) * yb
            + jax.nn.sigmoid(g2 + gb[:, 2 * D:3 * D]) * yc)


def _gated_merge(gates, gb, ya, yb, yc, name):
    def fn(tv, fv):
        return [_gmerge(tv[0], tv[1], tv[2], fv[0], tv[3], tv[4], tv[5])], []
    return _rowwise(fn, [(gates, D, 0), (gates, D, 1), (gates, D, 2), ya, yb, yc], [gb], [(D, _ACT)], [], name=name)[0]


def _gated_merge_bwd(gates, gb, ya, yb, yc, dm, name):
    def fn(tv, fv):
        _, vjp = jax.vjp(_gmerge, tv[0], tv[1], tv[2], fv[0], tv[3], tv[4], tv[5])
        d0, d1, d2, dgb, da, db, dc = vjp(tv[6])
        return [[d0, d1, d2], da, db, dc], [dgb]
    return _rowwise(fn, [(gates, D, 0), (gates, D, 1), (gates, D, 2), ya, yb, yc, dm], [gb],
                    [(3 * D, _ACT), (D, _ACT), (D, _ACT), (D, _ACT)], [(1, 3 * D)], name=name)


def _pool_affine(t1, pb, ps, dout, name):
    if dout is None:
        def fn(tv, fv):
            return [(tv[0] + fv[0]) * fv[1]], []
        return _rowwise(fn, [t1], [pb, ps], [(POOLW, _ACT)], [], name=name)[0]

    def fnb(tv, fv):
        t2, vjp = jax.vjp(lambda t, b, s: (t + b) * s, tv[0], fv[0], fv[1])
        dt, db, dsc = vjp(tv[1])
        return [dt, t2], [db, dsc]
    return _rowwise(fnb, [t1, dout], [pb, ps], [(POOLW, _ACT), (POOLW, _ACT)], [(1, POOLW), (1, POOLW)], name=name)


def _dt_softplus(dt_raw, dt_bias, ddt, name):
    f = lambda r, b: _softplus(r + b)
    if ddt is None:
        def fn(tv, fv):
            return [f(tv[0], fv[0])], []
        return _rowwise(fn, [dt_raw], [dt_bias], [(16, F32)], [], name=name, tm=1024)[0]

    def fnb(tv, fv):
        _, vjp = jax.vjp(f, tv[0], fv[0])
        dr, db = vjp(tv[1])
        return [dr], [db]
    return _rowwise(fnb, [dt_raw, ddt], [dt_bias], [(16, F32)], [(1, 16)], name=name, tm=1024)


def _adamw(w, g, m, v, name):
    R, C = w.shape
    tm = _pick(R, (256, 128, 64, 32, 16, 8))
    c1 = 1.0 / (1.0 - ADAM_B1 ** ADAM_STEP)
    c2 = 1.0 / (1.0 - ADAM_B2 ** ADAM_STEP)

    def fn(tv, fv):
        wv, gv, mv, vv = tv
        mn = ADAM_B1 * mv + (1.0 - ADAM_B1) * gv
        vn = ADAM_B2 * vv + (1.0 - ADAM_B2) * (gv * gv)
        delta = -ADAM_LR * ((mn * c1) / (jnp.sqrt(vn * c2) + ADAM_EPS) + ADAM_WD * wv)
        return [delta, mn, vn], []
    return _rowwise(fn, [w, g, m, v], [], [(C, F32)] * 3, [], name=name, tm=tm)


def _ffn_fwd(x, xm, w13, w2, g, b, tag):
    h = _mm(xm, w13, name=f"{tag}_h")
    s = _swiglu_act(h, name=f"{tag}_act")
    y = _mm(s, w2, name=f"{tag}_y")
    r, out, outm = _res_ln_fwd(x, y, g, b, 0.5, name=f"{tag}_ln")
    return out, outm, dict(x=xm, h=h, r=r)


def _ffn_bwd(dout, sv, w13, w2, g, b, tag):
    dskip, dy, dg, db = _ln_bwd(sv['r'], g, b, dout, 0.5, name=f"{tag}_lnb")
    ds = _mm(dy, w2, tb=True, name=f"{tag}_ds")
    dh, s = _swiglu_act_bwd(sv['h'], ds, name=f"{tag}_actb")
    dw2 = _mm(s, dy, ta=True, name=f"{tag}_dw2")
    dw13 = _mm(sv['x'], dh, ta=True, name=f"{tag}_dw13")
    dx = _mm(dh, w13, tb=True, add=dskip, name=f"{tag}_dx")
    return dx, dict(w13=dw13, w2=dw2, g=dg, b=db)


def _mixer_fwd(x1, x1m, W, bias_all, tag):
    T = x1.shape[0]
    hcat = _mm(x1m, W['w_in_r'], name=f"{tag}_hcat")
    u = hcat[:, O_U:O_Z]
    z = hcat[:, O_Z:O_XBC]
    xbc = hcat[:, O_XBC:O_Q]
    gates = hcat[:, O_G:O_DT]
    dt_raw = hcat[:, O_DT:O_DT + 16]
    pooled = _pool_mean(u, False, name=f"{tag}_pool")
    t1 = _mm(pooled, W['pool_wbd'], name=f"{tag}_pt1")
    t2 = _pool_affine(t1, W['pool_b'], W['pool_scale'], None, name=f"{tag}_paff")
    ya = _mm(t2, W['p_pool'], name=f"{tag}_ya")
    act = _conv_silu(xbc, W['conv_w'], W['conv_b'], name=f"{tag}_conv")
    xs, bm, cm = act[:, :1024], act[:, 1024:1536], act[:, 1536:2048]
    dt = _dt_softplus(dt_raw, W['dt_bias'], None, name=f"{tag}_dt")
    dtg = dt.reshape(T, 4, 4).transpose(1, 0, 2)
    yscan, hs = _ssd_fwd(xs, dtg, bm, cm, W['a_neg'], W['d_skip'], name=f"{tag}_ssd")
    ybn = _ssd_gate_norm(yscan, z, W['ssd_norm'], name=f"{tag}_gn")
    yb = _mm(ybn, W['p_ssd'], name=f"{tag}_yb")
    outs, lses, qkv = [], [], []
    for gi, dil in enumerate(ATTN_DILS):
        qd = _to_dil(hcat[:, O_Q + 256 * gi:O_Q + 256 * gi + 256], dil)
        kd = _to_dil(hcat[:, O_K + 256 * gi:O_K + 256 * gi + 256], dil)
        vd = _to_dil(hcat[:, O_V + 256 * gi:O_V + 256 * gi + 256], dil)
        o, l = _attn_fwd(qd, kd, vd, bias_all[4 * gi:4 * gi + 4], name=f"{tag}_attn{gi}")
        qkv.append((qd, kd, vd))
        outs.append(_from_dil(o))
        lses.append(_from_dil(l))
    ym = _attn_merge(outs, lses, None, name=f"{tag}_amerge")
    ycp = ym.transpose(0, 2, 1, 3).reshape(T, 256).astype(_ACT)
    yc = _mm(ycp, W['p_attn'], name=f"{tag}_yc")
    merged = _gated_merge(gates, W['gate_b'], ya, yb, yc, name=f"{tag}_gm")
    mix = _mm(merged, W['w_out'], name=f"{tag}_mix")
    r, out, outm = _res_ln_fwd(x1, mix, W['ln2_g'], W['ln2_b'], 1.0, name=f"{tag}_ln")
    sv = dict(x1=x1m, z=z, xbc=xbc, gates=gates, dt_raw=dt_raw, pooled=pooled, t1=t1, xs=xs, bm=bm, cm=cm, dtg=dtg,
              hs=hs, yscan=yscan, ybn=ybn, qkv=qkv, outs=outs, lses=lses, ycp=ycp, ya=ya, yb=yb, yc=yc,
              merged=merged, r=r)
    return out, outm, sv


def _mixer_bwd(dout, sv, W, bias_all, tag):
    T = dout.shape[0]
    B = T // S
    gr = {}
    dx1a, dr, gr['ln2_g'], gr['ln2_b'] = _ln_bwd(sv['r'], W['ln2_g'], W['ln2_b'], dout, 1.0, name=f"{tag}_lnb")
    dmerged = _mm(dr, W['w_out'], tb=True, name=f"{tag}_dmerged")
    gr['w_out'] = _mm(sv['merged'], dr, ta=True, name=f"{tag}_dwout")
    dgates, dya, dyb, dyc, gr['gate_b'] = _gated_merge_bwd(sv['gates'], W['gate_b'], sv['ya'], sv['yb'], sv['yc'],
                                                           dmerged, name=f"{tag}_gmb")
    dycp = _mm(dyc, W['p_attn'], tb=True, name=f"{tag}_dycp")
    gr['p_attn'] = _mm(sv['ycp'], dyc, ta=True, name=f"{tag}_dpattn")
    dym = dycp.reshape(B, S, 4, 64).transpose(0, 2, 1, 3)
    dml = _attn_merge(sv['outs'], sv['lses'], dym, name=f"{tag}_amergeb")
    dq, dk, dv, dbias = [], [], [], []
    for gi, dil in enumerate(ATTN_DILS):
        qd, kd, vd = sv['qkv'][gi]
        a, b, c, d = _attn_bwd(qd, kd, vd, bias_all[4 * gi:4 * gi + 4], _hm_to_dil(dml[gi], dil),
                               _hm_to_dil(dml[3 + gi], dil), name=f"{tag}_attnb{gi}")
        dq.append(_dil_to_rows(a))
        dk.append(_dil_to_rows(b))
        dv.append(_dil_to_rows(c))
        dbias.append(d)
    dbias = jnp.concatenate(dbias, axis=0)
    dybn = _mm(dyb, W['p_ssd'], tb=True, name=f"{tag}_dybn")
    gr['p_ssd'] = _mm(sv['ybn'], dyb, ta=True, name=f"{tag}_dpssd")
    dyscan, dz, gr['ssd_norm'] = _ssd_gate_norm_bwd(sv['yscan'], sv['z'], W['ssd_norm'], dybn, name=f"{tag}_gnb")
    dxs, ddtg, dbm, dcm, dak = _ssd_bwd(sv['xs'], sv['dtg'], sv['bm'], sv['cm'], W['a_neg'], W['d_skip'], sv['hs'],
                                        dyscan, name=f"{tag}_ssdb")
    gr['a_neg'], gr['d_skip'] = dak[:, 0, 0:4], dak[:, 1, 0:4]
    ddt = ddtg.transpose(1, 0, 2).reshape(T, 16)
    ddt_raw, gr['dt_bias'] = _dt_softplus(sv['dt_raw'], W['dt_bias'], ddt, name=f"{tag}_dtb")
    dact = jnp.concatenate([dxs, dbm, dcm], axis=1)
    dxbc, gr['conv_w'], gr['conv_b'] = _conv_silu_bwd(sv['xbc'], W['conv_w'], W['conv_b'], dact, name=f"{tag}_convb")
    dt2 = _mm(dya, W['p_pool'], tb=True, name=f"{tag}_dt2")
    dt1, t2, gr['pool_b'], gr['pool_scale'] = _pool_affine(sv['t1'], W['pool_b'], W['pool_scale'], dt2, name=f"{tag}_paffb")
    gr['p_pool'] = _mm(t2, dya, ta=True, name=f"{tag}_dppool")
    dpooled = _mm(dt1, W['pool_wbd'], tb=True, name=f"{tag}_dpooled")
    gr['pool_wbd'] = _mm(sv['pooled'], dt1, ta=True, name=f"{tag}_dpoolw")
    du = _pool_mean(dpooled, True, name=f"{tag}_poolb")
    dhcat = jnp.concatenate([t.astype(_ACT) for t in [du, dz, dxbc] + dq + dk + dv + [dgates, ddt_raw]]
                            + [jnp.zeros((T, HC - O_DT - 16), _ACT)], axis=1)
    dx1 = _mm(dhcat, W['w_in_r'], tb=True, add=dx1a, name=f"{tag}_dx1")
    gr['w_in_r'] = _mm(sv['x1'], dhcat, ta=True, name=f"{tag}_dwin")
    return dx1, gr, dbias


def _prep_layer_weights(i, inp, G):
    W = {}
    for n in BIG:
        g = G[n]
        if n == 'w_in':
            W['w_in_r'] = jnp.concatenate(_nat_pieces(g, 0, 3840) + _nat_pieces(g, 3856, 9232) + _nat_pieces(g, 3840, 3856)
                                          + [jnp.zeros((D, HC - 9232), g.dtype)], axis=1)
        elif n in COL_SHARDED:
            W[n] = jnp.concatenate([g[j] for j in range(4)], axis=1)
        else:
            W[n] = g.reshape(4 * g.shape[1], g.shape[2])
    pw = inp['pool_w'][i].astype(_MXU)
    wbd = jnp.zeros((POOLW, POOLW), _MXU)
    for g in range(4):
        wbd = lax.dynamic_update_slice(wbd, pw[g], (g * POOL_GDIM, g * POOL_GDIM))
    W['pool_wbd'] = wbd
    W['pool_b'] = inp['pool_b'][i].reshape(1, POOLW)
    W['pool_scale'] = inp['pool_scale'][i].reshape(1, POOLW)
    W['conv_w'] = jnp.concatenate([G['conv_w'][j] for j in range(4)], axis=1)
    W['conv_b'] = inp['conv_b'][i].reshape(1, 2048)
    W['dt_bias'] = inp['dt_bias'][i].reshape(1, 16)
    W['a_neg'] = (-jnp.exp(inp['a_log'][i])).reshape(4, 1, 4)
    W['d_skip'] = inp['d_skip'][i].reshape(4, 1, 4)
    W['ssd_norm'] = inp['ssd_norm'][i].reshape(1, D)
    W['gate_b'] = jnp.concatenate([G['gate_b'][j][b:b + 1] for b in range(3) for j in range(4)], axis=1)
    for n in ('ln1_g', 'ln1_b', 'ln2_g', 'ln2_b', 'ln3_g', 'ln3_b'):
        W[n] = inp[n][i].reshape(1, D)
    return W


def _gather_layer(i, inp):
    core = lax.axis_index("c")
    me = 2 * lax.axis_index("x") + lax.axis_index("y")
    halves = []
    for n in BIG:
        s = inp[n][i]
        halves.append(lax.dynamic_slice_in_dim(s, core * (s.shape[0] // 2), s.shape[0] // 2, axis=0).astype(BF16))
    small = [inp['gate_b'][i], inp['conv_w'][i]]
    outs = _exchange(halves + small, "chips", "gather", name="gather_weights")
    outs = [lax.dynamic_update_slice(o, a[None], (me, 0, 0)) for o, a in zip(outs, halves + small)]
    mine = outs[:len(BIG)]
    theirs = _exchange(mine, "cores", "gather", name="gather_share")
    G = {}
    for n, a, b in zip(BIG, mine, theirs):
        G[n] = jnp.concatenate([jnp.where(core == 0, a, b), jnp.where(core == 0, b, a)], axis=1)
    G['gate_b'], G['conv_w'] = outs[len(BIG)], outs[len(BIG) + 1]
    return G


W_IN_SHARD = 2308


def _nat_pieces(g, lo, hi):
    out = []
    for j in range(4):
        s, e = max(lo, W_IN_SHARD * j), min(hi, W_IN_SHARD * (j + 1))
        if s < e:
            out.append(g[j][:, s - W_IN_SHARD * j:e - W_IN_SHARD * j])
    return out


def _reord_ranges(lo, hi):
    out = []
    for a, b, off in ((0, 3840, 0), (3840, 3856, O_DT - 3840), (3856, 9232, -16)):
        s, e = max(lo, a), min(hi, b)
        if s < e:
            out.append((s + off, e + off))
    return out


def _halves_of(n, g):
    if n == 'w_in':
        shards = [jnp.concatenate([g[:, a:b] for a, b in _reord_ranges(W_IN_SHARD * j, W_IN_SHARD * (j + 1))], axis=1)
                  for j in range(4)]
    elif n in COL_SHARDED:
        c = g.shape[1] // 4
        shards = [g[:, j * c:(j + 1) * c] for j in range(4)]
    else:
        r = g.shape[0] // 4
        shards = [g[j * r:(j + 1) * r] for j in range(4)]
    r2 = shards[0].shape[0] // 2
    return jnp.stack([jnp.concatenate([s[h * r2:(h + 1) * r2] for s in shards], axis=0) for h in range(2)])


def _reduce_big(grads):
    names = list(grads)
    halves = [_halves_of(n, grads[n]) for n in names]
    core = lax.axis_index("c").reshape(1)
    chip_id = (2 * lax.axis_index("x") + lax.axis_index("y")).reshape(1)
    got = _exchange(halves, "cores", "scatter", name="rs_cores")
    chip = [_sum_own_recv(h, t, core, BF16, name="rs_sum2") for h, t in zip(halves, got)]
    chip = [t.reshape(4, t.shape[0] // 4, t.shape[1]) for t in chip]
    got = _exchange(chip, "chips", "scatter", name="rs_chips")
    red = [_sum_own_recv(h, t, chip_id, F32, name="rs_sum4") for h, t in zip(chip, got)]
    other = _exchange(red, "cores", "gather", name="rs_share")
    out = {}
    for n, mine, theirs in zip(names, red, other):
        both = jnp.where(lax.axis_index("c") == 0, jnp.concatenate([mine, theirs]), jnp.concatenate([theirs, mine]))
        out[n] = both
    return out


def _allreduce_small(vec):
    for group in ("cores", "x", "y"):
        recv = _exchange([vec], group, "gather", name=f"ar_{group}")[0]
        vec = _rowwise(lambda tv, fv: ([tv[0] + tv[1]], []), [vec, recv], [], [(128, F32)], [], name=f"ar_add_{group}")[0]
    return vec


def _pack(arrs):
    flat = jnp.concatenate([a.reshape(-1) for a in arrs])
    n = flat.shape[0]
    pad = (-n) % (256 * 128)
    flat = jnp.concatenate([flat, jnp.zeros((pad,), F32)])
    return flat.reshape(-1, 128)


def _unpack(p, shapes):
    flat = p.reshape(-1)
    out, off = [], 0
    for s in shapes:
        sz = int(np.prod(s))
        out.append(flat[off:off + sz].reshape(s))
        off += sz
    return out


def _forward_backward(inp, gather, bias_all):
    x = xm = inp['x'].reshape(-1, D)
    tgt = inp['loss_target'].reshape(-1, D)
    saved, Ws = [], []
    for i in range(NL):
        W = _prep_layer_weights(i, inp, gather(i))
        x1, x1m, s1 = _ffn_fwd(x, xm, W['ffn1_w13'], W['ffn1_w2'], W['ln1_g'], W['ln1_b'], "f1")
        x2, x2m, s2 = _mixer_fwd(x1, x1m, W, bias_all, "mx")
        x, xm, s3 = _ffn_fwd(x2, x2m, W['ffn2_w13'], W['ffn2_w2'], W['ln3_g'], W['ln3_b'], "f2")
        saved.append((s1, s2, s3))
        Ws.append(W)
    dy, lpart = _loss_fwd_bwd(x, tgt, name="loss")
    grads, dbiases = [None] * NL, [None] * NL
    for i in reversed(range(NL)):
        W = Ws[i]
        s1, s2, s3 = saved[i]
        g = {}
        dx2, f = _ffn_bwd(dy, s3, W['ffn2_w13'], W['ffn2_w2'], W['ln3_g'], W['ln3_b'], "f2")
        g['ffn2_w13'], g['ffn2_w2'], g['ln3_g'], g['ln3_b'] = f['w13'], f['w2'], f['g'], f['b']
        dx1, gm, dbiases[i] = _mixer_bwd(dx2, s2, W, bias_all, "mx")
        g.update(gm)
        dy, f = _ffn_bwd(dx1, s1, W['ffn1_w13'], W['ffn1_w2'], W['ln1_g'], W['ln1_b'], "f1")
        g['ffn1_w13'], g['ffn1_w2'], g['ln1_g'], g['ln1_b'] = f['w13'], f['w2'], f['g'], f['b']
        grads[i] = g
    return lpart, dy, grads, dbiases


def _finish_layer_grads(i, g, inp):
    out = {n: g[n] for n in BIG if n != 'w_in'}
    out['w_in'] = g['w_in_r']
    out['pool_w'] = jnp.stack([g['pool_wbd'][k * POOL_GDIM:(k + 1) * POOL_GDIM, k * POOL_GDIM:(k + 1) * POOL_GDIM] for k in range(4)])
    out['pool_b'] = g['pool_b'].reshape(4, POOL_GDIM)
    out['pool_scale'] = g['pool_scale'].reshape(POOLW)
    out['conv_w'] = g['conv_w']
    out['conv_b'] = g['conv_b'].reshape(2048)
    out['dt_bias'] = g['dt_bias'].reshape(16)
    out['a_log'] = (g['a_neg'].reshape(16)) * (-jnp.exp(inp['a_log'][i]))
    out['d_skip'] = g['d_skip'].reshape(16)
    out['ssd_norm'] = g['ssd_norm'].reshape(D)
    out['gate_b'] = g['gate_b'].reshape(3, D)
    for n in ('ln1_g', 'ln1_b', 'ln2_g', 'ln2_b', 'ln3_g', 'ln3_b'):
        out[n] = g[n].reshape(D)
    return out


def kernel(x, ffn1_w13, ffn1_w2, ln1_g, ln1_b, w_in, gate_b, pool_w, pool_b, pool_scale, conv_w, conv_b,
           dt_bias, a_log, d_skip, ssd_norm, rel_bias, p_pool, p_ssd, p_attn, w_out, ln2_g, ln2_b, ffn2_w13,
           ffn2_w2, ln3_g, ln3_b, loss_target, m_ffn1_w13, m_ffn1_w2, m_ln1_g, m_ln1_b, m_w_in, m_gate_b,
           m_pool_w, m_pool_b, m_pool_scale, m_conv_w, m_conv_b, m_dt_bias, m_a_log, m_d_skip, m_ssd_norm,
           m_rel_bias, m_p_pool, m_p_ssd, m_p_attn, m_w_out, m_ln2_g, m_ln2_b, m_ffn2_w13, m_ffn2_w2, m_ln3_g,
           m_ln3_b, v_ffn1_w13, v_ffn1_w2, v_ln1_g, v_ln1_b, v_w_in, v_gate_b, v_pool_w, v_pool_b,
           v_pool_scale, v_conv_w, v_conv_b, v_dt_bias, v_a_log, v_d_skip, v_ssd_norm, v_rel_bias, v_p_pool,
           v_p_ssd, v_p_attn, v_w_out, v_ln2_g, v_ln2_b, v_ffn2_w13, v_ffn2_w2, v_ln3_g, v_ln3_b):
    inp = dict(locals())
    maps = jnp.asarray(_bucket_maps())
    bias_all = _bias_build(rel_bias, maps, name="bias_build")
    lpart, gx, grads, dbiases = _forward_backward(inp, lambda i: _gather_layer(i, inp), bias_all)
    loss = lax.psum(lpart[0, 0], ("x", "y", "c"))
    fins = [_finish_layer_grads(i, grads[i], inp) for i in range(NL)]

    red = [_reduce_big({n: fins[i][n] for n in BIG}) for i in range(NL)]
    gout = {n: jnp.stack([red[i][n] for i in range(NL)]) for n in BIG}

    small_l = [n for n in SMALL if n != 'rel_bias']
    drel = _bias_reduce(jnp.stack(dbiases), maps, name="bias_reduce")[:, 0, :32].T
    small_arrs = [jnp.stack([fins[i][n] for i in range(NL)]) for n in small_l] + [drel]
    packed = _allreduce_small(_pack(small_arrs))
    gsmall = dict(zip(small_l + ['rel_bias'], _unpack(packed, [a.shape for a in small_arrs])))
    shard = 2 * lax.axis_index("x") + lax.axis_index("y")
    gsmall['gate_b'] = lax.dynamic_slice_in_dim(gsmall['gate_b'], shard * 256, 256, axis=2)
    gsmall['conv_w'] = lax.dynamic_slice_in_dim(gsmall['conv_w'], shard * 512, 512, axis=2)
    gout.update(gsmall)

    delta, new_m, new_v = {}, {}, {}
    for n in BIG:
        shp = inp[n].shape
        two_d = lambda a: a.reshape(shp[0] * shp[1], shp[2])
        d, m, v = _adamw(two_d(inp[n]), two_d(gout[n]), two_d(inp['m_' + n]), two_d(inp['v_' + n]), name="adamw_big")
        delta[n], new_m[n], new_v[n] = d.reshape(shp), m.reshape(shp), v.reshape(shp)
    shapes = [inp[n].shape for n in SMALL]
    d, m, v = _adamw(_pack([inp[n] for n in SMALL]), _pack([gout[n] for n in SMALL]),
                     _pack([inp['m_' + n] for n in SMALL]), _pack([inp['v_' + n] for n in SMALL]), name="adamw_small")
    for n, dd, mm, vv in zip(SMALL, _unpack(d, shapes), _unpack(m, shapes), _unpack(v, shapes)):
        delta[n], new_m[n], new_v[n] = dd, mm, vv

    return (loss, gx.reshape(x.shape), *[gout[n] for n in WEIGHTS], *[delta[n] for n in WEIGHTS],
            *[new_m[n] for n in WEIGHTS], *[new_v[n] for n in WEIGHTS])
```

```python
import functools

import numpy as np
import jax
import jax.numpy as jnp
from jax import lax
from jax.experimental import pallas as pl
from jax.experimental.pallas import tpu as pltpu

F32 = jnp.float32
BF16 = jnp.bfloat16
_MXU = jnp.bfloat16
_ACT = jnp.bfloat16
_VMEM_LIMIT = 56 * 1024 * 1024

S = 2048
D = 1024
NL = 4
DFF = 2816
LN_EPS = 1e-5
SSD_EPS = 1e-5
ALPHA = (2.0 * NL) ** 0.25
POOLW = 768
POOL_WINDOWS = (2, 4, 8, 16)
POOL_GDIM = 192
CH = 128
ATTN_DILS = (1, 4, 16)
HC = 9728
O_U, O_Z, O_XBC, O_Q, O_K, O_V, O_G, O_DT = 0, 768, 1792, 3840, 4608, 5376, 6144, 9216

ADAM_LR, ADAM_B1, ADAM_B2, ADAM_EPS, ADAM_WD, ADAM_STEP = 0.001, 0.9, 0.999, 1e-08, 0.01, 10

WEIGHTS = ['ffn1_w13', 'ffn1_w2', 'ln1_g', 'ln1_b', 'w_in', 'gate_b', 'pool_w', 'pool_b', 'pool_scale', 'conv_w',
           'conv_b', 'dt_bias', 'a_log', 'd_skip', 'ssd_norm', 'rel_bias', 'p_pool', 'p_ssd', 'p_attn', 'w_out',
           'ln2_g', 'ln2_b', 'ffn2_w13', 'ffn2_w2', 'ln3_g', 'ln3_b']
BIG = ['ffn1_w13', 'ffn1_w2', 'w_in', 'p_pool', 'p_ssd', 'p_attn', 'w_out', 'ffn2_w13', 'ffn2_w2']
COL_SHARDED = {'ffn1_w13', 'ffn2_w13', 'w_in', 'p_pool', 'p_attn'}
SMALL = [n for n in WEIGHTS if n not in BIG]


def _pcall(body, **kw):
    return pl.pallas_call(body, **kw)


def _cp(sem=None):
    return pltpu.CompilerParams(dimension_semantics=sem, vmem_limit_bytes=_VMEM_LIMIT)


def _pick(n, cands):
    for c in cands:
        if n % c == 0:
            return c
    raise ValueError(f"no tile for {n}")


def _mm(a, b, *, ta=False, tb=False, add=None, out_dtype=F32, dep=None, name):
    if ta:
        K, M = a.shape
    else:
        M, K = a.shape
    if tb:
        N, K2 = b.shape
    else:
        K2, N = b.shape
    assert K == K2, (a.shape, b.shape, ta, tb)
    sa, sb, so = a.dtype.itemsize, b.dtype.itemsize, jnp.dtype(out_dtype).itemsize
    tm, tn, tk = _mm_tiles(M, N, K, sa, sb, so + (4 if add is not None else 0))
    nk = K // tk
    a_bytes, b_bytes = M * K * sa, K * N * sb
    j_outer = nk == 1 and (b_bytes + a_bytes * (N // tn) < a_bytes + b_bytes * (M // tm))
    ij = (lambda p, q: (q, p)) if j_outer else (lambda p, q: (p, q))

    def im(f):
        return lambda p, q, k: f(*ij(p, q), k)

    a_spec = pl.BlockSpec((tk, tm), im(lambda i, j, k: (k, i))) if ta else pl.BlockSpec((tm, tk), im(lambda i, j, k: (i, k)))
    b_spec = pl.BlockSpec((tn, tk), im(lambda i, j, k: (j, k))) if tb else pl.BlockSpec((tk, tn), im(lambda i, j, k: (k, j)))
    o_spec = pl.BlockSpec((tm, tn), im(lambda i, j, k: (i, j)))
    dims = (((0 if ta else 1,), (1 if tb else 0,)), ((), ()))
    has_add = add is not None

    n_in = 2 + int(has_add) + int(dep is not None)

    def body(*refs):
        a_ref, b_ref = refs[0], refs[1]
        add_ref = refs[2] if has_add else None
        o_ref = refs[n_in]
        part = lax.dot_general(a_ref[...].astype(_MXU), b_ref[...].astype(_MXU), dims, preferred_element_type=F32)

        def finish(r):
            if has_add:
                r = r + add_ref[...]
            o_ref[...] = r.astype(out_dtype)

        if nk == 1:
            finish(part)
        else:
            acc = refs[-1]
            k = pl.program_id(2)

            @pl.when(k == 0)
            def _():
                acc[...] = part

            @pl.when(k > 0)
            def _():
                acc[...] += part

            @pl.when(k == nk - 1)
            def _():
                finish(acc[...])

    in_specs = [a_spec, b_spec]
    args = [a, b]
    if has_add:
        in_specs.append(o_spec)
        args.append(add)
    if dep is not None:
        in_specs.append(pl.BlockSpec(memory_space=pl.ANY))
        args.append(dep)
    gm, gn = M // tm, N // tn
    return _pcall(
        body, name=name, grid=((gn, gm, nk) if j_outer else (gm, gn, nk)), in_specs=in_specs, out_specs=o_spec,
        out_shape=jax.ShapeDtypeStruct((M, N), out_dtype),
        scratch_shapes=([pltpu.VMEM((tm, tn), F32)] if nk > 1 else []),
        compiler_params=_cp(("parallel", "parallel", "arbitrary")),
    )(*args)


_MM_VMEM_BUDGET = 40 * 1024 * 1024


def _divisors128(n, cap):
    return [d for d in range(128, min(n, cap) + 1, 128) if n % d == 0][::-1]


def _mm_tiles(M, N, K, sa, sb, so):
    best = None
    for tm in _divisors128(M, 1024):
        for tn in _divisors128(N, 2560):
            for tk in ([K] if K <= 4096 else []) + _divisors128(K, 2048):
                nk = K // tk
                need = 2 * (tm * tk * sa + tk * tn * sb + tm * tn * so) + (tm * tn * 4 if nk > 1 else 0)
                need += tm * tk * 2 + tk * tn * 2 + tm * tn * 4
                if need > _MM_VMEM_BUDGET:
                    continue
                score = (tm * tn, tk)
                if best is None or score > best[0]:
                    best = (score, (tm, tn, tk))
                break
    assert best is not None, (M, N, K)
    return best[1]


def _store(ref, val):
    if isinstance(val, (list, tuple)):
        off = 0
        for p in val:
            w = p.shape[1]
            ref[:, off:off + w] = p.astype(ref.dtype)
            off += w
    else:
        ref[...] = val.astype(ref.dtype)


def _acc_store(ref, val, first):
    pieces = val if isinstance(val, (list, tuple)) else [val]
    off = 0
    for p in pieces:
        w = p.shape[1]

        @pl.when(first)
        def _(p=p, off=off, w=w):
            ref[:, off:off + w] = p

        @pl.when(jnp.logical_not(first))
        def _(p=p, off=off, w=w):
            ref[:, off:off + w] += p

        off += w


def _rowwise(fn, tiled, full, out_tiled, out_acc, *, name, tm=256):
    arrs, specs = [], []
    for t in tiled:
        arr, w, cb = t if isinstance(t, tuple) else (t, t.shape[1], 0)
        arrs.append(arr)
        specs.append(pl.BlockSpec((tm, w), functools.partial(lambda i, cb: (i, cb), cb=cb)))
    R = arrs[0].shape[0]
    assert R % tm == 0
    for f in full:
        arrs.append(f)
        specs.append(pl.BlockSpec(f.shape, functools.partial(lambda i, nd: (0,) * nd, nd=f.ndim)))
    nt, nf, no = len(tiled), len(full), len(out_tiled)

    def body(*refs):
        tv = [r[...] for r in refs[:nt]]
        fv = [r[...] for r in refs[nt:nt + nf]]
        ot, oa = fn(tv, fv)
        for r, v in zip(refs[nt + nf:nt + nf + no], ot):
            _store(r, v)
        first = pl.program_id(0) == 0
        for r, v in zip(refs[nt + nf + no:], oa):
            _acc_store(r, v, first)

    out_shape = [jax.ShapeDtypeStruct((R, c), dt) for c, dt in out_tiled]
    out_specs = [pl.BlockSpec((tm, c), lambda i: (i, 0)) for c, _ in out_tiled]
    for shp in out_acc:
        out_shape.append(jax.ShapeDtypeStruct(shp, F32))
        out_specs.append(pl.BlockSpec(shp, lambda i: (0, 0)))
    return _pcall(body, name=name, grid=(R // tm,), in_specs=specs, out_specs=out_specs, out_shape=out_shape,
                  compiler_params=_cp(("arbitrary",)))(*arrs)


def _group(group):
    x, y, c = lax.axis_index("x"), lax.axis_index("y"), lax.axis_index("c")
    if group == "chips":
        return 2 * x + y, [((x, 1 - y, c), 2 * x + 1 - y), ((1 - x, y, c), 2 * (1 - x) + y),
                           ((1 - x, 1 - y, c), 2 * (1 - x) + 1 - y)]
    if group == "cores":
        return c, [((x, y, 1 - c), 1 - c)]
    if group == "x":
        return x, [((1 - x, y, c), 1 - x)]
    return y, [((x, 1 - y, c), 1 - y)]


def _exchange(arrs, group, mode, name):
    chips = group == "chips"
    k = len(arrs)
    npeer = 3 if chips else 1

    def body(*refs):
        ins, outs = refs[:k], refs[k:2 * k]
        send_sems, recv_sems = refs[2 * k:]
        me, peers = _group(group)
        remote = []
        for i in range(k):
            for p, (dev, slot) in enumerate(peers):
                src = ins[i].at[slot] if mode == "scatter" else ins[i]
                if not chips:
                    dst = outs[i]
                else:
                    dst = outs[i].at[p] if mode == "scatter" else outs[i].at[me]
                cp = pltpu.make_async_remote_copy(src_ref=src, dst_ref=dst, send_sem=send_sems.at[i, p],
                                                  recv_sem=recv_sems.at[i, p], device_id=dev,
                                                  device_id_type=pl.DeviceIdType.MESH)
                cp.start()
                remote.append(cp)
        for cp in remote:
            cp.wait_recv()
        for cp in remote:
            cp.wait_send()

    def oshape(a):
        piece = a.shape[1:] if mode == "scatter" else a.shape
        if chips:
            piece = ((3,) if mode == "scatter" else (4,)) + piece
        return jax.ShapeDtypeStruct(piece, a.dtype)

    any_spec = pl.BlockSpec(memory_space=pl.ANY)
    return _pcall(body, name=name, in_specs=[any_spec] * k, out_specs=[any_spec] * k, out_shape=[oshape(a) for a in arrs],
                  scratch_shapes=[pltpu.SemaphoreType.DMA((k, npeer)), pltpu.SemaphoreType.DMA((k, npeer))])(*arrs)


def _chip_copies(ins, lands, send_sems, recv_sems, mode):
    me, peers = _group("chips")
    out = []
    for i in range(len(ins)):
        for p, (dev, slot) in enumerate(peers):
            src = ins[i].at[slot] if mode == "scatter" else ins[i]
            dst = lands[i].at[p] if mode == "scatter" else lands[i].at[me]
            out.append(pltpu.make_async_remote_copy(src_ref=src, dst_ref=dst, send_sem=send_sems.at[3 * i + p],
                                                    recv_sem=recv_sems.at[3 * i + p], device_id=dev,
                                                    device_id_type=pl.DeviceIdType.MESH))
    return out


def _exchange_start(arrs, mode, name):
    k = len(arrs)
    hbm = pl.BlockSpec(memory_space=pltpu.HBM)
    sem = pl.BlockSpec(memory_space=pltpu.SEMAPHORE)

    def land_shape(a):
        piece = a.shape[1:] if mode == "scatter" else a.shape
        return ((3,) if mode == "scatter" else (4,)) + piece

    def body(*refs):
        ins, lands = refs[:k], refs[k:2 * k]
        send_sems, recv_sems = refs[2 * k], refs[2 * k + 1]
        token = refs[-1]
        for cp in _chip_copies(ins, lands, send_sems, recv_sems, mode):
            cp.start()
        token[...] = jnp.zeros_like(token)

    srcs = [pltpu.with_memory_space_constraint(a, pltpu.HBM) for a in arrs]
    lands = [pltpu.with_memory_space_constraint(lax.empty(land_shape(a), a.dtype), pltpu.HBM) for a in arrs]
    out_shape = ([pltpu.SemaphoreType.DMA((3 * k,)), pltpu.SemaphoreType.DMA((3 * k,))]
                 + [pltpu.HBM(a.shape, a.dtype) for a in arrs] + [pltpu.HBM(land_shape(a), a.dtype) for a in arrs]
                 + [jax.ShapeDtypeStruct((8, 128), F32)])
    outs = _pcall(body, name=name, in_specs=[hbm] * (2 * k),
                  out_specs=[sem, sem] + [hbm] * (2 * k) + [pl.BlockSpec(memory_space=pltpu.VMEM)], out_shape=out_shape,
                  input_output_aliases={i: 2 + i for i in range(2 * k)},
                  compiler_params=pltpu.CompilerParams(has_side_effects=pltpu.SideEffectType.DATAFLOW_SIDE_EFFECTING))(
                      *srcs, *lands)
    return (outs[0], outs[1], list(outs[2:2 + k]), list(outs[2 + k:2 + 2 * k])), outs[-1]


def _exchange_wait(state, after, mode, name):
    send_sems, recv_sems, srcs, lands = state
    k = len(srcs)
    hbm = pl.BlockSpec(memory_space=pltpu.HBM)
    sem = pl.BlockSpec(memory_space=pltpu.SEMAPHORE)

    def body(*refs):
        ins, lnd = refs[:k], refs[k:2 * k]
        send_sems, recv_sems = refs[2 * k], refs[2 * k + 1]
        for cp in _chip_copies(ins, lnd, send_sems, recv_sems, mode):
            cp.wait_send()
            cp.wait_recv()

    outs = _pcall(body, name=name, in_specs=[hbm] * (2 * k) + [sem, sem, pl.BlockSpec(memory_space=pl.ANY)],
                  out_specs=[hbm] * (2 * k),
                  out_shape=[pltpu.HBM(a.shape, a.dtype) for a in srcs] + [pltpu.HBM(a.shape, a.dtype) for a in lands],
                  input_output_aliases={i: i for i in range(2 * k)},
                  compiler_params=pltpu.CompilerParams(has_side_effects=pltpu.SideEffectType.DATAFLOW_SIDE_EFFECTING))(
                      *srcs, *lands, send_sems, recv_sems, after)
    return list(outs[:k]), list(outs[k:])


def _sum_own_recv(own, recv, me, out_dtype, name):
    n, R, C = own.shape
    nr = 1 if recv.ndim == 2 else recv.shape[0]
    tr = _pick(R, (256, 128, 64, 32, 16, 8))

    def body(me_ref, own_ref, *refs):
        o_ref = refs[-1]
        acc = own_ref[...].astype(F32)
        for r in refs[:-1]:
            acc = acc + r[...].astype(F32)
        o_ref[...] = acc.astype(out_dtype)

    specs = [pl.BlockSpec((None, tr, C), lambda i, me_ref: (me_ref[0], i, 0))]
    args = [own]
    if recv.ndim == 2:
        specs.append(pl.BlockSpec((tr, C), lambda i, me_ref: (i, 0)))
        args.append(recv)
    else:
        for p in range(nr):
            specs.append(pl.BlockSpec((None, tr, C), functools.partial(lambda i, me_ref, p: (p, i, 0), p=p)))
            args.append(recv)
    gs = pltpu.PrefetchScalarGridSpec(num_scalar_prefetch=1, grid=(R // tr,), in_specs=specs,
                                      out_specs=pl.BlockSpec((tr, C), lambda i, me_ref: (i, 0)))
    return _pcall(body, name=name, grid_spec=gs, out_shape=jax.ShapeDtypeStruct((R, C), out_dtype),
                  compiler_params=_cp(("parallel",)))(me, *args)


def _silu(x):
    return x * jax.nn.sigmoid(x)


def _ln(r, g, b):
    mu = jnp.mean(r, -1, keepdims=True)
    xc = r - mu
    var = jnp.mean(xc * xc, -1, keepdims=True)
    return xc * lax.rsqrt(var + LN_EPS) * g + b


def _softplus(x):
    return jnp.maximum(x, 0.0) + jnp.log1p(jnp.exp(-jnp.abs(x)))


def _res_ln_fwd(x, y, g, b, res, name):
    def fn(tv, fv):
        r = ALPHA * tv[0] + res * tv[1]
        out = _ln(r, fv[0], fv[1])
        return [r, out, out], []
    return _rowwise(fn, [x, y], [g, b], [(D, F32), (D, F32), (D, _ACT)], [], name=name)


def _ln_bwd(r, g, b, dout, res, name):
    def fn(tv, fv):
        _, vjp = jax.vjp(_ln, tv[0], fv[0], fv[1])
        dr, dg, db = vjp(tv[1])
        return [ALPHA * dr, res * dr], [dg, db]
    return _rowwise(fn, [r, dout], [g, b], [(D, F32), (D, _ACT)], [(1, D), (1, D)], name=name)


def _swiglu_act(h, name):
    def fn(tv, fv):
        return [_silu(tv[0]) * tv[1]], []
    return _rowwise(fn, [(h, DFF, 0), (h, DFF, 1)], [], [(DFF, _ACT)], [], name=name)[0]


def _swiglu_act_bwd(h, ds, name):
    def fn(tv, fv):
        s, vjp = jax.vjp(lambda a, g: _silu(a) * g, tv[0], tv[1])
        da, dg = vjp(tv[2])
        return [[da, dg], s], []
    return _rowwise(fn, [(h, DFF, 0), (h, DFF, 1), ds], [], [(2 * DFF, _ACT), (DFF, _ACT)], [], name=name)


def _loss_fwd_bwd(y, tgt, name):
    def fn(tv, fv):
        e = tv[0] - tv[1]
        row = jnp.sum(e * e, axis=1, keepdims=True)
        tot = jnp.sum(row, axis=0, keepdims=True) * (0.5 / D)
        return [e * (1.0 / D)], [jnp.broadcast_to(tot, (1, 128))]
    return _rowwise(fn, [y, tgt], [], [(D, F32)], [(1, 128)], name=name)


def _shift_down(x, k, row):
    return jnp.where(row >= k, pltpu.roll(x, k, axis=0), 0.0)


def _shift_up(x, k, row):
    n = x.shape[0]
    return jnp.where(row < n - k, pltpu.roll(x, n - k, axis=0), 0.0)


def _pool_window_masks(j):
    lane = lax.broadcasted_iota(jnp.int32, (1, 128), 1) + j * 128
    grp = lane // POOL_GDIM
    return [grp == g for g in range(4)]


def _pool_mean(u, bwd, name):
    T = u.shape[0]
    B = T // S

    def body(u_ref, o_ref):
        j = pl.program_id(1)
        x = u_ref[...]
        row = lax.broadcasted_iota(jnp.int32, (S, 1), 0)
        masks = _pool_window_masks(j)
        inv = [1.0 / jnp.minimum(row + 1, w).astype(F32) for w in POOL_WINDOWS]
        if not bwd:
            s2 = x + _shift_down(x, 1, row)
            s4 = s2 + _shift_down(s2, 2, row)
            s8 = s4 + _shift_down(s4, 4, row)
            s16 = s8 + _shift_down(s8, 8, row)
            mean = jnp.where(masks[0], s2 * inv[0], jnp.where(masks[1], s4 * inv[1],
                             jnp.where(masks[2], s8 * inv[2], s16 * inv[3])))
            o_ref[...] = (mean - x).astype(o_ref.dtype)
        else:
            g = [jnp.where(masks[i], x * inv[i], 0.0) for i in range(4)]
            t = g[3]
            t = t + _shift_up(t, 8, row) + g[2]
            t = t + _shift_up(t, 4, row) + g[1]
            t = t + _shift_up(t, 2, row) + g[0]
            t = t + _shift_up(t, 1, row)
            o_ref[...] = (t - x).astype(o_ref.dtype)

    spec = pl.BlockSpec((S, 128), lambda b, j: (b, j))
    return _pcall(body, name=name, grid=(B, POOLW // 128), in_specs=[spec], out_specs=spec,
                  out_shape=jax.ShapeDtypeStruct((T, POOLW), _ACT), compiler_params=_cp(("parallel", "parallel")))(u)


def _conv_silu(xbc, w, b, name):
    T, C = xbc.shape
    B = T // S

    def body(x_ref, w_ref, b_ref, o_ref):
        x = x_ref[...]
        row = lax.broadcasted_iota(jnp.int32, (S, 1), 0)
        c = b_ref[...] + w_ref[3:4, :] * x
        for s in range(1, 4):
            c = c + w_ref[3 - s:4 - s, :] * _shift_down(x, s, row)
        o_ref[...] = _silu(c)

    return _pcall(body, name=name, grid=(B, C // 128),
                  in_specs=[pl.BlockSpec((S, 128), lambda b, j: (b, j)), pl.BlockSpec((4, 128), lambda b, j: (0, j)),
                            pl.BlockSpec((1, 128), lambda b, j: (0, j))],
                  out_specs=pl.BlockSpec((S, 128), lambda b, j: (b, j)),
                  out_shape=jax.ShapeDtypeStruct((T, C), F32), compiler_params=_cp(("parallel", "parallel")))(xbc, w, b)


def _conv_silu_bwd(xbc, w, b, dact, name):
    T, C = xbc.shape
    B = T // S

    def body(x_ref, w_ref, b_ref, d_ref, dx_ref, dw_ref, db_ref):
        bi = pl.program_id(1)
        x = x_ref[...]
        row = lax.broadcasted_iota(jnp.int32, (S, 1), 0)
        xs = [x] + [_shift_down(x, s, row) for s in range(1, 4)]
        c = b_ref[...]
        for s in range(4):
            c = c + w_ref[3 - s:4 - s, :] * xs[s]
        _, vjp = jax.vjp(_silu, c)
        dc = vjp(d_ref[...])[0]
        dx = w_ref[3:4, :] * dc
        for s in range(1, 4):
            dx = dx + w_ref[3 - s:4 - s, :] * _shift_up(dc, s, row)
        dx_ref[...] = dx.astype(dx_ref.dtype)
        first = bi == 0
        for s in range(4):
            _acc_rows(dw_ref, 3 - s, jnp.sum(dc * xs[s], axis=0, keepdims=True), first)
        _acc_rows(db_ref, 0, jnp.sum(dc, axis=0, keepdims=True), first)

    blk = pl.BlockSpec((S, 128), lambda j, b: (b, j))
    return _pcall(body, name=name, grid=(C // 128, B),
                  in_specs=[blk, pl.BlockSpec((4, 128), lambda j, b: (0, j)), pl.BlockSpec((1, 128), lambda j, b: (0, j)), blk],
                  out_specs=[blk, pl.BlockSpec((4, 128), lambda j, b: (0, j)), pl.BlockSpec((1, 128), lambda j, b: (0, j))],
                  out_shape=[jax.ShapeDtypeStruct((T, C), _ACT), jax.ShapeDtypeStruct((4, C), F32),
                             jax.ShapeDtypeStruct((1, C), F32)],
                  compiler_params=_cp(("parallel", "arbitrary")))(xbc, w, b, dact)


def _acc_rows(ref, r, val, first):
    @pl.when(first)
    def _():
        ref[r:r + 1, :] = val

    @pl.when(jnp.logical_not(first))
    def _():
        ref[r:r + 1, :] += val


def _tri_consts():
    i = lax.broadcasted_iota(jnp.int32, (CH, CH), 0)
    j = lax.broadcasted_iota(jnp.int32, (CH, CH), 1)
    return (i == j).astype(F32), (j <= i).astype(F32), (i <= j).astype(F32), i >= j


def _ssd_chunk(h, x, dt, Bm, Cm, a, dsk, consts):
    eye, tril, triu, lower = consts
    Bb = Bm.astype(_MXU)
    Cb = Cm.astype(_MXU)
    cb = lax.dot_general(Cb, Bb, (((1,), (1,)), ((), ())), preferred_element_type=F32)
    ys, hn = [], []
    for e in range(4):
        adt = dt[e] * a[e]
        adt_row = jnp.sum(adt * eye, axis=0, keepdims=True)
        cs_col = jnp.sum(adt_row * tril, axis=1, keepdims=True)
        cs_row = jnp.sum(adt * triu, axis=0, keepdims=True)
        cs_last = jnp.sum(adt, axis=0, keepdims=True)
        decay = jnp.exp(jnp.where(lower, cs_col - cs_row, -jnp.inf))
        xb = (x[e] * dt[e]).astype(_MXU)
        y_diag = jnp.dot((cb * decay).astype(_MXU), xb, preferred_element_type=F32)
        bdec = (Bm * jnp.exp(cs_last - cs_col)).astype(_MXU)
        st = lax.dot_general(bdec, xb, (((0,), (0,)), ((), ())), preferred_element_type=F32)
        hn.append(h[e] * jnp.exp(cs_last) + st)
        y_off = jnp.exp(cs_col) * jnp.dot(Cb, h[e].astype(_MXU), preferred_element_type=F32)
        ys.append(y_diag + y_off + dsk[e] * x[e])
    return ys, hn


def _ssd_specs(order):
    def im(f):
        return lambda p, q: f(*order(p, q))
    xs = pl.BlockSpec((S, 256), im(lambda b, g: (b, g)))
    dt = pl.BlockSpec((None, S, 4), im(lambda b, g: (g, b, 0)))
    bc = pl.BlockSpec((S, 128), im(lambda b, g: (b, g)))
    hd = pl.BlockSpec((None, 1, 4), im(lambda b, g: (g, 0, 0)))
    hs = pl.BlockSpec((None, None, S // CH, 4, 128, 64), im(lambda b, g: (b, g, 0, 0, 0, 0)))
    return xs, dt, bc, hd, hs


def _ssd_fwd(xs, dtg, bm, cm, a, dsk, name):
    T = xs.shape[0]
    B = T // S
    nc = S // CH

    def body(x_ref, dt_ref, b_ref, c_ref, a_ref, k_ref, y_ref, hs_ref, h_ref):
        consts = _tri_consts()
        h_ref[...] = jnp.zeros_like(h_ref)
        al = [a_ref[:, e:e + 1] for e in range(4)]
        kl = [k_ref[:, e:e + 1] for e in range(4)]

        def step(c, carry):
            r0 = pl.multiple_of(c * CH, CH)
            rows = pl.ds(r0, CH)
            h = [h_ref[e] for e in range(4)]
            for e in range(4):
                hs_ref[c, e] = h[e]
            x = [x_ref[rows, 64 * e:64 * e + 64] for e in range(4)]
            dt = [dt_ref[rows, e:e + 1] for e in range(4)]
            ys, hn = _ssd_chunk(h, x, dt, b_ref[rows, :], c_ref[rows, :], al, kl, consts)
            for e in range(4):
                y_ref[rows, 64 * e:64 * e + 64] = ys[e]
                h_ref[e] = hn[e]
            return carry

        lax.fori_loop(0, nc, step, 0)

    sx, sdt, sbc, shd, shs = _ssd_specs(lambda b, g: (b, g))
    return _pcall(body, name=name, grid=(B, 4), in_specs=[sx, sdt, sbc, sbc, shd, shd], out_specs=[sx, shs],
                  out_shape=[jax.ShapeDtypeStruct((T, 1024), F32), jax.ShapeDtypeStruct((B, 4, nc, 4, 128, 64), F32)],
                  scratch_shapes=[pltpu.VMEM((4, 128, 64), F32)],
                  compiler_params=_cp(("parallel", "parallel")))(xs, dtg, bm, cm, a, dsk)


def _lane_place(vals, width):
    lane = lax.broadcasted_iota(jnp.int32, (1, width), 1)
    out = jnp.zeros((1, width), F32)
    for e, v in enumerate(vals):
        out = out + jnp.where(lane == e, v, 0.0)
    return out


def _ssd_bwd(xs, dtg, bm, cm, a, dsk, hs, dy, name):
    T = xs.shape[0]
    B = T // S
    nc = S // CH

    def body(x_ref, dt_ref, b_ref, c_ref, a_ref, k_ref, hs_ref, dy_ref,
             dx_ref, ddt_ref, db_ref, dc_ref, dak_ref, dh_ref, sc_ref):
        bi = pl.program_id(1)
        consts = _tri_consts()
        dh_ref[...] = jnp.zeros_like(dh_ref)
        sc_ref[...] = jnp.zeros_like(sc_ref)
        al = [a_ref[:, e:e + 1] for e in range(4)]
        kl = [k_ref[:, e:e + 1] for e in range(4)]

        def step(i, carry):
            c = nc - 1 - i
            r0 = pl.multiple_of(c * CH, CH)
            rows = pl.ds(r0, CH)
            h = [hs_ref[c, e] for e in range(4)]
            x = [x_ref[rows, 64 * e:64 * e + 64] for e in range(4)]
            dt = [dt_ref[rows, e:e + 1] for e in range(4)]
            f = functools.partial(_ssd_chunk, consts=consts)
            _, vjp = jax.vjp(f, h, x, dt, b_ref[rows, :], c_ref[rows, :], al, kl)
            dys = [dy_ref[rows, 64 * e:64 * e + 64] for e in range(4)]
            dhn = [dh_ref[e] for e in range(4)]
            dh, dx, ddt, dB, dC, da, dk = vjp((dys, dhn))
            for e in range(4):
                dh_ref[e] = dh[e]
                dx_ref[rows, 64 * e:64 * e + 64] = dx[e]
                ddt_ref[rows, e:e + 1] = ddt[e]
            db_ref[rows, :] = dB
            dc_ref[rows, :] = dC
            sc_ref[0:1, :] += _lane_place(da, 128)
            sc_ref[1:2, :] += _lane_place(dk, 128)
            return carry

        lax.fori_loop(0, nc, step, 0)
        first = bi == 0

        @pl.when(first)
        def _():
            dak_ref[...] = sc_ref[...]

        @pl.when(jnp.logical_not(first))
        def _():
            dak_ref[...] += sc_ref[...]

    sx, sdt, sbc, shd, shs = _ssd_specs(lambda g, b: (b, g))
    return _pcall(body, name=name, grid=(4, B), in_specs=[sx, sdt, sbc, sbc, shd, shd, shs, sx],
                  out_specs=[sx, sdt, sbc, sbc, pl.BlockSpec((None, 8, 128), lambda g, b: (g, 0, 0))],
                  out_shape=[jax.ShapeDtypeStruct((T, 1024), F32), jax.ShapeDtypeStruct((4, T, 4), F32),
                             jax.ShapeDtypeStruct((T, 512), F32), jax.ShapeDtypeStruct((T, 512), F32),
                             jax.ShapeDtypeStruct((4, 8, 128), F32)],
                  scratch_shapes=[pltpu.VMEM((4, 128, 64), F32), pltpu.VMEM((8, 128), F32)],
                  compiler_params=_cp(("parallel", "arbitrary")))(xs, dtg, bm, cm, a, dsk, hs, dy)


def _gate_norm(y, z, nw):
    t = y * _silu(z)
    return t * lax.rsqrt(jnp.mean(t * t, axis=-1, keepdims=True) + SSD_EPS) * nw


def _ssd_gate_norm(y, z, nw, name):
    def fn(tv, fv):
        return [[_gate_norm(tv[g], tv[4 + g], fv[0][:, 256 * g:256 * g + 256]) for g in range(4)]], []
    tiled = [(y, 256, g) for g in range(4)] + [(z, 256, g) for g in range(4)]
    return _rowwise(fn, tiled, [nw], [(1024, _ACT)], [], name=name)[0]


def _ssd_gate_norm_bwd(y, z, nw, dout, name):
    def fn(tv, fv):
        dys, dzs, dns = [], [], []
        for g in range(4):
            _, vjp = jax.vjp(_gate_norm, tv[g], tv[4 + g], fv[0][:, 256 * g:256 * g + 256])
            a, b, c = vjp(tv[8 + g])
            dys.append(a)
            dzs.append(b)
            dns.append(c)
        return [dys, dzs], [dns]
    tiled = [(y, 256, g) for g in range(4)] + [(z, 256, g) for g in range(4)] + [(dout, 256, g) for g in range(4)]
    return _rowwise(fn, tiled, [nw], [(1024, F32), (1024, _ACT)], [(1, 1024)], name=name)


def _t5_bucket_np(dist):
    dist = np.maximum(dist, 0)
    max_exact = 16
    large = max_exact + (np.log(np.maximum(dist, 1) / max_exact) / np.log(2048 / max_exact) * (32 - max_exact)).astype(np.int32)
    large = np.minimum(large, 31)
    return np.where(dist < max_exact, dist, large).astype(np.int32)


def _bucket_maps():
    qi = np.arange(128)[:, None]
    kj = np.arange(256)[None, :]
    return np.stack([_t5_bucket_np((qi - kj + 128) * dil) for dil in ATTN_DILS]).astype(np.int32)


def _bias_build(rel_bias, maps, name):
    def body(tab_ref, map_ref, o_ref):
        hh = pl.program_id(0)
        m = map_ref[...]
        acc = jnp.zeros((128, 256), F32)
        for b in range(32):
            acc = jnp.where(m == b, tab_ref[b, hh], acc)
        o_ref[...] = acc

    return _pcall(body, name=name, grid=(12,),
                  in_specs=[pl.BlockSpec(memory_space=pltpu.SMEM), pl.BlockSpec((None, 128, 256), lambda h: (h // 4, 0, 0))],
                  out_specs=pl.BlockSpec((None, 128, 256), lambda h: (h, 0, 0)),
                  out_shape=jax.ShapeDtypeStruct((12, 128, 256), F32), compiler_params=_cp(("parallel",)))(rel_bias, maps)


def _bias_reduce(dbias, maps, name):
    nl = dbias.shape[0]

    def body(d_ref, map_ref, o_ref):
        m = map_ref[...]
        d = d_ref[0]
        for i in range(1, nl):
            d = d + d_ref[i]
        lane = lax.broadcasted_iota(jnp.int32, (1, 128), 1)
        out = jnp.zeros((1, 128), F32)
        for b in range(32):
            s = jnp.sum(jnp.sum(jnp.where(m == b, d, 0.0), axis=1, keepdims=True), axis=0, keepdims=True)
            out = out + jnp.where(lane == b, s, 0.0)
        o_ref[...] = out

    return _pcall(body, name=name, grid=(12,),
                  in_specs=[pl.BlockSpec((nl, None, 128, 256), lambda h: (0, h, 0, 0)),
                            pl.BlockSpec((None, 128, 256), lambda h: (h // 4, 0, 0))],
                  out_specs=pl.BlockSpec((None, 1, 128), lambda h: (h, 0, 0)),
                  out_shape=jax.ShapeDtypeStruct((12, 1, 128), F32), compiler_params=_cp(("parallel",)))(dbias, maps)


def _attn_block(q, kb, vb, bias, mask):
    s = lax.dot_general(q.astype(_MXU), kb.astype(_MXU), (((1,), (1,)), ((), ())), preferred_element_type=F32) * 0.125 + bias
    s = jnp.where(mask, s, -jnp.inf)
    m = lax.stop_gradient(jnp.max(s, axis=-1, keepdims=True))
    p = jnp.exp(s - m)
    den = jnp.sum(p, axis=-1, keepdims=True)
    out = jnp.dot((p / den).astype(_MXU), vb.astype(_MXU), preferred_element_type=F32)
    return out, m + jnp.log(den)


def _band_mask():
    qi = lax.broadcasted_iota(jnp.int32, (128, 256), 0)
    kj = lax.broadcasted_iota(jnp.int32, (128, 256), 1)
    return (kj >= qi) & (kj <= qi + 128)


def _attn_fwd(q, k, v, bias, name):
    B, _, dil, L, _ = q.shape
    nb = L // 128

    def body(q_ref, k_ref, v_ref, b_ref, o_ref, l_ref):
        mask = _band_mask()
        bias_v = b_ref[...]
        o, l = _attn_block(q_ref[0:128, :], k_ref[0:128, :], v_ref[0:128, :], bias_v[:, 128:], mask[:, 128:])
        o_ref[0:128, :] = o
        l_ref[0:128, :] = l
        if nb > 1:
            def step(n, carry):
                r0 = pl.multiple_of(n * 128, 128)
                p0 = pl.multiple_of(n * 128 - 128, 128)
                o, l = _attn_block(q_ref[pl.ds(r0, 128), :], k_ref[pl.ds(p0, 256), :], v_ref[pl.ds(p0, 256), :],
                                   bias_v, mask)
                o_ref[pl.ds(r0, 128), :] = o
                l_ref[pl.ds(r0, 128), :] = l
                return carry
            lax.fori_loop(1, nb, step, 0)

    blk = pl.BlockSpec((None, None, None, L, 64), lambda b, h, r: (b, h, r, 0, 0))
    lblk = pl.BlockSpec((None, None, None, L, 1), lambda b, h, r: (b, h, r, 0, 0))
    return _pcall(body, name=name, grid=(B, 4, dil),
                  in_specs=[blk, blk, blk, pl.BlockSpec((None, 128, 256), lambda b, h, r: (h, 0, 0))],
                  out_specs=[blk, lblk],
                  out_shape=[jax.ShapeDtypeStruct(q.shape, F32), jax.ShapeDtypeStruct(q.shape[:4] + (1,), F32)],
                  compiler_params=_cp(("parallel", "parallel", "parallel")))(q, k, v, bias)


def _attn_bwd(q, k, v, bias, do, dl, name):
    B, _, dil, L, _ = q.shape
    nb = L // 128

    def body(q_ref, k_ref, v_ref, b_ref, do_ref, dl_ref, dq_ref, dk_ref, dv_ref, db_ref, acc_ref):
        bi, ri = pl.program_id(1), pl.program_id(2)
        mask = _band_mask()
        bias_v = b_ref[...]
        dk_ref[...] = jnp.zeros_like(dk_ref)
        dv_ref[...] = jnp.zeros_like(dv_ref)
        f0 = functools.partial(_attn_block, mask=mask[:, 128:])
        _, vjp = jax.vjp(f0, q_ref[0:128, :], k_ref[0:128, :], v_ref[0:128, :], bias_v[:, 128:])
        dq, dkb, dvb, dbs = vjp((do_ref[0:128, :], dl_ref[0:128, :]))
        dq_ref[0:128, :] = dq
        dk_ref[0:128, :] += dkb
        dv_ref[0:128, :] += dvb
        acc_ref[:, 0:128] = jnp.zeros((128, 128), F32)
        acc_ref[:, 128:256] = dbs
        if nb > 1:
            f1 = functools.partial(_attn_block, mask=mask)

            def step(n, carry):
                r0 = pl.multiple_of(n * 128, 128)
                p0 = pl.multiple_of(n * 128 - 128, 128)
                _, vjp = jax.vjp(f1, q_ref[pl.ds(r0, 128), :], k_ref[pl.ds(p0, 256), :], v_ref[pl.ds(p0, 256), :], bias_v)
                dq, dkb, dvb, dbs = vjp((do_ref[pl.ds(r0, 128), :], dl_ref[pl.ds(r0, 128), :]))
                dq_ref[pl.ds(r0, 128), :] = dq
                dk_ref[pl.ds(p0, 256), :] += dkb
                dv_ref[pl.ds(p0, 256), :] += dvb
                acc_ref[...] += dbs
                return carry
            lax.fori_loop(1, nb, step, 0)
        first = (bi == 0) & (ri == 0)

        @pl.when(first)
        def _():
            db_ref[...] = acc_ref[...]

        @pl.when(jnp.logical_not(first))
        def _():
            db_ref[...] += acc_ref[...]

    blk = pl.BlockSpec((None, None, None, L, 64), lambda h, b, r: (b, h, r, 0, 0))
    lblk = pl.BlockSpec((None, None, None, L, 1), lambda h, b, r: (b, h, r, 0, 0))
    bblk = pl.BlockSpec((None, 128, 256), lambda h, b, r: (h, 0, 0))
    sds = jax.ShapeDtypeStruct(q.shape, F32)
    return _pcall(body, name=name, grid=(4, B, dil), in_specs=[blk, blk, blk, bblk, blk, lblk],
                  out_specs=[blk, blk, blk, bblk], out_shape=[sds, sds, sds, jax.ShapeDtypeStruct((4, 128, 256), F32)],
                  scratch_shapes=[pltpu.VMEM((128, 256), F32)],
                  compiler_params=_cp(("parallel", "arbitrary", "arbitrary")))(q, k, v, bias, do, dl)


def _lse_merge(o0, o1, o2, l0, l1, l2):
    m = lax.stop_gradient(jnp.maximum(jnp.maximum(l0, l1), l2))
    e0, e1, e2 = jnp.exp(l0 - m), jnp.exp(l1 - m), jnp.exp(l2 - m)
    den = e0 + e1 + e2
    return (e0 / den) * o0 + (e1 / den) * o1 + (e2 / den) * o2


def _attn_merge(outs, lses, dy, name):
    B = outs[0].shape[0]
    bwd = dy is not None

    def body(*refs):
        vals = [r[...] for r in refs[:6]]
        if not bwd:
            refs[6][...] = _lse_merge(*vals)
        else:
            _, vjp = jax.vjp(_lse_merge, *vals)
            for r, g in zip(refs[7:], vjp(refs[6][...])):
                r[...] = g

    blk = pl.BlockSpec((None, None, S, 64), lambda b, h: (b, h, 0, 0))
    lblk = pl.BlockSpec((None, None, S, 1), lambda b, h: (b, h, 0, 0))
    osd = jax.ShapeDtypeStruct(outs[0].shape, F32)
    lsd = jax.ShapeDtypeStruct(lses[0].shape, F32)
    if not bwd:
        return _pcall(body, name=name, grid=(B, 4), in_specs=[blk] * 3 + [lblk] * 3, out_specs=blk, out_shape=osd,
                      compiler_params=_cp(("parallel", "parallel")))(*outs, *lses)
    return _pcall(body, name=name, grid=(B, 4), in_specs=[blk] * 3 + [lblk] * 3 + [blk],
                  out_specs=[blk] * 3 + [lblk] * 3, out_shape=[osd] * 3 + [lsd] * 3,
                  compiler_params=_cp(("parallel", "parallel")))(*outs, *lses, dy)


def _to_dil(t, dil):
    B = t.shape[0] // S
    return t.reshape(B, S // dil, dil, 4, 64).transpose(0, 3, 2, 1, 4)


def _from_dil(t):
    B, _, dil, L, w = t.shape
    return t.transpose(0, 1, 3, 2, 4).reshape(B, 4, S, w)


def _hm_to_dil(t, dil):
    B, _, _, w = t.shape
    return t.reshape(B, 4, S // dil, dil, w).transpose(0, 1, 3, 2, 4)


def _dil_to_rows(t):
    B, _, dil, L, _ = t.shape
    return t.transpose(0, 3, 2, 1, 4).reshape(B * S, 256)


def _gmerge(g0, g1, g2, gb, ya, yb, yc):
    return (jax.nn.sigmoid(g0 + gb[:, 0:D]) * ya + jax.nn.sigmoid(g1 + gb[:, D:2 * D]) * yb
            + jax.nn.sigmoid(g2 + gb[:, 2 * D:3 * D]) * yc)


def _gated_merge(gates, gb, ya, yb, yc, name):
    def fn(tv, fv):
        return [_gmerge(tv[0], tv[1], tv[2], fv[0], tv[3], tv[4], tv[5])], []
    return _rowwise(fn, [(gates, D, 0), (gates, D, 1), (gates, D, 2), ya, yb, yc], [gb], [(D, _ACT)], [], name=name)[0]


def _gated_merge_bwd(gates, gb, ya, yb, yc, dm, name):
    def fn(tv, fv):
        _, vjp = jax.vjp(_gmerge, tv[0], tv[1], tv[2], fv[0], tv[3], tv[4], tv[5])
        d0, d1, d2, dgb, da, db, dc = vjp(tv[6])
        return [[d0, d1, d2], da, db, dc], [dgb]
    return _rowwise(fn, [(gates, D, 0), (gates, D, 1), (gates, D, 2), ya, yb, yc, dm], [gb],
                    [(3 * D, _ACT), (D, _ACT), (D, _ACT), (D, _ACT)], [(1, 3 * D)], name=name)


def _pool_affine(t1, pb, ps, dout, name):
    if dout is None:
        def fn(tv, fv):
            return [(tv[0] + fv[0]) * fv[1]], []
        return _rowwise(fn, [t1], [pb, ps], [(POOLW, _ACT)], [], name=name)[0]

    def fnb(tv, fv):
        t2, vjp = jax.vjp(lambda t, b, s: (t + b) * s, tv[0], fv[0], fv[1])
        dt, db, dsc = vjp(tv[1])
        return [dt, t2], [db, dsc]
    return _rowwise(fnb, [t1, dout], [pb, ps], [(POOLW, _ACT), (POOLW, _ACT)], [(1, POOLW), (1, POOLW)], name=name)


def _dt_softplus(dt_raw, dt_bias, ddt, name):
    f = lambda r, b: _softplus(r + b)
    if ddt is None:
        def fn(tv, fv):
            return [f(tv[0], fv[0])], []
        return _rowwise(fn, [dt_raw], [dt_bias], [(16, F32)], [], name=name, tm=1024)[0]

    def fnb(tv, fv):
        _, vjp = jax.vjp(f, tv[0], fv[0])
        dr, db = vjp(tv[1])
        return [dr], [db]
    return _rowwise(fnb, [dt_raw, ddt], [dt_bias], [(16, F32)], [(1, 16)], name=name, tm=1024)


def _adamw(w, g, m, v, name):
    R, C = w.shape
    tm = _pick(R, (256, 128, 64, 32, 16, 8))
    c1 = 1.0 / (1.0 - ADAM_B1 ** ADAM_STEP)
    c2 = 1.0 / (1.0 - ADAM_B2 ** ADAM_STEP)

    def fn(tv, fv):
        wv, gv, mv, vv = tv
        mn = ADAM_B1 * mv + (1.0 - ADAM_B1) * gv
        vn = ADAM_B2 * vv + (1.0 - ADAM_B2) * (gv * gv)
        delta = -ADAM_LR * ((mn * c1) / (jnp.sqrt(vn * c2) + ADAM_EPS) + ADAM_WD * wv)
        return [delta, mn, vn], []
    return _rowwise(fn, [w, g, m, v], [], [(C, F32)] * 3, [], name=name, tm=tm)


def _ffn_fwd(x, xm, w13, w2, g, b, tag, dep=None):
    h = _mm(xm, w13, dep=dep, name=f"{tag}_h")
    s = _swiglu_act(h, name=f"{tag}_act")
    y = _mm(s, w2, name=f"{tag}_y")
    r, out, outm = _res_ln_fwd(x, y, g, b, 0.5, name=f"{tag}_ln")
    return out, outm, dict(x=xm, h=h, r=r)


def _ffn_bwd(dout, sv, w13, w2, g, b, tag, dep=None):
    dskip, dy, dg, db = _ln_bwd(sv['r'], g, b, dout, 0.5, name=f"{tag}_lnb")
    ds = _mm(dy, w2, tb=True, dep=dep, name=f"{tag}_ds")
    dh, s = _swiglu_act_bwd(sv['h'], ds, name=f"{tag}_actb")
    dw2 = _mm(s, dy, ta=True, name=f"{tag}_dw2")
    dw13 = _mm(sv['x'], dh, ta=True, name=f"{tag}_dw13")
    dx = _mm(dh, w13, tb=True, add=dskip, name=f"{tag}_dx")
    return dx, dict(w13=dw13, w2=dw2, g=dg, b=db)


def _mixer_fwd(x1, x1m, W, bias_all, tag):
    T = x1.shape[0]
    hcat = _mm(x1m, W['w_in_r'], name=f"{tag}_hcat")
    u = hcat[:, O_U:O_Z]
    z = hcat[:, O_Z:O_XBC]
    xbc = hcat[:, O_XBC:O_Q]
    gates = hcat[:, O_G:O_DT]
    dt_raw = hcat[:, O_DT:O_DT + 16]
    pooled = _pool_mean(u, False, name=f"{tag}_pool")
    t1 = _mm(pooled, W['pool_wbd'], name=f"{tag}_pt1")
    t2 = _pool_affine(t1, W['pool_b'], W['pool_scale'], None, name=f"{tag}_paff")
    ya = _mm(t2, W['p_pool'], name=f"{tag}_ya")
    act = _conv_silu(xbc, W['conv_w'], W['conv_b'], name=f"{tag}_conv")
    xs, bm, cm = act[:, :1024], act[:, 1024:1536], act[:, 1536:2048]
    dt = _dt_softplus(dt_raw, W['dt_bias'], None, name=f"{tag}_dt")
    dtg = dt.reshape(T, 4, 4).transpose(1, 0, 2)
    yscan, hs = _ssd_fwd(xs, dtg, bm, cm, W['a_neg'], W['d_skip'], name=f"{tag}_ssd")
    ybn = _ssd_gate_norm(yscan, z, W['ssd_norm'], name=f"{tag}_gn")
    yb = _mm(ybn, W['p_ssd'], name=f"{tag}_yb")
    outs, lses, qkv = [], [], []
    for gi, dil in enumerate(ATTN_DILS):
        qd = _to_dil(hcat[:, O_Q + 256 * gi:O_Q + 256 * gi + 256], dil)
        kd = _to_dil(hcat[:, O_K + 256 * gi:O_K + 256 * gi + 256], dil)
        vd = _to_dil(hcat[:, O_V + 256 * gi:O_V + 256 * gi + 256], dil)
        o, l = _attn_fwd(qd, kd, vd, bias_all[4 * gi:4 * gi + 4], name=f"{tag}_attn{gi}")
        qkv.append((qd, kd, vd))
        outs.append(_from_dil(o))
        lses.append(_from_dil(l))
    ym = _attn_merge(outs, lses, None, name=f"{tag}_amerge")
    ycp = ym.transpose(0, 2, 1, 3).reshape(T, 256).astype(_ACT)
    yc = _mm(ycp, W['p_attn'], name=f"{tag}_yc")
    merged = _gated_merge(gates, W['gate_b'], ya, yb, yc, name=f"{tag}_gm")
    mix = _mm(merged, W['w_out'], name=f"{tag}_mix")
    r, out, outm = _res_ln_fwd(x1, mix, W['ln2_g'], W['ln2_b'], 1.0, name=f"{tag}_ln")
    sv = dict(x1=x1m, z=z, xbc=xbc, gates=gates, dt_raw=dt_raw, pooled=pooled, t1=t1, xs=xs, bm=bm, cm=cm, dtg=dtg,
              hs=hs, yscan=yscan, ybn=ybn, qkv=qkv, outs=outs, lses=lses, ycp=ycp, ya=ya, yb=yb, yc=yc,
              merged=merged, r=r)
    return out, outm, sv


def _mixer_bwd(dout, sv, W, bias_all, tag):
    T = dout.shape[0]
    B = T // S
    gr = {}
    dx1a, dr, gr['ln2_g'], gr['ln2_b'] = _ln_bwd(sv['r'], W['ln2_g'], W['ln2_b'], dout, 1.0, name=f"{tag}_lnb")
    dmerged = _mm(dr, W['w_out'], tb=True, name=f"{tag}_dmerged")
    gr['w_out'] = _mm(sv['merged'], dr, ta=True, name=f"{tag}_dwout")
    dgates, dya, dyb, dyc, gr['gate_b'] = _gated_merge_bwd(sv['gates'], W['gate_b'], sv['ya'], sv['yb'], sv['yc'],
                                                           dmerged, name=f"{tag}_gmb")
    dycp = _mm(dyc, W['p_attn'], tb=True, name=f"{tag}_dycp")
    gr['p_attn'] = _mm(sv['ycp'], dyc, ta=True, name=f"{tag}_dpattn")
    dym = dycp.reshape(B, S, 4, 64).transpose(0, 2, 1, 3)
    dml = _attn_merge(sv['outs'], sv['lses'], dym, name=f"{tag}_amergeb")
    dq, dk, dv, dbias = [], [], [], []
    for gi, dil in enumerate(ATTN_DILS):
        qd, kd, vd = sv['qkv'][gi]
        a, b, c, d = _attn_bwd(qd, kd, vd, bias_all[4 * gi:4 * gi + 4], _hm_to_dil(dml[gi], dil),
                               _hm_to_dil(dml[3 + gi], dil), name=f"{tag}_attnb{gi}")
        dq.append(_dil_to_rows(a))
        dk.append(_dil_to_rows(b))
        dv.append(_dil_to_rows(c))
        dbias.append(d)
    dbias = jnp.concatenate(dbias, axis=0)
    dybn = _mm(dyb, W['p_ssd'], tb=True, name=f"{tag}_dybn")
    gr['p_ssd'] = _mm(sv['ybn'], dyb, ta=True, name=f"{tag}_dpssd")
    dyscan, dz, gr['ssd_norm'] = _ssd_gate_norm_bwd(sv['yscan'], sv['z'], W['ssd_norm'], dybn, name=f"{tag}_gnb")
    dxs, ddtg, dbm, dcm, dak = _ssd_bwd(sv['xs'], sv['dtg'], sv['bm'], sv['cm'], W['a_neg'], W['d_skip'], sv['hs'],
                                        dyscan, name=f"{tag}_ssdb")
    gr['a_neg'], gr['d_skip'] = dak[:, 0, 0:4], dak[:, 1, 0:4]
    ddt = ddtg.transpose(1, 0, 2).reshape(T, 16)
    ddt_raw, gr['dt_bias'] = _dt_softplus(sv['dt_raw'], W['dt_bias'], ddt, name=f"{tag}_dtb")
    dact = jnp.concatenate([dxs, dbm, dcm], axis=1)
    dxbc, gr['conv_w'], gr['conv_b'] = _conv_silu_bwd(sv['xbc'], W['conv_w'], W['conv_b'], dact, name=f"{tag}_convb")
    dt2 = _mm(dya, W['p_pool'], tb=True, name=f"{tag}_dt2")
    dt1, t2, gr['pool_b'], gr['pool_scale'] = _pool_affine(sv['t1'], W['pool_b'], W['pool_scale'], dt2, name=f"{tag}_paffb")
    gr['p_pool'] = _mm(t2, dya, ta=True, name=f"{tag}_dppool")
    dpooled = _mm(dt1, W['pool_wbd'], tb=True, name=f"{tag}_dpooled")
    gr['pool_wbd'] = _mm(sv['pooled'], dt1, ta=True, name=f"{tag}_dpoolw")
    du = _pool_mean(dpooled, True, name=f"{tag}_poolb")
    dhcat = jnp.concatenate([t.astype(_ACT) for t in [du, dz, dxbc] + dq + dk + dv + [dgates, ddt_raw]]
                            + [jnp.zeros((T, HC - O_DT - 16), _ACT)], axis=1)
    dx1 = _mm(dhcat, W['w_in_r'], tb=True, add=dx1a, name=f"{tag}_dx1")
    gr['w_in_r'] = _mm(sv['x1'], dhcat, ta=True, name=f"{tag}_dwin")
    return dx1, gr, dbias


def _prep_layer_weights(i, inp, G):
    W = {}
    for n in BIG:
        g = G[n]
        if n == 'w_in':
            W['w_in_r'] = jnp.concatenate(_nat_pieces(g, 0, 3840) + _nat_pieces(g, 3856, 9232) + _nat_pieces(g, 3840, 3856)
                                          + [jnp.zeros((D, HC - 9232), g.dtype)], axis=1)
        elif n in COL_SHARDED:
            W[n] = jnp.concatenate([g[j] for j in range(4)], axis=1)
        else:
            W[n] = g.reshape(4 * g.shape[1], g.shape[2])
    pw = inp['pool_w'][i].astype(_MXU)
    wbd = jnp.zeros((POOLW, POOLW), _MXU)
    for g in range(4):
        wbd = lax.dynamic_update_slice(wbd, pw[g], (g * POOL_GDIM, g * POOL_GDIM))
    W['pool_wbd'] = wbd
    W['pool_b'] = inp['pool_b'][i].reshape(1, POOLW)
    W['pool_scale'] = inp['pool_scale'][i].reshape(1, POOLW)
    W['conv_w'] = jnp.concatenate([G['conv_w'][j] for j in range(4)], axis=1)
    W['conv_b'] = inp['conv_b'][i].reshape(1, 2048)
    W['dt_bias'] = inp['dt_bias'][i].reshape(1, 16)
    W['a_neg'] = (-jnp.exp(inp['a_log'][i])).reshape(4, 1, 4)
    W['d_skip'] = inp['d_skip'][i].reshape(4, 1, 4)
    W['ssd_norm'] = inp['ssd_norm'][i].reshape(1, D)
    W['gate_b'] = jnp.concatenate([G['gate_b'][j][b:b + 1] for b in range(3) for j in range(4)], axis=1)
    for n in ('ln1_g', 'ln1_b', 'ln2_g', 'ln2_b', 'ln3_g', 'ln3_b'):
        W[n] = inp[n][i].reshape(1, D)
    return W


def _gather_layer(i, inp):
    core = lax.axis_index("c")
    me = 2 * lax.axis_index("x") + lax.axis_index("y")
    halves = []
    for n in BIG:
        s = inp[n][i]
        halves.append(lax.dynamic_slice_in_dim(s, core * (s.shape[0] // 2), s.shape[0] // 2, axis=0).astype(BF16))
    small = [inp['gate_b'][i], inp['conv_w'][i]]
    return _exchange_start(halves + small, "gather", name="gather_start")


def _gather_finish(state, after):
    core = lax.axis_index("c")
    me = 2 * lax.axis_index("x") + lax.axis_index("y")
    own, outs = _exchange_wait(state, after, "gather", name="gather_wait")
    outs = [lax.dynamic_update_slice(o, a[None], (me, 0, 0)) for o, a in zip(outs, own)]
    mine = outs[:len(BIG)]
    theirs = _exchange(mine, "cores", "gather", name="gather_share")
    G = {}
    for n, a, b in zip(BIG, mine, theirs):
        G[n] = jnp.concatenate([jnp.where(core == 0, a, b), jnp.where(core == 0, b, a)], axis=1)
    G['gate_b'], G['conv_w'] = outs[len(BIG)], outs[len(BIG) + 1]
    return G


W_IN_SHARD = 2308


def _nat_pieces(g, lo, hi):
    out = []
    for j in range(4):
        s, e = max(lo, W_IN_SHARD * j), min(hi, W_IN_SHARD * (j + 1))
        if s < e:
            out.append(g[j][:, s - W_IN_SHARD * j:e - W_IN_SHARD * j])
    return out


def _reord_ranges(lo, hi):
    out = []
    for a, b, off in ((0, 3840, 0), (3840, 3856, O_DT - 3840), (3856, 9232, -16)):
        s, e = max(lo, a), min(hi, b)
        if s < e:
            out.append((s + off, e + off))
    return out


def _halves_of(n, g):
    if n == 'w_in':
        shards = [jnp.concatenate([g[:, a:b] for a, b in _reord_ranges(W_IN_SHARD * j, W_IN_SHARD * (j + 1))], axis=1)
                  for j in range(4)]
    elif n in COL_SHARDED:
        c = g.shape[1] // 4
        shards = [g[:, j * c:(j + 1) * c] for j in range(4)]
    else:
        r = g.shape[0] // 4
        shards = [g[j * r:(j + 1) * r] for j in range(4)]
    r2 = shards[0].shape[0] // 2
    return jnp.stack([jnp.concatenate([s[h * r2:(h + 1) * r2] for s in shards], axis=0) for h in range(2)])


def _reduce_start(grads):
    names = list(grads)
    halves = [_halves_of(n, grads[n]) for n in names]
    core = lax.axis_index("c").reshape(1)
    got = _exchange(halves, "cores", "scatter", name="rs_cores")
    chip = [_sum_own_recv(h, t, core, BF16, name="rs_sum2") for h, t in zip(halves, got)]
    chip = [t.reshape(4, t.shape[0] // 4, t.shape[1]) for t in chip]
    state, token = _exchange_start(chip, "scatter", name="rs_start")
    return (names, state), token


def _reduce_finish(handle, after):
    names, state = handle
    chip_id = (2 * lax.axis_index("x") + lax.axis_index("y")).reshape(1)
    chip, got = _exchange_wait(state, after, "scatter", name="rs_wait")
    red = [_sum_own_recv(h, t, chip_id, F32, name="rs_sum4") for h, t in zip(chip, got)]
    other = _exchange(red, "cores", "gather", name="rs_share")
    out = {}
    for n, mine, theirs in zip(names, red, other):
        out[n] = jnp.where(lax.axis_index("c") == 0, jnp.concatenate([mine, theirs]), jnp.concatenate([theirs, mine]))
    return out


class _Comm:
    def __init__(self, inp):
        self.inp = inp

    def gather_start(self, i):
        return _gather_layer(i, self.inp)

    def gather_finish(self, state, after):
        return _gather_finish(state, after)

    def reduce_start(self, i, grads):
        return _reduce_start({n: grads[n] for n in BIG})

    def reduce_finish(self, handle, after):
        return _reduce_finish(handle, after)


def _allreduce_small(vec):
    for group in ("cores", "x", "y"):
        recv = _exchange([vec], group, "gather", name=f"ar_{group}")[0]
        vec = _rowwise(lambda tv, fv: ([tv[0] + tv[1]], []), [vec, recv], [], [(128, F32)], [], name=f"ar_add_{group}")[0]
    return vec


def _pack(arrs):
    flat = jnp.concatenate([a.reshape(-1) for a in arrs])
    n = flat.shape[0]
    pad = (-n) % (256 * 128)
    flat = jnp.concatenate([flat, jnp.zeros((pad,), F32)])
    return flat.reshape(-1, 128)


def _unpack(p, shapes):
    flat = p.reshape(-1)
    out, off = [], 0
    for s in shapes:
        sz = int(np.prod(s))
        out.append(flat[off:off + sz].reshape(s))
        off += sz
    return out


def _forward_backward(inp, comm, bias_all):
    x = xm = inp['x'].reshape(-1, D)
    tgt = inp['loss_target'].reshape(-1, D)
    saved, Ws = [], []
    state, _ = comm.gather_start(0)
    G = comm.gather_finish(state, x)
    for i in range(NL):
        W = _prep_layer_weights(i, inp, G)
        dep = None
        if i + 1 < NL:
            state, dep = comm.gather_start(i + 1)
        x1, x1m, s1 = _ffn_fwd(x, xm, W['ffn1_w13'], W['ffn1_w2'], W['ln1_g'], W['ln1_b'], "f1", dep)
        x2, x2m, s2 = _mixer_fwd(x1, x1m, W, bias_all, "mx")
        x, xm, s3 = _ffn_fwd(x2, x2m, W['ffn2_w13'], W['ffn2_w2'], W['ln3_g'], W['ln3_b'], "f2")
        if i + 1 < NL:
            G = comm.gather_finish(state, xm)
        saved.append((s1, s2, s3))
        Ws.append(W)
    dy, lpart = _loss_fwd_bwd(x, tgt, name="loss")
    fins, reduced, dbiases = [None] * NL, [None] * NL, [None] * NL
    pending, dep = None, None
    for i in reversed(range(NL)):
        W = Ws[i]
        s1, s2, s3 = saved[i]
        g = {}
        dx2, f = _ffn_bwd(dy, s3, W['ffn2_w13'], W['ffn2_w2'], W['ln3_g'], W['ln3_b'], "f2", dep)
        g['ffn2_w13'], g['ffn2_w2'], g['ln3_g'], g['ln3_b'] = f['w13'], f['w2'], f['g'], f['b']
        dx1, gm, dbiases[i] = _mixer_bwd(dx2, s2, W, bias_all, "mx")
        g.update(gm)
        dy, f = _ffn_bwd(dx1, s1, W['ffn1_w13'], W['ffn1_w2'], W['ln1_g'], W['ln1_b'], "f1")
        g['ffn1_w13'], g['ffn1_w2'], g['ln1_g'], g['ln1_b'] = f['w13'], f['w2'], f['g'], f['b']
        fins[i] = _finish_layer_grads(i, g, inp)
        if pending is not None:
            reduced[pending[0]] = comm.reduce_finish(pending[1], dy)
        handle, dep = comm.reduce_start(i, fins[i])
        pending = (i, handle)
    return lpart, dy, fins, reduced, pending, dbiases


def _finish_layer_grads(i, g, inp):
    out = {n: g[n] for n in BIG if n != 'w_in'}
    out['w_in'] = g['w_in_r']
    out['pool_w'] = jnp.stack([g['pool_wbd'][k * POOL_GDIM:(k + 1) * POOL_GDIM, k * POOL_GDIM:(k + 1) * POOL_GDIM] for k in range(4)])
    out['pool_b'] = g['pool_b'].reshape(4, POOL_GDIM)
    out['pool_scale'] = g['pool_scale'].reshape(POOLW)
    out['conv_w'] = g['conv_w']
    out['conv_b'] = g['conv_b'].reshape(2048)
    out['dt_bias'] = g['dt_bias'].reshape(16)
    out['a_log'] = (g['a_neg'].reshape(16)) * (-jnp.exp(inp['a_log'][i]))
    out['d_skip'] = g['d_skip'].reshape(16)
    out['ssd_norm'] = g['ssd_norm'].reshape(D)
    out['gate_b'] = g['gate_b'].reshape(3, D)
    for n in ('ln1_g', 'ln1_b', 'ln2_g', 'ln2_b', 'ln3_g', 'ln3_b'):
        out[n] = g[n].reshape(D)
    return out


def kernel(x, ffn1_w13, ffn1_w2, ln1_g, ln1_b, w_in, gate_b, pool_w, pool_b, pool_scale, conv_w, conv_b,
           dt_bias, a_log, d_skip, ssd_norm, rel_bias, p_pool, p_ssd, p_attn, w_out, ln2_g, ln2_b, ffn2_w13,
           ffn2_w2, ln3_g, ln3_b, loss_target, m_ffn1_w13, m_ffn1_w2, m_ln1_g, m_ln1_b, m_w_in, m_gate_b,
           m_pool_w, m_pool_b, m_pool_scale, m_conv_w, m_conv_b, m_dt_bias, m_a_log, m_d_skip, m_ssd_norm,
           m_rel_bias, m_p_pool, m_p_ssd, m_p_attn, m_w_out, m_ln2_g, m_ln2_b, m_ffn2_w13, m_ffn2_w2, m_ln3_g,
           m_ln3_b, v_ffn1_w13, v_ffn1_w2, v_ln1_g, v_ln1_b, v_w_in, v_gate_b, v_pool_w, v_pool_b,
           v_pool_scale, v_conv_w, v_conv_b, v_dt_bias, v_a_log, v_d_skip, v_ssd_norm, v_rel_bias, v_p_pool,
           v_p_ssd, v_p_attn, v_w_out, v_ln2_g, v_ln2_b, v_ffn2_w13, v_ffn2_w2, v_ln3_g, v_ln3_b):
    inp = dict(locals())
    maps = jnp.asarray(_bucket_maps())
    bias_all = _bias_build(rel_bias, maps, name="bias_build")
    comm = _Comm(inp)
    lpart, gx, fins, red, pending, dbiases = _forward_backward(inp, comm, bias_all)
    loss = lax.psum(lpart[0, 0], ("x", "y", "c"))

    small_l = [n for n in SMALL if n != 'rel_bias']
    drel = _bias_reduce(jnp.stack(dbiases), maps, name="bias_reduce")[:, 0, :32].T
    small_arrs = [jnp.stack([fins[i][n] for i in range(NL)]) for n in small_l] + [drel]
    packed = _allreduce_small(_pack(small_arrs))
    gsmall = dict(zip(small_l + ['rel_bias'], _unpack(packed, [a.shape for a in small_arrs])))
    shard = 2 * lax.axis_index("x") + lax.axis_index("y")
    gsmall['gate_b'] = lax.dynamic_slice_in_dim(gsmall['gate_b'], shard * 256, 256, axis=2)
    gsmall['conv_w'] = lax.dynamic_slice_in_dim(gsmall['conv_w'], shard * 512, 512, axis=2)

    red[pending[0]] = comm.reduce_finish(pending[1], packed)
    gout = {n: jnp.stack([red[i][n] for i in range(NL)]) for n in BIG}
    gout.update(gsmall)

    delta, new_m, new_v = {}, {}, {}
    for n in BIG:
        shp = inp[n].shape
        two_d = lambda a: a.reshape(shp[0] * shp[1], shp[2])
        d, m, v = _adamw(two_d(inp[n]), two_d(gout[n]), two_d(inp['m_' + n]), two_d(inp['v_' + n]), name="adamw_big")
        delta[n], new_m[n], new_v[n] = d.reshape(shp), m.reshape(shp), v.reshape(shp)
    shapes = [inp[n].shape for n in SMALL]
    d, m, v = _adamw(_pack([inp[n] for n in SMALL]), _pack([gout[n] for n in SMALL]),
                     _pack([inp['m_' + n] for n in SMALL]), _pack([inp['v_' + n] for n in SMALL]), name="adamw_small")
    for n, dd, mm, vv in zip(SMALL, _unpack(d, shapes), _unpack(m, shapes), _unpack(v, shapes)):
        delta[n], new_m[n], new_v[n] = dd, mm, vv

    return (loss, gx.reshape(x.shape), *[gout[n] for n in WEIGHTS], *[delta[n] for n in WEIGHTS],
            *[new_m[n] for n in WEIGHTS], *[new_v[n] for n in WEIGHTS])
```

```python
import functools

import numpy as np
import jax
import jax.numpy as jnp
from jax import lax
from jax.experimental import pallas as pl
from jax.experimental.pallas import tpu as pltpu

F32 = jnp.float32
BF16 = jnp.bfloat16
_MXU = jnp.bfloat16
_ACT = jnp.bfloat16
_VMEM_LIMIT = 56 * 1024 * 1024

S = 2048
D = 1024
NL = 4
DFF = 2816
LN_EPS = 1e-5
SSD_EPS = 1e-5
ALPHA = (2.0 * NL) ** 0.25
POOLW = 768
POOL_WINDOWS = (2, 4, 8, 16)
POOL_GDIM = 192
CH = 128
ATTN_DILS = (1, 4, 16)
HC = 9728
O_U, O_Z, O_XBC, O_Q, O_K, O_V, O_G, O_DT = 0, 768, 1792, 3840, 4608, 5376, 6144, 9216

ADAM_LR, ADAM_B1, ADAM_B2, ADAM_EPS, ADAM_WD, ADAM_STEP = 0.001, 0.9, 0.999, 1e-08, 0.01, 10

WEIGHTS = ['ffn1_w13', 'ffn1_w2', 'ln1_g', 'ln1_b', 'w_in', 'gate_b', 'pool_w', 'pool_b', 'pool_scale', 'conv_w',
           'conv_b', 'dt_bias', 'a_log', 'd_skip', 'ssd_norm', 'rel_bias', 'p_pool', 'p_ssd', 'p_attn', 'w_out',
           'ln2_g', 'ln2_b', 'ffn2_w13', 'ffn2_w2', 'ln3_g', 'ln3_b']
BIG = ['ffn1_w13', 'ffn1_w2', 'w_in', 'p_pool', 'p_ssd', 'p_attn', 'w_out', 'ffn2_w13', 'ffn2_w2']
COL_SHARDED = {'ffn1_w13', 'ffn2_w13', 'w_in', 'p_pool', 'p_attn'}
SMALL = [n for n in WEIGHTS if n not in BIG]


def _pcall(body, **kw):
    return pl.pallas_call(body, **kw)


def _cp(sem=None):
    return pltpu.CompilerParams(dimension_semantics=sem, vmem_limit_bytes=_VMEM_LIMIT)


def _pick(n, cands):
    for c in cands:
        if n % c == 0:
            return c
    raise ValueError(f"no tile for {n}")


def _mm(a, b, *, ta=False, tb=False, add=None, out_dtype=F32, dep=None, name):
    if ta:
        K, M = a.shape
    else:
        M, K = a.shape
    if tb:
        N, K2 = b.shape
    else:
        K2, N = b.shape
    assert K == K2, (a.shape, b.shape, ta, tb)
    sa, sb, so = a.dtype.itemsize, b.dtype.itemsize, jnp.dtype(out_dtype).itemsize
    tm, tn, tk = _mm_tiles(M, N, K, sa, sb, so + (4 if add is not None else 0))
    nk = K // tk
    a_bytes, b_bytes = M * K * sa, K * N * sb
    j_outer = nk == 1 and (b_bytes + a_bytes * (N // tn) < a_bytes + b_bytes * (M // tm))
    ij = (lambda p, q: (q, p)) if j_outer else (lambda p, q: (p, q))

    def im(f):
        return lambda p, q, k: f(*ij(p, q), k)

    a_spec = pl.BlockSpec((tk, tm), im(lambda i, j, k: (k, i))) if ta else pl.BlockSpec((tm, tk), im(lambda i, j, k: (i, k)))
    b_spec = pl.BlockSpec((tn, tk), im(lambda i, j, k: (j, k))) if tb else pl.BlockSpec((tk, tn), im(lambda i, j, k: (k, j)))
    o_spec = pl.BlockSpec((tm, tn), im(lambda i, j, k: (i, j)))
    dims = (((0 if ta else 1,), (1 if tb else 0,)), ((), ()))
    has_add = add is not None

    n_in = 2 + int(has_add) + int(dep is not None)

    def body(*refs):
        a_ref, b_ref = refs[0], refs[1]
        add_ref = refs[2] if has_add else None
        o_ref = refs[n_in]
        part = lax.dot_general(a_ref[...].astype(_MXU), b_ref[...].astype(_MXU), dims, preferred_element_type=F32)

        def finish(r):
            if has_add:
                r = r + add_ref[...]
            o_ref[...] = r.astype(out_dtype)

        if nk == 1:
            finish(part)
        else:
            acc = refs[-1]
            k = pl.program_id(2)

            @pl.when(k == 0)
            def _():
                acc[...] = part

            @pl.when(k > 0)
            def _():
                acc[...] += part

            @pl.when(k == nk - 1)
            def _():
                finish(acc[...])

    in_specs = [a_spec, b_spec]
    args = [a, b]
    if has_add:
        in_specs.append(o_spec)
        args.append(add)
    if dep is not None:
        in_specs.append(pl.BlockSpec(memory_space=pl.ANY))
        args.append(dep)
    gm, gn = M // tm, N // tn
    return _pcall(
        body, name=name, grid=((gn, gm, nk) if j_outer else (gm, gn, nk)), in_specs=in_specs, out_specs=o_spec,
        out_shape=jax.ShapeDtypeStruct((M, N), out_dtype),
        scratch_shapes=([pltpu.VMEM((tm, tn), F32)] if nk > 1 else []),
        compiler_params=_cp(("parallel", "parallel", "arbitrary")),
    )(*args)


_MM_VMEM_BUDGET = 40 * 1024 * 1024


def _divisors128(n, cap):
    return [d for d in range(128, min(n, cap) + 1, 128) if n % d == 0][::-1]


def _mm_tiles(M, N, K, sa, sb, so):
    best = None
    for tm in _divisors128(M, 1024):
        for tn in _divisors128(N, 2560):
            for tk in ([K] if K <= 4096 else []) + _divisors128(K, 2048):
                nk = K // tk
                need = 2 * (tm * tk * sa + tk * tn * sb + tm * tn * so) + (tm * tn * 4 if nk > 1 else 0)
                need += tm * tk * 2 + tk * tn * 2 + tm * tn * 4
                if need > _MM_VMEM_BUDGET:
                    continue
                score = (tm * tn, tk)
                if best is None or score > best[0]:
                    best = (score, (tm, tn, tk))
                break
    assert best is not None, (M, N, K)
    return best[1]


def _store(ref, val):
    if isinstance(val, (list, tuple)):
        off = 0
        for p in val:
            w = p.shape[1]
            ref[:, off:off + w] = p.astype(ref.dtype)
            off += w
    else:
        ref[...] = val.astype(ref.dtype)


def _acc_store(ref, val, first):
    pieces = val if isinstance(val, (list, tuple)) else [val]
    off = 0
    for p in pieces:
        w = p.shape[1]

        @pl.when(first)
        def _(p=p, off=off, w=w):
            ref[:, off:off + w] = p

        @pl.when(jnp.logical_not(first))
        def _(p=p, off=off, w=w):
            ref[:, off:off + w] += p

        off += w


def _rowwise(fn, tiled, full, out_tiled, out_acc, *, name, tm=256):
    arrs, specs = [], []
    for t in tiled:
        arr, w, cb = t if isinstance(t, tuple) else (t, t.shape[1], 0)
        arrs.append(arr)
        specs.append(pl.BlockSpec((tm, w), functools.partial(lambda i, cb: (i, cb), cb=cb)))
    R = arrs[0].shape[0]
    assert R % tm == 0
    for f in full:
        arrs.append(f)
        specs.append(pl.BlockSpec(f.shape, functools.partial(lambda i, nd: (0,) * nd, nd=f.ndim)))
    nt, nf, no = len(tiled), len(full), len(out_tiled)

    def body(*refs):
        tv = [r[...] for r in refs[:nt]]
        fv = [r[...] for r in refs[nt:nt + nf]]
        ot, oa = fn(tv, fv)
        for r, v in zip(refs[nt + nf:nt + nf + no], ot):
            _store(r, v)
        first = pl.program_id(0) == 0
        for r, v in zip(refs[nt + nf + no:], oa):
            _acc_store(r, v, first)

    out_shape = [jax.ShapeDtypeStruct((R, c), dt) for c, dt in out_tiled]
    out_specs = [pl.BlockSpec((tm, c), lambda i: (i, 0)) for c, _ in out_tiled]
    for shp in out_acc:
        out_shape.append(jax.ShapeDtypeStruct(shp, F32))
        out_specs.append(pl.BlockSpec(shp, lambda i: (0, 0)))
    return _pcall(body, name=name, grid=(R // tm,), in_specs=specs, out_specs=out_specs, out_shape=out_shape,
                  compiler_params=_cp(("arbitrary",)))(*arrs)


def _group(group):
    x, y, c = lax.axis_index("x"), lax.axis_index("y"), lax.axis_index("c")
    if group == "chips":
        return 2 * x + y, [((x, 1 - y, c), 2 * x + 1 - y), ((1 - x, y, c), 2 * (1 - x) + y),
                           ((1 - x, 1 - y, c), 2 * (1 - x) + 1 - y)]
    if group == "cores":
        return c, [((x, y, 1 - c), 1 - c)]
    if group == "x":
        return x, [((1 - x, y, c), 1 - x)]
    return y, [((x, 1 - y, c), 1 - y)]


def _exchange(arrs, group, mode, name):
    chips = group == "chips"
    k = len(arrs)
    npeer = 3 if chips else 1

    def body(*refs):
        ins, outs = refs[:k], refs[k:2 * k]
        send_sems, recv_sems = refs[2 * k:]
        me, peers = _group(group)
        remote = []
        for i in range(k):
            for p, (dev, slot) in enumerate(peers):
                src = ins[i].at[slot] if mode == "scatter" else ins[i]
                if not chips:
                    dst = outs[i]
                else:
                    dst = outs[i].at[p] if mode == "scatter" else outs[i].at[me]
                cp = pltpu.make_async_remote_copy(src_ref=src, dst_ref=dst, send_sem=send_sems.at[i, p],
                                                  recv_sem=recv_sems.at[i, p], device_id=dev,
                                                  device_id_type=pl.DeviceIdType.MESH)
                cp.start()
                remote.append(cp)
        for cp in remote:
            cp.wait_recv()
        for cp in remote:
            cp.wait_send()

    def oshape(a):
        piece = a.shape[1:] if mode == "scatter" else a.shape
        if chips:
            piece = ((3,) if mode == "scatter" else (4,)) + piece
        return jax.ShapeDtypeStruct(piece, a.dtype)

    any_spec = pl.BlockSpec(memory_space=pl.ANY)
    return _pcall(body, name=name, in_specs=[any_spec] * k, out_specs=[any_spec] * k, out_shape=[oshape(a) for a in arrs],
                  scratch_shapes=[pltpu.SemaphoreType.DMA((k, npeer)), pltpu.SemaphoreType.DMA((k, npeer))])(*arrs)


def _chip_copies(ins, lands, send_sems, recv_sems, mode):
    me, peers = _group("chips")
    out = []
    for i in range(len(ins)):
        for p, (dev, slot) in enumerate(peers):
            src = ins[i].at[slot] if mode == "scatter" else ins[i]
            dst = lands[i].at[p] if mode == "scatter" else lands[i].at[me]
            out.append(pltpu.make_async_remote_copy(src_ref=src, dst_ref=dst, send_sem=send_sems.at[3 * i + p],
                                                    recv_sem=recv_sems.at[3 * i + p], device_id=dev,
                                                    device_id_type=pl.DeviceIdType.MESH))
    return out


def _exchange_start(arrs, mode, name):
    k = len(arrs)
    hbm = pl.BlockSpec(memory_space=pltpu.HBM)
    sem = pl.BlockSpec(memory_space=pltpu.SEMAPHORE)

    def land_shape(a):
        piece = a.shape[1:] if mode == "scatter" else a.shape
        return ((3,) if mode == "scatter" else (4,)) + piece

    def body(*refs):
        ins, lands = refs[:k], refs[k:2 * k]
        send_sems, recv_sems = refs[2 * k], refs[2 * k + 1]
        token = refs[-1]
        for cp in _chip_copies(ins, lands, send_sems, recv_sems, mode):
            cp.start()
        token[...] = jnp.zeros_like(token)

    srcs = [pltpu.with_memory_space_constraint(a, pltpu.HBM) for a in arrs]
    lands = [pltpu.with_memory_space_constraint(lax.empty(land_shape(a), a.dtype), pltpu.HBM) for a in arrs]
    out_shape = ([pltpu.SemaphoreType.DMA((3 * k,)), pltpu.SemaphoreType.DMA((3 * k,))]
                 + [pltpu.HBM(a.shape, a.dtype) for a in arrs] + [pltpu.HBM(land_shape(a), a.dtype) for a in arrs]
                 + [jax.ShapeDtypeStruct((8, 128), F32)])
    outs = _pcall(body, name=name, in_specs=[hbm] * (2 * k),
                  out_specs=[sem, sem] + [hbm] * (2 * k) + [pl.BlockSpec(memory_space=pltpu.VMEM)], out_shape=out_shape,
                  input_output_aliases={i: 2 + i for i in range(2 * k)},
                  compiler_params=pltpu.CompilerParams(has_side_effects=pltpu.SideEffectType.DATAFLOW_SIDE_EFFECTING))(
                      *srcs, *lands)
    return (outs[0], outs[1], list(outs[2:2 + k]), list(outs[2 + k:2 + 2 * k])), outs[-1]


def _exchange_wait(state, after, mode, name):
    send_sems, recv_sems, srcs, lands = state
    k = len(srcs)
    hbm = pl.BlockSpec(memory_space=pltpu.HBM)
    sem = pl.BlockSpec(memory_space=pltpu.SEMAPHORE)

    def body(*refs):
        ins, lnd = refs[:k], refs[k:2 * k]
        send_sems, recv_sems = refs[2 * k], refs[2 * k + 1]
        for cp in _chip_copies(ins, lnd, send_sems, recv_sems, mode):
            cp.wait_send()
            cp.wait_recv()

    outs = _pcall(body, name=name, in_specs=[hbm] * (2 * k) + [sem, sem, pl.BlockSpec(memory_space=pl.ANY)],
                  out_specs=[hbm] * (2 * k),
                  out_shape=[pltpu.HBM(a.shape, a.dtype) for a in srcs] + [pltpu.HBM(a.shape, a.dtype) for a in lands],
                  input_output_aliases={i: i for i in range(2 * k)},
                  compiler_params=pltpu.CompilerParams(has_side_effects=pltpu.SideEffectType.DATAFLOW_SIDE_EFFECTING))(
                      *srcs, *lands, send_sems, recv_sems, after)
    return list(outs[:k]), list(outs[k:])


def _sum_own_recv(own, recv, me, out_dtype, name):
    n, R, C = own.shape
    nr = 1 if recv.ndim == 2 else recv.shape[0]
    tr = _pick(R, (256, 128, 64, 32, 16, 8))

    def body(me_ref, own_ref, *refs):
        o_ref = refs[-1]
        acc = own_ref[...].astype(F32)
        for r in refs[:-1]:
            acc = acc + r[...].astype(F32)
        o_ref[...] = acc.astype(out_dtype)

    specs = [pl.BlockSpec((None, tr, C), lambda i, me_ref: (me_ref[0], i, 0))]
    args = [own]
    if recv.ndim == 2:
        specs.append(pl.BlockSpec((tr, C), lambda i, me_ref: (i, 0)))
        args.append(recv)
    else:
        for p in range(nr):
            specs.append(pl.BlockSpec((None, tr, C), functools.partial(lambda i, me_ref, p: (p, i, 0), p=p)))
            args.append(recv)
    gs = pltpu.PrefetchScalarGridSpec(num_scalar_prefetch=1, grid=(R // tr,), in_specs=specs,
                                      out_specs=pl.BlockSpec((tr, C), lambda i, me_ref: (i, 0)))
    return _pcall(body, name=name, grid_spec=gs, out_shape=jax.ShapeDtypeStruct((R, C), out_dtype),
                  compiler_params=_cp(("parallel",)))(me, *args)


def _silu(x):
    return x * jax.nn.sigmoid(x)


def _ln(r, g, b):
    mu = jnp.mean(r, -1, keepdims=True)
    xc = r - mu
    var = jnp.mean(xc * xc, -1, keepdims=True)
    return xc * lax.rsqrt(var + LN_EPS) * g + b


def _softplus(x):
    return jnp.maximum(x, 0.0) + jnp.log1p(jnp.exp(-jnp.abs(x)))


def _res_ln_fwd(x, y, g, b, res, name):
    def fn(tv, fv):
        r = ALPHA * tv[0] + res * tv[1]
        out = _ln(r, fv[0], fv[1])
        return [r, out, out], []
    return _rowwise(fn, [x, y], [g, b], [(D, F32), (D, F32), (D, _ACT)], [], name=name)


def _ln_bwd(r, g, b, dout, res, name):
    def fn(tv, fv):
        _, vjp = jax.vjp(_ln, tv[0], fv[0], fv[1])
        dr, dg, db = vjp(tv[1])
        return [ALPHA * dr, res * dr], [dg, db]
    return _rowwise(fn, [r, dout], [g, b], [(D, F32), (D, _ACT)], [(1, D), (1, D)], name=name)


def _swiglu_act(h, name):
    def fn(tv, fv):
        return [_silu(tv[0]) * tv[1]], []
    return _rowwise(fn, [(h, DFF, 0), (h, DFF, 1)], [], [(DFF, _ACT)], [], name=name)[0]


def _swiglu_act_bwd(h, ds, name):
    def fn(tv, fv):
        s, vjp = jax.vjp(lambda a, g: _silu(a) * g, tv[0], tv[1])
        da, dg = vjp(tv[2])
        return [[da, dg], s], []
    return _rowwise(fn, [(h, DFF, 0), (h, DFF, 1), ds], [], [(2 * DFF, _ACT), (DFF, _ACT)], [], name=name)


def _loss_fwd_bwd(y, tgt, name):
    def fn(tv, fv):
        e = tv[0] - tv[1]
        row = jnp.sum(e * e, axis=1, keepdims=True)
        tot = jnp.sum(row, axis=0, keepdims=True) * (0.5 / D)
        return [e * (1.0 / D)], [jnp.broadcast_to(tot, (1, 128))]
    return _rowwise(fn, [y, tgt], [], [(D, F32)], [(1, 128)], name=name)


def _shift_down(x, k, row):
    return jnp.where(row >= k, pltpu.roll(x, k, axis=0), 0.0)


def _shift_up(x, k, row):
    n = x.shape[0]
    return jnp.where(row < n - k, pltpu.roll(x, n - k, axis=0), 0.0)


def _pool_window_masks(j):
    lane = lax.broadcasted_iota(jnp.int32, (1, 128), 1) + j * 128
    grp = lane // POOL_GDIM
    return [grp == g for g in range(4)]


def _pool_mean(u, bwd, name, col0=0):
    T = u.shape[0]
    B = T // S

    def body(u_ref, o_ref):
        j = pl.program_id(1)
        x = u_ref[...]
        row = lax.broadcasted_iota(jnp.int32, (S, 1), 0)
        masks = _pool_window_masks(j)
        inv = [1.0 / jnp.minimum(row + 1, w).astype(F32) for w in POOL_WINDOWS]
        if not bwd:
            s2 = x + _shift_down(x, 1, row)
            s4 = s2 + _shift_down(s2, 2, row)
            s8 = s4 + _shift_down(s4, 4, row)
            s16 = s8 + _shift_down(s8, 8, row)
            mean = jnp.where(masks[0], s2 * inv[0], jnp.where(masks[1], s4 * inv[1],
                             jnp.where(masks[2], s8 * inv[2], s16 * inv[3])))
            o_ref[...] = (mean - x).astype(o_ref.dtype)
        else:
            g = [jnp.where(masks[i], x * inv[i], 0.0) for i in range(4)]
            t = g[3]
            t = t + _shift_up(t, 8, row) + g[2]
            t = t + _shift_up(t, 4, row) + g[1]
            t = t + _shift_up(t, 2, row) + g[0]
            t = t + _shift_up(t, 1, row)
            o_ref[...] = (t - x).astype(o_ref.dtype)

    spec = pl.BlockSpec((S, 128), lambda b, j: (b, j))
    return _pcall(body, name=name, grid=(B, POOLW // 128),
                  in_specs=[pl.BlockSpec((S, 128), lambda b, j: (b, j + col0))], out_specs=spec,
                  out_shape=jax.ShapeDtypeStruct((T, POOLW), _ACT), compiler_params=_cp(("parallel", "parallel")))(u)


def _conv_silu(xbc, w, b, name, col0=0):
    T, C = xbc.shape[0], w.shape[1]
    B = T // S

    def body(x_ref, w_ref, b_ref, o_ref):
        x = x_ref[...]
        row = lax.broadcasted_iota(jnp.int32, (S, 1), 0)
        c = b_ref[...] + w_ref[3:4, :] * x
        for s in range(1, 4):
            c = c + w_ref[3 - s:4 - s, :] * _shift_down(x, s, row)
        o_ref[...] = _silu(c)

    return _pcall(body, name=name, grid=(B, C // 128),
                  in_specs=[pl.BlockSpec((S, 128), lambda b, j: (b, j + col0)), pl.BlockSpec((4, 128), lambda b, j: (0, j)),
                            pl.BlockSpec((1, 128), lambda b, j: (0, j))],
                  out_specs=pl.BlockSpec((S, 128), lambda b, j: (b, j)),
                  out_shape=jax.ShapeDtypeStruct((T, C), F32), compiler_params=_cp(("parallel", "parallel")))(xbc, w, b)


def _conv_silu_bwd(xbc, w, b, dact, name, col0=0):
    T, C = xbc.shape[0], w.shape[1]
    B = T // S

    def body(x_ref, w_ref, b_ref, d_ref, dx_ref, dw_ref, db_ref):
        bi = pl.program_id(1)
        x = x_ref[...]
        row = lax.broadcasted_iota(jnp.int32, (S, 1), 0)
        xs = [x] + [_shift_down(x, s, row) for s in range(1, 4)]
        c = b_ref[...]
        for s in range(4):
            c = c + w_ref[3 - s:4 - s, :] * xs[s]
        _, vjp = jax.vjp(_silu, c)
        dc = vjp(d_ref[...])[0]
        dx = w_ref[3:4, :] * dc
        for s in range(1, 4):
            dx = dx + w_ref[3 - s:4 - s, :] * _shift_up(dc, s, row)
        dx_ref[...] = dx.astype(dx_ref.dtype)
        first = bi == 0
        for s in range(4):
            _acc_rows(dw_ref, 3 - s, jnp.sum(dc * xs[s], axis=0, keepdims=True), first)
        _acc_rows(db_ref, 0, jnp.sum(dc, axis=0, keepdims=True), first)

    blk = pl.BlockSpec((S, 128), lambda j, b: (b, j))
    return _pcall(body, name=name, grid=(C // 128, B),
                  in_specs=[pl.BlockSpec((S, 128), lambda j, b: (b, j + col0)), pl.BlockSpec((4, 128), lambda j, b: (0, j)),
                            pl.BlockSpec((1, 128), lambda j, b: (0, j)), blk],
                  out_specs=[blk, pl.BlockSpec((4, 128), lambda j, b: (0, j)), pl.BlockSpec((1, 128), lambda j, b: (0, j))],
                  out_shape=[jax.ShapeDtypeStruct((T, C), _ACT), jax.ShapeDtypeStruct((4, C), F32),
                             jax.ShapeDtypeStruct((1, C), F32)],
                  compiler_params=_cp(("parallel", "arbitrary")))(xbc, w, b, dact)


def _acc_rows(ref, r, val, first):
    @pl.when(first)
    def _():
        ref[r:r + 1, :] = val

    @pl.when(jnp.logical_not(first))
    def _():
        ref[r:r + 1, :] += val


def _tri_consts():
    i = lax.broadcasted_iota(jnp.int32, (CH, CH), 0)
    j = lax.broadcasted_iota(jnp.int32, (CH, CH), 1)
    return (i == j).astype(F32), (j <= i).astype(F32), (i <= j).astype(F32), i >= j


def _ssd_chunk(h, x, dt, Bm, Cm, a, dsk, consts):
    eye, tril, triu, lower = consts
    Bb = Bm.astype(_MXU)
    Cb = Cm.astype(_MXU)
    cb = lax.dot_general(Cb, Bb, (((1,), (1,)), ((), ())), preferred_element_type=F32)
    ys, hn = [], []
    for e in range(4):
        adt = dt[e] * a[e]
        adt_row = jnp.sum(adt * eye, axis=0, keepdims=True)
        cs_col = jnp.sum(adt_row * tril, axis=1, keepdims=True)
        cs_row = jnp.sum(adt * triu, axis=0, keepdims=True)
        cs_last = jnp.sum(adt, axis=0, keepdims=True)
        decay = jnp.exp(jnp.where(lower, cs_col - cs_row, -jnp.inf))
        xb = (x[e] * dt[e]).astype(_MXU)
        y_diag = jnp.dot((cb * decay).astype(_MXU), xb, preferred_element_type=F32)
        bdec = (Bm * jnp.exp(cs_last - cs_col)).astype(_MXU)
        st = lax.dot_general(bdec, xb, (((0,), (0,)), ((), ())), preferred_element_type=F32)
        hn.append(h[e] * jnp.exp(cs_last) + st)
        y_off = jnp.exp(cs_col) * jnp.dot(Cb, h[e].astype(_MXU), preferred_element_type=F32)
        ys.append(y_diag + y_off + dsk[e] * x[e])
    return ys, hn


def _ssd_specs(order):
    def im(f):
        return lambda p, q: f(*order(p, q))
    xs = pl.BlockSpec((S, 256), im(lambda b, g: (b, g)))
    dt = pl.BlockSpec((None, S, 4), im(lambda b, g: (g, b, 0)))
    bc = pl.BlockSpec((S, 128), im(lambda b, g: (b, g)))
    hd = pl.BlockSpec((None, 1, 4), im(lambda b, g: (g, 0, 0)))
    hs = pl.BlockSpec((None, None, S // CH, 4, 128, 64), im(lambda b, g: (b, g, 0, 0, 0, 0)))
    bw = pl.BlockSpec((S, 128), im(lambda b, g: (b, 8 + g)))
    cw = pl.BlockSpec((S, 128), im(lambda b, g: (b, 12 + g)))
    return xs, dt, bc, hd, hs, bw, cw


def _ssd_fwd(act, dtg, a, dsk, name):
    xs = bm = cm = act
    T = xs.shape[0]
    B = T // S
    nc = S // CH

    def body(x_ref, dt_ref, b_ref, c_ref, a_ref, k_ref, y_ref, hs_ref, h_ref):
        consts = _tri_consts()
        h_ref[...] = jnp.zeros_like(h_ref)
        al = [a_ref[:, e:e + 1] for e in range(4)]
        kl = [k_ref[:, e:e + 1] for e in range(4)]

        def step(c, carry):
            r0 = pl.multiple_of(c * CH, CH)
            rows = pl.ds(r0, CH)
            h = [h_ref[e] for e in range(4)]
            for e in range(4):
                hs_ref[c, e] = h[e]
            x = [x_ref[rows, 64 * e:64 * e + 64] for e in range(4)]
            dt = [dt_ref[rows, e:e + 1] for e in range(4)]
            ys, hn = _ssd_chunk(h, x, dt, b_ref[rows, :], c_ref[rows, :], al, kl, consts)
            for e in range(4):
                y_ref[rows, 64 * e:64 * e + 64] = ys[e]
                h_ref[e] = hn[e]
            return carry

        lax.fori_loop(0, nc, step, 0)

    sx, sdt, sbc, shd, shs, sbw, scw = _ssd_specs(lambda b, g: (b, g))
    return _pcall(body, name=name, grid=(B, 4), in_specs=[sx, sdt, sbw, scw, shd, shd], out_specs=[sx, shs],
                  out_shape=[jax.ShapeDtypeStruct((T, 1024), F32), jax.ShapeDtypeStruct((B, 4, nc, 4, 128, 64), F32)],
                  scratch_shapes=[pltpu.VMEM((4, 128, 64), F32)],
                  compiler_params=_cp(("parallel", "parallel")))(xs, dtg, bm, cm, a, dsk)


def _lane_place(vals, width):
    lane = lax.broadcasted_iota(jnp.int32, (1, width), 1)
    out = jnp.zeros((1, width), F32)
    for e, v in enumerate(vals):
        out = out + jnp.where(lane == e, v, 0.0)
    return out


def _ssd_bwd(act, dtg, a, dsk, hs, dy, name):
    xs = bm = cm = act
    T = xs.shape[0]
    B = T // S
    nc = S // CH

    def body(x_ref, dt_ref, b_ref, c_ref, a_ref, k_ref, hs_ref, dy_ref,
             dx_ref, ddt_ref, db_ref, dc_ref, dak_ref, dh_ref, sc_ref):
        bi = pl.program_id(1)
        consts = _tri_consts()
        dh_ref[...] = jnp.zeros_like(dh_ref)
        sc_ref[...] = jnp.zeros_like(sc_ref)
        al = [a_ref[:, e:e + 1] for e in range(4)]
        kl = [k_ref[:, e:e + 1] for e in range(4)]

        def step(i, carry):
            c = nc - 1 - i
            r0 = pl.multiple_of(c * CH, CH)
            rows = pl.ds(r0, CH)
            h = [hs_ref[c, e] for e in range(4)]
            x = [x_ref[rows, 64 * e:64 * e + 64] for e in range(4)]
            dt = [dt_ref[rows, e:e + 1] for e in range(4)]
            f = functools.partial(_ssd_chunk, consts=consts)
            _, vjp = jax.vjp(f, h, x, dt, b_ref[rows, :], c_ref[rows, :], al, kl)
            dys = [dy_ref[rows, 64 * e:64 * e + 64] for e in range(4)]
            dhn = [dh_ref[e] for e in range(4)]
            dh, dx, ddt, dB, dC, da, dk = vjp((dys, dhn))
            for e in range(4):
                dh_ref[e] = dh[e]
                dx_ref[rows, 64 * e:64 * e + 64] = dx[e]
                ddt_ref[rows, e:e + 1] = ddt[e]
            db_ref[rows, :] = dB
            dc_ref[rows, :] = dC
            sc_ref[0:1, :] += _lane_place(da, 128)
            sc_ref[1:2, :] += _lane_place(dk, 128)
            return carry

        lax.fori_loop(0, nc, step, 0)
        first = bi == 0

        @pl.when(first)
        def _():
            dak_ref[...] = sc_ref[...]

        @pl.when(jnp.logical_not(first))
        def _():
            dak_ref[...] += sc_ref[...]

    sx, sdt, sbc, shd, shs, sbw, scw = _ssd_specs(lambda g, b: (b, g))
    return _pcall(body, name=name, grid=(4, B), in_specs=[sx, sdt, sbw, scw, shd, shd, shs, sx],
                  out_specs=[sx, sdt, sbc, sbc, pl.BlockSpec((None, 8, 128), lambda g, b: (g, 0, 0))],
                  out_shape=[jax.ShapeDtypeStruct((T, 1024), F32), jax.ShapeDtypeStruct((4, T, 4), F32),
                             jax.ShapeDtypeStruct((T, 512), F32), jax.ShapeDtypeStruct((T, 512), F32),
                             jax.ShapeDtypeStruct((4, 8, 128), F32)],
                  scratch_shapes=[pltpu.VMEM((4, 128, 64), F32), pltpu.VMEM((8, 128), F32)],
                  compiler_params=_cp(("parallel", "arbitrary")))(xs, dtg, bm, cm, a, dsk, hs, dy)


def _gate_norm(y, z, nw):
    t = y * _silu(z)
    return t * lax.rsqrt(jnp.mean(t * t, axis=-1, keepdims=True) + SSD_EPS) * nw


def _ssd_gate_norm(y, z, nw, name, zcol=0):
    def fn(tv, fv):
        return [[_gate_norm(tv[g], tv[4 + g], fv[0][:, 256 * g:256 * g + 256]) for g in range(4)]], []
    tiled = [(y, 256, g) for g in range(4)] + [(z, 256, zcol + g) for g in range(4)]
    return _rowwise(fn, tiled, [nw], [(1024, _ACT)], [], name=name)[0]


def _ssd_gate_norm_bwd(y, z, nw, dout, name, zcol=0):
    def fn(tv, fv):
        dys, dzs, dns = [], [], []
        for g in range(4):
            _, vjp = jax.vjp(_gate_norm, tv[g], tv[4 + g], fv[0][:, 256 * g:256 * g + 256])
            a, b, c = vjp(tv[8 + g])
            dys.append(a)
            dzs.append(b)
            dns.append(c)
        return [dys, dzs], [dns]
    tiled = [(y, 256, g) for g in range(4)] + [(z, 256, zcol + g) for g in range(4)] + [(dout, 256, g) for g in range(4)]
    return _rowwise(fn, tiled, [nw], [(1024, F32), (1024, _ACT)], [(1, 1024)], name=name)


def _t5_bucket_np(dist):
    dist = np.maximum(dist, 0)
    max_exact = 16
    large = max_exact + (np.log(np.maximum(dist, 1) / max_exact) / np.log(2048 / max_exact) * (32 - max_exact)).astype(np.int32)
    large = np.minimum(large, 31)
    return np.where(dist < max_exact, dist, large).astype(np.int32)


def _bucket_maps():
    qi = np.arange(128)[:, None]
    kj = np.arange(256)[None, :]
    return np.stack([_t5_bucket_np((qi - kj + 128) * dil) for dil in ATTN_DILS]).astype(np.int32)


def _bias_build(rel_bias, maps, name):
    def body(tab_ref, map_ref, o_ref):
        hh = pl.program_id(0)
        m = map_ref[...]
        acc = jnp.zeros((128, 256), F32)
        for b in range(32):
            acc = jnp.where(m == b, tab_ref[b, hh], acc)
        o_ref[...] = acc

    return _pcall(body, name=name, grid=(12,),
                  in_specs=[pl.BlockSpec(memory_space=pltpu.SMEM), pl.BlockSpec((None, 128, 256), lambda h: (h // 4, 0, 0))],
                  out_specs=pl.BlockSpec((None, 128, 256), lambda h: (h, 0, 0)),
                  out_shape=jax.ShapeDtypeStruct((12, 128, 256), F32), compiler_params=_cp(("parallel",)))(rel_bias, maps)


def _bias_reduce(dbias, maps, name):
    nl = dbias.shape[0]

    def body(d_ref, map_ref, o_ref):
        m = map_ref[...]
        d = d_ref[0]
        for i in range(1, nl):
            d = d + d_ref[i]
        lane = lax.broadcasted_iota(jnp.int32, (1, 128), 1)
        out = jnp.zeros((1, 128), F32)
        for b in range(32):
            s = jnp.sum(jnp.sum(jnp.where(m == b, d, 0.0), axis=1, keepdims=True), axis=0, keepdims=True)
            out = out + jnp.where(lane == b, s, 0.0)
        o_ref[...] = out

    return _pcall(body, name=name, grid=(12,),
                  in_specs=[pl.BlockSpec((nl, None, 128, 256), lambda h: (0, h, 0, 0)),
                            pl.BlockSpec((None, 128, 256), lambda h: (h // 4, 0, 0))],
                  out_specs=pl.BlockSpec((None, 1, 128), lambda h: (h, 0, 0)),
                  out_shape=jax.ShapeDtypeStruct((12, 1, 128), F32), compiler_params=_cp(("parallel",)))(dbias, maps)


def _attn_block(q, kb, vb, bias, mask):
    s = lax.dot_general(q.astype(_MXU), kb.astype(_MXU), (((1,), (1,)), ((), ())), preferred_element_type=F32) * 0.125 + bias
    s = jnp.where(mask, s, -jnp.inf)
    m = lax.stop_gradient(jnp.max(s, axis=-1, keepdims=True))
    p = jnp.exp(s - m)
    den = jnp.sum(p, axis=-1, keepdims=True)
    out = jnp.dot((p / den).astype(_MXU), vb.astype(_MXU), preferred_element_type=F32)
    return out, m + jnp.log(den)


def _band_mask():
    qi = lax.broadcasted_iota(jnp.int32, (128, 256), 0)
    kj = lax.broadcasted_iota(jnp.int32, (128, 256), 1)
    return (kj >= qi) & (kj <= qi + 128)


def _qkv_specs(gi, order):
    def spec(base):
        col = (base + 256 * gi) // 128
        return pl.BlockSpec((S, 128), lambda p, q: (order(p, q)[0], col + order(p, q)[1]))
    return [spec(O_Q), spec(O_K), spec(O_V)]


def _residue_rows(r, dil):
    return pl.ds(r, S // dil, stride=dil)


def _attn_fwd(hcat, bias_all, gi, name):
    dil = ATTN_DILS[gi]
    T = hcat.shape[0]
    B, L = T // S, S // dil
    nb = L // 128

    def body(q_ref, k_ref, v_ref, b_ref, o_ref, l_ref, *scr):
        mask = _band_mask()
        if dil > 1:
            qd, kd, vd, od, ld = scr
            for r in range(dil):
                rows = _residue_rows(r, dil)
                qd[r] = q_ref[rows, :]
                kd[r] = k_ref[rows, :]
                vd[r] = v_ref[rows, :]
            ld[...] = jnp.zeros_like(ld)
            rd = lambda ref, sref, r, rows, lanes: sref[r, rows, lanes]
        else:
            qd = kd = vd = od = ld = None
            l_ref[...] = jnp.zeros_like(l_ref)
            rd = lambda ref, sref, r, rows, lanes: ref[rows, lanes]

        def put(r, rows, e, o, l):
            if dil > 1:
                od[r, rows, 64 * e:64 * e + 64] = o
                ld[r, rows, e:e + 1] = l
            else:
                o_ref[rows, 64 * e:64 * e + 64] = o
                l_ref[rows, e:e + 1] = l

        for e in range(2):
            lanes = slice(64 * e, 64 * e + 64)
            bias_v = b_ref[e]

            def unit(r, carry, lanes=lanes, bias_v=bias_v, e=e):
                first = pl.ds(0, 128)
                o, l = _attn_block(rd(q_ref, qd, r, first, lanes), rd(k_ref, kd, r, first, lanes),
                                   rd(v_ref, vd, r, first, lanes), bias_v[:, 128:], mask[:, 128:])
                put(r, first, e, o, l)
                if nb > 1:
                    def step(n, c):
                        cur = pl.ds(pl.multiple_of(n * 128, 128), 128)
                        band = pl.ds(pl.multiple_of(n * 128 - 128, 128), 256)
                        o, l = _attn_block(rd(q_ref, qd, r, cur, lanes), rd(k_ref, kd, r, band, lanes),
                                           rd(v_ref, vd, r, band, lanes), bias_v, mask)
                        put(r, cur, e, o, l)
                        return c
                    lax.fori_loop(1, nb, step, 0)
                return carry

            if dil > 1:
                lax.fori_loop(0, dil, unit, 0)
            else:
                unit(0, 0)
        if dil > 1:
            for r in range(dil):
                rows = _residue_rows(r, dil)
                o_ref[rows, :] = od[r]
                l_ref[rows, :] = ld[r]

    scratch = [pltpu.VMEM((dil, L, 128), F32)] * 5 if dil > 1 else []
    return _pcall(body, name=name, grid=(B, 2),
                  in_specs=_qkv_specs(gi, lambda b, hp: (b, hp))
                  + [pl.BlockSpec((2, 128, 256), lambda b, hp: (2 * gi + hp, 0, 0))],
                  out_specs=[pl.BlockSpec((S, 128), lambda b, hp: (b, hp)),
                             pl.BlockSpec((None, S, 128), lambda b, hp: (hp, b, 0))],
                  out_shape=[jax.ShapeDtypeStruct((T, 256), F32), jax.ShapeDtypeStruct((2, T, 128), F32)],
                  scratch_shapes=scratch,
                  compiler_params=_cp(("parallel", "parallel")))(hcat, hcat, hcat, bias_all)


def _attn_bwd(hcat, bias_all, gi, do, dl, name):
    dil = ATTN_DILS[gi]
    T = hcat.shape[0]
    B, L = T // S, S // dil
    nb = L // 128

    def body(q_ref, k_ref, v_ref, b_ref, do_ref, dl_ref, dq_ref, dk_ref, dv_ref, db_ref, acc_ref, *scr):
        bi = pl.program_id(1)
        mask = _band_mask()
        if dil > 1:
            qd, kd, vd, dod, dld, dqd, dkd, dvd = scr
            for r in range(dil):
                rows = _residue_rows(r, dil)
                qd[r] = q_ref[rows, :]
                kd[r] = k_ref[rows, :]
                vd[r] = v_ref[rows, :]
                dod[r] = do_ref[rows, :]
                dld[r] = dl_ref[rows, :]
            srcs = (qd, kd, vd, dod, dld)
            dsts = (dqd, dkd, dvd)
            rd = lambda i, r, rows, lanes: srcs[i][r, rows, lanes]
        else:
            srcs = (q_ref, k_ref, v_ref, do_ref, dl_ref)
            dsts = (dq_ref, dk_ref, dv_ref)
            rd = lambda i, r, rows, lanes: srcs[i][rows, lanes]
        dsts[1][...] = jnp.zeros_like(dsts[1])
        dsts[2][...] = jnp.zeros_like(dsts[2])

        def at(i, r, rows, lanes):
            return (r, rows, lanes) if dil > 1 else (rows, lanes)

        for e in range(2):
            lanes = slice(64 * e, 64 * e + 64)
            one = slice(e, e + 1)
            bias_v = b_ref[e]
            acc_ref[...] = jnp.zeros_like(acc_ref)

            def unit(r, carry, lanes=lanes, one=one, bias_v=bias_v):
                first = pl.ds(0, 128)
                f0 = functools.partial(_attn_block, mask=mask[:, 128:])
                _, vjp = jax.vjp(f0, rd(0, r, first, lanes), rd(1, r, first, lanes), rd(2, r, first, lanes),
                                 bias_v[:, 128:])
                dq, dkb, dvb, dbs = vjp((rd(3, r, first, lanes), rd(4, r, first, one)))
                dsts[0][at(0, r, first, lanes)] = dq
                dsts[1][at(1, r, first, lanes)] += dkb
                dsts[2][at(2, r, first, lanes)] += dvb
                acc_ref[:, 128:256] += dbs
                if nb > 1:
                    f1 = functools.partial(_attn_block, mask=mask)

                    def step(n, c):
                        cur = pl.ds(pl.multiple_of(n * 128, 128), 128)
                        band = pl.ds(pl.multiple_of(n * 128 - 128, 128), 256)
                        _, vjp = jax.vjp(f1, rd(0, r, cur, lanes), rd(1, r, band, lanes), rd(2, r, band, lanes), bias_v)
                        dq, dkb, dvb, dbs = vjp((rd(3, r, cur, lanes), rd(4, r, cur, one)))
                        dsts[0][at(0, r, cur, lanes)] = dq
                        dsts[1][at(1, r, band, lanes)] += dkb
                        dsts[2][at(2, r, band, lanes)] += dvb
                        acc_ref[...] += dbs
                        return c
                    lax.fori_loop(1, nb, step, 0)
                return carry

            if dil > 1:
                lax.fori_loop(0, dil, unit, 0)
            else:
                unit(0, 0)

            @pl.when(bi == 0)
            def _(e=e):
                db_ref[e] = acc_ref[...]

            @pl.when(bi > 0)
            def _(e=e):
                db_ref[e] += acc_ref[...]

        if dil > 1:
            for r in range(dil):
                rows = _residue_rows(r, dil)
                dq_ref[rows, :] = dqd[r]
                dk_ref[rows, :] = dkd[r]
                dv_ref[rows, :] = dvd[r]

    order = lambda hp, b: (b, hp)
    blk = pl.BlockSpec((S, 128), lambda hp, b: (b, hp))
    lblk = pl.BlockSpec((None, S, 128), lambda hp, b: (hp, b, 0))
    sds = jax.ShapeDtypeStruct((T, 256), F32)
    scratch = [pltpu.VMEM((128, 256), F32)] + ([pltpu.VMEM((dil, L, 128), F32)] * 8 if dil > 1 else [])
    return _pcall(body, name=name, grid=(2, B),
                  in_specs=_qkv_specs(gi, order) + [pl.BlockSpec((2, 128, 256), lambda hp, b: (2 * gi + hp, 0, 0)), blk, lblk],
                  out_specs=[blk, blk, blk, pl.BlockSpec((2, 128, 256), lambda hp, b: (hp, 0, 0))],
                  out_shape=[sds, sds, sds, jax.ShapeDtypeStruct((4, 128, 256), F32)],
                  scratch_shapes=scratch,
                  compiler_params=_cp(("parallel", "arbitrary")))(hcat, hcat, hcat, bias_all, do, dl)


def _lse_merge(o0, o1, o2, l0, l1, l2):
    m = lax.stop_gradient(jnp.maximum(jnp.maximum(l0, l1), l2))
    e0, e1, e2 = jnp.exp(l0 - m), jnp.exp(l1 - m), jnp.exp(l2 - m)
    den = e0 + e1 + e2
    return (e0 / den) * o0 + (e1 / den) * o1 + (e2 / den) * o2


def _attn_merge(outs, lses, dy, name):
    T = outs[0].shape[0]
    bwd = dy is not None
    tm = 512

    def body(*refs):
        o_refs, l_refs = refs[:3], refs[3:6]
        if bwd:
            for r in refs[10:13]:
                r[...] = jnp.zeros_like(r)
        for e in range(2):
            lanes = slice(64 * e, 64 * e + 64)
            vals = [r[:, lanes] for r in o_refs] + [r[:, e:e + 1] for r in l_refs]
            if not bwd:
                refs[6][:, lanes] = _lse_merge(*vals).astype(refs[6].dtype)
            else:
                _, vjp = jax.vjp(_lse_merge, *vals)
                g = vjp(refs[6][:, lanes])
                for r, v in zip(refs[7:10], g[:3]):
                    r[:, lanes] = v
                for r, v in zip(refs[10:13], g[3:]):
                    r[:, e:e + 1] = v

    blk = pl.BlockSpec((tm, 128), lambda i, hp: (i, hp))
    lblk = pl.BlockSpec((None, tm, 128), lambda i, hp: (hp, i, 0))
    lsd = jax.ShapeDtypeStruct((2, T, 128), F32)
    if not bwd:
        return _pcall(body, name=name, grid=(T // tm, 2), in_specs=[blk] * 3 + [lblk] * 3, out_specs=blk,
                      out_shape=jax.ShapeDtypeStruct((T, 256), F32),
                      compiler_params=_cp(("parallel", "parallel")))(*outs, *lses)
    return _pcall(body, name=name, grid=(T // tm, 2), in_specs=[blk] * 3 + [lblk] * 3 + [blk],
                  out_specs=[blk] * 3 + [lblk] * 3, out_shape=[jax.ShapeDtypeStruct((T, 256), F32)] * 3 + [lsd] * 3,
                  compiler_params=_cp(("parallel", "parallel")))(*outs, *lses, dy)


def _gmerge(g0, g1, g2, gb, ya, yb, yc):
    return (jax.nn.sigmoid(g0 + gb[:, 0:D]) * ya + jax.nn.sigmoid(g1 + gb[:, D:2 * D]) * yb
            + jax.nn.sigmoid(g2 + gb[:, 2 * D:3 * D]) * yc)


def _gated_merge(gates, gb, ya, yb, yc, name, gcol=0):
    def fn(tv, fv):
        return [_gmerge(tv[0], tv[1], tv[2], fv[0], tv[3], tv[4], tv[5])], []
    return _rowwise(fn, [(gates, D, gcol), (gates, D, gcol + 1), (gates, D, gcol + 2), ya, yb, yc], [gb], [(D, _ACT)], [],
                    name=name)[0]


def _gated_merge_bwd(gates, gb, ya, yb, yc, dm, name, gcol=0):
    def fn(tv, fv):
        _, vjp = jax.vjp(_gmerge, tv[0], tv[1], tv[2], fv[0], tv[3], tv[4], tv[5])
        d0, d1, d2, dgb, da, db, dc = vjp(tv[6])
        return [[d0, d1, d2], da, db, dc], [dgb]
    return _rowwise(fn, [(gates, D, gcol), (gates, D, gcol + 1), (gates, D, gcol + 2), ya, yb, yc, dm], [gb],
                    [(3 * D, _ACT), (D, _ACT), (D, _ACT), (D, _ACT)], [(1, 3 * D)], name=name)


def _pool_affine(t1, pb, ps, dout, name):
    if dout is None:
        def fn(tv, fv):
            return [(tv[0] + fv[0]) * fv[1]], []
        return _rowwise(fn, [t1], [pb, ps], [(POOLW, _ACT)], [], name=name)[0]

    def fnb(tv, fv):
        t2, vjp = jax.vjp(lambda t, b, s: (t + b) * s, tv[0], fv[0], fv[1])
        dt, db, dsc = vjp(tv[1])
        return [dt, t2], [db, dsc]
    return _rowwise(fnb, [t1, dout], [pb, ps], [(POOLW, _ACT), (POOLW, _ACT)], [(1, POOLW), (1, POOLW)], name=name)


def _dt_softplus(dt_raw, dt_bias, ddt, name):
    f = lambda r, b: _softplus(r + b)
    if ddt is None:
        def fn(tv, fv):
            return [f(tv[0], fv[0])], []
        return _rowwise(fn, [dt_raw], [dt_bias], [(16, F32)], [], name=name, tm=1024)[0]

    def fnb(tv, fv):
        _, vjp = jax.vjp(f, tv[0], fv[0])
        dr, db = vjp(tv[1])
        return [dr], [db]
    return _rowwise(fnb, [dt_raw, ddt], [dt_bias], [(16, F32)], [(1, 16)], name=name, tm=1024)


def _adamw(w, g, m, v, name):
    R, C = w.shape
    tm = _pick(R, (256, 128, 64, 32, 16, 8))
    c1 = 1.0 / (1.0 - ADAM_B1 ** ADAM_STEP)
    c2 = 1.0 / (1.0 - ADAM_B2 ** ADAM_STEP)

    def fn(tv, fv):
        wv, gv, mv, vv = tv
        mn = ADAM_B1 * mv + (1.0 - ADAM_B1) * gv
        vn = ADAM_B2 * vv + (1.0 - ADAM_B2) * (gv * gv)
        delta = -ADAM_LR * ((mn * c1) / (jnp.sqrt(vn * c2) + ADAM_EPS) + ADAM_WD * wv)
        return [delta, mn, vn], []
    return _rowwise(fn, [w, g, m, v], [], [(C, F32)] * 3, [], name=name, tm=tm)


def _ffn_fwd(x, xm, w13, w2, g, b, tag, dep=None):
    h = _mm(xm, w13, dep=dep, name=f"{tag}_h")
    s = _swiglu_act(h, name=f"{tag}_act")
    y = _mm(s, w2, name=f"{tag}_y")
    r, out, outm = _res_ln_fwd(x, y, g, b, 0.5, name=f"{tag}_ln")
    return out, outm, dict(x=xm, h=h, r=r)


def _ffn_bwd(dout, sv, w13, w2, g, b, tag, dep=None):
    dskip, dy, dg, db = _ln_bwd(sv['r'], g, b, dout, 0.5, name=f"{tag}_lnb")
    ds = _mm(dy, w2, tb=True, dep=dep, name=f"{tag}_ds")
    dh, s = _swiglu_act_bwd(sv['h'], ds, name=f"{tag}_actb")
    dw2 = _mm(s, dy, ta=True, name=f"{tag}_dw2")
    dw13 = _mm(sv['x'], dh, ta=True, name=f"{tag}_dw13")
    dx = _mm(dh, w13, tb=True, add=dskip, name=f"{tag}_dx")
    return dx, dict(w13=dw13, w2=dw2, g=dg, b=db)


def _mixer_fwd(x1, x1m, W, bias_all, tag):
    T = x1.shape[0]
    hcat = _mm(x1m, W['w_in_r'], name=f"{tag}_hcat")
    dt_raw = hcat[:, O_DT:O_DT + 16]
    pooled = _pool_mean(hcat, False, name=f"{tag}_pool", col0=O_U // 128)
    t1 = _mm(pooled, W['pool_wbd'], name=f"{tag}_pt1")
    t2 = _pool_affine(t1, W['pool_b'], W['pool_scale'], None, name=f"{tag}_paff")
    ya = _mm(t2, W['p_pool'], name=f"{tag}_ya")
    act = _conv_silu(hcat, W['conv_w'], W['conv_b'], name=f"{tag}_conv", col0=O_XBC // 128)
    dt = _dt_softplus(dt_raw, W['dt_bias'], None, name=f"{tag}_dt")
    dtg = dt.reshape(T, 4, 4).transpose(1, 0, 2)
    yscan, hs = _ssd_fwd(act, dtg, W['a_neg'], W['d_skip'], name=f"{tag}_ssd")
    ybn = _ssd_gate_norm(yscan, hcat, W['ssd_norm'], name=f"{tag}_gn", zcol=O_Z // 256)
    yb = _mm(ybn, W['p_ssd'], name=f"{tag}_yb")
    outs, lses = [], []
    for gi in range(len(ATTN_DILS)):
        o, l = _attn_fwd(hcat, bias_all, gi, name=f"{tag}_attn{gi}")
        outs.append(o)
        lses.append(l)
    ycp = _attn_merge(outs, lses, None, name=f"{tag}_amerge")
    yc = _mm(ycp, W['p_attn'], name=f"{tag}_yc")
    merged = _gated_merge(hcat, W['gate_b'], ya, yb, yc, name=f"{tag}_gm", gcol=O_G // D)
    mix = _mm(merged, W['w_out'], name=f"{tag}_mix")
    r, out, outm = _res_ln_fwd(x1, mix, W['ln2_g'], W['ln2_b'], 1.0, name=f"{tag}_ln")
    sv = dict(x1=x1m, dt_raw=dt_raw, pooled=pooled, t1=t1, act=act, dtg=dtg,
              hs=hs, yscan=yscan, ybn=ybn, hcat=hcat, outs=outs, lses=lses, ycp=ycp, ya=ya, yb=yb, yc=yc,
              merged=merged, r=r)
    return out, outm, sv


def _mixer_bwd(dout, sv, W, bias_all, tag):
    T = dout.shape[0]
    B = T // S
    gr = {}
    dx1a, dr, gr['ln2_g'], gr['ln2_b'] = _ln_bwd(sv['r'], W['ln2_g'], W['ln2_b'], dout, 1.0, name=f"{tag}_lnb")
    dmerged = _mm(dr, W['w_out'], tb=True, name=f"{tag}_dmerged")
    gr['w_out'] = _mm(sv['merged'], dr, ta=True, name=f"{tag}_dwout")
    dgates, dya, dyb, dyc, gr['gate_b'] = _gated_merge_bwd(sv['hcat'], W['gate_b'], sv['ya'], sv['yb'], sv['yc'],
                                                           dmerged, name=f"{tag}_gmb", gcol=O_G // D)
    dycp = _mm(dyc, W['p_attn'], tb=True, name=f"{tag}_dycp")
    gr['p_attn'] = _mm(sv['ycp'], dyc, ta=True, name=f"{tag}_dpattn")
    dml = _attn_merge(sv['outs'], sv['lses'], dycp, name=f"{tag}_amergeb")
    dq, dk, dv, dbias = [], [], [], []
    for gi in range(len(ATTN_DILS)):
        a, b, c, d = _attn_bwd(sv['hcat'], bias_all, gi, dml[gi], dml[3 + gi], name=f"{tag}_attnb{gi}")
        dq.append(a)
        dk.append(b)
        dv.append(c)
        dbias.append(d)
    dbias = jnp.concatenate(dbias, axis=0)
    dybn = _mm(dyb, W['p_ssd'], tb=True, name=f"{tag}_dybn")
    gr['p_ssd'] = _mm(sv['ybn'], dyb, ta=True, name=f"{tag}_dpssd")
    dyscan, dz, gr['ssd_norm'] = _ssd_gate_norm_bwd(sv['yscan'], sv['hcat'], W['ssd_norm'], dybn, name=f"{tag}_gnb",
                                                    zcol=O_Z // 256)
    dxs, ddtg, dbm, dcm, dak = _ssd_bwd(sv['act'], sv['dtg'], W['a_neg'], W['d_skip'], sv['hs'], dyscan,
                                        name=f"{tag}_ssdb")
    gr['a_neg'], gr['d_skip'] = dak[:, 0, 0:4], dak[:, 1, 0:4]
    ddt = ddtg.transpose(1, 0, 2).reshape(T, 16)
    ddt_raw, gr['dt_bias'] = _dt_softplus(sv['dt_raw'], W['dt_bias'], ddt, name=f"{tag}_dtb")
    dact = jnp.concatenate([dxs, dbm, dcm], axis=1)
    dxbc, gr['conv_w'], gr['conv_b'] = _conv_silu_bwd(sv['hcat'], W['conv_w'], W['conv_b'], dact, name=f"{tag}_convb",
                                                      col0=O_XBC // 128)
    dt2 = _mm(dya, W['p_pool'], tb=True, name=f"{tag}_dt2")
    dt1, t2, gr['pool_b'], gr['pool_scale'] = _pool_affine(sv['t1'], W['pool_b'], W['pool_scale'], dt2, name=f"{tag}_paffb")
    gr['p_pool'] = _mm(t2, dya, ta=True, name=f"{tag}_dppool")
    dpooled = _mm(dt1, W['pool_wbd'], tb=True, name=f"{tag}_dpooled")
    gr['pool_wbd'] = _mm(sv['pooled'], dt1, ta=True, name=f"{tag}_dpoolw")
    du = _pool_mean(dpooled, True, name=f"{tag}_poolb")
    dhcat = jnp.concatenate([t.astype(_ACT) for t in [du, dz, dxbc] + dq + dk + dv + [dgates, ddt_raw]]
                            + [jnp.zeros((T, HC - O_DT - 16), _ACT)], axis=1)
    dx1 = _mm(dhcat, W['w_in_r'], tb=True, add=dx1a, name=f"{tag}_dx1")
    gr['w_in_r'] = _mm(sv['x1'], dhcat, ta=True, name=f"{tag}_dwin")
    return dx1, gr, dbias


def _prep_layer_weights(i, inp, G):
    W = {}
    for n in BIG:
        g = G[n]
        if n == 'w_in':
            W['w_in_r'] = jnp.concatenate(_nat_pieces(g, 0, 3840) + _nat_pieces(g, 3856, 9232) + _nat_pieces(g, 3840, 3856)
                                          + [jnp.zeros((D, HC - 9232), g.dtype)], axis=1)
        elif n in COL_SHARDED:
            W[n] = jnp.concatenate([g[j] for j in range(4)], axis=1)
        else:
            W[n] = g.reshape(4 * g.shape[1], g.shape[2])
    pw = inp['pool_w'][i].astype(_MXU)
    wbd = jnp.zeros((POOLW, POOLW), _MXU)
    for g in range(4):
        wbd = lax.dynamic_update_slice(wbd, pw[g], (g * POOL_GDIM, g * POOL_GDIM))
    W['pool_wbd'] = wbd
    W['pool_b'] = inp['pool_b'][i].reshape(1, POOLW)
    W['pool_scale'] = inp['pool_scale'][i].reshape(1, POOLW)
    W['conv_w'] = jnp.concatenate([G['conv_w'][j] for j in range(4)], axis=1)
    W['conv_b'] = inp['conv_b'][i].reshape(1, 2048)
    W['dt_bias'] = inp['dt_bias'][i].reshape(1, 16)
    W['a_neg'] = (-jnp.exp(inp['a_log'][i])).reshape(4, 1, 4)
    W['d_skip'] = inp['d_skip'][i].reshape(4, 1, 4)
    W['ssd_norm'] = inp['ssd_norm'][i].reshape(1, D)
    W['gate_b'] = jnp.concatenate([G['gate_b'][j][b:b + 1] for b in range(3) for j in range(4)], axis=1)
    for n in ('ln1_g', 'ln1_b', 'ln2_g', 'ln2_b', 'ln3_g', 'ln3_b'):
        W[n] = inp[n][i].reshape(1, D)
    return W


def _gather_layer(i, inp):
    core = lax.axis_index("c")
    me = 2 * lax.axis_index("x") + lax.axis_index("y")
    halves = []
    for n in BIG:
        s = inp[n][i]
        halves.append(lax.dynamic_slice_in_dim(s, core * (s.shape[0] // 2), s.shape[0] // 2, axis=0).astype(BF16))
    small = [inp['gate_b'][i], inp['conv_w'][i]]
    return _exchange_start(halves + small, "gather", name="gather_start")


def _gather_finish(state, after):
    core = lax.axis_index("c")
    me = 2 * lax.axis_index("x") + lax.axis_index("y")
    own, outs = _exchange_wait(state, after, "gather", name="gather_wait")
    outs = [lax.dynamic_update_slice(o, a[None], (me, 0, 0)) for o, a in zip(outs, own)]
    mine = outs[:len(BIG)]
    theirs = _exchange(mine, "cores", "gather", name="gather_share")
    G = {}
    for n, a, b in zip(BIG, mine, theirs):
        G[n] = jnp.concatenate([jnp.where(core == 0, a, b), jnp.where(core == 0, b, a)], axis=1)
    G['gate_b'], G['conv_w'] = outs[len(BIG)], outs[len(BIG) + 1]
    return G


W_IN_SHARD = 2308


def _nat_pieces(g, lo, hi):
    out = []
    for j in range(4):
        s, e = max(lo, W_IN_SHARD * j), min(hi, W_IN_SHARD * (j + 1))
        if s < e:
            out.append(g[j][:, s - W_IN_SHARD * j:e - W_IN_SHARD * j])
    return out


def _reord_ranges(lo, hi):
    out = []
    for a, b, off in ((0, 3840, 0), (3840, 3856, O_DT - 3840), (3856, 9232, -16)):
        s, e = max(lo, a), min(hi, b)
        if s < e:
            out.append((s + off, e + off))
    return out


def _halves_of(n, g):
    if n == 'w_in':
        shards = [jnp.concatenate([g[:, a:b] for a, b in _reord_ranges(W_IN_SHARD * j, W_IN_SHARD * (j + 1))], axis=1)
                  for j in range(4)]
    elif n in COL_SHARDED:
        c = g.shape[1] // 4
        shards = [g[:, j * c:(j + 1) * c] for j in range(4)]
    else:
        r = g.shape[0] // 4
        shards = [g[j * r:(j + 1) * r] for j in range(4)]
    r2 = shards[0].shape[0] // 2
    return jnp.stack([jnp.concatenate([s[h * r2:(h + 1) * r2] for s in shards], axis=0) for h in range(2)])


def _reduce_start(grads):
    names = list(grads)
    halves = [_halves_of(n, grads[n]) for n in names]
    core = lax.axis_index("c").reshape(1)
    got = _exchange(halves, "cores", "scatter", name="rs_cores")
    chip = [_sum_own_recv(h, t, core, BF16, name="rs_sum2") for h, t in zip(halves, got)]
    chip = [t.reshape(4, t.shape[0] // 4, t.shape[1]) for t in chip]
    state, token = _exchange_start(chip, "scatter", name="rs_start")
    return (names, state), token


def _reduce_finish(handle, after):
    names, state = handle
    chip_id = (2 * lax.axis_index("x") + lax.axis_index("y")).reshape(1)
    chip, got = _exchange_wait(state, after, "scatter", name="rs_wait")
    red = [_sum_own_recv(h, t, chip_id, F32, name="rs_sum4") for h, t in zip(chip, got)]
    other = _exchange(red, "cores", "gather", name="rs_share")
    out = {}
    for n, mine, theirs in zip(names, red, other):
        out[n] = jnp.where(lax.axis_index("c") == 0, jnp.concatenate([mine, theirs]), jnp.concatenate([theirs, mine]))
    return out


class _Comm:
    def __init__(self, inp):
        self.inp = inp

    def gather_start(self, i):
        return _gather_layer(i, self.inp)

    def gather_finish(self, state, after):
        return _gather_finish(state, after)

    def reduce_start(self, i, grads):
        return _reduce_start({n: grads[n] for n in BIG})

    def reduce_finish(self, handle, after):
        return _reduce_finish(handle, after)


def _allreduce_small(vec):
    for group in ("cores", "x", "y"):
        recv = _exchange([vec], group, "gather", name=f"ar_{group}")[0]
        vec = _rowwise(lambda tv, fv: ([tv[0] + tv[1]], []), [vec, recv], [], [(128, F32)], [], name=f"ar_add_{group}")[0]
    return vec


def _pack(arrs):
    flat = jnp.concatenate([a.reshape(-1) for a in arrs])
    n = flat.shape[0]
    pad = (-n) % (256 * 128)
    flat = jnp.concatenate([flat, jnp.zeros((pad,), F32)])
    return flat.reshape(-1, 128)


def _unpack(p, shapes):
    flat = p.reshape(-1)
    out, off = [], 0
    for s in shapes:
        sz = int(np.prod(s))
        out.append(flat[off:off + sz].reshape(s))
        off += sz
    return out


def _forward_backward(inp, comm, bias_all):
    x = xm = inp['x'].reshape(-1, D)
    tgt = inp['loss_target'].reshape(-1, D)
    saved, Ws = [], []
    state, _ = comm.gather_start(0)
    G = comm.gather_finish(state, x)
    for i in range(NL):
        W = _prep_layer_weights(i, inp, G)
        dep = None
        if i + 1 < NL:
            state, dep = comm.gather_start(i + 1)
        x1, x1m, s1 = _ffn_fwd(x, xm, W['ffn1_w13'], W['ffn1_w2'], W['ln1_g'], W['ln1_b'], "f1", dep)
        x2, x2m, s2 = _mixer_fwd(x1, x1m, W, bias_all, "mx")
        x, xm, s3 = _ffn_fwd(x2, x2m, W['ffn2_w13'], W['ffn2_w2'], W['ln3_g'], W['ln3_b'], "f2")
        if i + 1 < NL:
            G = comm.gather_finish(state, xm)
        saved.append((s1, s2, s3))
        Ws.append(W)
    dy, lpart = _loss_fwd_bwd(x, tgt, name="loss")
    fins, reduced, dbiases = [None] * NL, [None] * NL, [None] * NL
    pending, dep = None, None
    for i in reversed(range(NL)):
        W = Ws[i]
        s1, s2, s3 = saved[i]
        g = {}
        dx2, f = _ffn_bwd(dy, s3, W['ffn2_w13'], W['ffn2_w2'], W['ln3_g'], W['ln3_b'], "f2", dep)
        g['ffn2_w13'], g['ffn2_w2'], g['ln3_g'], g['ln3_b'] = f['w13'], f['w2'], f['g'], f['b']
        dx1, gm, dbiases[i] = _mixer_bwd(dx2, s2, W, bias_all, "mx")
        g.update(gm)
        dy, f = _ffn_bwd(dx1, s1, W['ffn1_w13'], W['ffn1_w2'], W['ln1_g'], W['ln1_b'], "f1")
        g['ffn1_w13'], g['ffn1_w2'], g['ln1_g'], g['ln1_b'] = f['w13'], f['w2'], f['g'], f['b']
        fins[i] = _finish_layer_grads(i, g, inp)
        if pending is not None:
            reduced[pending[0]] = comm.reduce_finish(pending[1], dy)
        handle, dep = comm.reduce_start(i, fins[i])
        pending = (i, handle)
    return lpart, dy, fins, reduced, pending, dbiases


def _finish_layer_grads(i, g, inp):
    out = {n: g[n] for n in BIG if n != 'w_in'}
    out['w_in'] = g['w_in_r']
    out['pool_w'] = jnp.stack([g['pool_wbd'][k * POOL_GDIM:(k + 1) * POOL_GDIM, k * POOL_GDIM:(k + 1) * POOL_GDIM] for k in range(4)])
    out['pool_b'] = g['pool_b'].reshape(4, POOL_GDIM)
    out['pool_scale'] = g['pool_scale'].reshape(POOLW)
    out['conv_w'] = g['conv_w']
    out['conv_b'] = g['conv_b'].reshape(2048)
    out['dt_bias'] = g['dt_bias'].reshape(16)
    out['a_log'] = (g['a_neg'].reshape(16)) * (-jnp.exp(inp['a_log'][i]))
    out['d_skip'] = g['d_skip'].reshape(16)
    out['ssd_norm'] = g['ssd_norm'].reshape(D)
    out['gate_b'] = g['gate_b'].reshape(3, D)
    for n in ('ln1_g', 'ln1_b', 'ln2_g', 'ln2_b', 'ln3_g', 'ln3_b'):
        out[n] = g[n].reshape(D)
    return out


def kernel(x, ffn1_w13, ffn1_w2, ln1_g, ln1_b, w_in, gate_b, pool_w, pool_b, pool_scale, conv_w, conv_b,
           dt_bias, a_log, d_skip, ssd_norm, rel_bias, p_pool, p_ssd, p_attn, w_out, ln2_g, ln2_b, ffn2_w13,
           ffn2_w2, ln3_g, ln3_b, loss_target, m_ffn1_w13, m_ffn1_w2, m_ln1_g, m_ln1_b, m_w_in, m_gate_b,
           m_pool_w, m_pool_b, m_pool_scale, m_conv_w, m_conv_b, m_dt_bias, m_a_log, m_d_skip, m_ssd_norm,
           m_rel_bias, m_p_pool, m_p_ssd, m_p_attn, m_w_out, m_ln2_g, m_ln2_b, m_ffn2_w13, m_ffn2_w2, m_ln3_g,
           m_ln3_b, v_ffn1_w13, v_ffn1_w2, v_ln1_g, v_ln1_b, v_w_in, v_gate_b, v_pool_w, v_pool_b,
           v_pool_scale, v_conv_w, v_conv_b, v_dt_bias, v_a_log, v_d_skip, v_ssd_norm, v_rel_bias, v_p_pool,
           v_p_ssd, v_p_attn, v_w_out, v_ln2_g, v_ln2_b, v_ffn2_w13, v_ffn2_w2, v_ln3_g, v_ln3_b):
    inp = dict(locals())
    maps = jnp.asarray(_bucket_maps())
    bias_all = _bias_build(rel_bias, maps, name="bias_build")
    comm = _Comm(inp)
    lpart, gx, fins, red, pending, dbiases = _forward_backward(inp, comm, bias_all)
    loss = lax.psum(lpart[0, 0], ("x", "y", "c"))

    small_l = [n for n in SMALL if n != 'rel_bias']
    drel = _bias_reduce(jnp.stack(dbiases), maps, name="bias_reduce")[:, 0, :32].T
    small_arrs = [jnp.stack([fins[i][n] for i in range(NL)]) for n in small_l] + [drel]
    packed = _allreduce_small(_pack(small_arrs))
    gsmall = dict(zip(small_l + ['rel_bias'], _unpack(packed, [a.shape for a in small_arrs])))
    shard = 2 * lax.axis_index("x") + lax.axis_index("y")
    gsmall['gate_b'] = lax.dynamic_slice_in_dim(gsmall['gate_b'], shard * 256, 256, axis=2)
    gsmall['conv_w'] = lax.dynamic_slice_in_dim(gsmall['conv_w'], shard * 512, 512, axis=2)

    red[pending[0]] = comm.reduce_finish(pending[1], packed)
    gout = {n: jnp.stack([red[i][n] for i in range(NL)]) for n in BIG}
    gout.update(gsmall)

    delta, new_m, new_v = {}, {}, {}
    for n in BIG:
        shp = inp[n].shape
        two_d = lambda a: a.reshape(shp[0] * shp[1], shp[2])
        d, m, v = _adamw(two_d(inp[n]), two_d(gout[n]), two_d(inp['m_' + n]), two_d(inp['v_' + n]), name="adamw_big")
        delta[n], new_m[n], new_v[n] = d.reshape(shp), m.reshape(shp), v.reshape(shp)
    shapes = [inp[n].shape for n in SMALL]
    d, m, v = _adamw(_pack([inp[n] for n in SMALL]), _pack([gout[n] for n in SMALL]),
                     _pack([inp['m_' + n] for n in SMALL]), _pack([inp['v_' + n] for n in SMALL]), name="adamw_small")
    for n, dd, mm, vv in zip(SMALL, _unpack(d, shapes), _unpack(m, shapes), _unpack(v, shapes)):
        delta[n], new_m[n], new_v[n] = dd, mm, vv

    return (loss, gx.reshape(x.shape), *[gout[n] for n in WEIGHTS], *[delta[n] for n in WEIGHTS],
            *[new_m[n] for n in WEIGHTS], *[new_v[n] for n in WEIGHTS])
```

```python
import functools

import numpy as np
import jax
import jax.numpy as jnp
from jax import lax
from jax.experimental import pallas as pl
from jax.experimental.pallas import tpu as pltpu

F32 = jnp.float32
BF16 = jnp.bfloat16
_MXU = jnp.bfloat16
_ACT = jnp.bfloat16
_VMEM_LIMIT = 56 * 1024 * 1024

S = 2048
D = 1024
NL = 4
DFF = 2816
LN_EPS = 1e-5
SSD_EPS = 1e-5
ALPHA = (2.0 * NL) ** 0.25
POOLW = 768
POOL_WINDOWS = (2, 4, 8, 16)
POOL_GDIM = 192
CH = 128
ATTN_DILS = (1, 4, 16)
HC = 9728
O_U, O_Z, O_XBC, O_Q, O_K, O_V, O_G, O_DT = 0, 768, 1792, 3840, 4608, 5376, 6144, 9216

ADAM_LR, ADAM_B1, ADAM_B2, ADAM_EPS, ADAM_WD, ADAM_STEP = 0.001, 0.9, 0.999, 1e-08, 0.01, 10

WEIGHTS = ['ffn1_w13', 'ffn1_w2', 'ln1_g', 'ln1_b', 'w_in', 'gate_b', 'pool_w', 'pool_b', 'pool_scale', 'conv_w',
           'conv_b', 'dt_bias', 'a_log', 'd_skip', 'ssd_norm', 'rel_bias', 'p_pool', 'p_ssd', 'p_attn', 'w_out',
           'ln2_g', 'ln2_b', 'ffn2_w13', 'ffn2_w2', 'ln3_g', 'ln3_b']
BIG = ['ffn1_w13', 'ffn1_w2', 'w_in', 'p_pool', 'p_ssd', 'p_attn', 'w_out', 'ffn2_w13', 'ffn2_w2']
COL_SHARDED = {'ffn1_w13', 'ffn2_w13', 'w_in', 'p_pool', 'p_attn'}
SMALL = [n for n in WEIGHTS if n not in BIG]


def _pcall(body, **kw):
    return pl.pallas_call(body, **kw)


def _cp(sem=None):
    return pltpu.CompilerParams(dimension_semantics=sem, vmem_limit_bytes=_VMEM_LIMIT)


def _pick(n, cands):
    for c in cands:
        if n % c == 0:
            return c
    raise ValueError(f"no tile for {n}")


def _mm(a, b, *, ta=False, tb=False, add=None, out_dtype=F32, dep=None, name):
    if ta:
        K, M = a.shape
    else:
        M, K = a.shape
    if tb:
        N, K2 = b.shape
    else:
        K2, N = b.shape
    assert K == K2, (a.shape, b.shape, ta, tb)
    sa, sb, so = a.dtype.itemsize, b.dtype.itemsize, jnp.dtype(out_dtype).itemsize
    tm, tn, tk = _mm_tiles(M, N, K, sa, sb, so + (4 if add is not None else 0))
    nk = K // tk
    a_bytes, b_bytes = M * K * sa, K * N * sb
    j_outer = nk == 1 and (b_bytes + a_bytes * (N // tn) < a_bytes + b_bytes * (M // tm))
    ij = (lambda p, q: (q, p)) if j_outer else (lambda p, q: (p, q))

    def im(f):
        return lambda p, q, k: f(*ij(p, q), k)

    a_spec = pl.BlockSpec((tk, tm), im(lambda i, j, k: (k, i))) if ta else pl.BlockSpec((tm, tk), im(lambda i, j, k: (i, k)))
    b_spec = pl.BlockSpec((tn, tk), im(lambda i, j, k: (j, k))) if tb else pl.BlockSpec((tk, tn), im(lambda i, j, k: (k, j)))
    o_spec = pl.BlockSpec((tm, tn), im(lambda i, j, k: (i, j)))
    dims = (((0 if ta else 1,), (1 if tb else 0,)), ((), ()))
    has_add = add is not None

    n_in = 2 + int(has_add) + int(dep is not None)

    def body(*refs):
        a_ref, b_ref = refs[0], refs[1]
        add_ref = refs[2] if has_add else None
        o_ref = refs[n_in]
        part = lax.dot_general(a_ref[...].astype(_MXU), b_ref[...].astype(_MXU), dims, preferred_element_type=F32)

        def finish(r):
            if has_add:
                r = r + add_ref[...]
            o_ref[...] = r.astype(out_dtype)

        if nk == 1:
            finish(part)
        else:
            acc = refs[-1]
            k = pl.program_id(2)

            @pl.when(k == 0)
            def _():
                acc[...] = part

            @pl.when(k > 0)
            def _():
                acc[...] += part

            @pl.when(k == nk - 1)
            def _():
                finish(acc[...])

    in_specs = [a_spec, b_spec]
    args = [a, b]
    if has_add:
        in_specs.append(o_spec)
        args.append(add)
    if dep is not None:
        in_specs.append(pl.BlockSpec(memory_space=pl.ANY))
        args.append(dep)
    gm, gn = M // tm, N // tn
    return _pcall(
        body, name=name, grid=((gn, gm, nk) if j_outer else (gm, gn, nk)), in_specs=in_specs, out_specs=o_spec,
        out_shape=jax.ShapeDtypeStruct((M, N), out_dtype),
        scratch_shapes=([pltpu.VMEM((tm, tn), F32)] if nk > 1 else []),
        compiler_params=_cp(("parallel", "parallel", "arbitrary")),
    )(*args)


_MM_VMEM_BUDGET = 40 * 1024 * 1024


def _divisors128(n, cap):
    return [d for d in range(128, min(n, cap) + 1, 128) if n % d == 0][::-1]


def _mm_tiles(M, N, K, sa, sb, so):
    best = None
    for tm in _divisors128(M, 1024):
        for tn in _divisors128(N, 2560):
            for tk in ([K] if K <= 4096 else []) + _divisors128(K, 2560):
                nk = K // tk
                need = 2 * (tm * tk * sa + tk * tn * sb + tm * tn * so) + (tm * tn * 4 if nk > 1 else 0)
                need += tm * tk * 2 + tk * tn * 2 + tm * tn * 4
                if need > _MM_VMEM_BUDGET:
                    continue
                score = (tm * tn, tk)
                if best is None or score > best[0]:
                    best = (score, (tm, tn, tk))
                break
    assert best is not None, (M, N, K)
    return best[1]


def _store(ref, val):
    if isinstance(val, (list, tuple)):
        off = 0
        for p in val:
            w = p.shape[1]
            ref[:, off:off + w] = p.astype(ref.dtype)
            off += w
    else:
        ref[...] = val.astype(ref.dtype)


def _acc_store(ref, val, first):
    pieces = val if isinstance(val, (list, tuple)) else [val]
    off = 0
    for p in pieces:
        w = p.shape[1]

        @pl.when(first)
        def _(p=p, off=off, w=w):
            ref[:, off:off + w] = p

        @pl.when(jnp.logical_not(first))
        def _(p=p, off=off, w=w):
            ref[:, off:off + w] += p

        off += w


def _rowwise(fn, tiled, full, out_tiled, out_acc, *, name, tm=256):
    arrs, specs = [], []
    for t in tiled:
        arr, w, cb = t if isinstance(t, tuple) else (t, t.shape[1], 0)
        arrs.append(arr)
        specs.append(pl.BlockSpec((tm, w), functools.partial(lambda i, cb: (i, cb), cb=cb)))
    R = arrs[0].shape[0]
    assert R % tm == 0
    for f in full:
        arrs.append(f)
        specs.append(pl.BlockSpec(f.shape, functools.partial(lambda i, nd: (0,) * nd, nd=f.ndim)))
    nt, nf, no = len(tiled), len(full), len(out_tiled)

    def body(*refs):
        tv = [r[...] for r in refs[:nt]]
        fv = [r[...] for r in refs[nt:nt + nf]]
        ot, oa = fn(tv, fv)
        for r, v in zip(refs[nt + nf:nt + nf + no], ot):
            _store(r, v)
        first = pl.program_id(0) == 0
        for r, v in zip(refs[nt + nf + no:], oa):
            _acc_store(r, v, first)

    out_shape = [jax.ShapeDtypeStruct((R, c), dt) for c, dt in out_tiled]
    out_specs = [pl.BlockSpec((tm, c), lambda i: (i, 0)) for c, _ in out_tiled]
    for shp in out_acc:
        out_shape.append(jax.ShapeDtypeStruct(shp, F32))
        out_specs.append(pl.BlockSpec(shp, lambda i: (0, 0)))
    return _pcall(body, name=name, grid=(R // tm,), in_specs=specs, out_specs=out_specs, out_shape=out_shape,
                  compiler_params=_cp(("arbitrary",)))(*arrs)


def _group(group):
    x, y, c = lax.axis_index("x"), lax.axis_index("y"), lax.axis_index("c")
    if group == "chips":
        return 2 * x + y, [((x, 1 - y, c), 2 * x + 1 - y), ((1 - x, y, c), 2 * (1 - x) + y),
                           ((1 - x, 1 - y, c), 2 * (1 - x) + 1 - y)]
    if group == "cores":
        return c, [((x, y, 1 - c), 1 - c)]
    if group == "x":
        return x, [((1 - x, y, c), 1 - x)]
    return y, [((x, 1 - y, c), 1 - y)]


def _exchange(arrs, group, mode, name):
    chips = group == "chips"
    k = len(arrs)
    npeer = 3 if chips else 1

    def body(*refs):
        ins, outs = refs[:k], refs[k:2 * k]
        send_sems, recv_sems = refs[2 * k:]
        me, peers = _group(group)
        remote = []
        for i in range(k):
            for p, (dev, slot) in enumerate(peers):
                src = ins[i].at[slot] if mode == "scatter" else ins[i]
                if not chips:
                    dst = outs[i]
                else:
                    dst = outs[i].at[p] if mode == "scatter" else outs[i].at[me]
                cp = pltpu.make_async_remote_copy(src_ref=src, dst_ref=dst, send_sem=send_sems.at[i, p],
                                                  recv_sem=recv_sems.at[i, p], device_id=dev,
                                                  device_id_type=pl.DeviceIdType.MESH)
                cp.start()
                remote.append(cp)
        for cp in remote:
            cp.wait_recv()
        for cp in remote:
            cp.wait_send()

    def oshape(a):
        piece = a.shape[1:] if mode == "scatter" else a.shape
        if chips:
            piece = ((3,) if mode == "scatter" else (4,)) + piece
        return jax.ShapeDtypeStruct(piece, a.dtype)

    any_spec = pl.BlockSpec(memory_space=pl.ANY)
    return _pcall(body, name=name, in_specs=[any_spec] * k, out_specs=[any_spec] * k, out_shape=[oshape(a) for a in arrs],
                  scratch_shapes=[pltpu.SemaphoreType.DMA((k, npeer)), pltpu.SemaphoreType.DMA((k, npeer))])(*arrs)


def _split_copies(ins, lands, send_sems, recv_sems, group, mode):
    chips = group == "chips"
    me, peers = _group(group)
    npeer = len(peers)
    out = []
    for i in range(len(ins)):
        for p, (dev, slot) in enumerate(peers):
            src = ins[i].at[slot] if mode == "scatter" else ins[i]
            if not chips:
                dst = lands[i]
            else:
                dst = lands[i].at[p] if mode == "scatter" else lands[i].at[me]
            out.append(pltpu.make_async_remote_copy(src_ref=src, dst_ref=dst, send_sem=send_sems.at[npeer * i + p],
                                                    recv_sem=recv_sems.at[npeer * i + p], device_id=dev,
                                                    device_id_type=pl.DeviceIdType.MESH))
    return out


def _exchange_start(arrs, group, mode, name):
    k = len(arrs)
    chips = group == "chips"
    nsem = (3 if chips else 1) * k
    hbm = pl.BlockSpec(memory_space=pltpu.HBM)
    sem = pl.BlockSpec(memory_space=pltpu.SEMAPHORE)

    def land_shape(a):
        piece = a.shape[1:] if mode == "scatter" else a.shape
        if chips:
            piece = ((3,) if mode == "scatter" else (4,)) + piece
        return piece

    def body(*refs):
        ins, lands = refs[:k], refs[k:2 * k]
        send_sems, recv_sems = refs[2 * k], refs[2 * k + 1]
        token = refs[-1]
        for cp in _split_copies(ins, lands, send_sems, recv_sems, group, mode):
            cp.start()
        token[...] = jnp.zeros_like(token)

    srcs = [pltpu.with_memory_space_constraint(a, pltpu.HBM) for a in arrs]
    lands = [pltpu.with_memory_space_constraint(lax.empty(land_shape(a), a.dtype), pltpu.HBM) for a in arrs]
    out_shape = ([pltpu.SemaphoreType.DMA((nsem,)), pltpu.SemaphoreType.DMA((nsem,))]
                 + [pltpu.HBM(a.shape, a.dtype) for a in arrs] + [pltpu.HBM(land_shape(a), a.dtype) for a in arrs]
                 + [jax.ShapeDtypeStruct((8, 128), F32)])
    outs = _pcall(body, name=name, in_specs=[hbm] * (2 * k),
                  out_specs=[sem, sem] + [hbm] * (2 * k) + [pl.BlockSpec(memory_space=pltpu.VMEM)], out_shape=out_shape,
                  input_output_aliases={i: 2 + i for i in range(2 * k)},
                  compiler_params=pltpu.CompilerParams(has_side_effects=pltpu.SideEffectType.DATAFLOW_SIDE_EFFECTING))(
                      *srcs, *lands)
    return (outs[0], outs[1], list(outs[2:2 + k]), list(outs[2 + k:2 + 2 * k])), outs[-1]


def _exchange_wait(state, after, group, mode, name):
    send_sems, recv_sems, srcs, lands = state
    k = len(srcs)
    hbm = pl.BlockSpec(memory_space=pltpu.HBM)
    sem = pl.BlockSpec(memory_space=pltpu.SEMAPHORE)

    def body(*refs):
        ins, lnd = refs[:k], refs[k:2 * k]
        send_sems, recv_sems = refs[2 * k], refs[2 * k + 1]
        for cp in _split_copies(ins, lnd, send_sems, recv_sems, group, mode):
            cp.wait_send()
            cp.wait_recv()

    outs = _pcall(body, name=name, in_specs=[hbm] * (2 * k) + [sem, sem, pl.BlockSpec(memory_space=pl.ANY)],
                  out_specs=[hbm] * (2 * k),
                  out_shape=[pltpu.HBM(a.shape, a.dtype) for a in srcs] + [pltpu.HBM(a.shape, a.dtype) for a in lands],
                  input_output_aliases={i: i for i in range(2 * k)},
                  compiler_params=pltpu.CompilerParams(has_side_effects=pltpu.SideEffectType.DATAFLOW_SIDE_EFFECTING))(
                      *srcs, *lands, send_sems, recv_sems, after)
    return list(outs[:k]), list(outs[k:])


def _sum_own_recv(own, recv, me, out_dtype, name):
    n, R, C = own.shape
    nr = 1 if recv.ndim == 2 else recv.shape[0]
    tr = _pick(R, (256, 128, 64, 32, 16, 8))

    def body(me_ref, own_ref, *refs):
        o_ref = refs[-1]
        acc = own_ref[...].astype(F32)
        for r in refs[:-1]:
            acc = acc + r[...].astype(F32)
        o_ref[...] = acc.astype(out_dtype)

    specs = [pl.BlockSpec((None, tr, C), lambda i, me_ref: (me_ref[0], i, 0))]
    args = [own]
    if recv.ndim == 2:
        specs.append(pl.BlockSpec((tr, C), lambda i, me_ref: (i, 0)))
        args.append(recv)
    else:
        for p in range(nr):
            specs.append(pl.BlockSpec((None, tr, C), functools.partial(lambda i, me_ref, p: (p, i, 0), p=p)))
            args.append(recv)
    gs = pltpu.PrefetchScalarGridSpec(num_scalar_prefetch=1, grid=(R // tr,), in_specs=specs,
                                      out_specs=pl.BlockSpec((tr, C), lambda i, me_ref: (i, 0)))
    return _pcall(body, name=name, grid_spec=gs, out_shape=jax.ShapeDtypeStruct((R, C), out_dtype),
                  compiler_params=_cp(("parallel",)))(me, *args)


def _silu(x):
    return x * jax.nn.sigmoid(x)


def _ln(r, g, b):
    mu = jnp.mean(r, -1, keepdims=True)
    xc = r - mu
    var = jnp.mean(xc * xc, -1, keepdims=True)
    return xc * lax.rsqrt(var + LN_EPS) * g + b


def _softplus(x):
    return jnp.maximum(x, 0.0) + jnp.log1p(jnp.exp(-jnp.abs(x)))


def _res_ln_fwd(x, y, g, b, res, name):
    def fn(tv, fv):
        r = ALPHA * tv[0] + res * tv[1]
        out = _ln(r, fv[0], fv[1])
        return [r, out, out], []
    return _rowwise(fn, [x, y], [g, b], [(D, F32), (D, F32), (D, _ACT)], [], name=name)


def _ln_bwd(r, g, b, dout, res, name):
    def fn(tv, fv):
        _, vjp = jax.vjp(_ln, tv[0], fv[0], fv[1])
        dr, dg, db = vjp(tv[1])
        return [ALPHA * dr, res * dr], [dg, db]
    return _rowwise(fn, [r, dout], [g, b], [(D, F32), (D, _ACT)], [(1, D), (1, D)], name=name)


def _swiglu_act(h, name):
    def fn(tv, fv):
        return [_silu(tv[0]) * tv[1]], []
    return _rowwise(fn, [(h, DFF, 0), (h, DFF, 1)], [], [(DFF, _ACT)], [], name=name)[0]


def _swiglu_act_bwd(h, ds, name):
    def fn(tv, fv):
        s, vjp = jax.vjp(lambda a, g: _silu(a) * g, tv[0], tv[1])
        da, dg = vjp(tv[2])
        return [[da, dg], s], []
    return _rowwise(fn, [(h, DFF, 0), (h, DFF, 1), ds], [], [(2 * DFF, _ACT), (DFF, _ACT)], [], name=name)


def _loss_fwd_bwd(y, tgt, name):
    def fn(tv, fv):
        e = tv[0] - tv[1]
        row = jnp.sum(e * e, axis=1, keepdims=True)
        tot = jnp.sum(row, axis=0, keepdims=True) * (0.5 / D)
        return [e * (1.0 / D)], [jnp.broadcast_to(tot, (1, 128))]
    return _rowwise(fn, [y, tgt], [], [(D, F32)], [(1, 128)], name=name)


def _shift_down(x, k, row):
    return jnp.where(row >= k, pltpu.roll(x, k, axis=0), 0.0)


def _shift_up(x, k, row):
    n = x.shape[0]
    return jnp.where(row < n - k, pltpu.roll(x, n - k, axis=0), 0.0)


def _pool_window_masks(j):
    lane = lax.broadcasted_iota(jnp.int32, (1, 128), 1) + j * 128
    grp = lane // POOL_GDIM
    return [grp == g for g in range(4)]


def _pool_mean(u, bwd, name, col0=0):
    T = u.shape[0]
    B = T // S

    def body(u_ref, o_ref):
        j = pl.program_id(1)
        x = u_ref[...]
        row = lax.broadcasted_iota(jnp.int32, (S, 1), 0)
        masks = _pool_window_masks(j)
        inv = [1.0 / jnp.minimum(row + 1, w).astype(F32) for w in POOL_WINDOWS]
        if not bwd:
            s2 = x + _shift_down(x, 1, row)
            s4 = s2 + _shift_down(s2, 2, row)
            s8 = s4 + _shift_down(s4, 4, row)
            s16 = s8 + _shift_down(s8, 8, row)
            mean = jnp.where(masks[0], s2 * inv[0], jnp.where(masks[1], s4 * inv[1],
                             jnp.where(masks[2], s8 * inv[2], s16 * inv[3])))
            o_ref[...] = (mean - x).astype(o_ref.dtype)
        else:
            g = [jnp.where(masks[i], x * inv[i], 0.0) for i in range(4)]
            t = g[3]
            t = t + _shift_up(t, 8, row) + g[2]
            t = t + _shift_up(t, 4, row) + g[1]
            t = t + _shift_up(t, 2, row) + g[0]
            t = t + _shift_up(t, 1, row)
            o_ref[...] = (t - x).astype(o_ref.dtype)

    spec = pl.BlockSpec((S, 128), lambda b, j: (b, j))
    return _pcall(body, name=name, grid=(B, POOLW // 128),
                  in_specs=[pl.BlockSpec((S, 128), lambda b, j: (b, j + col0))], out_specs=spec,
                  out_shape=jax.ShapeDtypeStruct((T, POOLW), _ACT), compiler_params=_cp(("parallel", "parallel")))(u)


def _conv_silu(xbc, w, b, name, col0=0):
    T, C = xbc.shape[0], w.shape[1]
    B = T // S

    def body(x_ref, w_ref, b_ref, o_ref):
        x = x_ref[...]
        row = lax.broadcasted_iota(jnp.int32, (S, 1), 0)
        c = b_ref[...] + w_ref[3:4, :] * x
        for s in range(1, 4):
            c = c + w_ref[3 - s:4 - s, :] * _shift_down(x, s, row)
        o_ref[...] = _silu(c)

    return _pcall(body, name=name, grid=(B, C // 128),
                  in_specs=[pl.BlockSpec((S, 128), lambda b, j: (b, j + col0)), pl.BlockSpec((4, 128), lambda b, j: (0, j)),
                            pl.BlockSpec((1, 128), lambda b, j: (0, j))],
                  out_specs=pl.BlockSpec((S, 128), lambda b, j: (b, j)),
                  out_shape=jax.ShapeDtypeStruct((T, C), F32), compiler_params=_cp(("parallel", "parallel")))(xbc, w, b)


def _conv_silu_bwd(xbc, w, b, dact, name, col0=0):
    T, C = xbc.shape[0], w.shape[1]
    B = T // S

    def body(x_ref, w_ref, b_ref, d_ref, dx_ref, dw_ref, db_ref):
        bi = pl.program_id(1)
        x = x_ref[...]
        row = lax.broadcasted_iota(jnp.int32, (S, 1), 0)
        xs = [x] + [_shift_down(x, s, row) for s in range(1, 4)]
        c = b_ref[...]
        for s in range(4):
            c = c + w_ref[3 - s:4 - s, :] * xs[s]
        _, vjp = jax.vjp(_silu, c)
        dc = vjp(d_ref[...])[0]
        dx = w_ref[3:4, :] * dc
        for s in range(1, 4):
            dx = dx + w_ref[3 - s:4 - s, :] * _shift_up(dc, s, row)
        dx_ref[...] = dx.astype(dx_ref.dtype)
        first = bi == 0
        for s in range(4):
            _acc_rows(dw_ref, 3 - s, jnp.sum(dc * xs[s], axis=0, keepdims=True), first)
        _acc_rows(db_ref, 0, jnp.sum(dc, axis=0, keepdims=True), first)

    blk = pl.BlockSpec((S, 128), lambda j, b: (b, j))
    return _pcall(body, name=name, grid=(C // 128, B),
                  in_specs=[pl.BlockSpec((S, 128), lambda j, b: (b, j + col0)), pl.BlockSpec((4, 128), lambda j, b: (0, j)),
                            pl.BlockSpec((1, 128), lambda j, b: (0, j)), blk],
                  out_specs=[blk, pl.BlockSpec((4, 128), lambda j, b: (0, j)), pl.BlockSpec((1, 128), lambda j, b: (0, j))],
                  out_shape=[jax.ShapeDtypeStruct((T, C), _ACT), jax.ShapeDtypeStruct((4, C), F32),
                             jax.ShapeDtypeStruct((1, C), F32)],
                  compiler_params=_cp(("parallel", "arbitrary")))(xbc, w, b, dact)


def _acc_rows(ref, r, val, first):
    @pl.when(first)
    def _():
        ref[r:r + 1, :] = val

    @pl.when(jnp.logical_not(first))
    def _():
        ref[r:r + 1, :] += val


def _tri_consts():
    i = lax.broadcasted_iota(jnp.int32, (CH, CH), 0)
    j = lax.broadcasted_iota(jnp.int32, (CH, CH), 1)
    return (i == j).astype(F32), (j <= i).astype(F32), (i <= j).astype(F32), i >= j


def _ssd_chunk(h, x, dt, Bm, Cm, a, dsk, consts):
    eye, tril, triu, lower = consts
    Bb = Bm.astype(_MXU)
    Cb = Cm.astype(_MXU)
    cb = lax.dot_general(Cb, Bb, (((1,), (1,)), ((), ())), preferred_element_type=F32)
    ys, hn = [], []
    for e in range(4):
        adt = dt[e] * a[e]
        adt_row = jnp.sum(adt * eye, axis=0, keepdims=True)
        cs_col = jnp.sum(adt_row * tril, axis=1, keepdims=True)
        cs_row = jnp.sum(adt * triu, axis=0, keepdims=True)
        cs_last = jnp.sum(adt, axis=0, keepdims=True)
        decay = jnp.exp(jnp.where(lower, cs_col - cs_row, -jnp.inf))
        xb = (x[e] * dt[e]).astype(_MXU)
        y_diag = jnp.dot((cb * decay).astype(_MXU), xb, preferred_element_type=F32)
        bdec = (Bm * jnp.exp(cs_last - cs_col)).astype(_MXU)
        st = lax.dot_general(bdec, xb, (((0,), (0,)), ((), ())), preferred_element_type=F32)
        hn.append(h[e] * jnp.exp(cs_last) + st)
        y_off = jnp.exp(cs_col) * jnp.dot(Cb, h[e].astype(_MXU), preferred_element_type=F32)
        ys.append(y_diag + y_off + dsk[e] * x[e])
    return ys, hn


def _ssd_specs(order):
    def im(f):
        return lambda p, q: f(*order(p, q))
    xs = pl.BlockSpec((S, 256), im(lambda b, g: (b, g)))
    dt = pl.BlockSpec((None, S, 4), im(lambda b, g: (g, b, 0)))
    bc = pl.BlockSpec((S, 128), im(lambda b, g: (b, g)))
    hd = pl.BlockSpec((None, 1, 4), im(lambda b, g: (g, 0, 0)))
    hs = pl.BlockSpec((None, None, S // CH, 4, 128, 64), im(lambda b, g: (b, g, 0, 0, 0, 0)))
    bw = pl.BlockSpec((S, 128), im(lambda b, g: (b, 8 + g)))
    cw = pl.BlockSpec((S, 128), im(lambda b, g: (b, 12 + g)))
    return xs, dt, bc, hd, hs, bw, cw


def _ssd_fwd(act, dtg, a, dsk, name):
    xs = bm = cm = act
    T = xs.shape[0]
    B = T // S
    nc = S // CH

    def body(x_ref, dt_ref, b_ref, c_ref, a_ref, k_ref, y_ref, hs_ref, h_ref):
        consts = _tri_consts()
        h_ref[...] = jnp.zeros_like(h_ref)
        al = [a_ref[:, e:e + 1] for e in range(4)]
        kl = [k_ref[:, e:e + 1] for e in range(4)]

        def step(c, carry):
            r0 = pl.multiple_of(c * CH, CH)
            rows = pl.ds(r0, CH)
            h = [h_ref[e] for e in range(4)]
            for e in range(4):
                hs_ref[c, e] = h[e]
            x = [x_ref[rows, 64 * e:64 * e + 64] for e in range(4)]
            dt = [dt_ref[rows, e:e + 1] for e in range(4)]
            ys, hn = _ssd_chunk(h, x, dt, b_ref[rows, :], c_ref[rows, :], al, kl, consts)
            for e in range(4):
                y_ref[rows, 64 * e:64 * e + 64] = ys[e]
                h_ref[e] = hn[e]
            return carry

        lax.fori_loop(0, nc, step, 0)

    sx, sdt, sbc, shd, shs, sbw, scw = _ssd_specs(lambda b, g: (b, g))
    return _pcall(body, name=name, grid=(B, 4), in_specs=[sx, sdt, sbw, scw, shd, shd], out_specs=[sx, shs],
                  out_shape=[jax.ShapeDtypeStruct((T, 1024), F32), jax.ShapeDtypeStruct((B, 4, nc, 4, 128, 64), F32)],
                  scratch_shapes=[pltpu.VMEM((4, 128, 64), F32)],
                  compiler_params=_cp(("parallel", "parallel")))(xs, dtg, bm, cm, a, dsk)


def _lane_place(vals, width):
    lane = lax.broadcasted_iota(jnp.int32, (1, width), 1)
    out = jnp.zeros((1, width), F32)
    for e, v in enumerate(vals):
        out = out + jnp.where(lane == e, v, 0.0)
    return out


def _ssd_bwd(act, dtg, a, dsk, hs, dy, name):
    xs = bm = cm = act
    T = xs.shape[0]
    B = T // S
    nc = S // CH

    def body(x_ref, dt_ref, b_ref, c_ref, a_ref, k_ref, hs_ref, dy_ref,
             dx_ref, ddt_ref, db_ref, dc_ref, dak_ref, dh_ref, sc_ref):
        bi = pl.program_id(1)
        consts = _tri_consts()
        dh_ref[...] = jnp.zeros_like(dh_ref)
        sc_ref[...] = jnp.zeros_like(sc_ref)
        al = [a_ref[:, e:e + 1] for e in range(4)]
        kl = [k_ref[:, e:e + 1] for e in range(4)]

        def step(i, carry):
            c = nc - 1 - i
            r0 = pl.multiple_of(c * CH, CH)
            rows = pl.ds(r0, CH)
            h = [hs_ref[c, e] for e in range(4)]
            x = [x_ref[rows, 64 * e:64 * e + 64] for e in range(4)]
            dt = [dt_ref[rows, e:e + 1] for e in range(4)]
            f = functools.partial(_ssd_chunk, consts=consts)
            _, vjp = jax.vjp(f, h, x, dt, b_ref[rows, :], c_ref[rows, :], al, kl)
            dys = [dy_ref[rows, 64 * e:64 * e + 64] for e in range(4)]
            dhn = [dh_ref[e] for e in range(4)]
            dh, dx, ddt, dB, dC, da, dk = vjp((dys, dhn))
            for e in range(4):
                dh_ref[e] = dh[e]
                dx_ref[rows, 64 * e:64 * e + 64] = dx[e]
                ddt_ref[rows, e:e + 1] = ddt[e]
            db_ref[rows, :] = dB
            dc_ref[rows, :] = dC
            sc_ref[0:1, :] += _lane_place(da, 128)
            sc_ref[1:2, :] += _lane_place(dk, 128)
            return carry

        lax.fori_loop(0, nc, step, 0)
        first = bi == 0

        @pl.when(first)
        def _():
            dak_ref[...] = sc_ref[...]

        @pl.when(jnp.logical_not(first))
        def _():
            dak_ref[...] += sc_ref[...]

    sx, sdt, sbc, shd, shs, sbw, scw = _ssd_specs(lambda g, b: (b, g))
    return _pcall(body, name=name, grid=(4, B), in_specs=[sx, sdt, sbw, scw, shd, shd, shs, sx],
                  out_specs=[sx, sdt, sbc, sbc, pl.BlockSpec((None, 8, 128), lambda g, b: (g, 0, 0))],
                  out_shape=[jax.ShapeDtypeStruct((T, 1024), F32), jax.ShapeDtypeStruct((4, T, 4), F32),
                             jax.ShapeDtypeStruct((T, 512), F32), jax.ShapeDtypeStruct((T, 512), F32),
                             jax.ShapeDtypeStruct((4, 8, 128), F32)],
                  scratch_shapes=[pltpu.VMEM((4, 128, 64), F32), pltpu.VMEM((8, 128), F32)],
                  compiler_params=_cp(("parallel", "arbitrary")))(xs, dtg, bm, cm, a, dsk, hs, dy)


def _gate_norm(y, z, nw):
    t = y * _silu(z)
    return t * lax.rsqrt(jnp.mean(t * t, axis=-1, keepdims=True) + SSD_EPS) * nw


def _ssd_gate_norm(y, z, nw, name, zcol=0):
    def fn(tv, fv):
        return [[_gate_norm(tv[g], tv[4 + g], fv[0][:, 256 * g:256 * g + 256]) for g in range(4)]], []
    tiled = [(y, 256, g) for g in range(4)] + [(z, 256, zcol + g) for g in range(4)]
    return _rowwise(fn, tiled, [nw], [(1024, _ACT)], [], name=name)[0]


def _ssd_gate_norm_bwd(y, z, nw, dout, name, zcol=0):
    def fn(tv, fv):
        dys, dzs, dns = [], [], []
        for g in range(4):
            _, vjp = jax.vjp(_gate_norm, tv[g], tv[4 + g], fv[0][:, 256 * g:256 * g + 256])
            a, b, c = vjp(tv[8 + g])
            dys.append(a)
            dzs.append(b)
            dns.append(c)
        return [dys, dzs], [dns]
    tiled = [(y, 256, g) for g in range(4)] + [(z, 256, zcol + g) for g in range(4)] + [(dout, 256, g) for g in range(4)]
    return _rowwise(fn, tiled, [nw], [(1024, F32), (1024, _ACT)], [(1, 1024)], name=name)


def _t5_bucket_np(dist):
    dist = np.maximum(dist, 0)
    max_exact = 16
    large = max_exact + (np.log(np.maximum(dist, 1) / max_exact) / np.log(2048 / max_exact) * (32 - max_exact)).astype(np.int32)
    large = np.minimum(large, 31)
    return np.where(dist < max_exact, dist, large).astype(np.int32)


def _bucket_maps():
    qi = np.arange(128)[:, None]
    kj = np.arange(256)[None, :]
    return np.stack([_t5_bucket_np((qi - kj + 128) * dil) for dil in ATTN_DILS]).astype(np.int32)


def _bias_build(rel_bias, maps, name):
    def body(tab_ref, map_ref, o_ref):
        hh = pl.program_id(0)
        m = map_ref[...]
        acc = jnp.zeros((128, 256), F32)
        for b in range(32):
            acc = jnp.where(m == b, tab_ref[b, hh], acc)
        o_ref[...] = acc

    return _pcall(body, name=name, grid=(12,),
                  in_specs=[pl.BlockSpec(memory_space=pltpu.SMEM), pl.BlockSpec((None, 128, 256), lambda h: (h // 4, 0, 0))],
                  out_specs=pl.BlockSpec((None, 128, 256), lambda h: (h, 0, 0)),
                  out_shape=jax.ShapeDtypeStruct((12, 128, 256), F32), compiler_params=_cp(("parallel",)))(rel_bias, maps)


def _bias_reduce(dbias, maps, name):
    nl = dbias.shape[0]

    def body(d_ref, map_ref, o_ref):
        m = map_ref[...]
        d = d_ref[0]
        for i in range(1, nl):
            d = d + d_ref[i]
        lane = lax.broadcasted_iota(jnp.int32, (1, 128), 1)
        out = jnp.zeros((1, 128), F32)
        for b in range(32):
            s = jnp.sum(jnp.sum(jnp.where(m == b, d, 0.0), axis=1, keepdims=True), axis=0, keepdims=True)
            out = out + jnp.where(lane == b, s, 0.0)
        o_ref[...] = out

    return _pcall(body, name=name, grid=(12,),
                  in_specs=[pl.BlockSpec((nl, None, 128, 256), lambda h: (0, h, 0, 0)),
                            pl.BlockSpec((None, 128, 256), lambda h: (h // 4, 0, 0))],
                  out_specs=pl.BlockSpec((None, 1, 128), lambda h: (h, 0, 0)),
                  out_shape=jax.ShapeDtypeStruct((12, 1, 128), F32), compiler_params=_cp(("parallel",)))(dbias, maps)


def _attn_block(q, kb, vb, bias, mask):
    s = lax.dot_general(q.astype(_MXU), kb.astype(_MXU), (((1,), (1,)), ((), ())), preferred_element_type=F32) * 0.125 + bias
    s = jnp.where(mask, s, -jnp.inf)
    m = lax.stop_gradient(jnp.max(s, axis=-1, keepdims=True))
    p = jnp.exp(s - m)
    den = jnp.sum(p, axis=-1, keepdims=True)
    out = jnp.dot((p / den).astype(_MXU), vb.astype(_MXU), preferred_element_type=F32)
    return out, m + jnp.log(den)


def _band_mask():
    qi = lax.broadcasted_iota(jnp.int32, (128, 256), 0)
    kj = lax.broadcasted_iota(jnp.int32, (128, 256), 1)
    return (kj >= qi) & (kj <= qi + 128)


def _qkv_specs(gi, order):
    def spec(base):
        col = (base + 256 * gi) // 128
        return pl.BlockSpec((S, 128), lambda p, q: (order(p, q)[0], col + order(p, q)[1]))
    return [spec(O_Q), spec(O_K), spec(O_V)]


def _residue_rows(r, dil):
    return pl.ds(r, S // dil, stride=dil)


def _attn_fwd(hcat, bias_all, gi, name):
    dil = ATTN_DILS[gi]
    T = hcat.shape[0]
    B, L = T // S, S // dil
    nb = L // 128

    def body(q_ref, k_ref, v_ref, b_ref, o_ref, l_ref, *scr):
        mask = _band_mask()
        if dil > 1:
            qd, kd, vd, od, ld = scr
            for r in range(dil):
                rows = _residue_rows(r, dil)
                qd[r] = q_ref[rows, :]
                kd[r] = k_ref[rows, :]
                vd[r] = v_ref[rows, :]
            ld[...] = jnp.zeros_like(ld)
            rd = lambda ref, sref, r, rows, lanes: sref[r, rows, lanes]
        else:
            qd = kd = vd = od = ld = None
            l_ref[...] = jnp.zeros_like(l_ref)
            rd = lambda ref, sref, r, rows, lanes: ref[rows, lanes]

        def put(r, rows, e, o, l):
            if dil > 1:
                od[r, rows, 64 * e:64 * e + 64] = o
                ld[r, rows, e:e + 1] = l
            else:
                o_ref[rows, 64 * e:64 * e + 64] = o
                l_ref[rows, e:e + 1] = l

        for e in range(2):
            lanes = slice(64 * e, 64 * e + 64)
            bias_v = b_ref[e]

            def unit(r, carry, lanes=lanes, bias_v=bias_v, e=e):
                first = pl.ds(0, 128)
                o, l = _attn_block(rd(q_ref, qd, r, first, lanes), rd(k_ref, kd, r, first, lanes),
                                   rd(v_ref, vd, r, first, lanes), bias_v[:, 128:], mask[:, 128:])
                put(r, first, e, o, l)
                if nb > 1:
                    def step(n, c):
                        cur = pl.ds(pl.multiple_of(n * 128, 128), 128)
                        band = pl.ds(pl.multiple_of(n * 128 - 128, 128), 256)
                        o, l = _attn_block(rd(q_ref, qd, r, cur, lanes), rd(k_ref, kd, r, band, lanes),
                                           rd(v_ref, vd, r, band, lanes), bias_v, mask)
                        put(r, cur, e, o, l)
                        return c
                    lax.fori_loop(1, nb, step, 0)
                return carry

            if dil > 1:
                lax.fori_loop(0, dil, unit, 0)
            else:
                unit(0, 0)
        if dil > 1:
            for r in range(dil):
                rows = _residue_rows(r, dil)
                o_ref[rows, :] = od[r]
                l_ref[rows, :] = ld[r]

    scratch = [pltpu.VMEM((dil, L, 128), F32)] * 5 if dil > 1 else []
    return _pcall(body, name=name, grid=(B, 2),
                  in_specs=_qkv_specs(gi, lambda b, hp: (b, hp))
                  + [pl.BlockSpec((2, 128, 256), lambda b, hp: (2 * gi + hp, 0, 0))],
                  out_specs=[pl.BlockSpec((S, 128), lambda b, hp: (b, hp)),
                             pl.BlockSpec((None, S, 128), lambda b, hp: (hp, b, 0))],
                  out_shape=[jax.ShapeDtypeStruct((T, 256), F32), jax.ShapeDtypeStruct((2, T, 128), F32)],
                  scratch_shapes=scratch,
                  compiler_params=_cp(("parallel", "parallel")))(hcat, hcat, hcat, bias_all)


def _attn_bwd(hcat, bias_all, gi, do, dl, name):
    dil = ATTN_DILS[gi]
    T = hcat.shape[0]
    B, L = T // S, S // dil
    nb = L // 128

    def body(q_ref, k_ref, v_ref, b_ref, do_ref, dl_ref, dq_ref, dk_ref, dv_ref, db_ref, acc_ref, *scr):
        bi = pl.program_id(1)
        mask = _band_mask()
        if dil > 1:
            qd, kd, vd, dod, dld, dqd, dkd, dvd = scr
            for r in range(dil):
                rows = _residue_rows(r, dil)
                qd[r] = q_ref[rows, :]
                kd[r] = k_ref[rows, :]
                vd[r] = v_ref[rows, :]
                dod[r] = do_ref[rows, :]
                dld[r] = dl_ref[rows, :]
            srcs = (qd, kd, vd, dod, dld)
            dsts = (dqd, dkd, dvd)
            rd = lambda i, r, rows, lanes: srcs[i][r, rows, lanes]
        else:
            srcs = (q_ref, k_ref, v_ref, do_ref, dl_ref)
            dsts = (dq_ref, dk_ref, dv_ref)
            rd = lambda i, r, rows, lanes: srcs[i][rows, lanes]
        dsts[1][...] = jnp.zeros_like(dsts[1])
        dsts[2][...] = jnp.zeros_like(dsts[2])

        def at(i, r, rows, lanes):
            return (r, rows, lanes) if dil > 1 else (rows, lanes)

        for e in range(2):
            lanes = slice(64 * e, 64 * e + 64)
            one = slice(e, e + 1)
            bias_v = b_ref[e]
            acc_ref[...] = jnp.zeros_like(acc_ref)

            def unit(r, carry, lanes=lanes, one=one, bias_v=bias_v):
                first = pl.ds(0, 128)
                f0 = functools.partial(_attn_block, mask=mask[:, 128:])
                _, vjp = jax.vjp(f0, rd(0, r, first, lanes), rd(1, r, first, lanes), rd(2, r, first, lanes),
                                 bias_v[:, 128:])
                dq, dkb, dvb, dbs = vjp((rd(3, r, first, lanes), rd(4, r, first, one)))
                dsts[0][at(0, r, first, lanes)] = dq
                dsts[1][at(1, r, first, lanes)] += dkb
                dsts[2][at(2, r, first, lanes)] += dvb
                acc_ref[:, 128:256] += dbs
                if nb > 1:
                    f1 = functools.partial(_attn_block, mask=mask)

                    def step(n, c):
                        cur = pl.ds(pl.multiple_of(n * 128, 128), 128)
                        band = pl.ds(pl.multiple_of(n * 128 - 128, 128), 256)
                        _, vjp = jax.vjp(f1, rd(0, r, cur, lanes), rd(1, r, band, lanes), rd(2, r, band, lanes), bias_v)
                        dq, dkb, dvb, dbs = vjp((rd(3, r, cur, lanes), rd(4, r, cur, one)))
                        dsts[0][at(0, r, cur, lanes)] = dq
                        dsts[1][at(1, r, band, lanes)] += dkb
                        dsts[2][at(2, r, band, lanes)] += dvb
                        acc_ref[...] += dbs
                        return c
                    lax.fori_loop(1, nb, step, 0)
                return carry

            if dil > 1:
                lax.fori_loop(0, dil, unit, 0)
            else:
                unit(0, 0)

            @pl.when(bi == 0)
            def _(e=e):
                db_ref[e] = acc_ref[...]

            @pl.when(bi > 0)
            def _(e=e):
                db_ref[e] += acc_ref[...]

        if dil > 1:
            for r in range(dil):
                rows = _residue_rows(r, dil)
                dq_ref[rows, :] = dqd[r]
                dk_ref[rows, :] = dkd[r]
                dv_ref[rows, :] = dvd[r]

    order = lambda hp, b: (b, hp)
    blk = pl.BlockSpec((S, 128), lambda hp, b: (b, hp))
    lblk = pl.BlockSpec((None, S, 128), lambda hp, b: (hp, b, 0))
    sds = jax.ShapeDtypeStruct((T, 256), F32)
    scratch = [pltpu.VMEM((128, 256), F32)] + ([pltpu.VMEM((dil, L, 128), F32)] * 8 if dil > 1 else [])
    return _pcall(body, name=name, grid=(2, B),
                  in_specs=_qkv_specs(gi, order) + [pl.BlockSpec((2, 128, 256), lambda hp, b: (2 * gi + hp, 0, 0)), blk, lblk],
                  out_specs=[blk, blk, blk, pl.BlockSpec((2, 128, 256), lambda hp, b: (hp, 0, 0))],
                  out_shape=[sds, sds, sds, jax.ShapeDtypeStruct((4, 128, 256), F32)],
                  scratch_shapes=scratch,
                  compiler_params=_cp(("parallel", "arbitrary")))(hcat, hcat, hcat, bias_all, do, dl)


def _lse_merge(o0, o1, o2, l0, l1, l2):
    m = lax.stop_gradient(jnp.maximum(jnp.maximum(l0, l1), l2))
    e0, e1, e2 = jnp.exp(l0 - m), jnp.exp(l1 - m), jnp.exp(l2 - m)
    den = e0 + e1 + e2
    return (e0 / den) * o0 + (e1 / den) * o1 + (e2 / den) * o2


def _attn_merge(outs, lses, dy, name):
    T = outs[0].shape[0]
    bwd = dy is not None
    tm = 512

    def body(*refs):
        o_refs, l_refs = refs[:3], refs[3:6]
        if bwd:
            for r in refs[10:13]:
                r[...] = jnp.zeros_like(r)
        for e in range(2):
            lanes = slice(64 * e, 64 * e + 64)
            vals = [r[:, lanes] for r in o_refs] + [r[:, e:e + 1] for r in l_refs]
            if not bwd:
                refs[6][:, lanes] = _lse_merge(*vals).astype(refs[6].dtype)
            else:
                _, vjp = jax.vjp(_lse_merge, *vals)
                g = vjp(refs[6][:, lanes])
                for r, v in zip(refs[7:10], g[:3]):
                    r[:, lanes] = v
                for r, v in zip(refs[10:13], g[3:]):
                    r[:, e:e + 1] = v

    blk = pl.BlockSpec((tm, 128), lambda i, hp: (i, hp))
    lblk = pl.BlockSpec((None, tm, 128), lambda i, hp: (hp, i, 0))
    lsd = jax.ShapeDtypeStruct((2, T, 128), F32)
    if not bwd:
        return _pcall(body, name=name, grid=(T // tm, 2), in_specs=[blk] * 3 + [lblk] * 3, out_specs=blk,
                      out_shape=jax.ShapeDtypeStruct((T, 256), F32),
                      compiler_params=_cp(("parallel", "parallel")))(*outs, *lses)
    return _pcall(body, name=name, grid=(T // tm, 2), in_specs=[blk] * 3 + [lblk] * 3 + [blk],
                  out_specs=[blk] * 3 + [lblk] * 3, out_shape=[jax.ShapeDtypeStruct((T, 256), F32)] * 3 + [lsd] * 3,
                  compiler_params=_cp(("parallel", "parallel")))(*outs, *lses, dy)


def _gmerge(g0, g1, g2, gb, ya, yb, yc):
    return (jax.nn.sigmoid(g0 + gb[:, 0:D]) * ya + jax.nn.sigmoid(g1 + gb[:, D:2 * D]) * yb
            + jax.nn.sigmoid(g2 + gb[:, 2 * D:3 * D]) * yc)


def _gated_merge(gates, gb, ya, yb, yc, name, gcol=0):
    def fn(tv, fv):
        return [_gmerge(tv[0], tv[1], tv[2], fv[0], tv[3], tv[4], tv[5])], []
    return _rowwise(fn, [(gates, D, gcol), (gates, D, gcol + 1), (gates, D, gcol + 2), ya, yb, yc], [gb], [(D, _ACT)], [],
                    name=name)[0]


def _gated_merge_bwd(gates, gb, ya, yb, yc, dm, name, gcol=0):
    def fn(tv, fv):
        _, vjp = jax.vjp(_gmerge, tv[0], tv[1], tv[2], fv[0], tv[3], tv[4], tv[5])
        d0, d1, d2, dgb, da, db, dc = vjp(tv[6])
        return [[d0, d1, d2], da, db, dc], [dgb]
    return _rowwise(fn, [(gates, D, gcol), (gates, D, gcol + 1), (gates, D, gcol + 2), ya, yb, yc, dm], [gb],
                    [(3 * D, _ACT), (D, _ACT), (D, _ACT), (D, _ACT)], [(1, 3 * D)], name=name)


def _pool_affine(t1, pb, ps, dout, name):
    if dout is None:
        def fn(tv, fv):
            return [(tv[0] + fv[0]) * fv[1]], []
        return _rowwise(fn, [t1], [pb, ps], [(POOLW, _ACT)], [], name=name)[0]

    def fnb(tv, fv):
        t2, vjp = jax.vjp(lambda t, b, s: (t + b) * s, tv[0], fv[0], fv[1])
        dt, db, dsc = vjp(tv[1])
        return [dt, t2], [db, dsc]
    return _rowwise(fnb, [t1, dout], [pb, ps], [(POOLW, _ACT), (POOLW, _ACT)], [(1, POOLW), (1, POOLW)], name=name)


def _dt_softplus(dt_raw, dt_bias, ddt, name):
    f = lambda r, b: _softplus(r + b)
    if ddt is None:
        def fn(tv, fv):
            return [f(tv[0], fv[0])], []
        return _rowwise(fn, [dt_raw], [dt_bias], [(16, F32)], [], name=name, tm=1024)[0]

    def fnb(tv, fv):
        _, vjp = jax.vjp(f, tv[0], fv[0])
        dr, db = vjp(tv[1])
        return [dr], [db]
    return _rowwise(fnb, [dt_raw, ddt], [dt_bias], [(16, F32)], [(1, 16)], name=name, tm=1024)


def _adamw(w, g, m, v, name):
    R, C = w.shape
    tm = _pick(R, (256, 128, 64, 32, 16, 8))
    c1 = 1.0 / (1.0 - ADAM_B1 ** ADAM_STEP)
    c2 = 1.0 / (1.0 - ADAM_B2 ** ADAM_STEP)

    def fn(tv, fv):
        wv, gv, mv, vv = tv
        mn = ADAM_B1 * mv + (1.0 - ADAM_B1) * gv
        vn = ADAM_B2 * vv + (1.0 - ADAM_B2) * (gv * gv)
        delta = -ADAM_LR * ((mn * c1) / (jnp.sqrt(vn * c2) + ADAM_EPS) + ADAM_WD * wv)
        return [delta, mn, vn], []
    return _rowwise(fn, [w, g, m, v], [], [(C, F32)] * 3, [], name=name, tm=tm)


def _ffn_fwd(x, xm, w13, w2, g, b, tag, dep=None):
    h = _mm(xm, w13, dep=dep, name=f"{tag}_h")
    s = _swiglu_act(h, name=f"{tag}_act")
    y = _mm(s, w2, name=f"{tag}_y")
    r, out, outm = _res_ln_fwd(x, y, g, b, 0.5, name=f"{tag}_ln")
    return out, outm, dict(x=xm, h=h, r=r)


def _ffn_bwd(dout, sv, w13, w2, g, b, tag, dep=None):
    dskip, dy, dg, db = _ln_bwd(sv['r'], g, b, dout, 0.5, name=f"{tag}_lnb")
    ds = _mm(dy, w2, tb=True, dep=dep, name=f"{tag}_ds")
    dh, s = _swiglu_act_bwd(sv['h'], ds, name=f"{tag}_actb")
    dw2 = _mm(s, dy, ta=True, name=f"{tag}_dw2")
    dw13 = _mm(sv['x'], dh, ta=True, name=f"{tag}_dw13")
    dx = _mm(dh, w13, tb=True, add=dskip, name=f"{tag}_dx")
    return dx, dict(w13=dw13, w2=dw2, g=dg, b=db)


def _mixer_fwd(x1, x1m, W, bias_all, tag, dep=None):
    T = x1.shape[0]
    hcat = _mm(x1m, W['w_in_r'], dep=dep, name=f"{tag}_hcat")
    dt_raw = hcat[:, O_DT:O_DT + 16]
    pooled = _pool_mean(hcat, False, name=f"{tag}_pool", col0=O_U // 128)
    t1 = _mm(pooled, W['pool_wbd'], name=f"{tag}_pt1")
    t2 = _pool_affine(t1, W['pool_b'], W['pool_scale'], None, name=f"{tag}_paff")
    ya = _mm(t2, W['p_pool'], name=f"{tag}_ya")
    act = _conv_silu(hcat, W['conv_w'], W['conv_b'], name=f"{tag}_conv", col0=O_XBC // 128)
    dt = _dt_softplus(dt_raw, W['dt_bias'], None, name=f"{tag}_dt")
    dtg = dt.reshape(T, 4, 4).transpose(1, 0, 2)
    yscan, hs = _ssd_fwd(act, dtg, W['a_neg'], W['d_skip'], name=f"{tag}_ssd")
    ybn = _ssd_gate_norm(yscan, hcat, W['ssd_norm'], name=f"{tag}_gn", zcol=O_Z // 256)
    yb = _mm(ybn, W['p_ssd'], name=f"{tag}_yb")
    outs, lses = [], []
    for gi in range(len(ATTN_DILS)):
        o, l = _attn_fwd(hcat, bias_all, gi, name=f"{tag}_attn{gi}")
        outs.append(o)
        lses.append(l)
    ycp = _attn_merge(outs, lses, None, name=f"{tag}_amerge")
    yc = _mm(ycp, W['p_attn'], name=f"{tag}_yc")
    merged = _gated_merge(hcat, W['gate_b'], ya, yb, yc, name=f"{tag}_gm", gcol=O_G // D)
    mix = _mm(merged, W['w_out'], name=f"{tag}_mix")
    r, out, outm = _res_ln_fwd(x1, mix, W['ln2_g'], W['ln2_b'], 1.0, name=f"{tag}_ln")
    sv = dict(x1=x1m, dt_raw=dt_raw, pooled=pooled, t1=t1, act=act, dtg=dtg,
              hs=hs, yscan=yscan, ybn=ybn, hcat=hcat, outs=outs, lses=lses, ycp=ycp, ya=ya, yb=yb, yc=yc,
              merged=merged, r=r)
    return out, outm, sv


def _mixer_bwd(dout, sv, W, bias_all, tag, dep=None):
    T = dout.shape[0]
    gr = {}
    dx1a, dr, gr['ln2_g'], gr['ln2_b'] = _ln_bwd(sv['r'], W['ln2_g'], W['ln2_b'], dout, 1.0, name=f"{tag}_lnb")
    dmerged = _mm(dr, W['w_out'], tb=True, dep=dep, name=f"{tag}_dmerged")
    gr['w_out'] = _mm(sv['merged'], dr, ta=True, name=f"{tag}_dwout")
    dgates, dya, dyb, dyc, gr['gate_b'] = _gated_merge_bwd(sv['hcat'], W['gate_b'], sv['ya'], sv['yb'], sv['yc'],
                                                           dmerged, name=f"{tag}_gmb", gcol=O_G // D)
    dycp = _mm(dyc, W['p_attn'], tb=True, name=f"{tag}_dycp")
    gr['p_attn'] = _mm(sv['ycp'], dyc, ta=True, name=f"{tag}_dpattn")
    dml = _attn_merge(sv['outs'], sv['lses'], dycp, name=f"{tag}_amergeb")
    dq, dk, dv, dbias = [], [], [], []
    for gi in range(len(ATTN_DILS)):
        a, b, c, d = _attn_bwd(sv['hcat'], bias_all, gi, dml[gi], dml[3 + gi], name=f"{tag}_attnb{gi}")
        dq.append(a)
        dk.append(b)
        dv.append(c)
        dbias.append(d)
    dbias = jnp.concatenate(dbias, axis=0)
    dybn = _mm(dyb, W['p_ssd'], tb=True, name=f"{tag}_dybn")
    gr['p_ssd'] = _mm(sv['ybn'], dyb, ta=True, name=f"{tag}_dpssd")
    dyscan, dz, gr['ssd_norm'] = _ssd_gate_norm_bwd(sv['yscan'], sv['hcat'], W['ssd_norm'], dybn, name=f"{tag}_gnb",
                                                    zcol=O_Z // 256)
    dxs, ddtg, dbm, dcm, dak = _ssd_bwd(sv['act'], sv['dtg'], W['a_neg'], W['d_skip'], sv['hs'], dyscan,
                                        name=f"{tag}_ssdb")
    gr['a_neg'], gr['d_skip'] = dak[:, 0, 0:4], dak[:, 1, 0:4]
    ddt = ddtg.transpose(1, 0, 2).reshape(T, 16)
    ddt_raw, gr['dt_bias'] = _dt_softplus(sv['dt_raw'], W['dt_bias'], ddt, name=f"{tag}_dtb")
    dact = jnp.concatenate([dxs, dbm, dcm], axis=1)
    dxbc, gr['conv_w'], gr['conv_b'] = _conv_silu_bwd(sv['hcat'], W['conv_w'], W['conv_b'], dact, name=f"{tag}_convb",
                                                      col0=O_XBC // 128)
    dt2 = _mm(dya, W['p_pool'], tb=True, name=f"{tag}_dt2")
    dt1, t2, gr['pool_b'], gr['pool_scale'] = _pool_affine(sv['t1'], W['pool_b'], W['pool_scale'], dt2, name=f"{tag}_paffb")
    gr['p_pool'] = _mm(t2, dya, ta=True, name=f"{tag}_dppool")
    dpooled = _mm(dt1, W['pool_wbd'], tb=True, name=f"{tag}_dpooled")
    gr['pool_wbd'] = _mm(sv['pooled'], dt1, ta=True, name=f"{tag}_dpoolw")
    du = _pool_mean(dpooled, True, name=f"{tag}_poolb")
    dhcat = jnp.concatenate([t.astype(_ACT) for t in [du, dz, dxbc] + dq + dk + dv + [dgates, ddt_raw]]
                            + [jnp.zeros((T, HC - O_DT - 16), _ACT)], axis=1)
    dx1 = _mm(dhcat, W['w_in_r'], tb=True, add=dx1a, name=f"{tag}_dx1")
    gr['w_in_r'] = _mm(sv['x1'], dhcat, ta=True, name=f"{tag}_dwin")
    return dx1, gr, dbias


def _prep_layer_weights(i, inp, G):
    W = {}
    for n in BIG:
        if n not in G:
            continue
        g = G[n]
        if n == 'w_in':
            W['w_in_r'] = jnp.concatenate(_nat_pieces(g, 0, 3840) + _nat_pieces(g, 3856, 9232) + _nat_pieces(g, 3840, 3856)
                                          + [jnp.zeros((D, HC - 9232), g.dtype)], axis=1)
        elif n in COL_SHARDED:
            W[n] = jnp.concatenate([g[j] for j in range(4)], axis=1)
        else:
            W[n] = g.reshape(4 * g.shape[1], g.shape[2])
    pw = inp['pool_w'][i].astype(_MXU)
    wbd = jnp.zeros((POOLW, POOLW), _MXU)
    for g in range(4):
        wbd = lax.dynamic_update_slice(wbd, pw[g], (g * POOL_GDIM, g * POOL_GDIM))
    W['pool_wbd'] = wbd
    W['pool_b'] = inp['pool_b'][i].reshape(1, POOLW)
    W['pool_scale'] = inp['pool_scale'][i].reshape(1, POOLW)
    if 'conv_w' in G:
        W['conv_w'] = jnp.concatenate([G['conv_w'][j] for j in range(4)], axis=1)
        W['gate_b'] = jnp.concatenate([G['gate_b'][j][b:b + 1] for b in range(3) for j in range(4)], axis=1)
    W['conv_b'] = inp['conv_b'][i].reshape(1, 2048)
    W['dt_bias'] = inp['dt_bias'][i].reshape(1, 16)
    W['a_neg'] = (-jnp.exp(inp['a_log'][i])).reshape(4, 1, 4)
    W['d_skip'] = inp['d_skip'][i].reshape(4, 1, 4)
    W['ssd_norm'] = inp['ssd_norm'][i].reshape(1, D)
    for n in ('ln1_g', 'ln1_b', 'ln2_g', 'ln2_b', 'ln3_g', 'ln3_b'):
        W[n] = inp[n][i].reshape(1, D)
    return W


GATHER_FIRST = ['ffn1_w13', 'ffn1_w2']
GATHER_REST = [n for n in BIG if n not in GATHER_FIRST] + ['gate_b', 'conv_w']


def _gather_start(inp, i, names):
    core = lax.axis_index("c")
    arrs = []
    for n in names:
        s = inp[n][i]
        if n in BIG:
            s = lax.dynamic_slice_in_dim(s, core * (s.shape[0] // 2), s.shape[0] // 2, axis=0).astype(BF16)
        arrs.append(s)
    state, token = _exchange_start(arrs, "chips", "gather", name="gather_start")
    return (names, state), token


def _gather_mid(handle, after):
    names, state = handle
    me = 2 * lax.axis_index("x") + lax.axis_index("y")
    own, outs = _exchange_wait(state, after, "chips", "gather", name="gather_wait")
    outs = [lax.dynamic_update_slice(o, a[None], (me, 0, 0)) for o, a in zip(outs, own)]
    big = [o for n, o in zip(names, outs) if n in BIG]
    state, token = _exchange_start(big, "cores", "gather", name="share_start")
    return (names, outs, state), token


def _gather_finish(handle, after):
    names, outs, state = handle
    core = lax.axis_index("c")
    mine, theirs = _exchange_wait(state, after, "cores", "gather", name="share_wait")
    G = {n: o for n, o in zip(names, outs) if n not in BIG}
    for n, a, b in zip([n for n in names if n in BIG], mine, theirs):
        G[n] = jnp.concatenate([jnp.where(core == 0, a, b), jnp.where(core == 0, b, a)], axis=1)
    return G


W_IN_SHARD = 2308


def _nat_pieces(g, lo, hi):
    out = []
    for j in range(4):
        s, e = max(lo, W_IN_SHARD * j), min(hi, W_IN_SHARD * (j + 1))
        if s < e:
            out.append(g[j][:, s - W_IN_SHARD * j:e - W_IN_SHARD * j])
    return out


def _reord_ranges(lo, hi):
    out = []
    for a, b, off in ((0, 3840, 0), (3840, 3856, O_DT - 3840), (3856, 9232, -16)):
        s, e = max(lo, a), min(hi, b)
        if s < e:
            out.append((s + off, e + off))
    return out


def _halves_of(n, g):
    if n == 'w_in':
        shards = [jnp.concatenate([g[:, a:b] for a, b in _reord_ranges(W_IN_SHARD * j, W_IN_SHARD * (j + 1))], axis=1)
                  for j in range(4)]
    elif n in COL_SHARDED:
        c = g.shape[1] // 4
        shards = [g[:, j * c:(j + 1) * c] for j in range(4)]
    else:
        r = g.shape[0] // 4
        shards = [g[j * r:(j + 1) * r] for j in range(4)]
    r2 = shards[0].shape[0] // 2
    return jnp.stack([jnp.concatenate([s[h * r2:(h + 1) * r2] for s in shards], axis=0) for h in range(2)])


def _reduce_a(grads):
    names = list(grads)
    halves = [_halves_of(n, grads[n]) for n in names]
    state, token = _exchange_start(halves, "cores", "scatter", name="rsc_start")
    return (names, state), token


def _reduce_b(handle, after):
    names, state = handle
    core = lax.axis_index("c").reshape(1)
    halves, got = _exchange_wait(state, after, "cores", "scatter", name="rsc_wait")
    chip = [_sum_own_recv(h, t, core, BF16, name="rs_sum2") for h, t in zip(halves, got)]
    chip = [t.reshape(4, t.shape[0] // 4, t.shape[1]) for t in chip]
    state, token = _exchange_start(chip, "chips", "scatter", name="rs_start")
    return (names, state), token


def _reduce_c(handle, after):
    names, state = handle
    chip_id = (2 * lax.axis_index("x") + lax.axis_index("y")).reshape(1)
    chip, got = _exchange_wait(state, after, "chips", "scatter", name="rs_wait")
    red = [_sum_own_recv(h, t, chip_id, F32, name="rs_sum4") for h, t in zip(chip, got)]
    other = _exchange(red, "cores", "gather", name="rs_share")
    out = {}
    for n, mine, theirs in zip(names, red, other):
        out[n] = jnp.where(lax.axis_index("c") == 0, jnp.concatenate([mine, theirs]), jnp.concatenate([theirs, mine]))
    return out


class _Comm:
    def __init__(self, inp):
        self.inp = inp

    def gather_start(self, i, names):
        return _gather_start(self.inp, i, names)

    gather_mid = staticmethod(_gather_mid)
    gather_finish = staticmethod(_gather_finish)

    def reduce_a(self, i, grads):
        return _reduce_a({n: grads[n] for n in BIG})

    reduce_b = staticmethod(_reduce_b)
    reduce_c = staticmethod(_reduce_c)


def _allreduce_small(vec):
    for group in ("cores", "x", "y"):
        recv = _exchange([vec], group, "gather", name=f"ar_{group}")[0]
        vec = _rowwise(lambda tv, fv: ([tv[0] + tv[1]], []), [vec, recv], [], [(128, F32)], [], name=f"ar_add_{group}")[0]
    return vec


def _pack(arrs):
    flat = jnp.concatenate([a.reshape(-1) for a in arrs])
    n = flat.shape[0]
    pad = (-n) % (256 * 128)
    flat = jnp.concatenate([flat, jnp.zeros((pad,), F32)])
    return flat.reshape(-1, 128)


def _unpack(p, shapes):
    flat = p.reshape(-1)
    out, off = [], 0
    for s in shapes:
        sz = int(np.prod(s))
        out.append(flat[off:off + sz].reshape(s))
        off += sz
    return out


def _forward_backward(inp, comm, bias_all):
    x = xm = inp['x'].reshape(-1, D)
    tgt = inp['loss_target'].reshape(-1, D)
    saved, Ws = [], []
    h_first, _ = comm.gather_start(0, GATHER_FIRST)
    h_rest, dep = comm.gather_start(0, GATHER_REST)
    h_first, tok = comm.gather_mid(h_first, x)
    G = comm.gather_finish(h_first, tok)
    for i in range(NL):
        W = _prep_layer_weights(i, inp, G)
        start_next = lambda: (comm.gather_start(i + 1, BIG + ['gate_b', 'conv_w']) if i + 1 < NL else (None, None))
        if i > 0:
            h_next, dep = start_next()
        x1, x1m, s1 = _ffn_fwd(x, xm, W['ffn1_w13'], W['ffn1_w2'], W['ln1_g'], W['ln1_b'], "f1", dep)
        if i == 0:
            h_rest, tok = comm.gather_mid(h_rest, x1m)
            W.update(_prep_layer_weights(i, inp, comm.gather_finish(h_rest, tok)))
            h_next, dep = start_next()
        x2, x2m, s2 = _mixer_fwd(x1, x1m, W, bias_all, "mx", dep if i == 0 else None)
        dep = None
        if h_next is not None:
            h_next, dep = comm.gather_mid(h_next, x2m)
        x, xm, s3 = _ffn_fwd(x2, x2m, W['ffn2_w13'], W['ffn2_w2'], W['ln3_g'], W['ln3_b'], "f2", dep)
        if h_next is not None:
            G = comm.gather_finish(h_next, xm)
        saved.append((s1, s2, s3))
        Ws.append(W)
    dy, lpart = _loss_fwd_bwd(x, tgt, name="loss")
    fins, reduced, dbiases = [None] * NL, [None] * NL, [None] * NL
    pend_a, pend_b, dep = None, None, None
    for i in reversed(range(NL)):
        W = Ws[i]
        s1, s2, s3 = saved[i]
        g = {}
        dx2, f = _ffn_bwd(dy, s3, W['ffn2_w13'], W['ffn2_w2'], W['ln3_g'], W['ln3_b'], "f2", dep)
        g['ffn2_w13'], g['ffn2_w2'], g['ln3_g'], g['ln3_b'] = f['w13'], f['w2'], f['g'], f['b']
        dep = None
        if pend_a is not None:
            handle, dep = comm.reduce_b(pend_a[1], dx2)
            pend_b = (pend_a[0], handle)
        dx1, gm, dbiases[i] = _mixer_bwd(dx2, s2, W, bias_all, "mx", dep)
        g.update(gm)
        dy, f = _ffn_bwd(dx1, s1, W['ffn1_w13'], W['ffn1_w2'], W['ln1_g'], W['ln1_b'], "f1")
        g['ffn1_w13'], g['ffn1_w2'], g['ln1_g'], g['ln1_b'] = f['w13'], f['w2'], f['g'], f['b']
        fins[i] = _finish_layer_grads(i, g, inp)
        if pend_b is not None:
            reduced[pend_b[0]] = comm.reduce_c(pend_b[1], dy)
            pend_b = None
        handle, dep = comm.reduce_a(i, fins[i])
        pend_a = (i, handle)
    return lpart, dy, fins, reduced, pend_a, dbiases


def _finish_layer_grads(i, g, inp):
    out = {n: g[n] for n in BIG if n != 'w_in'}
    out['w_in'] = g['w_in_r']
    out['pool_w'] = jnp.stack([g['pool_wbd'][k * POOL_GDIM:(k + 1) * POOL_GDIM, k * POOL_GDIM:(k + 1) * POOL_GDIM] for k in range(4)])
    out['pool_b'] = g['pool_b'].reshape(4, POOL_GDIM)
    out['pool_scale'] = g['pool_scale'].reshape(POOLW)
    out['conv_w'] = g['conv_w']
    out['conv_b'] = g['conv_b'].reshape(2048)
    out['dt_bias'] = g['dt_bias'].reshape(16)
    out['a_log'] = (g['a_neg'].reshape(16)) * (-jnp.exp(inp['a_log'][i]))
    out['d_skip'] = g['d_skip'].reshape(16)
    out['ssd_norm'] = g['ssd_norm'].reshape(D)
    out['gate_b'] = g['gate_b'].reshape(3, D)
    for n in ('ln1_g', 'ln1_b', 'ln2_g', 'ln2_b', 'ln3_g', 'ln3_b'):
        out[n] = g[n].reshape(D)
    return out


def kernel(x, ffn1_w13, ffn1_w2, ln1_g, ln1_b, w_in, gate_b, pool_w, pool_b, pool_scale, conv_w, conv_b,
           dt_bias, a_log, d_skip, ssd_norm, rel_bias, p_pool, p_ssd, p_attn, w_out, ln2_g, ln2_b, ffn2_w13,
           ffn2_w2, ln3_g, ln3_b, loss_target, m_ffn1_w13, m_ffn1_w2, m_ln1_g, m_ln1_b, m_w_in, m_gate_b,
           m_pool_w, m_pool_b, m_pool_scale, m_conv_w, m_conv_b, m_dt_bias, m_a_log, m_d_skip, m_ssd_norm,
           m_rel_bias, m_p_pool, m_p_ssd, m_p_attn, m_w_out, m_ln2_g, m_ln2_b, m_ffn2_w13, m_ffn2_w2, m_ln3_g,
           m_ln3_b, v_ffn1_w13, v_ffn1_w2, v_ln1_g, v_ln1_b, v_w_in, v_gate_b, v_pool_w, v_pool_b,
           v_pool_scale, v_conv_w, v_conv_b, v_dt_bias, v_a_log, v_d_skip, v_ssd_norm, v_rel_bias, v_p_pool,
           v_p_ssd, v_p_attn, v_w_out, v_ln2_g, v_ln2_b, v_ffn2_w13, v_ffn2_w2, v_ln3_g, v_ln3_b):
    inp = dict(locals())
    maps = jnp.asarray(_bucket_maps())
    bias_all = _bias_build(rel_bias, maps, name="bias_build")
    comm = _Comm(inp)
    lpart, gx, fins, red, pending, dbiases = _forward_backward(inp, comm, bias_all)
    loss = lax.psum(lpart[0, 0], ("x", "y", "c"))

    small_l = [n for n in SMALL if n != 'rel_bias']
    drel = _bias_reduce(jnp.stack(dbiases), maps, name="bias_reduce")[:, 0, :32].T
    handle_b, _ = comm.reduce_b(pending[1], drel)
    small_arrs = [jnp.stack([fins[i][n] for i in range(NL)]) for n in small_l] + [drel]
    packed = _allreduce_small(_pack(small_arrs))
    gsmall = dict(zip(small_l + ['rel_bias'], _unpack(packed, [a.shape for a in small_arrs])))
    shard = 2 * lax.axis_index("x") + lax.axis_index("y")
    gsmall['gate_b'] = lax.dynamic_slice_in_dim(gsmall['gate_b'], shard * 256, 256, axis=2)
    gsmall['conv_w'] = lax.dynamic_slice_in_dim(gsmall['conv_w'], shard * 512, 512, axis=2)

    red[pending[0]] = comm.reduce_c(handle_b, packed)
    gout = {n: jnp.stack([red[i][n] for i in range(NL)]) for n in BIG}
    gout.update(gsmall)

    delta, new_m, new_v = {}, {}, {}
    for n in BIG:
        shp = inp[n].shape
        two_d = lambda a: a.reshape(shp[0] * shp[1], shp[2])
        d, m, v = _adamw(two_d(inp[n]), two_d(gout[n]), two_d(inp['m_' + n]), two_d(inp['v_' + n]), name="adamw_big")
        delta[n], new_m[n], new_v[n] = d.reshape(shp), m.reshape(shp), v.reshape(shp)
    shapes = [inp[n].shape for n in SMALL]
    d, m, v = _adamw(_pack([inp[n] for n in SMALL]), _pack([gout[n] for n in SMALL]),
                     _pack([inp['m_' + n] for n in SMALL]), _pack([inp['v_' + n] for n in SMALL]), name="adamw_small")
    for n, dd, mm, vv in zip(SMALL, _unpack(d, shapes), _unpack(m, shapes), _unpack(v, shapes)):
        delta[n], new_m[n], new_v[n] = dd, mm, vv

    return (loss, gx.reshape(x.shape), *[gout[n] for n in WEIGHTS], *[delta[n] for n in WEIGHTS],
            *[new_m[n] for n in WEIGHTS], *[new_v[n] for n in WEIGHTS])
```

```python
import functools

import numpy as np
import jax
import jax.numpy as jnp
from jax import lax
from jax.experimental import pallas as pl
from jax.experimental.pallas import tpu as pltpu

F32 = jnp.float32
BF16 = jnp.bfloat16
_MXU = jnp.bfloat16
_ACT = jnp.bfloat16
_VMEM_LIMIT = 56 * 1024 * 1024

S = 2048
D = 1024
NL = 4
DFF = 2816
LN_EPS = 1e-5
SSD_EPS = 1e-5
ALPHA = (2.0 * NL) ** 0.25
POOLW = 768
POOL_WINDOWS = (2, 4, 8, 16)
POOL_GDIM = 192
CH = 128
ATTN_DILS = (1, 4, 16)
HC = 9728
O_U, O_Z, O_XBC, O_Q, O_K, O_V, O_G, O_DT = 0, 768, 1792, 3840, 4608, 5376, 6144, 9216

ADAM_LR, ADAM_B1, ADAM_B2, ADAM_EPS, ADAM_WD, ADAM_STEP = 0.001, 0.9, 0.999, 1e-08, 0.01, 10

WEIGHTS = ['ffn1_w13', 'ffn1_w2', 'ln1_g', 'ln1_b', 'w_in', 'gate_b', 'pool_w', 'pool_b', 'pool_scale', 'conv_w',
           'conv_b', 'dt_bias', 'a_log', 'd_skip', 'ssd_norm', 'rel_bias', 'p_pool', 'p_ssd', 'p_attn', 'w_out',
           'ln2_g', 'ln2_b', 'ffn2_w13', 'ffn2_w2', 'ln3_g', 'ln3_b']
BIG = ['ffn1_w13', 'ffn1_w2', 'w_in', 'p_pool', 'p_ssd', 'p_attn', 'w_out', 'ffn2_w13', 'ffn2_w2']
COL_SHARDED = {'ffn1_w13', 'ffn2_w13', 'w_in', 'p_pool', 'p_attn'}
SMALL = [n for n in WEIGHTS if n not in BIG]


def _pcall(body, **kw):
    return pl.pallas_call(body, **kw)


def _cp(sem=None):
    return pltpu.CompilerParams(dimension_semantics=sem, vmem_limit_bytes=_VMEM_LIMIT)


def _pick(n, cands):
    for c in cands:
        if n % c == 0:
            return c
    raise ValueError(f"no tile for {n}")


def _mm(a, b, *, ta=False, tb=False, add=None, out_dtype=F32, dep=None, name):
    if ta:
        K, M = a.shape
    else:
        M, K = a.shape
    if tb:
        N, K2 = b.shape
    else:
        K2, N = b.shape
    assert K == K2, (a.shape, b.shape, ta, tb)
    sa, sb, so = a.dtype.itemsize, b.dtype.itemsize, jnp.dtype(out_dtype).itemsize
    tm, tn, tk = _mm_tiles(M, N, K, sa, sb, so + (4 if add is not None else 0))
    nk = K // tk
    a_bytes, b_bytes = M * K * sa, K * N * sb
    j_outer = nk == 1 and (b_bytes + a_bytes * (N // tn) < a_bytes + b_bytes * (M // tm))
    ij = (lambda p, q: (q, p)) if j_outer else (lambda p, q: (p, q))

    def im(f):
        return lambda p, q, k: f(*ij(p, q), k)

    a_spec = pl.BlockSpec((tk, tm), im(lambda i, j, k: (k, i))) if ta else pl.BlockSpec((tm, tk), im(lambda i, j, k: (i, k)))
    b_spec = pl.BlockSpec((tn, tk), im(lambda i, j, k: (j, k))) if tb else pl.BlockSpec((tk, tn), im(lambda i, j, k: (k, j)))
    o_spec = pl.BlockSpec((tm, tn), im(lambda i, j, k: (i, j)))
    dims = (((0 if ta else 1,), (1 if tb else 0,)), ((), ()))
    has_add = add is not None

    n_in = 2 + int(has_add) + int(dep is not None)

    def body(*refs):
        a_ref, b_ref = refs[0], refs[1]
        add_ref = refs[2] if has_add else None
        o_ref = refs[n_in]
        part = lax.dot_general(a_ref[...].astype(_MXU), b_ref[...].astype(_MXU), dims, preferred_element_type=F32)

        def finish(r):
            if has_add:
                r = r + add_ref[...]
            o_ref[...] = r.astype(out_dtype)

        if nk == 1:
            finish(part)
        else:
            acc = refs[-1]
            k = pl.program_id(2)

            @pl.when(k == 0)
            def _():
                acc[...] = part

            @pl.when(k > 0)
            def _():
                acc[...] += part

            @pl.when(k == nk - 1)
            def _():
                finish(acc[...])

    in_specs = [a_spec, b_spec]
    args = [a, b]
    if has_add:
        in_specs.append(o_spec)
        args.append(add)
    if dep is not None:
        in_specs.append(pl.BlockSpec(memory_space=pl.ANY))
        args.append(dep)
    gm, gn = M // tm, N // tn
    return _pcall(
        body, name=name, grid=((gn, gm, nk) if j_outer else (gm, gn, nk)), in_specs=in_specs, out_specs=o_spec,
        out_shape=jax.ShapeDtypeStruct((M, N), out_dtype),
        scratch_shapes=([pltpu.VMEM((tm, tn), F32)] if nk > 1 else []),
        compiler_params=_cp(("parallel", "parallel", "arbitrary")),
    )(*args)


_MM_VMEM_BUDGET = 40 * 1024 * 1024


def _divisors128(n, cap):
    return [d for d in range(128, min(n, cap) + 1, 128) if n % d == 0][::-1]


_MM_CYC_PER_MMAC = 4.35
_MM_CYC_PER_ACC_VREG = 2.03
_MM_HBM_BYTES_PER_CYC = 1455.0
_MM_CYC_PER_STEP = 770.0


def _mm_tiles(M, N, K, sa, sb, so):
    best = None
    for tm in _divisors128(M, 1408):
        for tn in _divisors128(N, 2560):
            for tk in ([K] if K <= 4096 else []) + _divisors128(K, 2816):
                nk = K // tk
                need = 2 * (tm * tk * sa + tk * tn * sb + tm * tn * so) + (tm * tn * 4 if nk > 1 else 0)
                need += tm * tk * 2 + tk * tn * 2 + tm * tn * 4
                if need > _MM_VMEM_BUDGET:
                    continue
                gm, gn = M // tm, N // tn
                a_bytes, b_bytes = M * K * sa, K * N * sb
                hbm = min(b_bytes + a_bytes * gn, a_bytes + b_bytes * gm) if nk == 1 else a_bytes * gn + b_bytes * gm
                hbm += M * N * so
                work = _MM_CYC_PER_MMAC * M * N * K / 1e6 + _MM_CYC_PER_ACC_VREG * (M * N / 1024) * (nk if nk > 1 else 0.5)
                cost = max(work, hbm / _MM_HBM_BYTES_PER_CYC) + gm * gn * nk * _MM_CYC_PER_STEP
                if best is None or cost < best[0]:
                    best = (cost, (tm, tn, tk))
    assert best is not None, (M, N, K)
    return best[1]


def _store(ref, val):
    if isinstance(val, (list, tuple)):
        off = 0
        for p in val:
            w = p.shape[1]
            ref[:, off:off + w] = p.astype(ref.dtype)
            off += w
    else:
        ref[...] = val.astype(ref.dtype)


def _acc_store(ref, val, first):
    pieces = val if isinstance(val, (list, tuple)) else [val]
    off = 0
    for p in pieces:
        w = p.shape[1]

        @pl.when(first)
        def _(p=p, off=off, w=w):
            ref[:, off:off + w] = p

        @pl.when(jnp.logical_not(first))
        def _(p=p, off=off, w=w):
            ref[:, off:off + w] += p

        off += w


def _rowwise(fn, tiled, full, out_tiled, out_acc, *, name, tm=256):
    arrs, specs = [], []
    for t in tiled:
        arr, w, cb = t if isinstance(t, tuple) else (t, t.shape[1], 0)
        arrs.append(arr)
        specs.append(pl.BlockSpec((tm, w), functools.partial(lambda i, cb: (i, cb), cb=cb)))
    R = arrs[0].shape[0]
    assert R % tm == 0
    for f in full:
        arrs.append(f)
        specs.append(pl.BlockSpec(f.shape, functools.partial(lambda i, nd: (0,) * nd, nd=f.ndim)))
    nt, nf, no = len(tiled), len(full), len(out_tiled)

    def body(*refs):
        tv = [r[...] for r in refs[:nt]]
        fv = [r[...] for r in refs[nt:nt + nf]]
        ot, oa = fn(tv, fv)
        for r, v in zip(refs[nt + nf:nt + nf + no], ot):
            _store(r, v)
        first = pl.program_id(0) == 0
        for r, v in zip(refs[nt + nf + no:], oa):
            _acc_store(r, v, first)

    out_shape = [jax.ShapeDtypeStruct((R, c), dt) for c, dt in out_tiled]
    out_specs = [pl.BlockSpec((tm, c), lambda i: (i, 0)) for c, _ in out_tiled]
    for shp in out_acc:
        out_shape.append(jax.ShapeDtypeStruct(shp, F32))
        out_specs.append(pl.BlockSpec(shp, lambda i: (0, 0)))
    return _pcall(body, name=name, grid=(R // tm,), in_specs=specs, out_specs=out_specs, out_shape=out_shape,
                  compiler_params=_cp(("arbitrary",)))(*arrs)


def _group(group):
    x, y, c = lax.axis_index("x"), lax.axis_index("y"), lax.axis_index("c")
    if group == "chips":
        return 2 * x + y, [((x, 1 - y, c), 2 * x + 1 - y), ((1 - x, y, c), 2 * (1 - x) + y),
                           ((1 - x, 1 - y, c), 2 * (1 - x) + 1 - y)]
    if group == "cores":
        return c, [((x, y, 1 - c), 1 - c)]
    if group == "x":
        return x, [((1 - x, y, c), 1 - x)]
    return y, [((x, 1 - y, c), 1 - y)]


def _exchange(arrs, group, mode, name):
    chips = group == "chips"
    k = len(arrs)
    npeer = 3 if chips else 1

    def body(*refs):
        ins, outs = refs[:k], refs[k:2 * k]
        send_sems, recv_sems = refs[2 * k:]
        me, peers = _group(group)
        remote = []
        for i in range(k):
            for p, (dev, slot) in enumerate(peers):
                src = ins[i].at[slot] if mode == "scatter" else ins[i]
                if not chips:
                    dst = outs[i]
                else:
                    dst = outs[i].at[p] if mode == "scatter" else outs[i].at[me]
                cp = pltpu.make_async_remote_copy(src_ref=src, dst_ref=dst, send_sem=send_sems.at[i, p],
                                                  recv_sem=recv_sems.at[i, p], device_id=dev,
                                                  device_id_type=pl.DeviceIdType.MESH)
                cp.start()
                remote.append(cp)
        for cp in remote:
            cp.wait_recv()
        for cp in remote:
            cp.wait_send()

    def oshape(a):
        piece = a.shape[1:] if mode == "scatter" else a.shape
        if chips:
            piece = ((3,) if mode == "scatter" else (4,)) + piece
        return jax.ShapeDtypeStruct(piece, a.dtype)

    any_spec = pl.BlockSpec(memory_space=pl.ANY)
    return _pcall(body, name=name, in_specs=[any_spec] * k, out_specs=[any_spec] * k, out_shape=[oshape(a) for a in arrs],
                  scratch_shapes=[pltpu.SemaphoreType.DMA((k, npeer)), pltpu.SemaphoreType.DMA((k, npeer))])(*arrs)


def _split_copies(ins, lands, send_sems, recv_sems, group, mode):
    chips = group == "chips"
    me, peers = _group(group)
    npeer = len(peers)
    out = []
    for i in range(len(ins)):
        for p, (dev, slot) in enumerate(peers):
            src = ins[i].at[slot] if mode == "scatter" else ins[i]
            if not chips:
                dst = lands[i]
            else:
                dst = lands[i].at[p] if mode == "scatter" else lands[i].at[me]
            out.append(pltpu.make_async_remote_copy(src_ref=src, dst_ref=dst, send_sem=send_sems.at[npeer * i + p],
                                                    recv_sem=recv_sems.at[npeer * i + p], device_id=dev,
                                                    device_id_type=pl.DeviceIdType.MESH))
    return out


def _exchange_start(arrs, group, mode, name):
    k = len(arrs)
    chips = group == "chips"
    nsem = (3 if chips else 1) * k
    hbm = pl.BlockSpec(memory_space=pltpu.HBM)
    sem = pl.BlockSpec(memory_space=pltpu.SEMAPHORE)

    def land_shape(a):
        piece = a.shape[1:] if mode == "scatter" else a.shape
        if chips:
            piece = ((3,) if mode == "scatter" else (4,)) + piece
        return piece

    def body(*refs):
        ins, lands = refs[:k], refs[k:2 * k]
        send_sems, recv_sems = refs[2 * k], refs[2 * k + 1]
        token = refs[-1]
        for cp in _split_copies(ins, lands, send_sems, recv_sems, group, mode):
            cp.start()
        token[...] = jnp.zeros_like(token)

    srcs = [pltpu.with_memory_space_constraint(a, pltpu.HBM) for a in arrs]
    lands = [pltpu.with_memory_space_constraint(lax.empty(land_shape(a), a.dtype), pltpu.HBM) for a in arrs]
    out_shape = ([pltpu.SemaphoreType.DMA((nsem,)), pltpu.SemaphoreType.DMA((nsem,))]
                 + [pltpu.HBM(a.shape, a.dtype) for a in arrs] + [pltpu.HBM(land_shape(a), a.dtype) for a in arrs]
                 + [jax.ShapeDtypeStruct((8, 128), F32)])
    outs = _pcall(body, name=name, in_specs=[hbm] * (2 * k),
                  out_specs=[sem, sem] + [hbm] * (2 * k) + [pl.BlockSpec(memory_space=pltpu.VMEM)], out_shape=out_shape,
                  input_output_aliases={i: 2 + i for i in range(2 * k)},
                  compiler_params=pltpu.CompilerParams(has_side_effects=pltpu.SideEffectType.DATAFLOW_SIDE_EFFECTING))(
                      *srcs, *lands)
    return (outs[0], outs[1], list(outs[2:2 + k]), list(outs[2 + k:2 + 2 * k])), outs[-1]


def _exchange_wait(state, after, group, mode, name):
    send_sems, recv_sems, srcs, lands = state
    k = len(srcs)
    hbm = pl.BlockSpec(memory_space=pltpu.HBM)
    sem = pl.BlockSpec(memory_space=pltpu.SEMAPHORE)

    def body(*refs):
        ins, lnd = refs[:k], refs[k:2 * k]
        send_sems, recv_sems = refs[2 * k], refs[2 * k + 1]
        for cp in _split_copies(ins, lnd, send_sems, recv_sems, group, mode):
            cp.wait_send()
            cp.wait_recv()

    outs = _pcall(body, name=name, in_specs=[hbm] * (2 * k) + [sem, sem, pl.BlockSpec(memory_space=pl.ANY)],
                  out_specs=[hbm] * (2 * k),
                  out_shape=[pltpu.HBM(a.shape, a.dtype) for a in srcs] + [pltpu.HBM(a.shape, a.dtype) for a in lands],
                  input_output_aliases={i: i for i in range(2 * k)},
                  compiler_params=pltpu.CompilerParams(has_side_effects=pltpu.SideEffectType.DATAFLOW_SIDE_EFFECTING))(
                      *srcs, *lands, send_sems, recv_sems, after)
    return list(outs[:k]), list(outs[k:])


def _sum_own_recv(own, recv, me, out_dtype, name):
    n, R, C = own.shape
    nr = 1 if recv.ndim == 2 else recv.shape[0]
    tr = _pick(R, (256, 128, 64, 32, 16, 8))

    def body(me_ref, own_ref, *refs):
        o_ref = refs[-1]
        acc = own_ref[...].astype(F32)
        for r in refs[:-1]:
            acc = acc + r[...].astype(F32)
        o_ref[...] = acc.astype(out_dtype)

    specs = [pl.BlockSpec((None, tr, C), lambda i, me_ref: (me_ref[0], i, 0))]
    args = [own]
    if recv.ndim == 2:
        specs.append(pl.BlockSpec((tr, C), lambda i, me_ref: (i, 0)))
        args.append(recv)
    else:
        for p in range(nr):
            specs.append(pl.BlockSpec((None, tr, C), functools.partial(lambda i, me_ref, p: (p, i, 0), p=p)))
            args.append(recv)
    gs = pltpu.PrefetchScalarGridSpec(num_scalar_prefetch=1, grid=(R // tr,), in_specs=specs,
                                      out_specs=pl.BlockSpec((tr, C), lambda i, me_ref: (i, 0)))
    return _pcall(body, name=name, grid_spec=gs, out_shape=jax.ShapeDtypeStruct((R, C), out_dtype),
                  compiler_params=_cp(("parallel",)))(me, *args)


def _silu(x):
    return x * jax.nn.sigmoid(x)


def _ln(r, g, b):
    mu = jnp.mean(r, -1, keepdims=True)
    xc = r - mu
    var = jnp.mean(xc * xc, -1, keepdims=True)
    return xc * lax.rsqrt(var + LN_EPS) * g + b


def _softplus(x):
    return jnp.maximum(x, 0.0) + jnp.log1p(jnp.exp(-jnp.abs(x)))


def _res_ln_fwd(x, y, g, b, res, name):
    def fn(tv, fv):
        r = ALPHA * tv[0] + res * tv[1]
        out = _ln(r, fv[0], fv[1])
        return [r, out, out], []
    return _rowwise(fn, [x, y], [g, b], [(D, F32), (D, F32), (D, _ACT)], [], name=name)


def _ln_bwd(r, g, b, dout, res, name):
    def fn(tv, fv):
        _, vjp = jax.vjp(_ln, tv[0], fv[0], fv[1])
        dr, dg, db = vjp(tv[1])
        return [ALPHA * dr, res * dr], [dg, db]
    return _rowwise(fn, [r, dout], [g, b], [(D, F32), (D, _ACT)], [(1, D), (1, D)], name=name)


def _swiglu_act(h, name):
    def fn(tv, fv):
        return [_silu(tv[0]) * tv[1]], []
    return _rowwise(fn, [(h, DFF, 0), (h, DFF, 1)], [], [(DFF, _ACT)], [], name=name)[0]


def _swiglu_act_bwd(h, ds, name):
    def fn(tv, fv):
        s, vjp = jax.vjp(lambda a, g: _silu(a) * g, tv[0], tv[1])
        da, dg = vjp(tv[2])
        return [[da, dg], s], []
    return _rowwise(fn, [(h, DFF, 0), (h, DFF, 1), ds], [], [(2 * DFF, _ACT), (DFF, _ACT)], [], name=name)


def _loss_fwd_bwd(y, tgt, name):
    def fn(tv, fv):
        e = tv[0] - tv[1]
        row = jnp.sum(e * e, axis=1, keepdims=True)
        tot = jnp.sum(row, axis=0, keepdims=True) * (0.5 / D)
        return [e * (1.0 / D)], [jnp.broadcast_to(tot, (1, 128))]
    return _rowwise(fn, [y, tgt], [], [(D, F32)], [(1, 128)], name=name)


def _shift_down(x, k, row):
    return jnp.where(row >= k, pltpu.roll(x, k, axis=0), 0.0)


def _shift_up(x, k, row):
    n = x.shape[0]
    return jnp.where(row < n - k, pltpu.roll(x, n - k, axis=0), 0.0)


def _pool_window_masks(j):
    lane = lax.broadcasted_iota(jnp.int32, (1, 128), 1) + j * 128
    grp = lane // POOL_GDIM
    return [grp == g for g in range(4)]


def _pool_mean(u, bwd, name, col0=0):
    T = u.shape[0]
    B = T // S

    def body(u_ref, o_ref):
        j = pl.program_id(1)
        x = u_ref[...]
        row = lax.broadcasted_iota(jnp.int32, (S, 1), 0)
        masks = _pool_window_masks(j)
        inv = [1.0 / jnp.minimum(row + 1, w).astype(F32) for w in POOL_WINDOWS]
        if not bwd:
            s2 = x + _shift_down(x, 1, row)
            s4 = s2 + _shift_down(s2, 2, row)
            s8 = s4 + _shift_down(s4, 4, row)
            s16 = s8 + _shift_down(s8, 8, row)
            mean = jnp.where(masks[0], s2 * inv[0], jnp.where(masks[1], s4 * inv[1],
                             jnp.where(masks[2], s8 * inv[2], s16 * inv[3])))
            o_ref[...] = (mean - x).astype(o_ref.dtype)
        else:
            g = [jnp.where(masks[i], x * inv[i], 0.0) for i in range(4)]
            t = g[3]
            t = t + _shift_up(t, 8, row) + g[2]
            t = t + _shift_up(t, 4, row) + g[1]
            t = t + _shift_up(t, 2, row) + g[0]
            t = t + _shift_up(t, 1, row)
            o_ref[...] = (t - x).astype(o_ref.dtype)

    spec = pl.BlockSpec((S, 128), lambda b, j: (b, j))
    return _pcall(body, name=name, grid=(B, POOLW // 128),
                  in_specs=[pl.BlockSpec((S, 128), lambda b, j: (b, j + col0))], out_specs=spec,
                  out_shape=jax.ShapeDtypeStruct((T, POOLW), _ACT), compiler_params=_cp(("parallel", "parallel")))(u)


def _conv_silu(xbc, w, b, name, col0=0):
    T, C = xbc.shape[0], w.shape[1]
    B = T // S

    def body(x_ref, w_ref, b_ref, o_ref):
        x = x_ref[...]
        row = lax.broadcasted_iota(jnp.int32, (S, 1), 0)
        c = b_ref[...] + w_ref[3:4, :] * x
        for s in range(1, 4):
            c = c + w_ref[3 - s:4 - s, :] * _shift_down(x, s, row)
        o_ref[...] = _silu(c)

    return _pcall(body, name=name, grid=(B, C // 128),
                  in_specs=[pl.BlockSpec((S, 128), lambda b, j: (b, j + col0)), pl.BlockSpec((4, 128), lambda b, j: (0, j)),
                            pl.BlockSpec((1, 128), lambda b, j: (0, j))],
                  out_specs=pl.BlockSpec((S, 128), lambda b, j: (b, j)),
                  out_shape=jax.ShapeDtypeStruct((T, C), F32), compiler_params=_cp(("parallel", "parallel")))(xbc, w, b)


def _conv_silu_bwd(xbc, w, b, dact, name, col0=0):
    T, C = xbc.shape[0], w.shape[1]
    B = T // S

    def body(x_ref, w_ref, b_ref, d_ref, dx_ref, dw_ref, db_ref):
        bi = pl.program_id(1)
        x = x_ref[...]
        row = lax.broadcasted_iota(jnp.int32, (S, 1), 0)
        xs = [x] + [_shift_down(x, s, row) for s in range(1, 4)]
        c = b_ref[...]
        for s in range(4):
            c = c + w_ref[3 - s:4 - s, :] * xs[s]
        _, vjp = jax.vjp(_silu, c)
        dc = vjp(d_ref[...])[0]
        dx = w_ref[3:4, :] * dc
        for s in range(1, 4):
            dx = dx + w_ref[3 - s:4 - s, :] * _shift_up(dc, s, row)
        dx_ref[...] = dx.astype(dx_ref.dtype)
        first = bi == 0
        for s in range(4):
            _acc_rows(dw_ref, 3 - s, jnp.sum(dc * xs[s], axis=0, keepdims=True), first)
        _acc_rows(db_ref, 0, jnp.sum(dc, axis=0, keepdims=True), first)

    blk = pl.BlockSpec((S, 128), lambda j, b: (b, j))
    return _pcall(body, name=name, grid=(C // 128, B),
                  in_specs=[pl.BlockSpec((S, 128), lambda j, b: (b, j + col0)), pl.BlockSpec((4, 128), lambda j, b: (0, j)),
                            pl.BlockSpec((1, 128), lambda j, b: (0, j)), blk],
                  out_specs=[blk, pl.BlockSpec((4, 128), lambda j, b: (0, j)), pl.BlockSpec((1, 128), lambda j, b: (0, j))],
                  out_shape=[jax.ShapeDtypeStruct((T, C), _ACT), jax.ShapeDtypeStruct((4, C), F32),
                             jax.ShapeDtypeStruct((1, C), F32)],
                  compiler_params=_cp(("parallel", "arbitrary")))(xbc, w, b, dact)


def _acc_rows(ref, r, val, first):
    @pl.when(first)
    def _():
        ref[r:r + 1, :] = val

    @pl.when(jnp.logical_not(first))
    def _():
        ref[r:r + 1, :] += val


def _tri_consts():
    i = lax.broadcasted_iota(jnp.int32, (CH, CH), 0)
    j = lax.broadcasted_iota(jnp.int32, (CH, CH), 1)
    return (i == j).astype(F32), (j <= i).astype(F32), (i <= j).astype(F32), i >= j


def _ssd_chunk(h, x, dt, Bm, Cm, a, dsk, consts):
    eye, tril, triu, lower = consts
    Bb = Bm.astype(_MXU)
    Cb = Cm.astype(_MXU)
    cb = lax.dot_general(Cb, Bb, (((1,), (1,)), ((), ())), preferred_element_type=F32)
    ys, hn = [], []
    for e in range(4):
        adt = dt[e] * a[e]
        adt_row = jnp.sum(adt * eye, axis=0, keepdims=True)
        cs_col = jnp.sum(adt_row * tril, axis=1, keepdims=True)
        cs_row = jnp.sum(adt * triu, axis=0, keepdims=True)
        cs_last = jnp.sum(adt, axis=0, keepdims=True)
        decay = jnp.exp(jnp.where(lower, cs_col - cs_row, -jnp.inf))
        xb = (x[e] * dt[e]).astype(_MXU)
        y_diag = jnp.dot((cb * decay).astype(_MXU), xb, preferred_element_type=F32)
        bdec = (Bm * jnp.exp(cs_last - cs_col)).astype(_MXU)
        st = lax.dot_general(bdec, xb, (((0,), (0,)), ((), ())), preferred_element_type=F32)
        hn.append(h[e] * jnp.exp(cs_last) + st)
        y_off = jnp.exp(cs_col) * jnp.dot(Cb, h[e].astype(_MXU), preferred_element_type=F32)
        ys.append(y_diag + y_off + dsk[e] * x[e])
    return ys, hn


def _ssd_specs(order):
    def im(f):
        return lambda p, q: f(*order(p, q))
    xs = pl.BlockSpec((S, 256), im(lambda b, g: (b, g)))
    dt = pl.BlockSpec((None, S, 4), im(lambda b, g: (g, b, 0)))
    bc = pl.BlockSpec((S, 128), im(lambda b, g: (b, g)))
    hd = pl.BlockSpec((None, 1, 4), im(lambda b, g: (g, 0, 0)))
    hs = pl.BlockSpec((None, None, S // CH, 4, 128, 64), im(lambda b, g: (b, g, 0, 0, 0, 0)))
    bw = pl.BlockSpec((S, 128), im(lambda b, g: (b, 8 + g)))
    cw = pl.BlockSpec((S, 128), im(lambda b, g: (b, 12 + g)))
    return xs, dt, bc, hd, hs, bw, cw


def _ssd_fwd(act, dtg, a, dsk, name):
    xs = bm = cm = act
    T = xs.shape[0]
    B = T // S
    nc = S // CH

    def body(x_ref, dt_ref, b_ref, c_ref, a_ref, k_ref, y_ref, hs_ref, h_ref):
        consts = _tri_consts()
        h_ref[...] = jnp.zeros_like(h_ref)
        al = [a_ref[:, e:e + 1] for e in range(4)]
        kl = [k_ref[:, e:e + 1] for e in range(4)]

        def step(c, carry):
            r0 = pl.multiple_of(c * CH, CH)
            rows = pl.ds(r0, CH)
            h = [h_ref[e] for e in range(4)]
            for e in range(4):
                hs_ref[c, e] = h[e]
            x = [x_ref[rows, 64 * e:64 * e + 64] for e in range(4)]
            dt = [dt_ref[rows, e:e + 1] for e in range(4)]
            ys, hn = _ssd_chunk(h, x, dt, b_ref[rows, :], c_ref[rows, :], al, kl, consts)
            for e in range(4):
                y_ref[rows, 64 * e:64 * e + 64] = ys[e]
                h_ref[e] = hn[e]
            return carry

        lax.fori_loop(0, nc, step, 0)

    sx, sdt, sbc, shd, shs, sbw, scw = _ssd_specs(lambda b, g: (b, g))
    return _pcall(body, name=name, grid=(B, 4), in_specs=[sx, sdt, sbw, scw, shd, shd], out_specs=[sx, shs],
                  out_shape=[jax.ShapeDtypeStruct((T, 1024), F32), jax.ShapeDtypeStruct((B, 4, nc, 4, 128, 64), F32)],
                  scratch_shapes=[pltpu.VMEM((4, 128, 64), F32)],
                  compiler_params=_cp(("parallel", "parallel")))(xs, dtg, bm, cm, a, dsk)


def _lane_place(vals, width):
    lane = lax.broadcasted_iota(jnp.int32, (1, width), 1)
    out = jnp.zeros((1, width), F32)
    for e, v in enumerate(vals):
        out = out + jnp.where(lane == e, v, 0.0)
    return out


def _ssd_bwd(act, dtg, a, dsk, hs, dy, name):
    xs = bm = cm = act
    T = xs.shape[0]
    B = T // S
    nc = S // CH

    def body(x_ref, dt_ref, b_ref, c_ref, a_ref, k_ref, hs_ref, dy_ref,
             dx_ref, ddt_ref, db_ref, dc_ref, dak_ref, dh_ref, sc_ref):
        bi = pl.program_id(1)
        consts = _tri_consts()
        dh_ref[...] = jnp.zeros_like(dh_ref)
        sc_ref[...] = jnp.zeros_like(sc_ref)
        al = [a_ref[:, e:e + 1] for e in range(4)]
        kl = [k_ref[:, e:e + 1] for e in range(4)]

        def step(i, carry):
            c = nc - 1 - i
            r0 = pl.multiple_of(c * CH, CH)
            rows = pl.ds(r0, CH)
            h = [hs_ref[c, e] for e in range(4)]
            x = [x_ref[rows, 64 * e:64 * e + 64] for e in range(4)]
            dt = [dt_ref[rows, e:e + 1] for e in range(4)]
            f = functools.partial(_ssd_chunk, consts=consts)
            _, vjp = jax.vjp(f, h, x, dt, b_ref[rows, :], c_ref[rows, :], al, kl)
            dys = [dy_ref[rows, 64 * e:64 * e + 64] for e in range(4)]
            dhn = [dh_ref[e] for e in range(4)]
            dh, dx, ddt, dB, dC, da, dk = vjp((dys, dhn))
            for e in range(4):
                dh_ref[e] = dh[e]
                dx_ref[rows, 64 * e:64 * e + 64] = dx[e]
                ddt_ref[rows, e:e + 1] = ddt[e]
            db_ref[rows, :] = dB
            dc_ref[rows, :] = dC
            sc_ref[0:1, :] += _lane_place(da, 128)
            sc_ref[1:2, :] += _lane_place(dk, 128)
            return carry

        lax.fori_loop(0, nc, step, 0)
        first = bi == 0

        @pl.when(first)
        def _():
            dak_ref[...] = sc_ref[...]

        @pl.when(jnp.logical_not(first))
        def _():
            dak_ref[...] += sc_ref[...]

    sx, sdt, sbc, shd, shs, sbw, scw = _ssd_specs(lambda g, b: (b, g))
    return _pcall(body, name=name, grid=(4, B), in_specs=[sx, sdt, sbw, scw, shd, shd, shs, sx],
                  out_specs=[sx, sdt, sbc, sbc, pl.BlockSpec((None, 8, 128), lambda g, b: (g, 0, 0))],
                  out_shape=[jax.ShapeDtypeStruct((T, 1024), F32), jax.ShapeDtypeStruct((4, T, 4), F32),
                             jax.ShapeDtypeStruct((T, 512), F32), jax.ShapeDtypeStruct((T, 512), F32),
                             jax.ShapeDtypeStruct((4, 8, 128), F32)],
                  scratch_shapes=[pltpu.VMEM((4, 128, 64), F32), pltpu.VMEM((8, 128), F32)],
                  compiler_params=_cp(("parallel", "arbitrary")))(xs, dtg, bm, cm, a, dsk, hs, dy)


def _gate_norm(y, z, nw):
    t = y * _silu(z)
    return t * lax.rsqrt(jnp.mean(t * t, axis=-1, keepdims=True) + SSD_EPS) * nw


def _ssd_gate_norm(y, z, nw, name, zcol=0):
    def fn(tv, fv):
        return [[_gate_norm(tv[g], tv[4 + g], fv[0][:, 256 * g:256 * g + 256]) for g in range(4)]], []
    tiled = [(y, 256, g) for g in range(4)] + [(z, 256, zcol + g) for g in range(4)]
    return _rowwise(fn, tiled, [nw], [(1024, _ACT)], [], name=name)[0]


def _ssd_gate_norm_bwd(y, z, nw, dout, name, zcol=0):
    def fn(tv, fv):
        dys, dzs, dns = [], [], []
        for g in range(4):
            _, vjp = jax.vjp(_gate_norm, tv[g], tv[4 + g], fv[0][:, 256 * g:256 * g + 256])
            a, b, c = vjp(tv[8 + g])
            dys.append(a)
            dzs.append(b)
            dns.append(c)
        return [dys, dzs], [dns]
    tiled = [(y, 256, g) for g in range(4)] + [(z, 256, zcol + g) for g in range(4)] + [(dout, 256, g) for g in range(4)]
    return _rowwise(fn, tiled, [nw], [(1024, F32), (1024, _ACT)], [(1, 1024)], name=name)


def _t5_bucket_np(dist):
    dist = np.maximum(dist, 0)
    max_exact = 16
    large = max_exact + (np.log(np.maximum(dist, 1) / max_exact) / np.log(2048 / max_exact) * (32 - max_exact)).astype(np.int32)
    large = np.minimum(large, 31)
    return np.where(dist < max_exact, dist, large).astype(np.int32)


def _bucket_maps():
    qi = np.arange(128)[:, None]
    kj = np.arange(256)[None, :]
    return np.stack([_t5_bucket_np((qi - kj + 128) * dil) for dil in ATTN_DILS]).astype(np.int32)


def _bias_build(rel_bias, maps, name):
    def body(tab_ref, map_ref, o_ref):
        hh = pl.program_id(0)
        m = map_ref[...]
        acc = jnp.zeros((128, 256), F32)
        for b in range(32):
            acc = jnp.where(m == b, tab_ref[b, hh], acc)
        o_ref[...] = acc

    return _pcall(body, name=name, grid=(12,),
                  in_specs=[pl.BlockSpec(memory_space=pltpu.SMEM), pl.BlockSpec((None, 128, 256), lambda h: (h // 4, 0, 0))],
                  out_specs=pl.BlockSpec((None, 128, 256), lambda h: (h, 0, 0)),
                  out_shape=jax.ShapeDtypeStruct((12, 128, 256), F32), compiler_params=_cp(("parallel",)))(rel_bias, maps)


def _bias_reduce(dbias, maps, name):
    nl = dbias.shape[0]

    def body(d_ref, map_ref, o_ref):
        m = map_ref[...]
        d = d_ref[0]
        for i in range(1, nl):
            d = d + d_ref[i]
        lane = lax.broadcasted_iota(jnp.int32, (1, 128), 1)
        out = jnp.zeros((1, 128), F32)
        for b in range(32):
            s = jnp.sum(jnp.sum(jnp.where(m == b, d, 0.0), axis=1, keepdims=True), axis=0, keepdims=True)
            out = out + jnp.where(lane == b, s, 0.0)
        o_ref[...] = out

    return _pcall(body, name=name, grid=(12,),
                  in_specs=[pl.BlockSpec((nl, None, 128, 256), lambda h: (0, h, 0, 0)),
                            pl.BlockSpec((None, 128, 256), lambda h: (h // 4, 0, 0))],
                  out_specs=pl.BlockSpec((None, 1, 128), lambda h: (h, 0, 0)),
                  out_shape=jax.ShapeDtypeStruct((12, 1, 128), F32), compiler_params=_cp(("parallel",)))(dbias, maps)


def _attn_block(q, kb, vb, bias, mask):
    s = lax.dot_general(q.astype(_MXU), kb.astype(_MXU), (((1,), (1,)), ((), ())), preferred_element_type=F32) * 0.125 + bias
    s = jnp.where(mask, s, -jnp.inf)
    m = lax.stop_gradient(jnp.max(s, axis=-1, keepdims=True))
    p = jnp.exp(s - m)
    den = jnp.sum(p, axis=-1, keepdims=True)
    out = jnp.dot((p / den).astype(_MXU), vb.astype(_MXU), preferred_element_type=F32)
    return out, m + jnp.log(den)


def _band_mask():
    qi = lax.broadcasted_iota(jnp.int32, (128, 256), 0)
    kj = lax.broadcasted_iota(jnp.int32, (128, 256), 1)
    return (kj >= qi) & (kj <= qi + 128)


def _qkv_specs(gi, order):
    def spec(base):
        col = (base + 256 * gi) // 128
        return pl.BlockSpec((S, 128), lambda p, q: (order(p, q)[0], col + order(p, q)[1]))
    return [spec(O_Q), spec(O_K), spec(O_V)]


def _residue_rows(r, dil):
    return pl.ds(r, S // dil, stride=dil)


def _attn_fwd(hcat, bias_all, gi, name):
    dil = ATTN_DILS[gi]
    T = hcat.shape[0]
    B, L = T // S, S // dil
    nb = L // 128

    def body(q_ref, k_ref, v_ref, b_ref, o_ref, l_ref, *scr):
        mask = _band_mask()
        if dil > 1:
            qd, kd, vd, od, ld = scr
            for r in range(dil):
                rows = _residue_rows(r, dil)
                qd[r] = q_ref[rows, :]
                kd[r] = k_ref[rows, :]
                vd[r] = v_ref[rows, :]
            ld[...] = jnp.zeros_like(ld)
            rd = lambda ref, sref, r, rows, lanes: sref[r, rows, lanes]
        else:
            qd = kd = vd = od = ld = None
            l_ref[...] = jnp.zeros_like(l_ref)
            rd = lambda ref, sref, r, rows, lanes: ref[rows, lanes]

        def put(r, rows, e, o, l):
            if dil > 1:
                od[r, rows, 64 * e:64 * e + 64] = o
                ld[r, rows, e:e + 1] = l
            else:
                o_ref[rows, 64 * e:64 * e + 64] = o
                l_ref[rows, e:e + 1] = l

        for e in range(2):
            lanes = slice(64 * e, 64 * e + 64)
            bias_v = b_ref[e]

            def unit(r, carry, lanes=lanes, bias_v=bias_v, e=e):
                first = pl.ds(0, 128)
                o, l = _attn_block(rd(q_ref, qd, r, first, lanes), rd(k_ref, kd, r, first, lanes),
                                   rd(v_ref, vd, r, first, lanes), bias_v[:, 128:], mask[:, 128:])
                put(r, first, e, o, l)
                if nb > 1:
                    def step(n, c):
                        cur = pl.ds(pl.multiple_of(n * 128, 128), 128)
                        band = pl.ds(pl.multiple_of(n * 128 - 128, 128), 256)
                        o, l = _attn_block(rd(q_ref, qd, r, cur, lanes), rd(k_ref, kd, r, band, lanes),
                                           rd(v_ref, vd, r, band, lanes), bias_v, mask)
                        put(r, cur, e, o, l)
                        return c
                    lax.fori_loop(1, nb, step, 0)
                return carry

            if dil > 1:
                lax.fori_loop(0, dil, unit, 0)
            else:
                unit(0, 0)
        if dil > 1:
            for r in range(dil):
                rows = _residue_rows(r, dil)
                o_ref[rows, :] = od[r]
                l_ref[rows, :] = ld[r]

    scratch = [pltpu.VMEM((dil, L, 128), F32)] * 5 if dil > 1 else []
    return _pcall(body, name=name, grid=(B, 2),
                  in_specs=_qkv_specs(gi, lambda b, hp: (b, hp))
                  + [pl.BlockSpec((2, 128, 256), lambda b, hp: (2 * gi + hp, 0, 0))],
                  out_specs=[pl.BlockSpec((S, 128), lambda b, hp: (b, hp)),
                             pl.BlockSpec((None, S, 128), lambda b, hp: (hp, b, 0))],
                  out_shape=[jax.ShapeDtypeStruct((T, 256), F32), jax.ShapeDtypeStruct((2, T, 128), F32)],
                  scratch_shapes=scratch,
                  compiler_params=_cp(("parallel", "parallel")))(hcat, hcat, hcat, bias_all)


def _attn_bwd(hcat, bias_all, gi, do, dl, name):
    dil = ATTN_DILS[gi]
    T = hcat.shape[0]
    B, L = T // S, S // dil
    nb = L // 128

    def body(q_ref, k_ref, v_ref, b_ref, do_ref, dl_ref, dq_ref, dk_ref, dv_ref, db_ref, acc_ref, *scr):
        bi = pl.program_id(1)
        mask = _band_mask()
        if dil > 1:
            qd, kd, vd, dod, dld, dqd, dkd, dvd = scr
            for r in range(dil):
                rows = _residue_rows(r, dil)
                qd[r] = q_ref[rows, :]
                kd[r] = k_ref[rows, :]
                vd[r] = v_ref[rows, :]
                dod[r] = do_ref[rows, :]
                dld[r] = dl_ref[rows, :]
            srcs = (qd, kd, vd, dod, dld)
            dsts = (dqd, dkd, dvd)
            rd = lambda i, r, rows, lanes: srcs[i][r, rows, lanes]
        else:
            srcs = (q_ref, k_ref, v_ref, do_ref, dl_ref)
            dsts = (dq_ref, dk_ref, dv_ref)
            rd = lambda i, r, rows, lanes: srcs[i][rows, lanes]
        dsts[1][...] = jnp.zeros_like(dsts[1])
        dsts[2][...] = jnp.zeros_like(dsts[2])

        def at(i, r, rows, lanes):
            return (r, rows, lanes) if dil > 1 else (rows, lanes)

        for e in range(2):
            lanes = slice(64 * e, 64 * e + 64)
            one = slice(e, e + 1)
            bias_v = b_ref[e]
            acc_ref[...] = jnp.zeros_like(acc_ref)

            def unit(r, carry, lanes=lanes, one=one, bias_v=bias_v):
                first = pl.ds(0, 128)
                f0 = functools.partial(_attn_block, mask=mask[:, 128:])
                _, vjp = jax.vjp(f0, rd(0, r, first, lanes), rd(1, r, first, lanes), rd(2, r, first, lanes),
                                 bias_v[:, 128:])
                dq, dkb, dvb, dbs = vjp((rd(3, r, first, lanes), rd(4, r, first, one)))
                dsts[0][at(0, r, first, lanes)] = dq
                dsts[1][at(1, r, first, lanes)] += dkb
                dsts[2][at(2, r, first, lanes)] += dvb
                acc_ref[:, 128:256] += dbs
                if nb > 1:
                    f1 = functools.partial(_attn_block, mask=mask)

                    def step(n, c):
                        cur = pl.ds(pl.multiple_of(n * 128, 128), 128)
                        band = pl.ds(pl.multiple_of(n * 128 - 128, 128), 256)
                        _, vjp = jax.vjp(f1, rd(0, r, cur, lanes), rd(1, r, band, lanes), rd(2, r, band, lanes), bias_v)
                        dq, dkb, dvb, dbs = vjp((rd(3, r, cur, lanes), rd(4, r, cur, one)))
                        dsts[0][at(0, r, cur, lanes)] = dq
                        dsts[1][at(1, r, band, lanes)] += dkb
                        dsts[2][at(2, r, band, lanes)] += dvb
                        acc_ref[...] += dbs
                        return c
                    lax.fori_loop(1, nb, step, 0)
                return carry

            if dil > 1:
                lax.fori_loop(0, dil, unit, 0)
            else:
                unit(0, 0)

            @pl.when(bi == 0)
            def _(e=e):
                db_ref[e] = acc_ref[...]

            @pl.when(bi > 0)
            def _(e=e):
                db_ref[e] += acc_ref[...]

        if dil > 1:
            for r in range(dil):
                rows = _residue_rows(r, dil)
                dq_ref[rows, :] = dqd[r]
                dk_ref[rows, :] = dkd[r]
                dv_ref[rows, :] = dvd[r]

    order = lambda hp, b: (b, hp)
    blk = pl.BlockSpec((S, 128), lambda hp, b: (b, hp))
    lblk = pl.BlockSpec((None, S, 128), lambda hp, b: (hp, b, 0))
    sds = jax.ShapeDtypeStruct((T, 256), F32)
    scratch = [pltpu.VMEM((128, 256), F32)] + ([pltpu.VMEM((dil, L, 128), F32)] * 8 if dil > 1 else [])
    return _pcall(body, name=name, grid=(2, B),
                  in_specs=_qkv_specs(gi, order) + [pl.BlockSpec((2, 128, 256), lambda hp, b: (2 * gi + hp, 0, 0)), blk, lblk],
                  out_specs=[blk, blk, blk, pl.BlockSpec((2, 128, 256), lambda hp, b: (hp, 0, 0))],
                  out_shape=[sds, sds, sds, jax.ShapeDtypeStruct((4, 128, 256), F32)],
                  scratch_shapes=scratch,
                  compiler_params=_cp(("parallel", "arbitrary")))(hcat, hcat, hcat, bias_all, do, dl)


def _lse_merge(o0, o1, o2, l0, l1, l2):
    m = lax.stop_gradient(jnp.maximum(jnp.maximum(l0, l1), l2))
    e0, e1, e2 = jnp.exp(l0 - m), jnp.exp(l1 - m), jnp.exp(l2 - m)
    den = e0 + e1 + e2
    return (e0 / den) * o0 + (e1 / den) * o1 + (e2 / den) * o2


def _attn_merge(outs, lses, dy, name):
    T = outs[0].shape[0]
    bwd = dy is not None
    tm = 512

    def body(*refs):
        o_refs, l_refs = refs[:3], refs[3:6]
        if bwd:
            for r in refs[10:13]:
                r[...] = jnp.zeros_like(r)
        for e in range(2):
            lanes = slice(64 * e, 64 * e + 64)
            vals = [r[:, lanes] for r in o_refs] + [r[:, e:e + 1] for r in l_refs]
            if not bwd:
                refs[6][:, lanes] = _lse_merge(*vals).astype(refs[6].dtype)
            else:
                _, vjp = jax.vjp(_lse_merge, *vals)
                g = vjp(refs[6][:, lanes])
                for r, v in zip(refs[7:10], g[:3]):
                    r[:, lanes] = v
                for r, v in zip(refs[10:13], g[3:]):
                    r[:, e:e + 1] = v

    blk = pl.BlockSpec((tm, 128), lambda i, hp: (i, hp))
    lblk = pl.BlockSpec((None, tm, 128), lambda i, hp: (hp, i, 0))
    lsd = jax.ShapeDtypeStruct((2, T, 128), F32)
    if not bwd:
        return _pcall(body, name=name, grid=(T // tm, 2), in_specs=[blk] * 3 + [lblk] * 3, out_specs=blk,
                      out_shape=jax.ShapeDtypeStruct((T, 256), F32),
                      compiler_params=_cp(("parallel", "parallel")))(*outs, *lses)
    return _pcall(body, name=name, grid=(T // tm, 2), in_specs=[blk] * 3 + [lblk] * 3 + [blk],
                  out_specs=[blk] * 3 + [lblk] * 3, out_shape=[jax.ShapeDtypeStruct((T, 256), F32)] * 3 + [lsd] * 3,
                  compiler_params=_cp(("parallel", "parallel")))(*outs, *lses, dy)


def _gmerge(g0, g1, g2, gb, ya, yb, yc):
    return (jax.nn.sigmoid(g0 + gb[:, 0:D]) * ya + jax.nn.sigmoid(g1 + gb[:, D:2 * D]) * yb
            + jax.nn.sigmoid(g2 + gb[:, 2 * D:3 * D]) * yc)


def _gated_merge(gates, gb, ya, yb, yc, name, gcol=0):
    def fn(tv, fv):
        return [_gmerge(tv[0], tv[1], tv[2], fv[0], tv[3], tv[4], tv[5])], []
    return _rowwise(fn, [(gates, D, gcol), (gates, D, gcol + 1), (gates, D, gcol + 2), ya, yb, yc], [gb], [(D, _ACT)], [],
                    name=name)[0]


def _gated_merge_bwd(gates, gb, ya, yb, yc, dm, name, gcol=0):
    def fn(tv, fv):
        _, vjp = jax.vjp(_gmerge, tv[0], tv[1], tv[2], fv[0], tv[3], tv[4], tv[5])
        d0, d1, d2, dgb, da, db, dc = vjp(tv[6])
        return [[d0, d1, d2], da, db, dc], [dgb]
    return _rowwise(fn, [(gates, D, gcol), (gates, D, gcol + 1), (gates, D, gcol + 2), ya, yb, yc, dm], [gb],
                    [(3 * D, _ACT), (D, _ACT), (D, _ACT), (D, _ACT)], [(1, 3 * D)], name=name)


def _pool_affine(t1, pb, ps, dout, name):
    if dout is None:
        def fn(tv, fv):
            return [(tv[0] + fv[0]) * fv[1]], []
        return _rowwise(fn, [t1], [pb, ps], [(POOLW, _ACT)], [], name=name)[0]

    def fnb(tv, fv):
        t2, vjp = jax.vjp(lambda t, b, s: (t + b) * s, tv[0], fv[0], fv[1])
        dt, db, dsc = vjp(tv[1])
        return [dt, t2], [db, dsc]
    return _rowwise(fnb, [t1, dout], [pb, ps], [(POOLW, _ACT), (POOLW, _ACT)], [(1, POOLW), (1, POOLW)], name=name)


def _dt_softplus(dt_raw, dt_bias, ddt, name):
    f = lambda r, b: _softplus(r + b)
    if ddt is None:
        def fn(tv, fv):
            return [f(tv[0], fv[0])], []
        return _rowwise(fn, [dt_raw], [dt_bias], [(16, F32)], [], name=name, tm=1024)[0]

    def fnb(tv, fv):
        _, vjp = jax.vjp(f, tv[0], fv[0])
        dr, db = vjp(tv[1])
        return [dr], [db]
    return _rowwise(fnb, [dt_raw, ddt], [dt_bias], [(16, F32)], [(1, 16)], name=name, tm=1024)


def _adamw(w, g, m, v, name):
    R, C = w.shape
    tm = _pick(R, (256, 128, 64, 32, 16, 8))
    c1 = 1.0 / (1.0 - ADAM_B1 ** ADAM_STEP)
    c2 = 1.0 / (1.0 - ADAM_B2 ** ADAM_STEP)

    def fn(tv, fv):
        wv, gv, mv, vv = tv
        mn = ADAM_B1 * mv + (1.0 - ADAM_B1) * gv
        vn = ADAM_B2 * vv + (1.0 - ADAM_B2) * (gv * gv)
        delta = -ADAM_LR * ((mn * c1) / (jnp.sqrt(vn * c2) + ADAM_EPS) + ADAM_WD * wv)
        return [delta, mn, vn], []
    return _rowwise(fn, [w, g, m, v], [], [(C, F32)] * 3, [], name=name, tm=tm)


def _ffn_fwd(x, xm, w13, w2, g, b, tag, dep=None):
    h = _mm(xm, w13, dep=dep, name=f"{tag}_h")
    s = _swiglu_act(h, name=f"{tag}_act")
    y = _mm(s, w2, name=f"{tag}_y")
    r, out, outm = _res_ln_fwd(x, y, g, b, 0.5, name=f"{tag}_ln")
    return out, outm, dict(x=xm, h=h, r=r)


def _ffn_bwd(dout, sv, w13, w2, g, b, tag, dep=None):
    dskip, dy, dg, db = _ln_bwd(sv['r'], g, b, dout, 0.5, name=f"{tag}_lnb")
    ds = _mm(dy, w2, tb=True, dep=dep, name=f"{tag}_ds")
    dh, s = _swiglu_act_bwd(sv['h'], ds, name=f"{tag}_actb")
    dw2 = _mm(s, dy, ta=True, name=f"{tag}_dw2")
    dw13 = _mm(sv['x'], dh, ta=True, name=f"{tag}_dw13")
    dx = _mm(dh, w13, tb=True, add=dskip, name=f"{tag}_dx")
    return dx, dict(w13=dw13, w2=dw2, g=dg, b=db)


def _mixer_fwd(x1, x1m, W, bias_all, tag, dep=None):
    T = x1.shape[0]
    hcat = _mm(x1m, W['w_in_r'], dep=dep, name=f"{tag}_hcat")
    dt_raw = hcat[:, O_DT:O_DT + 16]
    pooled = _pool_mean(hcat, False, name=f"{tag}_pool", col0=O_U // 128)
    t1 = _mm(pooled, W['pool_wbd'], name=f"{tag}_pt1")
    t2 = _pool_affine(t1, W['pool_b'], W['pool_scale'], None, name=f"{tag}_paff")
    ya = _mm(t2, W['p_pool'], name=f"{tag}_ya")
    act = _conv_silu(hcat, W['conv_w'], W['conv_b'], name=f"{tag}_conv", col0=O_XBC // 128)
    dt = _dt_softplus(dt_raw, W['dt_bias'], None, name=f"{tag}_dt")
    dtg = dt.reshape(T, 4, 4).transpose(1, 0, 2)
    yscan, hs = _ssd_fwd(act, dtg, W['a_neg'], W['d_skip'], name=f"{tag}_ssd")
    ybn = _ssd_gate_norm(yscan, hcat, W['ssd_norm'], name=f"{tag}_gn", zcol=O_Z // 256)
    yb = _mm(ybn, W['p_ssd'], name=f"{tag}_yb")
    outs, lses = [], []
    for gi in range(len(ATTN_DILS)):
        o, l = _attn_fwd(hcat, bias_all, gi, name=f"{tag}_attn{gi}")
        outs.append(o)
        lses.append(l)
    ycp = _attn_merge(outs, lses, None, name=f"{tag}_amerge")
    yc = _mm(ycp, W['p_attn'], name=f"{tag}_yc")
    merged = _gated_merge(hcat, W['gate_b'], ya, yb, yc, name=f"{tag}_gm", gcol=O_G // D)
    mix = _mm(merged, W['w_out'], name=f"{tag}_mix")
    r, out, outm = _res_ln_fwd(x1, mix, W['ln2_g'], W['ln2_b'], 1.0, name=f"{tag}_ln")
    sv = dict(x1=x1m, dt_raw=dt_raw, pooled=pooled, t1=t1, act=act, dtg=dtg,
              hs=hs, yscan=yscan, ybn=ybn, hcat=hcat, outs=outs, lses=lses, ycp=ycp, ya=ya, yb=yb, yc=yc,
              merged=merged, r=r)
    return out, outm, sv


def _mixer_bwd(dout, sv, W, bias_all, tag, dep=None):
    T = dout.shape[0]
    gr = {}
    dx1a, dr, gr['ln2_g'], gr['ln2_b'] = _ln_bwd(sv['r'], W['ln2_g'], W['ln2_b'], dout, 1.0, name=f"{tag}_lnb")
    dmerged = _mm(dr, W['w_out'], tb=True, dep=dep, name=f"{tag}_dmerged")
    gr['w_out'] = _mm(sv['merged'], dr, ta=True, name=f"{tag}_dwout")
    dgates, dya, dyb, dyc, gr['gate_b'] = _gated_merge_bwd(sv['hcat'], W['gate_b'], sv['ya'], sv['yb'], sv['yc'],
                                                           dmerged, name=f"{tag}_gmb", gcol=O_G // D)
    dycp = _mm(dyc, W['p_attn'], tb=True, name=f"{tag}_dycp")
    gr['p_attn'] = _mm(sv['ycp'], dyc, ta=True, name=f"{tag}_dpattn")
    dml = _attn_merge(sv['outs'], sv['lses'], dycp, name=f"{tag}_amergeb")
    dq, dk, dv, dbias = [], [], [], []
    for gi in range(len(ATTN_DILS)):
        a, b, c, d = _attn_bwd(sv['hcat'], bias_all, gi, dml[gi], dml[3 + gi], name=f"{tag}_attnb{gi}")
        dq.append(a)
        dk.append(b)
        dv.append(c)
        dbias.append(d)
    dbias = jnp.concatenate(dbias, axis=0)
    dybn = _mm(dyb, W['p_ssd'], tb=True, name=f"{tag}_dybn")
    gr['p_ssd'] = _mm(sv['ybn'], dyb, ta=True, name=f"{tag}_dpssd")
    dyscan, dz, gr['ssd_norm'] = _ssd_gate_norm_bwd(sv['yscan'], sv['hcat'], W['ssd_norm'], dybn, name=f"{tag}_gnb",
                                                    zcol=O_Z // 256)
    dxs, ddtg, dbm, dcm, dak = _ssd_bwd(sv['act'], sv['dtg'], W['a_neg'], W['d_skip'], sv['hs'], dyscan,
                                        name=f"{tag}_ssdb")
    gr['a_neg'], gr['d_skip'] = dak[:, 0, 0:4], dak[:, 1, 0:4]
    ddt = ddtg.transpose(1, 0, 2).reshape(T, 16)
    ddt_raw, gr['dt_bias'] = _dt_softplus(sv['dt_raw'], W['dt_bias'], ddt, name=f"{tag}_dtb")
    dact = jnp.concatenate([dxs, dbm, dcm], axis=1)
    dxbc, gr['conv_w'], gr['conv_b'] = _conv_silu_bwd(sv['hcat'], W['conv_w'], W['conv_b'], dact, name=f"{tag}_convb",
                                                      col0=O_XBC // 128)
    dt2 = _mm(dya, W['p_pool'], tb=True, name=f"{tag}_dt2")
    dt1, t2, gr['pool_b'], gr['pool_scale'] = _pool_affine(sv['t1'], W['pool_b'], W['pool_scale'], dt2, name=f"{tag}_paffb")
    gr['p_pool'] = _mm(t2, dya, ta=True, name=f"{tag}_dppool")
    dpooled = _mm(dt1, W['pool_wbd'], tb=True, name=f"{tag}_dpooled")
    gr['pool_wbd'] = _mm(sv['pooled'], dt1, ta=True, name=f"{tag}_dpoolw")
    du = _pool_mean(dpooled, True, name=f"{tag}_poolb")
    dhcat = jnp.concatenate([t.astype(_ACT) for t in [du, dz, dxbc] + dq + dk + dv + [dgates, ddt_raw]]
                            + [jnp.zeros((T, HC - O_DT - 16), _ACT)], axis=1)
    dx1 = _mm(dhcat, W['w_in_r'], tb=True, add=dx1a, name=f"{tag}_dx1")
    gr['w_in_r'] = _mm(sv['x1'], dhcat, ta=True, name=f"{tag}_dwin")
    return dx1, gr, dbias


def _prep_layer_weights(i, inp, G):
    W = {}
    for n in BIG:
        if n not in G:
            continue
        g = G[n]
        if n == 'w_in':
            W['w_in_r'] = jnp.concatenate(_nat_pieces(g, 0, 3840) + _nat_pieces(g, 3856, 9232) + _nat_pieces(g, 3840, 3856)
                                          + [jnp.zeros((D, HC - 9232), g.dtype)], axis=1)
        elif n in COL_SHARDED:
            W[n] = jnp.concatenate([g[j] for j in range(4)], axis=1)
        else:
            W[n] = g.reshape(4 * g.shape[1], g.shape[2])
    pw = inp['pool_w'][i].astype(_MXU)
    wbd = jnp.zeros((POOLW, POOLW), _MXU)
    for g in range(4):
        wbd = lax.dynamic_update_slice(wbd, pw[g], (g * POOL_GDIM, g * POOL_GDIM))
    W['pool_wbd'] = wbd
    W['pool_b'] = inp['pool_b'][i].reshape(1, POOLW)
    W['pool_scale'] = inp['pool_scale'][i].reshape(1, POOLW)
    if 'conv_w' in G:
        W['conv_w'] = jnp.concatenate([G['conv_w'][j] for j in range(4)], axis=1)
        W['gate_b'] = jnp.concatenate([G['gate_b'][j][b:b + 1] for b in range(3) for j in range(4)], axis=1)
    W['conv_b'] = inp['conv_b'][i].reshape(1, 2048)
    W['dt_bias'] = inp['dt_bias'][i].reshape(1, 16)
    W['a_neg'] = (-jnp.exp(inp['a_log'][i])).reshape(4, 1, 4)
    W['d_skip'] = inp['d_skip'][i].reshape(4, 1, 4)
    W['ssd_norm'] = inp['ssd_norm'][i].reshape(1, D)
    for n in ('ln1_g', 'ln1_b', 'ln2_g', 'ln2_b', 'ln3_g', 'ln3_b'):
        W[n] = inp[n][i].reshape(1, D)
    return W


GATHER_FIRST = ['ffn1_w13', 'ffn1_w2']
GATHER_REST = [n for n in BIG if n not in GATHER_FIRST] + ['gate_b', 'conv_w']


def _gather_start(inp, i, names):
    core = lax.axis_index("c")
    arrs = []
    for n in names:
        s = inp[n][i]
        if n in BIG:
            s = lax.dynamic_slice_in_dim(s, core * (s.shape[0] // 2), s.shape[0] // 2, axis=0).astype(BF16)
        arrs.append(s)
    state, token = _exchange_start(arrs, "chips", "gather", name="gather_start")
    return (names, state), token


def _gather_mid(handle, after):
    names, state = handle
    me = 2 * lax.axis_index("x") + lax.axis_index("y")
    own, outs = _exchange_wait(state, after, "chips", "gather", name="gather_wait")
    outs = [lax.dynamic_update_slice(o, a[None], (me, 0, 0)) for o, a in zip(outs, own)]
    big = [o for n, o in zip(names, outs) if n in BIG]
    state, token = _exchange_start(big, "cores", "gather", name="share_start")
    return (names, outs, state), token


def _gather_finish(handle, after):
    names, outs, state = handle
    core = lax.axis_index("c")
    mine, theirs = _exchange_wait(state, after, "cores", "gather", name="share_wait")
    G = {n: o for n, o in zip(names, outs) if n not in BIG}
    for n, a, b in zip([n for n in names if n in BIG], mine, theirs):
        G[n] = jnp.concatenate([jnp.where(core == 0, a, b), jnp.where(core == 0, b, a)], axis=1)
    return G


W_IN_SHARD = 2308


def _nat_pieces(g, lo, hi):
    out = []
    for j in range(4):
        s, e = max(lo, W_IN_SHARD * j), min(hi, W_IN_SHARD * (j + 1))
        if s < e:
            out.append(g[j][:, s - W_IN_SHARD * j:e - W_IN_SHARD * j])
    return out


def _reord_ranges(lo, hi):
    out = []
    for a, b, off in ((0, 3840, 0), (3840, 3856, O_DT - 3840), (3856, 9232, -16)):
        s, e = max(lo, a), min(hi, b)
        if s < e:
            out.append((s + off, e + off))
    return out


def _halves_of(n, g):
    if n == 'w_in':
        shards = [jnp.concatenate([g[:, a:b] for a, b in _reord_ranges(W_IN_SHARD * j, W_IN_SHARD * (j + 1))], axis=1)
                  for j in range(4)]
    elif n in COL_SHARDED:
        c = g.shape[1] // 4
        shards = [g[:, j * c:(j + 1) * c] for j in range(4)]
    else:
        r = g.shape[0] // 4
        shards = [g[j * r:(j + 1) * r] for j in range(4)]
    r2 = shards[0].shape[0] // 2
    return jnp.stack([jnp.concatenate([s[h * r2:(h + 1) * r2] for s in shards], axis=0) for h in range(2)])


def _reduce_a(grads):
    names = list(grads)
    halves = [_halves_of(n, grads[n]) for n in names]
    state, token = _exchange_start(halves, "cores", "scatter", name="rsc_start")
    return (names, state), token


def _reduce_b(handle, after):
    names, state = handle
    core = lax.axis_index("c").reshape(1)
    halves, got = _exchange_wait(state, after, "cores", "scatter", name="rsc_wait")
    chip = [_sum_own_recv(h, t, core, BF16, name="rs_sum2") for h, t in zip(halves, got)]
    chip = [t.reshape(4, t.shape[0] // 4, t.shape[1]) for t in chip]
    state, token = _exchange_start(chip, "chips", "scatter", name="rs_start")
    return (names, state), token


def _reduce_c(handle, after):
    names, state = handle
    chip_id = (2 * lax.axis_index("x") + lax.axis_index("y")).reshape(1)
    chip, got = _exchange_wait(state, after, "chips", "scatter", name="rs_wait")
    red = [_sum_own_recv(h, t, chip_id, F32, name="rs_sum4") for h, t in zip(chip, got)]
    other = _exchange(red, "cores", "gather", name="rs_share")
    out = {}
    for n, mine, theirs in zip(names, red, other):
        out[n] = jnp.where(lax.axis_index("c") == 0, jnp.concatenate([mine, theirs]), jnp.concatenate([theirs, mine]))
    return out


class _Comm:
    def __init__(self, inp):
        self.inp = inp

    def gather_start(self, i, names):
        return _gather_start(self.inp, i, names)

    gather_mid = staticmethod(_gather_mid)
    gather_finish = staticmethod(_gather_finish)

    def reduce_a(self, i, grads):
        return _reduce_a({n: grads[n] for n in BIG})

    reduce_b = staticmethod(_reduce_b)
    reduce_c = staticmethod(_reduce_c)


def _allreduce_small(vec):
    for group in ("cores", "x", "y"):
        recv = _exchange([vec], group, "gather", name=f"ar_{group}")[0]
        vec = _rowwise(lambda tv, fv: ([tv[0] + tv[1]], []), [vec, recv], [], [(128, F32)], [], name=f"ar_add_{group}")[0]
    return vec


def _pack(arrs):
    flat = jnp.concatenate([a.reshape(-1) for a in arrs])
    n = flat.shape[0]
    pad = (-n) % (256 * 128)
    flat = jnp.concatenate([flat, jnp.zeros((pad,), F32)])
    return flat.reshape(-1, 128)


def _unpack(p, shapes):
    flat = p.reshape(-1)
    out, off = [], 0
    for s in shapes:
        sz = int(np.prod(s))
        out.append(flat[off:off + sz].reshape(s))
        off += sz
    return out


def _forward_backward(inp, comm, bias_all):
    x = xm = inp['x'].reshape(-1, D)
    tgt = inp['loss_target'].reshape(-1, D)
    saved, Ws = [], []
    h_first, _ = comm.gather_start(0, GATHER_FIRST)
    h_rest, dep = comm.gather_start(0, GATHER_REST)
    h_first, tok = comm.gather_mid(h_first, x)
    G = comm.gather_finish(h_first, tok)
    for i in range(NL):
        W = _prep_layer_weights(i, inp, G)
        start_next = lambda: (comm.gather_start(i + 1, BIG + ['gate_b', 'conv_w']) if i + 1 < NL else (None, None))
        if i > 0:
            h_next, dep = start_next()
        x1, x1m, s1 = _ffn_fwd(x, xm, W['ffn1_w13'], W['ffn1_w2'], W['ln1_g'], W['ln1_b'], "f1", dep)
        if i == 0:
            h_rest, tok = comm.gather_mid(h_rest, x1m)
            W.update(_prep_layer_weights(i, inp, comm.gather_finish(h_rest, tok)))
            h_next, dep = start_next()
        x2, x2m, s2 = _mixer_fwd(x1, x1m, W, bias_all, "mx", dep if i == 0 else None)
        dep = None
        if h_next is not None:
            h_next, dep = comm.gather_mid(h_next, x2m)
        x, xm, s3 = _ffn_fwd(x2, x2m, W['ffn2_w13'], W['ffn2_w2'], W['ln3_g'], W['ln3_b'], "f2", dep)
        if h_next is not None:
            G = comm.gather_finish(h_next, xm)
        saved.append((s1, s2, s3))
        Ws.append(W)
    dy, lpart = _loss_fwd_bwd(x, tgt, name="loss")
    fins, reduced, dbiases = [None] * NL, [None] * NL, [None] * NL
    pend_a, pend_b, dep = None, None, None
    for i in reversed(range(NL)):
        W = Ws[i]
        s1, s2, s3 = saved[i]
        g = {}
        dx2, f = _ffn_bwd(dy, s3, W['ffn2_w13'], W['ffn2_w2'], W['ln3_g'], W['ln3_b'], "f2", dep)
        g['ffn2_w13'], g['ffn2_w2'], g['ln3_g'], g['ln3_b'] = f['w13'], f['w2'], f['g'], f['b']
        dep = None
        if pend_a is not None:
            handle, dep = comm.reduce_b(pend_a[1], dx2)
            pend_b = (pend_a[0], handle)
        dx1, gm, dbiases[i] = _mixer_bwd(dx2, s2, W, bias_all, "mx", dep)
        g.update(gm)
        dy, f = _ffn_bwd(dx1, s1, W['ffn1_w13'], W['ffn1_w2'], W['ln1_g'], W['ln1_b'], "f1")
        g['ffn1_w13'], g['ffn1_w2'], g['ln1_g'], g['ln1_b'] = f['w13'], f['w2'], f['g'], f['b']
        fins[i] = _finish_layer_grads(i, g, inp)
        if pend_b is not None:
            reduced[pend_b[0]] = comm.reduce_c(pend_b[1], dy)
            pend_b = None
        handle, dep = comm.reduce_a(i, fins[i])
        pend_a = (i, handle)
    return lpart, dy, fins, reduced, pend_a, dbiases


def _finish_layer_grads(i, g, inp):
    out = {n: g[n] for n in BIG if n != 'w_in'}
    out['w_in'] = g['w_in_r']
    out['pool_w'] = jnp.stack([g['pool_wbd'][k * POOL_GDIM:(k + 1) * POOL_GDIM, k * POOL_GDIM:(k + 1) * POOL_GDIM] for k in range(4)])
    out['pool_b'] = g['pool_b'].reshape(4, POOL_GDIM)
    out['pool_scale'] = g['pool_scale'].reshape(POOLW)
    out['conv_w'] = g['conv_w']
    out['conv_b'] = g['conv_b'].reshape(2048)
    out['dt_bias'] = g['dt_bias'].reshape(16)
    out['a_log'] = (g['a_neg'].reshape(16)) * (-jnp.exp(inp['a_log'][i]))
    out['d_skip'] = g['d_skip'].reshape(16)
    out['ssd_norm'] = g['ssd_norm'].reshape(D)
    out['gate_b'] = g['gate_b'].reshape(3, D)
    for n in ('ln1_g', 'ln1_b', 'ln2_g', 'ln2_b', 'ln3_g', 'ln3_b'):
        out[n] = g[n].reshape(D)
    return out


def kernel(x, ffn1_w13, ffn1_w2, ln1_g, ln1_b, w_in, gate_b, pool_w, pool_b, pool_scale, conv_w, conv_b,
           dt_bias, a_log, d_skip, ssd_norm, rel_bias, p_pool, p_ssd, p_attn, w_out, ln2_g, ln2_b, ffn2_w13,
           ffn2_w2, ln3_g, ln3_b, loss_target, m_ffn1_w13, m_ffn1_w2, m_ln1_g, m_ln1_b, m_w_in, m_gate_b,
           m_pool_w, m_pool_b, m_pool_scale, m_conv_w, m_conv_b, m_dt_bias, m_a_log, m_d_skip, m_ssd_norm,
           m_rel_bias, m_p_pool, m_p_ssd, m_p_attn, m_w_out, m_ln2_g, m_ln2_b, m_ffn2_w13, m_ffn2_w2, m_ln3_g,
           m_ln3_b, v_ffn1_w13, v_ffn1_w2, v_ln1_g, v_ln1_b, v_w_in, v_gate_b, v_pool_w, v_pool_b,
           v_pool_scale, v_conv_w, v_conv_b, v_dt_bias, v_a_log, v_d_skip, v_ssd_norm, v_rel_bias, v_p_pool,
           v_p_ssd, v_p_attn, v_w_out, v_ln2_g, v_ln2_b, v_ffn2_w13, v_ffn2_w2, v_ln3_g, v_ln3_b):
    inp = dict(locals())
    maps = jnp.asarray(_bucket_maps())
    bias_all = _bias_build(rel_bias, maps, name="bias_build")
    comm = _Comm(inp)
    lpart, gx, fins, red, pending, dbiases = _forward_backward(inp, comm, bias_all)
    loss = lax.psum(lpart[0, 0], ("x", "y", "c"))

    small_l = [n for n in SMALL if n != 'rel_bias']
    drel = _bias_reduce(jnp.stack(dbiases), maps, name="bias_reduce")[:, 0, :32].T
    handle_b, _ = comm.reduce_b(pending[1], drel)
    small_arrs = [jnp.stack([fins[i][n] for i in range(NL)]) for n in small_l] + [drel]
    packed = _allreduce_small(_pack(small_arrs))
    gsmall = dict(zip(small_l + ['rel_bias'], _unpack(packed, [a.shape for a in small_arrs])))
    shard = 2 * lax.axis_index("x") + lax.axis_index("y")
    gsmall['gate_b'] = lax.dynamic_slice_in_dim(gsmall['gate_b'], shard * 256, 256, axis=2)
    gsmall['conv_w'] = lax.dynamic_slice_in_dim(gsmall['conv_w'], shard * 512, 512, axis=2)

    red[pending[0]] = comm.reduce_c(handle_b, packed)
    gout = {n: jnp.stack([red[i][n] for i in range(NL)]) for n in BIG}
    gout.update(gsmall)

    delta, new_m, new_v = {}, {}, {}
    for n in BIG:
        shp = inp[n].shape
        two_d = lambda a: a.reshape(shp[0] * shp[1], shp[2])
        d, m, v = _adamw(two_d(inp[n]), two_d(gout[n]), two_d(inp['m_' + n]), two_d(inp['v_' + n]), name="adamw_big")
        delta[n], new_m[n], new_v[n] = d.reshape(shp), m.reshape(shp), v.reshape(shp)
    shapes = [inp[n].shape for n in SMALL]
    d, m, v = _adamw(_pack([inp[n] for n in SMALL]), _pack([gout[n] for n in SMALL]),
                     _pack([inp['m_' + n] for n in SMALL]), _pack([inp['v_' + n] for n in SMALL]), name="adamw_small")
    for n, dd, mm, vv in zip(SMALL, _unpack(d, shapes), _unpack(m, shapes), _unpack(v, shapes)):
        delta[n], new_m[n], new_v[n] = dd, mm, vv

    return (loss, gx.reshape(x.shape), *[gout[n] for n in WEIGHTS], *[delta[n] for n in WEIGHTS],
            *[new_m[n] for n in WEIGHTS], *[new_v[n] for n in WEIGHTS])
```

```python
import functools

import numpy as np
import jax
import jax.numpy as jnp
from jax import lax
from jax.experimental import pallas as pl
from jax.experimental.pallas import tpu as pltpu

F32 = jnp.float32
BF16 = jnp.bfloat16
_MXU = jnp.bfloat16
_ACT = jnp.bfloat16
_VMEM_LIMIT = 56 * 1024 * 1024

S = 2048
D = 1024
NL = 4
DFF = 2816
LN_EPS = 1e-5
SSD_EPS = 1e-5
ALPHA = (2.0 * NL) ** 0.25
POOLW = 768
POOL_WINDOWS = (2, 4, 8, 16)
POOL_GDIM = 192
CH = 128
ATTN_DILS = (1, 4, 16)
HC = 9728
O_U, O_Z, O_XBC, O_Q, O_K, O_V, O_G, O_DT = 0, 768, 1792, 3840, 4608, 5376, 6144, 9216

ADAM_LR, ADAM_B1, ADAM_B2, ADAM_EPS, ADAM_WD, ADAM_STEP = 0.001, 0.9, 0.999, 1e-08, 0.01, 10

WEIGHTS = ['ffn1_w13', 'ffn1_w2', 'ln1_g', 'ln1_b', 'w_in', 'gate_b', 'pool_w', 'pool_b', 'pool_scale', 'conv_w',
           'conv_b', 'dt_bias', 'a_log', 'd_skip', 'ssd_norm', 'rel_bias', 'p_pool', 'p_ssd', 'p_attn', 'w_out',
           'ln2_g', 'ln2_b', 'ffn2_w13', 'ffn2_w2', 'ln3_g', 'ln3_b']
BIG = ['ffn1_w13', 'ffn1_w2', 'w_in', 'p_pool', 'p_ssd', 'p_attn', 'w_out', 'ffn2_w13', 'ffn2_w2']
COL_SHARDED = {'ffn1_w13', 'ffn2_w13', 'w_in', 'p_pool', 'p_attn'}
SMALL = [n for n in WEIGHTS if n not in BIG]


def _pcall(body, **kw):
    return pl.pallas_call(body, **kw)


def _cp(sem=None):
    return pltpu.CompilerParams(dimension_semantics=sem, vmem_limit_bytes=_VMEM_LIMIT)


def _pick(n, cands):
    for c in cands:
        if n % c == 0:
            return c
    raise ValueError(f"no tile for {n}")


def _mm(a, b, *, ta=False, tb=False, add=None, out_dtype=F32, dep=None, name):
    if ta:
        K, M = a.shape
    else:
        M, K = a.shape
    if tb:
        N, K2 = b.shape
    else:
        K2, N = b.shape
    assert K == K2, (a.shape, b.shape, ta, tb)
    sa, sb, so = a.dtype.itemsize, b.dtype.itemsize, jnp.dtype(out_dtype).itemsize
    tm, tn, tk = _mm_tiles(M, N, K, sa, sb, so + (4 if add is not None else 0))
    nk = K // tk
    a_bytes, b_bytes = M * K * sa, K * N * sb
    j_outer = nk == 1 and (b_bytes + a_bytes * (N // tn) < a_bytes + b_bytes * (M // tm))
    ij = (lambda p, q: (q, p)) if j_outer else (lambda p, q: (p, q))

    def im(f):
        return lambda p, q, k: f(*ij(p, q), k)

    a_spec = pl.BlockSpec((tk, tm), im(lambda i, j, k: (k, i))) if ta else pl.BlockSpec((tm, tk), im(lambda i, j, k: (i, k)))
    b_spec = pl.BlockSpec((tn, tk), im(lambda i, j, k: (j, k))) if tb else pl.BlockSpec((tk, tn), im(lambda i, j, k: (k, j)))
    o_spec = pl.BlockSpec((tm, tn), im(lambda i, j, k: (i, j)))
    dims = (((0 if ta else 1,), (1 if tb else 0,)), ((), ()))
    has_add = add is not None

    n_in = 2 + int(has_add) + int(dep is not None)

    def body(*refs):
        a_ref, b_ref = refs[0], refs[1]
        add_ref = refs[2] if has_add else None
        o_ref = refs[n_in]
        part = lax.dot_general(a_ref[...].astype(_MXU), b_ref[...].astype(_MXU), dims, preferred_element_type=F32)

        def finish(r):
            if has_add:
                r = r + add_ref[...]
            o_ref[...] = r.astype(out_dtype)

        if nk == 1:
            finish(part)
        else:
            acc = refs[-1]
            k = pl.program_id(2)

            @pl.when(k == 0)
            def _():
                acc[...] = part

            @pl.when(k > 0)
            def _():
                acc[...] += part

            @pl.when(k == nk - 1)
            def _():
                finish(acc[...])

    in_specs = [a_spec, b_spec]
    args = [a, b]
    if has_add:
        in_specs.append(o_spec)
        args.append(add)
    if dep is not None:
        in_specs.append(pl.BlockSpec(memory_space=pl.ANY))
        args.append(dep)
    gm, gn = M // tm, N // tn
    return _pcall(
        body, name=name, grid=((gn, gm, nk) if j_outer else (gm, gn, nk)), in_specs=in_specs, out_specs=o_spec,
        out_shape=jax.ShapeDtypeStruct((M, N), out_dtype),
        scratch_shapes=([pltpu.VMEM((tm, tn), F32)] if nk > 1 else []),
        compiler_params=_cp(("parallel", "parallel", "arbitrary")),
    )(*args)


_MM_VMEM_BUDGET = 40 * 1024 * 1024


def _divisors128(n, cap):
    return [d for d in range(128, min(n, cap) + 1, 128) if n % d == 0][::-1]


_MM_CYC_PER_MMAC = 4.35
_MM_CYC_PER_ACC_VREG = 2.03
_MM_HBM_BYTES_PER_CYC = 1455.0
_MM_CYC_PER_STEP = 770.0


def _mm_tiles(M, N, K, sa, sb, so):
    best = None
    for tm in _divisors128(M, 1408):
        for tn in _divisors128(N, 2560):
            for tk in ([K] if K <= 4096 else []) + _divisors128(K, 2816):
                nk = K // tk
                need = 2 * (tm * tk * sa + tk * tn * sb + tm * tn * so) + (tm * tn * 4 if nk > 1 else 0)
                need += tm * tk * 2 + tk * tn * 2 + tm * tn * 4
                if need > _MM_VMEM_BUDGET:
                    continue
                gm, gn = M // tm, N // tn
                a_bytes, b_bytes = M * K * sa, K * N * sb
                hbm = min(b_bytes + a_bytes * gn, a_bytes + b_bytes * gm) if nk == 1 else a_bytes * gn + b_bytes * gm
                hbm += M * N * so
                work = _MM_CYC_PER_MMAC * M * N * K / 1e6 + _MM_CYC_PER_ACC_VREG * (M * N / 1024) * (nk if nk > 1 else 0.5)
                cost = max(work, hbm / _MM_HBM_BYTES_PER_CYC) + gm * gn * nk * _MM_CYC_PER_STEP
                if best is None or cost < best[0]:
                    best = (cost, (tm, tn, tk))
    assert best is not None, (M, N, K)
    return best[1]


def _store(ref, val):
    if isinstance(val, (list, tuple)):
        off = 0
        for p in val:
            w = p.shape[1]
            ref[:, off:off + w] = p.astype(ref.dtype)
            off += w
    else:
        ref[...] = val.astype(ref.dtype)


def _acc_store(ref, val, first):
    pieces = val if isinstance(val, (list, tuple)) else [val]
    off = 0
    for p in pieces:
        w = p.shape[1]

        @pl.when(first)
        def _(p=p, off=off, w=w):
            ref[:, off:off + w] = p

        @pl.when(jnp.logical_not(first))
        def _(p=p, off=off, w=w):
            ref[:, off:off + w] += p

        off += w


def _rowwise(fn, tiled, full, out_tiled, out_acc, *, name, tm=256):
    arrs, specs = [], []
    for t in tiled:
        arr, w, cb = t if isinstance(t, tuple) else (t, t.shape[1], 0)
        arrs.append(arr)
        specs.append(pl.BlockSpec((tm, w), functools.partial(lambda i, cb: (i, cb), cb=cb)))
    R = arrs[0].shape[0]
    assert R % tm == 0
    for f in full:
        arrs.append(f)
        specs.append(pl.BlockSpec(f.shape, functools.partial(lambda i, nd: (0,) * nd, nd=f.ndim)))
    nt, nf, no = len(tiled), len(full), len(out_tiled)

    def body(*refs):
        tv = [r[...] for r in refs[:nt]]
        fv = [r[...] for r in refs[nt:nt + nf]]
        ot, oa = fn(tv, fv)
        for r, v in zip(refs[nt + nf:nt + nf + no], ot):
            _store(r, v)
        first = pl.program_id(0) == 0
        for r, v in zip(refs[nt + nf + no:], oa):
            _acc_store(r, v, first)

    out_shape = [jax.ShapeDtypeStruct((R, c), dt) for c, dt in out_tiled]
    out_specs = [pl.BlockSpec((tm, c), lambda i: (i, 0)) for c, _ in out_tiled]
    for shp in out_acc:
        out_shape.append(jax.ShapeDtypeStruct(shp, F32))
        out_specs.append(pl.BlockSpec(shp, lambda i: (0, 0)))
    return _pcall(body, name=name, grid=(R // tm,), in_specs=specs, out_specs=out_specs, out_shape=out_shape,
                  compiler_params=_cp(("arbitrary",)))(*arrs)


def _group(group):
    x, y, c = lax.axis_index("x"), lax.axis_index("y"), lax.axis_index("c")
    if group == "chips":
        return 2 * x + y, [((x, 1 - y, c), 2 * x + 1 - y), ((1 - x, y, c), 2 * (1 - x) + y),
                           ((1 - x, 1 - y, c), 2 * (1 - x) + 1 - y)]
    if group == "cores":
        return c, [((x, y, 1 - c), 1 - c)]
    if group == "x":
        return x, [((1 - x, y, c), 1 - x)]
    return y, [((x, 1 - y, c), 1 - y)]


def _exchange(arrs, group, mode, name):
    chips = group == "chips"
    k = len(arrs)
    npeer = 3 if chips else 1

    def body(*refs):
        ins, outs = refs[:k], refs[k:2 * k]
        send_sems, recv_sems = refs[2 * k:]
        me, peers = _group(group)
        remote = []
        for i in range(k):
            for p, (dev, slot) in enumerate(peers):
                src = ins[i].at[slot] if mode == "scatter" else ins[i]
                if not chips:
                    dst = outs[i]
                else:
                    dst = outs[i].at[p] if mode == "scatter" else outs[i].at[me]
                cp = pltpu.make_async_remote_copy(src_ref=src, dst_ref=dst, send_sem=send_sems.at[i, p],
                                                  recv_sem=recv_sems.at[i, p], device_id=dev,
                                                  device_id_type=pl.DeviceIdType.MESH)
                cp.start()
                remote.append(cp)
        for cp in remote:
            cp.wait_recv()
        for cp in remote:
            cp.wait_send()

    def oshape(a):
        piece = a.shape[1:] if mode == "scatter" else a.shape
        if chips:
            piece = ((3,) if mode == "scatter" else (4,)) + piece
        return jax.ShapeDtypeStruct(piece, a.dtype)

    any_spec = pl.BlockSpec(memory_space=pl.ANY)
    return _pcall(body, name=name, in_specs=[any_spec] * k, out_specs=[any_spec] * k, out_shape=[oshape(a) for a in arrs],
                  scratch_shapes=[pltpu.SemaphoreType.DMA((k, npeer)), pltpu.SemaphoreType.DMA((k, npeer))])(*arrs)


def _split_copies(ins, lands, send_sems, recv_sems, group, mode):
    chips = group == "chips"
    me, peers = _group(group)
    npeer = len(peers)
    out = []
    for i in range(len(ins)):
        for p, (dev, slot) in enumerate(peers):
            src = ins[i].at[slot] if mode == "scatter" else ins[i]
            if not chips:
                dst = lands[i]
            else:
                dst = lands[i].at[p] if mode == "scatter" else lands[i].at[me]
            out.append(pltpu.make_async_remote_copy(src_ref=src, dst_ref=dst, send_sem=send_sems.at[npeer * i + p],
                                                    recv_sem=recv_sems.at[npeer * i + p], device_id=dev,
                                                    device_id_type=pl.DeviceIdType.MESH))
    return out


def _exchange_start(arrs, group, mode, name):
    k = len(arrs)
    chips = group == "chips"
    nsem = (3 if chips else 1) * k
    hbm = pl.BlockSpec(memory_space=pltpu.HBM)
    sem = pl.BlockSpec(memory_space=pltpu.SEMAPHORE)

    def land_shape(a):
        piece = a.shape[1:] if mode == "scatter" else a.shape
        if chips:
            piece = ((3,) if mode == "scatter" else (4,)) + piece
        return piece

    def body(*refs):
        ins, lands = refs[:k], refs[k:2 * k]
        send_sems, recv_sems = refs[2 * k], refs[2 * k + 1]
        token = refs[-1]
        for cp in _split_copies(ins, lands, send_sems, recv_sems, group, mode):
            cp.start()
        token[...] = jnp.zeros_like(token)

    srcs = [pltpu.with_memory_space_constraint(a, pltpu.HBM) for a in arrs]
    lands = [pltpu.with_memory_space_constraint(lax.empty(land_shape(a), a.dtype), pltpu.HBM) for a in arrs]
    out_shape = ([pltpu.SemaphoreType.DMA((nsem,)), pltpu.SemaphoreType.DMA((nsem,))]
                 + [pltpu.HBM(a.shape, a.dtype) for a in arrs] + [pltpu.HBM(land_shape(a), a.dtype) for a in arrs]
                 + [jax.ShapeDtypeStruct((8, 128), F32)])
    outs = _pcall(body, name=name, in_specs=[hbm] * (2 * k),
                  out_specs=[sem, sem] + [hbm] * (2 * k) + [pl.BlockSpec(memory_space=pltpu.VMEM)], out_shape=out_shape,
                  input_output_aliases={i: 2 + i for i in range(2 * k)},
                  compiler_params=pltpu.CompilerParams(has_side_effects=pltpu.SideEffectType.DATAFLOW_SIDE_EFFECTING))(
                      *srcs, *lands)
    return (outs[0], outs[1], list(outs[2:2 + k]), list(outs[2 + k:2 + 2 * k])), outs[-1]


def _exchange_wait(state, after, group, mode, name):
    send_sems, recv_sems, srcs, lands = state
    k = len(srcs)
    hbm = pl.BlockSpec(memory_space=pltpu.HBM)
    sem = pl.BlockSpec(memory_space=pltpu.SEMAPHORE)

    def body(*refs):
        ins, lnd = refs[:k], refs[k:2 * k]
        send_sems, recv_sems = refs[2 * k], refs[2 * k + 1]
        for cp in _split_copies(ins, lnd, send_sems, recv_sems, group, mode):
            cp.wait_send()
            cp.wait_recv()

    outs = _pcall(body, name=name, in_specs=[hbm] * (2 * k) + [sem, sem, pl.BlockSpec(memory_space=pl.ANY)],
                  out_specs=[hbm] * (2 * k),
                  out_shape=[pltpu.HBM(a.shape, a.dtype) for a in srcs] + [pltpu.HBM(a.shape, a.dtype) for a in lands],
                  input_output_aliases={i: i for i in range(2 * k)},
                  compiler_params=pltpu.CompilerParams(has_side_effects=pltpu.SideEffectType.DATAFLOW_SIDE_EFFECTING))(
                      *srcs, *lands, send_sems, recv_sems, after)
    return list(outs[:k]), list(outs[k:])


def _sum_own_recv(own, recv, me, out_dtype, name):
    n, R, C = own.shape
    nr = 1 if recv.ndim == 2 else recv.shape[0]
    tr = _pick(R, (256, 128, 64, 32, 16, 8))

    def body(me_ref, own_ref, *refs):
        o_ref = refs[-1]
        acc = own_ref[...].astype(F32)
        for r in refs[:-1]:
            acc = acc + r[...].astype(F32)
        o_ref[...] = acc.astype(out_dtype)

    specs = [pl.BlockSpec((None, tr, C), lambda i, me_ref: (me_ref[0], i, 0))]
    args = [own]
    if recv.ndim == 2:
        specs.append(pl.BlockSpec((tr, C), lambda i, me_ref: (i, 0)))
        args.append(recv)
    else:
        for p in range(nr):
            specs.append(pl.BlockSpec((None, tr, C), functools.partial(lambda i, me_ref, p: (p, i, 0), p=p)))
            args.append(recv)
    gs = pltpu.PrefetchScalarGridSpec(num_scalar_prefetch=1, grid=(R // tr,), in_specs=specs,
                                      out_specs=pl.BlockSpec((tr, C), lambda i, me_ref: (i, 0)))
    return _pcall(body, name=name, grid_spec=gs, out_shape=jax.ShapeDtypeStruct((R, C), out_dtype),
                  compiler_params=_cp(("parallel",)))(me, *args)


def _silu(x):
    return x * jax.nn.sigmoid(x)


def _ln(r, g, b):
    mu = jnp.mean(r, -1, keepdims=True)
    xc = r - mu
    var = jnp.mean(xc * xc, -1, keepdims=True)
    return xc * lax.rsqrt(var + LN_EPS) * g + b


def _softplus(x):
    return jnp.maximum(x, 0.0) + jnp.log1p(jnp.exp(-jnp.abs(x)))


def _res_ln_fwd(x, y, g, b, res, name):
    def fn(tv, fv):
        r = ALPHA * tv[0] + res * tv[1]
        out = _ln(r, fv[0], fv[1])
        return [r, out, out], []
    return _rowwise(fn, [x, y], [g, b], [(D, F32), (D, F32), (D, _ACT)], [], name=name)


def _ln_bwd(r, g, b, dout, res, name):
    def fn(tv, fv):
        _, vjp = jax.vjp(_ln, tv[0], fv[0], fv[1])
        dr, dg, db = vjp(tv[1])
        return [ALPHA * dr, res * dr], [dg, db]
    return _rowwise(fn, [r, dout], [g, b], [(D, F32), (D, _ACT)], [(1, D), (1, D)], name=name)


def _swiglu_act(h, name):
    def fn(tv, fv):
        return [_silu(tv[0]) * tv[1]], []
    return _rowwise(fn, [(h, DFF, 0), (h, DFF, 1)], [], [(DFF, _ACT)], [], name=name)[0]


def _swiglu_act_bwd(h, ds, name):
    def fn(tv, fv):
        s, vjp = jax.vjp(lambda a, g: _silu(a) * g, tv[0], tv[1])
        da, dg = vjp(tv[2])
        return [[da, dg], s], []
    return _rowwise(fn, [(h, DFF, 0), (h, DFF, 1), ds], [], [(2 * DFF, _ACT), (DFF, _ACT)], [], name=name)


def _loss_fwd_bwd(y, tgt, name):
    def fn(tv, fv):
        e = tv[0] - tv[1]
        row = jnp.sum(e * e, axis=1, keepdims=True)
        tot = jnp.sum(row, axis=0, keepdims=True) * (0.5 / D)
        return [e * (1.0 / D)], [jnp.broadcast_to(tot, (1, 128))]
    return _rowwise(fn, [y, tgt], [], [(D, F32)], [(1, 128)], name=name)


def _shift_down(x, k, row):
    return jnp.where(row >= k, pltpu.roll(x, k, axis=0), 0.0)


def _shift_up(x, k, row):
    n = x.shape[0]
    return jnp.where(row < n - k, pltpu.roll(x, n - k, axis=0), 0.0)


def _pool_window_masks(j):
    lane = lax.broadcasted_iota(jnp.int32, (1, 128), 1) + j * 128
    grp = lane // POOL_GDIM
    return [grp == g for g in range(4)]


def _pool_mean(u, bwd, name, col0=0):
    T = u.shape[0]
    B = T // S

    def body(u_ref, o_ref):
        j = pl.program_id(1)
        x = u_ref[...]
        row = lax.broadcasted_iota(jnp.int32, (S, 1), 0)
        masks = _pool_window_masks(j)
        inv = [1.0 / jnp.minimum(row + 1, w).astype(F32) for w in POOL_WINDOWS]
        if not bwd:
            s2 = x + _shift_down(x, 1, row)
            s4 = s2 + _shift_down(s2, 2, row)
            s8 = s4 + _shift_down(s4, 4, row)
            s16 = s8 + _shift_down(s8, 8, row)
            mean = jnp.where(masks[0], s2 * inv[0], jnp.where(masks[1], s4 * inv[1],
                             jnp.where(masks[2], s8 * inv[2], s16 * inv[3])))
            o_ref[...] = (mean - x).astype(o_ref.dtype)
        else:
            g = [jnp.where(masks[i], x * inv[i], 0.0) for i in range(4)]
            t = g[3]
            t = t + _shift_up(t, 8, row) + g[2]
            t = t + _shift_up(t, 4, row) + g[1]
            t = t + _shift_up(t, 2, row) + g[0]
            t = t + _shift_up(t, 1, row)
            o_ref[...] = (t - x).astype(o_ref.dtype)

    spec = pl.BlockSpec((S, 128), lambda b, j: (b, j))
    return _pcall(body, name=name, grid=(B, POOLW // 128),
                  in_specs=[pl.BlockSpec((S, 128), lambda b, j: (b, j + col0))], out_specs=spec,
                  out_shape=jax.ShapeDtypeStruct((T, POOLW), _ACT), compiler_params=_cp(("parallel", "parallel")))(u)


def _conv_silu(xbc, w, b, name, col0=0):
    T, C = xbc.shape[0], w.shape[1]
    B = T // S

    def body(x_ref, w_ref, b_ref, o_ref):
        x = x_ref[...]
        row = lax.broadcasted_iota(jnp.int32, (S, 1), 0)
        c = b_ref[...] + w_ref[3:4, :] * x
        for s in range(1, 4):
            c = c + w_ref[3 - s:4 - s, :] * _shift_down(x, s, row)
        o_ref[...] = _silu(c)

    return _pcall(body, name=name, grid=(B, C // 128),
                  in_specs=[pl.BlockSpec((S, 128), lambda b, j: (b, j + col0)), pl.BlockSpec((4, 128), lambda b, j: (0, j)),
                            pl.BlockSpec((1, 128), lambda b, j: (0, j))],
                  out_specs=pl.BlockSpec((S, 128), lambda b, j: (b, j)),
                  out_shape=jax.ShapeDtypeStruct((T, C), F32), compiler_params=_cp(("parallel", "parallel")))(xbc, w, b)


def _conv_silu_bwd(xbc, w, b, dact, name, col0=0):
    T, C = xbc.shape[0], w.shape[1]
    B = T // S

    def body(x_ref, w_ref, b_ref, d_ref, dx_ref, dw_ref, db_ref):
        bi = pl.program_id(1)
        x = x_ref[...]
        row = lax.broadcasted_iota(jnp.int32, (S, 1), 0)
        xs = [x] + [_shift_down(x, s, row) for s in range(1, 4)]
        c = b_ref[...]
        for s in range(4):
            c = c + w_ref[3 - s:4 - s, :] * xs[s]
        _, vjp = jax.vjp(_silu, c)
        dc = vjp(d_ref[...])[0]
        dx = w_ref[3:4, :] * dc
        for s in range(1, 4):
            dx = dx + w_ref[3 - s:4 - s, :] * _shift_up(dc, s, row)
        dx_ref[...] = dx.astype(dx_ref.dtype)
        first = bi == 0
        for s in range(4):
            _acc_rows(dw_ref, 3 - s, jnp.sum(dc * xs[s], axis=0, keepdims=True), first)
        _acc_rows(db_ref, 0, jnp.sum(dc, axis=0, keepdims=True), first)

    blk = pl.BlockSpec((S, 128), lambda j, b: (b, j))
    return _pcall(body, name=name, grid=(C // 128, B),
                  in_specs=[pl.BlockSpec((S, 128), lambda j, b: (b, j + col0)), pl.BlockSpec((4, 128), lambda j, b: (0, j)),
                            pl.BlockSpec((1, 128), lambda j, b: (0, j)), blk],
                  out_specs=[blk, pl.BlockSpec((4, 128), lambda j, b: (0, j)), pl.BlockSpec((1, 128), lambda j, b: (0, j))],
                  out_shape=[jax.ShapeDtypeStruct((T, C), _ACT), jax.ShapeDtypeStruct((4, C), F32),
                             jax.ShapeDtypeStruct((1, C), F32)],
                  compiler_params=_cp(("parallel", "arbitrary")))(xbc, w, b, dact)


def _acc_rows(ref, r, val, first):
    @pl.when(first)
    def _():
        ref[r:r + 1, :] = val

    @pl.when(jnp.logical_not(first))
    def _():
        ref[r:r + 1, :] += val


def _tri_consts():
    i = lax.broadcasted_iota(jnp.int32, (CH, CH), 0)
    j = lax.broadcasted_iota(jnp.int32, (CH, CH), 1)
    return (i == j).astype(F32), (j <= i).astype(F32), (i <= j).astype(F32), i >= j


def _ssd_chunk(h, x, dt, Bm, Cm, a, dsk, consts):
    eye, tril, triu, lower = consts
    Bb = Bm.astype(_MXU)
    Cb = Cm.astype(_MXU)
    cb = lax.dot_general(Cb, Bb, (((1,), (1,)), ((), ())), preferred_element_type=F32)
    ys, hn = [], []
    for e in range(4):
        adt = dt[e] * a[e]
        adt_row = jnp.sum(adt * eye, axis=0, keepdims=True)
        cs_col = jnp.sum(adt_row * tril, axis=1, keepdims=True)
        cs_row = jnp.sum(adt * triu, axis=0, keepdims=True)
        cs_last = jnp.sum(adt, axis=0, keepdims=True)
        decay = jnp.exp(jnp.where(lower, cs_col - cs_row, -jnp.inf))
        xb = (x[e] * dt[e]).astype(_MXU)
        y_diag = jnp.dot((cb * decay).astype(_MXU), xb, preferred_element_type=F32)
        bdec = (Bm * jnp.exp(cs_last - cs_col)).astype(_MXU)
        st = lax.dot_general(bdec, xb, (((0,), (0,)), ((), ())), preferred_element_type=F32)
        hn.append(h[e] * jnp.exp(cs_last) + st)
        y_off = jnp.exp(cs_col) * jnp.dot(Cb, h[e].astype(_MXU), preferred_element_type=F32)
        ys.append(y_diag + y_off + dsk[e] * x[e])
    return ys, hn


def _ssd_specs(order):
    def im(f):
        return lambda p, q: f(*order(p, q))
    xs = pl.BlockSpec((S, 256), im(lambda b, g: (b, g)))
    dt = pl.BlockSpec((None, S, 4), im(lambda b, g: (g, b, 0)))
    bc = pl.BlockSpec((S, 128), im(lambda b, g: (b, g)))
    hd = pl.BlockSpec((None, 1, 4), im(lambda b, g: (g, 0, 0)))
    hs = pl.BlockSpec((None, None, S // CH, 4, 128, 64), im(lambda b, g: (b, g, 0, 0, 0, 0)))
    bw = pl.BlockSpec((S, 128), im(lambda b, g: (b, 8 + g)))
    cw = pl.BlockSpec((S, 128), im(lambda b, g: (b, 12 + g)))
    return xs, dt, bc, hd, hs, bw, cw


def _ssd_fwd(act, dtg, a, dsk, name):
    xs = bm = cm = act
    T = xs.shape[0]
    B = T // S
    nc = S // CH

    def body(x_ref, dt_ref, b_ref, c_ref, a_ref, k_ref, y_ref, hs_ref, h_ref):
        consts = _tri_consts()
        h_ref[...] = jnp.zeros_like(h_ref)
        al = [a_ref[:, e:e + 1] for e in range(4)]
        kl = [k_ref[:, e:e + 1] for e in range(4)]

        def step(c, carry):
            r0 = pl.multiple_of(c * CH, CH)
            rows = pl.ds(r0, CH)
            h = [h_ref[e] for e in range(4)]
            for e in range(4):
                hs_ref[c, e] = h[e]
            x = [x_ref[rows, 64 * e:64 * e + 64] for e in range(4)]
            dt = [dt_ref[rows, e:e + 1] for e in range(4)]
            ys, hn = _ssd_chunk(h, x, dt, b_ref[rows, :], c_ref[rows, :], al, kl, consts)
            for e in range(4):
                y_ref[rows, 64 * e:64 * e + 64] = ys[e]
                h_ref[e] = hn[e]
            return carry

        lax.fori_loop(0, nc, step, 0)

    sx, sdt, sbc, shd, shs, sbw, scw = _ssd_specs(lambda b, g: (b, g))
    return _pcall(body, name=name, grid=(B, 4), in_specs=[sx, sdt, sbw, scw, shd, shd], out_specs=[sx, shs],
                  out_shape=[jax.ShapeDtypeStruct((T, 1024), F32), jax.ShapeDtypeStruct((B, 4, nc, 4, 128, 64), F32)],
                  scratch_shapes=[pltpu.VMEM((4, 128, 64), F32)],
                  compiler_params=_cp(("parallel", "parallel")))(xs, dtg, bm, cm, a, dsk)


def _lane_place(vals, width):
    lane = lax.broadcasted_iota(jnp.int32, (1, width), 1)
    out = jnp.zeros((1, width), F32)
    for e, v in enumerate(vals):
        out = out + jnp.where(lane == e, v, 0.0)
    return out


def _ssd_bwd(act, dtg, a, dsk, hs, dy, name):
    xs = bm = cm = act
    T = xs.shape[0]
    B = T // S
    nc = S // CH

    def body(x_ref, dt_ref, b_ref, c_ref, a_ref, k_ref, hs_ref, dy_ref,
             dx_ref, ddt_ref, db_ref, dc_ref, dak_ref, dh_ref, sc_ref):
        bi = pl.program_id(1)
        consts = _tri_consts()
        dh_ref[...] = jnp.zeros_like(dh_ref)
        sc_ref[...] = jnp.zeros_like(sc_ref)
        al = [a_ref[:, e:e + 1] for e in range(4)]
        kl = [k_ref[:, e:e + 1] for e in range(4)]

        def step(i, carry):
            c = nc - 1 - i
            r0 = pl.multiple_of(c * CH, CH)
            rows = pl.ds(r0, CH)
            h = [hs_ref[c, e] for e in range(4)]
            x = [x_ref[rows, 64 * e:64 * e + 64] for e in range(4)]
            dt = [dt_ref[rows, e:e + 1] for e in range(4)]
            f = functools.partial(_ssd_chunk, consts=consts)
            _, vjp = jax.vjp(f, h, x, dt, b_ref[rows, :], c_ref[rows, :], al, kl)
            dys = [dy_ref[rows, 64 * e:64 * e + 64] for e in range(4)]
            dhn = [dh_ref[e] for e in range(4)]
            dh, dx, ddt, dB, dC, da, dk = vjp((dys, dhn))
            for e in range(4):
                dh_ref[e] = dh[e]
                dx_ref[rows, 64 * e:64 * e + 64] = dx[e]
                ddt_ref[rows, e:e + 1] = ddt[e]
            db_ref[rows, :] = dB
            dc_ref[rows, :] = dC
            sc_ref[0:1, :] += _lane_place(da, 128)
            sc_ref[1:2, :] += _lane_place(dk, 128)
            return carry

        lax.fori_loop(0, nc, step, 0)
        first = bi == 0

        @pl.when(first)
        def _():
            dak_ref[...] = sc_ref[...]

        @pl.when(jnp.logical_not(first))
        def _():
            dak_ref[...] += sc_ref[...]

    sx, sdt, sbc, shd, shs, sbw, scw = _ssd_specs(lambda g, b: (b, g))
    return _pcall(body, name=name, grid=(4, B), in_specs=[sx, sdt, sbw, scw, shd, shd, shs, sx],
                  out_specs=[sx, sdt, sbc, sbc, pl.BlockSpec((None, 8, 128), lambda g, b: (g, 0, 0))],
                  out_shape=[jax.ShapeDtypeStruct((T, 1024), F32), jax.ShapeDtypeStruct((4, T, 4), F32),
                             jax.ShapeDtypeStruct((T, 512), F32), jax.ShapeDtypeStruct((T, 512), F32),
                             jax.ShapeDtypeStruct((4, 8, 128), F32)],
                  scratch_shapes=[pltpu.VMEM((4, 128, 64), F32), pltpu.VMEM((8, 128), F32)],
                  compiler_params=_cp(("parallel", "arbitrary")))(xs, dtg, bm, cm, a, dsk, hs, dy)


def _gate_norm(y, z, nw):
    t = y * _silu(z)
    return t * lax.rsqrt(jnp.mean(t * t, axis=-1, keepdims=True) + SSD_EPS) * nw


def _ssd_gate_norm(y, z, nw, name, zcol=0):
    def fn(tv, fv):
        return [[_gate_norm(tv[g], tv[4 + g], fv[0][:, 256 * g:256 * g + 256]) for g in range(4)]], []
    tiled = [(y, 256, g) for g in range(4)] + [(z, 256, zcol + g) for g in range(4)]
    return _rowwise(fn, tiled, [nw], [(1024, _ACT)], [], name=name)[0]


def _ssd_gate_norm_bwd(y, z, nw, dout, name, zcol=0):
    def fn(tv, fv):
        dys, dzs, dns = [], [], []
        for g in range(4):
            _, vjp = jax.vjp(_gate_norm, tv[g], tv[4 + g], fv[0][:, 256 * g:256 * g + 256])
            a, b, c = vjp(tv[8 + g])
            dys.append(a)
            dzs.append(b)
            dns.append(c)
        return [dys, dzs], [dns]
    tiled = [(y, 256, g) for g in range(4)] + [(z, 256, zcol + g) for g in range(4)] + [(dout, 256, g) for g in range(4)]
    return _rowwise(fn, tiled, [nw], [(1024, F32), (1024, _ACT)], [(1, 1024)], name=name)


def _t5_bucket_np(dist):
    dist = np.maximum(dist, 0)
    max_exact = 16
    large = max_exact + (np.log(np.maximum(dist, 1) / max_exact) / np.log(2048 / max_exact) * (32 - max_exact)).astype(np.int32)
    large = np.minimum(large, 31)
    return np.where(dist < max_exact, dist, large).astype(np.int32)


def _bucket_maps():
    qi = np.arange(128)[:, None]
    kj = np.arange(256)[None, :]
    return np.stack([_t5_bucket_np((qi - kj + 128) * dil) for dil in ATTN_DILS]).astype(np.int32)


def _bias_build(rel_bias, maps, name):
    def body(tab_ref, map_ref, o_ref):
        hh = pl.program_id(0)
        m = map_ref[...]
        acc = jnp.zeros((128, 256), F32)
        for b in range(32):
            acc = jnp.where(m == b, tab_ref[b, hh], acc)
        o_ref[...] = acc

    return _pcall(body, name=name, grid=(12,),
                  in_specs=[pl.BlockSpec(memory_space=pltpu.SMEM), pl.BlockSpec((None, 128, 256), lambda h: (h // 4, 0, 0))],
                  out_specs=pl.BlockSpec((None, 128, 256), lambda h: (h, 0, 0)),
                  out_shape=jax.ShapeDtypeStruct((12, 128, 256), F32), compiler_params=_cp(("parallel",)))(rel_bias, maps)


def _bias_reduce(dbias, maps, name):
    nl = dbias.shape[0]

    def body(d_ref, map_ref, o_ref):
        m = map_ref[...]
        d = d_ref[0]
        for i in range(1, nl):
            d = d + d_ref[i]
        lane = lax.broadcasted_iota(jnp.int32, (1, 128), 1)
        out = jnp.zeros((1, 128), F32)
        for b in range(32):
            s = jnp.sum(jnp.sum(jnp.where(m == b, d, 0.0), axis=1, keepdims=True), axis=0, keepdims=True)
            out = out + jnp.where(lane == b, s, 0.0)
        o_ref[...] = out

    return _pcall(body, name=name, grid=(12,),
                  in_specs=[pl.BlockSpec((nl, None, 128, 256), lambda h: (0, h, 0, 0)),
                            pl.BlockSpec((None, 128, 256), lambda h: (h // 4, 0, 0))],
                  out_specs=pl.BlockSpec((None, 1, 128), lambda h: (h, 0, 0)),
                  out_shape=jax.ShapeDtypeStruct((12, 1, 128), F32), compiler_params=_cp(("parallel",)))(dbias, maps)


def _attn_block(q, kb, vb, bias, mask):
    s = lax.dot_general(q.astype(_MXU), kb.astype(_MXU), (((1,), (1,)), ((), ())), preferred_element_type=F32) * 0.125 + bias
    s = jnp.where(mask, s, -jnp.inf)
    m = lax.stop_gradient(jnp.max(s, axis=-1, keepdims=True))
    p = jnp.exp(s - m)
    den = jnp.sum(p, axis=-1, keepdims=True)
    out = jnp.dot((p / den).astype(_MXU), vb.astype(_MXU), preferred_element_type=F32)
    return out, m + jnp.log(den)


ATTN_QB = 512


def _attn_masks(dil):
    qi = lax.broadcasted_iota(jnp.int32, (ATTN_QB, ATTN_QB + 128), 0)
    kj = lax.broadcasted_iota(jnp.int32, (ATTN_QB, ATTN_QB + 128), 1)
    band = (kj >= qi) & (kj <= qi + 128)
    if dil == 16:
        q2 = lax.broadcasted_iota(jnp.int32, (ATTN_QB, ATTN_QB), 0)
        k2 = lax.broadcasted_iota(jnp.int32, (ATTN_QB, ATTN_QB), 1)
        return ((q2 // 128) == (k2 // 128)) & (k2 <= q2), None
    return band[:, 128:], band


def _attn_wide_bias(b, dil):
    if dil == 16:
        return jnp.tile(b[:, 128:], (4, 4)), None
    z = jnp.zeros((128, 128), F32)
    band = jnp.concatenate([jnp.concatenate([z] * i + [b] + [z] * (3 - i), axis=1) for i in range(4)], axis=0)
    return band[:, 128:], band


def _fold_dbias(dbs, dil, band_form):
    def blk(i, j):
        return dbs[128 * i:128 * i + 128, 128 * j:128 * j + 128]
    if band_form:
        return sum(blk(i, i) for i in range(4)), sum(blk(i, i + 1) for i in range(4))
    cur = sum(blk(i, i) for i in range(4))
    if dil == 16:
        return None, cur
    return sum(blk(i, i - 1) for i in range(1, 4)), cur


def _attn_chunks(dil):
    out = []
    for n in range(S // ATTN_QB):
        if dil == 1 and n > 0:
            out.append((n * ATTN_QB, n * ATTN_QB - 128, ATTN_QB + 128, True))
        else:
            out.append((n * ATTN_QB, n * ATTN_QB, ATTN_QB, False))
    return out


def _qkv_specs(gi, order):
    def spec(base):
        col = (base + 256 * gi) // 128
        return pl.BlockSpec((S, 128), lambda p, q: (order(p, q)[0], col + order(p, q)[1]))
    return [spec(O_Q), spec(O_K), spec(O_V)]


def _residue_rows(r, dil):
    return pl.ds(r, S // dil, stride=dil)


def _attn_fwd(hcat, bias_all, gi, name):
    dil = ATTN_DILS[gi]
    T = hcat.shape[0]
    B, L = T // S, S // dil

    def body(q_ref, k_ref, v_ref, b_ref, o_ref, l_ref, *scr):
        mask_first, mask_band = _attn_masks(dil)
        if dil > 1:
            qs, ks, vs, os_, ls = scr
            for r in range(dil):
                rows, dst = _residue_rows(r, dil), pl.ds(r * L, L)
                qs[dst, :] = q_ref[rows, :]
                ks[dst, :] = k_ref[rows, :]
                vs[dst, :] = v_ref[rows, :]
        else:
            qs, ks, vs, os_, ls = q_ref, k_ref, v_ref, o_ref, l_ref
        ls[...] = jnp.zeros_like(ls)
        for e in range(2):
            lanes = slice(64 * e, 64 * e + 64)
            bias_first, bias_band = _attn_wide_bias(b_ref[e], dil)
            for q0, k0, kn, band_form in _attn_chunks(dil):
                cur, keys = pl.ds(q0, ATTN_QB), pl.ds(k0, kn)
                o, l = _attn_block(qs[cur, lanes], ks[keys, lanes], vs[keys, lanes],
                                   bias_band if band_form else bias_first, mask_band if band_form else mask_first)
                os_[cur, lanes] = o
                ls[cur, e:e + 1] = l
        if dil > 1:
            for r in range(dil):
                rows, src = _residue_rows(r, dil), pl.ds(r * L, L)
                o_ref[rows, :] = os_[src, :]
                l_ref[rows, :] = ls[src, :]

    scratch = [pltpu.VMEM((S, 128), F32)] * 5 if dil > 1 else []
    return _pcall(body, name=name, grid=(B, 2),
                  in_specs=_qkv_specs(gi, lambda b, hp: (b, hp))
                  + [pl.BlockSpec((2, 128, 256), lambda b, hp: (2 * gi + hp, 0, 0))],
                  out_specs=[pl.BlockSpec((S, 128), lambda b, hp: (b, hp)),
                             pl.BlockSpec((None, S, 128), lambda b, hp: (hp, b, 0))],
                  out_shape=[jax.ShapeDtypeStruct((T, 256), F32), jax.ShapeDtypeStruct((2, T, 128), F32)],
                  scratch_shapes=scratch,
                  compiler_params=_cp(("parallel", "parallel")))(hcat, hcat, hcat, bias_all)


def _attn_bwd(hcat, bias_all, gi, do, dl, name):
    dil = ATTN_DILS[gi]
    T = hcat.shape[0]
    B, L = T // S, S // dil

    def body(q_ref, k_ref, v_ref, b_ref, do_ref, dl_ref, dq_ref, dk_ref, dv_ref, db_ref, acc_ref, *scr):
        bi = pl.program_id(1)
        mask_first, mask_band = _attn_masks(dil)
        if dil > 1:
            qs, ks, vs, dos, dls, dqs, dks, dvs = scr
            for r in range(dil):
                rows, dst = _residue_rows(r, dil), pl.ds(r * L, L)
                qs[dst, :] = q_ref[rows, :]
                ks[dst, :] = k_ref[rows, :]
                vs[dst, :] = v_ref[rows, :]
                dos[dst, :] = do_ref[rows, :]
                dls[dst, :] = dl_ref[rows, :]
        else:
            qs, ks, vs, dos, dls, dqs, dks, dvs = q_ref, k_ref, v_ref, do_ref, dl_ref, dq_ref, dk_ref, dv_ref
        dks[...] = jnp.zeros_like(dks)
        dvs[...] = jnp.zeros_like(dvs)
        for e in range(2):
            lanes = slice(64 * e, 64 * e + 64)
            bias_first, bias_band = _attn_wide_bias(b_ref[e], dil)
            acc_ref[...] = jnp.zeros_like(acc_ref)
            for q0, k0, kn, band_form in _attn_chunks(dil):
                cur, keys = pl.ds(q0, ATTN_QB), pl.ds(k0, kn)
                f = functools.partial(_attn_block, mask=mask_band if band_form else mask_first)
                _, vjp = jax.vjp(f, qs[cur, lanes], ks[keys, lanes], vs[keys, lanes],
                                 bias_band if band_form else bias_first)
                dq, dkb, dvb, dbs = vjp((dos[cur, lanes], dls[cur, e:e + 1]))
                dqs[cur, lanes] = dq
                dks[keys, lanes] += dkb
                dvs[keys, lanes] += dvb
                prev, here = _fold_dbias(dbs, dil, band_form)
                if prev is not None:
                    acc_ref[:, 0:128] += prev
                acc_ref[:, 128:256] += here

            @pl.when(bi == 0)
            def _(e=e):
                db_ref[e] = acc_ref[...]

            @pl.when(bi > 0)
            def _(e=e):
                db_ref[e] += acc_ref[...]

        if dil > 1:
            for r in range(dil):
                rows, src = _residue_rows(r, dil), pl.ds(r * L, L)
                dq_ref[rows, :] = dqs[src, :]
                dk_ref[rows, :] = dks[src, :]
                dv_ref[rows, :] = dvs[src, :]

    order = lambda hp, b: (b, hp)
    blk = pl.BlockSpec((S, 128), lambda hp, b: (b, hp))
    lblk = pl.BlockSpec((None, S, 128), lambda hp, b: (hp, b, 0))
    sds = jax.ShapeDtypeStruct((T, 256), F32)
    scratch = [pltpu.VMEM((128, 256), F32)] + ([pltpu.VMEM((S, 128), F32)] * 8 if dil > 1 else [])
    return _pcall(body, name=name, grid=(2, B),
                  in_specs=_qkv_specs(gi, order) + [pl.BlockSpec((2, 128, 256), lambda hp, b: (2 * gi + hp, 0, 0)), blk, lblk],
                  out_specs=[blk, blk, blk, pl.BlockSpec((2, 128, 256), lambda hp, b: (hp, 0, 0))],
                  out_shape=[sds, sds, sds, jax.ShapeDtypeStruct((4, 128, 256), F32)],
                  scratch_shapes=scratch,
                  compiler_params=_cp(("parallel", "arbitrary")))(hcat, hcat, hcat, bias_all, do, dl)


def _lse_merge(o0, o1, o2, l0, l1, l2):
    m = lax.stop_gradient(jnp.maximum(jnp.maximum(l0, l1), l2))
    e0, e1, e2 = jnp.exp(l0 - m), jnp.exp(l1 - m), jnp.exp(l2 - m)
    den = e0 + e1 + e2
    return (e0 / den) * o0 + (e1 / den) * o1 + (e2 / den) * o2


def _attn_merge(outs, lses, dy, name):
    T = outs[0].shape[0]
    bwd = dy is not None
    tm = 512

    def body(*refs):
        o_refs, l_refs = refs[:3], refs[3:6]
        if bwd:
            for r in refs[10:13]:
                r[...] = jnp.zeros_like(r)
        for e in range(2):
            lanes = slice(64 * e, 64 * e + 64)
            vals = [r[:, lanes] for r in o_refs] + [r[:, e:e + 1] for r in l_refs]
            if not bwd:
                refs[6][:, lanes] = _lse_merge(*vals).astype(refs[6].dtype)
            else:
                _, vjp = jax.vjp(_lse_merge, *vals)
                g = vjp(refs[6][:, lanes])
                for r, v in zip(refs[7:10], g[:3]):
                    r[:, lanes] = v
                for r, v in zip(refs[10:13], g[3:]):
                    r[:, e:e + 1] = v

    blk = pl.BlockSpec((tm, 128), lambda i, hp: (i, hp))
    lblk = pl.BlockSpec((None, tm, 128), lambda i, hp: (hp, i, 0))
    lsd = jax.ShapeDtypeStruct((2, T, 128), F32)
    if not bwd:
        return _pcall(body, name=name, grid=(T // tm, 2), in_specs=[blk] * 3 + [lblk] * 3, out_specs=blk,
                      out_shape=jax.ShapeDtypeStruct((T, 256), F32),
                      compiler_params=_cp(("parallel", "parallel")))(*outs, *lses)
    return _pcall(body, name=name, grid=(T // tm, 2), in_specs=[blk] * 3 + [lblk] * 3 + [blk],
                  out_specs=[blk] * 3 + [lblk] * 3, out_shape=[jax.ShapeDtypeStruct((T, 256), F32)] * 3 + [lsd] * 3,
                  compiler_params=_cp(("parallel", "parallel")))(*outs, *lses, dy)


def _gmerge(g0, g1, g2, gb, ya, yb, yc):
    return (jax.nn.sigmoid(g0 + gb[:, 0:D]) * ya + jax.nn.sigmoid(g1 + gb[:, D:2 * D]) * yb
            + jax.nn.sigmoid(g2 + gb[:, 2 * D:3 * D]) * yc)


def _gated_merge(gates, gb, ya, yb, yc, name, gcol=0):
    def fn(tv, fv):
        return [_gmerge(tv[0], tv[1], tv[2], fv[0], tv[3], tv[4], tv[5])], []
    return _rowwise(fn, [(gates, D, gcol), (gates, D, gcol + 1), (gates, D, gcol + 2), ya, yb, yc], [gb], [(D, _ACT)], [],
                    name=name)[0]


def _gated_merge_bwd(gates, gb, ya, yb, yc, dm, name, gcol=0):
    def fn(tv, fv):
        _, vjp = jax.vjp(_gmerge, tv[0], tv[1], tv[2], fv[0], tv[3], tv[4], tv[5])
        d0, d1, d2, dgb, da, db, dc = vjp(tv[6])
        return [[d0, d1, d2], da, db, dc], [dgb]
    return _rowwise(fn, [(gates, D, gcol), (gates, D, gcol + 1), (gates, D, gcol + 2), ya, yb, yc, dm], [gb],
                    [(3 * D, _ACT), (D, _ACT), (D, _ACT), (D, _ACT)], [(1, 3 * D)], name=name)


def _pool_affine(t1, pb, ps, dout, name):
    if dout is None:
        def fn(tv, fv):
            return [(tv[0] + fv[0]) * fv[1]], []
        return _rowwise(fn, [t1], [pb, ps], [(POOLW, _ACT)], [], name=name)[0]

    def fnb(tv, fv):
        t2, vjp = jax.vjp(lambda t, b, s: (t + b) * s, tv[0], fv[0], fv[1])
        dt, db, dsc = vjp(tv[1])
        return [dt, t2], [db, dsc]
    return _rowwise(fnb, [t1, dout], [pb, ps], [(POOLW, _ACT), (POOLW, _ACT)], [(1, POOLW), (1, POOLW)], name=name)


def _dt_softplus(dt_raw, dt_bias, ddt, name):
    f = lambda r, b: _softplus(r + b)
    if ddt is None:
        def fn(tv, fv):
            return [f(tv[0], fv[0])], []
        return _rowwise(fn, [dt_raw], [dt_bias], [(16, F32)], [], name=name, tm=1024)[0]

    def fnb(tv, fv):
        _, vjp = jax.vjp(f, tv[0], fv[0])
        dr, db = vjp(tv[1])
        return [dr], [db]
    return _rowwise(fnb, [dt_raw, ddt], [dt_bias], [(16, F32)], [(1, 16)], name=name, tm=1024)


def _adamw(w, g, m, v, name):
    R, C = w.shape
    tm = _pick(R, (256, 128, 64, 32, 16, 8))
    c1 = 1.0 / (1.0 - ADAM_B1 ** ADAM_STEP)
    c2 = 1.0 / (1.0 - ADAM_B2 ** ADAM_STEP)

    def fn(tv, fv):
        wv, gv, mv, vv = tv
        mn = ADAM_B1 * mv + (1.0 - ADAM_B1) * gv
        vn = ADAM_B2 * vv + (1.0 - ADAM_B2) * (gv * gv)
        delta = -ADAM_LR * ((mn * c1) / (jnp.sqrt(vn * c2) + ADAM_EPS) + ADAM_WD * wv)
        return [delta, mn, vn], []
    return _rowwise(fn, [w, g, m, v], [], [(C, F32)] * 3, [], name=name, tm=tm)


def _ffn_fwd(x, xm, w13, w2, g, b, tag, dep=None):
    h = _mm(xm, w13, dep=dep, name=f"{tag}_h")
    s = _swiglu_act(h, name=f"{tag}_act")
    y = _mm(s, w2, name=f"{tag}_y")
    r, out, outm = _res_ln_fwd(x, y, g, b, 0.5, name=f"{tag}_ln")
    return out, outm, dict(x=xm, h=h, r=r)


def _ffn_bwd(dout, sv, w13, w2, g, b, tag, dep=None):
    dskip, dy, dg, db = _ln_bwd(sv['r'], g, b, dout, 0.5, name=f"{tag}_lnb")
    ds = _mm(dy, w2, tb=True, dep=dep, name=f"{tag}_ds")
    dh, s = _swiglu_act_bwd(sv['h'], ds, name=f"{tag}_actb")
    dw2 = _mm(s, dy, ta=True, name=f"{tag}_dw2")
    dw13 = _mm(sv['x'], dh, ta=True, name=f"{tag}_dw13")
    dx = _mm(dh, w13, tb=True, add=dskip, name=f"{tag}_dx")
    return dx, dict(w13=dw13, w2=dw2, g=dg, b=db)


def _mixer_fwd(x1, x1m, W, bias_all, tag, dep=None):
    T = x1.shape[0]
    hcat = _mm(x1m, W['w_in_r'], dep=dep, name=f"{tag}_hcat")
    dt_raw = hcat[:, O_DT:O_DT + 16]
    pooled = _pool_mean(hcat, False, name=f"{tag}_pool", col0=O_U // 128)
    t1 = _mm(pooled, W['pool_wbd'], name=f"{tag}_pt1")
    t2 = _pool_affine(t1, W['pool_b'], W['pool_scale'], None, name=f"{tag}_paff")
    ya = _mm(t2, W['p_pool'], name=f"{tag}_ya")
    act = _conv_silu(hcat, W['conv_w'], W['conv_b'], name=f"{tag}_conv", col0=O_XBC // 128)
    dt = _dt_softplus(dt_raw, W['dt_bias'], None, name=f"{tag}_dt")
    dtg = dt.reshape(T, 4, 4).transpose(1, 0, 2)
    yscan, hs = _ssd_fwd(act, dtg, W['a_neg'], W['d_skip'], name=f"{tag}_ssd")
    ybn = _ssd_gate_norm(yscan, hcat, W['ssd_norm'], name=f"{tag}_gn", zcol=O_Z // 256)
    yb = _mm(ybn, W['p_ssd'], name=f"{tag}_yb")
    outs, lses = [], []
    for gi in range(len(ATTN_DILS)):
        o, l = _attn_fwd(hcat, bias_all, gi, name=f"{tag}_attn{gi}")
        outs.append(o)
        lses.append(l)
    ycp = _attn_merge(outs, lses, None, name=f"{tag}_amerge")
    yc = _mm(ycp, W['p_attn'], name=f"{tag}_yc")
    merged = _gated_merge(hcat, W['gate_b'], ya, yb, yc, name=f"{tag}_gm", gcol=O_G // D)
    mix = _mm(merged, W['w_out'], name=f"{tag}_mix")
    r, out, outm = _res_ln_fwd(x1, mix, W['ln2_g'], W['ln2_b'], 1.0, name=f"{tag}_ln")
    sv = dict(x1=x1m, dt_raw=dt_raw, pooled=pooled, t1=t1, act=act, dtg=dtg,
              hs=hs, yscan=yscan, ybn=ybn, hcat=hcat, outs=outs, lses=lses, ycp=ycp, ya=ya, yb=yb, yc=yc,
              merged=merged, r=r)
    return out, outm, sv


def _mixer_bwd(dout, sv, W, bias_all, tag, dep=None):
    T = dout.shape[0]
    gr = {}
    dx1a, dr, gr['ln2_g'], gr['ln2_b'] = _ln_bwd(sv['r'], W['ln2_g'], W['ln2_b'], dout, 1.0, name=f"{tag}_lnb")
    dmerged = _mm(dr, W['w_out'], tb=True, dep=dep, name=f"{tag}_dmerged")
    gr['w_out'] = _mm(sv['merged'], dr, ta=True, name=f"{tag}_dwout")
    dgates, dya, dyb, dyc, gr['gate_b'] = _gated_merge_bwd(sv['hcat'], W['gate_b'], sv['ya'], sv['yb'], sv['yc'],
                                                           dmerged, name=f"{tag}_gmb", gcol=O_G // D)
    dycp = _mm(dyc, W['p_attn'], tb=True, name=f"{tag}_dycp")
    gr['p_attn'] = _mm(sv['ycp'], dyc, ta=True, name=f"{tag}_dpattn")
    dml = _attn_merge(sv['outs'], sv['lses'], dycp, name=f"{tag}_amergeb")
    dq, dk, dv, dbias = [], [], [], []
    for gi in range(len(ATTN_DILS)):
        a, b, c, d = _attn_bwd(sv['hcat'], bias_all, gi, dml[gi], dml[3 + gi], name=f"{tag}_attnb{gi}")
        dq.append(a)
        dk.append(b)
        dv.append(c)
        dbias.append(d)
    dbias = jnp.concatenate(dbias, axis=0)
    dybn = _mm(dyb, W['p_ssd'], tb=True, name=f"{tag}_dybn")
    gr['p_ssd'] = _mm(sv['ybn'], dyb, ta=True, name=f"{tag}_dpssd")
    dyscan, dz, gr['ssd_norm'] = _ssd_gate_norm_bwd(sv['yscan'], sv['hcat'], W['ssd_norm'], dybn, name=f"{tag}_gnb",
                                                    zcol=O_Z // 256)
    dxs, ddtg, dbm, dcm, dak = _ssd_bwd(sv['act'], sv['dtg'], W['a_neg'], W['d_skip'], sv['hs'], dyscan,
                                        name=f"{tag}_ssdb")
    gr['a_neg'], gr['d_skip'] = dak[:, 0, 0:4], dak[:, 1, 0:4]
    ddt = ddtg.transpose(1, 0, 2).reshape(T, 16)
    ddt_raw, gr['dt_bias'] = _dt_softplus(sv['dt_raw'], W['dt_bias'], ddt, name=f"{tag}_dtb")
    dact = jnp.concatenate([dxs, dbm, dcm], axis=1)
    dxbc, gr['conv_w'], gr['conv_b'] = _conv_silu_bwd(sv['hcat'], W['conv_w'], W['conv_b'], dact, name=f"{tag}_convb",
                                                      col0=O_XBC // 128)
    dt2 = _mm(dya, W['p_pool'], tb=True, name=f"{tag}_dt2")
    dt1, t2, gr['pool_b'], gr['pool_scale'] = _pool_affine(sv['t1'], W['pool_b'], W['pool_scale'], dt2, name=f"{tag}_paffb")
    gr['p_pool'] = _mm(t2, dya, ta=True, name=f"{tag}_dppool")
    dpooled = _mm(dt1, W['pool_wbd'], tb=True, name=f"{tag}_dpooled")
    gr['pool_wbd'] = _mm(sv['pooled'], dt1, ta=True, name=f"{tag}_dpoolw")
    du = _pool_mean(dpooled, True, name=f"{tag}_poolb")
    dhcat = jnp.concatenate([t.astype(_ACT) for t in [du, dz, dxbc] + dq + dk + dv + [dgates, ddt_raw]]
                            + [jnp.zeros((T, HC - O_DT - 16), _ACT)], axis=1)
    dx1 = _mm(dhcat, W['w_in_r'], tb=True, add=dx1a, name=f"{tag}_dx1")
    gr['w_in_r'] = _mm(sv['x1'], dhcat, ta=True, name=f"{tag}_dwin")
    return dx1, gr, dbias


def _prep_layer_weights(i, inp, G):
    W = {}
    for n in BIG:
        if n not in G:
            continue
        g = G[n]
        if n == 'w_in':
            W['w_in_r'] = jnp.concatenate(_nat_pieces(g, 0, 3840) + _nat_pieces(g, 3856, 9232) + _nat_pieces(g, 3840, 3856)
                                          + [jnp.zeros((D, HC - 9232), g.dtype)], axis=1)
        elif n in COL_SHARDED:
            W[n] = jnp.concatenate([g[j] for j in range(4)], axis=1)
        else:
            W[n] = g.reshape(4 * g.shape[1], g.shape[2])
    pw = inp['pool_w'][i].astype(_MXU)
    wbd = jnp.zeros((POOLW, POOLW), _MXU)
    for g in range(4):
        wbd = lax.dynamic_update_slice(wbd, pw[g], (g * POOL_GDIM, g * POOL_GDIM))
    W['pool_wbd'] = wbd
    W['pool_b'] = inp['pool_b'][i].reshape(1, POOLW)
    W['pool_scale'] = inp['pool_scale'][i].reshape(1, POOLW)
    if 'conv_w' in G:
        W['conv_w'] = jnp.concatenate([G['conv_w'][j] for j in range(4)], axis=1)
        W['gate_b'] = jnp.concatenate([G['gate_b'][j][b:b + 1] for b in range(3) for j in range(4)], axis=1)
    W['conv_b'] = inp['conv_b'][i].reshape(1, 2048)
    W['dt_bias'] = inp['dt_bias'][i].reshape(1, 16)
    W['a_neg'] = (-jnp.exp(inp['a_log'][i])).reshape(4, 1, 4)
    W['d_skip'] = inp['d_skip'][i].reshape(4, 1, 4)
    W['ssd_norm'] = inp['ssd_norm'][i].reshape(1, D)
    for n in ('ln1_g', 'ln1_b', 'ln2_g', 'ln2_b', 'ln3_g', 'ln3_b'):
        W[n] = inp[n][i].reshape(1, D)
    return W


GATHER_FIRST = ['ffn1_w13', 'ffn1_w2']
GATHER_REST = [n for n in BIG if n not in GATHER_FIRST] + ['gate_b', 'conv_w']


def _gather_start(inp, i, names):
    core = lax.axis_index("c")
    arrs = []
    for n in names:
        s = inp[n][i]
        if n in BIG:
            s = lax.dynamic_slice_in_dim(s, core * (s.shape[0] // 2), s.shape[0] // 2, axis=0).astype(BF16)
        arrs.append(s)
    state, token = _exchange_start(arrs, "chips", "gather", name="gather_start")
    return (names, state), token


def _gather_mid(handle, after):
    names, state = handle
    me = 2 * lax.axis_index("x") + lax.axis_index("y")
    own, outs = _exchange_wait(state, after, "chips", "gather", name="gather_wait")
    outs = [lax.dynamic_update_slice(o, a[None], (me, 0, 0)) for o, a in zip(outs, own)]
    big = [o for n, o in zip(names, outs) if n in BIG]
    state, token = _exchange_start(big, "cores", "gather", name="share_start")
    return (names, outs, state), token


def _gather_finish(handle, after):
    names, outs, state = handle
    core = lax.axis_index("c")
    mine, theirs = _exchange_wait(state, after, "cores", "gather", name="share_wait")
    G = {n: o for n, o in zip(names, outs) if n not in BIG}
    for n, a, b in zip([n for n in names if n in BIG], mine, theirs):
        G[n] = jnp.concatenate([jnp.where(core == 0, a, b), jnp.where(core == 0, b, a)], axis=1)
    return G


W_IN_SHARD = 2308


def _nat_pieces(g, lo, hi):
    out = []
    for j in range(4):
        s, e = max(lo, W_IN_SHARD * j), min(hi, W_IN_SHARD * (j + 1))
        if s < e:
            out.append(g[j][:, s - W_IN_SHARD * j:e - W_IN_SHARD * j])
    return out


def _reord_ranges(lo, hi):
    out = []
    for a, b, off in ((0, 3840, 0), (3840, 3856, O_DT - 3840), (3856, 9232, -16)):
        s, e = max(lo, a), min(hi, b)
        if s < e:
            out.append((s + off, e + off))
    return out


def _halves_of(n, g):
    if n == 'w_in':
        shards = [jnp.concatenate([g[:, a:b] for a, b in _reord_ranges(W_IN_SHARD * j, W_IN_SHARD * (j + 1))], axis=1)
                  for j in range(4)]
    elif n in COL_SHARDED:
        c = g.shape[1] // 4
        shards = [g[:, j * c:(j + 1) * c] for j in range(4)]
    else:
        r = g.shape[0] // 4
        shards = [g[j * r:(j + 1) * r] for j in range(4)]
    r2 = shards[0].shape[0] // 2
    return jnp.stack([jnp.concatenate([s[h * r2:(h + 1) * r2] for s in shards], axis=0) for h in range(2)])


def _reduce_a(grads):
    names = list(grads)
    halves = [_halves_of(n, grads[n]) for n in names]
    state, token = _exchange_start(halves, "cores", "scatter", name="rsc_start")
    return (names, state), token


def _reduce_b(handle, after):
    names, state = handle
    core = lax.axis_index("c").reshape(1)
    halves, got = _exchange_wait(state, after, "cores", "scatter", name="rsc_wait")
    chip = [_sum_own_recv(h, t, core, BF16, name="rs_sum2") for h, t in zip(halves, got)]
    chip = [t.reshape(4, t.shape[0] // 4, t.shape[1]) for t in chip]
    state, token = _exchange_start(chip, "chips", "scatter", name="rs_start")
    return (names, state), token


def _reduce_c(handle, after):
    names, state = handle
    chip_id = (2 * lax.axis_index("x") + lax.axis_index("y")).reshape(1)
    chip, got = _exchange_wait(state, after, "chips", "scatter", name="rs_wait")
    red = [_sum_own_recv(h, t, chip_id, F32, name="rs_sum4") for h, t in zip(chip, got)]
    other = _exchange(red, "cores", "gather", name="rs_share")
    out = {}
    for n, mine, theirs in zip(names, red, other):
        out[n] = jnp.where(lax.axis_index("c") == 0, jnp.concatenate([mine, theirs]), jnp.concatenate([theirs, mine]))
    return out


class _Comm:
    def __init__(self, inp):
        self.inp = inp

    def gather_start(self, i, names):
        return _gather_start(self.inp, i, names)

    gather_mid = staticmethod(_gather_mid)
    gather_finish = staticmethod(_gather_finish)

    def reduce_a(self, i, grads):
        return _reduce_a({n: grads[n] for n in BIG})

    reduce_b = staticmethod(_reduce_b)
    reduce_c = staticmethod(_reduce_c)


def _allreduce_small(vec):
    for group in ("cores", "x", "y"):
        recv = _exchange([vec], group, "gather", name=f"ar_{group}")[0]
        vec = _rowwise(lambda tv, fv: ([tv[0] + tv[1]], []), [vec, recv], [], [(128, F32)], [], name=f"ar_add_{group}")[0]
    return vec


def _pack(arrs):
    flat = jnp.concatenate([a.reshape(-1) for a in arrs])
    n = flat.shape[0]
    pad = (-n) % (256 * 128)
    flat = jnp.concatenate([flat, jnp.zeros((pad,), F32)])
    return flat.reshape(-1, 128)


def _unpack(p, shapes):
    flat = p.reshape(-1)
    out, off = [], 0
    for s in shapes:
        sz = int(np.prod(s))
        out.append(flat[off:off + sz].reshape(s))
        off += sz
    return out


def _forward_backward(inp, comm, bias_all):
    x = xm = inp['x'].reshape(-1, D)
    tgt = inp['loss_target'].reshape(-1, D)
    saved, Ws = [], []
    h_first, _ = comm.gather_start(0, GATHER_FIRST)
    h_rest, dep = comm.gather_start(0, GATHER_REST)
    h_first, tok = comm.gather_mid(h_first, x)
    G = comm.gather_finish(h_first, tok)
    for i in range(NL):
        W = _prep_layer_weights(i, inp, G)
        start_next = lambda: (comm.gather_start(i + 1, BIG + ['gate_b', 'conv_w']) if i + 1 < NL else (None, None))
        if i > 0:
            h_next, dep = start_next()
        x1, x1m, s1 = _ffn_fwd(x, xm, W['ffn1_w13'], W['ffn1_w2'], W['ln1_g'], W['ln1_b'], "f1", dep)
        if i == 0:
            h_rest, tok = comm.gather_mid(h_rest, x1m)
            W.update(_prep_layer_weights(i, inp, comm.gather_finish(h_rest, tok)))
            h_next, dep = start_next()
        x2, x2m, s2 = _mixer_fwd(x1, x1m, W, bias_all, "mx", dep if i == 0 else None)
        dep = None
        if h_next is not None:
            h_next, dep = comm.gather_mid(h_next, x2m)
        x, xm, s3 = _ffn_fwd(x2, x2m, W['ffn2_w13'], W['ffn2_w2'], W['ln3_g'], W['ln3_b'], "f2", dep)
        if h_next is not None:
            G = comm.gather_finish(h_next, xm)
        saved.append((s1, s2, s3))
        Ws.append(W)
    dy, lpart = _loss_fwd_bwd(x, tgt, name="loss")
    fins, reduced, dbiases = [None] * NL, [None] * NL, [None] * NL
    pend_a, pend_b, dep = None, None, None
    for i in reversed(range(NL)):
        W = Ws[i]
        s1, s2, s3 = saved[i]
        g = {}
        dx2, f = _ffn_bwd(dy, s3, W['ffn2_w13'], W['ffn2_w2'], W['ln3_g'], W['ln3_b'], "f2", dep)
        g['ffn2_w13'], g['ffn2_w2'], g['ln3_g'], g['ln3_b'] = f['w13'], f['w2'], f['g'], f['b']
        dep = None
        if pend_a is not None:
            handle, dep = comm.reduce_b(pend_a[1], dx2)
            pend_b = (pend_a[0], handle)
        dx1, gm, dbiases[i] = _mixer_bwd(dx2, s2, W, bias_all, "mx", dep)
        g.update(gm)
        dy, f = _ffn_bwd(dx1, s1, W['ffn1_w13'], W['ffn1_w2'], W['ln1_g'], W['ln1_b'], "f1")
        g['ffn1_w13'], g['ffn1_w2'], g['ln1_g'], g['ln1_b'] = f['w13'], f['w2'], f['g'], f['b']
        fins[i] = _finish_layer_grads(i, g, inp)
        if pend_b is not None:
            reduced[pend_b[0]] = comm.reduce_c(pend_b[1], dy)
            pend_b = None
        handle, dep = comm.reduce_a(i, fins[i])
        pend_a = (i, handle)
    return lpart, dy, fins, reduced, pend_a, dbiases


def _finish_layer_grads(i, g, inp):
    out = {n: g[n] for n in BIG if n != 'w_in'}
    out['w_in'] = g['w_in_r']
    out['pool_w'] = jnp.stack([g['pool_wbd'][k * POOL_GDIM:(k + 1) * POOL_GDIM, k * POOL_GDIM:(k + 1) * POOL_GDIM] for k in range(4)])
    out['pool_b'] = g['pool_b'].reshape(4, POOL_GDIM)
    out['pool_scale'] = g['pool_scale'].reshape(POOLW)
    out['conv_w'] = g['conv_w']
    out['conv_b'] = g['conv_b'].reshape(2048)
    out['dt_bias'] = g['dt_bias'].reshape(16)
    out['a_log'] = (g['a_neg'].reshape(16)) * (-jnp.exp(inp['a_log'][i]))
    out['d_skip'] = g['d_skip'].reshape(16)
    out['ssd_norm'] = g['ssd_norm'].reshape(D)
    out['gate_b'] = g['gate_b'].reshape(3, D)
    for n in ('ln1_g', 'ln1_b', 'ln2_g', 'ln2_b', 'ln3_g', 'ln3_b'):
        out[n] = g[n].reshape(D)
    return out


def kernel(x, ffn1_w13, ffn1_w2, ln1_g, ln1_b, w_in, gate_b, pool_w, pool_b, pool_scale, conv_w, conv_b,
           dt_bias, a_log, d_skip, ssd_norm, rel_bias, p_pool, p_ssd, p_attn, w_out, ln2_g, ln2_b, ffn2_w13,
           ffn2_w2, ln3_g, ln3_b, loss_target, m_ffn1_w13, m_ffn1_w2, m_ln1_g, m_ln1_b, m_w_in, m_gate_b,
           m_pool_w, m_pool_b, m_pool_scale, m_conv_w, m_conv_b, m_dt_bias, m_a_log, m_d_skip, m_ssd_norm,
           m_rel_bias, m_p_pool, m_p_ssd, m_p_attn, m_w_out, m_ln2_g, m_ln2_b, m_ffn2_w13, m_ffn2_w2, m_ln3_g,
           m_ln3_b, v_ffn1_w13, v_ffn1_w2, v_ln1_g, v_ln1_b, v_w_in, v_gate_b, v_pool_w, v_pool_b,
           v_pool_scale, v_conv_w, v_conv_b, v_dt_bias, v_a_log, v_d_skip, v_ssd_norm, v_rel_bias, v_p_pool,
           v_p_ssd, v_p_attn, v_w_out, v_ln2_g, v_ln2_b, v_ffn2_w13, v_ffn2_w2, v_ln3_g, v_ln3_b):
    inp = dict(locals())
    maps = jnp.asarray(_bucket_maps())
    bias_all = _bias_build(rel_bias, maps, name="bias_build")
    comm = _Comm(inp)
    lpart, gx, fins, red, pending, dbiases = _forward_backward(inp, comm, bias_all)
    loss = lax.psum(lpart[0, 0], ("x", "y", "c"))

    small_l = [n for n in SMALL if n != 'rel_bias']
    drel = _bias_reduce(jnp.stack(dbiases), maps, name="bias_reduce")[:, 0, :32].T
    handle_b, _ = comm.reduce_b(pending[1], drel)
    small_arrs = [jnp.stack([fins[i][n] for i in range(NL)]) for n in small_l] + [drel]
    packed = _allreduce_small(_pack(small_arrs))
    gsmall = dict(zip(small_l + ['rel_bias'], _unpack(packed, [a.shape for a in small_arrs])))
    shard = 2 * lax.axis_index("x") + lax.axis_index("y")
    gsmall['gate_b'] = lax.dynamic_slice_in_dim(gsmall['gate_b'], shard * 256, 256, axis=2)
    gsmall['conv_w'] = lax.dynamic_slice_in_dim(gsmall['conv_w'], shard * 512, 512, axis=2)

    red[pending[0]] = comm.reduce_c(handle_b, packed)
    gout = {n: jnp.stack([red[i][n] for i in range(NL)]) for n in BIG}
    gout.update(gsmall)

    delta, new_m, new_v = {}, {}, {}
    for n in BIG:
        shp = inp[n].shape
        two_d = lambda a: a.reshape(shp[0] * shp[1], shp[2])
        d, m, v = _adamw(two_d(inp[n]), two_d(gout[n]), two_d(inp['m_' + n]), two_d(inp['v_' + n]), name="adamw_big")
        delta[n], new_m[n], new_v[n] = d.reshape(shp), m.reshape(shp), v.reshape(shp)
    shapes = [inp[n].shape for n in SMALL]
    d, m, v = _adamw(_pack([inp[n] for n in SMALL]), _pack([gout[n] for n in SMALL]),
                     _pack([inp['m_' + n] for n in SMALL]), _pack([inp['v_' + n] for n in SMALL]), name="adamw_small")
    for n, dd, mm, vv in zip(SMALL, _unpack(d, shapes), _unpack(m, shapes), _unpack(v, shapes)):
        delta[n], new_m[n], new_v[n] = dd, mm, vv

    return (loss, gx.reshape(x.shape), *[gout[n] for n in WEIGHTS], *[delta[n] for n in WEIGHTS],
            *[new_m[n] for n in WEIGHTS], *[new_v[n] for n in WEIGHTS])
```

```python
import functools

import numpy as np
import jax
import jax.numpy as jnp
from jax import lax
from jax.experimental import pallas as pl
from jax.experimental.pallas import tpu as pltpu

F32 = jnp.float32
BF16 = jnp.bfloat16
_MXU = jnp.bfloat16
_ACT = jnp.bfloat16
_VMEM_LIMIT = 56 * 1024 * 1024

S = 2048
D = 1024
NL = 4
DFF = 2816
LN_EPS = 1e-5
SSD_EPS = 1e-5
ALPHA = (2.0 * NL) ** 0.25
POOLW = 768
POOL_WINDOWS = (2, 4, 8, 16)
POOL_GDIM = 192
CH = 128
ATTN_DILS = (1, 4, 16)
HC = 9728
O_U, O_Z, O_XBC, O_Q, O_K, O_V, O_G, O_DT = 0, 768, 1792, 3840, 4608, 5376, 6144, 9216

ADAM_LR, ADAM_B1, ADAM_B2, ADAM_EPS, ADAM_WD, ADAM_STEP = 0.001, 0.9, 0.999, 1e-08, 0.01, 10

WEIGHTS = ['ffn1_w13', 'ffn1_w2', 'ln1_g', 'ln1_b', 'w_in', 'gate_b', 'pool_w', 'pool_b', 'pool_scale', 'conv_w',
           'conv_b', 'dt_bias', 'a_log', 'd_skip', 'ssd_norm', 'rel_bias', 'p_pool', 'p_ssd', 'p_attn', 'w_out',
           'ln2_g', 'ln2_b', 'ffn2_w13', 'ffn2_w2', 'ln3_g', 'ln3_b']
BIG = ['ffn1_w13', 'ffn1_w2', 'w_in', 'p_pool', 'p_ssd', 'p_attn', 'w_out', 'ffn2_w13', 'ffn2_w2']
COL_SHARDED = {'ffn1_w13', 'ffn2_w13', 'w_in', 'p_pool', 'p_attn'}
SMALL = [n for n in WEIGHTS if n not in BIG]


def _pcall(body, **kw):
    return pl.pallas_call(body, **kw)


def _cp(sem=None):
    return pltpu.CompilerParams(dimension_semantics=sem, vmem_limit_bytes=_VMEM_LIMIT)


def _pick(n, cands):
    for c in cands:
        if n % c == 0:
            return c
    raise ValueError(f"no tile for {n}")


def _mm(a, b, *, ta=False, tb=False, add=None, out_dtype=F32, dep=None, name):
    if ta:
        K, M = a.shape
    else:
        M, K = a.shape
    if tb:
        N, K2 = b.shape
    else:
        K2, N = b.shape
    assert K == K2, (a.shape, b.shape, ta, tb)
    sa, sb, so = a.dtype.itemsize, b.dtype.itemsize, jnp.dtype(out_dtype).itemsize
    tm, tn, tk = _mm_tiles(M, N, K, sa, sb, so + (4 if add is not None else 0))
    nk = K // tk
    a_bytes, b_bytes = M * K * sa, K * N * sb
    j_outer = nk == 1 and (b_bytes + a_bytes * (N // tn) < a_bytes + b_bytes * (M // tm))
    ij = (lambda p, q: (q, p)) if j_outer else (lambda p, q: (p, q))

    def im(f):
        return lambda p, q, k: f(*ij(p, q), k)

    a_spec = pl.BlockSpec((tk, tm), im(lambda i, j, k: (k, i))) if ta else pl.BlockSpec((tm, tk), im(lambda i, j, k: (i, k)))
    b_spec = pl.BlockSpec((tn, tk), im(lambda i, j, k: (j, k))) if tb else pl.BlockSpec((tk, tn), im(lambda i, j, k: (k, j)))
    o_spec = pl.BlockSpec((tm, tn), im(lambda i, j, k: (i, j)))
    dims = (((0 if ta else 1,), (1 if tb else 0,)), ((), ()))
    has_add = add is not None

    n_in = 2 + int(has_add) + int(dep is not None)

    def body(*refs):
        a_ref, b_ref = refs[0], refs[1]
        add_ref = refs[2] if has_add else None
        o_ref = refs[n_in]
        part = lax.dot_general(a_ref[...].astype(_MXU), b_ref[...].astype(_MXU), dims, preferred_element_type=F32)

        def finish(r):
            if has_add:
                r = r + add_ref[...]
            o_ref[...] = r.astype(out_dtype)

        if nk == 1:
            finish(part)
        else:
            acc = refs[-1]
            k = pl.program_id(2)

            @pl.when(k == 0)
            def _():
                acc[...] = part

            @pl.when(k > 0)
            def _():
                acc[...] += part

            @pl.when(k == nk - 1)
            def _():
                finish(acc[...])

    in_specs = [a_spec, b_spec]
    args = [a, b]
    if has_add:
        in_specs.append(o_spec)
        args.append(add)
    if dep is not None:
        in_specs.append(pl.BlockSpec(memory_space=pl.ANY))
        args.append(dep)
    gm, gn = M // tm, N // tn
    return _pcall(
        body, name=name, grid=((gn, gm, nk) if j_outer else (gm, gn, nk)), in_specs=in_specs, out_specs=o_spec,
        out_shape=jax.ShapeDtypeStruct((M, N), out_dtype),
        scratch_shapes=([pltpu.VMEM((tm, tn), F32)] if nk > 1 else []),
        compiler_params=_cp(("parallel", "parallel", "arbitrary")),
    )(*args)


_MM_VMEM_BUDGET = 40 * 1024 * 1024


def _divisors128(n, cap):
    return [d for d in range(128, min(n, cap) + 1, 128) if n % d == 0][::-1]


_MM_CYC_PER_MMAC = 4.35
_MM_CYC_PER_ACC_VREG = 2.03
_MM_HBM_BYTES_PER_CYC = 1455.0
_MM_CYC_PER_STEP = 770.0


def _mm_tiles(M, N, K, sa, sb, so):
    best = None
    for tm in _divisors128(M, 1408):
        for tn in _divisors128(N, 2560):
            for tk in ([K] if K <= 4096 else []) + _divisors128(K, 2816):
                nk = K // tk
                need = 2 * (tm * tk * sa + tk * tn * sb + tm * tn * so) + (tm * tn * 4 if nk > 1 else 0)
                need += tm * tk * 2 + tk * tn * 2 + tm * tn * 4
                if need > _MM_VMEM_BUDGET:
                    continue
                gm, gn = M // tm, N // tn
                a_bytes, b_bytes = M * K * sa, K * N * sb
                hbm = min(b_bytes + a_bytes * gn, a_bytes + b_bytes * gm) if nk == 1 else a_bytes * gn + b_bytes * gm
                hbm += M * N * so
                work = _MM_CYC_PER_MMAC * M * N * K / 1e6 + _MM_CYC_PER_ACC_VREG * (M * N / 1024) * (nk if nk > 1 else 0.5)
                cost = max(work, hbm / _MM_HBM_BYTES_PER_CYC) + gm * gn * nk * _MM_CYC_PER_STEP
                if best is None or cost < best[0]:
                    best = (cost, (tm, tn, tk))
    assert best is not None, (M, N, K)
    return best[1]


def _store(ref, val):
    if isinstance(val, (list, tuple)):
        off = 0
        for p in val:
            w = p.shape[1]
            ref[:, off:off + w] = p.astype(ref.dtype)
            off += w
    else:
        ref[...] = val.astype(ref.dtype)


def _acc_store(ref, val, first):
    pieces = val if isinstance(val, (list, tuple)) else [val]
    off = 0
    for p in pieces:
        w = p.shape[1]

        @pl.when(first)
        def _(p=p, off=off, w=w):
            ref[:, off:off + w] = p

        @pl.when(jnp.logical_not(first))
        def _(p=p, off=off, w=w):
            ref[:, off:off + w] += p

        off += w


def _rowwise(fn, tiled, full, out_tiled, out_acc, *, name, tm=256):
    arrs, specs = [], []
    for t in tiled:
        arr, w, cb = t if isinstance(t, tuple) else (t, t.shape[1], 0)
        arrs.append(arr)
        specs.append(pl.BlockSpec((tm, w), functools.partial(lambda i, cb: (i, cb), cb=cb)))
    R = arrs[0].shape[0]
    assert R % tm == 0
    for f in full:
        arrs.append(f)
        specs.append(pl.BlockSpec(f.shape, functools.partial(lambda i, nd: (0,) * nd, nd=f.ndim)))
    nt, nf, no = len(tiled), len(full), len(out_tiled)

    def body(*refs):
        tv = [r[...] for r in refs[:nt]]
        fv = [r[...] for r in refs[nt:nt + nf]]
        ot, oa = fn(tv, fv)
        for r, v in zip(refs[nt + nf:nt + nf + no], ot):
            _store(r, v)
        first = pl.program_id(0) == 0
        for r, v in zip(refs[nt + nf + no:], oa):
            _acc_store(r, v, first)

    out_shape = [jax.ShapeDtypeStruct((R, c), dt) for c, dt in out_tiled]
    out_specs = [pl.BlockSpec((tm, c), lambda i: (i, 0)) for c, _ in out_tiled]
    for shp in out_acc:
        out_shape.append(jax.ShapeDtypeStruct(shp, F32))
        out_specs.append(pl.BlockSpec(shp, lambda i: (0, 0)))
    return _pcall(body, name=name, grid=(R // tm,), in_specs=specs, out_specs=out_specs, out_shape=out_shape,
                  compiler_params=_cp(("arbitrary",)))(*arrs)


def _group(group):
    x, y, c = lax.axis_index("x"), lax.axis_index("y"), lax.axis_index("c")
    if group == "chips":
        return 2 * x + y, [((x, 1 - y, c), 2 * x + 1 - y), ((1 - x, y, c), 2 * (1 - x) + y),
                           ((1 - x, 1 - y, c), 2 * (1 - x) + 1 - y)]
    if group == "cores":
        return c, [((x, y, 1 - c), 1 - c)]
    if group == "x":
        return x, [((1 - x, y, c), 1 - x)]
    return y, [((x, 1 - y, c), 1 - y)]


def _exchange(arrs, group, mode, name):
    chips = group == "chips"
    k = len(arrs)
    npeer = 3 if chips else 1

    def body(*refs):
        ins, outs = refs[:k], refs[k:2 * k]
        send_sems, recv_sems = refs[2 * k:]
        me, peers = _group(group)
        remote = []
        for i in range(k):
            for p, (dev, slot) in enumerate(peers):
                src = ins[i].at[slot] if mode == "scatter" else ins[i]
                if not chips:
                    dst = outs[i]
                else:
                    dst = outs[i].at[p] if mode == "scatter" else outs[i].at[me]
                cp = pltpu.make_async_remote_copy(src_ref=src, dst_ref=dst, send_sem=send_sems.at[i, p],
                                                  recv_sem=recv_sems.at[i, p], device_id=dev,
                                                  device_id_type=pl.DeviceIdType.MESH)
                cp.start()
                remote.append(cp)
        for cp in remote:
            cp.wait_recv()
        for cp in remote:
            cp.wait_send()

    def oshape(a):
        piece = a.shape[1:] if mode == "scatter" else a.shape
        if chips:
            piece = ((3,) if mode == "scatter" else (4,)) + piece
        return jax.ShapeDtypeStruct(piece, a.dtype)

    any_spec = pl.BlockSpec(memory_space=pl.ANY)
    return _pcall(body, name=name, in_specs=[any_spec] * k, out_specs=[any_spec] * k, out_shape=[oshape(a) for a in arrs],
                  scratch_shapes=[pltpu.SemaphoreType.DMA((k, npeer)), pltpu.SemaphoreType.DMA((k, npeer))])(*arrs)


def _split_copies(ins, lands, send_sems, recv_sems, group, mode):
    chips = group == "chips"
    me, peers = _group(group)
    npeer = len(peers)
    out = []
    for i in range(len(ins)):
        for p, (dev, slot) in enumerate(peers):
            src = ins[i].at[slot] if mode == "scatter" else ins[i]
            if not chips:
                dst = lands[i]
            else:
                dst = lands[i].at[p] if mode == "scatter" else lands[i].at[me]
            out.append(pltpu.make_async_remote_copy(src_ref=src, dst_ref=dst, send_sem=send_sems.at[npeer * i + p],
                                                    recv_sem=recv_sems.at[npeer * i + p], device_id=dev,
                                                    device_id_type=pl.DeviceIdType.MESH))
    return out


def _exchange_start(arrs, group, mode, name):
    k = len(arrs)
    chips = group == "chips"
    nsem = (3 if chips else 1) * k
    hbm = pl.BlockSpec(memory_space=pltpu.HBM)
    sem = pl.BlockSpec(memory_space=pltpu.SEMAPHORE)

    def land_shape(a):
        piece = a.shape[1:] if mode == "scatter" else a.shape
        if chips:
            piece = ((3,) if mode == "scatter" else (4,)) + piece
        return piece

    def body(*refs):
        ins, lands = refs[:k], refs[k:2 * k]
        send_sems, recv_sems = refs[2 * k], refs[2 * k + 1]
        token = refs[-1]
        for cp in _split_copies(ins, lands, send_sems, recv_sems, group, mode):
            cp.start()
        token[...] = jnp.zeros_like(token)

    srcs = [pltpu.with_memory_space_constraint(a, pltpu.HBM) for a in arrs]
    lands = [pltpu.with_memory_space_constraint(lax.empty(land_shape(a), a.dtype), pltpu.HBM) for a in arrs]
    out_shape = ([pltpu.SemaphoreType.DMA((nsem,)), pltpu.SemaphoreType.DMA((nsem,))]
                 + [pltpu.HBM(a.shape, a.dtype) for a in arrs] + [pltpu.HBM(land_shape(a), a.dtype) for a in arrs]
                 + [jax.ShapeDtypeStruct((8, 128), F32)])
    outs = _pcall(body, name=name, in_specs=[hbm] * (2 * k),
                  out_specs=[sem, sem] + [hbm] * (2 * k) + [pl.BlockSpec(memory_space=pltpu.VMEM)], out_shape=out_shape,
                  input_output_aliases={i: 2 + i for i in range(2 * k)},
                  compiler_params=pltpu.CompilerParams(has_side_effects=pltpu.SideEffectType.DATAFLOW_SIDE_EFFECTING))(
                      *srcs, *lands)
    return (outs[0], outs[1], list(outs[2:2 + k]), list(outs[2 + k:2 + 2 * k])), outs[-1]


def _exchange_wait(state, after, group, mode, name):
    send_sems, recv_sems, srcs, lands = state
    k = len(srcs)
    hbm = pl.BlockSpec(memory_space=pltpu.HBM)
    sem = pl.BlockSpec(memory_space=pltpu.SEMAPHORE)

    def body(*refs):
        ins, lnd = refs[:k], refs[k:2 * k]
        send_sems, recv_sems = refs[2 * k], refs[2 * k + 1]
        for cp in _split_copies(ins, lnd, send_sems, recv_sems, group, mode):
            cp.wait_send()
            cp.wait_recv()

    outs = _pcall(body, name=name, in_specs=[hbm] * (2 * k) + [sem, sem, pl.BlockSpec(memory_space=pl.ANY)],
                  out_specs=[hbm] * (2 * k),
                  out_shape=[pltpu.HBM(a.shape, a.dtype) for a in srcs] + [pltpu.HBM(a.shape, a.dtype) for a in lands],
                  input_output_aliases={i: i for i in range(2 * k)},
                  compiler_params=pltpu.CompilerParams(has_side_effects=pltpu.SideEffectType.DATAFLOW_SIDE_EFFECTING))(
                      *srcs, *lands, send_sems, recv_sems, after)
    return list(outs[:k]), list(outs[k:])


def _sum_own_recv(own, recv, me, out_dtype, name):
    n, R, C = own.shape
    nr = 1 if recv.ndim == 2 else recv.shape[0]
    tr = _pick(R, (256, 128, 64, 32, 16, 8))

    def body(me_ref, own_ref, *refs):
        o_ref = refs[-1]
        acc = own_ref[...].astype(F32)
        for r in refs[:-1]:
            acc = acc + r[...].astype(F32)
        o_ref[...] = acc.astype(out_dtype)

    specs = [pl.BlockSpec((None, tr, C), lambda i, me_ref: (me_ref[0], i, 0))]
    args = [own]
    if recv.ndim == 2:
        specs.append(pl.BlockSpec((tr, C), lambda i, me_ref: (i, 0)))
        args.append(recv)
    else:
        for p in range(nr):
            specs.append(pl.BlockSpec((None, tr, C), functools.partial(lambda i, me_ref, p: (p, i, 0), p=p)))
            args.append(recv)
    gs = pltpu.PrefetchScalarGridSpec(num_scalar_prefetch=1, grid=(R // tr,), in_specs=specs,
                                      out_specs=pl.BlockSpec((tr, C), lambda i, me_ref: (i, 0)))
    return _pcall(body, name=name, grid_spec=gs, out_shape=jax.ShapeDtypeStruct((R, C), out_dtype),
                  compiler_params=_cp(("parallel",)))(me, *args)


def _silu(x):
    return x * jax.nn.sigmoid(x)


def _ln(r, g, b):
    mu = jnp.mean(r, -1, keepdims=True)
    xc = r - mu
    var = jnp.mean(xc * xc, -1, keepdims=True)
    return xc * lax.rsqrt(var + LN_EPS) * g + b


def _softplus(x):
    return jnp.maximum(x, 0.0) + jnp.log1p(jnp.exp(-jnp.abs(x)))


def _res_ln_fwd(x, y, g, b, res, name):
    def fn(tv, fv):
        r = ALPHA * tv[0] + res * tv[1]
        out = _ln(r, fv[0], fv[1])
        return [r, out, out], []
    return _rowwise(fn, [x, y], [g, b], [(D, F32), (D, F32), (D, _ACT)], [], name=name)


def _ln_bwd(r, g, b, dout, res, name):
    def fn(tv, fv):
        _, vjp = jax.vjp(_ln, tv[0], fv[0], fv[1])
        dr, dg, db = vjp(tv[1])
        return [ALPHA * dr, res * dr], [dg, db]
    return _rowwise(fn, [r, dout], [g, b], [(D, F32), (D, _ACT)], [(1, D), (1, D)], name=name)


def _swiglu_act(h, name):
    def fn(tv, fv):
        return [_silu(tv[0]) * tv[1]], []
    return _rowwise(fn, [(h, DFF, 0), (h, DFF, 1)], [], [(DFF, _ACT)], [], name=name)[0]


def _swiglu_act_bwd(h, ds, name):
    def fn(tv, fv):
        s, vjp = jax.vjp(lambda a, g: _silu(a) * g, tv[0], tv[1])
        da, dg = vjp(tv[2])
        return [[da, dg], s], []
    return _rowwise(fn, [(h, DFF, 0), (h, DFF, 1), ds], [], [(2 * DFF, _ACT), (DFF, _ACT)], [], name=name)


def _loss_fwd_bwd(y, tgt, name):
    def fn(tv, fv):
        e = tv[0] - tv[1]
        row = jnp.sum(e * e, axis=1, keepdims=True)
        tot = jnp.sum(row, axis=0, keepdims=True) * (0.5 / D)
        return [e * (1.0 / D)], [jnp.broadcast_to(tot, (1, 128))]
    return _rowwise(fn, [y, tgt], [], [(D, F32)], [(1, 128)], name=name)


def _shift_down(x, k, row):
    return jnp.where(row >= k, pltpu.roll(x, k, axis=0), 0.0)


def _shift_up(x, k, row):
    n = x.shape[0]
    return jnp.where(row < n - k, pltpu.roll(x, n - k, axis=0), 0.0)


def _pool_window_masks(j):
    lane = lax.broadcasted_iota(jnp.int32, (1, 128), 1) + j * 128
    grp = lane // POOL_GDIM
    return [grp == g for g in range(4)]


def _pool_mean(u, bwd, name, col0=0):
    T = u.shape[0]
    B = T // S

    def body(u_ref, o_ref):
        j = pl.program_id(1)
        x = u_ref[...]
        row = lax.broadcasted_iota(jnp.int32, (S, 1), 0)
        masks = _pool_window_masks(j)
        inv = [1.0 / jnp.minimum(row + 1, w).astype(F32) for w in POOL_WINDOWS]
        if not bwd:
            s2 = x + _shift_down(x, 1, row)
            s4 = s2 + _shift_down(s2, 2, row)
            s8 = s4 + _shift_down(s4, 4, row)
            s16 = s8 + _shift_down(s8, 8, row)
            mean = jnp.where(masks[0], s2 * inv[0], jnp.where(masks[1], s4 * inv[1],
                             jnp.where(masks[2], s8 * inv[2], s16 * inv[3])))
            o_ref[...] = (mean - x).astype(o_ref.dtype)
        else:
            g = [jnp.where(masks[i], x * inv[i], 0.0) for i in range(4)]
            t = g[3]
            t = t + _shift_up(t, 8, row) + g[2]
            t = t + _shift_up(t, 4, row) + g[1]
            t = t + _shift_up(t, 2, row) + g[0]
            t = t + _shift_up(t, 1, row)
            o_ref[...] = (t - x).astype(o_ref.dtype)

    spec = pl.BlockSpec((S, 128), lambda b, j: (b, j))
    return _pcall(body, name=name, grid=(B, POOLW // 128),
                  in_specs=[pl.BlockSpec((S, 128), lambda b, j: (b, j + col0))], out_specs=spec,
                  out_shape=jax.ShapeDtypeStruct((T, POOLW), _ACT), compiler_params=_cp(("parallel", "parallel")))(u)


def _conv_silu(xbc, w, b, name, col0=0):
    T, C = xbc.shape[0], w.shape[1]
    B = T // S

    def body(x_ref, w_ref, b_ref, o_ref):
        x = x_ref[...]
        row = lax.broadcasted_iota(jnp.int32, (S, 1), 0)
        c = b_ref[...] + w_ref[3:4, :] * x
        for s in range(1, 4):
            c = c + w_ref[3 - s:4 - s, :] * _shift_down(x, s, row)
        o_ref[...] = _silu(c)

    return _pcall(body, name=name, grid=(B, C // 128),
                  in_specs=[pl.BlockSpec((S, 128), lambda b, j: (b, j + col0)), pl.BlockSpec((4, 128), lambda b, j: (0, j)),
                            pl.BlockSpec((1, 128), lambda b, j: (0, j))],
                  out_specs=pl.BlockSpec((S, 128), lambda b, j: (b, j)),
                  out_shape=jax.ShapeDtypeStruct((T, C), F32), compiler_params=_cp(("parallel", "parallel")))(xbc, w, b)


def _conv_silu_bwd(xbc, w, b, dact, name, col0=0):
    T, C = xbc.shape[0], w.shape[1]
    B = T // S

    def body(x_ref, w_ref, b_ref, d_ref, dx_ref, dw_ref, db_ref):
        bi = pl.program_id(1)
        x = x_ref[...]
        row = lax.broadcasted_iota(jnp.int32, (S, 1), 0)
        xs = [x] + [_shift_down(x, s, row) for s in range(1, 4)]
        c = b_ref[...]
        for s in range(4):
            c = c + w_ref[3 - s:4 - s, :] * xs[s]
        _, vjp = jax.vjp(_silu, c)
        dc = vjp(d_ref[...])[0]
        dx = w_ref[3:4, :] * dc
        for s in range(1, 4):
            dx = dx + w_ref[3 - s:4 - s, :] * _shift_up(dc, s, row)
        dx_ref[...] = dx.astype(dx_ref.dtype)
        first = bi == 0
        for s in range(4):
            _acc_rows(dw_ref, 3 - s, jnp.sum(dc * xs[s], axis=0, keepdims=True), first)
        _acc_rows(db_ref, 0, jnp.sum(dc, axis=0, keepdims=True), first)

    blk = pl.BlockSpec((S, 128), lambda j, b: (b, j))
    return _pcall(body, name=name, grid=(C // 128, B),
                  in_specs=[pl.BlockSpec((S, 128), lambda j, b: (b, j + col0)), pl.BlockSpec((4, 128), lambda j, b: (0, j)),
                            pl.BlockSpec((1, 128), lambda j, b: (0, j)), blk],
                  out_specs=[blk, pl.BlockSpec((4, 128), lambda j, b: (0, j)), pl.BlockSpec((1, 128), lambda j, b: (0, j))],
                  out_shape=[jax.ShapeDtypeStruct((T, C), _ACT), jax.ShapeDtypeStruct((4, C), F32),
                             jax.ShapeDtypeStruct((1, C), F32)],
                  compiler_params=_cp(("parallel", "arbitrary")))(xbc, w, b, dact)


def _acc_rows(ref, r, val, first):
    @pl.when(first)
    def _():
        ref[r:r + 1, :] = val

    @pl.when(jnp.logical_not(first))
    def _():
        ref[r:r + 1, :] += val


def _tri_consts():
    i = lax.broadcasted_iota(jnp.int32, (CH, CH), 0)
    j = lax.broadcasted_iota(jnp.int32, (CH, CH), 1)
    return (i == j).astype(F32), (j <= i).astype(F32), (i <= j).astype(F32), i >= j


def _ssd_chunk(h, x, dt, Bm, Cm, a, dsk, consts):
    eye, tril, triu, lower = consts
    Bb = Bm.astype(_MXU)
    Cb = Cm.astype(_MXU)
    cb = lax.dot_general(Cb, Bb, (((1,), (1,)), ((), ())), preferred_element_type=F32)
    ys, hn = [], []
    for e in range(4):
        adt = dt[e] * a[e]
        adt_row = jnp.sum(adt * eye, axis=0, keepdims=True)
        cs_col = jnp.sum(adt_row * tril, axis=1, keepdims=True)
        cs_row = jnp.sum(adt * triu, axis=0, keepdims=True)
        cs_last = jnp.sum(adt, axis=0, keepdims=True)
        decay = jnp.exp(jnp.where(lower, cs_col - cs_row, -jnp.inf))
        xb = (x[e] * dt[e]).astype(_MXU)
        y_diag = jnp.dot((cb * decay).astype(_MXU), xb, preferred_element_type=F32)
        bdec = (Bm * jnp.exp(cs_last - cs_col)).astype(_MXU)
        st = lax.dot_general(bdec, xb, (((0,), (0,)), ((), ())), preferred_element_type=F32)
        hn.append(h[e] * jnp.exp(cs_last) + st)
        y_off = jnp.exp(cs_col) * jnp.dot(Cb, h[e].astype(_MXU), preferred_element_type=F32)
        ys.append(y_diag + y_off + dsk[e] * x[e])
    return ys, hn


def _ssd_specs(order):
    def im(f):
        return lambda p, q: f(*order(p, q))
    xs = pl.BlockSpec((S, 256), im(lambda b, g: (b, g)))
    dt = pl.BlockSpec((None, S, 4), im(lambda b, g: (g, b, 0)))
    bc = pl.BlockSpec((S, 128), im(lambda b, g: (b, g)))
    hd = pl.BlockSpec((None, 1, 4), im(lambda b, g: (g, 0, 0)))
    hs = pl.BlockSpec((None, None, S // CH, 4, 128, 64), im(lambda b, g: (b, g, 0, 0, 0, 0)))
    bw = pl.BlockSpec((S, 128), im(lambda b, g: (b, 8 + g)))
    cw = pl.BlockSpec((S, 128), im(lambda b, g: (b, 12 + g)))
    return xs, dt, bc, hd, hs, bw, cw


def _ssd_fwd(act, dtg, a, dsk, name):
    xs = bm = cm = act
    T = xs.shape[0]
    B = T // S
    nc = S // CH

    def body(x_ref, dt_ref, b_ref, c_ref, a_ref, k_ref, y_ref, hs_ref, h_ref):
        consts = _tri_consts()
        h_ref[...] = jnp.zeros_like(h_ref)
        al = [a_ref[:, e:e + 1] for e in range(4)]
        kl = [k_ref[:, e:e + 1] for e in range(4)]

        def step(c, carry):
            r0 = pl.multiple_of(c * CH, CH)
            rows = pl.ds(r0, CH)
            h = [h_ref[e] for e in range(4)]
            for e in range(4):
                hs_ref[c, e] = h[e]
            x = [x_ref[rows, 64 * e:64 * e + 64] for e in range(4)]
            dt = [dt_ref[rows, e:e + 1] for e in range(4)]
            ys, hn = _ssd_chunk(h, x, dt, b_ref[rows, :], c_ref[rows, :], al, kl, consts)
            for e in range(4):
                y_ref[rows, 64 * e:64 * e + 64] = ys[e]
                h_ref[e] = hn[e]
            return carry

        lax.fori_loop(0, nc, step, 0)

    sx, sdt, sbc, shd, shs, sbw, scw = _ssd_specs(lambda b, g: (b, g))
    return _pcall(body, name=name, grid=(B, 4), in_specs=[sx, sdt, sbw, scw, shd, shd], out_specs=[sx, shs],
                  out_shape=[jax.ShapeDtypeStruct((T, 1024), F32), jax.ShapeDtypeStruct((B, 4, nc, 4, 128, 64), F32)],
                  scratch_shapes=[pltpu.VMEM((4, 128, 64), F32)],
                  compiler_params=_cp(("parallel", "parallel")))(xs, dtg, bm, cm, a, dsk)


def _lane_place(vals, width):
    lane = lax.broadcasted_iota(jnp.int32, (1, width), 1)
    out = jnp.zeros((1, width), F32)
    for e, v in enumerate(vals):
        out = out + jnp.where(lane == e, v, 0.0)
    return out


def _ssd_bwd(act, dtg, a, dsk, hs, dy, name):
    xs = bm = cm = act
    T = xs.shape[0]
    B = T // S
    nc = S // CH

    def body(x_ref, dt_ref, b_ref, c_ref, a_ref, k_ref, hs_ref, dy_ref,
             dx_ref, ddt_ref, db_ref, dc_ref, dak_ref, dh_ref, sc_ref):
        bi = pl.program_id(1)
        consts = _tri_consts()
        dh_ref[...] = jnp.zeros_like(dh_ref)
        sc_ref[...] = jnp.zeros_like(sc_ref)
        al = [a_ref[:, e:e + 1] for e in range(4)]
        kl = [k_ref[:, e:e + 1] for e in range(4)]

        def step(i, carry):
            c = nc - 1 - i
            r0 = pl.multiple_of(c * CH, CH)
            rows = pl.ds(r0, CH)
            h = [hs_ref[c, e] for e in range(4)]
            x = [x_ref[rows, 64 * e:64 * e + 64] for e in range(4)]
            dt = [dt_ref[rows, e:e + 1] for e in range(4)]
            f = functools.partial(_ssd_chunk, consts=consts)
            _, vjp = jax.vjp(f, h, x, dt, b_ref[rows, :], c_ref[rows, :], al, kl)
            dys = [dy_ref[rows, 64 * e:64 * e + 64] for e in range(4)]
            dhn = [dh_ref[e] for e in range(4)]
            dh, dx, ddt, dB, dC, da, dk = vjp((dys, dhn))
            for e in range(4):
                dh_ref[e] = dh[e]
                dx_ref[rows, 64 * e:64 * e + 64] = dx[e]
                ddt_ref[rows, e:e + 1] = ddt[e]
            db_ref[rows, :] = dB
            dc_ref[rows, :] = dC
            sc_ref[0:1, :] += _lane_place(da, 128)
            sc_ref[1:2, :] += _lane_place(dk, 128)
            return carry

        lax.fori_loop(0, nc, step, 0)
        first = bi == 0

        @pl.when(first)
        def _():
            dak_ref[...] = sc_ref[...]

        @pl.when(jnp.logical_not(first))
        def _():
            dak_ref[...] += sc_ref[...]

    sx, sdt, sbc, shd, shs, sbw, scw = _ssd_specs(lambda g, b: (b, g))
    return _pcall(body, name=name, grid=(4, B), in_specs=[sx, sdt, sbw, scw, shd, shd, shs, sx],
                  out_specs=[sx, sdt, sbc, sbc, pl.BlockSpec((None, 8, 128), lambda g, b: (g, 0, 0))],
                  out_shape=[jax.ShapeDtypeStruct((T, 1024), F32), jax.ShapeDtypeStruct((4, T, 4), F32),
                             jax.ShapeDtypeStruct((T, 512), F32), jax.ShapeDtypeStruct((T, 512), F32),
                             jax.ShapeDtypeStruct((4, 8, 128), F32)],
                  scratch_shapes=[pltpu.VMEM((4, 128, 64), F32), pltpu.VMEM((8, 128), F32)],
                  compiler_params=_cp(("parallel", "arbitrary")))(xs, dtg, bm, cm, a, dsk, hs, dy)


def _gate_norm(y, z, nw):
    t = y * _silu(z)
    return t * lax.rsqrt(jnp.mean(t * t, axis=-1, keepdims=True) + SSD_EPS) * nw


def _ssd_gate_norm(y, z, nw, name, zcol=0):
    def fn(tv, fv):
        return [[_gate_norm(tv[g], tv[4 + g], fv[0][:, 256 * g:256 * g + 256]) for g in range(4)]], []
    tiled = [(y, 256, g) for g in range(4)] + [(z, 256, zcol + g) for g in range(4)]
    return _rowwise(fn, tiled, [nw], [(1024, _ACT)], [], name=name)[0]


def _ssd_gate_norm_bwd(y, z, nw, dout, name, zcol=0):
    def fn(tv, fv):
        dys, dzs, dns = [], [], []
        for g in range(4):
            _, vjp = jax.vjp(_gate_norm, tv[g], tv[4 + g], fv[0][:, 256 * g:256 * g + 256])
            a, b, c = vjp(tv[8 + g])
            dys.append(a)
            dzs.append(b)
            dns.append(c)
        return [dys, dzs], [dns]
    tiled = [(y, 256, g) for g in range(4)] + [(z, 256, zcol + g) for g in range(4)] + [(dout, 256, g) for g in range(4)]
    return _rowwise(fn, tiled, [nw], [(1024, F32), (1024, _ACT)], [(1, 1024)], name=name)


def _t5_bucket_np(dist):
    dist = np.maximum(dist, 0)
    max_exact = 16
    large = max_exact + (np.log(np.maximum(dist, 1) / max_exact) / np.log(2048 / max_exact) * (32 - max_exact)).astype(np.int32)
    large = np.minimum(large, 31)
    return np.where(dist < max_exact, dist, large).astype(np.int32)


def _bucket_maps():
    qi = np.arange(128)[:, None]
    kj = np.arange(256)[None, :]
    return np.stack([_t5_bucket_np((qi - kj + 128) * dil) for dil in ATTN_DILS]).astype(np.int32)


def _bias_build(rel_bias, maps, name):
    def body(tab_ref, map_ref, o_ref):
        hh = pl.program_id(0)
        m = map_ref[...]
        acc = jnp.zeros((128, 256), F32)
        for b in range(32):
            acc = jnp.where(m == b, tab_ref[b, hh], acc)
        o_ref[...] = acc

    return _pcall(body, name=name, grid=(12,),
                  in_specs=[pl.BlockSpec(memory_space=pltpu.SMEM), pl.BlockSpec((None, 128, 256), lambda h: (h // 4, 0, 0))],
                  out_specs=pl.BlockSpec((None, 128, 256), lambda h: (h, 0, 0)),
                  out_shape=jax.ShapeDtypeStruct((12, 128, 256), F32), compiler_params=_cp(("parallel",)))(rel_bias, maps)


def _bias_reduce(dbias, maps, name):
    nl = dbias.shape[0]

    def body(d_ref, map_ref, o_ref):
        m = map_ref[...]
        d = d_ref[0]
        for i in range(1, nl):
            d = d + d_ref[i]
        lane = lax.broadcasted_iota(jnp.int32, (1, 128), 1)
        out = jnp.zeros((1, 128), F32)
        for b in range(32):
            s = jnp.sum(jnp.sum(jnp.where(m == b, d, 0.0), axis=1, keepdims=True), axis=0, keepdims=True)
            out = out + jnp.where(lane == b, s, 0.0)
        o_ref[...] = out

    return _pcall(body, name=name, grid=(12,),
                  in_specs=[pl.BlockSpec((nl, None, 128, 256), lambda h: (0, h, 0, 0)),
                            pl.BlockSpec((None, 128, 256), lambda h: (h // 4, 0, 0))],
                  out_specs=pl.BlockSpec((None, 1, 128), lambda h: (h, 0, 0)),
                  out_shape=jax.ShapeDtypeStruct((12, 1, 128), F32), compiler_params=_cp(("parallel",)))(dbias, maps)


def _attn_block(q, kb, vb, bias, mask):
    s = lax.dot_general(q.astype(_MXU), kb.astype(_MXU), (((1,), (1,)), ((), ())), preferred_element_type=F32) * 0.125 + bias
    s = jnp.where(mask, s, -jnp.inf)
    m = lax.stop_gradient(jnp.max(s, axis=-1, keepdims=True))
    p = jnp.exp(s - m)
    den = jnp.sum(p, axis=-1, keepdims=True)
    out = jnp.dot((p / den).astype(_MXU), vb.astype(_MXU), preferred_element_type=F32)
    return out, m + jnp.log(den)


ATTN_QB = 512


def _attn_masks(dil):
    qi = lax.broadcasted_iota(jnp.int32, (ATTN_QB, ATTN_QB + 128), 0)
    kj = lax.broadcasted_iota(jnp.int32, (ATTN_QB, ATTN_QB + 128), 1)
    band = (kj >= qi) & (kj <= qi + 128)
    if dil == 16:
        q2 = lax.broadcasted_iota(jnp.int32, (ATTN_QB, ATTN_QB), 0)
        k2 = lax.broadcasted_iota(jnp.int32, (ATTN_QB, ATTN_QB), 1)
        return ((q2 // 128) == (k2 // 128)) & (k2 <= q2), None
    return band[:, 128:], band


def _attn_wide_bias(b, dil):
    if dil == 16:
        return jnp.tile(b[:, 128:], (4, 4)), None
    z = jnp.zeros((128, 128), F32)
    band = jnp.concatenate([jnp.concatenate([z] * i + [b] + [z] * (3 - i), axis=1) for i in range(4)], axis=0)
    return band[:, 128:], band


def _fold_dbias(dbs, dil, band_form):
    def blk(i, j):
        return dbs[128 * i:128 * i + 128, 128 * j:128 * j + 128]
    if band_form:
        return sum(blk(i, i) for i in range(4)), sum(blk(i, i + 1) for i in range(4))
    cur = sum(blk(i, i) for i in range(4))
    if dil == 16:
        return None, cur
    return sum(blk(i, i - 1) for i in range(1, 4)), cur


def _attn_chunks(dil):
    out = []
    for n in range(S // ATTN_QB):
        if dil == 1 and n > 0:
            out.append((n * ATTN_QB, n * ATTN_QB - 128, ATTN_QB + 128, True))
        else:
            out.append((n * ATTN_QB, n * ATTN_QB, ATTN_QB, False))
    return out


def _qkv_specs(gi, order):
    def spec(base):
        col = (base + 256 * gi) // 128
        return pl.BlockSpec((S, 128), lambda p, q: (order(p, q)[0], col + order(p, q)[1]))
    return [spec(O_Q), spec(O_K), spec(O_V)]


def _residue_rows(r, dil):
    return pl.ds(r, S // dil, stride=dil)


def _attn_fwd(hcat, bias_all, gi, name):
    dil = ATTN_DILS[gi]
    T = hcat.shape[0]
    B, L = T // S, S // dil

    def body(q_ref, k_ref, v_ref, b_ref, o_ref, l_ref, *scr):
        mask_first, mask_band = _attn_masks(dil)
        if dil > 1:
            qs, ks, vs, os_, ls = scr
            for r in range(dil):
                rows, dst = _residue_rows(r, dil), pl.ds(r * L, L)
                qs[dst, :] = q_ref[rows, :]
                ks[dst, :] = k_ref[rows, :]
                vs[dst, :] = v_ref[rows, :]
        else:
            qs, ks, vs, os_, ls = q_ref, k_ref, v_ref, o_ref, l_ref
        ls[...] = jnp.zeros_like(ls)
        for e in range(2):
            lanes = slice(64 * e, 64 * e + 64)
            bias_first, bias_band = _attn_wide_bias(b_ref[e], dil)
            for q0, k0, kn, band_form in _attn_chunks(dil):
                cur, keys = pl.ds(q0, ATTN_QB), pl.ds(k0, kn)
                o, l = _attn_block(qs[cur, lanes], ks[keys, lanes], vs[keys, lanes],
                                   bias_band if band_form else bias_first, mask_band if band_form else mask_first)
                os_[cur, lanes] = o
                ls[cur, e:e + 1] = l
        if dil > 1:
            for r in range(dil):
                rows, src = _residue_rows(r, dil), pl.ds(r * L, L)
                o_ref[rows, :] = os_[src, :]
                l_ref[rows, :] = ls[src, :]

    scratch = [pltpu.VMEM((S, 128), F32)] * 5 if dil > 1 else []
    return _pcall(body, name=name, grid=(B, 2),
                  in_specs=_qkv_specs(gi, lambda b, hp: (b, hp))
                  + [pl.BlockSpec((2, 128, 256), lambda b, hp: (2 * gi + hp, 0, 0))],
                  out_specs=[pl.BlockSpec((S, 128), lambda b, hp: (b, hp)),
                             pl.BlockSpec((None, S, 128), lambda b, hp: (hp, b, 0))],
                  out_shape=[jax.ShapeDtypeStruct((T, 256), F32), jax.ShapeDtypeStruct((2, T, 128), F32)],
                  scratch_shapes=scratch,
                  compiler_params=_cp(("parallel", "parallel")))(hcat, hcat, hcat, bias_all)


def _attn_bwd(hcat, bias_all, gi, do, dl, name):
    dil = ATTN_DILS[gi]
    T = hcat.shape[0]
    B, L = T // S, S // dil

    def body(q_ref, k_ref, v_ref, b_ref, do_ref, dl_ref, dq_ref, dk_ref, dv_ref, db_ref, acc_ref, *scr):
        bi = pl.program_id(1)
        mask_first, mask_band = _attn_masks(dil)
        if dil > 1:
            qs, ks, vs, dos, dls, dqs, dks, dvs = scr
            for r in range(dil):
                rows, dst = _residue_rows(r, dil), pl.ds(r * L, L)
                qs[dst, :] = q_ref[rows, :]
                ks[dst, :] = k_ref[rows, :]
                vs[dst, :] = v_ref[rows, :]
                dos[dst, :] = do_ref[rows, :]
                dls[dst, :] = dl_ref[rows, :]
        else:
            qs, ks, vs, dos, dls, dqs, dks, dvs = q_ref, k_ref, v_ref, do_ref, dl_ref, dq_ref, dk_ref, dv_ref
        dks[...] = jnp.zeros_like(dks)
        dvs[...] = jnp.zeros_like(dvs)
        for e in range(2):
            lanes = slice(64 * e, 64 * e + 64)
            bias_first, bias_band = _attn_wide_bias(b_ref[e], dil)
            acc_ref[...] = jnp.zeros_like(acc_ref)
            for q0, k0, kn, band_form in _attn_chunks(dil):
                cur, keys = pl.ds(q0, ATTN_QB), pl.ds(k0, kn)
                f = functools.partial(_attn_block, mask=mask_band if band_form else mask_first)
                _, vjp = jax.vjp(f, qs[cur, lanes], ks[keys, lanes], vs[keys, lanes],
                                 bias_band if band_form else bias_first)
                dq, dkb, dvb, dbs = vjp((dos[cur, lanes], dls[cur, e:e + 1]))
                dqs[cur, lanes] = dq
                dks[keys, lanes] += dkb
                dvs[keys, lanes] += dvb
                prev, here = _fold_dbias(dbs, dil, band_form)
                if prev is not None:
                    acc_ref[:, 0:128] += prev
                acc_ref[:, 128:256] += here

            @pl.when(bi == 0)
            def _(e=e):
                db_ref[e] = acc_ref[...]

            @pl.when(bi > 0)
            def _(e=e):
                db_ref[e] += acc_ref[...]

        if dil > 1:
            for r in range(dil):
                rows, src = _residue_rows(r, dil), pl.ds(r * L, L)
                dq_ref[rows, :] = dqs[src, :]
                dk_ref[rows, :] = dks[src, :]
                dv_ref[rows, :] = dvs[src, :]

    order = lambda hp, b: (b, hp)
    blk = pl.BlockSpec((S, 128), lambda hp, b: (b, hp))
    lblk = pl.BlockSpec((None, S, 128), lambda hp, b: (hp, b, 0))
    sds = jax.ShapeDtypeStruct((T, 256), F32)
    scratch = [pltpu.VMEM((128, 256), F32)] + ([pltpu.VMEM((S, 128), F32)] * 8 if dil > 1 else [])
    return _pcall(body, name=name, grid=(2, B),
                  in_specs=_qkv_specs(gi, order) + [pl.BlockSpec((2, 128, 256), lambda hp, b: (2 * gi + hp, 0, 0)), blk, lblk],
                  out_specs=[blk, blk, blk, pl.BlockSpec((2, 128, 256), lambda hp, b: (hp, 0, 0))],
                  out_shape=[sds, sds, sds, jax.ShapeDtypeStruct((4, 128, 256), F32)],
                  scratch_shapes=scratch,
                  compiler_params=_cp(("parallel", "arbitrary")))(hcat, hcat, hcat, bias_all, do, dl)


def _lse_merge(o0, o1, o2, l0, l1, l2):
    m = lax.stop_gradient(jnp.maximum(jnp.maximum(l0, l1), l2))
    e0, e1, e2 = jnp.exp(l0 - m), jnp.exp(l1 - m), jnp.exp(l2 - m)
    den = e0 + e1 + e2
    return (e0 / den) * o0 + (e1 / den) * o1 + (e2 / den) * o2


def _attn_merge(outs, lses, dy, name):
    T = outs[0].shape[0]
    bwd = dy is not None
    tm = 512

    def body(*refs):
        o_refs, l_refs = refs[:3], refs[3:6]
        if bwd:
            for r in refs[10:13]:
                r[...] = jnp.zeros_like(r)
        for e in range(2):
            lanes = slice(64 * e, 64 * e + 64)
            vals = [r[:, lanes] for r in o_refs] + [r[:, e:e + 1] for r in l_refs]
            if not bwd:
                refs[6][:, lanes] = _lse_merge(*vals).astype(refs[6].dtype)
            else:
                _, vjp = jax.vjp(_lse_merge, *vals)
                g = vjp(refs[6][:, lanes])
                for r, v in zip(refs[7:10], g[:3]):
                    r[:, lanes] = v
                for r, v in zip(refs[10:13], g[3:]):
                    r[:, e:e + 1] = v

    blk = pl.BlockSpec((tm, 128), lambda i, hp: (i, hp))
    lblk = pl.BlockSpec((None, tm, 128), lambda i, hp: (hp, i, 0))
    lsd = jax.ShapeDtypeStruct((2, T, 128), F32)
    if not bwd:
        return _pcall(body, name=name, grid=(T // tm, 2), in_specs=[blk] * 3 + [lblk] * 3, out_specs=blk,
                      out_shape=jax.ShapeDtypeStruct((T, 256), F32),
                      compiler_params=_cp(("parallel", "parallel")))(*outs, *lses)
    return _pcall(body, name=name, grid=(T // tm, 2), in_specs=[blk] * 3 + [lblk] * 3 + [blk],
                  out_specs=[blk] * 3 + [lblk] * 3, out_shape=[jax.ShapeDtypeStruct((T, 256), F32)] * 3 + [lsd] * 3,
                  compiler_params=_cp(("parallel", "parallel")))(*outs, *lses, dy)


def _gmerge(g0, g1, g2, gb, ya, yb, yc):
    return (jax.nn.sigmoid(g0 + gb[:, 0:D]) * ya + jax.nn.sigmoid(g1 + gb[:, D:2 * D]) * yb
            + jax.nn.sigmoid(g2 + gb[:, 2 * D:3 * D]) * yc)


def _gated_merge(gates, gb, ya, yb, yc, name, gcol=0):
    def fn(tv, fv):
        return [_gmerge(tv[0], tv[1], tv[2], fv[0], tv[3], tv[4], tv[5])], []
    return _rowwise(fn, [(gates, D, gcol), (gates, D, gcol + 1), (gates, D, gcol + 2), ya, yb, yc], [gb], [(D, _ACT)], [],
                    name=name)[0]


def _gated_merge_bwd(gates, gb, ya, yb, yc, dm, name, gcol=0):
    def fn(tv, fv):
        _, vjp = jax.vjp(_gmerge, tv[0], tv[1], tv[2], fv[0], tv[3], tv[4], tv[5])
        d0, d1, d2, dgb, da, db, dc = vjp(tv[6])
        return [[d0, d1, d2], da, db, dc], [dgb]
    return _rowwise(fn, [(gates, D, gcol), (gates, D, gcol + 1), (gates, D, gcol + 2), ya, yb, yc, dm], [gb],
                    [(3 * D, _ACT), (D, _ACT), (D, _ACT), (D, _ACT)], [(1, 3 * D)], name=name)


def _pool_affine(t1, pb, ps, dout, name):
    if dout is None:
        def fn(tv, fv):
            return [(tv[0] + fv[0]) * fv[1]], []
        return _rowwise(fn, [t1], [pb, ps], [(POOLW, _ACT)], [], name=name)[0]

    def fnb(tv, fv):
        t2, vjp = jax.vjp(lambda t, b, s: (t + b) * s, tv[0], fv[0], fv[1])
        dt, db, dsc = vjp(tv[1])
        return [dt, t2], [db, dsc]
    return _rowwise(fnb, [t1, dout], [pb, ps], [(POOLW, _ACT), (POOLW, _ACT)], [(1, POOLW), (1, POOLW)], name=name)


def _dt_softplus(dt_raw, dt_bias, ddt, name):
    f = lambda r, b: _softplus(r + b)
    if ddt is None:
        def fn(tv, fv):
            return [f(tv[0], fv[0])], []
        return _rowwise(fn, [dt_raw], [dt_bias], [(16, F32)], [], name=name, tm=1024)[0]

    def fnb(tv, fv):
        _, vjp = jax.vjp(f, tv[0], fv[0])
        dr, db = vjp(tv[1])
        return [dr], [db]
    return _rowwise(fnb, [dt_raw, ddt], [dt_bias], [(16, F32)], [(1, 16)], name=name, tm=1024)


def _adamw_math(wv, gv, mv, vv):
    c1 = 1.0 / (1.0 - ADAM_B1 ** ADAM_STEP)
    c2 = 1.0 / (1.0 - ADAM_B2 ** ADAM_STEP)
    mn = ADAM_B1 * mv + (1.0 - ADAM_B1) * gv
    vn = ADAM_B2 * vv + (1.0 - ADAM_B2) * (gv * gv)
    delta = -ADAM_LR * ((mn * c1) / (jnp.sqrt(vn * c2) + ADAM_EPS) + ADAM_WD * wv)
    return delta, mn, vn


def _adamw(w, g, m, v, name):
    R, C = w.shape
    tm = _pick(R, (256, 128, 64, 32, 16, 8))
    return _rowwise(lambda tv, fv: (list(_adamw_math(*tv)), []), [w, g, m, v], [], [(C, F32)] * 3, [], name=name, tm=tm)


def _adamw_layer(i, w, g, m, v, accs, name):
    R, C = w.shape
    r = R // NL
    tm = _pick(r, (256, 128, 64, 32, 16, 8))
    nt = r // tm
    if accs is None:
        accs = [lax.empty((R, C), F32) for _ in range(4)]

    def body(w_ref, g_ref, m_ref, v_ref, a0, a1, a2, a3, go_ref, do_ref, mo_ref, vo_ref):
        gv = g_ref[...]
        delta, mn, vn = _adamw_math(w_ref[...], gv, m_ref[...], v_ref[...])
        go_ref[...] = gv
        do_ref[...] = delta
        mo_ref[...] = mn
        vo_ref[...] = vn

    slab = pl.BlockSpec((tm, C), lambda t: (i * nt + t, 0))
    anyspec = pl.BlockSpec(memory_space=pl.ANY)
    return _pcall(body, name=name, grid=(nt,),
                  in_specs=[slab, pl.BlockSpec((tm, C), lambda t: (t, 0)), slab, slab] + [anyspec] * 4,
                  out_specs=[slab] * 4, out_shape=[jax.ShapeDtypeStruct((R, C), F32)] * 4,
                  input_output_aliases={4 + k: k for k in range(4)},
                  compiler_params=_cp(("parallel",)))(w, g, m, v, *accs)


def _ffn_fwd(x, xm, w13, w2, g, b, tag, dep=None):
    h = _mm(xm, w13, dep=dep, name=f"{tag}_h")
    s = _swiglu_act(h, name=f"{tag}_act")
    y = _mm(s, w2, name=f"{tag}_y")
    r, out, outm = _res_ln_fwd(x, y, g, b, 0.5, name=f"{tag}_ln")
    return out, outm, dict(x=xm, h=h, r=r)


def _ffn_bwd(dout, sv, w13, w2, g, b, tag, dep=None):
    dskip, dy, dg, db = _ln_bwd(sv['r'], g, b, dout, 0.5, name=f"{tag}_lnb")
    ds = _mm(dy, w2, tb=True, dep=dep, name=f"{tag}_ds")
    dh, s = _swiglu_act_bwd(sv['h'], ds, name=f"{tag}_actb")
    dw2 = _mm(s, dy, ta=True, name=f"{tag}_dw2")
    dw13 = _mm(sv['x'], dh, ta=True, name=f"{tag}_dw13")
    dx = _mm(dh, w13, tb=True, add=dskip, name=f"{tag}_dx")
    return dx, dict(w13=dw13, w2=dw2, g=dg, b=db)


def _mixer_fwd(x1, x1m, W, bias_all, tag, dep=None):
    T = x1.shape[0]
    hcat = _mm(x1m, W['w_in_r'], dep=dep, name=f"{tag}_hcat")
    dt_raw = hcat[:, O_DT:O_DT + 16]
    pooled = _pool_mean(hcat, False, name=f"{tag}_pool", col0=O_U // 128)
    t1 = _mm(pooled, W['pool_wbd'], name=f"{tag}_pt1")
    t2 = _pool_affine(t1, W['pool_b'], W['pool_scale'], None, name=f"{tag}_paff")
    ya = _mm(t2, W['p_pool'], name=f"{tag}_ya")
    act = _conv_silu(hcat, W['conv_w'], W['conv_b'], name=f"{tag}_conv", col0=O_XBC // 128)
    dt = _dt_softplus(dt_raw, W['dt_bias'], None, name=f"{tag}_dt")
    dtg = dt.reshape(T, 4, 4).transpose(1, 0, 2)
    yscan, hs = _ssd_fwd(act, dtg, W['a_neg'], W['d_skip'], name=f"{tag}_ssd")
    ybn = _ssd_gate_norm(yscan, hcat, W['ssd_norm'], name=f"{tag}_gn", zcol=O_Z // 256)
    yb = _mm(ybn, W['p_ssd'], name=f"{tag}_yb")
    outs, lses = [], []
    for gi in range(len(ATTN_DILS)):
        o, l = _attn_fwd(hcat, bias_all, gi, name=f"{tag}_attn{gi}")
        outs.append(o)
        lses.append(l)
    ycp = _attn_merge(outs, lses, None, name=f"{tag}_amerge")
    yc = _mm(ycp, W['p_attn'], name=f"{tag}_yc")
    merged = _gated_merge(hcat, W['gate_b'], ya, yb, yc, name=f"{tag}_gm", gcol=O_G // D)
    mix = _mm(merged, W['w_out'], name=f"{tag}_mix")
    r, out, outm = _res_ln_fwd(x1, mix, W['ln2_g'], W['ln2_b'], 1.0, name=f"{tag}_ln")
    sv = dict(x1=x1m, dt_raw=dt_raw, pooled=pooled, t1=t1, act=act, dtg=dtg,
              hs=hs, yscan=yscan, ybn=ybn, hcat=hcat, outs=outs, lses=lses, ycp=ycp, ya=ya, yb=yb, yc=yc,
              merged=merged, r=r)
    return out, outm, sv


def _mixer_bwd(dout, sv, W, bias_all, tag, dep=None):
    T = dout.shape[0]
    gr = {}
    dx1a, dr, gr['ln2_g'], gr['ln2_b'] = _ln_bwd(sv['r'], W['ln2_g'], W['ln2_b'], dout, 1.0, name=f"{tag}_lnb")
    dmerged = _mm(dr, W['w_out'], tb=True, dep=dep, name=f"{tag}_dmerged")
    gr['w_out'] = _mm(sv['merged'], dr, ta=True, name=f"{tag}_dwout")
    dgates, dya, dyb, dyc, gr['gate_b'] = _gated_merge_bwd(sv['hcat'], W['gate_b'], sv['ya'], sv['yb'], sv['yc'],
                                                           dmerged, name=f"{tag}_gmb", gcol=O_G // D)
    dycp = _mm(dyc, W['p_attn'], tb=True, name=f"{tag}_dycp")
    gr['p_attn'] = _mm(sv['ycp'], dyc, ta=True, name=f"{tag}_dpattn")
    dml = _attn_merge(sv['outs'], sv['lses'], dycp, name=f"{tag}_amergeb")
    dq, dk, dv, dbias = [], [], [], []
    for gi in range(len(ATTN_DILS)):
        a, b, c, d = _attn_bwd(sv['hcat'], bias_all, gi, dml[gi], dml[3 + gi], name=f"{tag}_attnb{gi}")
        dq.append(a)
        dk.append(b)
        dv.append(c)
        dbias.append(d)
    dbias = jnp.concatenate(dbias, axis=0)
    dybn = _mm(dyb, W['p_ssd'], tb=True, name=f"{tag}_dybn")
    gr['p_ssd'] = _mm(sv['ybn'], dyb, ta=True, name=f"{tag}_dpssd")
    dyscan, dz, gr['ssd_norm'] = _ssd_gate_norm_bwd(sv['yscan'], sv['hcat'], W['ssd_norm'], dybn, name=f"{tag}_gnb",
                                                    zcol=O_Z // 256)
    dxs, ddtg, dbm, dcm, dak = _ssd_bwd(sv['act'], sv['dtg'], W['a_neg'], W['d_skip'], sv['hs'], dyscan,
                                        name=f"{tag}_ssdb")
    gr['a_neg'], gr['d_skip'] = dak[:, 0, 0:4], dak[:, 1, 0:4]
    ddt = ddtg.transpose(1, 0, 2).reshape(T, 16)
    ddt_raw, gr['dt_bias'] = _dt_softplus(sv['dt_raw'], W['dt_bias'], ddt, name=f"{tag}_dtb")
    dact = jnp.concatenate([dxs, dbm, dcm], axis=1)
    dxbc, gr['conv_w'], gr['conv_b'] = _conv_silu_bwd(sv['hcat'], W['conv_w'], W['conv_b'], dact, name=f"{tag}_convb",
                                                      col0=O_XBC // 128)
    dt2 = _mm(dya, W['p_pool'], tb=True, name=f"{tag}_dt2")
    dt1, t2, gr['pool_b'], gr['pool_scale'] = _pool_affine(sv['t1'], W['pool_b'], W['pool_scale'], dt2, name=f"{tag}_paffb")
    gr['p_pool'] = _mm(t2, dya, ta=True, name=f"{tag}_dppool")
    dpooled = _mm(dt1, W['pool_wbd'], tb=True, name=f"{tag}_dpooled")
    gr['pool_wbd'] = _mm(sv['pooled'], dt1, ta=True, name=f"{tag}_dpoolw")
    du = _pool_mean(dpooled, True, name=f"{tag}_poolb")
    dhcat = jnp.concatenate([t.astype(_ACT) for t in [du, dz, dxbc] + dq + dk + dv + [dgates, ddt_raw]]
                            + [jnp.zeros((T, HC - O_DT - 16), _ACT)], axis=1)
    dx1 = _mm(dhcat, W['w_in_r'], tb=True, add=dx1a, name=f"{tag}_dx1")
    gr['w_in_r'] = _mm(sv['x1'], dhcat, ta=True, name=f"{tag}_dwin")
    return dx1, gr, dbias


def _prep_layer_weights(i, inp, G):
    W = {}
    for n in BIG:
        if n not in G:
            continue
        g = G[n]
        if n == 'w_in':
            W['w_in_r'] = jnp.concatenate(_nat_pieces(g, 0, 3840) + _nat_pieces(g, 3856, 9232) + _nat_pieces(g, 3840, 3856)
                                          + [jnp.zeros((D, HC - 9232), g.dtype)], axis=1)
        elif n in COL_SHARDED:
            W[n] = jnp.concatenate([g[j] for j in range(4)], axis=1)
        else:
            W[n] = g.reshape(4 * g.shape[1], g.shape[2])
    pw = inp['pool_w'][i].astype(_MXU)
    wbd = jnp.zeros((POOLW, POOLW), _MXU)
    for g in range(4):
        wbd = lax.dynamic_update_slice(wbd, pw[g], (g * POOL_GDIM, g * POOL_GDIM))
    W['pool_wbd'] = wbd
    W['pool_b'] = inp['pool_b'][i].reshape(1, POOLW)
    W['pool_scale'] = inp['pool_scale'][i].reshape(1, POOLW)
    if 'conv_w' in G:
        W['conv_w'] = jnp.concatenate([G['conv_w'][j] for j in range(4)], axis=1)
        W['gate_b'] = jnp.concatenate([G['gate_b'][j][b:b + 1] for b in range(3) for j in range(4)], axis=1)
    W['conv_b'] = inp['conv_b'][i].reshape(1, 2048)
    W['dt_bias'] = inp['dt_bias'][i].reshape(1, 16)
    W['a_neg'] = (-jnp.exp(inp['a_log'][i])).reshape(4, 1, 4)
    W['d_skip'] = inp['d_skip'][i].reshape(4, 1, 4)
    W['ssd_norm'] = inp['ssd_norm'][i].reshape(1, D)
    for n in ('ln1_g', 'ln1_b', 'ln2_g', 'ln2_b', 'ln3_g', 'ln3_b'):
        W[n] = inp[n][i].reshape(1, D)
    return W


GATHER_FIRST = ['ffn1_w13', 'ffn1_w2']
GATHER_REST = [n for n in BIG if n not in GATHER_FIRST] + ['gate_b', 'conv_w']


def _gather_start(inp, i, names):
    core = lax.axis_index("c")
    arrs = []
    for n in names:
        s = inp[n][i]
        if n in BIG:
            s = lax.dynamic_slice_in_dim(s, core * (s.shape[0] // 2), s.shape[0] // 2, axis=0).astype(BF16)
        arrs.append(s)
    state, token = _exchange_start(arrs, "chips", "gather", name="gather_start")
    return (names, state), token


def _gather_mid(handle, after):
    names, state = handle
    me = 2 * lax.axis_index("x") + lax.axis_index("y")
    own, outs = _exchange_wait(state, after, "chips", "gather", name="gather_wait")
    outs = [lax.dynamic_update_slice(o, a[None], (me, 0, 0)) for o, a in zip(outs, own)]
    big = [o for n, o in zip(names, outs) if n in BIG]
    state, token = _exchange_start(big, "cores", "gather", name="share_start")
    return (names, outs, state), token


def _gather_finish(handle, after):
    names, outs, state = handle
    core = lax.axis_index("c")
    mine, theirs = _exchange_wait(state, after, "cores", "gather", name="share_wait")
    G = {n: o for n, o in zip(names, outs) if n not in BIG}
    for n, a, b in zip([n for n in names if n in BIG], mine, theirs):
        G[n] = jnp.concatenate([jnp.where(core == 0, a, b), jnp.where(core == 0, b, a)], axis=1)
    return G


W_IN_SHARD = 2308


def _nat_pieces(g, lo, hi):
    out = []
    for j in range(4):
        s, e = max(lo, W_IN_SHARD * j), min(hi, W_IN_SHARD * (j + 1))
        if s < e:
            out.append(g[j][:, s - W_IN_SHARD * j:e - W_IN_SHARD * j])
    return out


def _reord_ranges(lo, hi):
    out = []
    for a, b, off in ((0, 3840, 0), (3840, 3856, O_DT - 3840), (3856, 9232, -16)):
        s, e = max(lo, a), min(hi, b)
        if s < e:
            out.append((s + off, e + off))
    return out


def _halves_of(n, g):
    if n == 'w_in':
        shards = [jnp.concatenate([g[:, a:b] for a, b in _reord_ranges(W_IN_SHARD * j, W_IN_SHARD * (j + 1))], axis=1)
                  for j in range(4)]
    elif n in COL_SHARDED:
        c = g.shape[1] // 4
        shards = [g[:, j * c:(j + 1) * c] for j in range(4)]
    else:
        r = g.shape[0] // 4
        shards = [g[j * r:(j + 1) * r] for j in range(4)]
    r2 = shards[0].shape[0] // 2
    return jnp.stack([jnp.concatenate([s[h * r2:(h + 1) * r2] for s in shards], axis=0) for h in range(2)])


def _reduce_a(grads):
    names = list(grads)
    halves = [_halves_of(n, grads[n]) for n in names]
    state, token = _exchange_start(halves, "cores", "scatter", name="rsc_start")
    return (names, state), token


def _reduce_b(handle, after):
    names, state = handle
    core = lax.axis_index("c").reshape(1)
    halves, got = _exchange_wait(state, after, "cores", "scatter", name="rsc_wait")
    chip = [_sum_own_recv(h, t, core, BF16, name="rs_sum2") for h, t in zip(halves, got)]
    chip = [t.reshape(4, t.shape[0] // 4, t.shape[1]) for t in chip]
    state, token = _exchange_start(chip, "chips", "scatter", name="rs_start")
    return (names, state), token


def _reduce_c(handle, after):
    names, state = handle
    chip_id = (2 * lax.axis_index("x") + lax.axis_index("y")).reshape(1)
    chip, got = _exchange_wait(state, after, "chips", "scatter", name="rs_wait")
    red = [_sum_own_recv(h, t, chip_id, F32, name="rs_sum4") for h, t in zip(chip, got)]
    other = _exchange(red, "cores", "gather", name="rs_share")
    out = {}
    for n, mine, theirs in zip(names, red, other):
        out[n] = jnp.where(lax.axis_index("c") == 0, jnp.concatenate([mine, theirs]), jnp.concatenate([theirs, mine]))
    return out


class _Comm:
    def __init__(self, inp):
        self.inp = inp

    def gather_start(self, i, names):
        return _gather_start(self.inp, i, names)

    gather_mid = staticmethod(_gather_mid)
    gather_finish = staticmethod(_gather_finish)

    def reduce_a(self, i, grads):
        return _reduce_a({n: grads[n] for n in BIG})

    reduce_b = staticmethod(_reduce_b)
    reduce_c = staticmethod(_reduce_c)


def _allreduce_small(vec):
    for group in ("cores", "x", "y"):
        recv = _exchange([vec], group, "gather", name=f"ar_{group}")[0]
        vec = _rowwise(lambda tv, fv: ([tv[0] + tv[1]], []), [vec, recv], [], [(128, F32)], [], name=f"ar_add_{group}")[0]
    return vec


def _pack(arrs):
    flat = jnp.concatenate([a.reshape(-1) for a in arrs])
    n = flat.shape[0]
    pad = (-n) % (256 * 128)
    flat = jnp.concatenate([flat, jnp.zeros((pad,), F32)])
    return flat.reshape(-1, 128)


def _unpack(p, shapes):
    flat = p.reshape(-1)
    out, off = [], 0
    for s in shapes:
        sz = int(np.prod(s))
        out.append(flat[off:off + sz].reshape(s))
        off += sz
    return out


def _forward_backward(inp, comm, bias_all):
    x = xm = inp['x'].reshape(-1, D)
    tgt = inp['loss_target'].reshape(-1, D)
    saved, Ws = [], []
    h_first, _ = comm.gather_start(0, GATHER_FIRST)
    h_rest, dep = comm.gather_start(0, GATHER_REST)
    h_first, tok = comm.gather_mid(h_first, x)
    G = comm.gather_finish(h_first, tok)
    for i in range(NL):
        W = _prep_layer_weights(i, inp, G)
        start_next = lambda: (comm.gather_start(i + 1, BIG + ['gate_b', 'conv_w']) if i + 1 < NL else (None, None))
        if i > 0:
            h_next, dep = start_next()
        x1, x1m, s1 = _ffn_fwd(x, xm, W['ffn1_w13'], W['ffn1_w2'], W['ln1_g'], W['ln1_b'], "f1", dep)
        if i == 0:
            h_rest, tok = comm.gather_mid(h_rest, x1m)
            W.update(_prep_layer_weights(i, inp, comm.gather_finish(h_rest, tok)))
            h_next, dep = start_next()
        x2, x2m, s2 = _mixer_fwd(x1, x1m, W, bias_all, "mx", dep if i == 0 else None)
        dep = None
        if h_next is not None:
            h_next, dep = comm.gather_mid(h_next, x2m)
        x, xm, s3 = _ffn_fwd(x2, x2m, W['ffn2_w13'], W['ffn2_w2'], W['ln3_g'], W['ln3_b'], "f2", dep)
        if h_next is not None:
            G = comm.gather_finish(h_next, xm)
        saved.append((s1, s2, s3))
        Ws.append(W)
    dy, lpart = _loss_fwd_bwd(x, tgt, name="loss")
    fins, reduced, dbiases = [None] * NL, [None] * NL, [None] * NL
    pend_a, pend_b, dep = None, None, None
    for i in reversed(range(NL)):
        W = Ws[i]
        s1, s2, s3 = saved[i]
        g = {}
        dx2, f = _ffn_bwd(dy, s3, W['ffn2_w13'], W['ffn2_w2'], W['ln3_g'], W['ln3_b'], "f2", dep)
        g['ffn2_w13'], g['ffn2_w2'], g['ln3_g'], g['ln3_b'] = f['w13'], f['w2'], f['g'], f['b']
        dep = None
        if pend_a is not None:
            handle, dep = comm.reduce_b(pend_a[1], dx2)
            pend_b = (pend_a[0], handle)
        dx1, gm, dbiases[i] = _mixer_bwd(dx2, s2, W, bias_all, "mx", dep)
        g.update(gm)
        dy, f = _ffn_bwd(dx1, s1, W['ffn1_w13'], W['ffn1_w2'], W['ln1_g'], W['ln1_b'], "f1")
        g['ffn1_w13'], g['ffn1_w2'], g['ln1_g'], g['ln1_b'] = f['w13'], f['w2'], f['g'], f['b']
        fins[i] = _finish_layer_grads(i, g, inp)
        if pend_b is not None:
            reduced[pend_b[0]] = comm.reduce_c(pend_b[1], dy)
            pend_b = None
        handle, dep = comm.reduce_a(i, fins[i])
        pend_a = (i, handle)
    return lpart, dy, fins, reduced, pend_a, dbiases


def _finish_layer_grads(i, g, inp):
    out = {n: g[n] for n in BIG if n != 'w_in'}
    out['w_in'] = g['w_in_r']
    out['pool_w'] = jnp.stack([g['pool_wbd'][k * POOL_GDIM:(k + 1) * POOL_GDIM, k * POOL_GDIM:(k + 1) * POOL_GDIM] for k in range(4)])
    out['pool_b'] = g['pool_b'].reshape(4, POOL_GDIM)
    out['pool_scale'] = g['pool_scale'].reshape(POOLW)
    out['conv_w'] = g['conv_w']
    out['conv_b'] = g['conv_b'].reshape(2048)
    out['dt_bias'] = g['dt_bias'].reshape(16)
    out['a_log'] = (g['a_neg'].reshape(16)) * (-jnp.exp(inp['a_log'][i]))
    out['d_skip'] = g['d_skip'].reshape(16)
    out['ssd_norm'] = g['ssd_norm'].reshape(D)
    out['gate_b'] = g['gate_b'].reshape(3, D)
    for n in ('ln1_g', 'ln1_b', 'ln2_g', 'ln2_b', 'ln3_g', 'ln3_b'):
        out[n] = g[n].reshape(D)
    return out


def kernel(x, ffn1_w13, ffn1_w2, ln1_g, ln1_b, w_in, gate_b, pool_w, pool_b, pool_scale, conv_w, conv_b,
           dt_bias, a_log, d_skip, ssd_norm, rel_bias, p_pool, p_ssd, p_attn, w_out, ln2_g, ln2_b, ffn2_w13,
           ffn2_w2, ln3_g, ln3_b, loss_target, m_ffn1_w13, m_ffn1_w2, m_ln1_g, m_ln1_b, m_w_in, m_gate_b,
           m_pool_w, m_pool_b, m_pool_scale, m_conv_w, m_conv_b, m_dt_bias, m_a_log, m_d_skip, m_ssd_norm,
           m_rel_bias, m_p_pool, m_p_ssd, m_p_attn, m_w_out, m_ln2_g, m_ln2_b, m_ffn2_w13, m_ffn2_w2, m_ln3_g,
           m_ln3_b, v_ffn1_w13, v_ffn1_w2, v_ln1_g, v_ln1_b, v_w_in, v_gate_b, v_pool_w, v_pool_b,
           v_pool_scale, v_conv_w, v_conv_b, v_dt_bias, v_a_log, v_d_skip, v_ssd_norm, v_rel_bias, v_p_pool,
           v_p_ssd, v_p_attn, v_w_out, v_ln2_g, v_ln2_b, v_ffn2_w13, v_ffn2_w2, v_ln3_g, v_ln3_b):
    inp = dict(locals())
    maps = jnp.asarray(_bucket_maps())
    bias_all = _bias_build(rel_bias, maps, name="bias_build")
    comm = _Comm(inp)
    lpart, gx, fins, red, pending, dbiases = _forward_backward(inp, comm, bias_all)
    loss = lax.psum(lpart[0, 0], ("x", "y", "c"))

    small_l = [n for n in SMALL if n != 'rel_bias']
    drel = _bias_reduce(jnp.stack(dbiases), maps, name="bias_reduce")[:, 0, :32].T
    handle_b, _ = comm.reduce_b(pending[1], drel)
    small_arrs = [jnp.stack([fins[i][n] for i in range(NL)]) for n in small_l] + [drel]
    packed = _allreduce_small(_pack(small_arrs))
    gsmall = dict(zip(small_l + ['rel_bias'], _unpack(packed, [a.shape for a in small_arrs])))
    shard = 2 * lax.axis_index("x") + lax.axis_index("y")
    gsmall['gate_b'] = lax.dynamic_slice_in_dim(gsmall['gate_b'], shard * 256, 256, axis=2)
    gsmall['conv_w'] = lax.dynamic_slice_in_dim(gsmall['conv_w'], shard * 512, 512, axis=2)

    two_d = lambda a: a.reshape(a.shape[0] * a.shape[1], a.shape[2])
    accs = {n: None for n in BIG}

    def adamw_layer(i):
        for n in BIG:
            accs[n] = _adamw_layer(i, two_d(inp[n]), red[i][n], two_d(inp['m_' + n]), two_d(inp['v_' + n]), accs[n],
                                   name="adamw_big")

    done = [i for i in range(NL) if i != pending[0]]
    for i in done:
        adamw_layer(i)
    red[pending[0]] = comm.reduce_c(handle_b, accs[BIG[-1]][1] if done else packed)
    adamw_layer(pending[0])
    gout, delta, new_m, new_v = {}, {}, {}, {}
    for n in BIG:
        gout[n], delta[n], new_m[n], new_v[n] = [a.reshape(inp[n].shape) for a in accs[n]]
    gout.update(gsmall)
    shapes = [inp[n].shape for n in SMALL]
    d, m, v = _adamw(_pack([inp[n] for n in SMALL]), _pack([gout[n] for n in SMALL]),
                     _pack([inp['m_' + n] for n in SMALL]), _pack([inp['v_' + n] for n in SMALL]), name="adamw_small")
    for n, dd, mm, vv in zip(SMALL, _unpack(d, shapes), _unpack(m, shapes), _unpack(v, shapes)):
        delta[n], new_m[n], new_v[n] = dd, mm, vv

    return (loss, gx.reshape(x.shape), *[gout[n] for n in WEIGHTS], *[delta[n] for n in WEIGHTS],
            *[new_m[n] for n in WEIGHTS], *[new_v[n] for n in WEIGHTS])
```

```python
import functools

import numpy as np
import jax
import jax.numpy as jnp
from jax import lax
from jax.experimental import pallas as pl
from jax.experimental.pallas import tpu as pltpu

F32 = jnp.float32
BF16 = jnp.bfloat16
_MXU = jnp.bfloat16
_ACT = jnp.bfloat16
_VMEM_LIMIT = 56 * 1024 * 1024

S = 2048
D = 1024
NL = 4
DFF = 2816
LN_EPS = 1e-5
SSD_EPS = 1e-5
ALPHA = (2.0 * NL) ** 0.25
POOLW = 768
POOL_WINDOWS = (2, 4, 8, 16)
POOL_GDIM = 192
CH = 128
ATTN_DILS = (1, 4, 16)
HC = 9728
O_U, O_Z, O_XBC, O_Q, O_K, O_V, O_G, O_DT = 0, 768, 1792, 3840, 4608, 5376, 6144, 9216

ADAM_LR, ADAM_B1, ADAM_B2, ADAM_EPS, ADAM_WD, ADAM_STEP = 0.001, 0.9, 0.999, 1e-08, 0.01, 10

WEIGHTS = ['ffn1_w13', 'ffn1_w2', 'ln1_g', 'ln1_b', 'w_in', 'gate_b', 'pool_w', 'pool_b', 'pool_scale', 'conv_w',
           'conv_b', 'dt_bias', 'a_log', 'd_skip', 'ssd_norm', 'rel_bias', 'p_pool', 'p_ssd', 'p_attn', 'w_out',
           'ln2_g', 'ln2_b', 'ffn2_w13', 'ffn2_w2', 'ln3_g', 'ln3_b']
BIG = ['ffn1_w13', 'ffn1_w2', 'w_in', 'p_pool', 'p_ssd', 'p_attn', 'w_out', 'ffn2_w13', 'ffn2_w2']
COL_SHARDED = {'ffn1_w13', 'ffn2_w13', 'w_in', 'p_pool', 'p_attn'}
SMALL = [n for n in WEIGHTS if n not in BIG]


def _pcall(body, **kw):
    return pl.pallas_call(body, **kw)


def _cp(sem=None):
    return pltpu.CompilerParams(dimension_semantics=sem, vmem_limit_bytes=_VMEM_LIMIT)


def _pick(n, cands):
    for c in cands:
        if n % c == 0:
            return c
    raise ValueError(f"no tile for {n}")


def _mm(a, b, *, ta=False, tb=False, add=None, out_dtype=F32, dep=None, name):
    if ta:
        K, M = a.shape
    else:
        M, K = a.shape
    if tb:
        N, K2 = b.shape
    else:
        K2, N = b.shape
    assert K == K2, (a.shape, b.shape, ta, tb)
    sa, sb, so = a.dtype.itemsize, b.dtype.itemsize, jnp.dtype(out_dtype).itemsize
    tm, tn, tk = _mm_tiles(M, N, K, sa, sb, so + (4 if add is not None else 0))
    nk = K // tk
    a_bytes, b_bytes = M * K * sa, K * N * sb
    j_outer = nk == 1 and (b_bytes + a_bytes * (N // tn) < a_bytes + b_bytes * (M // tm))
    ij = (lambda p, q: (q, p)) if j_outer else (lambda p, q: (p, q))

    def im(f):
        return lambda p, q, k: f(*ij(p, q), k)

    a_spec = pl.BlockSpec((tk, tm), im(lambda i, j, k: (k, i))) if ta else pl.BlockSpec((tm, tk), im(lambda i, j, k: (i, k)))
    b_spec = pl.BlockSpec((tn, tk), im(lambda i, j, k: (j, k))) if tb else pl.BlockSpec((tk, tn), im(lambda i, j, k: (k, j)))
    o_spec = pl.BlockSpec((tm, tn), im(lambda i, j, k: (i, j)))
    dims = (((0 if ta else 1,), (1 if tb else 0,)), ((), ()))
    has_add = add is not None

    n_in = 2 + int(has_add) + int(dep is not None)

    def body(*refs):
        a_ref, b_ref = refs[0], refs[1]
        add_ref = refs[2] if has_add else None
        o_ref = refs[n_in]
        part = lax.dot_general(a_ref[...].astype(_MXU), b_ref[...].astype(_MXU), dims, preferred_element_type=F32)

        def finish(r):
            if has_add:
                r = r + add_ref[...]
            o_ref[...] = r.astype(out_dtype)

        if nk == 1:
            finish(part)
        else:
            acc = refs[-1]
            k = pl.program_id(2)

            @pl.when(k == 0)
            def _():
                acc[...] = part

            @pl.when(k > 0)
            def _():
                acc[...] += part

            @pl.when(k == nk - 1)
            def _():
                finish(acc[...])

    in_specs = [a_spec, b_spec]
    args = [a, b]
    if has_add:
        in_specs.append(o_spec)
        args.append(add)
    if dep is not None:
        in_specs.append(pl.BlockSpec(memory_space=pl.ANY))
        args.append(dep)
    gm, gn = M // tm, N // tn
    return _pcall(
        body, name=name, grid=((gn, gm, nk) if j_outer else (gm, gn, nk)), in_specs=in_specs, out_specs=o_spec,
        out_shape=jax.ShapeDtypeStruct((M, N), out_dtype),
        scratch_shapes=([pltpu.VMEM((tm, tn), F32)] if nk > 1 else []),
        compiler_params=_cp(("parallel", "parallel", "arbitrary")),
    )(*args)


_MM_VMEM_BUDGET = 40 * 1024 * 1024


def _divisors128(n, cap):
    return [d for d in range(128, min(n, cap) + 1, 128) if n % d == 0][::-1]


_MM_CYC_PER_MMAC = 4.35
_MM_CYC_PER_ACC_VREG = 2.03
_MM_HBM_BYTES_PER_CYC = 1455.0
_MM_CYC_PER_STEP = 770.0


def _mm_tiles(M, N, K, sa, sb, so):
    best = None
    for tm in _divisors128(M, 1408):
        for tn in _divisors128(N, 2560):
            for tk in ([K] if K <= 4096 else []) + _divisors128(K, 2816):
                nk = K // tk
                need = 2 * (tm * tk * sa + tk * tn * sb + tm * tn * so) + (tm * tn * 4 if nk > 1 else 0)
                need += tm * tk * 2 + tk * tn * 2 + tm * tn * 4
                if need > _MM_VMEM_BUDGET:
                    continue
                gm, gn = M // tm, N // tn
                a_bytes, b_bytes = M * K * sa, K * N * sb
                hbm = min(b_bytes + a_bytes * gn, a_bytes + b_bytes * gm) if nk == 1 else a_bytes * gn + b_bytes * gm
                hbm += M * N * so
                work = _MM_CYC_PER_MMAC * M * N * K / 1e6 + _MM_CYC_PER_ACC_VREG * (M * N / 1024) * (nk if nk > 1 else 0.5)
                cost = max(work, hbm / _MM_HBM_BYTES_PER_CYC) + gm * gn * nk * _MM_CYC_PER_STEP
                if best is None or cost < best[0]:
                    best = (cost, (tm, tn, tk))
    assert best is not None, (M, N, K)
    return best[1]


def _store(ref, val):
    if isinstance(val, (list, tuple)):
        off = 0
        for p in val:
            w = p.shape[1]
            ref[:, off:off + w] = p.astype(ref.dtype)
            off += w
    else:
        ref[...] = val.astype(ref.dtype)


def _acc_store(ref, val, first):
    pieces = val if isinstance(val, (list, tuple)) else [val]
    off = 0
    for p in pieces:
        w = p.shape[1]

        @pl.when(first)
        def _(p=p, off=off, w=w):
            ref[:, off:off + w] = p

        @pl.when(jnp.logical_not(first))
        def _(p=p, off=off, w=w):
            ref[:, off:off + w] += p

        off += w


def _rowwise(fn, tiled, full, out_tiled, out_acc, *, name, tm=256):
    arrs, specs = [], []
    for t in tiled:
        arr, w, cb = t if isinstance(t, tuple) else (t, t.shape[1], 0)
        arrs.append(arr)
        specs.append(pl.BlockSpec((tm, w), functools.partial(lambda i, cb: (i, cb), cb=cb)))
    R = arrs[0].shape[0]
    assert R % tm == 0
    for f in full:
        arrs.append(f)
        specs.append(pl.BlockSpec(f.shape, functools.partial(lambda i, nd: (0,) * nd, nd=f.ndim)))
    nt, nf, no = len(tiled), len(full), len(out_tiled)

    def body(*refs):
        tv = [r[...] for r in refs[:nt]]
        fv = [r[...] for r in refs[nt:nt + nf]]
        ot, oa = fn(tv, fv)
        for r, v in zip(refs[nt + nf:nt + nf + no], ot):
            _store(r, v)
        first = pl.program_id(0) == 0
        for r, v in zip(refs[nt + nf + no:], oa):
            _acc_store(r, v, first)

    out_shape = [jax.ShapeDtypeStruct((R, c), dt) for c, dt in out_tiled]
    out_specs = [pl.BlockSpec((tm, c), lambda i: (i, 0)) for c, _ in out_tiled]
    for shp in out_acc:
        out_shape.append(jax.ShapeDtypeStruct(shp, F32))
        out_specs.append(pl.BlockSpec(shp, lambda i: (0, 0)))
    return _pcall(body, name=name, grid=(R // tm,), in_specs=specs, out_specs=out_specs, out_shape=out_shape,
                  compiler_params=_cp(("arbitrary",)))(*arrs)


def _group(group):
    x, y, c = lax.axis_index("x"), lax.axis_index("y"), lax.axis_index("c")
    if group == "chips":
        return 2 * x + y, [((x, 1 - y, c), 2 * x + 1 - y), ((1 - x, y, c), 2 * (1 - x) + y),
                           ((1 - x, 1 - y, c), 2 * (1 - x) + 1 - y)]
    if group == "cores":
        return c, [((x, y, 1 - c), 1 - c)]
    if group == "x":
        return x, [((1 - x, y, c), 1 - x)]
    return y, [((x, 1 - y, c), 1 - y)]


def _exchange(arrs, group, mode, name, dep=None):
    chips = group == "chips"
    k = len(arrs)
    npeer = 3 if chips else 1

    def body(*refs):
        nd = 0 if dep is None else 1
        ins, outs = refs[:k], refs[k + nd:2 * k + nd]
        send_sems, recv_sems = refs[2 * k + nd:]
        me, peers = _group(group)
        remote = []
        for i in range(k):
            for p, (dev, slot) in enumerate(peers):
                src = ins[i].at[slot] if mode == "scatter" else ins[i]
                if not chips:
                    dst = outs[i]
                else:
                    dst = outs[i].at[p] if mode == "scatter" else outs[i].at[me]
                cp = pltpu.make_async_remote_copy(src_ref=src, dst_ref=dst, send_sem=send_sems.at[i, p],
                                                  recv_sem=recv_sems.at[i, p], device_id=dev,
                                                  device_id_type=pl.DeviceIdType.MESH)
                cp.start()
                remote.append(cp)
        for cp in remote:
            cp.wait_recv()
        for cp in remote:
            cp.wait_send()

    def oshape(a):
        piece = a.shape[1:] if mode == "scatter" else a.shape
        if chips:
            piece = ((3,) if mode == "scatter" else (4,)) + piece
        return jax.ShapeDtypeStruct(piece, a.dtype)

    any_spec = pl.BlockSpec(memory_space=pl.ANY)
    extra = [] if dep is None else [dep]
    return _pcall(body, name=name, in_specs=[any_spec] * (k + len(extra)), out_specs=[any_spec] * k,
                  out_shape=[oshape(a) for a in arrs],
                  scratch_shapes=[pltpu.SemaphoreType.DMA((k, npeer)), pltpu.SemaphoreType.DMA((k, npeer))])(*arrs, *extra)


def _split_copies(ins, lands, send_sems, recv_sems, group, mode):
    chips = group == "chips"
    me, peers = _group(group)
    npeer = len(peers)
    out = []
    for i in range(len(ins)):
        for p, (dev, slot) in enumerate(peers):
            src = ins[i].at[slot] if mode == "scatter" else ins[i]
            if not chips:
                dst = lands[i]
            else:
                dst = lands[i].at[p] if mode == "scatter" else lands[i].at[me]
            out.append(pltpu.make_async_remote_copy(src_ref=src, dst_ref=dst, send_sem=send_sems.at[npeer * i + p],
                                                    recv_sem=recv_sems.at[npeer * i + p], device_id=dev,
                                                    device_id_type=pl.DeviceIdType.MESH))
    return out


def _exchange_start(arrs, group, mode, name):
    k = len(arrs)
    chips = group == "chips"
    nsem = (3 if chips else 1) * k
    hbm = pl.BlockSpec(memory_space=pltpu.HBM)
    sem = pl.BlockSpec(memory_space=pltpu.SEMAPHORE)

    def land_shape(a):
        piece = a.shape[1:] if mode == "scatter" else a.shape
        if chips:
            piece = ((3,) if mode == "scatter" else (4,)) + piece
        return piece

    def body(*refs):
        ins, lands = refs[:k], refs[k:2 * k]
        send_sems, recv_sems = refs[2 * k], refs[2 * k + 1]
        token = refs[-1]
        for cp in _split_copies(ins, lands, send_sems, recv_sems, group, mode):
            cp.start()
        token[...] = jnp.zeros_like(token)

    srcs = [pltpu.with_memory_space_constraint(a, pltpu.HBM) for a in arrs]
    lands = [pltpu.with_memory_space_constraint(lax.empty(land_shape(a), a.dtype), pltpu.HBM) for a in arrs]
    out_shape = ([pltpu.SemaphoreType.DMA((nsem,)), pltpu.SemaphoreType.DMA((nsem,))]
                 + [pltpu.HBM(a.shape, a.dtype) for a in arrs] + [pltpu.HBM(land_shape(a), a.dtype) for a in arrs]
                 + [jax.ShapeDtypeStruct((8, 128), F32)])
    outs = _pcall(body, name=name, in_specs=[hbm] * (2 * k),
                  out_specs=[sem, sem] + [hbm] * (2 * k) + [pl.BlockSpec(memory_space=pltpu.VMEM)], out_shape=out_shape,
                  input_output_aliases={i: 2 + i for i in range(2 * k)},
                  compiler_params=pltpu.CompilerParams(has_side_effects=pltpu.SideEffectType.DATAFLOW_SIDE_EFFECTING))(
                      *srcs, *lands)
    return (outs[0], outs[1], list(outs[2:2 + k]), list(outs[2 + k:2 + 2 * k])), outs[-1]


def _exchange_wait(state, after, group, mode, name):
    send_sems, recv_sems, srcs, lands = state
    k = len(srcs)
    after = list(after) if isinstance(after, (list, tuple)) else [after]
    hbm = pl.BlockSpec(memory_space=pltpu.HBM)
    sem = pl.BlockSpec(memory_space=pltpu.SEMAPHORE)

    def body(*refs):
        ins, lnd = refs[:k], refs[k:2 * k]
        send_sems, recv_sems = refs[2 * k], refs[2 * k + 1]
        for cp in _split_copies(ins, lnd, send_sems, recv_sems, group, mode):
            cp.wait_send()
            cp.wait_recv()

    outs = _pcall(body, name=name,
                  in_specs=[hbm] * (2 * k) + [sem, sem] + [pl.BlockSpec(memory_space=pl.ANY)] * len(after),
                  out_specs=[hbm] * (2 * k),
                  out_shape=[pltpu.HBM(a.shape, a.dtype) for a in srcs] + [pltpu.HBM(a.shape, a.dtype) for a in lands],
                  input_output_aliases={i: i for i in range(2 * k)},
                  compiler_params=pltpu.CompilerParams(has_side_effects=pltpu.SideEffectType.DATAFLOW_SIDE_EFFECTING))(
                      *srcs, *lands, send_sems, recv_sems, *after)
    return list(outs[:k]), list(outs[k:])


def _sum_own_recv(own, recv, me, out_dtype, name):
    n, R, C = own.shape
    nr = 1 if recv.ndim == 2 else recv.shape[0]
    tr = _pick(R, (256, 128, 64, 32, 16, 8))

    def body(me_ref, own_ref, *refs):
        o_ref = refs[-1]
        acc = own_ref[...].astype(F32)
        for r in refs[:-1]:
            acc = acc + r[...].astype(F32)
        o_ref[...] = acc.astype(out_dtype)

    specs = [pl.BlockSpec((None, tr, C), lambda i, me_ref: (me_ref[0], i, 0))]
    args = [own]
    if recv.ndim == 2:
        specs.append(pl.BlockSpec((tr, C), lambda i, me_ref: (i, 0)))
        args.append(recv)
    else:
        for p in range(nr):
            specs.append(pl.BlockSpec((None, tr, C), functools.partial(lambda i, me_ref, p: (p, i, 0), p=p)))
            args.append(recv)
    gs = pltpu.PrefetchScalarGridSpec(num_scalar_prefetch=1, grid=(R // tr,), in_specs=specs,
                                      out_specs=pl.BlockSpec((tr, C), lambda i, me_ref: (i, 0)))
    return _pcall(body, name=name, grid_spec=gs, out_shape=jax.ShapeDtypeStruct((R, C), out_dtype),
                  compiler_params=_cp(("parallel",)))(me, *args)


def _silu(x):
    return x * jax.nn.sigmoid(x)


def _ln(r, g, b):
    mu = jnp.mean(r, -1, keepdims=True)
    xc = r - mu
    var = jnp.mean(xc * xc, -1, keepdims=True)
    return xc * lax.rsqrt(var + LN_EPS) * g + b


def _softplus(x):
    return jnp.maximum(x, 0.0) + jnp.log1p(jnp.exp(-jnp.abs(x)))


def _res_ln_fwd(x, y, g, b, res, name):
    def fn(tv, fv):
        r = ALPHA * tv[0] + res * tv[1]
        out = _ln(r, fv[0], fv[1])
        return [r, out, out], []
    return _rowwise(fn, [x, y], [g, b], [(D, F32), (D, F32), (D, _ACT)], [], name=name)


def _ln_bwd(r, g, b, dout, res, name):
    def fn(tv, fv):
        _, vjp = jax.vjp(_ln, tv[0], fv[0], fv[1])
        dr, dg, db = vjp(tv[1])
        return [ALPHA * dr, res * dr], [dg, db]
    return _rowwise(fn, [r, dout], [g, b], [(D, F32), (D, _ACT)], [(1, D), (1, D)], name=name)


def _swiglu_act(h, name):
    def fn(tv, fv):
        return [_silu(tv[0]) * tv[1]], []
    return _rowwise(fn, [(h, DFF, 0), (h, DFF, 1)], [], [(DFF, _ACT)], [], name=name)[0]


def _swiglu_act_bwd(h, ds, name):
    def fn(tv, fv):
        s, vjp = jax.vjp(lambda a, g: _silu(a) * g, tv[0], tv[1])
        da, dg = vjp(tv[2])
        return [[da, dg], s], []
    return _rowwise(fn, [(h, DFF, 0), (h, DFF, 1), ds], [], [(2 * DFF, _ACT), (DFF, _ACT)], [], name=name)


def _loss_fwd_bwd(y, tgt, name):
    def fn(tv, fv):
        e = tv[0] - tv[1]
        row = jnp.sum(e * e, axis=1, keepdims=True)
        tot = jnp.sum(row, axis=0, keepdims=True) * (0.5 / D)
        return [e * (1.0 / D)], [jnp.broadcast_to(tot, (1, 128))]
    return _rowwise(fn, [y, tgt], [], [(D, F32)], [(1, 128)], name=name)


def _shift_down(x, k, row):
    return jnp.where(row >= k, pltpu.roll(x, k, axis=0), 0.0)


def _shift_up(x, k, row):
    n = x.shape[0]
    return jnp.where(row < n - k, pltpu.roll(x, n - k, axis=0), 0.0)


def _pool_window_masks(j):
    lane = lax.broadcasted_iota(jnp.int32, (1, 128), 1) + j * 128
    grp = lane // POOL_GDIM
    return [grp == g for g in range(4)]


def _pool_mean(u, bwd, name, col0=0):
    T = u.shape[0]
    B = T // S

    def body(u_ref, o_ref):
        j = pl.program_id(1)
        x = u_ref[...]
        row = lax.broadcasted_iota(jnp.int32, (S, 1), 0)
        masks = _pool_window_masks(j)
        inv = [1.0 / jnp.minimum(row + 1, w).astype(F32) for w in POOL_WINDOWS]
        if not bwd:
            s2 = x + _shift_down(x, 1, row)
            s4 = s2 + _shift_down(s2, 2, row)
            s8 = s4 + _shift_down(s4, 4, row)
            s16 = s8 + _shift_down(s8, 8, row)
            mean = jnp.where(masks[0], s2 * inv[0], jnp.where(masks[1], s4 * inv[1],
                             jnp.where(masks[2], s8 * inv[2], s16 * inv[3])))
            o_ref[...] = (mean - x).astype(o_ref.dtype)
        else:
            g = [jnp.where(masks[i], x * inv[i], 0.0) for i in range(4)]
            t = g[3]
            t = t + _shift_up(t, 8, row) + g[2]
            t = t + _shift_up(t, 4, row) + g[1]
            t = t + _shift_up(t, 2, row) + g[0]
            t = t + _shift_up(t, 1, row)
            o_ref[...] = (t - x).astype(o_ref.dtype)

    spec = pl.BlockSpec((S, 128), lambda b, j: (b, j))
    return _pcall(body, name=name, grid=(B, POOLW // 128),
                  in_specs=[pl.BlockSpec((S, 128), lambda b, j: (b, j + col0))], out_specs=spec,
                  out_shape=jax.ShapeDtypeStruct((T, POOLW), _ACT), compiler_params=_cp(("parallel", "parallel")))(u)


def _conv_silu(xbc, w, b, name, col0=0):
    T, C = xbc.shape[0], w.shape[1]
    B = T // S

    def body(x_ref, w_ref, b_ref, o_ref):
        x = x_ref[...]
        row = lax.broadcasted_iota(jnp.int32, (S, 1), 0)
        c = b_ref[...] + w_ref[3:4, :] * x
        for s in range(1, 4):
            c = c + w_ref[3 - s:4 - s, :] * _shift_down(x, s, row)
        o_ref[...] = _silu(c)

    return _pcall(body, name=name, grid=(B, C // 128),
                  in_specs=[pl.BlockSpec((S, 128), lambda b, j: (b, j + col0)), pl.BlockSpec((4, 128), lambda b, j: (0, j)),
                            pl.BlockSpec((1, 128), lambda b, j: (0, j))],
                  out_specs=pl.BlockSpec((S, 128), lambda b, j: (b, j)),
                  out_shape=jax.ShapeDtypeStruct((T, C), F32), compiler_params=_cp(("parallel", "parallel")))(xbc, w, b)


def _conv_silu_bwd(xbc, w, b, dact, name, col0=0):
    T, C = xbc.shape[0], w.shape[1]
    B = T // S

    def body(x_ref, w_ref, b_ref, d_ref, dx_ref, dw_ref, db_ref):
        bi = pl.program_id(1)
        x = x_ref[...]
        row = lax.broadcasted_iota(jnp.int32, (S, 1), 0)
        xs = [x] + [_shift_down(x, s, row) for s in range(1, 4)]
        c = b_ref[...]
        for s in range(4):
            c = c + w_ref[3 - s:4 - s, :] * xs[s]
        _, vjp = jax.vjp(_silu, c)
        dc = vjp(d_ref[...])[0]
        dx = w_ref[3:4, :] * dc
        for s in range(1, 4):
            dx = dx + w_ref[3 - s:4 - s, :] * _shift_up(dc, s, row)
        dx_ref[...] = dx.astype(dx_ref.dtype)
        first = bi == 0
        for s in range(4):
            _acc_rows(dw_ref, 3 - s, jnp.sum(dc * xs[s], axis=0, keepdims=True), first)
        _acc_rows(db_ref, 0, jnp.sum(dc, axis=0, keepdims=True), first)

    blk = pl.BlockSpec((S, 128), lambda j, b: (b, j))
    return _pcall(body, name=name, grid=(C // 128, B),
                  in_specs=[pl.BlockSpec((S, 128), lambda j, b: (b, j + col0)), pl.BlockSpec((4, 128), lambda j, b: (0, j)),
                            pl.BlockSpec((1, 128), lambda j, b: (0, j)), blk],
                  out_specs=[blk, pl.BlockSpec((4, 128), lambda j, b: (0, j)), pl.BlockSpec((1, 128), lambda j, b: (0, j))],
                  out_shape=[jax.ShapeDtypeStruct((T, C), _ACT), jax.ShapeDtypeStruct((4, C), F32),
                             jax.ShapeDtypeStruct((1, C), F32)],
                  compiler_params=_cp(("parallel", "arbitrary")))(xbc, w, b, dact)


def _acc_rows(ref, r, val, first):
    @pl.when(first)
    def _():
        ref[r:r + 1, :] = val

    @pl.when(jnp.logical_not(first))
    def _():
        ref[r:r + 1, :] += val


def _tri_consts():
    i = lax.broadcasted_iota(jnp.int32, (CH, CH), 0)
    j = lax.broadcasted_iota(jnp.int32, (CH, CH), 1)
    return (i == j).astype(F32), (j <= i).astype(F32), (i <= j).astype(F32), i >= j


def _ssd_chunk(h, x, dt, Bm, Cm, a, dsk, consts):
    eye, tril, triu, lower = consts
    Bb = Bm.astype(_MXU)
    Cb = Cm.astype(_MXU)
    cb = lax.dot_general(Cb, Bb, (((1,), (1,)), ((), ())), preferred_element_type=F32)
    ys, hn = [], []
    for e in range(4):
        adt = dt[e] * a[e]
        adt_row = jnp.sum(adt * eye, axis=0, keepdims=True)
        cs_col = jnp.sum(adt_row * tril, axis=1, keepdims=True)
        cs_row = jnp.sum(adt * triu, axis=0, keepdims=True)
        cs_last = jnp.sum(adt, axis=0, keepdims=True)
        decay = jnp.exp(jnp.where(lower, cs_col - cs_row, -jnp.inf))
        xb = (x[e] * dt[e]).astype(_MXU)
        y_diag = jnp.dot((cb * decay).astype(_MXU), xb, preferred_element_type=F32)
        bdec = (Bm * jnp.exp(cs_last - cs_col)).astype(_MXU)
        st = lax.dot_general(bdec, xb, (((0,), (0,)), ((), ())), preferred_element_type=F32)
        hn.append(h[e] * jnp.exp(cs_last) + st)
        y_off = jnp.exp(cs_col) * jnp.dot(Cb, h[e].astype(_MXU), preferred_element_type=F32)
        ys.append(y_diag + y_off + dsk[e] * x[e])
    return ys, hn


def _ssd_specs(order):
    def im(f):
        return lambda p, q: f(*order(p, q))
    xs = pl.BlockSpec((S, 256), im(lambda b, g: (b, g)))
    dt = pl.BlockSpec((None, S, 4), im(lambda b, g: (g, b, 0)))
    bc = pl.BlockSpec((S, 128), im(lambda b, g: (b, g)))
    hd = pl.BlockSpec((None, 1, 4), im(lambda b, g: (g, 0, 0)))
    hs = pl.BlockSpec((None, None, S // CH, 4, 128, 64), im(lambda b, g: (b, g, 0, 0, 0, 0)))
    bw = pl.BlockSpec((S, 128), im(lambda b, g: (b, 8 + g)))
    cw = pl.BlockSpec((S, 128), im(lambda b, g: (b, 12 + g)))
    return xs, dt, bc, hd, hs, bw, cw


def _ssd_fwd(act, dtg, a, dsk, name):
    xs = bm = cm = act
    T = xs.shape[0]
    B = T // S
    nc = S // CH

    def body(x_ref, dt_ref, b_ref, c_ref, a_ref, k_ref, y_ref, hs_ref, h_ref):
        consts = _tri_consts()
        h_ref[...] = jnp.zeros_like(h_ref)
        al = [a_ref[:, e:e + 1] for e in range(4)]
        kl = [k_ref[:, e:e + 1] for e in range(4)]

        def step(c, carry):
            r0 = pl.multiple_of(c * CH, CH)
            rows = pl.ds(r0, CH)
            h = [h_ref[e] for e in range(4)]
            for e in range(4):
                hs_ref[c, e] = h[e]
            x = [x_ref[rows, 64 * e:64 * e + 64] for e in range(4)]
            dt = [dt_ref[rows, e:e + 1] for e in range(4)]
            ys, hn = _ssd_chunk(h, x, dt, b_ref[rows, :], c_ref[rows, :], al, kl, consts)
            for e in range(4):
                y_ref[rows, 64 * e:64 * e + 64] = ys[e]
                h_ref[e] = hn[e]
            return carry

        lax.fori_loop(0, nc, step, 0)

    sx, sdt, sbc, shd, shs, sbw, scw = _ssd_specs(lambda b, g: (b, g))
    return _pcall(body, name=name, grid=(B, 4), in_specs=[sx, sdt, sbw, scw, shd, shd], out_specs=[sx, shs],
                  out_shape=[jax.ShapeDtypeStruct((T, 1024), F32), jax.ShapeDtypeStruct((B, 4, nc, 4, 128, 64), F32)],
                  scratch_shapes=[pltpu.VMEM((4, 128, 64), F32)],
                  compiler_params=_cp(("parallel", "parallel")))(xs, dtg, bm, cm, a, dsk)


def _lane_place(vals, width):
    lane = lax.broadcasted_iota(jnp.int32, (1, width), 1)
    out = jnp.zeros((1, width), F32)
    for e, v in enumerate(vals):
        out = out + jnp.where(lane == e, v, 0.0)
    return out


def _ssd_bwd(act, dtg, a, dsk, hs, dy, name):
    xs = bm = cm = act
    T = xs.shape[0]
    B = T // S
    nc = S // CH

    def body(x_ref, dt_ref, b_ref, c_ref, a_ref, k_ref, hs_ref, dy_ref,
             dx_ref, ddt_ref, db_ref, dc_ref, dak_ref, dh_ref, sc_ref):
        bi = pl.program_id(1)
        consts = _tri_consts()
        dh_ref[...] = jnp.zeros_like(dh_ref)
        sc_ref[...] = jnp.zeros_like(sc_ref)
        al = [a_ref[:, e:e + 1] for e in range(4)]
        kl = [k_ref[:, e:e + 1] for e in range(4)]

        def step(i, carry):
            c = nc - 1 - i
            r0 = pl.multiple_of(c * CH, CH)
            rows = pl.ds(r0, CH)
            h = [hs_ref[c, e] for e in range(4)]
            x = [x_ref[rows, 64 * e:64 * e + 64] for e in range(4)]
            dt = [dt_ref[rows, e:e + 1] for e in range(4)]
            f = functools.partial(_ssd_chunk, consts=consts)
            _, vjp = jax.vjp(f, h, x, dt, b_ref[rows, :], c_ref[rows, :], al, kl)
            dys = [dy_ref[rows, 64 * e:64 * e + 64] for e in range(4)]
            dhn = [dh_ref[e] for e in range(4)]
            dh, dx, ddt, dB, dC, da, dk = vjp((dys, dhn))
            for e in range(4):
                dh_ref[e] = dh[e]
                dx_ref[rows, 64 * e:64 * e + 64] = dx[e]
                ddt_ref[rows, e:e + 1] = ddt[e]
            db_ref[rows, :] = dB
            dc_ref[rows, :] = dC
            sc_ref[0:1, :] += _lane_place(da, 128)
            sc_ref[1:2, :] += _lane_place(dk, 128)
            return carry

        lax.fori_loop(0, nc, step, 0)
        first = bi == 0

        @pl.when(first)
        def _():
            dak_ref[...] = sc_ref[...]

        @pl.when(jnp.logical_not(first))
        def _():
            dak_ref[...] += sc_ref[...]

    sx, sdt, sbc, shd, shs, sbw, scw = _ssd_specs(lambda g, b: (b, g))
    return _pcall(body, name=name, grid=(4, B), in_specs=[sx, sdt, sbw, scw, shd, shd, shs, sx],
                  out_specs=[sx, sdt, sbc, sbc, pl.BlockSpec((None, 8, 128), lambda g, b: (g, 0, 0))],
                  out_shape=[jax.ShapeDtypeStruct((T, 1024), F32), jax.ShapeDtypeStruct((4, T, 4), F32),
                             jax.ShapeDtypeStruct((T, 512), F32), jax.ShapeDtypeStruct((T, 512), F32),
                             jax.ShapeDtypeStruct((4, 8, 128), F32)],
                  scratch_shapes=[pltpu.VMEM((4, 128, 64), F32), pltpu.VMEM((8, 128), F32)],
                  compiler_params=_cp(("parallel", "arbitrary")))(xs, dtg, bm, cm, a, dsk, hs, dy)


def _gate_norm(y, z, nw):
    t = y * _silu(z)
    return t * lax.rsqrt(jnp.mean(t * t, axis=-1, keepdims=True) + SSD_EPS) * nw


def _ssd_gate_norm(y, z, nw, name, zcol=0):
    def fn(tv, fv):
        return [[_gate_norm(tv[g], tv[4 + g], fv[0][:, 256 * g:256 * g + 256]) for g in range(4)]], []
    tiled = [(y, 256, g) for g in range(4)] + [(z, 256, zcol + g) for g in range(4)]
    return _rowwise(fn, tiled, [nw], [(1024, _ACT)], [], name=name)[0]


def _ssd_gate_norm_bwd(y, z, nw, dout, name, zcol=0):
    def fn(tv, fv):
        dys, dzs, dns = [], [], []
        for g in range(4):
            _, vjp = jax.vjp(_gate_norm, tv[g], tv[4 + g], fv[0][:, 256 * g:256 * g + 256])
            a, b, c = vjp(tv[8 + g])
            dys.append(a)
            dzs.append(b)
            dns.append(c)
        return [dys, dzs], [dns]
    tiled = [(y, 256, g) for g in range(4)] + [(z, 256, zcol + g) for g in range(4)] + [(dout, 256, g) for g in range(4)]
    return _rowwise(fn, tiled, [nw], [(1024, F32), (1024, _ACT)], [(1, 1024)], name=name)


def _t5_bucket_np(dist):
    dist = np.maximum(dist, 0)
    max_exact = 16
    large = max_exact + (np.log(np.maximum(dist, 1) / max_exact) / np.log(2048 / max_exact) * (32 - max_exact)).astype(np.int32)
    large = np.minimum(large, 31)
    return np.where(dist < max_exact, dist, large).astype(np.int32)


def _bucket_maps():
    qi = np.arange(128)[:, None]
    kj = np.arange(256)[None, :]
    return np.stack([_t5_bucket_np((qi - kj + 128) * dil) for dil in ATTN_DILS]).astype(np.int32)


def _bias_build(rel_bias, maps, name):
    def body(tab_ref, map_ref, o_ref):
        hh = pl.program_id(0)
        m = map_ref[...]
        acc = jnp.zeros((128, 256), F32)
        for b in range(32):
            acc = jnp.where(m == b, tab_ref[b, hh], acc)
        o_ref[...] = acc

    return _pcall(body, name=name, grid=(12,),
                  in_specs=[pl.BlockSpec(memory_space=pltpu.SMEM), pl.BlockSpec((None, 128, 256), lambda h: (h // 4, 0, 0))],
                  out_specs=pl.BlockSpec((None, 128, 256), lambda h: (h, 0, 0)),
                  out_shape=jax.ShapeDtypeStruct((12, 128, 256), F32), compiler_params=_cp(("parallel",)))(rel_bias, maps)


def _bias_reduce(dbias, maps, name):
    nl = dbias.shape[0]

    def body(d_ref, map_ref, o_ref):
        m = map_ref[...]
        d = d_ref[0]
        for i in range(1, nl):
            d = d + d_ref[i]
        lane = lax.broadcasted_iota(jnp.int32, (1, 128), 1)
        out = jnp.zeros((1, 128), F32)
        for b in range(32):
            s = jnp.sum(jnp.sum(jnp.where(m == b, d, 0.0), axis=1, keepdims=True), axis=0, keepdims=True)
            out = out + jnp.where(lane == b, s, 0.0)
        o_ref[...] = out

    return _pcall(body, name=name, grid=(12,),
                  in_specs=[pl.BlockSpec((nl, None, 128, 256), lambda h: (0, h, 0, 0)),
                            pl.BlockSpec((None, 128, 256), lambda h: (h // 4, 0, 0))],
                  out_specs=pl.BlockSpec((None, 1, 128), lambda h: (h, 0, 0)),
                  out_shape=jax.ShapeDtypeStruct((12, 1, 128), F32), compiler_params=_cp(("parallel",)))(dbias, maps)


def _attn_block(q, kb, vb, bias, mask):
    s = lax.dot_general(q.astype(_MXU), kb.astype(_MXU), (((1,), (1,)), ((), ())), preferred_element_type=F32) * 0.125 + bias
    s = jnp.where(mask, s, -jnp.inf)
    m = lax.stop_gradient(jnp.max(s, axis=-1, keepdims=True))
    p = jnp.exp(s - m)
    den = jnp.sum(p, axis=-1, keepdims=True)
    out = jnp.dot((p / den).astype(_MXU), vb.astype(_MXU), preferred_element_type=F32)
    return out, m + jnp.log(den)


ATTN_QB = 512


def _attn_masks(dil):
    qi = lax.broadcasted_iota(jnp.int32, (ATTN_QB, ATTN_QB + 128), 0)
    kj = lax.broadcasted_iota(jnp.int32, (ATTN_QB, ATTN_QB + 128), 1)
    band = (kj >= qi) & (kj <= qi + 128)
    if dil == 16:
        q2 = lax.broadcasted_iota(jnp.int32, (ATTN_QB, ATTN_QB), 0)
        k2 = lax.broadcasted_iota(jnp.int32, (ATTN_QB, ATTN_QB), 1)
        return ((q2 // 128) == (k2 // 128)) & (k2 <= q2), None
    return band[:, 128:], band


def _attn_wide_bias(b, dil):
    if dil == 16:
        return jnp.tile(b[:, 128:], (4, 4)), None
    z = jnp.zeros((128, 128), F32)
    band = jnp.concatenate([jnp.concatenate([z] * i + [b] + [z] * (3 - i), axis=1) for i in range(4)], axis=0)
    return band[:, 128:], band


def _fold_dbias(dbs, dil, band_form):
    def blk(i, j):
        return dbs[128 * i:128 * i + 128, 128 * j:128 * j + 128]
    if band_form:
        return sum(blk(i, i) for i in range(4)), sum(blk(i, i + 1) for i in range(4))
    cur = sum(blk(i, i) for i in range(4))
    if dil == 16:
        return None, cur
    return sum(blk(i, i - 1) for i in range(1, 4)), cur


def _attn_chunks(dil):
    out = []
    for n in range(S // ATTN_QB):
        if dil == 1 and n > 0:
            out.append((n * ATTN_QB, n * ATTN_QB - 128, ATTN_QB + 128, True))
        else:
            out.append((n * ATTN_QB, n * ATTN_QB, ATTN_QB, False))
    return out


def _qkv_specs(gi, order):
    def spec(base):
        col = (base + 256 * gi) // 128
        return pl.BlockSpec((S, 128), lambda p, q: (order(p, q)[0], col + order(p, q)[1]))
    return [spec(O_Q), spec(O_K), spec(O_V)]


def _residue_rows(r, dil):
    return pl.ds(r, S // dil, stride=dil)


def _attn_fwd(hcat, bias_all, gi, name):
    dil = ATTN_DILS[gi]
    T = hcat.shape[0]
    B, L = T // S, S // dil

    def body(q_ref, k_ref, v_ref, b_ref, o_ref, l_ref, *scr):
        mask_first, mask_band = _attn_masks(dil)
        if dil > 1:
            qs, ks, vs, os_, ls = scr
            for r in range(dil):
                rows, dst = _residue_rows(r, dil), pl.ds(r * L, L)
                qs[dst, :] = q_ref[rows, :]
                ks[dst, :] = k_ref[rows, :]
                vs[dst, :] = v_ref[rows, :]
        else:
            qs, ks, vs, os_, ls = q_ref, k_ref, v_ref, o_ref, l_ref
        ls[...] = jnp.zeros_like(ls)
        for e in range(2):
            lanes = slice(64 * e, 64 * e + 64)
            bias_first, bias_band = _attn_wide_bias(b_ref[e], dil)
            for q0, k0, kn, band_form in _attn_chunks(dil):
                cur, keys = pl.ds(q0, ATTN_QB), pl.ds(k0, kn)
                o, l = _attn_block(qs[cur, lanes], ks[keys, lanes], vs[keys, lanes],
                                   bias_band if band_form else bias_first, mask_band if band_form else mask_first)
                os_[cur, lanes] = o
                ls[cur, e:e + 1] = l
        if dil > 1:
            for r in range(dil):
                rows, src = _residue_rows(r, dil), pl.ds(r * L, L)
                o_ref[rows, :] = os_[src, :]
                l_ref[rows, :] = ls[src, :]

    scratch = [pltpu.VMEM((S, 128), F32)] * 5 if dil > 1 else []
    return _pcall(body, name=name, grid=(B, 2),
                  in_specs=_qkv_specs(gi, lambda b, hp: (b, hp))
                  + [pl.BlockSpec((2, 128, 256), lambda b, hp: (2 * gi + hp, 0, 0))],
                  out_specs=[pl.BlockSpec((S, 128), lambda b, hp: (b, hp)),
                             pl.BlockSpec((None, S, 128), lambda b, hp: (hp, b, 0))],
                  out_shape=[jax.ShapeDtypeStruct((T, 256), F32), jax.ShapeDtypeStruct((2, T, 128), F32)],
                  scratch_shapes=scratch,
                  compiler_params=_cp(("parallel", "parallel")))(hcat, hcat, hcat, bias_all)


def _attn_bwd(hcat, bias_all, gi, do, dl, name):
    dil = ATTN_DILS[gi]
    T = hcat.shape[0]
    B, L = T // S, S // dil

    def body(q_ref, k_ref, v_ref, b_ref, do_ref, dl_ref, dq_ref, dk_ref, dv_ref, db_ref, acc_ref, *scr):
        bi = pl.program_id(1)
        mask_first, mask_band = _attn_masks(dil)
        if dil > 1:
            qs, ks, vs, dos, dls, dqs, dks, dvs = scr
            for r in range(dil):
                rows, dst = _residue_rows(r, dil), pl.ds(r * L, L)
                qs[dst, :] = q_ref[rows, :]
                ks[dst, :] = k_ref[rows, :]
                vs[dst, :] = v_ref[rows, :]
                dos[dst, :] = do_ref[rows, :]
                dls[dst, :] = dl_ref[rows, :]
        else:
            qs, ks, vs, dos, dls, dqs, dks, dvs = q_ref, k_ref, v_ref, do_ref, dl_ref, dq_ref, dk_ref, dv_ref
        dks[...] = jnp.zeros_like(dks)
        dvs[...] = jnp.zeros_like(dvs)
        for e in range(2):
            lanes = slice(64 * e, 64 * e + 64)
            bias_first, bias_band = _attn_wide_bias(b_ref[e], dil)
            acc_ref[...] = jnp.zeros_like(acc_ref)
            for q0, k0, kn, band_form in _attn_chunks(dil):
                cur, keys = pl.ds(q0, ATTN_QB), pl.ds(k0, kn)
                f = functools.partial(_attn_block, mask=mask_band if band_form else mask_first)
                _, vjp = jax.vjp(f, qs[cur, lanes], ks[keys, lanes], vs[keys, lanes],
                                 bias_band if band_form else bias_first)
                dq, dkb, dvb, dbs = vjp((dos[cur, lanes], dls[cur, e:e + 1]))
                dqs[cur, lanes] = dq
                dks[keys, lanes] += dkb
                dvs[keys, lanes] += dvb
                prev, here = _fold_dbias(dbs, dil, band_form)
                if prev is not None:
                    acc_ref[:, 0:128] += prev
                acc_ref[:, 128:256] += here

            @pl.when(bi == 0)
            def _(e=e):
                db_ref[e] = acc_ref[...]

            @pl.when(bi > 0)
            def _(e=e):
                db_ref[e] += acc_ref[...]

        if dil > 1:
            for r in range(dil):
                rows, src = _residue_rows(r, dil), pl.ds(r * L, L)
                dq_ref[rows, :] = dqs[src, :]
                dk_ref[rows, :] = dks[src, :]
                dv_ref[rows, :] = dvs[src, :]

    order = lambda hp, b: (b, hp)
    blk = pl.BlockSpec((S, 128), lambda hp, b: (b, hp))
    lblk = pl.BlockSpec((None, S, 128), lambda hp, b: (hp, b, 0))
    sds = jax.ShapeDtypeStruct((T, 256), F32)
    scratch = [pltpu.VMEM((128, 256), F32)] + ([pltpu.VMEM((S, 128), F32)] * 8 if dil > 1 else [])
    return _pcall(body, name=name, grid=(2, B),
                  in_specs=_qkv_specs(gi, order) + [pl.BlockSpec((2, 128, 256), lambda hp, b: (2 * gi + hp, 0, 0)), blk, lblk],
                  out_specs=[blk, blk, blk, pl.BlockSpec((2, 128, 256), lambda hp, b: (hp, 0, 0))],
                  out_shape=[sds, sds, sds, jax.ShapeDtypeStruct((4, 128, 256), F32)],
                  scratch_shapes=scratch,
                  compiler_params=_cp(("parallel", "arbitrary")))(hcat, hcat, hcat, bias_all, do, dl)


def _lse_merge(o0, o1, o2, l0, l1, l2):
    m = lax.stop_gradient(jnp.maximum(jnp.maximum(l0, l1), l2))
    e0, e1, e2 = jnp.exp(l0 - m), jnp.exp(l1 - m), jnp.exp(l2 - m)
    den = e0 + e1 + e2
    return (e0 / den) * o0 + (e1 / den) * o1 + (e2 / den) * o2


def _attn_merge(outs, lses, dy, name):
    T = outs[0].shape[0]
    bwd = dy is not None
    tm = 512

    def body(*refs):
        o_refs, l_refs = refs[:3], refs[3:6]
        if bwd:
            for r in refs[10:13]:
                r[...] = jnp.zeros_like(r)
        for e in range(2):
            lanes = slice(64 * e, 64 * e + 64)
            vals = [r[:, lanes] for r in o_refs] + [r[:, e:e + 1] for r in l_refs]
            if not bwd:
                refs[6][:, lanes] = _lse_merge(*vals).astype(refs[6].dtype)
            else:
                _, vjp = jax.vjp(_lse_merge, *vals)
                g = vjp(refs[6][:, lanes])
                for r, v in zip(refs[7:10], g[:3]):
                    r[:, lanes] = v
                for r, v in zip(refs[10:13], g[3:]):
                    r[:, e:e + 1] = v

    blk = pl.BlockSpec((tm, 128), lambda i, hp: (i, hp))
    lblk = pl.BlockSpec((None, tm, 128), lambda i, hp: (hp, i, 0))
    lsd = jax.ShapeDtypeStruct((2, T, 128), F32)
    if not bwd:
        return _pcall(body, name=name, grid=(T // tm, 2), in_specs=[blk] * 3 + [lblk] * 3, out_specs=blk,
                      out_shape=jax.ShapeDtypeStruct((T, 256), F32),
                      compiler_params=_cp(("parallel", "parallel")))(*outs, *lses)
    return _pcall(body, name=name, grid=(T // tm, 2), in_specs=[blk] * 3 + [lblk] * 3 + [blk],
                  out_specs=[blk] * 3 + [lblk] * 3, out_shape=[jax.ShapeDtypeStruct((T, 256), F32)] * 3 + [lsd] * 3,
                  compiler_params=_cp(("parallel", "parallel")))(*outs, *lses, dy)


def _gmerge(g0, g1, g2, gb, ya, yb, yc):
    return (jax.nn.sigmoid(g0 + gb[:, 0:D]) * ya + jax.nn.sigmoid(g1 + gb[:, D:2 * D]) * yb
            + jax.nn.sigmoid(g2 + gb[:, 2 * D:3 * D]) * yc)


def _gated_merge(gates, gb, ya, yb, yc, name, gcol=0):
    def fn(tv, fv):
        return [_gmerge(tv[0], tv[1], tv[2], fv[0], tv[3], tv[4], tv[5])], []
    return _rowwise(fn, [(gates, D, gcol), (gates, D, gcol + 1), (gates, D, gcol + 2), ya, yb, yc], [gb], [(D, _ACT)], [],
                    name=name)[0]


def _gated_merge_bwd(gates, gb, ya, yb, yc, dm, name, gcol=0):
    def fn(tv, fv):
        _, vjp = jax.vjp(_gmerge, tv[0], tv[1], tv[2], fv[0], tv[3], tv[4], tv[5])
        d0, d1, d2, dgb, da, db, dc = vjp(tv[6])
        return [[d0, d1, d2], da, db, dc], [dgb]
    return _rowwise(fn, [(gates, D, gcol), (gates, D, gcol + 1), (gates, D, gcol + 2), ya, yb, yc, dm], [gb],
                    [(3 * D, _ACT), (D, _ACT), (D, _ACT), (D, _ACT)], [(1, 3 * D)], name=name)


def _pool_affine(t1, pb, ps, dout, name):
    if dout is None:
        def fn(tv, fv):
            return [(tv[0] + fv[0]) * fv[1]], []
        return _rowwise(fn, [t1], [pb, ps], [(POOLW, _ACT)], [], name=name)[0]

    def fnb(tv, fv):
        t2, vjp = jax.vjp(lambda t, b, s: (t + b) * s, tv[0], fv[0], fv[1])
        dt, db, dsc = vjp(tv[1])
        return [dt, t2], [db, dsc]
    return _rowwise(fnb, [t1, dout], [pb, ps], [(POOLW, _ACT), (POOLW, _ACT)], [(1, POOLW), (1, POOLW)], name=name)


def _dt_softplus(dt_raw, dt_bias, ddt, name):
    f = lambda r, b: _softplus(r + b)
    if ddt is None:
        def fn(tv, fv):
            return [f(tv[0], fv[0])], []
        return _rowwise(fn, [dt_raw], [dt_bias], [(16, F32)], [], name=name, tm=1024)[0]

    def fnb(tv, fv):
        _, vjp = jax.vjp(f, tv[0], fv[0])
        dr, db = vjp(tv[1])
        return [dr], [db]
    return _rowwise(fnb, [dt_raw, ddt], [dt_bias], [(16, F32)], [(1, 16)], name=name, tm=1024)


def _adamw_math(wv, gv, mv, vv):
    c1 = 1.0 / (1.0 - ADAM_B1 ** ADAM_STEP)
    c2 = 1.0 / (1.0 - ADAM_B2 ** ADAM_STEP)
    mn = ADAM_B1 * mv + (1.0 - ADAM_B1) * gv
    vn = ADAM_B2 * vv + (1.0 - ADAM_B2) * (gv * gv)
    delta = -ADAM_LR * ((mn * c1) / (jnp.sqrt(vn * c2) + ADAM_EPS) + ADAM_WD * wv)
    return delta, mn, vn


def _adamw(w, g, m, v, name):
    R, C = w.shape
    tm = _pick(R, (256, 128, 64, 32, 16, 8))
    return _rowwise(lambda tv, fv: (list(_adamw_math(*tv)), []), [w, g, m, v], [], [(C, F32)] * 3, [], name=name, tm=tm)


def _adamw_layer(i, w, g, m, v, accs, name):
    R, C = w.shape
    r = R // NL
    tm = _pick(r, (256, 128, 64, 32, 16, 8))
    nt = r // tm
    if accs is None:
        accs = [lax.empty((R, C), F32) for _ in range(4)]

    def body(w_ref, g_ref, m_ref, v_ref, a0, a1, a2, a3, go_ref, do_ref, mo_ref, vo_ref):
        gv = g_ref[...]
        delta, mn, vn = _adamw_math(w_ref[...], gv, m_ref[...], v_ref[...])
        go_ref[...] = gv
        do_ref[...] = delta
        mo_ref[...] = mn
        vo_ref[...] = vn

    slab = pl.BlockSpec((tm, C), lambda t: (i * nt + t, 0))
    anyspec = pl.BlockSpec(memory_space=pl.ANY)
    return _pcall(body, name=name, grid=(nt,),
                  in_specs=[slab, pl.BlockSpec((tm, C), lambda t: (t, 0)), slab, slab] + [anyspec] * 4,
                  out_specs=[slab] * 4, out_shape=[jax.ShapeDtypeStruct((R, C), F32)] * 4,
                  input_output_aliases={4 + k: k for k in range(4)},
                  compiler_params=_cp(("parallel",)))(w, g, m, v, *accs)


def _ffn_fwd(x, xm, w13, w2, g, b, tag, dep=None):
    h = _mm(xm, w13, dep=dep, name=f"{tag}_h")
    s = _swiglu_act(h, name=f"{tag}_act")
    y = _mm(s, w2, name=f"{tag}_y")
    r, out, outm = _res_ln_fwd(x, y, g, b, 0.5, name=f"{tag}_ln")
    return out, outm, dict(x=xm, h=h, r=r)


def _ffn_bwd(dout, sv, w13, w2, g, b, tag, dep=None):
    dskip, dy, dg, db = _ln_bwd(sv['r'], g, b, dout, 0.5, name=f"{tag}_lnb")
    ds = _mm(dy, w2, tb=True, dep=dep, name=f"{tag}_ds")
    dh, s = _swiglu_act_bwd(sv['h'], ds, name=f"{tag}_actb")
    dw2 = _mm(s, dy, ta=True, name=f"{tag}_dw2")
    dw13 = _mm(sv['x'], dh, ta=True, name=f"{tag}_dw13")
    dx = _mm(dh, w13, tb=True, add=dskip, name=f"{tag}_dx")
    return dx, dict(w13=dw13, w2=dw2, g=dg, b=db)


def _mixer_fwd(x1, x1m, W, bias_all, tag, dep=None):
    T = x1.shape[0]
    hcat = _mm(x1m, W['w_in_r'], dep=dep, name=f"{tag}_hcat")
    dt_raw = hcat[:, O_DT:O_DT + 16]
    pooled = _pool_mean(hcat, False, name=f"{tag}_pool", col0=O_U // 128)
    t1 = _mm(pooled, W['pool_wbd'], name=f"{tag}_pt1")
    t2 = _pool_affine(t1, W['pool_b'], W['pool_scale'], None, name=f"{tag}_paff")
    ya = _mm(t2, W['p_pool'], name=f"{tag}_ya")
    act = _conv_silu(hcat, W['conv_w'], W['conv_b'], name=f"{tag}_conv", col0=O_XBC // 128)
    dt = _dt_softplus(dt_raw, W['dt_bias'], None, name=f"{tag}_dt")
    dtg = dt.reshape(T, 4, 4).transpose(1, 0, 2)
    yscan, hs = _ssd_fwd(act, dtg, W['a_neg'], W['d_skip'], name=f"{tag}_ssd")
    ybn = _ssd_gate_norm(yscan, hcat, W['ssd_norm'], name=f"{tag}_gn", zcol=O_Z // 256)
    yb = _mm(ybn, W['p_ssd'], name=f"{tag}_yb")
    outs, lses = [], []
    for gi in range(len(ATTN_DILS)):
        o, l = _attn_fwd(hcat, bias_all, gi, name=f"{tag}_attn{gi}")
        outs.append(o)
        lses.append(l)
    ycp = _attn_merge(outs, lses, None, name=f"{tag}_amerge")
    yc = _mm(ycp, W['p_attn'], name=f"{tag}_yc")
    merged = _gated_merge(hcat, W['gate_b'], ya, yb, yc, name=f"{tag}_gm", gcol=O_G // D)
    mix = _mm(merged, W['w_out'], name=f"{tag}_mix")
    r, out, outm = _res_ln_fwd(x1, mix, W['ln2_g'], W['ln2_b'], 1.0, name=f"{tag}_ln")
    sv = dict(x1=x1m, dt_raw=dt_raw, pooled=pooled, t1=t1, act=act, dtg=dtg,
              hs=hs, yscan=yscan, ybn=ybn, hcat=hcat, outs=outs, lses=lses, ycp=ycp, ya=ya, yb=yb, yc=yc,
              merged=merged, r=r)
    return out, outm, sv


def _mixer_bwd(dout, sv, W, bias_all, tag, dep=None):
    T = dout.shape[0]
    gr = {}
    dx1a, dr, gr['ln2_g'], gr['ln2_b'] = _ln_bwd(sv['r'], W['ln2_g'], W['ln2_b'], dout, 1.0, name=f"{tag}_lnb")
    dmerged = _mm(dr, W['w_out'], tb=True, dep=dep, name=f"{tag}_dmerged")
    gr['w_out'] = _mm(sv['merged'], dr, ta=True, name=f"{tag}_dwout")
    dgates, dya, dyb, dyc, gr['gate_b'] = _gated_merge_bwd(sv['hcat'], W['gate_b'], sv['ya'], sv['yb'], sv['yc'],
                                                           dmerged, name=f"{tag}_gmb", gcol=O_G // D)
    dycp = _mm(dyc, W['p_attn'], tb=True, name=f"{tag}_dycp")
    gr['p_attn'] = _mm(sv['ycp'], dyc, ta=True, name=f"{tag}_dpattn")
    dml = _attn_merge(sv['outs'], sv['lses'], dycp, name=f"{tag}_amergeb")
    dq, dk, dv, dbias = [], [], [], []
    for gi in range(len(ATTN_DILS)):
        a, b, c, d = _attn_bwd(sv['hcat'], bias_all, gi, dml[gi], dml[3 + gi], name=f"{tag}_attnb{gi}")
        dq.append(a)
        dk.append(b)
        dv.append(c)
        dbias.append(d)
    dbias = jnp.concatenate(dbias, axis=0)
    dybn = _mm(dyb, W['p_ssd'], tb=True, name=f"{tag}_dybn")
    gr['p_ssd'] = _mm(sv['ybn'], dyb, ta=True, name=f"{tag}_dpssd")
    dyscan, dz, gr['ssd_norm'] = _ssd_gate_norm_bwd(sv['yscan'], sv['hcat'], W['ssd_norm'], dybn, name=f"{tag}_gnb",
                                                    zcol=O_Z // 256)
    dxs, ddtg, dbm, dcm, dak = _ssd_bwd(sv['act'], sv['dtg'], W['a_neg'], W['d_skip'], sv['hs'], dyscan,
                                        name=f"{tag}_ssdb")
    gr['a_neg'], gr['d_skip'] = dak[:, 0, 0:4], dak[:, 1, 0:4]
    ddt = ddtg.transpose(1, 0, 2).reshape(T, 16)
    ddt_raw, gr['dt_bias'] = _dt_softplus(sv['dt_raw'], W['dt_bias'], ddt, name=f"{tag}_dtb")
    dact = jnp.concatenate([dxs, dbm, dcm], axis=1)
    dxbc, gr['conv_w'], gr['conv_b'] = _conv_silu_bwd(sv['hcat'], W['conv_w'], W['conv_b'], dact, name=f"{tag}_convb",
                                                      col0=O_XBC // 128)
    dt2 = _mm(dya, W['p_pool'], tb=True, name=f"{tag}_dt2")
    dt1, t2, gr['pool_b'], gr['pool_scale'] = _pool_affine(sv['t1'], W['pool_b'], W['pool_scale'], dt2, name=f"{tag}_paffb")
    gr['p_pool'] = _mm(t2, dya, ta=True, name=f"{tag}_dppool")
    dpooled = _mm(dt1, W['pool_wbd'], tb=True, name=f"{tag}_dpooled")
    gr['pool_wbd'] = _mm(sv['pooled'], dt1, ta=True, name=f"{tag}_dpoolw")
    du = _pool_mean(dpooled, True, name=f"{tag}_poolb")
    dhcat = jnp.concatenate([t.astype(_ACT) for t in [du, dz, dxbc] + dq + dk + dv + [dgates, ddt_raw]]
                            + [jnp.zeros((T, HC - O_DT - 16), _ACT)], axis=1)
    dx1 = _mm(dhcat, W['w_in_r'], tb=True, add=dx1a, name=f"{tag}_dx1")
    gr['w_in_r'] = _mm(sv['x1'], dhcat, ta=True, name=f"{tag}_dwin")
    return dx1, gr, dbias


def _prep_layer_weights(i, inp, G):
    W = {}
    for n in BIG:
        if n not in G:
            continue
        g = G[n]
        if n == 'w_in':
            W['w_in_r'] = jnp.concatenate(_nat_pieces(g, 0, 3840) + _nat_pieces(g, 3856, 9232) + _nat_pieces(g, 3840, 3856)
                                          + [jnp.zeros((D, HC - 9232), g.dtype)], axis=1)
        elif n in COL_SHARDED:
            W[n] = jnp.concatenate([g[j] for j in range(4)], axis=1)
        else:
            W[n] = g.reshape(4 * g.shape[1], g.shape[2])
    pw = inp['pool_w'][i].astype(_MXU)
    wbd = jnp.zeros((POOLW, POOLW), _MXU)
    for g in range(4):
        wbd = lax.dynamic_update_slice(wbd, pw[g], (g * POOL_GDIM, g * POOL_GDIM))
    W['pool_wbd'] = wbd
    W['pool_b'] = inp['pool_b'][i].reshape(1, POOLW)
    W['pool_scale'] = inp['pool_scale'][i].reshape(1, POOLW)
    if 'conv_w' in G:
        W['conv_w'] = jnp.concatenate([G['conv_w'][j] for j in range(4)], axis=1)
        W['gate_b'] = jnp.concatenate([G['gate_b'][j][b:b + 1] for b in range(3) for j in range(4)], axis=1)
    W['conv_b'] = inp['conv_b'][i].reshape(1, 2048)
    W['dt_bias'] = inp['dt_bias'][i].reshape(1, 16)
    W['a_neg'] = (-jnp.exp(inp['a_log'][i])).reshape(4, 1, 4)
    W['d_skip'] = inp['d_skip'][i].reshape(4, 1, 4)
    W['ssd_norm'] = inp['ssd_norm'][i].reshape(1, D)
    for n in ('ln1_g', 'ln1_b', 'ln2_g', 'ln2_b', 'ln3_g', 'ln3_b'):
        W[n] = inp[n][i].reshape(1, D)
    return W


GATHER_FIRST = ['ffn1_w13', 'ffn1_w2']
GATHER_REST = [n for n in BIG if n not in GATHER_FIRST] + ['gate_b', 'conv_w']


def _gather_start(inp, i, names):
    core = lax.axis_index("c")
    arrs = []
    for n in names:
        s = inp[n][i]
        if n in BIG:
            s = lax.dynamic_slice_in_dim(s, core * (s.shape[0] // 2), s.shape[0] // 2, axis=0).astype(BF16)
        arrs.append(s)
    state, token = _exchange_start(arrs, "chips", "gather", name="gather_start")
    return (names, state), token


def _gather_mid(handle, after):
    names, state = handle
    me = 2 * lax.axis_index("x") + lax.axis_index("y")
    own, outs = _exchange_wait(state, after, "chips", "gather", name="gather_wait")
    outs = [lax.dynamic_update_slice(o, a[None], (me, 0, 0)) for o, a in zip(outs, own)]
    big = [o for n, o in zip(names, outs) if n in BIG]
    state, token = _exchange_start(big, "cores", "gather", name="share_start")
    return (names, outs, state), token


def _gather_finish(handle, after):
    names, outs, state = handle
    core = lax.axis_index("c")
    mine, theirs = _exchange_wait(state, after, "cores", "gather", name="share_wait")
    G = {n: o for n, o in zip(names, outs) if n not in BIG}
    for n, a, b in zip([n for n in names if n in BIG], mine, theirs):
        G[n] = jnp.concatenate([jnp.where(core == 0, a, b), jnp.where(core == 0, b, a)], axis=1)
    return G


W_IN_SHARD = 2308


def _nat_pieces(g, lo, hi):
    out = []
    for j in range(4):
        s, e = max(lo, W_IN_SHARD * j), min(hi, W_IN_SHARD * (j + 1))
        if s < e:
            out.append(g[j][:, s - W_IN_SHARD * j:e - W_IN_SHARD * j])
    return out


def _reord_ranges(lo, hi):
    out = []
    for a, b, off in ((0, 3840, 0), (3840, 3856, O_DT - 3840), (3856, 9232, -16)):
        s, e = max(lo, a), min(hi, b)
        if s < e:
            out.append((s + off, e + off))
    return out


def _halves_of(n, g):
    if n == 'w_in':
        shards = [jnp.concatenate([g[:, a:b] for a, b in _reord_ranges(W_IN_SHARD * j, W_IN_SHARD * (j + 1))], axis=1)
                  for j in range(4)]
    elif n in COL_SHARDED:
        c = g.shape[1] // 4
        shards = [g[:, j * c:(j + 1) * c] for j in range(4)]
    else:
        r = g.shape[0] // 4
        shards = [g[j * r:(j + 1) * r] for j in range(4)]
    r2 = shards[0].shape[0] // 2
    return jnp.stack([jnp.concatenate([s[h * r2:(h + 1) * r2] for s in shards], axis=0) for h in range(2)])


def _reduce_a(grads):
    names = list(grads)
    halves = [_halves_of(n, grads[n]) for n in names]
    state, token = _exchange_start(halves, "cores", "scatter", name="rsc_start")
    return (names, state), token


def _reduce_b(handle, after):
    names, state = handle
    core = lax.axis_index("c").reshape(1)
    halves, got = _exchange_wait(state, after, "cores", "scatter", name="rsc_wait")
    chip = [_sum_own_recv(h, t, core, BF16, name="rs_sum2") for h, t in zip(halves, got)]
    chip = [t.reshape(4, t.shape[0] // 4, t.shape[1]) for t in chip]
    state, token = _exchange_start(chip, "chips", "scatter", name="rs_start")
    return (names, state), token


def _reduce_c(handle, after):
    names, state = handle
    chip_id = (2 * lax.axis_index("x") + lax.axis_index("y")).reshape(1)
    chip, got = _exchange_wait(state, after, "chips", "scatter", name="rs_wait")
    red = [_sum_own_recv(h, t, chip_id, F32, name="rs_sum4") for h, t in zip(chip, got)]
    other = _exchange(red, "cores", "gather", name="rs_share")
    out = {}
    for n, mine, theirs in zip(names, red, other):
        out[n] = jnp.where(lax.axis_index("c") == 0, jnp.concatenate([mine, theirs]), jnp.concatenate([theirs, mine]))
    return out


class _Comm:
    def __init__(self, inp):
        self.inp = inp

    def gather_start(self, i, names):
        return _gather_start(self.inp, i, names)

    gather_mid = staticmethod(_gather_mid)
    gather_finish = staticmethod(_gather_finish)

    def reduce_a(self, i, grads):
        return _reduce_a({n: grads[n] for n in BIG})

    reduce_b = staticmethod(_reduce_b)
    reduce_c = staticmethod(_reduce_c)


def _allreduce_small(vec, dep=None):
    for group in ("cores", "x", "y"):
        recv = _exchange([vec], group, "gather", name=f"ar_{group}", dep=dep if group == "cores" else None)[0]
        vec = _rowwise(lambda tv, fv: ([tv[0] + tv[1]], []), [vec, recv], [], [(128, F32)], [], name=f"ar_add_{group}")[0]
    return vec


def _pack(arrs):
    flat = jnp.concatenate([a.reshape(-1) for a in arrs])
    n = flat.shape[0]
    pad = (-n) % (256 * 128)
    flat = jnp.concatenate([flat, jnp.zeros((pad,), F32)])
    return flat.reshape(-1, 128)


def _unpack(p, shapes):
    flat = p.reshape(-1)
    out, off = [], 0
    for s in shapes:
        sz = int(np.prod(s))
        out.append(flat[off:off + sz].reshape(s))
        off += sz
    return out


def _forward_backward(inp, comm, bias_all):
    x = xm = inp['x'].reshape(-1, D)
    tgt = inp['loss_target'].reshape(-1, D)
    saved, Ws = [], []
    h_first, _ = comm.gather_start(0, GATHER_FIRST)
    h_rest, dep = comm.gather_start(0, GATHER_REST)
    h_first, tok = comm.gather_mid(h_first, x)
    G = comm.gather_finish(h_first, tok)
    for i in range(NL):
        W = _prep_layer_weights(i, inp, G)
        start_next = lambda: (comm.gather_start(i + 1, BIG + ['gate_b', 'conv_w']) if i + 1 < NL else (None, None))
        if i > 0:
            h_next, dep = start_next()
        x1, x1m, s1 = _ffn_fwd(x, xm, W['ffn1_w13'], W['ffn1_w2'], W['ln1_g'], W['ln1_b'], "f1", dep)
        if i == 0:
            h_rest, tok = comm.gather_mid(h_rest, x1m)
            W.update(_prep_layer_weights(i, inp, comm.gather_finish(h_rest, tok)))
            h_next, dep = start_next()
        x2, x2m, s2 = _mixer_fwd(x1, x1m, W, bias_all, "mx", dep if i == 0 else None)
        dep = None
        if h_next is not None:
            h_next, dep = comm.gather_mid(h_next, x2m)
        x, xm, s3 = _ffn_fwd(x2, x2m, W['ffn2_w13'], W['ffn2_w2'], W['ln3_g'], W['ln3_b'], "f2", dep)
        if h_next is not None:
            G = comm.gather_finish(h_next, xm)
        saved.append((s1, s2, s3))
        Ws.append(W)
    dy, lpart = _loss_fwd_bwd(x, tgt, name="loss")
    fins, reduced, dbiases = [None] * NL, [None] * NL, [None] * NL
    pend_a, pend_b, dep = None, None, None
    for i in reversed(range(NL)):
        W = Ws[i]
        s1, s2, s3 = saved[i]
        g = {}
        dx2, f = _ffn_bwd(dy, s3, W['ffn2_w13'], W['ffn2_w2'], W['ln3_g'], W['ln3_b'], "f2", dep)
        g['ffn2_w13'], g['ffn2_w2'], g['ln3_g'], g['ln3_b'] = f['w13'], f['w2'], f['g'], f['b']
        dep = None
        if pend_a is not None:
            handle, dep = comm.reduce_b(pend_a[1], dx2)
            pend_b = (pend_a[0], handle)
        dx1, gm, dbiases[i] = _mixer_bwd(dx2, s2, W, bias_all, "mx", dep)
        g.update(gm)
        dy, f = _ffn_bwd(dx1, s1, W['ffn1_w13'], W['ffn1_w2'], W['ln1_g'], W['ln1_b'], "f1")
        g['ffn1_w13'], g['ffn1_w2'], g['ln1_g'], g['ln1_b'] = f['w13'], f['w2'], f['g'], f['b']
        fins[i] = _finish_layer_grads(i, g, inp)
        if pend_b is not None:
            reduced[pend_b[0]] = comm.reduce_c(pend_b[1], dy)
            pend_b = None
        handle, dep = comm.reduce_a(i, fins[i])
        pend_a = (i, handle)
    return lpart, dy, fins, reduced, pend_a, dbiases


def _finish_layer_grads(i, g, inp):
    out = {n: g[n] for n in BIG if n != 'w_in'}
    out['w_in'] = g['w_in_r']
    out['pool_w'] = jnp.stack([g['pool_wbd'][k * POOL_GDIM:(k + 1) * POOL_GDIM, k * POOL_GDIM:(k + 1) * POOL_GDIM] for k in range(4)])
    out['pool_b'] = g['pool_b'].reshape(4, POOL_GDIM)
    out['pool_scale'] = g['pool_scale'].reshape(POOLW)
    out['conv_w'] = g['conv_w']
    out['conv_b'] = g['conv_b'].reshape(2048)
    out['dt_bias'] = g['dt_bias'].reshape(16)
    out['a_log'] = (g['a_neg'].reshape(16)) * (-jnp.exp(inp['a_log'][i]))
    out['d_skip'] = g['d_skip'].reshape(16)
    out['ssd_norm'] = g['ssd_norm'].reshape(D)
    out['gate_b'] = g['gate_b'].reshape(3, D)
    for n in ('ln1_g', 'ln1_b', 'ln2_g', 'ln2_b', 'ln3_g', 'ln3_b'):
        out[n] = g[n].reshape(D)
    return out


def kernel(x, ffn1_w13, ffn1_w2, ln1_g, ln1_b, w_in, gate_b, pool_w, pool_b, pool_scale, conv_w, conv_b,
           dt_bias, a_log, d_skip, ssd_norm, rel_bias, p_pool, p_ssd, p_attn, w_out, ln2_g, ln2_b, ffn2_w13,
           ffn2_w2, ln3_g, ln3_b, loss_target, m_ffn1_w13, m_ffn1_w2, m_ln1_g, m_ln1_b, m_w_in, m_gate_b,
           m_pool_w, m_pool_b, m_pool_scale, m_conv_w, m_conv_b, m_dt_bias, m_a_log, m_d_skip, m_ssd_norm,
           m_rel_bias, m_p_pool, m_p_ssd, m_p_attn, m_w_out, m_ln2_g, m_ln2_b, m_ffn2_w13, m_ffn2_w2, m_ln3_g,
           m_ln3_b, v_ffn1_w13, v_ffn1_w2, v_ln1_g, v_ln1_b, v_w_in, v_gate_b, v_pool_w, v_pool_b,
           v_pool_scale, v_conv_w, v_conv_b, v_dt_bias, v_a_log, v_d_skip, v_ssd_norm, v_rel_bias, v_p_pool,
           v_p_ssd, v_p_attn, v_w_out, v_ln2_g, v_ln2_b, v_ffn2_w13, v_ffn2_w2, v_ln3_g, v_ln3_b):
    inp = dict(locals())
    maps = jnp.asarray(_bucket_maps())
    bias_all = _bias_build(rel_bias, maps, name="bias_build")
    comm = _Comm(inp)
    lpart, gx, fins, red, pending, dbiases = _forward_backward(inp, comm, bias_all)
    loss = lax.psum(lpart[0, 0], ("x", "y", "c"))

    small_l = [n for n in SMALL if n != 'rel_bias']
    drel = _bias_reduce(jnp.stack(dbiases), maps, name="bias_reduce")[:, 0, :32].T
    handle_b, started = comm.reduce_b(pending[1], drel)
    small_arrs = [jnp.stack([fins[i][n] for i in range(NL)]) for n in small_l] + [drel]
    packed = _allreduce_small(_pack(small_arrs), dep=started)
    gsmall = dict(zip(small_l + ['rel_bias'], _unpack(packed, [a.shape for a in small_arrs])))
    shard = 2 * lax.axis_index("x") + lax.axis_index("y")
    gsmall['gate_b'] = lax.dynamic_slice_in_dim(gsmall['gate_b'], shard * 256, 256, axis=2)
    gsmall['conv_w'] = lax.dynamic_slice_in_dim(gsmall['conv_w'], shard * 512, 512, axis=2)
    gout, delta, new_m, new_v = dict(gsmall), {}, {}, {}
    shapes = [inp[n].shape for n in SMALL]
    d, m, v = _adamw(_pack([inp[n] for n in SMALL]), _pack([gsmall[n] for n in SMALL]),
                     _pack([inp['m_' + n] for n in SMALL]), _pack([inp['v_' + n] for n in SMALL]), name="adamw_small")
    for n, dd, mm, vv in zip(SMALL, _unpack(d, shapes), _unpack(m, shapes), _unpack(v, shapes)):
        delta[n], new_m[n], new_v[n] = dd, mm, vv

    two_d = lambda a: a.reshape(a.shape[0] * a.shape[1], a.shape[2])
    accs = {n: None for n in BIG}

    def adamw_layer(i):
        for n in BIG:
            accs[n] = _adamw_layer(i, two_d(inp[n]), red[i][n], two_d(inp['m_' + n]), two_d(inp['v_' + n]), accs[n],
                                   name="adamw_big")

    done = [i for i in range(NL) if i != pending[0]]
    for i in done:
        adamw_layer(i)
    red[pending[0]] = comm.reduce_c(handle_b, [d] + ([accs[n][1] for n in BIG] if done else []))
    adamw_layer(pending[0])
    for n in BIG:
        gout[n], delta[n], new_m[n], new_v[n] = [a.reshape(inp[n].shape) for a in accs[n]]

    return (loss, gx.reshape(x.shape), *[gout[n] for n in WEIGHTS], *[delta[n] for n in WEIGHTS],
            *[new_m[n] for n in WEIGHTS], *[new_v[n] for n in WEIGHTS])
```

```python
import functools

import numpy as np
import jax
import jax.numpy as jnp
from jax import lax
from jax.experimental import pallas as pl
from jax.experimental.pallas import tpu as pltpu

F32 = jnp.float32
BF16 = jnp.bfloat16
_MXU = jnp.bfloat16
_ACT = jnp.bfloat16
_VMEM_LIMIT = 56 * 1024 * 1024

S = 2048
D = 1024
NL = 4
DFF = 2816
LN_EPS = 1e-5
SSD_EPS = 1e-5
ALPHA = (2.0 * NL) ** 0.25
POOLW = 768
POOL_WINDOWS = (2, 4, 8, 16)
POOL_GDIM = 192
CH = 128
ATTN_DILS = (1, 4, 16)
HC = 9728
O_U, O_Z, O_XBC, O_Q, O_K, O_V, O_G, O_DT = 0, 768, 1792, 3840, 4608, 5376, 6144, 9216

ADAM_LR, ADAM_B1, ADAM_B2, ADAM_EPS, ADAM_WD, ADAM_STEP = 0.001, 0.9, 0.999, 1e-08, 0.01, 10

WEIGHTS = ['ffn1_w13', 'ffn1_w2', 'ln1_g', 'ln1_b', 'w_in', 'gate_b', 'pool_w', 'pool_b', 'pool_scale', 'conv_w',
           'conv_b', 'dt_bias', 'a_log', 'd_skip', 'ssd_norm', 'rel_bias', 'p_pool', 'p_ssd', 'p_attn', 'w_out',
           'ln2_g', 'ln2_b', 'ffn2_w13', 'ffn2_w2', 'ln3_g', 'ln3_b']
BIG = ['ffn1_w13', 'ffn1_w2', 'w_in', 'p_pool', 'p_ssd', 'p_attn', 'w_out', 'ffn2_w13', 'ffn2_w2']
COL_SHARDED = {'ffn1_w13', 'ffn2_w13', 'w_in', 'p_pool', 'p_attn'}
SMALL = [n for n in WEIGHTS if n not in BIG]


def _pcall(body, **kw):
    return pl.pallas_call(body, **kw)


def _cp(sem=None):
    return pltpu.CompilerParams(dimension_semantics=sem, vmem_limit_bytes=_VMEM_LIMIT)


def _pick(n, cands):
    for c in cands:
        if n % c == 0:
            return c
    raise ValueError(f"no tile for {n}")


def _mm(a, b, *, ta=False, tb=False, add=None, out_dtype=F32, dep=None, name):
    if ta:
        K, M = a.shape
    else:
        M, K = a.shape
    if tb:
        N, K2 = b.shape
    else:
        K2, N = b.shape
    assert K == K2, (a.shape, b.shape, ta, tb)
    sa, sb, so = a.dtype.itemsize, b.dtype.itemsize, jnp.dtype(out_dtype).itemsize
    tm, tn, tk = _mm_tiles(M, N, K, sa, sb, so + (4 if add is not None else 0))
    nk = K // tk
    a_bytes, b_bytes = M * K * sa, K * N * sb
    j_outer = nk == 1 and (b_bytes + a_bytes * (N // tn) < a_bytes + b_bytes * (M // tm))
    ij = (lambda p, q: (q, p)) if j_outer else (lambda p, q: (p, q))

    def im(f):
        return lambda p, q, k: f(*ij(p, q), k)

    a_spec = pl.BlockSpec((tk, tm), im(lambda i, j, k: (k, i))) if ta else pl.BlockSpec((tm, tk), im(lambda i, j, k: (i, k)))
    b_spec = pl.BlockSpec((tn, tk), im(lambda i, j, k: (j, k))) if tb else pl.BlockSpec((tk, tn), im(lambda i, j, k: (k, j)))
    o_spec = pl.BlockSpec((tm, tn), im(lambda i, j, k: (i, j)))
    dims = (((0 if ta else 1,), (1 if tb else 0,)), ((), ()))
    has_add = add is not None

    n_in = 2 + int(has_add) + int(dep is not None)

    def body(*refs):
        a_ref, b_ref = refs[0], refs[1]
        add_ref = refs[2] if has_add else None
        o_ref = refs[n_in]
        part = lax.dot_general(a_ref[...].astype(_MXU), b_ref[...].astype(_MXU), dims, preferred_element_type=F32)

        def finish(r):
            if has_add:
                r = r + add_ref[...]
            o_ref[...] = r.astype(out_dtype)

        if nk == 1:
            finish(part)
        else:
            acc = refs[-1]
            k = pl.program_id(2)

            @pl.when(k == 0)
            def _():
                acc[...] = part

            @pl.when(k > 0)
            def _():
                acc[...] += part

            @pl.when(k == nk - 1)
            def _():
                finish(acc[...])

    in_specs = [a_spec, b_spec]
    args = [a, b]
    if has_add:
        in_specs.append(o_spec)
        args.append(add)
    if dep is not None:
        in_specs.append(pl.BlockSpec(memory_space=pl.ANY))
        args.append(dep)
    gm, gn = M // tm, N // tn
    return _pcall(
        body, name=name, grid=((gn, gm, nk) if j_outer else (gm, gn, nk)), in_specs=in_specs, out_specs=o_spec,
        out_shape=jax.ShapeDtypeStruct((M, N), out_dtype),
        scratch_shapes=([pltpu.VMEM((tm, tn), F32)] if nk > 1 else []),
        compiler_params=_cp(("parallel", "parallel", "arbitrary")),
    )(*args)


_MM_VMEM_BUDGET = 40 * 1024 * 1024


def _divisors128(n, cap):
    return [d for d in range(128, min(n, cap) + 1, 128) if n % d == 0][::-1]


_MM_CYC_PER_MMAC = 4.35
_MM_CYC_PER_ACC_VREG = 2.03
_MM_HBM_BYTES_PER_CYC = 1455.0
_MM_CYC_PER_STEP = 770.0


def _mm_tiles(M, N, K, sa, sb, so):
    best = None
    for tm in _divisors128(M, 1408):
        for tn in _divisors128(N, 2560):
            for tk in ([K] if K <= 4096 else []) + _divisors128(K, 2816):
                nk = K // tk
                need = 2 * (tm * tk * sa + tk * tn * sb + tm * tn * so) + (tm * tn * 4 if nk > 1 else 0)
                need += tm * tk * 2 + tk * tn * 2 + tm * tn * 4
                if need > _MM_VMEM_BUDGET:
                    continue
                gm, gn = M // tm, N // tn
                a_bytes, b_bytes = M * K * sa, K * N * sb
                hbm = min(b_bytes + a_bytes * gn, a_bytes + b_bytes * gm) if nk == 1 else a_bytes * gn + b_bytes * gm
                hbm += M * N * so
                work = _MM_CYC_PER_MMAC * M * N * K / 1e6 + _MM_CYC_PER_ACC_VREG * (M * N / 1024) * (nk if nk > 1 else 0.5)
                cost = max(work, hbm / _MM_HBM_BYTES_PER_CYC) + gm * gn * nk * _MM_CYC_PER_STEP
                if best is None or cost < best[0]:
                    best = (cost, (tm, tn, tk))
    assert best is not None, (M, N, K)
    return best[1]


def _store(ref, val):
    if isinstance(val, (list, tuple)):
        off = 0
        for p in val:
            w = p.shape[1]
            ref[:, off:off + w] = p.astype(ref.dtype)
            off += w
    else:
        ref[...] = val.astype(ref.dtype)


def _acc_store(ref, val, first):
    pieces = val if isinstance(val, (list, tuple)) else [val]
    off = 0
    for p in pieces:
        w = p.shape[1]

        @pl.when(first)
        def _(p=p, off=off, w=w):
            ref[:, off:off + w] = p

        @pl.when(jnp.logical_not(first))
        def _(p=p, off=off, w=w):
            ref[:, off:off + w] += p

        off += w


def _rowwise(fn, tiled, full, out_tiled, out_acc, *, name, tm=256):
    arrs, specs = [], []
    for t in tiled:
        arr, w, cb = t if isinstance(t, tuple) else (t, t.shape[1], 0)
        arrs.append(arr)
        specs.append(pl.BlockSpec((tm, w), functools.partial(lambda i, cb: (i, cb), cb=cb)))
    R = arrs[0].shape[0]
    assert R % tm == 0
    for f in full:
        arrs.append(f)
        specs.append(pl.BlockSpec(f.shape, functools.partial(lambda i, nd: (0,) * nd, nd=f.ndim)))
    nt, nf, no = len(tiled), len(full), len(out_tiled)

    def body(*refs):
        tv = [r[...] for r in refs[:nt]]
        fv = [r[...] for r in refs[nt:nt + nf]]
        ot, oa = fn(tv, fv)
        for r, v in zip(refs[nt + nf:nt + nf + no], ot):
            _store(r, v)
        first = pl.program_id(0) == 0
        for r, v in zip(refs[nt + nf + no:], oa):
            _acc_store(r, v, first)

    out_shape = [jax.ShapeDtypeStruct((R, c), dt) for c, dt in out_tiled]
    out_specs = [pl.BlockSpec((tm, c), lambda i: (i, 0)) for c, _ in out_tiled]
    for shp in out_acc:
        out_shape.append(jax.ShapeDtypeStruct(shp, F32))
        out_specs.append(pl.BlockSpec(shp, lambda i: (0, 0)))
    return _pcall(body, name=name, grid=(R // tm,), in_specs=specs, out_specs=out_specs, out_shape=out_shape,
                  compiler_params=_cp(("arbitrary",)))(*arrs)


def _group(group):
    x, y, c = lax.axis_index("x"), lax.axis_index("y"), lax.axis_index("c")
    if group == "chips":
        return 2 * x + y, [((x, 1 - y, c), 2 * x + 1 - y), ((1 - x, y, c), 2 * (1 - x) + y),
                           ((1 - x, 1 - y, c), 2 * (1 - x) + 1 - y)]
    if group == "cores":
        return c, [((x, y, 1 - c), 1 - c)]
    if group == "x":
        return x, [((1 - x, y, c), 1 - x)]
    return y, [((x, 1 - y, c), 1 - y)]


def _exchange(arrs, group, mode, name, dep=None):
    chips = group == "chips"
    k = len(arrs)
    npeer = 3 if chips else 1

    def body(*refs):
        nd = 0 if dep is None else 1
        ins, outs = refs[:k], refs[k + nd:2 * k + nd]
        send_sems, recv_sems = refs[2 * k + nd:]
        me, peers = _group(group)
        remote = []
        for i in range(k):
            for p, (dev, slot) in enumerate(peers):
                src = ins[i].at[slot] if mode == "scatter" else ins[i]
                if not chips:
                    dst = outs[i]
                else:
                    dst = outs[i].at[p] if mode == "scatter" else outs[i].at[me]
                cp = pltpu.make_async_remote_copy(src_ref=src, dst_ref=dst, send_sem=send_sems.at[i, p],
                                                  recv_sem=recv_sems.at[i, p], device_id=dev,
                                                  device_id_type=pl.DeviceIdType.MESH)
                cp.start()
                remote.append(cp)
        for cp in remote:
            cp.wait_recv()
        for cp in remote:
            cp.wait_send()

    def oshape(a):
        piece = a.shape[1:] if mode == "scatter" else a.shape
        if chips:
            piece = ((3,) if mode == "scatter" else (4,)) + piece
        return jax.ShapeDtypeStruct(piece, a.dtype)

    any_spec = pl.BlockSpec(memory_space=pl.ANY)
    extra = [] if dep is None else [dep]
    return _pcall(body, name=name, in_specs=[any_spec] * (k + len(extra)), out_specs=[any_spec] * k,
                  out_shape=[oshape(a) for a in arrs],
                  scratch_shapes=[pltpu.SemaphoreType.DMA((k, npeer)), pltpu.SemaphoreType.DMA((k, npeer))])(*arrs, *extra)


def _split_copies(ins, lands, send_sems, recv_sems, group, mode):
    chips = group == "chips"
    me, peers = _group(group)
    npeer = len(peers)
    out = []
    for i in range(len(ins)):
        for p, (dev, slot) in enumerate(peers):
            src = ins[i].at[slot] if mode == "scatter" else ins[i]
            if not chips:
                dst = lands[i]
            else:
                dst = lands[i].at[p] if mode == "scatter" else lands[i].at[me]
            out.append(pltpu.make_async_remote_copy(src_ref=src, dst_ref=dst, send_sem=send_sems.at[npeer * i + p],
                                                    recv_sem=recv_sems.at[npeer * i + p], device_id=dev,
                                                    device_id_type=pl.DeviceIdType.MESH))
    return out


def _exchange_start(arrs, group, mode, name):
    k = len(arrs)
    chips = group == "chips"
    nsem = (3 if chips else 1) * k
    hbm = pl.BlockSpec(memory_space=pltpu.HBM)
    sem = pl.BlockSpec(memory_space=pltpu.SEMAPHORE)

    def land_shape(a):
        piece = a.shape[1:] if mode == "scatter" else a.shape
        if chips:
            piece = ((3,) if mode == "scatter" else (4,)) + piece
        return piece

    def body(*refs):
        ins, lands = refs[:k], refs[k:2 * k]
        send_sems, recv_sems = refs[2 * k], refs[2 * k + 1]
        token = refs[-1]
        for cp in _split_copies(ins, lands, send_sems, recv_sems, group, mode):
            cp.start()
        token[...] = jnp.zeros_like(token)

    srcs = [pltpu.with_memory_space_constraint(a, pltpu.HBM) for a in arrs]
    lands = [pltpu.with_memory_space_constraint(lax.empty(land_shape(a), a.dtype), pltpu.HBM) for a in arrs]
    out_shape = ([pltpu.SemaphoreType.DMA((nsem,)), pltpu.SemaphoreType.DMA((nsem,))]
                 + [pltpu.HBM(a.shape, a.dtype) for a in arrs] + [pltpu.HBM(land_shape(a), a.dtype) for a in arrs]
                 + [jax.ShapeDtypeStruct((8, 128), F32)])
    outs = _pcall(body, name=name, in_specs=[hbm] * (2 * k),
                  out_specs=[sem, sem] + [hbm] * (2 * k) + [pl.BlockSpec(memory_space=pltpu.VMEM)], out_shape=out_shape,
                  input_output_aliases={i: 2 + i for i in range(2 * k)},
                  compiler_params=pltpu.CompilerParams(has_side_effects=pltpu.SideEffectType.DATAFLOW_SIDE_EFFECTING))(
                      *srcs, *lands)
    return (outs[0], outs[1], list(outs[2:2 + k]), list(outs[2 + k:2 + 2 * k])), outs[-1]


def _exchange_wait(state, after, group, mode, name):
    send_sems, recv_sems, srcs, lands = state
    k = len(srcs)
    after = list(after) if isinstance(after, (list, tuple)) else [after]
    hbm = pl.BlockSpec(memory_space=pltpu.HBM)
    sem = pl.BlockSpec(memory_space=pltpu.SEMAPHORE)

    def body(*refs):
        ins, lnd = refs[:k], refs[k:2 * k]
        send_sems, recv_sems = refs[2 * k], refs[2 * k + 1]
        for cp in _split_copies(ins, lnd, send_sems, recv_sems, group, mode):
            cp.wait_send()
            cp.wait_recv()

    outs = _pcall(body, name=name,
                  in_specs=[hbm] * (2 * k) + [sem, sem] + [pl.BlockSpec(memory_space=pl.ANY)] * len(after),
                  out_specs=[hbm] * (2 * k),
                  out_shape=[pltpu.HBM(a.shape, a.dtype) for a in srcs] + [pltpu.HBM(a.shape, a.dtype) for a in lands],
                  input_output_aliases={i: i for i in range(2 * k)},
                  compiler_params=pltpu.CompilerParams(has_side_effects=pltpu.SideEffectType.DATAFLOW_SIDE_EFFECTING))(
                      *srcs, *lands, send_sems, recv_sems, *after)
    return list(outs[:k]), list(outs[k:])


def _sum_own_recv(own, recv, me, out_dtype, name):
    n, R, C = own.shape
    nr = 1 if recv.ndim == 2 else recv.shape[0]
    tr = _pick(R, (256, 128, 64, 32, 16, 8))

    def body(me_ref, own_ref, *refs):
        o_ref = refs[-1]
        acc = own_ref[...].astype(F32)
        for r in refs[:-1]:
            acc = acc + r[...].astype(F32)
        o_ref[...] = acc.astype(out_dtype)

    specs = [pl.BlockSpec((None, tr, C), lambda i, me_ref: (me_ref[0], i, 0))]
    args = [own]
    if recv.ndim == 2:
        specs.append(pl.BlockSpec((tr, C), lambda i, me_ref: (i, 0)))
        args.append(recv)
    else:
        for p in range(nr):
            specs.append(pl.BlockSpec((None, tr, C), functools.partial(lambda i, me_ref, p: (p, i, 0), p=p)))
            args.append(recv)
    gs = pltpu.PrefetchScalarGridSpec(num_scalar_prefetch=1, grid=(R // tr,), in_specs=specs,
                                      out_specs=pl.BlockSpec((tr, C), lambda i, me_ref: (i, 0)))
    return _pcall(body, name=name, grid_spec=gs, out_shape=jax.ShapeDtypeStruct((R, C), out_dtype),
                  compiler_params=_cp(("parallel",)))(me, *args)


def _silu(x):
    return x * jax.nn.sigmoid(x)


def _ln(r, g, b):
    mu = jnp.mean(r, -1, keepdims=True)
    xc = r - mu
    var = jnp.mean(xc * xc, -1, keepdims=True)
    return xc * lax.rsqrt(var + LN_EPS) * g + b


def _softplus(x):
    return jnp.maximum(x, 0.0) + jnp.log1p(jnp.exp(-jnp.abs(x)))


def _res_ln_fwd(x, y, g, b, res, name):
    def fn(tv, fv):
        r = ALPHA * tv[0] + res * tv[1]
        out = _ln(r, fv[0], fv[1])
        return [r, out, out], []
    return _rowwise(fn, [x, y], [g, b], [(D, F32), (D, F32), (D, _ACT)], [], name=name)


def _ln_bwd(r, g, b, dout, res, name):
    def fn(tv, fv):
        _, vjp = jax.vjp(_ln, tv[0], fv[0], fv[1])
        dr, dg, db = vjp(tv[1])
        return [ALPHA * dr, res * dr], [dg, db]
    return _rowwise(fn, [r, dout], [g, b], [(D, F32), (D, _ACT)], [(1, D), (1, D)], name=name)


def _swiglu_act(h, name):
    def fn(tv, fv):
        return [_silu(tv[0]) * tv[1]], []
    return _rowwise(fn, [(h, DFF, 0), (h, DFF, 1)], [], [(DFF, _ACT)], [], name=name)[0]


def _swiglu_act_bwd(h, ds, name):
    def fn(tv, fv):
        s, vjp = jax.vjp(lambda a, g: _silu(a) * g, tv[0], tv[1])
        da, dg = vjp(tv[2])
        return [[da, dg], s], []
    return _rowwise(fn, [(h, DFF, 0), (h, DFF, 1), ds], [], [(2 * DFF, _ACT), (DFF, _ACT)], [], name=name)


def _loss_fwd_bwd(y, tgt, name):
    def fn(tv, fv):
        e = tv[0] - tv[1]
        row = jnp.sum(e * e, axis=1, keepdims=True)
        tot = jnp.sum(row, axis=0, keepdims=True) * (0.5 / D)
        return [e * (1.0 / D)], [jnp.broadcast_to(tot, (1, 128))]
    return _rowwise(fn, [y, tgt], [], [(D, F32)], [(1, 128)], name=name)


def _shift_down(x, k, row):
    return jnp.where(row >= k, pltpu.roll(x, k, axis=0), 0.0)


def _shift_up(x, k, row):
    n = x.shape[0]
    return jnp.where(row < n - k, pltpu.roll(x, n - k, axis=0), 0.0)


def _pool_window_masks(j):
    lane = lax.broadcasted_iota(jnp.int32, (1, 128), 1) + j * 128
    grp = lane // POOL_GDIM
    return [grp == g for g in range(4)]


def _pool_mean(u, bwd, name, col0=0):
    T = u.shape[0]
    B = T // S

    def body(u_ref, o_ref):
        j = pl.program_id(1)
        x = u_ref[...]
        row = lax.broadcasted_iota(jnp.int32, (S, 1), 0)
        masks = _pool_window_masks(j)
        inv = [1.0 / jnp.minimum(row + 1, w).astype(F32) for w in POOL_WINDOWS]
        if not bwd:
            s2 = x + _shift_down(x, 1, row)
            s4 = s2 + _shift_down(s2, 2, row)
            s8 = s4 + _shift_down(s4, 4, row)
            s16 = s8 + _shift_down(s8, 8, row)
            mean = jnp.where(masks[0], s2 * inv[0], jnp.where(masks[1], s4 * inv[1],
                             jnp.where(masks[2], s8 * inv[2], s16 * inv[3])))
            o_ref[...] = (mean - x).astype(o_ref.dtype)
        else:
            g = [jnp.where(masks[i], x * inv[i], 0.0) for i in range(4)]
            t = g[3]
            t = t + _shift_up(t, 8, row) + g[2]
            t = t + _shift_up(t, 4, row) + g[1]
            t = t + _shift_up(t, 2, row) + g[0]
            t = t + _shift_up(t, 1, row)
            o_ref[...] = (t - x).astype(o_ref.dtype)

    spec = pl.BlockSpec((S, 128), lambda b, j: (b, j))
    return _pcall(body, name=name, grid=(B, POOLW // 128),
                  in_specs=[pl.BlockSpec((S, 128), lambda b, j: (b, j + col0))], out_specs=spec,
                  out_shape=jax.ShapeDtypeStruct((T, POOLW), _ACT), compiler_params=_cp(("parallel", "parallel")))(u)


def _conv_silu(xbc, w, b, name, col0=0):
    T, C = xbc.shape[0], w.shape[1]
    B = T // S

    def body(x_ref, w_ref, b_ref, o_ref):
        x = x_ref[...]
        row = lax.broadcasted_iota(jnp.int32, (S, 1), 0)
        c = b_ref[...] + w_ref[3:4, :] * x
        for s in range(1, 4):
            c = c + w_ref[3 - s:4 - s, :] * _shift_down(x, s, row)
        o_ref[...] = _silu(c)

    return _pcall(body, name=name, grid=(B, C // 128),
                  in_specs=[pl.BlockSpec((S, 128), lambda b, j: (b, j + col0)), pl.BlockSpec((4, 128), lambda b, j: (0, j)),
                            pl.BlockSpec((1, 128), lambda b, j: (0, j))],
                  out_specs=pl.BlockSpec((S, 128), lambda b, j: (b, j)),
                  out_shape=jax.ShapeDtypeStruct((T, C), F32), compiler_params=_cp(("parallel", "parallel")))(xbc, w, b)


def _conv_silu_bwd(xbc, w, b, dact, name, col0=0):
    T, C = xbc.shape[0], w.shape[1]
    B = T // S

    def body(x_ref, w_ref, b_ref, d_ref, dx_ref, dw_ref, db_ref):
        bi = pl.program_id(1)
        x = x_ref[...]
        row = lax.broadcasted_iota(jnp.int32, (S, 1), 0)
        xs = [x] + [_shift_down(x, s, row) for s in range(1, 4)]
        c = b_ref[...]
        for s in range(4):
            c = c + w_ref[3 - s:4 - s, :] * xs[s]
        _, vjp = jax.vjp(_silu, c)
        dc = vjp(d_ref[...])[0]
        dx = w_ref[3:4, :] * dc
        for s in range(1, 4):
            dx = dx + w_ref[3 - s:4 - s, :] * _shift_up(dc, s, row)
        dx_ref[...] = dx.astype(dx_ref.dtype)
        first = bi == 0
        for s in range(4):
            _acc_rows(dw_ref, 3 - s, jnp.sum(dc * xs[s], axis=0, keepdims=True), first)
        _acc_rows(db_ref, 0, jnp.sum(dc, axis=0, keepdims=True), first)

    blk = pl.BlockSpec((S, 128), lambda j, b: (b, j))
    return _pcall(body, name=name, grid=(C // 128, B),
                  in_specs=[pl.BlockSpec((S, 128), lambda j, b: (b, j + col0)), pl.BlockSpec((4, 128), lambda j, b: (0, j)),
                            pl.BlockSpec((1, 128), lambda j, b: (0, j)), blk],
                  out_specs=[blk, pl.BlockSpec((4, 128), lambda j, b: (0, j)), pl.BlockSpec((1, 128), lambda j, b: (0, j))],
                  out_shape=[jax.ShapeDtypeStruct((T, C), _ACT), jax.ShapeDtypeStruct((4, C), F32),
                             jax.ShapeDtypeStruct((1, C), F32)],
                  compiler_params=_cp(("parallel", "arbitrary")))(xbc, w, b, dact)


def _acc_rows(ref, r, val, first):
    @pl.when(first)
    def _():
        ref[r:r + 1, :] = val

    @pl.when(jnp.logical_not(first))
    def _():
        ref[r:r + 1, :] += val


def _tri_consts():
    i = lax.broadcasted_iota(jnp.int32, (CH, CH), 0)
    j = lax.broadcasted_iota(jnp.int32, (CH, CH), 1)
    return (i == j).astype(F32), (j <= i).astype(F32), (i <= j).astype(F32), i >= j


def _ssd_chunk(h, x, dt, Bm, Cm, a, dsk, consts):
    eye, tril, triu, lower = consts
    Bb = Bm.astype(_MXU)
    Cb = Cm.astype(_MXU)
    cb = lax.dot_general(Cb, Bb, (((1,), (1,)), ((), ())), preferred_element_type=F32)
    ys, hn = [], []
    for e in range(4):
        adt = dt[e] * a[e]
        adt_row = jnp.sum(adt * eye, axis=0, keepdims=True)
        cs_col = jnp.sum(adt_row * tril, axis=1, keepdims=True)
        cs_row = jnp.sum(adt * triu, axis=0, keepdims=True)
        cs_last = jnp.sum(adt, axis=0, keepdims=True)
        decay = jnp.exp(jnp.where(lower, cs_col - cs_row, -jnp.inf))
        xb = (x[e] * dt[e]).astype(_MXU)
        y_diag = jnp.dot((cb * decay).astype(_MXU), xb, preferred_element_type=F32)
        bdec = (Bm * jnp.exp(cs_last - cs_col)).astype(_MXU)
        st = lax.dot_general(bdec, xb, (((0,), (0,)), ((), ())), preferred_element_type=F32)
        hn.append(h[e] * jnp.exp(cs_last) + st)
        y_off = jnp.exp(cs_col) * jnp.dot(Cb, h[e].astype(_MXU), preferred_element_type=F32)
        ys.append(y_diag + y_off + dsk[e] * x[e])
    return ys, hn


def _ssd_specs(order):
    def im(f):
        return lambda p, q: f(*order(p, q))
    xs = pl.BlockSpec((S, 256), im(lambda b, g: (b, g)))
    dt = pl.BlockSpec((None, S, 4), im(lambda b, g: (g, b, 0)))
    bc = pl.BlockSpec((S, 128), im(lambda b, g: (b, g)))
    hd = pl.BlockSpec((None, 1, 4), im(lambda b, g: (g, 0, 0)))
    hs = pl.BlockSpec((None, None, S // CH, 4, 128, 64), im(lambda b, g: (b, g, 0, 0, 0, 0)))
    bw = pl.BlockSpec((S, 128), im(lambda b, g: (b, 8 + g)))
    cw = pl.BlockSpec((S, 128), im(lambda b, g: (b, 12 + g)))
    return xs, dt, bc, hd, hs, bw, cw


def _ssd_fwd(act, dtg, a, dsk, name):
    xs = bm = cm = act
    T = xs.shape[0]
    B = T // S
    nc = S // CH

    def body(x_ref, dt_ref, b_ref, c_ref, a_ref, k_ref, y_ref, hs_ref, h_ref):
        consts = _tri_consts()
        h_ref[...] = jnp.zeros_like(h_ref)
        al = [a_ref[:, e:e + 1] for e in range(4)]
        kl = [k_ref[:, e:e + 1] for e in range(4)]

        def step(c, carry):
            r0 = pl.multiple_of(c * CH, CH)
            rows = pl.ds(r0, CH)
            h = [h_ref[e] for e in range(4)]
            for e in range(4):
                hs_ref[c, e] = h[e]
            x = [x_ref[rows, 64 * e:64 * e + 64] for e in range(4)]
            dt = [dt_ref[rows, e:e + 1] for e in range(4)]
            ys, hn = _ssd_chunk(h, x, dt, b_ref[rows, :], c_ref[rows, :], al, kl, consts)
            for e in range(4):
                y_ref[rows, 64 * e:64 * e + 64] = ys[e]
                h_ref[e] = hn[e]
            return carry

        lax.fori_loop(0, nc, step, 0)

    sx, sdt, sbc, shd, shs, sbw, scw = _ssd_specs(lambda b, g: (b, g))
    return _pcall(body, name=name, grid=(B, 4), in_specs=[sx, sdt, sbw, scw, shd, shd], out_specs=[sx, shs],
                  out_shape=[jax.ShapeDtypeStruct((T, 1024), F32), jax.ShapeDtypeStruct((B, 4, nc, 4, 128, 64), F32)],
                  scratch_shapes=[pltpu.VMEM((4, 128, 64), F32)],
                  compiler_params=_cp(("parallel", "parallel")))(xs, dtg, bm, cm, a, dsk)


def _lane_place(vals, width):
    lane = lax.broadcasted_iota(jnp.int32, (1, width), 1)
    out = jnp.zeros((1, width), F32)
    for e, v in enumerate(vals):
        out = out + jnp.where(lane == e, v, 0.0)
    return out


def _ssd_bwd(act, dtg, a, dsk, hs, dy, name):
    xs = bm = cm = act
    T = xs.shape[0]
    B = T // S
    nc = S // CH

    def body(x_ref, dt_ref, b_ref, c_ref, a_ref, k_ref, hs_ref, dy_ref,
             dx_ref, ddt_ref, db_ref, dc_ref, dak_ref, dh_ref, sc_ref):
        bi = pl.program_id(1)
        consts = _tri_consts()
        dh_ref[...] = jnp.zeros_like(dh_ref)
        sc_ref[...] = jnp.zeros_like(sc_ref)
        al = [a_ref[:, e:e + 1] for e in range(4)]
        kl = [k_ref[:, e:e + 1] for e in range(4)]

        def step(i, carry):
            c = nc - 1 - i
            r0 = pl.multiple_of(c * CH, CH)
            rows = pl.ds(r0, CH)
            h = [hs_ref[c, e] for e in range(4)]
            x = [x_ref[rows, 64 * e:64 * e + 64] for e in range(4)]
            dt = [dt_ref[rows, e:e + 1] for e in range(4)]
            f = functools.partial(_ssd_chunk, consts=consts)
            _, vjp = jax.vjp(f, h, x, dt, b_ref[rows, :], c_ref[rows, :], al, kl)
            dys = [dy_ref[rows, 64 * e:64 * e + 64] for e in range(4)]
            dhn = [dh_ref[e] for e in range(4)]
            dh, dx, ddt, dB, dC, da, dk = vjp((dys, dhn))
            for e in range(4):
                dh_ref[e] = dh[e]
                dx_ref[rows, 64 * e:64 * e + 64] = dx[e]
                ddt_ref[rows, e:e + 1] = ddt[e]
            db_ref[rows, :] = dB
            dc_ref[rows, :] = dC
            sc_ref[0:1, :] += _lane_place(da, 128)
            sc_ref[1:2, :] += _lane_place(dk, 128)
            return carry

        lax.fori_loop(0, nc, step, 0)
        first = bi == 0

        @pl.when(first)
        def _():
            dak_ref[...] = sc_ref[...]

        @pl.when(jnp.logical_not(first))
        def _():
            dak_ref[...] += sc_ref[...]

    sx, sdt, sbc, shd, shs, sbw, scw = _ssd_specs(lambda g, b: (b, g))
    return _pcall(body, name=name, grid=(4, B), in_specs=[sx, sdt, sbw, scw, shd, shd, shs, sx],
                  out_specs=[sx, sdt, sbc, sbc, pl.BlockSpec((None, 8, 128), lambda g, b: (g, 0, 0))],
                  out_shape=[jax.ShapeDtypeStruct((T, 1024), F32), jax.ShapeDtypeStruct((4, T, 4), F32),
                             jax.ShapeDtypeStruct((T, 512), F32), jax.ShapeDtypeStruct((T, 512), F32),
                             jax.ShapeDtypeStruct((4, 8, 128), F32)],
                  scratch_shapes=[pltpu.VMEM((4, 128, 64), F32), pltpu.VMEM((8, 128), F32)],
                  compiler_params=_cp(("parallel", "arbitrary")))(xs, dtg, bm, cm, a, dsk, hs, dy)


def _gate_norm(y, z, nw):
    t = y * _silu(z)
    return t * lax.rsqrt(jnp.mean(t * t, axis=-1, keepdims=True) + SSD_EPS) * nw


def _ssd_gate_norm(y, z, nw, name, zcol=0):
    def fn(tv, fv):
        return [[_gate_norm(tv[g], tv[4 + g], fv[0][:, 256 * g:256 * g + 256]) for g in range(4)]], []
    tiled = [(y, 256, g) for g in range(4)] + [(z, 256, zcol + g) for g in range(4)]
    return _rowwise(fn, tiled, [nw], [(1024, _ACT)], [], name=name)[0]


def _ssd_gate_norm_bwd(y, z, nw, dout, name, zcol=0):
    def fn(tv, fv):
        dys, dzs, dns = [], [], []
        for g in range(4):
            _, vjp = jax.vjp(_gate_norm, tv[g], tv[4 + g], fv[0][:, 256 * g:256 * g + 256])
            a, b, c = vjp(tv[8 + g])
            dys.append(a)
            dzs.append(b)
            dns.append(c)
        return [dys, dzs], [dns]
    tiled = [(y, 256, g) for g in range(4)] + [(z, 256, zcol + g) for g in range(4)] + [(dout, 256, g) for g in range(4)]
    return _rowwise(fn, tiled, [nw], [(1024, F32), (1024, _ACT)], [(1, 1024)], name=name)


def _t5_bucket_np(dist):
    dist = np.maximum(dist, 0)
    max_exact = 16
    large = max_exact + (np.log(np.maximum(dist, 1) / max_exact) / np.log(2048 / max_exact) * (32 - max_exact)).astype(np.int32)
    large = np.minimum(large, 31)
    return np.where(dist < max_exact, dist, large).astype(np.int32)


def _bucket_maps():
    qi = np.arange(128)[:, None]
    kj = np.arange(256)[None, :]
    return np.stack([_t5_bucket_np((qi - kj + 128) * dil) for dil in ATTN_DILS]).astype(np.int32)


def _bias_build(rel_bias, maps, name):
    def body(tab_ref, map_ref, o_ref):
        hh = pl.program_id(0)
        m = map_ref[...]
        acc = jnp.zeros((128, 256), F32)
        for b in range(32):
            acc = jnp.where(m == b, tab_ref[b, hh], acc)
        o_ref[...] = acc

    return _pcall(body, name=name, grid=(12,),
                  in_specs=[pl.BlockSpec(memory_space=pltpu.SMEM), pl.BlockSpec((None, 128, 256), lambda h: (h // 4, 0, 0))],
                  out_specs=pl.BlockSpec((None, 128, 256), lambda h: (h, 0, 0)),
                  out_shape=jax.ShapeDtypeStruct((12, 128, 256), F32), compiler_params=_cp(("parallel",)))(rel_bias, maps)


def _bias_reduce(dbias, maps, name):
    nl = dbias.shape[0]

    def body(d_ref, map_ref, o_ref):
        m = map_ref[...]
        d = d_ref[0]
        for i in range(1, nl):
            d = d + d_ref[i]
        lane = lax.broadcasted_iota(jnp.int32, (1, 128), 1)
        out = jnp.zeros((1, 128), F32)
        for b in range(32):
            s = jnp.sum(jnp.sum(jnp.where(m == b, d, 0.0), axis=1, keepdims=True), axis=0, keepdims=True)
            out = out + jnp.where(lane == b, s, 0.0)
        o_ref[...] = out

    return _pcall(body, name=name, grid=(12,),
                  in_specs=[pl.BlockSpec((nl, None, 128, 256), lambda h: (0, h, 0, 0)),
                            pl.BlockSpec((None, 128, 256), lambda h: (h // 4, 0, 0))],
                  out_specs=pl.BlockSpec((None, 1, 128), lambda h: (h, 0, 0)),
                  out_shape=jax.ShapeDtypeStruct((12, 1, 128), F32), compiler_params=_cp(("parallel",)))(dbias, maps)


def _attn_block(q, kb, vb, bias, mask):
    s = lax.dot_general(q.astype(_MXU), kb.astype(_MXU), (((1,), (1,)), ((), ())), preferred_element_type=F32) * 0.125 + bias
    s = jnp.where(mask, s, -jnp.inf)
    m = lax.stop_gradient(jnp.max(s, axis=-1, keepdims=True))
    p = jnp.exp(s - m)
    den = jnp.sum(p, axis=-1, keepdims=True)
    out = jnp.dot((p / den).astype(_MXU), vb.astype(_MXU), preferred_element_type=F32)
    return out, m + jnp.log(den)


ATTN_QB = 512


def _attn_masks(dil):
    qi = lax.broadcasted_iota(jnp.int32, (ATTN_QB, ATTN_QB + 128), 0)
    kj = lax.broadcasted_iota(jnp.int32, (ATTN_QB, ATTN_QB + 128), 1)
    band = (kj >= qi) & (kj <= qi + 128)
    if dil == 16:
        q2 = lax.broadcasted_iota(jnp.int32, (ATTN_QB, ATTN_QB), 0)
        k2 = lax.broadcasted_iota(jnp.int32, (ATTN_QB, ATTN_QB), 1)
        return ((q2 // 128) == (k2 // 128)) & (k2 <= q2), None
    return band[:, 128:], band


def _attn_wide_bias(b, dil):
    if dil == 16:
        return jnp.tile(b[:, 128:], (4, 4)), None
    z = jnp.zeros((128, 128), F32)
    band = jnp.concatenate([jnp.concatenate([z] * i + [b] + [z] * (3 - i), axis=1) for i in range(4)], axis=0)
    return band[:, 128:], band


def _fold_dbias(dbs, dil, band_form):
    def blk(i, j):
        return dbs[128 * i:128 * i + 128, 128 * j:128 * j + 128]
    if band_form:
        return sum(blk(i, i) for i in range(4)), sum(blk(i, i + 1) for i in range(4))
    cur = sum(blk(i, i) for i in range(4))
    if dil == 16:
        return None, cur
    return sum(blk(i, i - 1) for i in range(1, 4)), cur


def _attn_chunks(dil):
    out = []
    for n in range(S // ATTN_QB):
        if dil == 1 and n > 0:
            out.append((n * ATTN_QB, n * ATTN_QB - 128, ATTN_QB + 128, True))
        else:
            out.append((n * ATTN_QB, n * ATTN_QB, ATTN_QB, False))
    return out


def _qkv_specs(gi, order):
    def spec(base):
        col = (base + 256 * gi) // 128
        return pl.BlockSpec((S, 128), lambda p, q: (order(p, q)[0], col + order(p, q)[1]))
    return [spec(O_Q), spec(O_K), spec(O_V)]


def _residue_rows(r, dil):
    return pl.ds(r, S // dil, stride=dil)


def _attn_fwd(hcat, bias_all, gi, name):
    dil = ATTN_DILS[gi]
    T = hcat.shape[0]
    B, L = T // S, S // dil

    def body(q_ref, k_ref, v_ref, b_ref, o_ref, l_ref, *scr):
        mask_first, mask_band = _attn_masks(dil)
        if dil > 1:
            qs, ks, vs, os_, ls = scr
            for r in range(dil):
                rows, dst = _residue_rows(r, dil), pl.ds(r * L, L)
                qs[dst, :] = q_ref[rows, :]
                ks[dst, :] = k_ref[rows, :]
                vs[dst, :] = v_ref[rows, :]
        else:
            qs, ks, vs, os_, ls = q_ref, k_ref, v_ref, o_ref, l_ref
        ls[...] = jnp.zeros_like(ls)
        for e in range(2):
            lanes = slice(64 * e, 64 * e + 64)
            bias_first, bias_band = _attn_wide_bias(b_ref[e], dil)
            for q0, k0, kn, band_form in _attn_chunks(dil):
                cur, keys = pl.ds(q0, ATTN_QB), pl.ds(k0, kn)
                o, l = _attn_block(qs[cur, lanes], ks[keys, lanes], vs[keys, lanes],
                                   bias_band if band_form else bias_first, mask_band if band_form else mask_first)
                os_[cur, lanes] = o
                ls[cur, e:e + 1] = l
        if dil > 1:
            for r in range(dil):
                rows, src = _residue_rows(r, dil), pl.ds(r * L, L)
                o_ref[rows, :] = os_[src, :]
                l_ref[rows, :] = ls[src, :]

    scratch = [pltpu.VMEM((S, 128), F32)] * 5 if dil > 1 else []
    return _pcall(body, name=name, grid=(B, 2),
                  in_specs=_qkv_specs(gi, lambda b, hp: (b, hp))
                  + [pl.BlockSpec((2, 128, 256), lambda b, hp: (2 * gi + hp, 0, 0))],
                  out_specs=[pl.BlockSpec((S, 128), lambda b, hp: (b, hp)),
                             pl.BlockSpec((None, S, 128), lambda b, hp: (hp, b, 0))],
                  out_shape=[jax.ShapeDtypeStruct((T, 256), F32), jax.ShapeDtypeStruct((2, T, 128), F32)],
                  scratch_shapes=scratch,
                  compiler_params=_cp(("parallel", "parallel")))(hcat, hcat, hcat, bias_all)


def _attn_bwd(hcat, bias_all, gi, do, dl, name):
    dil = ATTN_DILS[gi]
    T = hcat.shape[0]
    B, L = T // S, S // dil

    def body(q_ref, k_ref, v_ref, b_ref, do_ref, dl_ref, dq_ref, dk_ref, dv_ref, db_ref, acc_ref, *scr):
        bi = pl.program_id(1)
        mask_first, mask_band = _attn_masks(dil)
        if dil > 1:
            qs, ks, vs, dos, dls, dqs, dks, dvs = scr
            for r in range(dil):
                rows, dst = _residue_rows(r, dil), pl.ds(r * L, L)
                qs[dst, :] = q_ref[rows, :]
                ks[dst, :] = k_ref[rows, :]
                vs[dst, :] = v_ref[rows, :]
                dos[dst, :] = do_ref[rows, :]
                dls[dst, :] = dl_ref[rows, :]
        else:
            qs, ks, vs, dos, dls, dqs, dks, dvs = q_ref, k_ref, v_ref, do_ref, dl_ref, dq_ref, dk_ref, dv_ref
        dks[...] = jnp.zeros_like(dks)
        dvs[...] = jnp.zeros_like(dvs)
        for e in range(2):
            lanes = slice(64 * e, 64 * e + 64)
            bias_first, bias_band = _attn_wide_bias(b_ref[e], dil)
            acc_ref[...] = jnp.zeros_like(acc_ref)
            for q0, k0, kn, band_form in _attn_chunks(dil):
                cur, keys = pl.ds(q0, ATTN_QB), pl.ds(k0, kn)
                f = functools.partial(_attn_block, mask=mask_band if band_form else mask_first)
                _, vjp = jax.vjp(f, qs[cur, lanes], ks[keys, lanes], vs[keys, lanes],
                                 bias_band if band_form else bias_first)
                dq, dkb, dvb, dbs = vjp((dos[cur, lanes], dls[cur, e:e + 1]))
                dqs[cur, lanes] = dq
                dks[keys, lanes] += dkb
                dvs[keys, lanes] += dvb
                prev, here = _fold_dbias(dbs, dil, band_form)
                if prev is not None:
                    acc_ref[:, 0:128] += prev
                acc_ref[:, 128:256] += here

            @pl.when(bi == 0)
            def _(e=e):
                db_ref[e] = acc_ref[...]

            @pl.when(bi > 0)
            def _(e=e):
                db_ref[e] += acc_ref[...]

        if dil > 1:
            for r in range(dil):
                rows, src = _residue_rows(r, dil), pl.ds(r * L, L)
                dq_ref[rows, :] = dqs[src, :]
                dk_ref[rows, :] = dks[src, :]
                dv_ref[rows, :] = dvs[src, :]

    order = lambda hp, b: (b, hp)
    blk = pl.BlockSpec((S, 128), lambda hp, b: (b, hp))
    lblk = pl.BlockSpec((None, S, 128), lambda hp, b: (hp, b, 0))
    sds = jax.ShapeDtypeStruct((T, 256), F32)
    scratch = [pltpu.VMEM((128, 256), F32)] + ([pltpu.VMEM((S, 128), F32)] * 8 if dil > 1 else [])
    return _pcall(body, name=name, grid=(2, B),
                  in_specs=_qkv_specs(gi, order) + [pl.BlockSpec((2, 128, 256), lambda hp, b: (2 * gi + hp, 0, 0)), blk, lblk],
                  out_specs=[blk, blk, blk, pl.BlockSpec((2, 128, 256), lambda hp, b: (hp, 0, 0))],
                  out_shape=[sds, sds, sds, jax.ShapeDtypeStruct((4, 128, 256), F32)],
                  scratch_shapes=scratch,
                  compiler_params=_cp(("parallel", "arbitrary")))(hcat, hcat, hcat, bias_all, do, dl)


def _lse_merge(o0, o1, o2, l0, l1, l2):
    m = lax.stop_gradient(jnp.maximum(jnp.maximum(l0, l1), l2))
    e0, e1, e2 = jnp.exp(l0 - m), jnp.exp(l1 - m), jnp.exp(l2 - m)
    den = e0 + e1 + e2
    return (e0 / den) * o0 + (e1 / den) * o1 + (e2 / den) * o2


def _attn_merge(outs, lses, dy, name):
    T = outs[0].shape[0]
    bwd = dy is not None
    tm = 512

    def body(*refs):
        o_refs, l_refs = refs[:3], refs[3:6]
        if bwd:
            for r in refs[10:13]:
                r[...] = jnp.zeros_like(r)
        for e in range(2):
            lanes = slice(64 * e, 64 * e + 64)
            vals = [r[:, lanes] for r in o_refs] + [r[:, e:e + 1] for r in l_refs]
            if not bwd:
                refs[6][:, lanes] = _lse_merge(*vals).astype(refs[6].dtype)
            else:
                _, vjp = jax.vjp(_lse_merge, *vals)
                g = vjp(refs[6][:, lanes])
                for r, v in zip(refs[7:10], g[:3]):
                    r[:, lanes] = v
                for r, v in zip(refs[10:13], g[3:]):
                    r[:, e:e + 1] = v

    blk = pl.BlockSpec((tm, 128), lambda i, hp: (i, hp))
    lblk = pl.BlockSpec((None, tm, 128), lambda i, hp: (hp, i, 0))
    lsd = jax.ShapeDtypeStruct((2, T, 128), F32)
    if not bwd:
        return _pcall(body, name=name, grid=(T // tm, 2), in_specs=[blk] * 3 + [lblk] * 3, out_specs=blk,
                      out_shape=jax.ShapeDtypeStruct((T, 256), F32),
                      compiler_params=_cp(("parallel", "parallel")))(*outs, *lses)
    return _pcall(body, name=name, grid=(T // tm, 2), in_specs=[blk] * 3 + [lblk] * 3 + [blk],
                  out_specs=[blk] * 3 + [lblk] * 3, out_shape=[jax.ShapeDtypeStruct((T, 256), F32)] * 3 + [lsd] * 3,
                  compiler_params=_cp(("parallel", "parallel")))(*outs, *lses, dy)


def _gmerge(g0, g1, g2, gb, ya, yb, yc):
    return (jax.nn.sigmoid(g0 + gb[:, 0:D]) * ya + jax.nn.sigmoid(g1 + gb[:, D:2 * D]) * yb
            + jax.nn.sigmoid(g2 + gb[:, 2 * D:3 * D]) * yc)


def _gated_merge(gates, gb, ya, yb, yc, name, gcol=0):
    def fn(tv, fv):
        return [_gmerge(tv[0], tv[1], tv[2], fv[0], tv[3], tv[4], tv[5])], []
    return _rowwise(fn, [(gates, D, gcol), (gates, D, gcol + 1), (gates, D, gcol + 2), ya, yb, yc], [gb], [(D, _ACT)], [],
                    name=name)[0]


def _gated_merge_bwd(gates, gb, ya, yb, yc, dm, name, gcol=0):
    def fn(tv, fv):
        _, vjp = jax.vjp(_gmerge, tv[0], tv[1], tv[2], fv[0], tv[3], tv[4], tv[5])
        d0, d1, d2, dgb, da, db, dc = vjp(tv[6])
        return [[d0, d1, d2], da, db, dc], [dgb]
    return _rowwise(fn, [(gates, D, gcol), (gates, D, gcol + 1), (gates, D, gcol + 2), ya, yb, yc, dm], [gb],
                    [(3 * D, _ACT), (D, _ACT), (D, _ACT), (D, _ACT)], [(1, 3 * D)], name=name)


def _pool_affine(t1, pb, ps, dout, name):
    if dout is None:
        def fn(tv, fv):
            return [(tv[0] + fv[0]) * fv[1]], []
        return _rowwise(fn, [t1], [pb, ps], [(POOLW, _ACT)], [], name=name)[0]

    def fnb(tv, fv):
        t2, vjp = jax.vjp(lambda t, b, s: (t + b) * s, tv[0], fv[0], fv[1])
        dt, db, dsc = vjp(tv[1])
        return [dt, t2], [db, dsc]
    return _rowwise(fnb, [t1, dout], [pb, ps], [(POOLW, _ACT), (POOLW, _ACT)], [(1, POOLW), (1, POOLW)], name=name)


def _dt_softplus(dt_raw, dt_bias, ddt, name):
    f = lambda r, b: _softplus(r + b)
    if ddt is None:
        def fn(tv, fv):
            return [f(tv[0], fv[0])], []
        return _rowwise(fn, [dt_raw], [dt_bias], [(16, F32)], [], name=name, tm=1024)[0]

    def fnb(tv, fv):
        _, vjp = jax.vjp(f, tv[0], fv[0])
        dr, db = vjp(tv[1])
        return [dr], [db]
    return _rowwise(fnb, [dt_raw, ddt], [dt_bias], [(16, F32)], [(1, 16)], name=name, tm=1024)


def _adamw_math(wv, gv, mv, vv):
    c1 = 1.0 / (1.0 - ADAM_B1 ** ADAM_STEP)
    c2 = 1.0 / (1.0 - ADAM_B2 ** ADAM_STEP)
    mn = ADAM_B1 * mv + (1.0 - ADAM_B1) * gv
    vn = ADAM_B2 * vv + (1.0 - ADAM_B2) * (gv * gv)
    delta = -ADAM_LR * ((mn * c1) / (jnp.sqrt(vn * c2) + ADAM_EPS) + ADAM_WD * wv)
    return delta, mn, vn


def _adamw(w, g, m, v, name):
    R, C = w.shape
    tm = _pick(R, (256, 128, 64, 32, 16, 8))
    return _rowwise(lambda tv, fv: (list(_adamw_math(*tv)), []), [w, g, m, v], [], [(C, F32)] * 3, [], name=name, tm=tm)


def _adamw_layer(i, w, g, m, v, accs, name):
    R, C = w.shape
    r = R // NL
    tm = _pick(r, (256, 128, 64, 32, 16, 8))
    nt = r // tm
    if accs is None:
        accs = [lax.empty((R, C), F32) for _ in range(4)]

    def body(w_ref, g_ref, m_ref, v_ref, a0, a1, a2, a3, go_ref, do_ref, mo_ref, vo_ref):
        gv = g_ref[...]
        delta, mn, vn = _adamw_math(w_ref[...], gv, m_ref[...], v_ref[...])
        go_ref[...] = gv
        do_ref[...] = delta
        mo_ref[...] = mn
        vo_ref[...] = vn

    slab = pl.BlockSpec((tm, C), lambda t: (i * nt + t, 0))
    anyspec = pl.BlockSpec(memory_space=pl.ANY)
    return _pcall(body, name=name, grid=(nt,),
                  in_specs=[slab, pl.BlockSpec((tm, C), lambda t: (t, 0)), slab, slab] + [anyspec] * 4,
                  out_specs=[slab] * 4, out_shape=[jax.ShapeDtypeStruct((R, C), F32)] * 4,
                  input_output_aliases={4 + k: k for k in range(4)},
                  compiler_params=_cp(("parallel",)))(w, g, m, v, *accs)


def _ffn_fwd(x, xm, w13, w2, g, b, tag, dep=None):
    h = _mm(xm, w13, dep=dep, name=f"{tag}_h")
    s = _swiglu_act(h, name=f"{tag}_act")
    y = _mm(s, w2, name=f"{tag}_y")
    r, out, outm = _res_ln_fwd(x, y, g, b, 0.5, name=f"{tag}_ln")
    return out, outm, dict(x=xm, h=h, r=r)


def _ffn_bwd(dout, sv, w13, w2, g, b, tag, dep=None):
    dskip, dy, dg, db = _ln_bwd(sv['r'], g, b, dout, 0.5, name=f"{tag}_lnb")
    ds = _mm(dy, w2, tb=True, dep=dep, name=f"{tag}_ds")
    dh, s = _swiglu_act_bwd(sv['h'], ds, name=f"{tag}_actb")
    dw2 = _mm(s, dy, ta=True, name=f"{tag}_dw2")
    dw13 = _mm(sv['x'], dh, ta=True, name=f"{tag}_dw13")
    dx = _mm(dh, w13, tb=True, add=dskip, name=f"{tag}_dx")
    return dx, dict(w13=dw13, w2=dw2, g=dg, b=db)


def _mixer_fwd(x1, x1m, W, bias_all, tag, dep=None):
    T = x1.shape[0]
    hcat = _mm(x1m, W['w_in_r'], dep=dep, name=f"{tag}_hcat")
    dt_raw = hcat[:, O_DT:O_DT + 16]
    pooled = _pool_mean(hcat, False, name=f"{tag}_pool", col0=O_U // 128)
    t1 = _mm(pooled, W['pool_wbd'], name=f"{tag}_pt1")
    t2 = _pool_affine(t1, W['pool_b'], W['pool_scale'], None, name=f"{tag}_paff")
    ya = _mm(t2, W['p_pool'], name=f"{tag}_ya")
    act = _conv_silu(hcat, W['conv_w'], W['conv_b'], name=f"{tag}_conv", col0=O_XBC // 128)
    dt = _dt_softplus(dt_raw, W['dt_bias'], None, name=f"{tag}_dt")
    dtg = dt.reshape(T, 4, 4).transpose(1, 0, 2)
    yscan, hs = _ssd_fwd(act, dtg, W['a_neg'], W['d_skip'], name=f"{tag}_ssd")
    ybn = _ssd_gate_norm(yscan, hcat, W['ssd_norm'], name=f"{tag}_gn", zcol=O_Z // 256)
    yb = _mm(ybn, W['p_ssd'], name=f"{tag}_yb")
    outs, lses = [], []
    for gi in range(len(ATTN_DILS)):
        o, l = _attn_fwd(hcat, bias_all, gi, name=f"{tag}_attn{gi}")
        outs.append(o)
        lses.append(l)
    ycp = _attn_merge(outs, lses, None, name=f"{tag}_amerge")
    yc = _mm(ycp, W['p_attn'], name=f"{tag}_yc")
    merged = _gated_merge(hcat, W['gate_b'], ya, yb, yc, name=f"{tag}_gm", gcol=O_G // D)
    mix = _mm(merged, W['w_out'], name=f"{tag}_mix")
    r, out, outm = _res_ln_fwd(x1, mix, W['ln2_g'], W['ln2_b'], 1.0, name=f"{tag}_ln")
    sv = dict(x1=x1m, dt_raw=dt_raw, pooled=pooled, t1=t1, act=act, dtg=dtg,
              hs=hs, yscan=yscan, ybn=ybn, hcat=hcat, outs=outs, lses=lses, ycp=ycp, ya=ya, yb=yb, yc=yc,
              merged=merged, r=r)
    return out, outm, sv


def _mixer_bwd(dout, sv, W, bias_all, tag, dep=None):
    T = dout.shape[0]
    gr = {}
    dx1a, dr, gr['ln2_g'], gr['ln2_b'] = _ln_bwd(sv['r'], W['ln2_g'], W['ln2_b'], dout, 1.0, name=f"{tag}_lnb")
    dmerged = _mm(dr, W['w_out'], tb=True, dep=dep, name=f"{tag}_dmerged")
    gr['w_out'] = _mm(sv['merged'], dr, ta=True, name=f"{tag}_dwout")
    dgates, dya, dyb, dyc, gr['gate_b'] = _gated_merge_bwd(sv['hcat'], W['gate_b'], sv['ya'], sv['yb'], sv['yc'],
                                                           dmerged, name=f"{tag}_gmb", gcol=O_G // D)
    dycp = _mm(dyc, W['p_attn'], tb=True, name=f"{tag}_dycp")
    gr['p_attn'] = _mm(sv['ycp'], dyc, ta=True, name=f"{tag}_dpattn")
    dml = _attn_merge(sv['outs'], sv['lses'], dycp, name=f"{tag}_amergeb")
    dq, dk, dv, dbias = [], [], [], []
    for gi in range(len(ATTN_DILS)):
        a, b, c, d = _attn_bwd(sv['hcat'], bias_all, gi, dml[gi], dml[3 + gi], name=f"{tag}_attnb{gi}")
        dq.append(a)
        dk.append(b)
        dv.append(c)
        dbias.append(d)
    dbias = jnp.concatenate(dbias, axis=0)
    dybn = _mm(dyb, W['p_ssd'], tb=True, name=f"{tag}_dybn")
    gr['p_ssd'] = _mm(sv['ybn'], dyb, ta=True, name=f"{tag}_dpssd")
    dyscan, dz, gr['ssd_norm'] = _ssd_gate_norm_bwd(sv['yscan'], sv['hcat'], W['ssd_norm'], dybn, name=f"{tag}_gnb",
                                                    zcol=O_Z // 256)
    dxs, ddtg, dbm, dcm, dak = _ssd_bwd(sv['act'], sv['dtg'], W['a_neg'], W['d_skip'], sv['hs'], dyscan,
                                        name=f"{tag}_ssdb")
    gr['a_neg'], gr['d_skip'] = dak[:, 0, 0:4], dak[:, 1, 0:4]
    ddt = ddtg.transpose(1, 0, 2).reshape(T, 16)
    ddt_raw, gr['dt_bias'] = _dt_softplus(sv['dt_raw'], W['dt_bias'], ddt, name=f"{tag}_dtb")
    dact = jnp.concatenate([dxs, dbm, dcm], axis=1)
    dxbc, gr['conv_w'], gr['conv_b'] = _conv_silu_bwd(sv['hcat'], W['conv_w'], W['conv_b'], dact, name=f"{tag}_convb",
                                                      col0=O_XBC // 128)
    dt2 = _mm(dya, W['p_pool'], tb=True, name=f"{tag}_dt2")
    dt1, t2, gr['pool_b'], gr['pool_scale'] = _pool_affine(sv['t1'], W['pool_b'], W['pool_scale'], dt2, name=f"{tag}_paffb")
    gr['p_pool'] = _mm(t2, dya, ta=True, name=f"{tag}_dppool")
    dpooled = _mm(dt1, W['pool_wbd'], tb=True, name=f"{tag}_dpooled")
    gr['pool_wbd'] = _mm(sv['pooled'], dt1, ta=True, name=f"{tag}_dpoolw")
    du = _pool_mean(dpooled, True, name=f"{tag}_poolb")
    dhcat = jnp.concatenate([t.astype(_ACT) for t in [du, dz, dxbc] + dq + dk + dv + [dgates, ddt_raw]]
                            + [jnp.zeros((T, HC - O_DT - 16), _ACT)], axis=1)
    dx1 = _mm(dhcat, W['w_in_r'], tb=True, add=dx1a, name=f"{tag}_dx1")
    gr['w_in_r'] = _mm(sv['x1'], dhcat, ta=True, name=f"{tag}_dwin")
    return dx1, gr, dbias


def _prep_layer_weights(i, inp, G):
    W = {}
    for n in BIG:
        if n not in G:
            continue
        g = G[n]
        if n == 'w_in':
            W['w_in_r'] = jnp.concatenate(_nat_pieces(g, 0, 3840) + _nat_pieces(g, 3856, 9232) + _nat_pieces(g, 3840, 3856)
                                          + [jnp.zeros((D, HC - 9232), g.dtype)], axis=1)
        elif n in COL_SHARDED:
            W[n] = jnp.concatenate([g[j] for j in range(4)], axis=1)
        else:
            W[n] = g.reshape(4 * g.shape[1], g.shape[2])
    pw = inp['pool_w'][i].astype(_MXU)
    wbd = jnp.zeros((POOLW, POOLW), _MXU)
    for g in range(4):
        wbd = lax.dynamic_update_slice(wbd, pw[g], (g * POOL_GDIM, g * POOL_GDIM))
    W['pool_wbd'] = wbd
    W['pool_b'] = inp['pool_b'][i].reshape(1, POOLW)
    W['pool_scale'] = inp['pool_scale'][i].reshape(1, POOLW)
    if 'conv_w' in G:
        W['conv_w'] = jnp.concatenate([G['conv_w'][j] for j in range(4)], axis=1)
        W['gate_b'] = jnp.concatenate([G['gate_b'][j][b:b + 1] for b in range(3) for j in range(4)], axis=1)
    W['conv_b'] = inp['conv_b'][i].reshape(1, 2048)
    W['dt_bias'] = inp['dt_bias'][i].reshape(1, 16)
    W['a_neg'] = (-jnp.exp(inp['a_log'][i])).reshape(4, 1, 4)
    W['d_skip'] = inp['d_skip'][i].reshape(4, 1, 4)
    W['ssd_norm'] = inp['ssd_norm'][i].reshape(1, D)
    for n in ('ln1_g', 'ln1_b', 'ln2_g', 'ln2_b', 'ln3_g', 'ln3_b'):
        W[n] = inp[n][i].reshape(1, D)
    return W


GATHER_FIRST = ['ffn1_w13', 'ffn1_w2']
GATHER_REST = [n for n in BIG if n not in GATHER_FIRST] + ['gate_b', 'conv_w']


def _gather_start(inp, i, names):
    core = lax.axis_index("c")
    arrs = []
    for n in names:
        s = inp[n][i]
        if n in BIG:
            s = lax.dynamic_slice_in_dim(s, core * (s.shape[0] // 2), s.shape[0] // 2, axis=0).astype(BF16)
        arrs.append(s)
    state, token = _exchange_start(arrs, "chips", "gather", name="gather_start")
    return (names, state), token


def _gather_mid(handle, after):
    names, state = handle
    me = 2 * lax.axis_index("x") + lax.axis_index("y")
    own, outs = _exchange_wait(state, after, "chips", "gather", name="gather_wait")
    outs = [lax.dynamic_update_slice(o, a[None], (me, 0, 0)) for o, a in zip(outs, own)]
    big = [o for n, o in zip(names, outs) if n in BIG]
    state, token = _exchange_start(big, "cores", "gather", name="share_start")
    return (names, outs, state), token


def _gather_finish(handle, after):
    names, outs, state = handle
    core = lax.axis_index("c")
    mine, theirs = _exchange_wait(state, after, "cores", "gather", name="share_wait")
    G = {n: o for n, o in zip(names, outs) if n not in BIG}
    for n, a, b in zip([n for n in names if n in BIG], mine, theirs):
        G[n] = jnp.concatenate([jnp.where(core == 0, a, b), jnp.where(core == 0, b, a)], axis=1)
    return G


W_IN_SHARD = 2308


def _nat_pieces(g, lo, hi):
    out = []
    for j in range(4):
        s, e = max(lo, W_IN_SHARD * j), min(hi, W_IN_SHARD * (j + 1))
        if s < e:
            out.append(g[j][:, s - W_IN_SHARD * j:e - W_IN_SHARD * j])
    return out


def _reord_ranges(lo, hi):
    out = []
    for a, b, off in ((0, 3840, 0), (3840, 3856, O_DT - 3840), (3856, 9232, -16)):
        s, e = max(lo, a), min(hi, b)
        if s < e:
            out.append((s + off, e + off))
    return out


def _halves_of(n, g):
    if n == 'w_in':
        shards = [jnp.concatenate([g[:, a:b] for a, b in _reord_ranges(W_IN_SHARD * j, W_IN_SHARD * (j + 1))], axis=1)
                  for j in range(4)]
    elif n in COL_SHARDED:
        c = g.shape[1] // 4
        shards = [g[:, j * c:(j + 1) * c] for j in range(4)]
    else:
        r = g.shape[0] // 4
        shards = [g[j * r:(j + 1) * r] for j in range(4)]
    r2 = shards[0].shape[0] // 2
    return jnp.stack([jnp.concatenate([s[h * r2:(h + 1) * r2] for s in shards], axis=0) for h in range(2)])


def _reduce_a(grads):
    names = list(grads)
    halves = [_halves_of(n, grads[n]) for n in names]
    state, token = _exchange_start(halves, "cores", "scatter", name="rsc_start")
    return (names, state), token


def _reduce_b(handle, after):
    names, state = handle
    core = lax.axis_index("c").reshape(1)
    halves, got = _exchange_wait(state, after, "cores", "scatter", name="rsc_wait")
    chip = [_sum_own_recv(h, t, core, BF16, name="rs_sum2") for h, t in zip(halves, got)]
    chip = [t.reshape(4, t.shape[0] // 4, t.shape[1]) for t in chip]
    state, token = _exchange_start(chip, "chips", "scatter", name="rs_start")
    return (names, state), token


def _reduce_c(handle, after):
    names, state = handle
    chip_id = (2 * lax.axis_index("x") + lax.axis_index("y")).reshape(1)
    chip, got = _exchange_wait(state, after, "chips", "scatter", name="rs_wait")
    red = [_sum_own_recv(h, t, chip_id, F32, name="rs_sum4") for h, t in zip(chip, got)]
    other = _exchange(red, "cores", "gather", name="rs_share")
    out = {}
    for n, mine, theirs in zip(names, red, other):
        out[n] = jnp.where(lax.axis_index("c") == 0, jnp.concatenate([mine, theirs]), jnp.concatenate([theirs, mine]))
    return out


class _Comm:
    def __init__(self, inp):
        self.inp = inp

    def gather_start(self, i, names):
        return _gather_start(self.inp, i, names)

    gather_mid = staticmethod(_gather_mid)
    gather_finish = staticmethod(_gather_finish)

    def reduce_a(self, i, grads):
        return _reduce_a({n: grads[n] for n in BIG})

    reduce_b = staticmethod(_reduce_b)
    reduce_c = staticmethod(_reduce_c)


def _allreduce_small(vec, dep=None):
    for group in ("cores", "x", "y"):
        recv = _exchange([vec], group, "gather", name=f"ar_{group}", dep=dep if group == "cores" else None)[0]
        vec = _rowwise(lambda tv, fv: ([tv[0] + tv[1]], []), [vec, recv], [], [(128, F32)], [], name=f"ar_add_{group}")[0]
    return vec


def _pack(arrs):
    flat = jnp.concatenate([a.reshape(-1) for a in arrs])
    n = flat.shape[0]
    pad = (-n) % (256 * 128)
    flat = jnp.concatenate([flat, jnp.zeros((pad,), F32)])
    return flat.reshape(-1, 128)


def _unpack(p, shapes):
    flat = p.reshape(-1)
    out, off = [], 0
    for s in shapes:
        sz = int(np.prod(s))
        out.append(flat[off:off + sz].reshape(s))
        off += sz
    return out


def _forward_backward(inp, comm, bias_all):
    x = xm = inp['x'].reshape(-1, D)
    tgt = inp['loss_target'].reshape(-1, D)
    saved, Ws = [], []
    h_first, _ = comm.gather_start(0, GATHER_FIRST)
    h_rest, dep = comm.gather_start(0, GATHER_REST)
    h_first, tok = comm.gather_mid(h_first, x)
    G = comm.gather_finish(h_first, tok)
    for i in range(NL):
        W = _prep_layer_weights(i, inp, G)
        start_next = lambda: (comm.gather_start(i + 1, BIG + ['gate_b', 'conv_w']) if i + 1 < NL else (None, None))
        if i > 0:
            h_next, dep = start_next()
        x1, x1m, s1 = _ffn_fwd(x, xm, W['ffn1_w13'], W['ffn1_w2'], W['ln1_g'], W['ln1_b'], "f1", dep)
        if i == 0:
            h_rest, tok = comm.gather_mid(h_rest, x1m)
            W.update(_prep_layer_weights(i, inp, comm.gather_finish(h_rest, tok)))
            h_next, dep = start_next()
        x2, x2m, s2 = _mixer_fwd(x1, x1m, W, bias_all, "mx", dep if i == 0 else None)
        dep = None
        if h_next is not None:
            h_next, dep = comm.gather_mid(h_next, x2m)
        x, xm, s3 = _ffn_fwd(x2, x2m, W['ffn2_w13'], W['ffn2_w2'], W['ln3_g'], W['ln3_b'], "f2", dep)
        if h_next is not None:
            G = comm.gather_finish(h_next, xm)
        saved.append((s1, s2, s3))
        Ws.append(W)
    dy, lpart = _loss_fwd_bwd(x, tgt, name="loss")
    fins, reduced, dbiases = [None] * NL, [None] * NL, [None] * NL
    pend_a, pend_b, dep = None, None, None
    for i in reversed(range(NL)):
        W = Ws[i]
        s1, s2, s3 = saved[i]
        g = {}
        dx2, f = _ffn_bwd(dy, s3, W['ffn2_w13'], W['ffn2_w2'], W['ln3_g'], W['ln3_b'], "f2", dep)
        g['ffn2_w13'], g['ffn2_w2'], g['ln3_g'], g['ln3_b'] = f['w13'], f['w2'], f['g'], f['b']
        dep = None
        if pend_a is not None:
            handle, dep = comm.reduce_b(pend_a[1], dx2)
            pend_b = (pend_a[0], handle)
        dx1, gm, dbiases[i] = _mixer_bwd(dx2, s2, W, bias_all, "mx", dep)
        g.update(gm)
        dy, f = _ffn_bwd(dx1, s1, W['ffn1_w13'], W['ffn1_w2'], W['ln1_g'], W['ln1_b'], "f1")
        g['ffn1_w13'], g['ffn1_w2'], g['ln1_g'], g['ln1_b'] = f['w13'], f['w2'], f['g'], f['b']
        fins[i] = _finish_layer_grads(i, g, inp)
        if pend_b is not None:
            reduced[pend_b[0]] = comm.reduce_c(pend_b[1], dy)
            pend_b = None
        handle, dep = comm.reduce_a(i, fins[i])
        pend_a = (i, handle)
    return lpart, dy, fins, reduced, pend_a, dbiases


def _finish_layer_grads(i, g, inp):
    out = {n: g[n] for n in BIG if n != 'w_in'}
    out['w_in'] = g['w_in_r']
    out['pool_w'] = jnp.stack([g['pool_wbd'][k * POOL_GDIM:(k + 1) * POOL_GDIM, k * POOL_GDIM:(k + 1) * POOL_GDIM] for k in range(4)])
    out['pool_b'] = g['pool_b'].reshape(4, POOL_GDIM)
    out['pool_scale'] = g['pool_scale'].reshape(POOLW)
    out['conv_w'] = g['conv_w']
    out['conv_b'] = g['conv_b'].reshape(2048)
    out['dt_bias'] = g['dt_bias'].reshape(16)
    out['a_log'] = (g['a_neg'].reshape(16)) * (-jnp.exp(inp['a_log'][i]))
    out['d_skip'] = g['d_skip'].reshape(16)
    out['ssd_norm'] = g['ssd_norm'].reshape(D)
    out['gate_b'] = g['gate_b'].reshape(3, D)
    for n in ('ln1_g', 'ln1_b', 'ln2_g', 'ln2_b', 'ln3_g', 'ln3_b'):
        out[n] = g[n].reshape(D)
    return out


def kernel(x, ffn1_w13, ffn1_w2, ln1_g, ln1_b, w_in, gate_b, pool_w, pool_b, pool_scale, conv_w, conv_b,
           dt_bias, a_log, d_skip, ssd_norm, rel_bias, p_pool, p_ssd, p_attn, w_out, ln2_g, ln2_b, ffn2_w13,
           ffn2_w2, ln3_g, ln3_b, loss_target, m_ffn1_w13, m_ffn1_w2, m_ln1_g, m_ln1_b, m_w_in, m_gate_b,
           m_pool_w, m_pool_b, m_pool_scale, m_conv_w, m_conv_b, m_dt_bias, m_a_log, m_d_skip, m_ssd_norm,
           m_rel_bias, m_p_pool, m_p_ssd, m_p_attn, m_w_out, m_ln2_g, m_ln2_b, m_ffn2_w13, m_ffn2_w2, m_ln3_g,
           m_ln3_b, v_ffn1_w13, v_ffn1_w2, v_ln1_g, v_ln1_b, v_w_in, v_gate_b, v_pool_w, v_pool_b,
           v_pool_scale, v_conv_w, v_conv_b, v_dt_bias, v_a_log, v_d_skip, v_ssd_norm, v_rel_bias, v_p_pool,
           v_p_ssd, v_p_attn, v_w_out, v_ln2_g, v_ln2_b, v_ffn2_w13, v_ffn2_w2, v_ln3_g, v_ln3_b):
    inp = dict(locals())
    maps = jnp.asarray(_bucket_maps())
    bias_all = _bias_build(rel_bias, maps, name="bias_build")
    comm = _Comm(inp)
    lpart, gx, fins, red, pending, dbiases = _forward_backward(inp, comm, bias_all)
    loss = lax.psum(lpart[0, 0], ("x", "y", "c"))

    small_l = [n for n in SMALL if n != 'rel_bias']
    drel = _bias_reduce(jnp.stack(dbiases), maps, name="bias_reduce")[:, 0, :32].T
    small_arrs = [jnp.stack([fins[i][n] for i in range(NL)]) for n in small_l] + [drel]
    packed = _allreduce_small(_pack(small_arrs))
    handle_b, _ = comm.reduce_b(pending[1], packed)
    gsmall = dict(zip(small_l + ['rel_bias'], _unpack(packed, [a.shape for a in small_arrs])))
    shard = 2 * lax.axis_index("x") + lax.axis_index("y")
    gsmall['gate_b'] = lax.dynamic_slice_in_dim(gsmall['gate_b'], shard * 256, 256, axis=2)
    gsmall['conv_w'] = lax.dynamic_slice_in_dim(gsmall['conv_w'], shard * 512, 512, axis=2)
    gout, delta, new_m, new_v = dict(gsmall), {}, {}, {}
    shapes = [inp[n].shape for n in SMALL]
    d, m, v = _adamw(_pack([inp[n] for n in SMALL]), _pack([gsmall[n] for n in SMALL]),
                     _pack([inp['m_' + n] for n in SMALL]), _pack([inp['v_' + n] for n in SMALL]), name="adamw_small")
    for n, dd, mm, vv in zip(SMALL, _unpack(d, shapes), _unpack(m, shapes), _unpack(v, shapes)):
        delta[n], new_m[n], new_v[n] = dd, mm, vv

    two_d = lambda a: a.reshape(a.shape[0] * a.shape[1], a.shape[2])
    accs = {n: None for n in BIG}

    def adamw_layer(i):
        for n in BIG:
            accs[n] = _adamw_layer(i, two_d(inp[n]), red[i][n], two_d(inp['m_' + n]), two_d(inp['v_' + n]), accs[n],
                                   name="adamw_big")

    done = [i for i in range(NL) if i != pending[0]]
    for i in done:
        adamw_layer(i)
    red[pending[0]] = comm.reduce_c(handle_b, [d] + ([accs[n][1] for n in BIG] if done else []))
    adamw_layer(pending[0])
    for n in BIG:
        gout[n], delta[n], new_m[n], new_v[n] = [a.reshape(inp[n].shape) for a in accs[n]]

    return (loss, gx.reshape(x.shape), *[gout[n] for n in WEIGHTS], *[delta[n] for n in WEIGHTS],
            *[new_m[n] for n in WEIGHTS], *[new_v[n] for n in WEIGHTS])
```

```python
import functools

import numpy as np
import jax
import jax.numpy as jnp
from jax import lax
from jax.experimental import pallas as pl
from jax.experimental.pallas import tpu as pltpu

F32 = jnp.float32
BF16 = jnp.bfloat16
_MXU = jnp.bfloat16
_ACT = jnp.bfloat16
_VMEM_LIMIT = 56 * 1024 * 1024

S = 2048
D = 1024
NL = 4
DFF = 2816
LN_EPS = 1e-5
SSD_EPS = 1e-5
ALPHA = (2.0 * NL) ** 0.25
POOLW = 768
POOL_WINDOWS = (2, 4, 8, 16)
POOL_GDIM = 192
CH = 128
ATTN_DILS = (1, 4, 16)
HC = 9728
O_U, O_Z, O_XBC, O_Q, O_K, O_V, O_G, O_DT = 0, 768, 1792, 3840, 4608, 5376, 6144, 9216

ADAM_LR, ADAM_B1, ADAM_B2, ADAM_EPS, ADAM_WD, ADAM_STEP = 0.001, 0.9, 0.999, 1e-08, 0.01, 10

WEIGHTS = ['ffn1_w13', 'ffn1_w2', 'ln1_g', 'ln1_b', 'w_in', 'gate_b', 'pool_w', 'pool_b', 'pool_scale', 'conv_w',
           'conv_b', 'dt_bias', 'a_log', 'd_skip', 'ssd_norm', 'rel_bias', 'p_pool', 'p_ssd', 'p_attn', 'w_out',
           'ln2_g', 'ln2_b', 'ffn2_w13', 'ffn2_w2', 'ln3_g', 'ln3_b']
BIG = ['ffn1_w13', 'ffn1_w2', 'w_in', 'p_pool', 'p_ssd', 'p_attn', 'w_out', 'ffn2_w13', 'ffn2_w2']
COL_SHARDED = {'ffn1_w13', 'ffn2_w13', 'w_in', 'p_pool', 'p_attn'}
SMALL = [n for n in WEIGHTS if n not in BIG]


def _pcall(body, **kw):
    return pl.pallas_call(body, **kw)


def _cp(sem=None):
    return pltpu.CompilerParams(dimension_semantics=sem, vmem_limit_bytes=_VMEM_LIMIT)


def _pick(n, cands):
    for c in cands:
        if n % c == 0:
            return c
    raise ValueError(f"no tile for {n}")


def _mm(a, b, *, ta=False, tb=False, add=None, out_dtype=F32, dep=None, name):
    if ta:
        K, M = a.shape
    else:
        M, K = a.shape
    if tb:
        N, K2 = b.shape
    else:
        K2, N = b.shape
    assert K == K2, (a.shape, b.shape, ta, tb)
    sa, sb, so = a.dtype.itemsize, b.dtype.itemsize, jnp.dtype(out_dtype).itemsize
    tm, tn, tk = _mm_tiles(M, N, K, sa, sb, so + (4 if add is not None else 0))
    nk = K // tk
    a_bytes, b_bytes = M * K * sa, K * N * sb
    j_outer = nk == 1 and (b_bytes + a_bytes * (N // tn) < a_bytes + b_bytes * (M // tm))
    ij = (lambda p, q: (q, p)) if j_outer else (lambda p, q: (p, q))

    def im(f):
        return lambda p, q, k: f(*ij(p, q), k)

    a_spec = pl.BlockSpec((tk, tm), im(lambda i, j, k: (k, i))) if ta else pl.BlockSpec((tm, tk), im(lambda i, j, k: (i, k)))
    b_spec = pl.BlockSpec((tn, tk), im(lambda i, j, k: (j, k))) if tb else pl.BlockSpec((tk, tn), im(lambda i, j, k: (k, j)))
    o_spec = pl.BlockSpec((tm, tn), im(lambda i, j, k: (i, j)))
    dims = (((0 if ta else 1,), (1 if tb else 0,)), ((), ()))
    has_add = add is not None

    n_in = 2 + int(has_add) + int(dep is not None)

    def body(*refs):
        a_ref, b_ref = refs[0], refs[1]
        add_ref = refs[2] if has_add else None
        o_ref = refs[n_in]
        part = lax.dot_general(a_ref[...].astype(_MXU), b_ref[...].astype(_MXU), dims, preferred_element_type=F32)

        def finish(r):
            if has_add:
                r = r + add_ref[...]
            o_ref[...] = r.astype(out_dtype)

        if nk == 1:
            finish(part)
        else:
            acc = refs[-1]
            k = pl.program_id(2)

            @pl.when(k == 0)
            def _():
                acc[...] = part

            @pl.when(k > 0)
            def _():
                acc[...] += part

            @pl.when(k == nk - 1)
            def _():
                finish(acc[...])

    in_specs = [a_spec, b_spec]
    args = [a, b]
    if has_add:
        in_specs.append(o_spec)
        args.append(add)
    if dep is not None:
        in_specs.append(pl.BlockSpec(memory_space=pl.ANY))
        args.append(dep)
    gm, gn = M // tm, N // tn
    return _pcall(
        body, name=name, grid=((gn, gm, nk) if j_outer else (gm, gn, nk)), in_specs=in_specs, out_specs=o_spec,
        out_shape=jax.ShapeDtypeStruct((M, N), out_dtype),
        scratch_shapes=([pltpu.VMEM((tm, tn), F32)] if nk > 1 else []),
        compiler_params=_cp(("parallel", "parallel", "arbitrary")),
    )(*args)


_MM_VMEM_BUDGET = 40 * 1024 * 1024


def _divisors128(n, cap):
    return [d for d in range(128, min(n, cap) + 1, 128) if n % d == 0][::-1]


_MM_CYC_PER_MMAC = 4.35
_MM_CYC_PER_ACC_VREG = 2.03
_MM_HBM_BYTES_PER_CYC = 1455.0
_MM_CYC_PER_STEP = 770.0


def _mm_tiles(M, N, K, sa, sb, so):
    best = None
    for tm in _divisors128(M, 1408):
        for tn in _divisors128(N, 2560):
            for tk in ([K] if K <= 4096 else []) + _divisors128(K, 2816):
                nk = K // tk
                need = 2 * (tm * tk * sa + tk * tn * sb + tm * tn * so) + (tm * tn * 4 if nk > 1 else 0)
                need += tm * tk * 2 + tk * tn * 2 + tm * tn * 4
                if need > _MM_VMEM_BUDGET:
                    continue
                gm, gn = M // tm, N // tn
                a_bytes, b_bytes = M * K * sa, K * N * sb
                hbm = min(b_bytes + a_bytes * gn, a_bytes + b_bytes * gm) if nk == 1 else a_bytes * gn + b_bytes * gm
                hbm += M * N * so
                work = _MM_CYC_PER_MMAC * M * N * K / 1e6 + _MM_CYC_PER_ACC_VREG * (M * N / 1024) * (nk if nk > 1 else 0.5)
                cost = max(work, hbm / _MM_HBM_BYTES_PER_CYC) + gm * gn * nk * _MM_CYC_PER_STEP
                if best is None or cost < best[0]:
                    best = (cost, (tm, tn, tk))
    assert best is not None, (M, N, K)
    return best[1]


def _store(ref, val):
    if isinstance(val, (list, tuple)):
        off = 0
        for p in val:
            w = p.shape[1]
            ref[:, off:off + w] = p.astype(ref.dtype)
            off += w
    else:
        ref[...] = val.astype(ref.dtype)


def _acc_store(ref, val, first):
    pieces = val if isinstance(val, (list, tuple)) else [val]
    off = 0
    for p in pieces:
        w = p.shape[1]

        @pl.when(first)
        def _(p=p, off=off, w=w):
            ref[:, off:off + w] = p

        @pl.when(jnp.logical_not(first))
        def _(p=p, off=off, w=w):
            ref[:, off:off + w] += p

        off += w


def _rowwise(fn, tiled, full, out_tiled, out_acc, *, name, tm=256):
    arrs, specs = [], []
    for t in tiled:
        arr, w, cb = t if isinstance(t, tuple) else (t, t.shape[1], 0)
        arrs.append(arr)
        specs.append(pl.BlockSpec((tm, w), functools.partial(lambda i, cb: (i, cb), cb=cb)))
    R = arrs[0].shape[0]
    assert R % tm == 0
    for f in full:
        arrs.append(f)
        specs.append(pl.BlockSpec(f.shape, functools.partial(lambda i, nd: (0,) * nd, nd=f.ndim)))
    nt, nf, no = len(tiled), len(full), len(out_tiled)

    def body(*refs):
        tv = [r[...] for r in refs[:nt]]
        fv = [r[...] for r in refs[nt:nt + nf]]
        ot, oa = fn(tv, fv)
        for r, v in zip(refs[nt + nf:nt + nf + no], ot):
            _store(r, v)
        first = pl.program_id(0) == 0
        for r, v in zip(refs[nt + nf + no:], oa):
            _acc_store(r, v, first)

    out_shape = [jax.ShapeDtypeStruct((R, c), dt) for c, dt in out_tiled]
    out_specs = [pl.BlockSpec((tm, c), lambda i: (i, 0)) for c, _ in out_tiled]
    for shp in out_acc:
        out_shape.append(jax.ShapeDtypeStruct(shp, F32))
        out_specs.append(pl.BlockSpec(shp, lambda i: (0, 0)))
    return _pcall(body, name=name, grid=(R // tm,), in_specs=specs, out_specs=out_specs, out_shape=out_shape,
                  compiler_params=_cp(("arbitrary",)))(*arrs)


def _group(group):
    x, y, c = lax.axis_index("x"), lax.axis_index("y"), lax.axis_index("c")
    if group == "chips":
        return 2 * x + y, [((x, 1 - y, c), 2 * x + 1 - y), ((1 - x, y, c), 2 * (1 - x) + y),
                           ((1 - x, 1 - y, c), 2 * (1 - x) + 1 - y)]
    if group == "cores":
        return c, [((x, y, 1 - c), 1 - c)]
    if group == "x":
        return x, [((1 - x, y, c), 1 - x)]
    return y, [((x, 1 - y, c), 1 - y)]


def _exchange(arrs, group, mode, name, dep=None):
    chips = group == "chips"
    k = len(arrs)
    npeer = 3 if chips else 1

    def body(*refs):
        nd = 0 if dep is None else 1
        ins, outs = refs[:k], refs[k + nd:2 * k + nd]
        send_sems, recv_sems = refs[2 * k + nd:]
        me, peers = _group(group)
        remote = []
        for i in range(k):
            for p, (dev, slot) in enumerate(peers):
                src = ins[i].at[slot] if mode == "scatter" else ins[i]
                if not chips:
                    dst = outs[i]
                else:
                    dst = outs[i].at[p] if mode == "scatter" else outs[i].at[me]
                cp = pltpu.make_async_remote_copy(src_ref=src, dst_ref=dst, send_sem=send_sems.at[i, p],
                                                  recv_sem=recv_sems.at[i, p], device_id=dev,
                                                  device_id_type=pl.DeviceIdType.MESH)
                cp.start()
                remote.append(cp)
        for cp in remote:
            cp.wait_recv()
        for cp in remote:
            cp.wait_send()

    def oshape(a):
        piece = a.shape[1:] if mode == "scatter" else a.shape
        if chips:
            piece = ((3,) if mode == "scatter" else (4,)) + piece
        return jax.ShapeDtypeStruct(piece, a.dtype)

    any_spec = pl.BlockSpec(memory_space=pl.ANY)
    extra = [] if dep is None else [dep]
    return _pcall(body, name=name, in_specs=[any_spec] * (k + len(extra)), out_specs=[any_spec] * k,
                  out_shape=[oshape(a) for a in arrs],
                  scratch_shapes=[pltpu.SemaphoreType.DMA((k, npeer)), pltpu.SemaphoreType.DMA((k, npeer))])(*arrs, *extra)


def _split_copies(ins, lands, send_sems, recv_sems, group, mode):
    chips = group == "chips"
    me, peers = _group(group)
    npeer = len(peers)
    out = []
    for i in range(len(ins)):
        for p, (dev, slot) in enumerate(peers):
            src = ins[i].at[slot] if mode == "scatter" else ins[i]
            if not chips:
                dst = lands[i]
            else:
                dst = lands[i].at[p] if mode == "scatter" else lands[i].at[me]
            out.append(pltpu.make_async_remote_copy(src_ref=src, dst_ref=dst, send_sem=send_sems.at[npeer * i + p],
                                                    recv_sem=recv_sems.at[npeer * i + p], device_id=dev,
                                                    device_id_type=pl.DeviceIdType.MESH))
    return out


def _exchange_start(arrs, group, mode, name):
    k = len(arrs)
    chips = group == "chips"
    nsem = (3 if chips else 1) * k
    hbm = pl.BlockSpec(memory_space=pltpu.HBM)
    sem = pl.BlockSpec(memory_space=pltpu.SEMAPHORE)

    def land_shape(a):
        piece = a.shape[1:] if mode == "scatter" else a.shape
        if chips:
            piece = ((3,) if mode == "scatter" else (4,)) + piece
        return piece

    def body(*refs):
        ins, lands = refs[:k], refs[k:2 * k]
        send_sems, recv_sems = refs[2 * k], refs[2 * k + 1]
        token = refs[-1]
        for cp in _split_copies(ins, lands, send_sems, recv_sems, group, mode):
            cp.start()
        token[...] = jnp.zeros_like(token)

    srcs = [pltpu.with_memory_space_constraint(a, pltpu.HBM) for a in arrs]
    lands = [pltpu.with_memory_space_constraint(lax.empty(land_shape(a), a.dtype), pltpu.HBM) for a in arrs]
    out_shape = ([pltpu.SemaphoreType.DMA((nsem,)), pltpu.SemaphoreType.DMA((nsem,))]
                 + [pltpu.HBM(a.shape, a.dtype) for a in arrs] + [pltpu.HBM(land_shape(a), a.dtype) for a in arrs]
                 + [jax.ShapeDtypeStruct((8, 128), F32)])
    outs = _pcall(body, name=name, in_specs=[hbm] * (2 * k),
                  out_specs=[sem, sem] + [hbm] * (2 * k) + [pl.BlockSpec(memory_space=pltpu.VMEM)], out_shape=out_shape,
                  input_output_aliases={i: 2 + i for i in range(2 * k)},
                  compiler_params=pltpu.CompilerParams(has_side_effects=pltpu.SideEffectType.DATAFLOW_SIDE_EFFECTING))(
                      *srcs, *lands)
    return (outs[0], outs[1], list(outs[2:2 + k]), list(outs[2 + k:2 + 2 * k])), outs[-1]


def _exchange_wait(state, after, group, mode, name):
    send_sems, recv_sems, srcs, lands = state
    k = len(srcs)
    after = list(after) if isinstance(after, (list, tuple)) else [after]
    hbm = pl.BlockSpec(memory_space=pltpu.HBM)
    sem = pl.BlockSpec(memory_space=pltpu.SEMAPHORE)

    def body(*refs):
        ins, lnd = refs[:k], refs[k:2 * k]
        send_sems, recv_sems = refs[2 * k], refs[2 * k + 1]
        for cp in _split_copies(ins, lnd, send_sems, recv_sems, group, mode):
            cp.wait_send()
            cp.wait_recv()

    outs = _pcall(body, name=name,
                  in_specs=[hbm] * (2 * k) + [sem, sem] + [pl.BlockSpec(memory_space=pl.ANY)] * len(after),
                  out_specs=[hbm] * (2 * k),
                  out_shape=[pltpu.HBM(a.shape, a.dtype) for a in srcs] + [pltpu.HBM(a.shape, a.dtype) for a in lands],
                  input_output_aliases={i: i for i in range(2 * k)},
                  compiler_params=pltpu.CompilerParams(has_side_effects=pltpu.SideEffectType.DATAFLOW_SIDE_EFFECTING))(
                      *srcs, *lands, send_sems, recv_sems, *after)
    return list(outs[:k]), list(outs[k:])


def _sum_own_recv(own, recv, me, out_dtype, name):
    n, R, C = own.shape
    nr = 1 if recv.ndim == 2 else recv.shape[0]
    tr = _pick(R, (256, 128, 64, 32, 16, 8))

    def body(me_ref, own_ref, *refs):
        o_ref = refs[-1]
        acc = own_ref[...].astype(F32)
        for r in refs[:-1]:
            acc = acc + r[...].astype(F32)
        o_ref[...] = acc.astype(out_dtype)

    specs = [pl.BlockSpec((None, tr, C), lambda i, me_ref: (me_ref[0], i, 0))]
    args = [own]
    if recv.ndim == 2:
        specs.append(pl.BlockSpec((tr, C), lambda i, me_ref: (i, 0)))
        args.append(recv)
    else:
        for p in range(nr):
            specs.append(pl.BlockSpec((None, tr, C), functools.partial(lambda i, me_ref, p: (p, i, 0), p=p)))
            args.append(recv)
    gs = pltpu.PrefetchScalarGridSpec(num_scalar_prefetch=1, grid=(R // tr,), in_specs=specs,
                                      out_specs=pl.BlockSpec((tr, C), lambda i, me_ref: (i, 0)))
    return _pcall(body, name=name, grid_spec=gs, out_shape=jax.ShapeDtypeStruct((R, C), out_dtype),
                  compiler_params=_cp(("parallel",)))(me, *args)


def _silu(x):
    return x * jax.nn.sigmoid(x)


def _ln(r, g, b):
    mu = jnp.mean(r, -1, keepdims=True)
    xc = r - mu
    var = jnp.mean(xc * xc, -1, keepdims=True)
    return xc * lax.rsqrt(var + LN_EPS) * g + b


def _softplus(x):
    return jnp.maximum(x, 0.0) + jnp.log1p(jnp.exp(-jnp.abs(x)))


def _res_ln_fwd(x, y, g, b, res, name):
    def fn(tv, fv):
        r = ALPHA * tv[0] + res * tv[1]
        out = _ln(r, fv[0], fv[1])
        return [r, out, out], []
    return _rowwise(fn, [x, y], [g, b], [(D, F32), (D, F32), (D, _ACT)], [], name=name)


def _ln_bwd(r, g, b, dout, res, name):
    def fn(tv, fv):
        _, vjp = jax.vjp(_ln, tv[0], fv[0], fv[1])
        dr, dg, db = vjp(tv[1])
        return [ALPHA * dr, res * dr], [dg, db]
    return _rowwise(fn, [r, dout], [g, b], [(D, F32), (D, _ACT)], [(1, D), (1, D)], name=name)


def _swiglu_act(h, name):
    def fn(tv, fv):
        return [_silu(tv[0]) * tv[1]], []
    return _rowwise(fn, [(h, DFF, 0), (h, DFF, 1)], [], [(DFF, _ACT)], [], name=name)[0]


def _swiglu_act_bwd(h, ds, name):
    def fn(tv, fv):
        s, vjp = jax.vjp(lambda a, g: _silu(a) * g, tv[0], tv[1])
        da, dg = vjp(tv[2])
        return [[da, dg], s], []
    return _rowwise(fn, [(h, DFF, 0), (h, DFF, 1), ds], [], [(2 * DFF, _ACT), (DFF, _ACT)], [], name=name)


def _loss_fwd_bwd(y, tgt, name):
    def fn(tv, fv):
        e = tv[0] - tv[1]
        row = jnp.sum(e * e, axis=1, keepdims=True)
        tot = jnp.sum(row, axis=0, keepdims=True) * (0.5 / D)
        return [e * (1.0 / D)], [jnp.broadcast_to(tot, (1, 128))]
    return _rowwise(fn, [y, tgt], [], [(D, F32)], [(1, 128)], name=name)


def _shift_down(x, k, row):
    return jnp.where(row >= k, pltpu.roll(x, k, axis=0), 0.0)


def _shift_up(x, k, row):
    n = x.shape[0]
    return jnp.where(row < n - k, pltpu.roll(x, n - k, axis=0), 0.0)


def _pool_window_masks(j):
    lane = lax.broadcasted_iota(jnp.int32, (1, 128), 1) + j * 128
    grp = lane // POOL_GDIM
    return [grp == g for g in range(4)]


def _pool_mean(u, bwd, name, col0=0):
    T = u.shape[0]
    B = T // S

    def body(u_ref, o_ref):
        j = pl.program_id(1)
        x = u_ref[...]
        row = lax.broadcasted_iota(jnp.int32, (S, 1), 0)
        masks = _pool_window_masks(j)
        inv = [1.0 / jnp.minimum(row + 1, w).astype(F32) for w in POOL_WINDOWS]
        if not bwd:
            s2 = x + _shift_down(x, 1, row)
            s4 = s2 + _shift_down(s2, 2, row)
            s8 = s4 + _shift_down(s4, 4, row)
            s16 = s8 + _shift_down(s8, 8, row)
            mean = jnp.where(masks[0], s2 * inv[0], jnp.where(masks[1], s4 * inv[1],
                             jnp.where(masks[2], s8 * inv[2], s16 * inv[3])))
            o_ref[...] = (mean - x).astype(o_ref.dtype)
        else:
            g = [jnp.where(masks[i], x * inv[i], 0.0) for i in range(4)]
            t = g[3]
            t = t + _shift_up(t, 8, row) + g[2]
            t = t + _shift_up(t, 4, row) + g[1]
            t = t + _shift_up(t, 2, row) + g[0]
            t = t + _shift_up(t, 1, row)
            o_ref[...] = (t - x).astype(o_ref.dtype)

    spec = pl.BlockSpec((S, 128), lambda b, j: (b, j))
    return _pcall(body, name=name, grid=(B, POOLW // 128),
                  in_specs=[pl.BlockSpec((S, 128), lambda b, j: (b, j + col0))], out_specs=spec,
                  out_shape=jax.ShapeDtypeStruct((T, POOLW), _ACT), compiler_params=_cp(("parallel", "parallel")))(u)


def _conv_silu(xbc, w, b, name, col0=0):
    T, C = xbc.shape[0], w.shape[1]
    B = T // S

    def body(x_ref, w_ref, b_ref, o_ref):
        x = x_ref[...]
        row = lax.broadcasted_iota(jnp.int32, (S, 1), 0)
        c = b_ref[...] + w_ref[3:4, :] * x
        for s in range(1, 4):
            c = c + w_ref[3 - s:4 - s, :] * _shift_down(x, s, row)
        o_ref[...] = _silu(c)

    return _pcall(body, name=name, grid=(B, C // 128),
                  in_specs=[pl.BlockSpec((S, 128), lambda b, j: (b, j + col0)), pl.BlockSpec((4, 128), lambda b, j: (0, j)),
                            pl.BlockSpec((1, 128), lambda b, j: (0, j))],
                  out_specs=pl.BlockSpec((S, 128), lambda b, j: (b, j)),
                  out_shape=jax.ShapeDtypeStruct((T, C), F32), compiler_params=_cp(("parallel", "parallel")))(xbc, w, b)


def _conv_silu_bwd(xbc, w, b, dact, name, col0=0):
    T, C = xbc.shape[0], w.shape[1]
    B = T // S

    def body(x_ref, w_ref, b_ref, d_ref, dx_ref, dw_ref, db_ref):
        bi = pl.program_id(1)
        x = x_ref[...]
        row = lax.broadcasted_iota(jnp.int32, (S, 1), 0)
        xs = [x] + [_shift_down(x, s, row) for s in range(1, 4)]
        c = b_ref[...]
        for s in range(4):
            c = c + w_ref[3 - s:4 - s, :] * xs[s]
        _, vjp = jax.vjp(_silu, c)
        dc = vjp(d_ref[...])[0]
        dx = w_ref[3:4, :] * dc
        for s in range(1, 4):
            dx = dx + w_ref[3 - s:4 - s, :] * _shift_up(dc, s, row)
        dx_ref[...] = dx.astype(dx_ref.dtype)
        first = bi == 0
        for s in range(4):
            _acc_rows(dw_ref, 3 - s, jnp.sum(dc * xs[s], axis=0, keepdims=True), first)
        _acc_rows(db_ref, 0, jnp.sum(dc, axis=0, keepdims=True), first)

    blk = pl.BlockSpec((S, 128), lambda j, b: (b, j))
    return _pcall(body, name=name, grid=(C // 128, B),
                  in_specs=[pl.BlockSpec((S, 128), lambda j, b: (b, j + col0)), pl.BlockSpec((4, 128), lambda j, b: (0, j)),
                            pl.BlockSpec((1, 128), lambda j, b: (0, j)), blk],
                  out_specs=[blk, pl.BlockSpec((4, 128), lambda j, b: (0, j)), pl.BlockSpec((1, 128), lambda j, b: (0, j))],
                  out_shape=[jax.ShapeDtypeStruct((T, C), _ACT), jax.ShapeDtypeStruct((4, C), F32),
                             jax.ShapeDtypeStruct((1, C), F32)],
                  compiler_params=_cp(("parallel", "arbitrary")))(xbc, w, b, dact)


def _acc_rows(ref, r, val, first):
    @pl.when(first)
    def _():
        ref[r:r + 1, :] = val

    @pl.when(jnp.logical_not(first))
    def _():
        ref[r:r + 1, :] += val


def _tri_consts():
    i = lax.broadcasted_iota(jnp.int32, (CH, CH), 0)
    j = lax.broadcasted_iota(jnp.int32, (CH, CH), 1)
    return (i == j).astype(F32), (j <= i).astype(F32), (i <= j).astype(F32), i >= j


def _ssd_chunk(h, x, dt, Bm, Cm, a, dsk, consts):
    eye, tril, triu, lower = consts
    Bb = Bm.astype(_MXU)
    Cb = Cm.astype(_MXU)
    cb = lax.dot_general(Cb, Bb, (((1,), (1,)), ((), ())), preferred_element_type=F32)
    ys, hn = [], []
    for e in range(4):
        adt = dt[e] * a[e]
        adt_row = jnp.sum(adt * eye, axis=0, keepdims=True)
        cs_col = jnp.sum(adt_row * tril, axis=1, keepdims=True)
        cs_row = jnp.sum(adt * triu, axis=0, keepdims=True)
        cs_last = jnp.sum(adt, axis=0, keepdims=True)
        decay = jnp.exp(jnp.where(lower, cs_col - cs_row, -jnp.inf))
        xb = (x[e] * dt[e]).astype(_MXU)
        y_diag = jnp.dot((cb * decay).astype(_MXU), xb, preferred_element_type=F32)
        bdec = (Bm * jnp.exp(cs_last - cs_col)).astype(_MXU)
        st = lax.dot_general(bdec, xb, (((0,), (0,)), ((), ())), preferred_element_type=F32)
        hn.append(h[e] * jnp.exp(cs_last) + st)
        y_off = jnp.exp(cs_col) * jnp.dot(Cb, h[e].astype(_MXU), preferred_element_type=F32)
        ys.append(y_diag + y_off + dsk[e] * x[e])
    return ys, hn


def _ssd_specs(order):
    def im(f):
        return lambda p, q: f(*order(p, q))
    xs = pl.BlockSpec((S, 256), im(lambda b, g: (b, g)))
    dt = pl.BlockSpec((None, S, 4), im(lambda b, g: (g, b, 0)))
    bc = pl.BlockSpec((S, 128), im(lambda b, g: (b, g)))
    hd = pl.BlockSpec((None, 1, 4), im(lambda b, g: (g, 0, 0)))
    hs = pl.BlockSpec((None, None, S // CH, 4, 128, 64), im(lambda b, g: (b, g, 0, 0, 0, 0)))
    bw = pl.BlockSpec((S, 128), im(lambda b, g: (b, 8 + g)))
    cw = pl.BlockSpec((S, 128), im(lambda b, g: (b, 12 + g)))
    return xs, dt, bc, hd, hs, bw, cw


def _ssd_fwd(act, dtg, a, dsk, name):
    xs = bm = cm = act
    T = xs.shape[0]
    B = T // S
    nc = S // CH

    def body(x_ref, dt_ref, b_ref, c_ref, a_ref, k_ref, y_ref, hs_ref, h_ref):
        consts = _tri_consts()
        h_ref[...] = jnp.zeros_like(h_ref)
        al = [a_ref[:, e:e + 1] for e in range(4)]
        kl = [k_ref[:, e:e + 1] for e in range(4)]

        def step(c, carry):
            r0 = pl.multiple_of(c * CH, CH)
            rows = pl.ds(r0, CH)
            h = [h_ref[e] for e in range(4)]
            for e in range(4):
                hs_ref[c, e] = h[e]
            x = [x_ref[rows, 64 * e:64 * e + 64] for e in range(4)]
            dt = [dt_ref[rows, e:e + 1] for e in range(4)]
            ys, hn = _ssd_chunk(h, x, dt, b_ref[rows, :], c_ref[rows, :], al, kl, consts)
            for e in range(4):
                y_ref[rows, 64 * e:64 * e + 64] = ys[e]
                h_ref[e] = hn[e]
            return carry

        lax.fori_loop(0, nc, step, 0)

    sx, sdt, sbc, shd, shs, sbw, scw = _ssd_specs(lambda b, g: (b, g))
    return _pcall(body, name=name, grid=(B, 4), in_specs=[sx, sdt, sbw, scw, shd, shd], out_specs=[sx, shs],
                  out_shape=[jax.ShapeDtypeStruct((T, 1024), F32), jax.ShapeDtypeStruct((B, 4, nc, 4, 128, 64), F32)],
                  scratch_shapes=[pltpu.VMEM((4, 128, 64), F32)],
                  compiler_params=_cp(("parallel", "parallel")))(xs, dtg, bm, cm, a, dsk)


def _lane_place(vals, width):
    lane = lax.broadcasted_iota(jnp.int32, (1, width), 1)
    out = jnp.zeros((1, width), F32)
    for e, v in enumerate(vals):
        out = out + jnp.where(lane == e, v, 0.0)
    return out


def _ssd_bwd(act, dtg, a, dsk, hs, dy, name):
    xs = bm = cm = act
    T = xs.shape[0]
    B = T // S
    nc = S // CH

    def body(x_ref, dt_ref, b_ref, c_ref, a_ref, k_ref, hs_ref, dy_ref,
             dx_ref, ddt_ref, db_ref, dc_ref, dak_ref, dh_ref, sc_ref):
        bi = pl.program_id(1)
        consts = _tri_consts()
        dh_ref[...] = jnp.zeros_like(dh_ref)
        sc_ref[...] = jnp.zeros_like(sc_ref)
        al = [a_ref[:, e:e + 1] for e in range(4)]
        kl = [k_ref[:, e:e + 1] for e in range(4)]

        def step(i, carry):
            c = nc - 1 - i
            r0 = pl.multiple_of(c * CH, CH)
            rows = pl.ds(r0, CH)
            h = [hs_ref[c, e] for e in range(4)]
            x = [x_ref[rows, 64 * e:64 * e + 64] for e in range(4)]
            dt = [dt_ref[rows, e:e + 1] for e in range(4)]
            f = functools.partial(_ssd_chunk, consts=consts)
            _, vjp = jax.vjp(f, h, x, dt, b_ref[rows, :], c_ref[rows, :], al, kl)
            dys = [dy_ref[rows, 64 * e:64 * e + 64] for e in range(4)]
            dhn = [dh_ref[e] for e in range(4)]
            dh, dx, ddt, dB, dC, da, dk = vjp((dys, dhn))
            for e in range(4):
                dh_ref[e] = dh[e]
                dx_ref[rows, 64 * e:64 * e + 64] = dx[e]
                ddt_ref[rows, e:e + 1] = ddt[e]
            db_ref[rows, :] = dB
            dc_ref[rows, :] = dC
            sc_ref[0:1, :] += _lane_place(da, 128)
            sc_ref[1:2, :] += _lane_place(dk, 128)
            return carry

        lax.fori_loop(0, nc, step, 0)
        first = bi == 0

        @pl.when(first)
        def _():
            dak_ref[...] = sc_ref[...]

        @pl.when(jnp.logical_not(first))
        def _():
            dak_ref[...] += sc_ref[...]

    sx, sdt, sbc, shd, shs, sbw, scw = _ssd_specs(lambda g, b: (b, g))
    return _pcall(body, name=name, grid=(4, B), in_specs=[sx, sdt, sbw, scw, shd, shd, shs, sx],
                  out_specs=[sx, sdt, sbc, sbc, pl.BlockSpec((None, 8, 128), lambda g, b: (g, 0, 0))],
                  out_shape=[jax.ShapeDtypeStruct((T, 1024), F32), jax.ShapeDtypeStruct((4, T, 4), F32),
                             jax.ShapeDtypeStruct((T, 512), F32), jax.ShapeDtypeStruct((T, 512), F32),
                             jax.ShapeDtypeStruct((4, 8, 128), F32)],
                  scratch_shapes=[pltpu.VMEM((4, 128, 64), F32), pltpu.VMEM((8, 128), F32)],
                  compiler_params=_cp(("parallel", "arbitrary")))(xs, dtg, bm, cm, a, dsk, hs, dy)


def _gate_norm(y, z, nw):
    t = y * _silu(z)
    return t * lax.rsqrt(jnp.mean(t * t, axis=-1, keepdims=True) + SSD_EPS) * nw


def _ssd_gate_norm(y, z, nw, name, zcol=0):
    def fn(tv, fv):
        return [[_gate_norm(tv[g], tv[4 + g], fv[0][:, 256 * g:256 * g + 256]) for g in range(4)]], []
    tiled = [(y, 256, g) for g in range(4)] + [(z, 256, zcol + g) for g in range(4)]
    return _rowwise(fn, tiled, [nw], [(1024, _ACT)], [], name=name)[0]


def _ssd_gate_norm_bwd(y, z, nw, dout, name, zcol=0):
    def fn(tv, fv):
        dys, dzs, dns = [], [], []
        for g in range(4):
            _, vjp = jax.vjp(_gate_norm, tv[g], tv[4 + g], fv[0][:, 256 * g:256 * g + 256])
            a, b, c = vjp(tv[8 + g])
            dys.append(a)
            dzs.append(b)
            dns.append(c)
        return [dys, dzs], [dns]
    tiled = [(y, 256, g) for g in range(4)] + [(z, 256, zcol + g) for g in range(4)] + [(dout, 256, g) for g in range(4)]
    return _rowwise(fn, tiled, [nw], [(1024, F32), (1024, _ACT)], [(1, 1024)], name=name)


def _t5_bucket_np(dist):
    dist = np.maximum(dist, 0)
    max_exact = 16
    large = max_exact + (np.log(np.maximum(dist, 1) / max_exact) / np.log(2048 / max_exact) * (32 - max_exact)).astype(np.int32)
    large = np.minimum(large, 31)
    return np.where(dist < max_exact, dist, large).astype(np.int32)


def _bucket_maps():
    qi = np.arange(128)[:, None]
    kj = np.arange(256)[None, :]
    return np.stack([_t5_bucket_np((qi - kj + 128) * dil) for dil in ATTN_DILS]).astype(np.int32)


def _bias_build(rel_bias, maps, name):
    def body(tab_ref, map_ref, o_ref):
        hh = pl.program_id(0)
        m = map_ref[...]
        acc = jnp.zeros((128, 256), F32)
        for b in range(32):
            acc = jnp.where(m == b, tab_ref[b, hh], acc)
        o_ref[...] = acc

    return _pcall(body, name=name, grid=(12,),
                  in_specs=[pl.BlockSpec(memory_space=pltpu.SMEM), pl.BlockSpec((None, 128, 256), lambda h: (h // 4, 0, 0))],
                  out_specs=pl.BlockSpec((None, 128, 256), lambda h: (h, 0, 0)),
                  out_shape=jax.ShapeDtypeStruct((12, 128, 256), F32), compiler_params=_cp(("parallel",)))(rel_bias, maps)


def _bias_reduce(dbias, maps, name):
    nl = dbias.shape[0]

    def body(d_ref, map_ref, o_ref):
        m = map_ref[...]
        d = d_ref[0]
        for i in range(1, nl):
            d = d + d_ref[i]
        lane = lax.broadcasted_iota(jnp.int32, (1, 128), 1)
        out = jnp.zeros((1, 128), F32)
        for b in range(32):
            s = jnp.sum(jnp.sum(jnp.where(m == b, d, 0.0), axis=1, keepdims=True), axis=0, keepdims=True)
            out = out + jnp.where(lane == b, s, 0.0)
        o_ref[...] = out

    return _pcall(body, name=name, grid=(12,),
                  in_specs=[pl.BlockSpec((nl, None, 128, 256), lambda h: (0, h, 0, 0)),
                            pl.BlockSpec((None, 128, 256), lambda h: (h // 4, 0, 0))],
                  out_specs=pl.BlockSpec((None, 1, 128), lambda h: (h, 0, 0)),
                  out_shape=jax.ShapeDtypeStruct((12, 1, 128), F32), compiler_params=_cp(("parallel",)))(dbias, maps)


def _attn_block(q, kb, vb, bias, mask):
    s = lax.dot_general(q.astype(_MXU), kb.astype(_MXU), (((1,), (1,)), ((), ())), preferred_element_type=F32) * 0.125 + bias
    s = jnp.where(mask, s, -jnp.inf)
    m = lax.stop_gradient(jnp.max(s, axis=-1, keepdims=True))
    p = jnp.exp(s - m)
    den = jnp.sum(p, axis=-1, keepdims=True)
    out = jnp.dot((p / den).astype(_MXU), vb.astype(_MXU), preferred_element_type=F32)
    return out, m + jnp.log(den)


ATTN_QB = 512


def _attn_masks(dil):
    qi = lax.broadcasted_iota(jnp.int32, (ATTN_QB, ATTN_QB + 128), 0)
    kj = lax.broadcasted_iota(jnp.int32, (ATTN_QB, ATTN_QB + 128), 1)
    band = (kj >= qi) & (kj <= qi + 128)
    if dil == 16:
        q2 = lax.broadcasted_iota(jnp.int32, (ATTN_QB, ATTN_QB), 0)
        k2 = lax.broadcasted_iota(jnp.int32, (ATTN_QB, ATTN_QB), 1)
        return ((q2 // 128) == (k2 // 128)) & (k2 <= q2), None
    return band[:, 128:], band


def _attn_wide_bias(b, dil):
    if dil == 16:
        return jnp.tile(b[:, 128:], (4, 4)), None
    z = jnp.zeros((128, 128), F32)
    band = jnp.concatenate([jnp.concatenate([z] * i + [b] + [z] * (3 - i), axis=1) for i in range(4)], axis=0)
    return band[:, 128:], band


def _fold_dbias(dbs, dil, band_form):
    def blk(i, j):
        return dbs[128 * i:128 * i + 128, 128 * j:128 * j + 128]
    if band_form:
        return sum(blk(i, i) for i in range(4)), sum(blk(i, i + 1) for i in range(4))
    cur = sum(blk(i, i) for i in range(4))
    if dil == 16:
        return None, cur
    return sum(blk(i, i - 1) for i in range(1, 4)), cur


def _attn_chunks(dil):
    out = []
    for n in range(S // ATTN_QB):
        if dil == 1 and n > 0:
            out.append((n * ATTN_QB, n * ATTN_QB - 128, ATTN_QB + 128, True))
        else:
            out.append((n * ATTN_QB, n * ATTN_QB, ATTN_QB, False))
    return out


def _qkv_specs(gi, order):
    def spec(base):
        col = (base + 256 * gi) // 128
        return pl.BlockSpec((S, 128), lambda p, q: (order(p, q)[0], col + order(p, q)[1]))
    return [spec(O_Q), spec(O_K), spec(O_V)]


def _residue_rows(r, dil):
    return pl.ds(r, S // dil, stride=dil)


def _attn_fwd(hcat, bias_all, gi, name):
    dil = ATTN_DILS[gi]
    T = hcat.shape[0]
    B, L = T // S, S // dil

    def body(q_ref, k_ref, v_ref, b_ref, o_ref, l_ref, *scr):
        mask_first, mask_band = _attn_masks(dil)
        if dil > 1:
            qs, ks, vs, os_, ls = scr
            for r in range(dil):
                rows, dst = _residue_rows(r, dil), pl.ds(r * L, L)
                qs[dst, :] = q_ref[rows, :]
                ks[dst, :] = k_ref[rows, :]
                vs[dst, :] = v_ref[rows, :]
        else:
            qs, ks, vs, os_, ls = q_ref, k_ref, v_ref, o_ref, l_ref
        ls[...] = jnp.zeros_like(ls)
        for e in range(2):
            lanes = slice(64 * e, 64 * e + 64)
            bias_first, bias_band = _attn_wide_bias(b_ref[e], dil)
            for q0, k0, kn, band_form in _attn_chunks(dil):
                cur, keys = pl.ds(q0, ATTN_QB), pl.ds(k0, kn)
                o, l = _attn_block(qs[cur, lanes], ks[keys, lanes], vs[keys, lanes],
                                   bias_band if band_form else bias_first, mask_band if band_form else mask_first)
                os_[cur, lanes] = o
                ls[cur, e:e + 1] = l
        if dil > 1:
            for r in range(dil):
                rows, src = _residue_rows(r, dil), pl.ds(r * L, L)
                o_ref[rows, :] = os_[src, :]
                l_ref[rows, :] = ls[src, :]

    scratch = [pltpu.VMEM((S, 128), F32)] * 5 if dil > 1 else []
    return _pcall(body, name=name, grid=(B, 2),
                  in_specs=_qkv_specs(gi, lambda b, hp: (b, hp))
                  + [pl.BlockSpec((2, 128, 256), lambda b, hp: (2 * gi + hp, 0, 0))],
                  out_specs=[pl.BlockSpec((S, 128), lambda b, hp: (b, hp)),
                             pl.BlockSpec((None, S, 128), lambda b, hp: (hp, b, 0))],
                  out_shape=[jax.ShapeDtypeStruct((T, 256), F32), jax.ShapeDtypeStruct((2, T, 128), F32)],
                  scratch_shapes=scratch,
                  compiler_params=_cp(("parallel", "parallel")))(hcat, hcat, hcat, bias_all)


def _attn_bwd(hcat, bias_all, gi, do, dl, name):
    dil = ATTN_DILS[gi]
    T = hcat.shape[0]
    B, L = T // S, S // dil

    def body(q_ref, k_ref, v_ref, b_ref, do_ref, dl_ref, dq_ref, dk_ref, dv_ref, db_ref, acc_ref, *scr):
        bi = pl.program_id(1)
        mask_first, mask_band = _attn_masks(dil)
        if dil > 1:
            qs, ks, vs, dos, dls, dqs, dks, dvs = scr
            for r in range(dil):
                rows, dst = _residue_rows(r, dil), pl.ds(r * L, L)
                qs[dst, :] = q_ref[rows, :]
                ks[dst, :] = k_ref[rows, :]
                vs[dst, :] = v_ref[rows, :]
                dos[dst, :] = do_ref[rows, :]
                dls[dst, :] = dl_ref[rows, :]
        else:
            qs, ks, vs, dos, dls, dqs, dks, dvs = q_ref, k_ref, v_ref, do_ref, dl_ref, dq_ref, dk_ref, dv_ref
        dks[...] = jnp.zeros_like(dks)
        dvs[...] = jnp.zeros_like(dvs)
        for e in range(2):
            lanes = slice(64 * e, 64 * e + 64)
            bias_first, bias_band = _attn_wide_bias(b_ref[e], dil)
            acc_ref[...] = jnp.zeros_like(acc_ref)
            for q0, k0, kn, band_form in _attn_chunks(dil):
                cur, keys = pl.ds(q0, ATTN_QB), pl.ds(k0, kn)
                f = functools.partial(_attn_block, mask=mask_band if band_form else mask_first)
                _, vjp = jax.vjp(f, qs[cur, lanes], ks[keys, lanes], vs[keys, lanes],
                                 bias_band if band_form else bias_first)
                dq, dkb, dvb, dbs = vjp((dos[cur, lanes], dls[cur, e:e + 1]))
                dqs[cur, lanes] = dq
                dks[keys, lanes] += dkb
                dvs[keys, lanes] += dvb
                prev, here = _fold_dbias(dbs, dil, band_form)
                if prev is not None:
                    acc_ref[:, 0:128] += prev
                acc_ref[:, 128:256] += here

            @pl.when(bi == 0)
            def _(e=e):
                db_ref[e] = acc_ref[...]

            @pl.when(bi > 0)
            def _(e=e):
                db_ref[e] += acc_ref[...]

        if dil > 1:
            for r in range(dil):
                rows, src = _residue_rows(r, dil), pl.ds(r * L, L)
                dq_ref[rows, :] = dqs[src, :]
                dk_ref[rows, :] = dks[src, :]
                dv_ref[rows, :] = dvs[src, :]

    order = lambda hp, b: (b, hp)
    blk = pl.BlockSpec((S, 128), lambda hp, b: (b, hp))
    lblk = pl.BlockSpec((None, S, 128), lambda hp, b: (hp, b, 0))
    sds = jax.ShapeDtypeStruct((T, 256), F32)
    scratch = [pltpu.VMEM((128, 256), F32)] + ([pltpu.VMEM((S, 128), F32)] * 8 if dil > 1 else [])
    return _pcall(body, name=name, grid=(2, B),
                  in_specs=_qkv_specs(gi, order) + [pl.BlockSpec((2, 128, 256), lambda hp, b: (2 * gi + hp, 0, 0)), blk, lblk],
                  out_specs=[blk, blk, blk, pl.BlockSpec((2, 128, 256), lambda hp, b: (hp, 0, 0))],
                  out_shape=[sds, sds, sds, jax.ShapeDtypeStruct((4, 128, 256), F32)],
                  scratch_shapes=scratch,
                  compiler_params=_cp(("parallel", "arbitrary")))(hcat, hcat, hcat, bias_all, do, dl)


def _lse_merge(o0, o1, o2, l0, l1, l2):
    m = lax.stop_gradient(jnp.maximum(jnp.maximum(l0, l1), l2))
    e0, e1, e2 = jnp.exp(l0 - m), jnp.exp(l1 - m), jnp.exp(l2 - m)
    den = e0 + e1 + e2
    return (e0 / den) * o0 + (e1 / den) * o1 + (e2 / den) * o2


def _attn_merge(outs, lses, dy, name):
    T = outs[0].shape[0]
    bwd = dy is not None
    tm = 512

    def body(*refs):
        o_refs, l_refs = refs[:3], refs[3:6]
        if bwd:
            for r in refs[10:13]:
                r[...] = jnp.zeros_like(r)
        for e in range(2):
            lanes = slice(64 * e, 64 * e + 64)
            vals = [r[:, lanes] for r in o_refs] + [r[:, e:e + 1] for r in l_refs]
            if not bwd:
                refs[6][:, lanes] = _lse_merge(*vals).astype(refs[6].dtype)
            else:
                _, vjp = jax.vjp(_lse_merge, *vals)
                g = vjp(refs[6][:, lanes])
                for r, v in zip(refs[7:10], g[:3]):
                    r[:, lanes] = v
                for r, v in zip(refs[10:13], g[3:]):
                    r[:, e:e + 1] = v

    blk = pl.BlockSpec((tm, 128), lambda i, hp: (i, hp))
    lblk = pl.BlockSpec((None, tm, 128), lambda i, hp: (hp, i, 0))
    lsd = jax.ShapeDtypeStruct((2, T, 128), F32)
    if not bwd:
        return _pcall(body, name=name, grid=(T // tm, 2), in_specs=[blk] * 3 + [lblk] * 3, out_specs=blk,
                      out_shape=jax.ShapeDtypeStruct((T, 256), F32),
                      compiler_params=_cp(("parallel", "parallel")))(*outs, *lses)
    return _pcall(body, name=name, grid=(T // tm, 2), in_specs=[blk] * 3 + [lblk] * 3 + [blk],
                  out_specs=[blk] * 3 + [lblk] * 3, out_shape=[jax.ShapeDtypeStruct((T, 256), F32)] * 3 + [lsd] * 3,
                  compiler_params=_cp(("parallel", "parallel")))(*outs, *lses, dy)


def _gmerge(g0, g1, g2, gb, ya, yb, yc):
    return (jax.nn.sigmoid(g0 + gb[:, 0:D]) * ya + jax.nn.sigmoid(g1 + gb[:, D:2 * D]) * yb
            + jax.nn.sigmoid(g2 + gb[:, 2 * D:3 * D]) * yc)


def _gated_merge(gates, gb, ya, yb, yc, name, gcol=0):
    def fn(tv, fv):
        return [_gmerge(tv[0], tv[1], tv[2], fv[0], tv[3], tv[4], tv[5])], []
    return _rowwise(fn, [(gates, D, gcol), (gates, D, gcol + 1), (gates, D, gcol + 2), ya, yb, yc], [gb], [(D, _ACT)], [],
                    name=name)[0]


def _gated_merge_bwd(gates, gb, ya, yb, yc, dm, name, gcol=0):
    def fn(tv, fv):
        _, vjp = jax.vjp(_gmerge, tv[0], tv[1], tv[2], fv[0], tv[3], tv[4], tv[5])
        d0, d1, d2, dgb, da, db, dc = vjp(tv[6])
        return [[d0, d1, d2], da, db, dc], [dgb]
    return _rowwise(fn, [(gates, D, gcol), (gates, D, gcol + 1), (gates, D, gcol + 2), ya, yb, yc, dm], [gb],
                    [(3 * D, _ACT), (D, _ACT), (D, _ACT), (D, _ACT)], [(1, 3 * D)], name=name)


def _pool_affine(t1, pb, ps, dout, name):
    if dout is None:
        def fn(tv, fv):
            return [(tv[0] + fv[0]) * fv[1]], []
        return _rowwise(fn, [t1], [pb, ps], [(POOLW, _ACT)], [], name=name)[0]

    def fnb(tv, fv):
        t2, vjp = jax.vjp(lambda t, b, s: (t + b) * s, tv[0], fv[0], fv[1])
        dt, db, dsc = vjp(tv[1])
        return [dt, t2], [db, dsc]
    return _rowwise(fnb, [t1, dout], [pb, ps], [(POOLW, _ACT), (POOLW, _ACT)], [(1, POOLW), (1, POOLW)], name=name)


def _dt_softplus(dt_raw, dt_bias, ddt, name):
    f = lambda r, b: _softplus(r + b)
    if ddt is None:
        def fn(tv, fv):
            return [f(tv[0], fv[0])], []
        return _rowwise(fn, [dt_raw], [dt_bias], [(16, F32)], [], name=name, tm=1024)[0]

    def fnb(tv, fv):
        _, vjp = jax.vjp(f, tv[0], fv[0])
        dr, db = vjp(tv[1])
        return [dr], [db]
    return _rowwise(fnb, [dt_raw, ddt], [dt_bias], [(16, F32)], [(1, 16)], name=name, tm=1024)


def _adamw_math(wv, gv, mv, vv):
    c1 = 1.0 / (1.0 - ADAM_B1 ** ADAM_STEP)
    c2 = 1.0 / (1.0 - ADAM_B2 ** ADAM_STEP)
    mn = ADAM_B1 * mv + (1.0 - ADAM_B1) * gv
    vn = ADAM_B2 * vv + (1.0 - ADAM_B2) * (gv * gv)
    delta = -ADAM_LR * ((mn * c1) / (jnp.sqrt(vn * c2) + ADAM_EPS) + ADAM_WD * wv)
    return delta, mn, vn


def _adamw(w, g, m, v, name):
    R, C = w.shape
    tm = _pick(R, (256, 128, 64, 32, 16, 8))
    return _rowwise(lambda tv, fv: (list(_adamw_math(*tv)), []), [w, g, m, v], [], [(C, F32)] * 3, [], name=name, tm=tm)


def _adamw_layer(i, w, g, m, v, accs, name, dep=None):
    R, C = w.shape
    r = R // NL
    tm = _pick(r, (256, 128, 64, 32, 16, 8))
    nt = r // tm
    if accs is None:
        accs = [lax.empty((R, C), F32) for _ in range(4)]
    extra = [] if dep is None else [dep]

    def body(w_ref, g_ref, m_ref, v_ref, *rest):
        go_ref, do_ref, mo_ref, vo_ref = rest[-4:]
        gv = g_ref[...]
        delta, mn, vn = _adamw_math(w_ref[...], gv, m_ref[...], v_ref[...])
        go_ref[...] = gv
        do_ref[...] = delta
        mo_ref[...] = mn
        vo_ref[...] = vn

    slab = pl.BlockSpec((tm, C), lambda t: (i * nt + t, 0))
    anyspec = pl.BlockSpec(memory_space=pl.ANY)
    return _pcall(body, name=name, grid=(nt,),
                  in_specs=[slab, pl.BlockSpec((tm, C), lambda t: (t, 0)), slab, slab] + [anyspec] * (4 + len(extra)),
                  out_specs=[slab] * 4, out_shape=[jax.ShapeDtypeStruct((R, C), F32)] * 4,
                  input_output_aliases={4 + k: k for k in range(4)},
                  compiler_params=_cp(("parallel",)))(w, g, m, v, *accs, *extra)


def _ffn_fwd(x, xm, w13, w2, g, b, tag, dep=None):
    h = _mm(xm, w13, dep=dep, name=f"{tag}_h")
    s = _swiglu_act(h, name=f"{tag}_act")
    y = _mm(s, w2, name=f"{tag}_y")
    r, out, outm = _res_ln_fwd(x, y, g, b, 0.5, name=f"{tag}_ln")
    return out, outm, dict(x=xm, h=h, r=r)


def _ffn_bwd(dout, sv, w13, w2, g, b, tag, dep=None):
    dskip, dy, dg, db = _ln_bwd(sv['r'], g, b, dout, 0.5, name=f"{tag}_lnb")
    ds = _mm(dy, w2, tb=True, dep=dep, name=f"{tag}_ds")
    dh, s = _swiglu_act_bwd(sv['h'], ds, name=f"{tag}_actb")
    dw2 = _mm(s, dy, ta=True, name=f"{tag}_dw2")
    dw13 = _mm(sv['x'], dh, ta=True, name=f"{tag}_dw13")
    dx = _mm(dh, w13, tb=True, add=dskip, name=f"{tag}_dx")
    return dx, dict(w13=dw13, w2=dw2, g=dg, b=db)


def _mixer_fwd(x1, x1m, W, bias_all, tag, dep=None):
    T = x1.shape[0]
    hcat = _mm(x1m, W['w_in_r'], dep=dep, name=f"{tag}_hcat")
    dt_raw = hcat[:, O_DT:O_DT + 16]
    pooled = _pool_mean(hcat, False, name=f"{tag}_pool", col0=O_U // 128)
    t1 = _mm(pooled, W['pool_wbd'], name=f"{tag}_pt1")
    t2 = _pool_affine(t1, W['pool_b'], W['pool_scale'], None, name=f"{tag}_paff")
    ya = _mm(t2, W['p_pool'], name=f"{tag}_ya")
    act = _conv_silu(hcat, W['conv_w'], W['conv_b'], name=f"{tag}_conv", col0=O_XBC // 128)
    dt = _dt_softplus(dt_raw, W['dt_bias'], None, name=f"{tag}_dt")
    dtg = dt.reshape(T, 4, 4).transpose(1, 0, 2)
    yscan, hs = _ssd_fwd(act, dtg, W['a_neg'], W['d_skip'], name=f"{tag}_ssd")
    ybn = _ssd_gate_norm(yscan, hcat, W['ssd_norm'], name=f"{tag}_gn", zcol=O_Z // 256)
    yb = _mm(ybn, W['p_ssd'], name=f"{tag}_yb")
    outs, lses = [], []
    for gi in range(len(ATTN_DILS)):
        o, l = _attn_fwd(hcat, bias_all, gi, name=f"{tag}_attn{gi}")
        outs.append(o)
        lses.append(l)
    ycp = _attn_merge(outs, lses, None, name=f"{tag}_amerge")
    yc = _mm(ycp, W['p_attn'], name=f"{tag}_yc")
    merged = _gated_merge(hcat, W['gate_b'], ya, yb, yc, name=f"{tag}_gm", gcol=O_G // D)
    mix = _mm(merged, W['w_out'], name=f"{tag}_mix")
    r, out, outm = _res_ln_fwd(x1, mix, W['ln2_g'], W['ln2_b'], 1.0, name=f"{tag}_ln")
    sv = dict(x1=x1m, dt_raw=dt_raw, pooled=pooled, t1=t1, act=act, dtg=dtg,
              hs=hs, yscan=yscan, ybn=ybn, hcat=hcat, outs=outs, lses=lses, ycp=ycp, ya=ya, yb=yb, yc=yc,
              merged=merged, r=r)
    return out, outm, sv


def _mixer_bwd(dout, sv, W, bias_all, tag, dep=None):
    T = dout.shape[0]
    gr = {}
    dx1a, dr, gr['ln2_g'], gr['ln2_b'] = _ln_bwd(sv['r'], W['ln2_g'], W['ln2_b'], dout, 1.0, name=f"{tag}_lnb")
    dmerged = _mm(dr, W['w_out'], tb=True, dep=dep, name=f"{tag}_dmerged")
    gr['w_out'] = _mm(sv['merged'], dr, ta=True, name=f"{tag}_dwout")
    dgates, dya, dyb, dyc, gr['gate_b'] = _gated_merge_bwd(sv['hcat'], W['gate_b'], sv['ya'], sv['yb'], sv['yc'],
                                                           dmerged, name=f"{tag}_gmb", gcol=O_G // D)
    dycp = _mm(dyc, W['p_attn'], tb=True, name=f"{tag}_dycp")
    gr['p_attn'] = _mm(sv['ycp'], dyc, ta=True, name=f"{tag}_dpattn")
    dml = _attn_merge(sv['outs'], sv['lses'], dycp, name=f"{tag}_amergeb")
    dq, dk, dv, dbias = [], [], [], []
    for gi in range(len(ATTN_DILS)):
        a, b, c, d = _attn_bwd(sv['hcat'], bias_all, gi, dml[gi], dml[3 + gi], name=f"{tag}_attnb{gi}")
        dq.append(a)
        dk.append(b)
        dv.append(c)
        dbias.append(d)
    dbias = jnp.concatenate(dbias, axis=0)
    dybn = _mm(dyb, W['p_ssd'], tb=True, name=f"{tag}_dybn")
    gr['p_ssd'] = _mm(sv['ybn'], dyb, ta=True, name=f"{tag}_dpssd")
    dyscan, dz, gr['ssd_norm'] = _ssd_gate_norm_bwd(sv['yscan'], sv['hcat'], W['ssd_norm'], dybn, name=f"{tag}_gnb",
                                                    zcol=O_Z // 256)
    dxs, ddtg, dbm, dcm, dak = _ssd_bwd(sv['act'], sv['dtg'], W['a_neg'], W['d_skip'], sv['hs'], dyscan,
                                        name=f"{tag}_ssdb")
    gr['a_neg'], gr['d_skip'] = dak[:, 0, 0:4], dak[:, 1, 0:4]
    ddt = ddtg.transpose(1, 0, 2).reshape(T, 16)
    ddt_raw, gr['dt_bias'] = _dt_softplus(sv['dt_raw'], W['dt_bias'], ddt, name=f"{tag}_dtb")
    dact = jnp.concatenate([dxs, dbm, dcm], axis=1)
    dxbc, gr['conv_w'], gr['conv_b'] = _conv_silu_bwd(sv['hcat'], W['conv_w'], W['conv_b'], dact, name=f"{tag}_convb",
                                                      col0=O_XBC // 128)
    dt2 = _mm(dya, W['p_pool'], tb=True, name=f"{tag}_dt2")
    dt1, t2, gr['pool_b'], gr['pool_scale'] = _pool_affine(sv['t1'], W['pool_b'], W['pool_scale'], dt2, name=f"{tag}_paffb")
    gr['p_pool'] = _mm(t2, dya, ta=True, name=f"{tag}_dppool")
    dpooled = _mm(dt1, W['pool_wbd'], tb=True, name=f"{tag}_dpooled")
    gr['pool_wbd'] = _mm(sv['pooled'], dt1, ta=True, name=f"{tag}_dpoolw")
    du = _pool_mean(dpooled, True, name=f"{tag}_poolb")
    dhcat = jnp.concatenate([t.astype(_ACT) for t in [du, dz, dxbc] + dq + dk + dv + [dgates, ddt_raw]]
                            + [jnp.zeros((T, HC - O_DT - 16), _ACT)], axis=1)
    dx1 = _mm(dhcat, W['w_in_r'], tb=True, add=dx1a, name=f"{tag}_dx1")
    gr['w_in_r'] = _mm(sv['x1'], dhcat, ta=True, name=f"{tag}_dwin")
    return dx1, gr, dbias


def _prep_layer_weights(i, inp, G):
    W = {}
    for n in BIG:
        if n not in G:
            continue
        g = G[n]
        if n == 'w_in':
            W['w_in_r'] = jnp.concatenate(_nat_pieces(g, 0, 3840) + _nat_pieces(g, 3856, 9232) + _nat_pieces(g, 3840, 3856)
                                          + [jnp.zeros((D, HC - 9232), g.dtype)], axis=1)
        elif n in COL_SHARDED:
            W[n] = jnp.concatenate([g[j] for j in range(4)], axis=1)
        else:
            W[n] = g.reshape(4 * g.shape[1], g.shape[2])
    pw = inp['pool_w'][i].astype(_MXU)
    wbd = jnp.zeros((POOLW, POOLW), _MXU)
    for g in range(4):
        wbd = lax.dynamic_update_slice(wbd, pw[g], (g * POOL_GDIM, g * POOL_GDIM))
    W['pool_wbd'] = wbd
    W['pool_b'] = inp['pool_b'][i].reshape(1, POOLW)
    W['pool_scale'] = inp['pool_scale'][i].reshape(1, POOLW)
    if 'conv_w' in G:
        W['conv_w'] = jnp.concatenate([G['conv_w'][j] for j in range(4)], axis=1)
        W['gate_b'] = jnp.concatenate([G['gate_b'][j][b:b + 1] for b in range(3) for j in range(4)], axis=1)
    W['conv_b'] = inp['conv_b'][i].reshape(1, 2048)
    W['dt_bias'] = inp['dt_bias'][i].reshape(1, 16)
    W['a_neg'] = (-jnp.exp(inp['a_log'][i])).reshape(4, 1, 4)
    W['d_skip'] = inp['d_skip'][i].reshape(4, 1, 4)
    W['ssd_norm'] = inp['ssd_norm'][i].reshape(1, D)
    for n in ('ln1_g', 'ln1_b', 'ln2_g', 'ln2_b', 'ln3_g', 'ln3_b'):
        W[n] = inp[n][i].reshape(1, D)
    return W


GATHER_FIRST = ['ffn1_w13', 'ffn1_w2']
GATHER_REST = [n for n in BIG if n not in GATHER_FIRST] + ['gate_b', 'conv_w']


def _gather_start(inp, i, names):
    core = lax.axis_index("c")
    arrs = []
    for n in names:
        s = inp[n][i]
        if n in BIG:
            s = lax.dynamic_slice_in_dim(s, core * (s.shape[0] // 2), s.shape[0] // 2, axis=0).astype(BF16)
        arrs.append(s)
    state, token = _exchange_start(arrs, "chips", "gather", name="gather_start")
    return (names, state), token


def _gather_mid(handle, after):
    names, state = handle
    me = 2 * lax.axis_index("x") + lax.axis_index("y")
    own, outs = _exchange_wait(state, after, "chips", "gather", name="gather_wait")
    outs = [lax.dynamic_update_slice(o, a[None], (me, 0, 0)) for o, a in zip(outs, own)]
    big = [o for n, o in zip(names, outs) if n in BIG]
    state, token = _exchange_start(big, "cores", "gather", name="share_start")
    return (names, outs, state), token


def _gather_finish(handle, after):
    names, outs, state = handle
    core = lax.axis_index("c")
    mine, theirs = _exchange_wait(state, after, "cores", "gather", name="share_wait")
    G = {n: o for n, o in zip(names, outs) if n not in BIG}
    for n, a, b in zip([n for n in names if n in BIG], mine, theirs):
        G[n] = jnp.concatenate([jnp.where(core == 0, a, b), jnp.where(core == 0, b, a)], axis=1)
    return G


W_IN_SHARD = 2308


def _nat_pieces(g, lo, hi):
    out = []
    for j in range(4):
        s, e = max(lo, W_IN_SHARD * j), min(hi, W_IN_SHARD * (j + 1))
        if s < e:
            out.append(g[j][:, s - W_IN_SHARD * j:e - W_IN_SHARD * j])
    return out


def _reord_ranges(lo, hi):
    out = []
    for a, b, off in ((0, 3840, 0), (3840, 3856, O_DT - 3840), (3856, 9232, -16)):
        s, e = max(lo, a), min(hi, b)
        if s < e:
            out.append((s + off, e + off))
    return out


def _halves_of(n, g):
    if n == 'w_in':
        shards = [jnp.concatenate([g[:, a:b] for a, b in _reord_ranges(W_IN_SHARD * j, W_IN_SHARD * (j + 1))], axis=1)
                  for j in range(4)]
    elif n in COL_SHARDED:
        c = g.shape[1] // 4
        shards = [g[:, j * c:(j + 1) * c] for j in range(4)]
    else:
        r = g.shape[0] // 4
        shards = [g[j * r:(j + 1) * r] for j in range(4)]
    r2 = shards[0].shape[0] // 2
    return jnp.stack([jnp.concatenate([s[h * r2:(h + 1) * r2] for s in shards], axis=0) for h in range(2)])


def _reduce_a(grads):
    names = list(grads)
    halves = [_halves_of(n, grads[n]) for n in names]
    state, token = _exchange_start(halves, "cores", "scatter", name="rsc_start")
    return (names, state), token


def _reduce_b(handle, after):
    names, state = handle
    core = lax.axis_index("c").reshape(1)
    halves, got = _exchange_wait(state, after, "cores", "scatter", name="rsc_wait")
    chip = [_sum_own_recv(h, t, core, BF16, name="rs_sum2") for h, t in zip(halves, got)]
    chip = [t.reshape(4, t.shape[0] // 4, t.shape[1]) for t in chip]
    state, token = _exchange_start(chip, "chips", "scatter", name="rs_start")
    return (names, state), token


def _reduce_c(handle, after):
    names, state = handle
    chip_id = (2 * lax.axis_index("x") + lax.axis_index("y")).reshape(1)
    chip, got = _exchange_wait(state, after, "chips", "scatter", name="rs_wait")
    red = [_sum_own_recv(h, t, chip_id, F32, name="rs_sum4") for h, t in zip(chip, got)]
    other = _exchange(red, "cores", "gather", name="rs_share")
    out = {}
    for n, mine, theirs in zip(names, red, other):
        out[n] = jnp.where(lax.axis_index("c") == 0, jnp.concatenate([mine, theirs]), jnp.concatenate([theirs, mine]))
    return out


class _Comm:
    def __init__(self, inp):
        self.inp = inp

    def gather_start(self, i, names):
        return _gather_start(self.inp, i, names)

    gather_mid = staticmethod(_gather_mid)
    gather_finish = staticmethod(_gather_finish)

    def reduce_a(self, i, grads):
        return _reduce_a({n: grads[n] for n in BIG})

    reduce_b = staticmethod(_reduce_b)
    reduce_c = staticmethod(_reduce_c)


def _allreduce_small(vec, dep=None):
    for group in ("cores", "x", "y"):
        recv = _exchange([vec], group, "gather", name=f"ar_{group}", dep=dep if group == "cores" else None)[0]
        vec = _rowwise(lambda tv, fv: ([tv[0] + tv[1]], []), [vec, recv], [], [(128, F32)], [], name=f"ar_add_{group}")[0]
    return vec


def _pack(arrs):
    flat = jnp.concatenate([a.reshape(-1) for a in arrs])
    n = flat.shape[0]
    pad = (-n) % (256 * 128)
    flat = jnp.concatenate([flat, jnp.zeros((pad,), F32)])
    return flat.reshape(-1, 128)


def _unpack(p, shapes):
    flat = p.reshape(-1)
    out, off = [], 0
    for s in shapes:
        sz = int(np.prod(s))
        out.append(flat[off:off + sz].reshape(s))
        off += sz
    return out


def _forward_backward(inp, comm, bias_all):
    x = xm = inp['x'].reshape(-1, D)
    tgt = inp['loss_target'].reshape(-1, D)
    saved, Ws = [], []
    h_first, _ = comm.gather_start(0, GATHER_FIRST)
    h_rest, dep = comm.gather_start(0, GATHER_REST)
    h_first, tok = comm.gather_mid(h_first, x)
    G = comm.gather_finish(h_first, tok)
    for i in range(NL):
        W = _prep_layer_weights(i, inp, G)
        start_next = lambda: (comm.gather_start(i + 1, BIG + ['gate_b', 'conv_w']) if i + 1 < NL else (None, None))
        if i > 0:
            h_next, dep = start_next()
        x1, x1m, s1 = _ffn_fwd(x, xm, W['ffn1_w13'], W['ffn1_w2'], W['ln1_g'], W['ln1_b'], "f1", dep)
        if i == 0:
            h_rest, tok = comm.gather_mid(h_rest, x1m)
            W.update(_prep_layer_weights(i, inp, comm.gather_finish(h_rest, tok)))
            h_next, dep = start_next()
        x2, x2m, s2 = _mixer_fwd(x1, x1m, W, bias_all, "mx", dep if i == 0 else None)
        dep = None
        if h_next is not None:
            h_next, dep = comm.gather_mid(h_next, x2m)
        x, xm, s3 = _ffn_fwd(x2, x2m, W['ffn2_w13'], W['ffn2_w2'], W['ln3_g'], W['ln3_b'], "f2", dep)
        if h_next is not None:
            G = comm.gather_finish(h_next, xm)
        saved.append((s1, s2, s3))
        Ws.append(W)
    dy, lpart = _loss_fwd_bwd(x, tgt, name="loss")
    fins, reduced, dbiases = [None] * NL, [None] * NL, [None] * NL
    pend_a, pend_b, dep = None, None, None
    for i in reversed(range(NL)):
        W = Ws[i]
        s1, s2, s3 = saved[i]
        g = {}
        dx2, f = _ffn_bwd(dy, s3, W['ffn2_w13'], W['ffn2_w2'], W['ln3_g'], W['ln3_b'], "f2", dep)
        g['ffn2_w13'], g['ffn2_w2'], g['ln3_g'], g['ln3_b'] = f['w13'], f['w2'], f['g'], f['b']
        dep = None
        if pend_a is not None:
            handle, dep = comm.reduce_b(pend_a[1], dx2)
            pend_b = (pend_a[0], handle)
        dx1, gm, dbiases[i] = _mixer_bwd(dx2, s2, W, bias_all, "mx", dep)
        g.update(gm)
        dy, f = _ffn_bwd(dx1, s1, W['ffn1_w13'], W['ffn1_w2'], W['ln1_g'], W['ln1_b'], "f1")
        g['ffn1_w13'], g['ffn1_w2'], g['ln1_g'], g['ln1_b'] = f['w13'], f['w2'], f['g'], f['b']
        fins[i] = _finish_layer_grads(i, g, inp)
        if pend_b is not None:
            reduced[pend_b[0]] = comm.reduce_c(pend_b[1], dy)
            pend_b = None
        handle, dep = comm.reduce_a(i, fins[i])
        pend_a = (i, handle)
    return lpart, dy, fins, reduced, pend_a, dbiases


def _finish_layer_grads(i, g, inp):
    out = {n: g[n] for n in BIG if n != 'w_in'}
    out['w_in'] = g['w_in_r']
    out['pool_w'] = jnp.stack([g['pool_wbd'][k * POOL_GDIM:(k + 1) * POOL_GDIM, k * POOL_GDIM:(k + 1) * POOL_GDIM] for k in range(4)])
    out['pool_b'] = g['pool_b'].reshape(4, POOL_GDIM)
    out['pool_scale'] = g['pool_scale'].reshape(POOLW)
    out['conv_w'] = g['conv_w']
    out['conv_b'] = g['conv_b'].reshape(2048)
    out['dt_bias'] = g['dt_bias'].reshape(16)
    out['a_log'] = (g['a_neg'].reshape(16)) * (-jnp.exp(inp['a_log'][i]))
    out['d_skip'] = g['d_skip'].reshape(16)
    out['ssd_norm'] = g['ssd_norm'].reshape(D)
    out['gate_b'] = g['gate_b'].reshape(3, D)
    for n in ('ln1_g', 'ln1_b', 'ln2_g', 'ln2_b', 'ln3_g', 'ln3_b'):
        out[n] = g[n].reshape(D)
    return out


def kernel(x, ffn1_w13, ffn1_w2, ln1_g, ln1_b, w_in, gate_b, pool_w, pool_b, pool_scale, conv_w, conv_b,
           dt_bias, a_log, d_skip, ssd_norm, rel_bias, p_pool, p_ssd, p_attn, w_out, ln2_g, ln2_b, ffn2_w13,
           ffn2_w2, ln3_g, ln3_b, loss_target, m_ffn1_w13, m_ffn1_w2, m_ln1_g, m_ln1_b, m_w_in, m_gate_b,
           m_pool_w, m_pool_b, m_pool_scale, m_conv_w, m_conv_b, m_dt_bias, m_a_log, m_d_skip, m_ssd_norm,
           m_rel_bias, m_p_pool, m_p_ssd, m_p_attn, m_w_out, m_ln2_g, m_ln2_b, m_ffn2_w13, m_ffn2_w2, m_ln3_g,
           m_ln3_b, v_ffn1_w13, v_ffn1_w2, v_ln1_g, v_ln1_b, v_w_in, v_gate_b, v_pool_w, v_pool_b,
           v_pool_scale, v_conv_w, v_conv_b, v_dt_bias, v_a_log, v_d_skip, v_ssd_norm, v_rel_bias, v_p_pool,
           v_p_ssd, v_p_attn, v_w_out, v_ln2_g, v_ln2_b, v_ffn2_w13, v_ffn2_w2, v_ln3_g, v_ln3_b):
    inp = dict(locals())
    maps = jnp.asarray(_bucket_maps())
    bias_all = _bias_build(rel_bias, maps, name="bias_build")
    comm = _Comm(inp)
    lpart, gx, fins, red, pending, dbiases = _forward_backward(inp, comm, bias_all)
    loss = lax.psum(lpart[0, 0], ("x", "y", "c"))

    small_l = [n for n in SMALL if n != 'rel_bias']
    drel = _bias_reduce(jnp.stack(dbiases), maps, name="bias_reduce")[:, 0, :32].T
    small_arrs = [jnp.stack([fins[i][n] for i in range(NL)]) for n in small_l] + [drel]
    packed = _allreduce_small(_pack(small_arrs))
    handle_b, started = comm.reduce_b(pending[1], packed)
    gsmall = dict(zip(small_l + ['rel_bias'], _unpack(packed, [a.shape for a in small_arrs])))
    shard = 2 * lax.axis_index("x") + lax.axis_index("y")
    gsmall['gate_b'] = lax.dynamic_slice_in_dim(gsmall['gate_b'], shard * 256, 256, axis=2)
    gsmall['conv_w'] = lax.dynamic_slice_in_dim(gsmall['conv_w'], shard * 512, 512, axis=2)
    gout, delta, new_m, new_v = dict(gsmall), {}, {}, {}
    shapes = [inp[n].shape for n in SMALL]
    d, m, v = _adamw(_pack([inp[n] for n in SMALL]), _pack([gsmall[n] for n in SMALL]),
                     _pack([inp['m_' + n] for n in SMALL]), _pack([inp['v_' + n] for n in SMALL]), name="adamw_small")
    for n, dd, mm, vv in zip(SMALL, _unpack(d, shapes), _unpack(m, shapes), _unpack(v, shapes)):
        delta[n], new_m[n], new_v[n] = dd, mm, vv

    two_d = lambda a: a.reshape(a.shape[0] * a.shape[1], a.shape[2])
    accs = {n: None for n in BIG}

    def adamw_layer(i, dep=None):
        for n in BIG:
            accs[n] = _adamw_layer(i, two_d(inp[n]), red[i][n], two_d(inp['m_' + n]), two_d(inp['v_' + n]), accs[n],
                                   name="adamw_big", dep=dep)

    done = [i for i in range(NL) if i != pending[0]]
    for i in done:
        adamw_layer(i, dep=started)
    red[pending[0]] = comm.reduce_c(handle_b, [d] + ([accs[n][1] for n in BIG] if done else []))
    adamw_layer(pending[0])
    for n in BIG:
        gout[n], delta[n], new_m[n], new_v[n] = [a.reshape(inp[n].shape) for a in accs[n]]

    return (loss, gx.reshape(x.shape), *[gout[n] for n in WEIGHTS], *[delta[n] for n in WEIGHTS],
            *[new_m[n] for n in WEIGHTS], *[new_v[n] for n in WEIGHTS])
```

```python
import functools

import numpy as np
import jax
import jax.numpy as jnp
from jax import lax
from jax.experimental import pallas as pl
from jax.experimental.pallas import tpu as pltpu

F32 = jnp.float32
BF16 = jnp.bfloat16
_MXU = jnp.bfloat16
_ACT = jnp.bfloat16
_VMEM_LIMIT = 56 * 1024 * 1024

S = 2048
D = 1024
NL = 4
DFF = 2816
LN_EPS = 1e-5
SSD_EPS = 1e-5
ALPHA = (2.0 * NL) ** 0.25
POOLW = 768
POOL_WINDOWS = (2, 4, 8, 16)
POOL_GDIM = 192
CH = 128
ATTN_DILS = (1, 4, 16)
HC = 9728
O_U, O_Z, O_XBC, O_Q, O_K, O_V, O_G, O_DT = 0, 768, 1792, 3840, 4608, 5376, 6144, 9216

ADAM_LR, ADAM_B1, ADAM_B2, ADAM_EPS, ADAM_WD, ADAM_STEP = 0.001, 0.9, 0.999, 1e-08, 0.01, 10

WEIGHTS = ['ffn1_w13', 'ffn1_w2', 'ln1_g', 'ln1_b', 'w_in', 'gate_b', 'pool_w', 'pool_b', 'pool_scale', 'conv_w',
           'conv_b', 'dt_bias', 'a_log', 'd_skip', 'ssd_norm', 'rel_bias', 'p_pool', 'p_ssd', 'p_attn', 'w_out',
           'ln2_g', 'ln2_b', 'ffn2_w13', 'ffn2_w2', 'ln3_g', 'ln3_b']
BIG = ['ffn1_w13', 'ffn1_w2', 'w_in', 'p_pool', 'p_ssd', 'p_attn', 'w_out', 'ffn2_w13', 'ffn2_w2']
COL_SHARDED = {'ffn1_w13', 'ffn2_w13', 'w_in', 'p_pool', 'p_attn'}
SMALL = [n for n in WEIGHTS if n not in BIG]


def _pcall(body, **kw):
    return pl.pallas_call(body, **kw)


def _cp(sem=None):
    return pltpu.CompilerParams(dimension_semantics=sem, vmem_limit_bytes=_VMEM_LIMIT)


def _pick(n, cands):
    for c in cands:
        if n % c == 0:
            return c
    raise ValueError(f"no tile for {n}")


def _mm(a, b, *, ta=False, tb=False, add=None, out_dtype=F32, dep=None, name):
    if ta:
        K, M = a.shape
    else:
        M, K = a.shape
    if tb:
        N, K2 = b.shape
    else:
        K2, N = b.shape
    assert K == K2, (a.shape, b.shape, ta, tb)
    sa, sb, so = a.dtype.itemsize, b.dtype.itemsize, jnp.dtype(out_dtype).itemsize
    tm, tn, tk = _mm_tiles(M, N, K, sa, sb, so + (4 if add is not None else 0))
    nk = K // tk
    a_bytes, b_bytes = M * K * sa, K * N * sb
    j_outer = nk == 1 and (b_bytes + a_bytes * (N // tn) < a_bytes + b_bytes * (M // tm))
    ij = (lambda p, q: (q, p)) if j_outer else (lambda p, q: (p, q))

    def im(f):
        return lambda p, q, k: f(*ij(p, q), k)

    a_spec = pl.BlockSpec((tk, tm), im(lambda i, j, k: (k, i))) if ta else pl.BlockSpec((tm, tk), im(lambda i, j, k: (i, k)))
    b_spec = pl.BlockSpec((tn, tk), im(lambda i, j, k: (j, k))) if tb else pl.BlockSpec((tk, tn), im(lambda i, j, k: (k, j)))
    o_spec = pl.BlockSpec((tm, tn), im(lambda i, j, k: (i, j)))
    dims = (((0 if ta else 1,), (1 if tb else 0,)), ((), ()))
    has_add = add is not None

    n_in = 2 + int(has_add) + int(dep is not None)

    def body(*refs):
        a_ref, b_ref = refs[0], refs[1]
        add_ref = refs[2] if has_add else None
        o_ref = refs[n_in]
        part = lax.dot_general(a_ref[...].astype(_MXU), b_ref[...].astype(_MXU), dims, preferred_element_type=F32)

        def finish(r):
            if has_add:
                r = r + add_ref[...]
            o_ref[...] = r.astype(out_dtype)

        if nk == 1:
            finish(part)
        else:
            acc = refs[-1]
            k = pl.program_id(2)

            @pl.when(k == 0)
            def _():
                acc[...] = part

            @pl.when(k > 0)
            def _():
                acc[...] += part

            @pl.when(k == nk - 1)
            def _():
                finish(acc[...])

    in_specs = [a_spec, b_spec]
    args = [a, b]
    if has_add:
        in_specs.append(o_spec)
        args.append(add)
    if dep is not None:
        in_specs.append(pl.BlockSpec(memory_space=pl.ANY))
        args.append(dep)
    gm, gn = M // tm, N // tn
    return _pcall(
        body, name=name, grid=((gn, gm, nk) if j_outer else (gm, gn, nk)), in_specs=in_specs, out_specs=o_spec,
        out_shape=jax.ShapeDtypeStruct((M, N), out_dtype),
        scratch_shapes=([pltpu.VMEM((tm, tn), F32)] if nk > 1 else []),
        compiler_params=_cp(("parallel", "parallel", "arbitrary")),
    )(*args)


_MM_VMEM_BUDGET = 40 * 1024 * 1024


def _divisors128(n, cap):
    return [d for d in range(128, min(n, cap) + 1, 128) if n % d == 0][::-1]


_MM_CYC_PER_MMAC = 4.35
_MM_CYC_PER_ACC_VREG = 2.03
_MM_HBM_BYTES_PER_CYC = 1455.0
_MM_CYC_PER_STEP = 770.0


def _mm_tiles(M, N, K, sa, sb, so):
    best = None
    for tm in _divisors128(M, 1408):
        for tn in _divisors128(N, 2560):
            for tk in ([K] if K <= 4096 else []) + _divisors128(K, 2816):
                nk = K // tk
                need = 2 * (tm * tk * sa + tk * tn * sb + tm * tn * so) + (tm * tn * 4 if nk > 1 else 0)
                need += tm * tk * 2 + tk * tn * 2 + tm * tn * 4
                if need > _MM_VMEM_BUDGET:
                    continue
                gm, gn = M // tm, N // tn
                a_bytes, b_bytes = M * K * sa, K * N * sb
                hbm = min(b_bytes + a_bytes * gn, a_bytes + b_bytes * gm) if nk == 1 else a_bytes * gn + b_bytes * gm
                hbm += M * N * so
                work = _MM_CYC_PER_MMAC * M * N * K / 1e6 + _MM_CYC_PER_ACC_VREG * (M * N / 1024) * (nk if nk > 1 else 0.5)
                cost = max(work, hbm / _MM_HBM_BYTES_PER_CYC) + gm * gn * nk * _MM_CYC_PER_STEP
                if best is None or cost < best[0]:
                    best = (cost, (tm, tn, tk))
    assert best is not None, (M, N, K)
    return best[1]


def _store(ref, val):
    if isinstance(val, (list, tuple)):
        off = 0
        for p in val:
            w = p.shape[1]
            ref[:, off:off + w] = p.astype(ref.dtype)
            off += w
    else:
        ref[...] = val.astype(ref.dtype)


def _acc_store(ref, val, first):
    pieces = val if isinstance(val, (list, tuple)) else [val]
    off = 0
    for p in pieces:
        w = p.shape[1]

        @pl.when(first)
        def _(p=p, off=off, w=w):
            ref[:, off:off + w] = p

        @pl.when(jnp.logical_not(first))
        def _(p=p, off=off, w=w):
            ref[:, off:off + w] += p

        off += w


def _rowwise(fn, tiled, full, out_tiled, out_acc, *, name, tm=256):
    arrs, specs = [], []
    for t in tiled:
        arr, w, cb = t if isinstance(t, tuple) else (t, t.shape[1], 0)
        arrs.append(arr)
        specs.append(pl.BlockSpec((tm, w), functools.partial(lambda i, cb: (i, cb), cb=cb)))
    R = arrs[0].shape[0]
    assert R % tm == 0
    for f in full:
        arrs.append(f)
        specs.append(pl.BlockSpec(f.shape, functools.partial(lambda i, nd: (0,) * nd, nd=f.ndim)))
    nt, nf, no = len(tiled), len(full), len(out_tiled)

    def body(*refs):
        tv = [r[...] for r in refs[:nt]]
        fv = [r[...] for r in refs[nt:nt + nf]]
        ot, oa = fn(tv, fv)
        for r, v in zip(refs[nt + nf:nt + nf + no], ot):
            _store(r, v)
        first = pl.program_id(0) == 0
        for r, v in zip(refs[nt + nf + no:], oa):
            _acc_store(r, v, first)

    out_shape = [jax.ShapeDtypeStruct((R, c), dt) for c, dt in out_tiled]
    out_specs = [pl.BlockSpec((tm, c), lambda i: (i, 0)) for c, _ in out_tiled]
    for shp in out_acc:
        out_shape.append(jax.ShapeDtypeStruct(shp, F32))
        out_specs.append(pl.BlockSpec(shp, lambda i: (0, 0)))
    return _pcall(body, name=name, grid=(R // tm,), in_specs=specs, out_specs=out_specs, out_shape=out_shape,
                  compiler_params=_cp(("arbitrary",)))(*arrs)


def _group(group):
    x, y, c = lax.axis_index("x"), lax.axis_index("y"), lax.axis_index("c")
    if group == "chips":
        return 2 * x + y, [((x, 1 - y, c), 2 * x + 1 - y), ((1 - x, y, c), 2 * (1 - x) + y),
                           ((1 - x, 1 - y, c), 2 * (1 - x) + 1 - y)]
    if group == "cores":
        return c, [((x, y, 1 - c), 1 - c)]
    if group == "x":
        return x, [((1 - x, y, c), 1 - x)]
    return y, [((x, 1 - y, c), 1 - y)]


def _exchange(arrs, group, mode, name, dep=None):
    chips = group == "chips"
    k = len(arrs)
    npeer = 3 if chips else 1

    def body(*refs):
        nd = 0 if dep is None else 1
        ins, outs = refs[:k], refs[k + nd:2 * k + nd]
        send_sems, recv_sems = refs[2 * k + nd:]
        me, peers = _group(group)
        remote = []
        for i in range(k):
            for p, (dev, slot) in enumerate(peers):
                src = ins[i].at[slot] if mode == "scatter" else ins[i]
                if not chips:
                    dst = outs[i]
                else:
                    dst = outs[i].at[p] if mode == "scatter" else outs[i].at[me]
                cp = pltpu.make_async_remote_copy(src_ref=src, dst_ref=dst, send_sem=send_sems.at[i, p],
                                                  recv_sem=recv_sems.at[i, p], device_id=dev,
                                                  device_id_type=pl.DeviceIdType.MESH)
                cp.start()
                remote.append(cp)
        for cp in remote:
            cp.wait_recv()
        for cp in remote:
            cp.wait_send()

    def oshape(a):
        piece = a.shape[1:] if mode == "scatter" else a.shape
        if chips:
            piece = ((3,) if mode == "scatter" else (4,)) + piece
        return jax.ShapeDtypeStruct(piece, a.dtype)

    any_spec = pl.BlockSpec(memory_space=pl.ANY)
    extra = [] if dep is None else [dep]
    return _pcall(body, name=name, in_specs=[any_spec] * (k + len(extra)), out_specs=[any_spec] * k,
                  out_shape=[oshape(a) for a in arrs],
                  scratch_shapes=[pltpu.SemaphoreType.DMA((k, npeer)), pltpu.SemaphoreType.DMA((k, npeer))])(*arrs, *extra)


def _split_copies(ins, lands, send_sems, recv_sems, group, mode):
    chips = group == "chips"
    me, peers = _group(group)
    npeer = len(peers)
    out = []
    for i in range(len(ins)):
        for p, (dev, slot) in enumerate(peers):
            src = ins[i].at[slot] if mode == "scatter" else ins[i]
            if not chips:
                dst = lands[i]
            else:
                dst = lands[i].at[p] if mode == "scatter" else lands[i].at[me]
            out.append(pltpu.make_async_remote_copy(src_ref=src, dst_ref=dst, send_sem=send_sems.at[npeer * i + p],
                                                    recv_sem=recv_sems.at[npeer * i + p], device_id=dev,
                                                    device_id_type=pl.DeviceIdType.MESH))
    return out


def _exchange_start(arrs, group, mode, name):
    k = len(arrs)
    chips = group == "chips"
    nsem = (3 if chips else 1) * k
    hbm = pl.BlockSpec(memory_space=pltpu.HBM)
    sem = pl.BlockSpec(memory_space=pltpu.SEMAPHORE)

    def land_shape(a):
        piece = a.shape[1:] if mode == "scatter" else a.shape
        if chips:
            piece = ((3,) if mode == "scatter" else (4,)) + piece
        return piece

    def body(*refs):
        ins, lands = refs[:k], refs[k:2 * k]
        send_sems, recv_sems = refs[2 * k], refs[2 * k + 1]
        token = refs[-1]
        for cp in _split_copies(ins, lands, send_sems, recv_sems, group, mode):
            cp.start()
        token[...] = jnp.zeros_like(token)

    srcs = [pltpu.with_memory_space_constraint(a, pltpu.HBM) for a in arrs]
    lands = [pltpu.with_memory_space_constraint(lax.empty(land_shape(a), a.dtype), pltpu.HBM) for a in arrs]
    out_shape = ([pltpu.SemaphoreType.DMA((nsem,)), pltpu.SemaphoreType.DMA((nsem,))]
                 + [pltpu.HBM(a.shape, a.dtype) for a in arrs] + [pltpu.HBM(land_shape(a), a.dtype) for a in arrs]
                 + [jax.ShapeDtypeStruct((8, 128), F32)])
    outs = _pcall(body, name=name, in_specs=[hbm] * (2 * k),
                  out_specs=[sem, sem] + [hbm] * (2 * k) + [pl.BlockSpec(memory_space=pltpu.VMEM)], out_shape=out_shape,
                  input_output_aliases={i: 2 + i for i in range(2 * k)},
                  compiler_params=pltpu.CompilerParams(has_side_effects=pltpu.SideEffectType.DATAFLOW_SIDE_EFFECTING))(
                      *srcs, *lands)
    return (outs[0], outs[1], list(outs[2:2 + k]), list(outs[2 + k:2 + 2 * k])), outs[-1]


def _exchange_wait(state, after, group, mode, name):
    send_sems, recv_sems, srcs, lands = state
    k = len(srcs)
    after = list(after) if isinstance(after, (list, tuple)) else [after]
    hbm = pl.BlockSpec(memory_space=pltpu.HBM)
    sem = pl.BlockSpec(memory_space=pltpu.SEMAPHORE)

    def body(*refs):
        ins, lnd = refs[:k], refs[k:2 * k]
        send_sems, recv_sems = refs[2 * k], refs[2 * k + 1]
        for cp in _split_copies(ins, lnd, send_sems, recv_sems, group, mode):
            cp.wait_send()
            cp.wait_recv()

    outs = _pcall(body, name=name,
                  in_specs=[hbm] * (2 * k) + [sem, sem] + [pl.BlockSpec(memory_space=pl.ANY)] * len(after),
                  out_specs=[hbm] * (2 * k),
                  out_shape=[pltpu.HBM(a.shape, a.dtype) for a in srcs] + [pltpu.HBM(a.shape, a.dtype) for a in lands],
                  input_output_aliases={i: i for i in range(2 * k)},
                  compiler_params=pltpu.CompilerParams(has_side_effects=pltpu.SideEffectType.DATAFLOW_SIDE_EFFECTING))(
                      *srcs, *lands, send_sems, recv_sems, *after)
    return list(outs[:k]), list(outs[k:])


def _sum_own_recv(own, recv, me, out_dtype, name):
    n, R, C = own.shape
    nr = 1 if recv.ndim == 2 else recv.shape[0]
    tr = _pick(R, (256, 128, 64, 32, 16, 8))

    def body(me_ref, own_ref, *refs):
        o_ref = refs[-1]
        acc = own_ref[...].astype(F32)
        for r in refs[:-1]:
            acc = acc + r[...].astype(F32)
        o_ref[...] = acc.astype(out_dtype)

    specs = [pl.BlockSpec((None, tr, C), lambda i, me_ref: (me_ref[0], i, 0))]
    args = [own]
    if recv.ndim == 2:
        specs.append(pl.BlockSpec((tr, C), lambda i, me_ref: (i, 0)))
        args.append(recv)
    else:
        for p in range(nr):
            specs.append(pl.BlockSpec((None, tr, C), functools.partial(lambda i, me_ref, p: (p, i, 0), p=p)))
            args.append(recv)
    gs = pltpu.PrefetchScalarGridSpec(num_scalar_prefetch=1, grid=(R // tr,), in_specs=specs,
                                      out_specs=pl.BlockSpec((tr, C), lambda i, me_ref: (i, 0)))
    return _pcall(body, name=name, grid_spec=gs, out_shape=jax.ShapeDtypeStruct((R, C), out_dtype),
                  compiler_params=_cp(("parallel",)))(me, *args)


def _silu(x):
    return x * jax.nn.sigmoid(x)


def _ln(r, g, b):
    mu = jnp.mean(r, -1, keepdims=True)
    xc = r - mu
    var = jnp.mean(xc * xc, -1, keepdims=True)
    return xc * lax.rsqrt(var + LN_EPS) * g + b


def _softplus(x):
    return jnp.maximum(x, 0.0) + jnp.log1p(jnp.exp(-jnp.abs(x)))


def _res_ln_fwd(x, y, g, b, res, name):
    def fn(tv, fv):
        r = ALPHA * tv[0] + res * tv[1]
        out = _ln(r, fv[0], fv[1])
        return [r, out, out], []
    return _rowwise(fn, [x, y], [g, b], [(D, F32), (D, F32), (D, _ACT)], [], name=name)


def _ln_bwd(r, g, b, dout, res, name):
    def fn(tv, fv):
        _, vjp = jax.vjp(_ln, tv[0], fv[0], fv[1])
        dr, dg, db = vjp(tv[1])
        return [ALPHA * dr, res * dr], [dg, db]
    return _rowwise(fn, [r, dout], [g, b], [(D, F32), (D, _ACT)], [(1, D), (1, D)], name=name)


SWIGLU_TM, SWIGLU_TN = 512, 1408


def _swiglu_fwd(x, w13, dep, name):
    T, K = x.shape
    tm, tn = SWIGLU_TM, SWIGLU_TN
    nj = DFF // tn
    has_dep = dep is not None

    def body(x_ref, wa_ref, wg_ref, *rest):
        a_ref, g_ref, s_ref = rest[-3:]
        xv = x_ref[...].astype(_MXU)
        a = jnp.dot(xv, wa_ref[...].astype(_MXU), preferred_element_type=F32)
        g = jnp.dot(xv, wg_ref[...].astype(_MXU), preferred_element_type=F32)
        a_ref[...] = a
        g_ref[...] = g
        s_ref[...] = (_silu(a) * g).astype(s_ref.dtype)

    out = pl.BlockSpec((tm, tn), lambda j, i: (i, j))
    in_specs = [pl.BlockSpec((tm, K), lambda j, i: (i, 0)), pl.BlockSpec((K, tn), lambda j, i: (0, j)),
                pl.BlockSpec((K, tn), lambda j, i: (0, nj + j))]
    args = [x, w13, w13]
    if has_dep:
        in_specs.append(pl.BlockSpec(memory_space=pl.ANY))
        args.append(dep)
    return _pcall(body, name=name, grid=(nj, T // tm), in_specs=in_specs, out_specs=[out, out, out],
                  out_shape=[jax.ShapeDtypeStruct((T, DFF), F32), jax.ShapeDtypeStruct((T, DFF), F32),
                             jax.ShapeDtypeStruct((T, DFF), _ACT)],
                  compiler_params=_cp(("parallel", "parallel")))(*args)


def _swiglu_act_bwd(a, g, ds, name):
    def fn(tv, fv):
        s, vjp = jax.vjp(lambda a, g: _silu(a) * g, tv[0], tv[1])
        da, dg = vjp(tv[2])
        return [[da, dg], s], []
    return _rowwise(fn, [a, g, ds], [], [(2 * DFF, _ACT), (DFF, _ACT)], [], name=name)


def _loss_fwd_bwd(y, tgt, name):
    def fn(tv, fv):
        e = tv[0] - tv[1]
        row = jnp.sum(e * e, axis=1, keepdims=True)
        tot = jnp.sum(row, axis=0, keepdims=True) * (0.5 / D)
        return [e * (1.0 / D)], [jnp.broadcast_to(tot, (1, 128))]
    return _rowwise(fn, [y, tgt], [], [(D, F32)], [(1, 128)], name=name)


def _shift_down(x, k, row):
    return jnp.where(row >= k, pltpu.roll(x, k, axis=0), 0.0)


def _shift_up(x, k, row):
    n = x.shape[0]
    return jnp.where(row < n - k, pltpu.roll(x, n - k, axis=0), 0.0)


def _pool_window_masks(j):
    lane = lax.broadcasted_iota(jnp.int32, (1, 128), 1) + j * 128
    grp = lane // POOL_GDIM
    return [grp == g for g in range(4)]


def _pool_mean(u, bwd, name, col0=0):
    T = u.shape[0]
    B = T // S

    def body(u_ref, o_ref):
        j = pl.program_id(1)
        x = u_ref[...]
        row = lax.broadcasted_iota(jnp.int32, (S, 1), 0)
        masks = _pool_window_masks(j)
        inv = [1.0 / jnp.minimum(row + 1, w).astype(F32) for w in POOL_WINDOWS]
        if not bwd:
            s2 = x + _shift_down(x, 1, row)
            s4 = s2 + _shift_down(s2, 2, row)
            s8 = s4 + _shift_down(s4, 4, row)
            s16 = s8 + _shift_down(s8, 8, row)
            mean = jnp.where(masks[0], s2 * inv[0], jnp.where(masks[1], s4 * inv[1],
                             jnp.where(masks[2], s8 * inv[2], s16 * inv[3])))
            o_ref[...] = (mean - x).astype(o_ref.dtype)
        else:
            g = [jnp.where(masks[i], x * inv[i], 0.0) for i in range(4)]
            t = g[3]
            t = t + _shift_up(t, 8, row) + g[2]
            t = t + _shift_up(t, 4, row) + g[1]
            t = t + _shift_up(t, 2, row) + g[0]
            t = t + _shift_up(t, 1, row)
            o_ref[...] = (t - x).astype(o_ref.dtype)

    spec = pl.BlockSpec((S, 128), lambda b, j: (b, j))
    return _pcall(body, name=name, grid=(B, POOLW // 128),
                  in_specs=[pl.BlockSpec((S, 128), lambda b, j: (b, j + col0))], out_specs=spec,
                  out_shape=jax.ShapeDtypeStruct((T, POOLW), _ACT), compiler_params=_cp(("parallel", "parallel")))(u)


def _conv_silu(xbc, w, b, name, col0=0):
    T, C = xbc.shape[0], w.shape[1]
    B = T // S

    def body(x_ref, w_ref, b_ref, o_ref):
        x = x_ref[...]
        row = lax.broadcasted_iota(jnp.int32, (S, 1), 0)
        c = b_ref[...] + w_ref[3:4, :] * x
        for s in range(1, 4):
            c = c + w_ref[3 - s:4 - s, :] * _shift_down(x, s, row)
        o_ref[...] = _silu(c)

    return _pcall(body, name=name, grid=(B, C // 128),
                  in_specs=[pl.BlockSpec((S, 128), lambda b, j: (b, j + col0)), pl.BlockSpec((4, 128), lambda b, j: (0, j)),
                            pl.BlockSpec((1, 128), lambda b, j: (0, j))],
                  out_specs=pl.BlockSpec((S, 128), lambda b, j: (b, j)),
                  out_shape=jax.ShapeDtypeStruct((T, C), F32), compiler_params=_cp(("parallel", "parallel")))(xbc, w, b)


def _conv_silu_bwd(xbc, w, b, dact, name, col0=0):
    T, C = xbc.shape[0], w.shape[1]
    B = T // S

    def body(x_ref, w_ref, b_ref, d_ref, dx_ref, dw_ref, db_ref):
        bi = pl.program_id(1)
        x = x_ref[...]
        row = lax.broadcasted_iota(jnp.int32, (S, 1), 0)
        xs = [x] + [_shift_down(x, s, row) for s in range(1, 4)]
        c = b_ref[...]
        for s in range(4):
            c = c + w_ref[3 - s:4 - s, :] * xs[s]
        _, vjp = jax.vjp(_silu, c)
        dc = vjp(d_ref[...])[0]
        dx = w_ref[3:4, :] * dc
        for s in range(1, 4):
            dx = dx + w_ref[3 - s:4 - s, :] * _shift_up(dc, s, row)
        dx_ref[...] = dx.astype(dx_ref.dtype)
        first = bi == 0
        for s in range(4):
            _acc_rows(dw_ref, 3 - s, jnp.sum(dc * xs[s], axis=0, keepdims=True), first)
        _acc_rows(db_ref, 0, jnp.sum(dc, axis=0, keepdims=True), first)

    blk = pl.BlockSpec((S, 128), lambda j, b: (b, j))
    return _pcall(body, name=name, grid=(C // 128, B),
                  in_specs=[pl.BlockSpec((S, 128), lambda j, b: (b, j + col0)), pl.BlockSpec((4, 128), lambda j, b: (0, j)),
                            pl.BlockSpec((1, 128), lambda j, b: (0, j)), blk],
                  out_specs=[blk, pl.BlockSpec((4, 128), lambda j, b: (0, j)), pl.BlockSpec((1, 128), lambda j, b: (0, j))],
                  out_shape=[jax.ShapeDtypeStruct((T, C), _ACT), jax.ShapeDtypeStruct((4, C), F32),
                             jax.ShapeDtypeStruct((1, C), F32)],
                  compiler_params=_cp(("parallel", "arbitrary")))(xbc, w, b, dact)


def _acc_rows(ref, r, val, first):
    @pl.when(first)
    def _():
        ref[r:r + 1, :] = val

    @pl.when(jnp.logical_not(first))
    def _():
        ref[r:r + 1, :] += val


def _tri_consts():
    i = lax.broadcasted_iota(jnp.int32, (CH, CH), 0)
    j = lax.broadcasted_iota(jnp.int32, (CH, CH), 1)
    return (i == j).astype(F32), (j <= i).astype(F32), (i <= j).astype(F32), i >= j


def _ssd_chunk(h, x, dt, Bm, Cm, a, dsk, consts):
    eye, tril, triu, lower = consts
    Bb = Bm.astype(_MXU)
    Cb = Cm.astype(_MXU)
    cb = lax.dot_general(Cb, Bb, (((1,), (1,)), ((), ())), preferred_element_type=F32)
    ys, hn = [], []
    for e in range(4):
        adt = dt[e] * a[e]
        adt_row = jnp.sum(adt * eye, axis=0, keepdims=True)
        cs_col = jnp.sum(adt_row * tril, axis=1, keepdims=True)
        cs_row = jnp.sum(adt * triu, axis=0, keepdims=True)
        cs_last = jnp.sum(adt, axis=0, keepdims=True)
        decay = jnp.exp(jnp.where(lower, cs_col - cs_row, -jnp.inf))
        xb = (x[e] * dt[e]).astype(_MXU)
        y_diag = jnp.dot((cb * decay).astype(_MXU), xb, preferred_element_type=F32)
        bdec = (Bm * jnp.exp(cs_last - cs_col)).astype(_MXU)
        st = lax.dot_general(bdec, xb, (((0,), (0,)), ((), ())), preferred_element_type=F32)
        hn.append(h[e] * jnp.exp(cs_last) + st)
        y_off = jnp.exp(cs_col) * jnp.dot(Cb, h[e].astype(_MXU), preferred_element_type=F32)
        ys.append(y_diag + y_off + dsk[e] * x[e])
    return ys, hn


def _ssd_specs(order):
    def im(f):
        return lambda p, q: f(*order(p, q))
    xs = pl.BlockSpec((S, 256), im(lambda b, g: (b, g)))
    dt = pl.BlockSpec((None, S, 4), im(lambda b, g: (g, b, 0)))
    bc = pl.BlockSpec((S, 128), im(lambda b, g: (b, g)))
    hd = pl.BlockSpec((None, 1, 4), im(lambda b, g: (g, 0, 0)))
    hs = pl.BlockSpec((None, None, S // CH, 4, 128, 64), im(lambda b, g: (b, g, 0, 0, 0, 0)))
    bw = pl.BlockSpec((S, 128), im(lambda b, g: (b, 8 + g)))
    cw = pl.BlockSpec((S, 128), im(lambda b, g: (b, 12 + g)))
    return xs, dt, bc, hd, hs, bw, cw


def _ssd_fwd(act, dtg, a, dsk, name):
    xs = bm = cm = act
    T = xs.shape[0]
    B = T // S
    nc = S // CH

    def body(x_ref, dt_ref, b_ref, c_ref, a_ref, k_ref, y_ref, hs_ref, h_ref):
        consts = _tri_consts()
        h_ref[...] = jnp.zeros_like(h_ref)
        al = [a_ref[:, e:e + 1] for e in range(4)]
        kl = [k_ref[:, e:e + 1] for e in range(4)]

        def step(c, carry):
            r0 = pl.multiple_of(c * CH, CH)
            rows = pl.ds(r0, CH)
            h = [h_ref[e] for e in range(4)]
            for e in range(4):
                hs_ref[c, e] = h[e]
            x = [x_ref[rows, 64 * e:64 * e + 64] for e in range(4)]
            dt = [dt_ref[rows, e:e + 1] for e in range(4)]
            ys, hn = _ssd_chunk(h, x, dt, b_ref[rows, :], c_ref[rows, :], al, kl, consts)
            for e in range(4):
                y_ref[rows, 64 * e:64 * e + 64] = ys[e]
                h_ref[e] = hn[e]
            return carry

        lax.fori_loop(0, nc, step, 0)

    sx, sdt, sbc, shd, shs, sbw, scw = _ssd_specs(lambda b, g: (b, g))
    return _pcall(body, name=name, grid=(B, 4), in_specs=[sx, sdt, sbw, scw, shd, shd], out_specs=[sx, shs],
                  out_shape=[jax.ShapeDtypeStruct((T, 1024), F32), jax.ShapeDtypeStruct((B, 4, nc, 4, 128, 64), F32)],
                  scratch_shapes=[pltpu.VMEM((4, 128, 64), F32)],
                  compiler_params=_cp(("parallel", "parallel")))(xs, dtg, bm, cm, a, dsk)


def _lane_place(vals, width):
    lane = lax.broadcasted_iota(jnp.int32, (1, width), 1)
    out = jnp.zeros((1, width), F32)
    for e, v in enumerate(vals):
        out = out + jnp.where(lane == e, v, 0.0)
    return out


def _ssd_bwd(act, dtg, a, dsk, hs, dy, name):
    xs = bm = cm = act
    T = xs.shape[0]
    B = T // S
    nc = S // CH

    def body(x_ref, dt_ref, b_ref, c_ref, a_ref, k_ref, hs_ref, dy_ref,
             dx_ref, ddt_ref, db_ref, dc_ref, dak_ref, dh_ref, sc_ref):
        bi = pl.program_id(1)
        consts = _tri_consts()
        dh_ref[...] = jnp.zeros_like(dh_ref)
        sc_ref[...] = jnp.zeros_like(sc_ref)
        al = [a_ref[:, e:e + 1] for e in range(4)]
        kl = [k_ref[:, e:e + 1] for e in range(4)]

        def step(i, carry):
            c = nc - 1 - i
            r0 = pl.multiple_of(c * CH, CH)
            rows = pl.ds(r0, CH)
            h = [hs_ref[c, e] for e in range(4)]
            x = [x_ref[rows, 64 * e:64 * e + 64] for e in range(4)]
            dt = [dt_ref[rows, e:e + 1] for e in range(4)]
            f = functools.partial(_ssd_chunk, consts=consts)
            _, vjp = jax.vjp(f, h, x, dt, b_ref[rows, :], c_ref[rows, :], al, kl)
            dys = [dy_ref[rows, 64 * e:64 * e + 64] for e in range(4)]
            dhn = [dh_ref[e] for e in range(4)]
            dh, dx, ddt, dB, dC, da, dk = vjp((dys, dhn))
            for e in range(4):
                dh_ref[e] = dh[e]
                dx_ref[rows, 64 * e:64 * e + 64] = dx[e]
                ddt_ref[rows, e:e + 1] = ddt[e]
            db_ref[rows, :] = dB
            dc_ref[rows, :] = dC
            sc_ref[0:1, :] += _lane_place(da, 128)
            sc_ref[1:2, :] += _lane_place(dk, 128)
            return carry

        lax.fori_loop(0, nc, step, 0)
        first = bi == 0

        @pl.when(first)
        def _():
            dak_ref[...] = sc_ref[...]

        @pl.when(jnp.logical_not(first))
        def _():
            dak_ref[...] += sc_ref[...]

    sx, sdt, sbc, shd, shs, sbw, scw = _ssd_specs(lambda g, b: (b, g))
    return _pcall(body, name=name, grid=(4, B), in_specs=[sx, sdt, sbw, scw, shd, shd, shs, sx],
                  out_specs=[sx, sdt, sbc, sbc, pl.BlockSpec((None, 8, 128), lambda g, b: (g, 0, 0))],
                  out_shape=[jax.ShapeDtypeStruct((T, 1024), F32), jax.ShapeDtypeStruct((4, T, 4), F32),
                             jax.ShapeDtypeStruct((T, 512), F32), jax.ShapeDtypeStruct((T, 512), F32),
                             jax.ShapeDtypeStruct((4, 8, 128), F32)],
                  scratch_shapes=[pltpu.VMEM((4, 128, 64), F32), pltpu.VMEM((8, 128), F32)],
                  compiler_params=_cp(("parallel", "arbitrary")))(xs, dtg, bm, cm, a, dsk, hs, dy)


def _gate_norm(y, z, nw):
    t = y * _silu(z)
    return t * lax.rsqrt(jnp.mean(t * t, axis=-1, keepdims=True) + SSD_EPS) * nw


def _ssd_gate_norm(y, z, nw, name, zcol=0):
    def fn(tv, fv):
        return [[_gate_norm(tv[g], tv[4 + g], fv[0][:, 256 * g:256 * g + 256]) for g in range(4)]], []
    tiled = [(y, 256, g) for g in range(4)] + [(z, 256, zcol + g) for g in range(4)]
    return _rowwise(fn, tiled, [nw], [(1024, _ACT)], [], name=name)[0]


def _ssd_gate_norm_bwd(y, z, nw, dout, name, zcol=0):
    def fn(tv, fv):
        dys, dzs, dns = [], [], []
        for g in range(4):
            _, vjp = jax.vjp(_gate_norm, tv[g], tv[4 + g], fv[0][:, 256 * g:256 * g + 256])
            a, b, c = vjp(tv[8 + g])
            dys.append(a)
            dzs.append(b)
            dns.append(c)
        return [dys, dzs], [dns]
    tiled = [(y, 256, g) for g in range(4)] + [(z, 256, zcol + g) for g in range(4)] + [(dout, 256, g) for g in range(4)]
    return _rowwise(fn, tiled, [nw], [(1024, F32), (1024, _ACT)], [(1, 1024)], name=name)


def _t5_bucket_np(dist):
    dist = np.maximum(dist, 0)
    max_exact = 16
    large = max_exact + (np.log(np.maximum(dist, 1) / max_exact) / np.log(2048 / max_exact) * (32 - max_exact)).astype(np.int32)
    large = np.minimum(large, 31)
    return np.where(dist < max_exact, dist, large).astype(np.int32)


def _bucket_maps():
    qi = np.arange(128)[:, None]
    kj = np.arange(256)[None, :]
    return np.stack([_t5_bucket_np((qi - kj + 128) * dil) for dil in ATTN_DILS]).astype(np.int32)


def _bias_build(rel_bias, maps, name):
    def body(tab_ref, map_ref, o_ref):
        hh = pl.program_id(0)
        m = map_ref[...]
        acc = jnp.zeros((128, 256), F32)
        for b in range(32):
            acc = jnp.where(m == b, tab_ref[b, hh], acc)
        o_ref[...] = acc

    return _pcall(body, name=name, grid=(12,),
                  in_specs=[pl.BlockSpec(memory_space=pltpu.SMEM), pl.BlockSpec((None, 128, 256), lambda h: (h // 4, 0, 0))],
                  out_specs=pl.BlockSpec((None, 128, 256), lambda h: (h, 0, 0)),
                  out_shape=jax.ShapeDtypeStruct((12, 128, 256), F32), compiler_params=_cp(("parallel",)))(rel_bias, maps)


def _bias_reduce(dbias, maps, name):
    nl = dbias.shape[0]

    def body(d_ref, map_ref, o_ref):
        m = map_ref[...]
        d = d_ref[0]
        for i in range(1, nl):
            d = d + d_ref[i]
        lane = lax.broadcasted_iota(jnp.int32, (1, 128), 1)
        out = jnp.zeros((1, 128), F32)
        for b in range(32):
            s = jnp.sum(jnp.sum(jnp.where(m == b, d, 0.0), axis=1, keepdims=True), axis=0, keepdims=True)
            out = out + jnp.where(lane == b, s, 0.0)
        o_ref[...] = out

    return _pcall(body, name=name, grid=(12,),
                  in_specs=[pl.BlockSpec((nl, None, 128, 256), lambda h: (0, h, 0, 0)),
                            pl.BlockSpec((None, 128, 256), lambda h: (h // 4, 0, 0))],
                  out_specs=pl.BlockSpec((None, 1, 128), lambda h: (h, 0, 0)),
                  out_shape=jax.ShapeDtypeStruct((12, 1, 128), F32), compiler_params=_cp(("parallel",)))(dbias, maps)


def _attn_block(q, kb, vb, bias, mask):
    s = lax.dot_general(q.astype(_MXU), kb.astype(_MXU), (((1,), (1,)), ((), ())), preferred_element_type=F32) * 0.125 + bias
    s = jnp.where(mask, s, -jnp.inf)
    m = lax.stop_gradient(jnp.max(s, axis=-1, keepdims=True))
    p = jnp.exp(s - m)
    den = jnp.sum(p, axis=-1, keepdims=True)
    out = jnp.dot((p / den).astype(_MXU), vb.astype(_MXU), preferred_element_type=F32)
    return out, m + jnp.log(den)


ATTN_QB = 512


def _attn_masks(dil):
    qi = lax.broadcasted_iota(jnp.int32, (ATTN_QB, ATTN_QB + 128), 0)
    kj = lax.broadcasted_iota(jnp.int32, (ATTN_QB, ATTN_QB + 128), 1)
    band = (kj >= qi) & (kj <= qi + 128)
    if dil == 16:
        q2 = lax.broadcasted_iota(jnp.int32, (ATTN_QB, ATTN_QB), 0)
        k2 = lax.broadcasted_iota(jnp.int32, (ATTN_QB, ATTN_QB), 1)
        return ((q2 // 128) == (k2 // 128)) & (k2 <= q2), None
    return band[:, 128:], band


def _attn_wide_bias(b, dil):
    if dil == 16:
        return jnp.tile(b[:, 128:], (4, 4)), None
    z = jnp.zeros((128, 128), F32)
    band = jnp.concatenate([jnp.concatenate([z] * i + [b] + [z] * (3 - i), axis=1) for i in range(4)], axis=0)
    return band[:, 128:], band


def _fold_dbias(dbs, dil, band_form):
    def blk(i, j):
        return dbs[128 * i:128 * i + 128, 128 * j:128 * j + 128]
    if band_form:
        return sum(blk(i, i) for i in range(4)), sum(blk(i, i + 1) for i in range(4))
    cur = sum(blk(i, i) for i in range(4))
    if dil == 16:
        return None, cur
    return sum(blk(i, i - 1) for i in range(1, 4)), cur


def _attn_chunks(dil):
    out = []
    for n in range(S // ATTN_QB):
        if dil == 1 and n > 0:
            out.append((n * ATTN_QB, n * ATTN_QB - 128, ATTN_QB + 128, True))
        else:
            out.append((n * ATTN_QB, n * ATTN_QB, ATTN_QB, False))
    return out


def _qkv_specs(gi, order):
    def spec(base):
        col = (base + 256 * gi) // 128
        return pl.BlockSpec((S, 128), lambda p, q: (order(p, q)[0], col + order(p, q)[1]))
    return [spec(O_Q), spec(O_K), spec(O_V)]


def _residue_rows(r, dil):
    return pl.ds(r, S // dil, stride=dil)


def _attn_fwd(hcat, bias_all, gi, name):
    dil = ATTN_DILS[gi]
    T = hcat.shape[0]
    B, L = T // S, S // dil

    def body(q_ref, k_ref, v_ref, b_ref, o_ref, l_ref, *scr):
        mask_first, mask_band = _attn_masks(dil)
        if dil > 1:
            qs, ks, vs, os_, ls = scr
            for r in range(dil):
                rows, dst = _residue_rows(r, dil), pl.ds(r * L, L)
                qs[dst, :] = q_ref[rows, :]
                ks[dst, :] = k_ref[rows, :]
                vs[dst, :] = v_ref[rows, :]
        else:
            qs, ks, vs, os_, ls = q_ref, k_ref, v_ref, o_ref, l_ref
        ls[...] = jnp.zeros_like(ls)
        for e in range(2):
            lanes = slice(64 * e, 64 * e + 64)
            bias_first, bias_band = _attn_wide_bias(b_ref[e], dil)
            for q0, k0, kn, band_form in _attn_chunks(dil):
                cur, keys = pl.ds(q0, ATTN_QB), pl.ds(k0, kn)
                o, l = _attn_block(qs[cur, lanes], ks[keys, lanes], vs[keys, lanes],
                                   bias_band if band_form else bias_first, mask_band if band_form else mask_first)
                os_[cur, lanes] = o
                ls[cur, e:e + 1] = l
        if dil > 1:
            for r in range(dil):
                rows, src = _residue_rows(r, dil), pl.ds(r * L, L)
                o_ref[rows, :] = os_[src, :]
                l_ref[rows, :] = ls[src, :]

    scratch = [pltpu.VMEM((S, 128), F32)] * 5 if dil > 1 else []
    return _pcall(body, name=name, grid=(B, 2),
                  in_specs=_qkv_specs(gi, lambda b, hp: (b, hp))
                  + [pl.BlockSpec((2, 128, 256), lambda b, hp: (2 * gi + hp, 0, 0))],
                  out_specs=[pl.BlockSpec((S, 128), lambda b, hp: (b, hp)),
                             pl.BlockSpec((None, S, 128), lambda b, hp: (hp, b, 0))],
                  out_shape=[jax.ShapeDtypeStruct((T, 256), F32), jax.ShapeDtypeStruct((2, T, 128), F32)],
                  scratch_shapes=scratch,
                  compiler_params=_cp(("parallel", "parallel")))(hcat, hcat, hcat, bias_all)


def _attn_bwd(hcat, bias_all, gi, do, dl, name):
    dil = ATTN_DILS[gi]
    T = hcat.shape[0]
    B, L = T // S, S // dil

    def body(q_ref, k_ref, v_ref, b_ref, do_ref, dl_ref, dq_ref, dk_ref, dv_ref, db_ref, acc_ref, *scr):
        bi = pl.program_id(1)
        mask_first, mask_band = _attn_masks(dil)
        if dil > 1:
            qs, ks, vs, dos, dls, dqs, dks, dvs = scr
            for r in range(dil):
                rows, dst = _residue_rows(r, dil), pl.ds(r * L, L)
                qs[dst, :] = q_ref[rows, :]
                ks[dst, :] = k_ref[rows, :]
                vs[dst, :] = v_ref[rows, :]
                dos[dst, :] = do_ref[rows, :]
                dls[dst, :] = dl_ref[rows, :]
        else:
            qs, ks, vs, dos, dls, dqs, dks, dvs = q_ref, k_ref, v_ref, do_ref, dl_ref, dq_ref, dk_ref, dv_ref
        dks[...] = jnp.zeros_like(dks)
        dvs[...] = jnp.zeros_like(dvs)
        for e in range(2):
            lanes = slice(64 * e, 64 * e + 64)
            bias_first, bias_band = _attn_wide_bias(b_ref[e], dil)
            acc_ref[...] = jnp.zeros_like(acc_ref)
            for q0, k0, kn, band_form in _attn_chunks(dil):
                cur, keys = pl.ds(q0, ATTN_QB), pl.ds(k0, kn)
                f = functools.partial(_attn_block, mask=mask_band if band_form else mask_first)
                _, vjp = jax.vjp(f, qs[cur, lanes], ks[keys, lanes], vs[keys, lanes],
                                 bias_band if band_form else bias_first)
                dq, dkb, dvb, dbs = vjp((dos[cur, lanes], dls[cur, e:e + 1]))
                dqs[cur, lanes] = dq
                dks[keys, lanes] += dkb
                dvs[keys, lanes] += dvb
                prev, here = _fold_dbias(dbs, dil, band_form)
                if prev is not None:
                    acc_ref[:, 0:128] += prev
                acc_ref[:, 128:256] += here

            @pl.when(bi == 0)
            def _(e=e):
                db_ref[e] = acc_ref[...]

            @pl.when(bi > 0)
            def _(e=e):
                db_ref[e] += acc_ref[...]

        if dil > 1:
            for r in range(dil):
                rows, src = _residue_rows(r, dil), pl.ds(r * L, L)
                dq_ref[rows, :] = dqs[src, :]
                dk_ref[rows, :] = dks[src, :]
                dv_ref[rows, :] = dvs[src, :]

    order = lambda hp, b: (b, hp)
    blk = pl.BlockSpec((S, 128), lambda hp, b: (b, hp))
    lblk = pl.BlockSpec((None, S, 128), lambda hp, b: (hp, b, 0))
    sds = jax.ShapeDtypeStruct((T, 256), F32)
    scratch = [pltpu.VMEM((128, 256), F32)] + ([pltpu.VMEM((S, 128), F32)] * 8 if dil > 1 else [])
    return _pcall(body, name=name, grid=(2, B),
                  in_specs=_qkv_specs(gi, order) + [pl.BlockSpec((2, 128, 256), lambda hp, b: (2 * gi + hp, 0, 0)), blk, lblk],
                  out_specs=[blk, blk, blk, pl.BlockSpec((2, 128, 256), lambda hp, b: (hp, 0, 0))],
                  out_shape=[sds, sds, sds, jax.ShapeDtypeStruct((4, 128, 256), F32)],
                  scratch_shapes=scratch,
                  compiler_params=_cp(("parallel", "arbitrary")))(hcat, hcat, hcat, bias_all, do, dl)


def _lse_merge(o0, o1, o2, l0, l1, l2):
    m = lax.stop_gradient(jnp.maximum(jnp.maximum(l0, l1), l2))
    e0, e1, e2 = jnp.exp(l0 - m), jnp.exp(l1 - m), jnp.exp(l2 - m)
    den = e0 + e1 + e2
    return (e0 / den) * o0 + (e1 / den) * o1 + (e2 / den) * o2


def _attn_merge(outs, lses, dy, name):
    T = outs[0].shape[0]
    bwd = dy is not None
    tm = 512

    def body(*refs):
        o_refs, l_refs = refs[:3], refs[3:6]
        if bwd:
            for r in refs[10:13]:
                r[...] = jnp.zeros_like(r)
        for e in range(2):
            lanes = slice(64 * e, 64 * e + 64)
            vals = [r[:, lanes] for r in o_refs] + [r[:, e:e + 1] for r in l_refs]
            if not bwd:
                refs[6][:, lanes] = _lse_merge(*vals).astype(refs[6].dtype)
            else:
                _, vjp = jax.vjp(_lse_merge, *vals)
                g = vjp(refs[6][:, lanes])
                for r, v in zip(refs[7:10], g[:3]):
                    r[:, lanes] = v
                for r, v in zip(refs[10:13], g[3:]):
                    r[:, e:e + 1] = v

    blk = pl.BlockSpec((tm, 128), lambda i, hp: (i, hp))
    lblk = pl.BlockSpec((None, tm, 128), lambda i, hp: (hp, i, 0))
    lsd = jax.ShapeDtypeStruct((2, T, 128), F32)
    if not bwd:
        return _pcall(body, name=name, grid=(T // tm, 2), in_specs=[blk] * 3 + [lblk] * 3, out_specs=blk,
                      out_shape=jax.ShapeDtypeStruct((T, 256), F32),
                      compiler_params=_cp(("parallel", "parallel")))(*outs, *lses)
    return _pcall(body, name=name, grid=(T // tm, 2), in_specs=[blk] * 3 + [lblk] * 3 + [blk],
                  out_specs=[blk] * 3 + [lblk] * 3, out_shape=[jax.ShapeDtypeStruct((T, 256), F32)] * 3 + [lsd] * 3,
                  compiler_params=_cp(("parallel", "parallel")))(*outs, *lses, dy)


def _gmerge(g0, g1, g2, gb, ya, yb, yc):
    return (jax.nn.sigmoid(g0 + gb[:, 0:D]) * ya + jax.nn.sigmoid(g1 + gb[:, D:2 * D]) * yb
            + jax.nn.sigmoid(g2 + gb[:, 2 * D:3 * D]) * yc)


def _gated_merge(gates, gb, ya, yb, yc, name, gcol=0):
    def fn(tv, fv):
        return [_gmerge(tv[0], tv[1], tv[2], fv[0], tv[3], tv[4], tv[5])], []
    return _rowwise(fn, [(gates, D, gcol), (gates, D, gcol + 1), (gates, D, gcol + 2), ya, yb, yc], [gb], [(D, _ACT)], [],
                    name=name)[0]


def _gated_merge_bwd(gates, gb, ya, yb, yc, dm, name, gcol=0):
    def fn(tv, fv):
        _, vjp = jax.vjp(_gmerge, tv[0], tv[1], tv[2], fv[0], tv[3], tv[4], tv[5])
        d0, d1, d2, dgb, da, db, dc = vjp(tv[6])
        return [[d0, d1, d2], da, db, dc], [dgb]
    return _rowwise(fn, [(gates, D, gcol), (gates, D, gcol + 1), (gates, D, gcol + 2), ya, yb, yc, dm], [gb],
                    [(3 * D, _ACT), (D, _ACT), (D, _ACT), (D, _ACT)], [(1, 3 * D)], name=name)


def _pool_affine(t1, pb, ps, dout, name):
    if dout is None:
        def fn(tv, fv):
            return [(tv[0] + fv[0]) * fv[1]], []
        return _rowwise(fn, [t1], [pb, ps], [(POOLW, _ACT)], [], name=name)[0]

    def fnb(tv, fv):
        t2, vjp = jax.vjp(lambda t, b, s: (t + b) * s, tv[0], fv[0], fv[1])
        dt, db, dsc = vjp(tv[1])
        return [dt, t2], [db, dsc]
    return _rowwise(fnb, [t1, dout], [pb, ps], [(POOLW, _ACT), (POOLW, _ACT)], [(1, POOLW), (1, POOLW)], name=name)


def _dt_softplus(dt_raw, dt_bias, ddt, name):
    f = lambda r, b: _softplus(r + b)
    if ddt is None:
        def fn(tv, fv):
            return [f(tv[0], fv[0])], []
        return _rowwise(fn, [dt_raw], [dt_bias], [(16, F32)], [], name=name, tm=1024)[0]

    def fnb(tv, fv):
        _, vjp = jax.vjp(f, tv[0], fv[0])
        dr, db = vjp(tv[1])
        return [dr], [db]
    return _rowwise(fnb, [dt_raw, ddt], [dt_bias], [(16, F32)], [(1, 16)], name=name, tm=1024)


def _adamw_math(wv, gv, mv, vv):
    c1 = 1.0 / (1.0 - ADAM_B1 ** ADAM_STEP)
    c2 = 1.0 / (1.0 - ADAM_B2 ** ADAM_STEP)
    mn = ADAM_B1 * mv + (1.0 - ADAM_B1) * gv
    vn = ADAM_B2 * vv + (1.0 - ADAM_B2) * (gv * gv)
    delta = -ADAM_LR * ((mn * c1) / (jnp.sqrt(vn * c2) + ADAM_EPS) + ADAM_WD * wv)
    return delta, mn, vn


def _adamw(w, g, m, v, name):
    R, C = w.shape
    tm = _pick(R, (256, 128, 64, 32, 16, 8))
    return _rowwise(lambda tv, fv: (list(_adamw_math(*tv)), []), [w, g, m, v], [], [(C, F32)] * 3, [], name=name, tm=tm)


def _adamw_layer(i, w, g, m, v, accs, name, dep=None):
    R, C = w.shape
    r = R // NL
    tm = _pick(r, (256, 128, 64, 32, 16, 8))
    nt = r // tm
    if accs is None:
        accs = [lax.empty((R, C), F32) for _ in range(4)]
    extra = [] if dep is None else [dep]

    def body(w_ref, g_ref, m_ref, v_ref, *rest):
        go_ref, do_ref, mo_ref, vo_ref = rest[-4:]
        gv = g_ref[...]
        delta, mn, vn = _adamw_math(w_ref[...], gv, m_ref[...], v_ref[...])
        go_ref[...] = gv
        do_ref[...] = delta
        mo_ref[...] = mn
        vo_ref[...] = vn

    slab = pl.BlockSpec((tm, C), lambda t: (i * nt + t, 0))
    anyspec = pl.BlockSpec(memory_space=pl.ANY)
    return _pcall(body, name=name, grid=(nt,),
                  in_specs=[slab, pl.BlockSpec((tm, C), lambda t: (t, 0)), slab, slab] + [anyspec] * (4 + len(extra)),
                  out_specs=[slab] * 4, out_shape=[jax.ShapeDtypeStruct((R, C), F32)] * 4,
                  input_output_aliases={4 + k: k for k in range(4)},
                  compiler_params=_cp(("parallel",)))(w, g, m, v, *accs, *extra)


def _ffn_fwd(x, xm, w13, w2, g, b, tag, dep=None):
    ha, hg, s = _swiglu_fwd(xm, w13, dep, name=f"{tag}_h")
    y = _mm(s, w2, name=f"{tag}_y")
    r, out, outm = _res_ln_fwd(x, y, g, b, 0.5, name=f"{tag}_ln")
    return out, outm, dict(x=xm, ha=ha, hg=hg, r=r)


def _ffn_bwd(dout, sv, w13, w2, g, b, tag, dep=None):
    dskip, dy, dg, db = _ln_bwd(sv['r'], g, b, dout, 0.5, name=f"{tag}_lnb")
    ds = _mm(dy, w2, tb=True, dep=dep, name=f"{tag}_ds")
    dh, s = _swiglu_act_bwd(sv['ha'], sv['hg'], ds, name=f"{tag}_actb")
    dw2 = _mm(s, dy, ta=True, name=f"{tag}_dw2")
    dw13 = _mm(sv['x'], dh, ta=True, name=f"{tag}_dw13")
    dx = _mm(dh, w13, tb=True, add=dskip, name=f"{tag}_dx")
    return dx, dict(w13=dw13, w2=dw2, g=dg, b=db)


def _mixer_fwd(x1, x1m, W, bias_all, tag, dep=None):
    T = x1.shape[0]
    hcat = _mm(x1m, W['w_in_r'], dep=dep, name=f"{tag}_hcat")
    dt_raw = hcat[:, O_DT:O_DT + 16]
    pooled = _pool_mean(hcat, False, name=f"{tag}_pool", col0=O_U // 128)
    t1 = _mm(pooled, W['pool_wbd'], name=f"{tag}_pt1")
    t2 = _pool_affine(t1, W['pool_b'], W['pool_scale'], None, name=f"{tag}_paff")
    ya = _mm(t2, W['p_pool'], name=f"{tag}_ya")
    act = _conv_silu(hcat, W['conv_w'], W['conv_b'], name=f"{tag}_conv", col0=O_XBC // 128)
    dt = _dt_softplus(dt_raw, W['dt_bias'], None, name=f"{tag}_dt")
    dtg = dt.reshape(T, 4, 4).transpose(1, 0, 2)
    yscan, hs = _ssd_fwd(act, dtg, W['a_neg'], W['d_skip'], name=f"{tag}_ssd")
    ybn = _ssd_gate_norm(yscan, hcat, W['ssd_norm'], name=f"{tag}_gn", zcol=O_Z // 256)
    yb = _mm(ybn, W['p_ssd'], name=f"{tag}_yb")
    outs, lses = [], []
    for gi in range(len(ATTN_DILS)):
        o, l = _attn_fwd(hcat, bias_all, gi, name=f"{tag}_attn{gi}")
        outs.append(o)
        lses.append(l)
    ycp = _attn_merge(outs, lses, None, name=f"{tag}_amerge")
    yc = _mm(ycp, W['p_attn'], name=f"{tag}_yc")
    merged = _gated_merge(hcat, W['gate_b'], ya, yb, yc, name=f"{tag}_gm", gcol=O_G // D)
    mix = _mm(merged, W['w_out'], name=f"{tag}_mix")
    r, out, outm = _res_ln_fwd(x1, mix, W['ln2_g'], W['ln2_b'], 1.0, name=f"{tag}_ln")
    sv = dict(x1=x1m, dt_raw=dt_raw, pooled=pooled, t1=t1, act=act, dtg=dtg,
              hs=hs, yscan=yscan, ybn=ybn, hcat=hcat, outs=outs, lses=lses, ycp=ycp, ya=ya, yb=yb, yc=yc,
              merged=merged, r=r)
    return out, outm, sv


def _mixer_bwd(dout, sv, W, bias_all, tag, dep=None):
    T = dout.shape[0]
    gr = {}
    dx1a, dr, gr['ln2_g'], gr['ln2_b'] = _ln_bwd(sv['r'], W['ln2_g'], W['ln2_b'], dout, 1.0, name=f"{tag}_lnb")
    dmerged = _mm(dr, W['w_out'], tb=True, dep=dep, name=f"{tag}_dmerged")
    gr['w_out'] = _mm(sv['merged'], dr, ta=True, name=f"{tag}_dwout")
    dgates, dya, dyb, dyc, gr['gate_b'] = _gated_merge_bwd(sv['hcat'], W['gate_b'], sv['ya'], sv['yb'], sv['yc'],
                                                           dmerged, name=f"{tag}_gmb", gcol=O_G // D)
    dycp = _mm(dyc, W['p_attn'], tb=True, name=f"{tag}_dycp")
    gr['p_attn'] = _mm(sv['ycp'], dyc, ta=True, name=f"{tag}_dpattn")
    dml = _attn_merge(sv['outs'], sv['lses'], dycp, name=f"{tag}_amergeb")
    dq, dk, dv, dbias = [], [], [], []
    for gi in range(len(ATTN_DILS)):
        a, b, c, d = _attn_bwd(sv['hcat'], bias_all, gi, dml[gi], dml[3 + gi], name=f"{tag}_attnb{gi}")
        dq.append(a)
        dk.append(b)
        dv.append(c)
        dbias.append(d)
    dbias = jnp.concatenate(dbias, axis=0)
    dybn = _mm(dyb, W['p_ssd'], tb=True, name=f"{tag}_dybn")
    gr['p_ssd'] = _mm(sv['ybn'], dyb, ta=True, name=f"{tag}_dpssd")
    dyscan, dz, gr['ssd_norm'] = _ssd_gate_norm_bwd(sv['yscan'], sv['hcat'], W['ssd_norm'], dybn, name=f"{tag}_gnb",
                                                    zcol=O_Z // 256)
    dxs, ddtg, dbm, dcm, dak = _ssd_bwd(sv['act'], sv['dtg'], W['a_neg'], W['d_skip'], sv['hs'], dyscan,
                                        name=f"{tag}_ssdb")
    gr['a_neg'], gr['d_skip'] = dak[:, 0, 0:4], dak[:, 1, 0:4]
    ddt = ddtg.transpose(1, 0, 2).reshape(T, 16)
    ddt_raw, gr['dt_bias'] = _dt_softplus(sv['dt_raw'], W['dt_bias'], ddt, name=f"{tag}_dtb")
    dact = jnp.concatenate([dxs, dbm, dcm], axis=1)
    dxbc, gr['conv_w'], gr['conv_b'] = _conv_silu_bwd(sv['hcat'], W['conv_w'], W['conv_b'], dact, name=f"{tag}_convb",
                                                      col0=O_XBC // 128)
    dt2 = _mm(dya, W['p_pool'], tb=True, name=f"{tag}_dt2")
    dt1, t2, gr['pool_b'], gr['pool_scale'] = _pool_affine(sv['t1'], W['pool_b'], W['pool_scale'], dt2, name=f"{tag}_paffb")
    gr['p_pool'] = _mm(t2, dya, ta=True, name=f"{tag}_dppool")
    dpooled = _mm(dt1, W['pool_wbd'], tb=True, name=f"{tag}_dpooled")
    gr['pool_wbd'] = _mm(sv['pooled'], dt1, ta=True, name=f"{tag}_dpoolw")
    du = _pool_mean(dpooled, True, name=f"{tag}_poolb")
    dhcat = jnp.concatenate([t.astype(_ACT) for t in [du, dz, dxbc] + dq + dk + dv + [dgates, ddt_raw]]
                            + [jnp.zeros((T, HC - O_DT - 16), _ACT)], axis=1)
    dx1 = _mm(dhcat, W['w_in_r'], tb=True, add=dx1a, name=f"{tag}_dx1")
    gr['w_in_r'] = _mm(sv['x1'], dhcat, ta=True, name=f"{tag}_dwin")
    return dx1, gr, dbias


def _prep_layer_weights(i, inp, G):
    W = {}
    for n in BIG:
        if n not in G:
            continue
        g = G[n]
        if n == 'w_in':
            W['w_in_r'] = jnp.concatenate(_nat_pieces(g, 0, 3840) + _nat_pieces(g, 3856, 9232) + _nat_pieces(g, 3840, 3856)
                                          + [jnp.zeros((D, HC - 9232), g.dtype)], axis=1)
        elif n in COL_SHARDED:
            W[n] = jnp.concatenate([g[j] for j in range(4)], axis=1)
        else:
            W[n] = g.reshape(4 * g.shape[1], g.shape[2])
    pw = inp['pool_w'][i].astype(_MXU)
    wbd = jnp.zeros((POOLW, POOLW), _MXU)
    for g in range(4):
        wbd = lax.dynamic_update_slice(wbd, pw[g], (g * POOL_GDIM, g * POOL_GDIM))
    W['pool_wbd'] = wbd
    W['pool_b'] = inp['pool_b'][i].reshape(1, POOLW)
    W['pool_scale'] = inp['pool_scale'][i].reshape(1, POOLW)
    if 'conv_w' in G:
        W['conv_w'] = jnp.concatenate([G['conv_w'][j] for j in range(4)], axis=1)
        W['gate_b'] = jnp.concatenate([G['gate_b'][j][b:b + 1] for b in range(3) for j in range(4)], axis=1)
    W['conv_b'] = inp['conv_b'][i].reshape(1, 2048)
    W['dt_bias'] = inp['dt_bias'][i].reshape(1, 16)
    W['a_neg'] = (-jnp.exp(inp['a_log'][i])).reshape(4, 1, 4)
    W['d_skip'] = inp['d_skip'][i].reshape(4, 1, 4)
    W['ssd_norm'] = inp['ssd_norm'][i].reshape(1, D)
    for n in ('ln1_g', 'ln1_b', 'ln2_g', 'ln2_b', 'ln3_g', 'ln3_b'):
        W[n] = inp[n][i].reshape(1, D)
    return W


GATHER_FIRST = ['ffn1_w13', 'ffn1_w2']
GATHER_REST = [n for n in BIG if n not in GATHER_FIRST] + ['gate_b', 'conv_w']


def _gather_start(inp, i, names):
    core = lax.axis_index("c")
    arrs = []
    for n in names:
        s = inp[n][i]
        if n in BIG:
            s = lax.dynamic_slice_in_dim(s, core * (s.shape[0] // 2), s.shape[0] // 2, axis=0).astype(BF16)
        arrs.append(s)
    state, token = _exchange_start(arrs, "chips", "gather", name="gather_start")
    return (names, state), token


def _gather_mid(handle, after):
    names, state = handle
    me = 2 * lax.axis_index("x") + lax.axis_index("y")
    own, outs = _exchange_wait(state, after, "chips", "gather", name="gather_wait")
    outs = [lax.dynamic_update_slice(o, a[None], (me, 0, 0)) for o, a in zip(outs, own)]
    big = [o for n, o in zip(names, outs) if n in BIG]
    state, token = _exchange_start(big, "cores", "gather", name="share_start")
    return (names, outs, state), token


def _gather_finish(handle, after):
    names, outs, state = handle
    core = lax.axis_index("c")
    mine, theirs = _exchange_wait(state, after, "cores", "gather", name="share_wait")
    G = {n: o for n, o in zip(names, outs) if n not in BIG}
    for n, a, b in zip([n for n in names if n in BIG], mine, theirs):
        G[n] = jnp.concatenate([jnp.where(core == 0, a, b), jnp.where(core == 0, b, a)], axis=1)
    return G


W_IN_SHARD = 2308


def _nat_pieces(g, lo, hi):
    out = []
    for j in range(4):
        s, e = max(lo, W_IN_SHARD * j), min(hi, W_IN_SHARD * (j + 1))
        if s < e:
            out.append(g[j][:, s - W_IN_SHARD * j:e - W_IN_SHARD * j])
    return out


def _reord_ranges(lo, hi):
    out = []
    for a, b, off in ((0, 3840, 0), (3840, 3856, O_DT - 3840), (3856, 9232, -16)):
        s, e = max(lo, a), min(hi, b)
        if s < e:
            out.append((s + off, e + off))
    return out


def _halves_of(n, g):
    if n == 'w_in':
        shards = [jnp.concatenate([g[:, a:b] for a, b in _reord_ranges(W_IN_SHARD * j, W_IN_SHARD * (j + 1))], axis=1)
                  for j in range(4)]
    elif n in COL_SHARDED:
        c = g.shape[1] // 4
        shards = [g[:, j * c:(j + 1) * c] for j in range(4)]
    else:
        r = g.shape[0] // 4
        shards = [g[j * r:(j + 1) * r] for j in range(4)]
    r2 = shards[0].shape[0] // 2
    return jnp.stack([jnp.concatenate([s[h * r2:(h + 1) * r2] for s in shards], axis=0) for h in range(2)])


def _reduce_a(grads):
    names = list(grads)
    halves = [_halves_of(n, grads[n]) for n in names]
    state, token = _exchange_start(halves, "cores", "scatter", name="rsc_start")
    return (names, state), token


def _reduce_b(handle, after):
    names, state = handle
    core = lax.axis_index("c").reshape(1)
    halves, got = _exchange_wait(state, after, "cores", "scatter", name="rsc_wait")
    chip = [_sum_own_recv(h, t, core, BF16, name="rs_sum2") for h, t in zip(halves, got)]
    chip = [t.reshape(4, t.shape[0] // 4, t.shape[1]) for t in chip]
    state, token = _exchange_start(chip, "chips", "scatter", name="rs_start")
    return (names, state), token


def _reduce_c(handle, after):
    names, state = handle
    chip_id = (2 * lax.axis_index("x") + lax.axis_index("y")).reshape(1)
    chip, got = _exchange_wait(state, after, "chips", "scatter", name="rs_wait")
    red = [_sum_own_recv(h, t, chip_id, F32, name="rs_sum4") for h, t in zip(chip, got)]
    other = _exchange(red, "cores", "gather", name="rs_share")
    out = {}
    for n, mine, theirs in zip(names, red, other):
        out[n] = jnp.where(lax.axis_index("c") == 0, jnp.concatenate([mine, theirs]), jnp.concatenate([theirs, mine]))
    return out


class _Comm:
    def __init__(self, inp):
        self.inp = inp

    def gather_start(self, i, names):
        return _gather_start(self.inp, i, names)

    gather_mid = staticmethod(_gather_mid)
    gather_finish = staticmethod(_gather_finish)

    def reduce_a(self, i, grads):
        return _reduce_a({n: grads[n] for n in BIG})

    reduce_b = staticmethod(_reduce_b)
    reduce_c = staticmethod(_reduce_c)


def _allreduce_small(vec, dep=None):
    for group in ("cores", "x", "y"):
        recv = _exchange([vec], group, "gather", name=f"ar_{group}", dep=dep if group == "cores" else None)[0]
        vec = _rowwise(lambda tv, fv: ([tv[0] + tv[1]], []), [vec, recv], [], [(128, F32)], [], name=f"ar_add_{group}")[0]
    return vec


def _pack(arrs):
    flat = jnp.concatenate([a.reshape(-1) for a in arrs])
    n = flat.shape[0]
    pad = (-n) % (256 * 128)
    flat = jnp.concatenate([flat, jnp.zeros((pad,), F32)])
    return flat.reshape(-1, 128)


def _unpack(p, shapes):
    flat = p.reshape(-1)
    out, off = [], 0
    for s in shapes:
        sz = int(np.prod(s))
        out.append(flat[off:off + sz].reshape(s))
        off += sz
    return out


def _forward_backward(inp, comm, bias_all):
    x = xm = inp['x'].reshape(-1, D)
    tgt = inp['loss_target'].reshape(-1, D)
    saved, Ws = [], []
    h_first, _ = comm.gather_start(0, GATHER_FIRST)
    h_rest, dep = comm.gather_start(0, GATHER_REST)
    h_first, tok = comm.gather_mid(h_first, x)
    G = comm.gather_finish(h_first, tok)
    for i in range(NL):
        W = _prep_layer_weights(i, inp, G)
        start_next = lambda: (comm.gather_start(i + 1, BIG + ['gate_b', 'conv_w']) if i + 1 < NL else (None, None))
        if i > 0:
            h_next, dep = start_next()
        x1, x1m, s1 = _ffn_fwd(x, xm, W['ffn1_w13'], W['ffn1_w2'], W['ln1_g'], W['ln1_b'], "f1", dep)
        if i == 0:
            h_rest, tok = comm.gather_mid(h_rest, x1m)
            W.update(_prep_layer_weights(i, inp, comm.gather_finish(h_rest, tok)))
            h_next, dep = start_next()
        x2, x2m, s2 = _mixer_fwd(x1, x1m, W, bias_all, "mx", dep if i == 0 else None)
        dep = None
        if h_next is not None:
            h_next, dep = comm.gather_mid(h_next, x2m)
        x, xm, s3 = _ffn_fwd(x2, x2m, W['ffn2_w13'], W['ffn2_w2'], W['ln3_g'], W['ln3_b'], "f2", dep)
        if h_next is not None:
            G = comm.gather_finish(h_next, xm)
        saved.append((s1, s2, s3))
        Ws.append(W)
    dy, lpart = _loss_fwd_bwd(x, tgt, name="loss")
    fins, reduced, dbiases = [None] * NL, [None] * NL, [None] * NL
    pend_a, pend_b, dep = None, None, None
    for i in reversed(range(NL)):
        W = Ws[i]
        s1, s2, s3 = saved[i]
        g = {}
        dx2, f = _ffn_bwd(dy, s3, W['ffn2_w13'], W['ffn2_w2'], W['ln3_g'], W['ln3_b'], "f2", dep)
        g['ffn2_w13'], g['ffn2_w2'], g['ln3_g'], g['ln3_b'] = f['w13'], f['w2'], f['g'], f['b']
        dep = None
        if pend_a is not None:
            handle, dep = comm.reduce_b(pend_a[1], dx2)
            pend_b = (pend_a[0], handle)
        dx1, gm, dbiases[i] = _mixer_bwd(dx2, s2, W, bias_all, "mx", dep)
        g.update(gm)
        dy, f = _ffn_bwd(dx1, s1, W['ffn1_w13'], W['ffn1_w2'], W['ln1_g'], W['ln1_b'], "f1")
        g['ffn1_w13'], g['ffn1_w2'], g['ln1_g'], g['ln1_b'] = f['w13'], f['w2'], f['g'], f['b']
        fins[i] = _finish_layer_grads(i, g, inp)
        if pend_b is not None:
            reduced[pend_b[0]] = comm.reduce_c(pend_b[1], dy)
            pend_b = None
        handle, dep = comm.reduce_a(i, fins[i])
        pend_a = (i, handle)
    return lpart, dy, fins, reduced, pend_a, dbiases, dep


def _finish_layer_grads(i, g, inp):
    out = {n: g[n] for n in BIG if n != 'w_in'}
    out['w_in'] = g['w_in_r']
    out['pool_w'] = jnp.stack([g['pool_wbd'][k * POOL_GDIM:(k + 1) * POOL_GDIM, k * POOL_GDIM:(k + 1) * POOL_GDIM] for k in range(4)])
    out['pool_b'] = g['pool_b'].reshape(4, POOL_GDIM)
    out['pool_scale'] = g['pool_scale'].reshape(POOLW)
    out['conv_w'] = g['conv_w']
    out['conv_b'] = g['conv_b'].reshape(2048)
    out['dt_bias'] = g['dt_bias'].reshape(16)
    out['a_log'] = (g['a_neg'].reshape(16)) * (-jnp.exp(inp['a_log'][i]))
    out['d_skip'] = g['d_skip'].reshape(16)
    out['ssd_norm'] = g['ssd_norm'].reshape(D)
    out['gate_b'] = g['gate_b'].reshape(3, D)
    for n in ('ln1_g', 'ln1_b', 'ln2_g', 'ln2_b', 'ln3_g', 'ln3_b'):
        out[n] = g[n].reshape(D)
    return out


def kernel(x, ffn1_w13, ffn1_w2, ln1_g, ln1_b, w_in, gate_b, pool_w, pool_b, pool_scale, conv_w, conv_b,
           dt_bias, a_log, d_skip, ssd_norm, rel_bias, p_pool, p_ssd, p_attn, w_out, ln2_g, ln2_b, ffn2_w13,
           ffn2_w2, ln3_g, ln3_b, loss_target, m_ffn1_w13, m_ffn1_w2, m_ln1_g, m_ln1_b, m_w_in, m_gate_b,
           m_pool_w, m_pool_b, m_pool_scale, m_conv_w, m_conv_b, m_dt_bias, m_a_log, m_d_skip, m_ssd_norm,
           m_rel_bias, m_p_pool, m_p_ssd, m_p_attn, m_w_out, m_ln2_g, m_ln2_b, m_ffn2_w13, m_ffn2_w2, m_ln3_g,
           m_ln3_b, v_ffn1_w13, v_ffn1_w2, v_ln1_g, v_ln1_b, v_w_in, v_gate_b, v_pool_w, v_pool_b,
           v_pool_scale, v_conv_w, v_conv_b, v_dt_bias, v_a_log, v_d_skip, v_ssd_norm, v_rel_bias, v_p_pool,
           v_p_ssd, v_p_attn, v_w_out, v_ln2_g, v_ln2_b, v_ffn2_w13, v_ffn2_w2, v_ln3_g, v_ln3_b):
    inp = dict(locals())
    maps = jnp.asarray(_bucket_maps())
    bias_all = _bias_build(rel_bias, maps, name="bias_build")
    comm = _Comm(inp)
    lpart, gx, fins, red, pending, dbiases, halves_started = _forward_backward(inp, comm, bias_all)
    loss = lax.psum(lpart[0, 0], ("x", "y", "c"))

    small_l = [n for n in SMALL if n != 'rel_bias']
    drel = _bias_reduce(jnp.stack(dbiases), maps, name="bias_reduce")[:, 0, :32].T
    small_arrs = [jnp.stack([fins[i][n] for i in range(NL)]) for n in small_l] + [drel]
    packed = _allreduce_small(_pack(small_arrs), dep=halves_started)
    handle_b, started = comm.reduce_b(pending[1], packed)
    gsmall = dict(zip(small_l + ['rel_bias'], _unpack(packed, [a.shape for a in small_arrs])))
    shard = 2 * lax.axis_index("x") + lax.axis_index("y")
    gsmall['gate_b'] = lax.dynamic_slice_in_dim(gsmall['gate_b'], shard * 256, 256, axis=2)
    gsmall['conv_w'] = lax.dynamic_slice_in_dim(gsmall['conv_w'], shard * 512, 512, axis=2)
    gout, delta, new_m, new_v = dict(gsmall), {}, {}, {}
    shapes = [inp[n].shape for n in SMALL]
    d, m, v = _adamw(_pack([inp[n] for n in SMALL]), _pack([gsmall[n] for n in SMALL]),
                     _pack([inp['m_' + n] for n in SMALL]), _pack([inp['v_' + n] for n in SMALL]), name="adamw_small")
    for n, dd, mm, vv in zip(SMALL, _unpack(d, shapes), _unpack(m, shapes), _unpack(v, shapes)):
        delta[n], new_m[n], new_v[n] = dd, mm, vv

    two_d = lambda a: a.reshape(a.shape[0] * a.shape[1], a.shape[2])
    accs = {n: None for n in BIG}

    def adamw_layer(i, dep=None):
        for n in BIG:
            accs[n] = _adamw_layer(i, two_d(inp[n]), red[i][n], two_d(inp['m_' + n]), two_d(inp['v_' + n]), accs[n],
                                   name="adamw_big", dep=dep)

    done = [i for i in range(NL) if i != pending[0]]
    for i in done:
        adamw_layer(i, dep=started)
    red[pending[0]] = comm.reduce_c(handle_b, [d] + ([accs[n][1] for n in BIG] if done else []))
    adamw_layer(pending[0])
    for n in BIG:
        gout[n], delta[n], new_m[n], new_v[n] = [a.reshape(inp[n].shape) for a in accs[n]]

    return (loss, gx.reshape(x.shape), *[gout[n] for n in WEIGHTS], *[delta[n] for n in WEIGHTS],
            *[new_m[n] for n in WEIGHTS], *[new_v[n] for n in WEIGHTS])
```

```python
import functools

import numpy as np
import jax
import jax.numpy as jnp
from jax import lax
from jax.experimental import pallas as pl
from jax.experimental.pallas import tpu as pltpu

F32 = jnp.float32
BF16 = jnp.bfloat16
_MXU = jnp.bfloat16
_ACT = jnp.bfloat16
_VMEM_LIMIT = 56 * 1024 * 1024

S = 2048
D = 1024
NL = 4
DFF = 2816
LN_EPS = 1e-5
SSD_EPS = 1e-5
ALPHA = (2.0 * NL) ** 0.25
POOLW = 768
POOL_WINDOWS = (2, 4, 8, 16)
POOL_GDIM = 192
CH = 128
ATTN_DILS = (1, 4, 16)
HC = 9728
O_U, O_Z, O_XBC, O_Q, O_K, O_V, O_G, O_DT = 0, 768, 1792, 3840, 4608, 5376, 6144, 9216

ADAM_LR, ADAM_B1, ADAM_B2, ADAM_EPS, ADAM_WD, ADAM_STEP = 0.001, 0.9, 0.999, 1e-08, 0.01, 10

WEIGHTS = ['ffn1_w13', 'ffn1_w2', 'ln1_g', 'ln1_b', 'w_in', 'gate_b', 'pool_w', 'pool_b', 'pool_scale', 'conv_w',
           'conv_b', 'dt_bias', 'a_log', 'd_skip', 'ssd_norm', 'rel_bias', 'p_pool', 'p_ssd', 'p_attn', 'w_out',
           'ln2_g', 'ln2_b', 'ffn2_w13', 'ffn2_w2', 'ln3_g', 'ln3_b']
BIG = ['ffn1_w13', 'ffn1_w2', 'w_in', 'p_pool', 'p_ssd', 'p_attn', 'w_out', 'ffn2_w13', 'ffn2_w2']
COL_SHARDED = {'ffn1_w13', 'ffn2_w13', 'w_in', 'p_pool', 'p_attn'}
SMALL = [n for n in WEIGHTS if n not in BIG]


def _pcall(body, **kw):
    return pl.pallas_call(body, **kw)


def _cp(sem=None):
    return pltpu.CompilerParams(dimension_semantics=sem, vmem_limit_bytes=_VMEM_LIMIT)


def _pick(n, cands):
    for c in cands:
        if n % c == 0:
            return c
    raise ValueError(f"no tile for {n}")


def _mm(a, b, *, ta=False, tb=False, add=None, out_dtype=F32, dep=None, name):
    if ta:
        K, M = a.shape
    else:
        M, K = a.shape
    if tb:
        N, K2 = b.shape
    else:
        K2, N = b.shape
    assert K == K2, (a.shape, b.shape, ta, tb)
    sa, sb, so = a.dtype.itemsize, b.dtype.itemsize, jnp.dtype(out_dtype).itemsize
    tm, tn, tk = _mm_tiles(M, N, K, sa, sb, so + (4 if add is not None else 0))
    nk = K // tk
    a_bytes, b_bytes = M * K * sa, K * N * sb
    j_outer = nk == 1 and (b_bytes + a_bytes * (N // tn) < a_bytes + b_bytes * (M // tm))
    ij = (lambda p, q: (q, p)) if j_outer else (lambda p, q: (p, q))

    def im(f):
        return lambda p, q, k: f(*ij(p, q), k)

    a_spec = pl.BlockSpec((tk, tm), im(lambda i, j, k: (k, i))) if ta else pl.BlockSpec((tm, tk), im(lambda i, j, k: (i, k)))
    b_spec = pl.BlockSpec((tn, tk), im(lambda i, j, k: (j, k))) if tb else pl.BlockSpec((tk, tn), im(lambda i, j, k: (k, j)))
    o_spec = pl.BlockSpec((tm, tn), im(lambda i, j, k: (i, j)))
    dims = (((0 if ta else 1,), (1 if tb else 0,)), ((), ()))
    has_add = add is not None

    n_in = 2 + int(has_add) + int(dep is not None)

    def body(*refs):
        a_ref, b_ref = refs[0], refs[1]
        add_ref = refs[2] if has_add else None
        o_ref = refs[n_in]
        part = lax.dot_general(a_ref[...].astype(_MXU), b_ref[...].astype(_MXU), dims, preferred_element_type=F32)

        def finish(r):
            if has_add:
                r = r + add_ref[...]
            o_ref[...] = r.astype(out_dtype)

        if nk == 1:
            finish(part)
        else:
            acc = refs[-1]
            k = pl.program_id(2)

            @pl.when(k == 0)
            def _():
                acc[...] = part

            @pl.when(k > 0)
            def _():
                acc[...] += part

            @pl.when(k == nk - 1)
            def _():
                finish(acc[...])

    in_specs = [a_spec, b_spec]
    args = [a, b]
    if has_add:
        in_specs.append(o_spec)
        args.append(add)
    if dep is not None:
        in_specs.append(pl.BlockSpec(memory_space=pl.ANY))
        args.append(dep)
    gm, gn = M // tm, N // tn
    return _pcall(
        body, name=name, grid=((gn, gm, nk) if j_outer else (gm, gn, nk)), in_specs=in_specs, out_specs=o_spec,
        out_shape=jax.ShapeDtypeStruct((M, N), out_dtype),
        scratch_shapes=([pltpu.VMEM((tm, tn), F32)] if nk > 1 else []),
        compiler_params=_cp(("parallel", "parallel", "arbitrary")),
    )(*args)


_MM_VMEM_BUDGET = 40 * 1024 * 1024


def _divisors128(n, cap):
    return [d for d in range(128, min(n, cap) + 1, 128) if n % d == 0][::-1]


_MM_CYC_PER_MMAC = 4.35
_MM_CYC_PER_ACC_VREG = 2.03
_MM_HBM_BYTES_PER_CYC = 1455.0
_MM_CYC_PER_STEP = 770.0


def _mm_tiles(M, N, K, sa, sb, so):
    best = None
    for tm in _divisors128(M, 1408):
        for tn in _divisors128(N, 2560):
            for tk in ([K] if K <= 4096 else []) + _divisors128(K, 2816):
                nk = K // tk
                need = 2 * (tm * tk * sa + tk * tn * sb + tm * tn * so) + (tm * tn * 4 if nk > 1 else 0)
                need += tm * tk * 2 + tk * tn * 2 + tm * tn * 4
                if need > _MM_VMEM_BUDGET:
                    continue
                gm, gn = M // tm, N // tn
                a_bytes, b_bytes = M * K * sa, K * N * sb
                hbm = min(b_bytes + a_bytes * gn, a_bytes + b_bytes * gm) if nk == 1 else a_bytes * gn + b_bytes * gm
                hbm += M * N * so
                work = _MM_CYC_PER_MMAC * M * N * K / 1e6 + _MM_CYC_PER_ACC_VREG * (M * N / 1024) * (nk if nk > 1 else 0.5)
                cost = max(work, hbm / _MM_HBM_BYTES_PER_CYC) + gm * gn * nk * _MM_CYC_PER_STEP
                if best is None or cost < best[0]:
                    best = (cost, (tm, tn, tk))
    assert best is not None, (M, N, K)
    return best[1]


def _store(ref, val):
    if isinstance(val, (list, tuple)):
        off = 0
        for p in val:
            w = p.shape[1]
            ref[:, off:off + w] = p.astype(ref.dtype)
            off += w
    else:
        ref[...] = val.astype(ref.dtype)


def _acc_store(ref, val, first):
    pieces = val if isinstance(val, (list, tuple)) else [val]
    off = 0
    for p in pieces:
        w = p.shape[1]

        @pl.when(first)
        def _(p=p, off=off, w=w):
            ref[:, off:off + w] = p

        @pl.when(jnp.logical_not(first))
        def _(p=p, off=off, w=w):
            ref[:, off:off + w] += p

        off += w


def _rowwise(fn, tiled, full, out_tiled, out_acc, *, name, tm=256):
    arrs, specs = [], []
    for t in tiled:
        arr, w, cb = t if isinstance(t, tuple) else (t, t.shape[1], 0)
        arrs.append(arr)
        specs.append(pl.BlockSpec((tm, w), functools.partial(lambda i, cb: (i, cb), cb=cb)))
    R = arrs[0].shape[0]
    assert R % tm == 0
    for f in full:
        arrs.append(f)
        specs.append(pl.BlockSpec(f.shape, functools.partial(lambda i, nd: (0,) * nd, nd=f.ndim)))
    nt, nf, no = len(tiled), len(full), len(out_tiled)

    def body(*refs):
        tv = [r[...] for r in refs[:nt]]
        fv = [r[...] for r in refs[nt:nt + nf]]
        ot, oa = fn(tv, fv)
        for r, v in zip(refs[nt + nf:nt + nf + no], ot):
            _store(r, v)
        first = pl.program_id(0) == 0
        for r, v in zip(refs[nt + nf + no:], oa):
            _acc_store(r, v, first)

    out_shape = [jax.ShapeDtypeStruct((R, c), dt) for c, dt in out_tiled]
    out_specs = [pl.BlockSpec((tm, c), lambda i: (i, 0)) for c, _ in out_tiled]
    for shp in out_acc:
        out_shape.append(jax.ShapeDtypeStruct(shp, F32))
        out_specs.append(pl.BlockSpec(shp, lambda i: (0, 0)))
    return _pcall(body, name=name, grid=(R // tm,), in_specs=specs, out_specs=out_specs, out_shape=out_shape,
                  compiler_params=_cp(("arbitrary",)))(*arrs)


def _group(group):
    x, y, c = lax.axis_index("x"), lax.axis_index("y"), lax.axis_index("c")
    if group == "chips":
        return 2 * x + y, [((x, 1 - y, c), 2 * x + 1 - y), ((1 - x, y, c), 2 * (1 - x) + y),
                           ((1 - x, 1 - y, c), 2 * (1 - x) + 1 - y)]
    if group == "cores":
        return c, [((x, y, 1 - c), 1 - c)]
    if group == "x":
        return x, [((1 - x, y, c), 1 - x)]
    return y, [((x, 1 - y, c), 1 - y)]


def _exchange(arrs, group, mode, name, dep=None):
    chips = group == "chips"
    k = len(arrs)
    npeer = 3 if chips else 1

    def body(*refs):
        nd = 0 if dep is None else 1
        ins, outs = refs[:k], refs[k + nd:2 * k + nd]
        send_sems, recv_sems = refs[2 * k + nd:]
        me, peers = _group(group)
        remote = []
        for i in range(k):
            for p, (dev, slot) in enumerate(peers):
                src = ins[i].at[slot] if mode == "scatter" else ins[i]
                if not chips:
                    dst = outs[i]
                else:
                    dst = outs[i].at[p] if mode == "scatter" else outs[i].at[me]
                cp = pltpu.make_async_remote_copy(src_ref=src, dst_ref=dst, send_sem=send_sems.at[i, p],
                                                  recv_sem=recv_sems.at[i, p], device_id=dev,
                                                  device_id_type=pl.DeviceIdType.MESH)
                cp.start()
                remote.append(cp)
        for cp in remote:
            cp.wait_recv()
        for cp in remote:
            cp.wait_send()

    def oshape(a):
        piece = a.shape[1:] if mode == "scatter" else a.shape
        if chips:
            piece = ((3,) if mode == "scatter" else (4,)) + piece
        return jax.ShapeDtypeStruct(piece, a.dtype)

    any_spec = pl.BlockSpec(memory_space=pl.ANY)
    extra = [] if dep is None else [dep]
    return _pcall(body, name=name, in_specs=[any_spec] * (k + len(extra)), out_specs=[any_spec] * k,
                  out_shape=[oshape(a) for a in arrs],
                  scratch_shapes=[pltpu.SemaphoreType.DMA((k, npeer)), pltpu.SemaphoreType.DMA((k, npeer))])(*arrs, *extra)


def _split_copies(ins, lands, send_sems, recv_sems, group, mode):
    chips = group == "chips"
    me, peers = _group(group)
    npeer = len(peers)
    out = []
    for i in range(len(ins)):
        for p, (dev, slot) in enumerate(peers):
            src = ins[i].at[slot] if mode == "scatter" else ins[i]
            if not chips:
                dst = lands[i]
            else:
                dst = lands[i].at[p] if mode == "scatter" else lands[i].at[me]
            out.append(pltpu.make_async_remote_copy(src_ref=src, dst_ref=dst, send_sem=send_sems.at[npeer * i + p],
                                                    recv_sem=recv_sems.at[npeer * i + p], device_id=dev,
                                                    device_id_type=pl.DeviceIdType.MESH))
    return out


def _exchange_start(arrs, group, mode, name):
    k = len(arrs)
    chips = group == "chips"
    nsem = (3 if chips else 1) * k
    hbm = pl.BlockSpec(memory_space=pltpu.HBM)
    sem = pl.BlockSpec(memory_space=pltpu.SEMAPHORE)

    def land_shape(a):
        piece = a.shape[1:] if mode == "scatter" else a.shape
        if chips:
            piece = ((3,) if mode == "scatter" else (4,)) + piece
        return piece

    def body(*refs):
        ins, lands = refs[:k], refs[k:2 * k]
        send_sems, recv_sems = refs[2 * k], refs[2 * k + 1]
        token = refs[-1]
        for cp in _split_copies(ins, lands, send_sems, recv_sems, group, mode):
            cp.start()
        token[...] = jnp.zeros_like(token)

    srcs = [pltpu.with_memory_space_constraint(a, pltpu.HBM) for a in arrs]
    lands = [pltpu.with_memory_space_constraint(lax.empty(land_shape(a), a.dtype), pltpu.HBM) for a in arrs]
    out_shape = ([pltpu.SemaphoreType.DMA((nsem,)), pltpu.SemaphoreType.DMA((nsem,))]
                 + [pltpu.HBM(a.shape, a.dtype) for a in arrs] + [pltpu.HBM(land_shape(a), a.dtype) for a in arrs]
                 + [jax.ShapeDtypeStruct((8, 128), F32)])
    outs = _pcall(body, name=name, in_specs=[hbm] * (2 * k),
                  out_specs=[sem, sem] + [hbm] * (2 * k) + [pl.BlockSpec(memory_space=pltpu.VMEM)], out_shape=out_shape,
                  input_output_aliases={i: 2 + i for i in range(2 * k)},
                  compiler_params=pltpu.CompilerParams(has_side_effects=pltpu.SideEffectType.DATAFLOW_SIDE_EFFECTING))(
                      *srcs, *lands)
    return (outs[0], outs[1], list(outs[2:2 + k]), list(outs[2 + k:2 + 2 * k])), outs[-1]


def _exchange_wait(state, after, group, mode, name):
    send_sems, recv_sems, srcs, lands = state
    k = len(srcs)
    after = list(after) if isinstance(after, (list, tuple)) else [after]
    hbm = pl.BlockSpec(memory_space=pltpu.HBM)
    sem = pl.BlockSpec(memory_space=pltpu.SEMAPHORE)

    def body(*refs):
        ins, lnd = refs[:k], refs[k:2 * k]
        send_sems, recv_sems = refs[2 * k], refs[2 * k + 1]
        for cp in _split_copies(ins, lnd, send_sems, recv_sems, group, mode):
            cp.wait_send()
            cp.wait_recv()

    outs = _pcall(body, name=name,
                  in_specs=[hbm] * (2 * k) + [sem, sem] + [pl.BlockSpec(memory_space=pl.ANY)] * len(after),
                  out_specs=[hbm] * (2 * k),
                  out_shape=[pltpu.HBM(a.shape, a.dtype) for a in srcs] + [pltpu.HBM(a.shape, a.dtype) for a in lands],
                  input_output_aliases={i: i for i in range(2 * k)},
                  compiler_params=pltpu.CompilerParams(has_side_effects=pltpu.SideEffectType.DATAFLOW_SIDE_EFFECTING))(
                      *srcs, *lands, send_sems, recv_sems, *after)
    return list(outs[:k]), list(outs[k:])


def _sum_own_recv(own, recv, me, out_dtype, name):
    n, R, C = own.shape
    nr = 1 if recv.ndim == 2 else recv.shape[0]
    tr = _pick(R, (256, 128, 64, 32, 16, 8))

    def body(me_ref, own_ref, *refs):
        o_ref = refs[-1]
        acc = own_ref[...].astype(F32)
        for r in refs[:-1]:
            acc = acc + r[...].astype(F32)
        o_ref[...] = acc.astype(out_dtype)

    specs = [pl.BlockSpec((None, tr, C), lambda i, me_ref: (me_ref[0], i, 0))]
    args = [own]
    if recv.ndim == 2:
        specs.append(pl.BlockSpec((tr, C), lambda i, me_ref: (i, 0)))
        args.append(recv)
    else:
        for p in range(nr):
            specs.append(pl.BlockSpec((None, tr, C), functools.partial(lambda i, me_ref, p: (p, i, 0), p=p)))
            args.append(recv)
    gs = pltpu.PrefetchScalarGridSpec(num_scalar_prefetch=1, grid=(R // tr,), in_specs=specs,
                                      out_specs=pl.BlockSpec((tr, C), lambda i, me_ref: (i, 0)))
    return _pcall(body, name=name, grid_spec=gs, out_shape=jax.ShapeDtypeStruct((R, C), out_dtype),
                  compiler_params=_cp(("parallel",)))(me, *args)


def _silu(x):
    return x * jax.nn.sigmoid(x)


def _ln(r, g, b):
    mu = jnp.mean(r, -1, keepdims=True)
    xc = r - mu
    var = jnp.mean(xc * xc, -1, keepdims=True)
    return xc * lax.rsqrt(var + LN_EPS) * g + b


def _softplus(x):
    return jnp.maximum(x, 0.0) + jnp.log1p(jnp.exp(-jnp.abs(x)))


RES_LN_TM = 512


def _mm_res_ln(a, w, x, g, b, res, name):
    T, K = a.shape
    tm = RES_LN_TM

    def body(a_ref, w_ref, x_ref, g_ref, b_ref, r_ref, o_ref, om_ref):
        y = jnp.dot(a_ref[...].astype(_MXU), w_ref[...].astype(_MXU), preferred_element_type=F32)
        r = ALPHA * x_ref[...] + res * y
        out = _ln(r, g_ref[...], b_ref[...])
        r_ref[...] = r
        o_ref[...] = out
        om_ref[...] = out.astype(om_ref.dtype)

    row = pl.BlockSpec((tm, D), lambda i: (i, 0))
    vec = pl.BlockSpec((1, D), lambda i: (0, 0))
    return _pcall(body, name=name, grid=(T // tm,),
                  in_specs=[pl.BlockSpec((tm, K), lambda i: (i, 0)), pl.BlockSpec((K, D), lambda i: (0, 0)), row, vec, vec],
                  out_specs=[row, row, row],
                  out_shape=[jax.ShapeDtypeStruct((T, D), F32), jax.ShapeDtypeStruct((T, D), F32),
                             jax.ShapeDtypeStruct((T, D), _ACT)],
                  compiler_params=_cp(("parallel",)))(a, w, x, g, b)


def _ln_bwd(r, g, b, dout, res, name):
    def fn(tv, fv):
        _, vjp = jax.vjp(_ln, tv[0], fv[0], fv[1])
        dr, dg, db = vjp(tv[1])
        return [ALPHA * dr, res * dr], [dg, db]
    return _rowwise(fn, [r, dout], [g, b], [(D, F32), (D, _ACT)], [(1, D), (1, D)], name=name)


SWIGLU_TM, SWIGLU_TN = 512, 1408


def _swiglu_fwd(x, w13, dep, name):
    T, K = x.shape
    tm, tn = SWIGLU_TM, SWIGLU_TN
    nj = DFF // tn
    has_dep = dep is not None

    def body(x_ref, wa_ref, wg_ref, *rest):
        a_ref, g_ref, s_ref = rest[-3:]
        xv = x_ref[...].astype(_MXU)
        a = jnp.dot(xv, wa_ref[...].astype(_MXU), preferred_element_type=F32)
        g = jnp.dot(xv, wg_ref[...].astype(_MXU), preferred_element_type=F32)
        a_ref[...] = a
        g_ref[...] = g
        s_ref[...] = (_silu(a) * g).astype(s_ref.dtype)

    out = pl.BlockSpec((tm, tn), lambda j, i: (i, j))
    in_specs = [pl.BlockSpec((tm, K), lambda j, i: (i, 0)), pl.BlockSpec((K, tn), lambda j, i: (0, j)),
                pl.BlockSpec((K, tn), lambda j, i: (0, nj + j))]
    args = [x, w13, w13]
    if has_dep:
        in_specs.append(pl.BlockSpec(memory_space=pl.ANY))
        args.append(dep)
    return _pcall(body, name=name, grid=(nj, T // tm), in_specs=in_specs, out_specs=[out, out, out],
                  out_shape=[jax.ShapeDtypeStruct((T, DFF), F32), jax.ShapeDtypeStruct((T, DFF), F32),
                             jax.ShapeDtypeStruct((T, DFF), _ACT)],
                  compiler_params=_cp(("parallel", "parallel")))(*args)


def _swiglu_act_bwd(a, g, ds, name):
    def fn(tv, fv):
        s, vjp = jax.vjp(lambda a, g: _silu(a) * g, tv[0], tv[1])
        da, dg = vjp(tv[2])
        return [[da, dg], s], []
    return _rowwise(fn, [a, g, ds], [], [(2 * DFF, _ACT), (DFF, _ACT)], [], name=name)


def _loss_fwd_bwd(y, tgt, name):
    def fn(tv, fv):
        e = tv[0] - tv[1]
        row = jnp.sum(e * e, axis=1, keepdims=True)
        tot = jnp.sum(row, axis=0, keepdims=True) * (0.5 / D)
        return [e * (1.0 / D)], [jnp.broadcast_to(tot, (1, 128))]
    return _rowwise(fn, [y, tgt], [], [(D, F32)], [(1, 128)], name=name)


def _shift_down(x, k, row):
    return jnp.where(row >= k, pltpu.roll(x, k, axis=0), 0.0)


def _shift_up(x, k, row):
    n = x.shape[0]
    return jnp.where(row < n - k, pltpu.roll(x, n - k, axis=0), 0.0)


def _pool_window_masks(j):
    lane = lax.broadcasted_iota(jnp.int32, (1, 128), 1) + j * 128
    grp = lane // POOL_GDIM
    return [grp == g for g in range(4)]


def _pool_mean(u, bwd, name, col0=0):
    T = u.shape[0]
    B = T // S

    def body(u_ref, o_ref):
        j = pl.program_id(1)
        x = u_ref[...]
        row = lax.broadcasted_iota(jnp.int32, (S, 1), 0)
        masks = _pool_window_masks(j)
        inv = [1.0 / jnp.minimum(row + 1, w).astype(F32) for w in POOL_WINDOWS]
        if not bwd:
            s2 = x + _shift_down(x, 1, row)
            s4 = s2 + _shift_down(s2, 2, row)
            s8 = s4 + _shift_down(s4, 4, row)
            s16 = s8 + _shift_down(s8, 8, row)
            mean = jnp.where(masks[0], s2 * inv[0], jnp.where(masks[1], s4 * inv[1],
                             jnp.where(masks[2], s8 * inv[2], s16 * inv[3])))
            o_ref[...] = (mean - x).astype(o_ref.dtype)
        else:
            g = [jnp.where(masks[i], x * inv[i], 0.0) for i in range(4)]
            t = g[3]
            t = t + _shift_up(t, 8, row) + g[2]
            t = t + _shift_up(t, 4, row) + g[1]
            t = t + _shift_up(t, 2, row) + g[0]
            t = t + _shift_up(t, 1, row)
            o_ref[...] = (t - x).astype(o_ref.dtype)

    spec = pl.BlockSpec((S, 128), lambda b, j: (b, j))
    return _pcall(body, name=name, grid=(B, POOLW // 128),
                  in_specs=[pl.BlockSpec((S, 128), lambda b, j: (b, j + col0))], out_specs=spec,
                  out_shape=jax.ShapeDtypeStruct((T, POOLW), _ACT), compiler_params=_cp(("parallel", "parallel")))(u)


def _conv_silu(xbc, w, b, name, col0=0):
    T, C = xbc.shape[0], w.shape[1]
    B = T // S

    def body(x_ref, w_ref, b_ref, o_ref):
        x = x_ref[...]
        row = lax.broadcasted_iota(jnp.int32, (S, 1), 0)
        c = b_ref[...] + w_ref[3:4, :] * x
        for s in range(1, 4):
            c = c + w_ref[3 - s:4 - s, :] * _shift_down(x, s, row)
        o_ref[...] = _silu(c)

    return _pcall(body, name=name, grid=(B, C // 128),
                  in_specs=[pl.BlockSpec((S, 128), lambda b, j: (b, j + col0)), pl.BlockSpec((4, 128), lambda b, j: (0, j)),
                            pl.BlockSpec((1, 128), lambda b, j: (0, j))],
                  out_specs=pl.BlockSpec((S, 128), lambda b, j: (b, j)),
                  out_shape=jax.ShapeDtypeStruct((T, C), F32), compiler_params=_cp(("parallel", "parallel")))(xbc, w, b)


def _conv_silu_bwd(xbc, w, b, dact, name, col0=0):
    T, C = xbc.shape[0], w.shape[1]
    B = T // S

    def body(x_ref, w_ref, b_ref, d_ref, dx_ref, dw_ref, db_ref):
        bi = pl.program_id(1)
        x = x_ref[...]
        row = lax.broadcasted_iota(jnp.int32, (S, 1), 0)
        xs = [x] + [_shift_down(x, s, row) for s in range(1, 4)]
        c = b_ref[...]
        for s in range(4):
            c = c + w_ref[3 - s:4 - s, :] * xs[s]
        _, vjp = jax.vjp(_silu, c)
        dc = vjp(d_ref[...])[0]
        dx = w_ref[3:4, :] * dc
        for s in range(1, 4):
            dx = dx + w_ref[3 - s:4 - s, :] * _shift_up(dc, s, row)
        dx_ref[...] = dx.astype(dx_ref.dtype)
        first = bi == 0
        for s in range(4):
            _acc_rows(dw_ref, 3 - s, jnp.sum(dc * xs[s], axis=0, keepdims=True), first)
        _acc_rows(db_ref, 0, jnp.sum(dc, axis=0, keepdims=True), first)

    blk = pl.BlockSpec((S, 128), lambda j, b: (b, j))
    return _pcall(body, name=name, grid=(C // 128, B),
                  in_specs=[pl.BlockSpec((S, 128), lambda j, b: (b, j + col0)), pl.BlockSpec((4, 128), lambda j, b: (0, j)),
                            pl.BlockSpec((1, 128), lambda j, b: (0, j)), blk],
                  out_specs=[blk, pl.BlockSpec((4, 128), lambda j, b: (0, j)), pl.BlockSpec((1, 128), lambda j, b: (0, j))],
                  out_shape=[jax.ShapeDtypeStruct((T, C), _ACT), jax.ShapeDtypeStruct((4, C), F32),
                             jax.ShapeDtypeStruct((1, C), F32)],
                  compiler_params=_cp(("parallel", "arbitrary")))(xbc, w, b, dact)


def _acc_rows(ref, r, val, first):
    @pl.when(first)
    def _():
        ref[r:r + 1, :] = val

    @pl.when(jnp.logical_not(first))
    def _():
        ref[r:r + 1, :] += val


def _tri_consts():
    i = lax.broadcasted_iota(jnp.int32, (CH, CH), 0)
    j = lax.broadcasted_iota(jnp.int32, (CH, CH), 1)
    return (i == j).astype(F32), (j <= i).astype(F32), (i <= j).astype(F32), i >= j


def _ssd_chunk(h, x, dt, Bm, Cm, a, dsk, consts):
    eye, tril, triu, lower = consts
    Bb = Bm.astype(_MXU)
    Cb = Cm.astype(_MXU)
    cb = lax.dot_general(Cb, Bb, (((1,), (1,)), ((), ())), preferred_element_type=F32)
    ys, hn = [], []
    for e in range(4):
        adt = dt[e] * a[e]
        adt_row = jnp.sum(adt * eye, axis=0, keepdims=True)
        cs_col = jnp.sum(adt_row * tril, axis=1, keepdims=True)
        cs_row = jnp.sum(adt * triu, axis=0, keepdims=True)
        cs_last = jnp.sum(adt, axis=0, keepdims=True)
        decay = jnp.exp(jnp.where(lower, cs_col - cs_row, -jnp.inf))
        xb = (x[e] * dt[e]).astype(_MXU)
        y_diag = jnp.dot((cb * decay).astype(_MXU), xb, preferred_element_type=F32)
        bdec = (Bm * jnp.exp(cs_last - cs_col)).astype(_MXU)
        st = lax.dot_general(bdec, xb, (((0,), (0,)), ((), ())), preferred_element_type=F32)
        hn.append(h[e] * jnp.exp(cs_last) + st)
        y_off = jnp.exp(cs_col) * jnp.dot(Cb, h[e].astype(_MXU), preferred_element_type=F32)
        ys.append(y_diag + y_off + dsk[e] * x[e])
    return ys, hn


def _ssd_specs(order):
    def im(f):
        return lambda p, q: f(*order(p, q))
    xs = pl.BlockSpec((S, 256), im(lambda b, g: (b, g)))
    dt = pl.BlockSpec((None, S, 4), im(lambda b, g: (g, b, 0)))
    bc = pl.BlockSpec((S, 128), im(lambda b, g: (b, g)))
    hd = pl.BlockSpec((None, 1, 4), im(lambda b, g: (g, 0, 0)))
    hs = pl.BlockSpec((None, None, S // CH, 4, 128, 64), im(lambda b, g: (b, g, 0, 0, 0, 0)))
    bw = pl.BlockSpec((S, 128), im(lambda b, g: (b, 8 + g)))
    cw = pl.BlockSpec((S, 128), im(lambda b, g: (b, 12 + g)))
    return xs, dt, bc, hd, hs, bw, cw


def _ssd_fwd(act, dtg, a, dsk, name):
    xs = bm = cm = act
    T = xs.shape[0]
    B = T // S
    nc = S // CH

    def body(x_ref, dt_ref, b_ref, c_ref, a_ref, k_ref, y_ref, hs_ref, h_ref):
        consts = _tri_consts()
        h_ref[...] = jnp.zeros_like(h_ref)
        al = [a_ref[:, e:e + 1] for e in range(4)]
        kl = [k_ref[:, e:e + 1] for e in range(4)]

        def step(c, carry):
            r0 = pl.multiple_of(c * CH, CH)
            rows = pl.ds(r0, CH)
            h = [h_ref[e] for e in range(4)]
            for e in range(4):
                hs_ref[c, e] = h[e]
            x = [x_ref[rows, 64 * e:64 * e + 64] for e in range(4)]
            dt = [dt_ref[rows, e:e + 1] for e in range(4)]
            ys, hn = _ssd_chunk(h, x, dt, b_ref[rows, :], c_ref[rows, :], al, kl, consts)
            for e in range(4):
                y_ref[rows, 64 * e:64 * e + 64] = ys[e]
                h_ref[e] = hn[e]
            return carry

        lax.fori_loop(0, nc, step, 0)

    sx, sdt, sbc, shd, shs, sbw, scw = _ssd_specs(lambda b, g: (b, g))
    return _pcall(body, name=name, grid=(B, 4), in_specs=[sx, sdt, sbw, scw, shd, shd], out_specs=[sx, shs],
                  out_shape=[jax.ShapeDtypeStruct((T, 1024), F32), jax.ShapeDtypeStruct((B, 4, nc, 4, 128, 64), F32)],
                  scratch_shapes=[pltpu.VMEM((4, 128, 64), F32)],
                  compiler_params=_cp(("parallel", "parallel")))(xs, dtg, bm, cm, a, dsk)


def _lane_place(vals, width):
    lane = lax.broadcasted_iota(jnp.int32, (1, width), 1)
    out = jnp.zeros((1, width), F32)
    for e, v in enumerate(vals):
        out = out + jnp.where(lane == e, v, 0.0)
    return out


def _ssd_bwd(act, dtg, a, dsk, hs, dy, name):
    xs = bm = cm = act
    T = xs.shape[0]
    B = T // S
    nc = S // CH

    def body(x_ref, dt_ref, b_ref, c_ref, a_ref, k_ref, hs_ref, dy_ref,
             dx_ref, ddt_ref, db_ref, dc_ref, dak_ref, dh_ref, sc_ref):
        bi = pl.program_id(1)
        consts = _tri_consts()
        dh_ref[...] = jnp.zeros_like(dh_ref)
        sc_ref[...] = jnp.zeros_like(sc_ref)
        al = [a_ref[:, e:e + 1] for e in range(4)]
        kl = [k_ref[:, e:e + 1] for e in range(4)]

        def step(i, carry):
            c = nc - 1 - i
            r0 = pl.multiple_of(c * CH, CH)
            rows = pl.ds(r0, CH)
            h = [hs_ref[c, e] for e in range(4)]
            x = [x_ref[rows, 64 * e:64 * e + 64] for e in range(4)]
            dt = [dt_ref[rows, e:e + 1] for e in range(4)]
            f = functools.partial(_ssd_chunk, consts=consts)
            _, vjp = jax.vjp(f, h, x, dt, b_ref[rows, :], c_ref[rows, :], al, kl)
            dys = [dy_ref[rows, 64 * e:64 * e + 64] for e in range(4)]
            dhn = [dh_ref[e] for e in range(4)]
            dh, dx, ddt, dB, dC, da, dk = vjp((dys, dhn))
            for e in range(4):
                dh_ref[e] = dh[e]
                dx_ref[rows, 64 * e:64 * e + 64] = dx[e]
                ddt_ref[rows, e:e + 1] = ddt[e]
            db_ref[rows, :] = dB
            dc_ref[rows, :] = dC
            sc_ref[0:1, :] += _lane_place(da, 128)
            sc_ref[1:2, :] += _lane_place(dk, 128)
            return carry

        lax.fori_loop(0, nc, step, 0)
        first = bi == 0

        @pl.when(first)
        def _():
            dak_ref[...] = sc_ref[...]

        @pl.when(jnp.logical_not(first))
        def _():
            dak_ref[...] += sc_ref[...]

    sx, sdt, sbc, shd, shs, sbw, scw = _ssd_specs(lambda g, b: (b, g))
    return _pcall(body, name=name, grid=(4, B), in_specs=[sx, sdt, sbw, scw, shd, shd, shs, sx],
                  out_specs=[sx, sdt, sbc, sbc, pl.BlockSpec((None, 8, 128), lambda g, b: (g, 0, 0))],
                  out_shape=[jax.ShapeDtypeStruct((T, 1024), F32), jax.ShapeDtypeStruct((4, T, 4), F32),
                             jax.ShapeDtypeStruct((T, 512), F32), jax.ShapeDtypeStruct((T, 512), F32),
                             jax.ShapeDtypeStruct((4, 8, 128), F32)],
                  scratch_shapes=[pltpu.VMEM((4, 128, 64), F32), pltpu.VMEM((8, 128), F32)],
                  compiler_params=_cp(("parallel", "arbitrary")))(xs, dtg, bm, cm, a, dsk, hs, dy)


def _gate_norm(y, z, nw):
    t = y * _silu(z)
    return t * lax.rsqrt(jnp.mean(t * t, axis=-1, keepdims=True) + SSD_EPS) * nw


def _ssd_gate_norm(y, z, nw, name, zcol=0):
    def fn(tv, fv):
        return [[_gate_norm(tv[g], tv[4 + g], fv[0][:, 256 * g:256 * g + 256]) for g in range(4)]], []
    tiled = [(y, 256, g) for g in range(4)] + [(z, 256, zcol + g) for g in range(4)]
    return _rowwise(fn, tiled, [nw], [(1024, _ACT)], [], name=name)[0]


def _ssd_gate_norm_bwd(y, z, nw, dout, name, zcol=0):
    def fn(tv, fv):
        dys, dzs, dns = [], [], []
        for g in range(4):
            _, vjp = jax.vjp(_gate_norm, tv[g], tv[4 + g], fv[0][:, 256 * g:256 * g + 256])
            a, b, c = vjp(tv[8 + g])
            dys.append(a)
            dzs.append(b)
            dns.append(c)
        return [dys, dzs], [dns]
    tiled = [(y, 256, g) for g in range(4)] + [(z, 256, zcol + g) for g in range(4)] + [(dout, 256, g) for g in range(4)]
    return _rowwise(fn, tiled, [nw], [(1024, F32), (1024, _ACT)], [(1, 1024)], name=name)


def _t5_bucket_np(dist):
    dist = np.maximum(dist, 0)
    max_exact = 16
    large = max_exact + (np.log(np.maximum(dist, 1) / max_exact) / np.log(2048 / max_exact) * (32 - max_exact)).astype(np.int32)
    large = np.minimum(large, 31)
    return np.where(dist < max_exact, dist, large).astype(np.int32)


def _bucket_maps():
    qi = np.arange(128)[:, None]
    kj = np.arange(256)[None, :]
    return np.stack([_t5_bucket_np((qi - kj + 128) * dil) for dil in ATTN_DILS]).astype(np.int32)


def _bias_build(rel_bias, maps, name):
    def body(tab_ref, map_ref, o_ref):
        hh = pl.program_id(0)
        m = map_ref[...]
        acc = jnp.zeros((128, 256), F32)
        for b in range(32):
            acc = jnp.where(m == b, tab_ref[b, hh], acc)
        o_ref[...] = acc

    return _pcall(body, name=name, grid=(12,),
                  in_specs=[pl.BlockSpec(memory_space=pltpu.SMEM), pl.BlockSpec((None, 128, 256), lambda h: (h // 4, 0, 0))],
                  out_specs=pl.BlockSpec((None, 128, 256), lambda h: (h, 0, 0)),
                  out_shape=jax.ShapeDtypeStruct((12, 128, 256), F32), compiler_params=_cp(("parallel",)))(rel_bias, maps)


def _bias_reduce(dbias, maps, name):
    nl = dbias.shape[0]

    def body(d_ref, map_ref, o_ref):
        m = map_ref[...]
        d = d_ref[0]
        for i in range(1, nl):
            d = d + d_ref[i]
        lane = lax.broadcasted_iota(jnp.int32, (1, 128), 1)
        out = jnp.zeros((1, 128), F32)
        for b in range(32):
            s = jnp.sum(jnp.sum(jnp.where(m == b, d, 0.0), axis=1, keepdims=True), axis=0, keepdims=True)
            out = out + jnp.where(lane == b, s, 0.0)
        o_ref[...] = out

    return _pcall(body, name=name, grid=(12,),
                  in_specs=[pl.BlockSpec((nl, None, 128, 256), lambda h: (0, h, 0, 0)),
                            pl.BlockSpec((None, 128, 256), lambda h: (h // 4, 0, 0))],
                  out_specs=pl.BlockSpec((None, 1, 128), lambda h: (h, 0, 0)),
                  out_shape=jax.ShapeDtypeStruct((12, 1, 128), F32), compiler_params=_cp(("parallel",)))(dbias, maps)


def _attn_block(q, kb, vb, bias, mask):
    s = lax.dot_general(q.astype(_MXU), kb.astype(_MXU), (((1,), (1,)), ((), ())), preferred_element_type=F32) * 0.125 + bias
    s = jnp.where(mask, s, -jnp.inf)
    m = lax.stop_gradient(jnp.max(s, axis=-1, keepdims=True))
    p = jnp.exp(s - m)
    den = jnp.sum(p, axis=-1, keepdims=True)
    out = jnp.dot((p / den).astype(_MXU), vb.astype(_MXU), preferred_element_type=F32)
    return out, m + jnp.log(den)


ATTN_QB = 512


def _attn_masks(dil):
    qi = lax.broadcasted_iota(jnp.int32, (ATTN_QB, ATTN_QB + 128), 0)
    kj = lax.broadcasted_iota(jnp.int32, (ATTN_QB, ATTN_QB + 128), 1)
    band = (kj >= qi) & (kj <= qi + 128)
    if dil == 16:
        q2 = lax.broadcasted_iota(jnp.int32, (ATTN_QB, ATTN_QB), 0)
        k2 = lax.broadcasted_iota(jnp.int32, (ATTN_QB, ATTN_QB), 1)
        return ((q2 // 128) == (k2 // 128)) & (k2 <= q2), None
    return band[:, 128:], band


def _attn_wide_bias(b, dil):
    if dil == 16:
        return jnp.tile(b[:, 128:], (4, 4)), None
    z = jnp.zeros((128, 128), F32)
    band = jnp.concatenate([jnp.concatenate([z] * i + [b] + [z] * (3 - i), axis=1) for i in range(4)], axis=0)
    return band[:, 128:], band


def _fold_dbias(dbs, dil, band_form):
    def blk(i, j):
        return dbs[128 * i:128 * i + 128, 128 * j:128 * j + 128]
    if band_form:
        return sum(blk(i, i) for i in range(4)), sum(blk(i, i + 1) for i in range(4))
    cur = sum(blk(i, i) for i in range(4))
    if dil == 16:
        return None, cur
    return sum(blk(i, i - 1) for i in range(1, 4)), cur


def _attn_chunks(dil):
    out = []
    for n in range(S // ATTN_QB):
        if dil == 1 and n > 0:
            out.append((n * ATTN_QB, n * ATTN_QB - 128, ATTN_QB + 128, True))
        else:
            out.append((n * ATTN_QB, n * ATTN_QB, ATTN_QB, False))
    return out


def _qkv_specs(gi, order):
    def spec(base):
        col = (base + 256 * gi) // 128
        return pl.BlockSpec((S, 128), lambda p, q: (order(p, q)[0], col + order(p, q)[1]))
    return [spec(O_Q), spec(O_K), spec(O_V)]


def _residue_rows(r, dil):
    return pl.ds(r, S // dil, stride=dil)


def _attn_fwd(hcat, bias_all, gi, name):
    dil = ATTN_DILS[gi]
    T = hcat.shape[0]
    B, L = T // S, S // dil

    def body(q_ref, k_ref, v_ref, b_ref, o_ref, l_ref, *scr):
        mask_first, mask_band = _attn_masks(dil)
        if dil > 1:
            qs, ks, vs, os_, ls = scr
            for r in range(dil):
                rows, dst = _residue_rows(r, dil), pl.ds(r * L, L)
                qs[dst, :] = q_ref[rows, :]
                ks[dst, :] = k_ref[rows, :]
                vs[dst, :] = v_ref[rows, :]
        else:
            qs, ks, vs, os_, ls = q_ref, k_ref, v_ref, o_ref, l_ref
        ls[...] = jnp.zeros_like(ls)
        for e in range(2):
            lanes = slice(64 * e, 64 * e + 64)
            bias_first, bias_band = _attn_wide_bias(b_ref[e], dil)
            for q0, k0, kn, band_form in _attn_chunks(dil):
                cur, keys = pl.ds(q0, ATTN_QB), pl.ds(k0, kn)
                o, l = _attn_block(qs[cur, lanes], ks[keys, lanes], vs[keys, lanes],
                                   bias_band if band_form else bias_first, mask_band if band_form else mask_first)
                os_[cur, lanes] = o
                ls[cur, e:e + 1] = l
        if dil > 1:
            for r in range(dil):
                rows, src = _residue_rows(r, dil), pl.ds(r * L, L)
                o_ref[rows, :] = os_[src, :]
                l_ref[rows, :] = ls[src, :]

    scratch = [pltpu.VMEM((S, 128), F32)] * 5 if dil > 1 else []
    return _pcall(body, name=name, grid=(B, 2),
                  in_specs=_qkv_specs(gi, lambda b, hp: (b, hp))
                  + [pl.BlockSpec((2, 128, 256), lambda b, hp: (2 * gi + hp, 0, 0))],
                  out_specs=[pl.BlockSpec((S, 128), lambda b, hp: (b, hp)),
                             pl.BlockSpec((None, S, 128), lambda b, hp: (hp, b, 0))],
                  out_shape=[jax.ShapeDtypeStruct((T, 256), F32), jax.ShapeDtypeStruct((2, T, 128), F32)],
                  scratch_shapes=scratch,
                  compiler_params=_cp(("parallel", "parallel")))(hcat, hcat, hcat, bias_all)


def _attn_bwd(hcat, bias_all, gi, do, dl, name):
    dil = ATTN_DILS[gi]
    T = hcat.shape[0]
    B, L = T // S, S // dil

    def body(q_ref, k_ref, v_ref, b_ref, do_ref, dl_ref, dq_ref, dk_ref, dv_ref, db_ref, acc_ref, *scr):
        bi = pl.program_id(1)
        mask_first, mask_band = _attn_masks(dil)
        if dil > 1:
            qs, ks, vs, dos, dls, dqs, dks, dvs = scr
            for r in range(dil):
                rows, dst = _residue_rows(r, dil), pl.ds(r * L, L)
                qs[dst, :] = q_ref[rows, :]
                ks[dst, :] = k_ref[rows, :]
                vs[dst, :] = v_ref[rows, :]
                dos[dst, :] = do_ref[rows, :]
                dls[dst, :] = dl_ref[rows, :]
        else:
            qs, ks, vs, dos, dls, dqs, dks, dvs = q_ref, k_ref, v_ref, do_ref, dl_ref, dq_ref, dk_ref, dv_ref
        dks[...] = jnp.zeros_like(dks)
        dvs[...] = jnp.zeros_like(dvs)
        for e in range(2):
            lanes = slice(64 * e, 64 * e + 64)
            bias_first, bias_band = _attn_wide_bias(b_ref[e], dil)
            acc_ref[...] = jnp.zeros_like(acc_ref)
            for q0, k0, kn, band_form in _attn_chunks(dil):
                cur, keys = pl.ds(q0, ATTN_QB), pl.ds(k0, kn)
                f = functools.partial(_attn_block, mask=mask_band if band_form else mask_first)
                _, vjp = jax.vjp(f, qs[cur, lanes], ks[keys, lanes], vs[keys, lanes],
                                 bias_band if band_form else bias_first)
                dq, dkb, dvb, dbs = vjp((dos[cur, lanes], dls[cur, e:e + 1]))
                dqs[cur, lanes] = dq
                dks[keys, lanes] += dkb
                dvs[keys, lanes] += dvb
                prev, here = _fold_dbias(dbs, dil, band_form)
                if prev is not None:
                    acc_ref[:, 0:128] += prev
                acc_ref[:, 128:256] += here

            @pl.when(bi == 0)
            def _(e=e):
                db_ref[e] = acc_ref[...]

            @pl.when(bi > 0)
            def _(e=e):
                db_ref[e] += acc_ref[...]

        if dil > 1:
            for r in range(dil):
                rows, src = _residue_rows(r, dil), pl.ds(r * L, L)
                dq_ref[rows, :] = dqs[src, :]
                dk_ref[rows, :] = dks[src, :]
                dv_ref[rows, :] = dvs[src, :]

    order = lambda hp, b: (b, hp)
    blk = pl.BlockSpec((S, 128), lambda hp, b: (b, hp))
    lblk = pl.BlockSpec((None, S, 128), lambda hp, b: (hp, b, 0))
    sds = jax.ShapeDtypeStruct((T, 256), F32)
    scratch = [pltpu.VMEM((128, 256), F32)] + ([pltpu.VMEM((S, 128), F32)] * 8 if dil > 1 else [])
    return _pcall(body, name=name, grid=(2, B),
                  in_specs=_qkv_specs(gi, order) + [pl.BlockSpec((2, 128, 256), lambda hp, b: (2 * gi + hp, 0, 0)), blk, lblk],
                  out_specs=[blk, blk, blk, pl.BlockSpec((2, 128, 256), lambda hp, b: (hp, 0, 0))],
                  out_shape=[sds, sds, sds, jax.ShapeDtypeStruct((4, 128, 256), F32)],
                  scratch_shapes=scratch,
                  compiler_params=_cp(("parallel", "arbitrary")))(hcat, hcat, hcat, bias_all, do, dl)


def _lse_merge(o0, o1, o2, l0, l1, l2):
    m = lax.stop_gradient(jnp.maximum(jnp.maximum(l0, l1), l2))
    e0, e1, e2 = jnp.exp(l0 - m), jnp.exp(l1 - m), jnp.exp(l2 - m)
    den = e0 + e1 + e2
    return (e0 / den) * o0 + (e1 / den) * o1 + (e2 / den) * o2


def _attn_merge(outs, lses, dy, name):
    T = outs[0].shape[0]
    bwd = dy is not None
    tm = 512

    def body(*refs):
        o_refs, l_refs = refs[:3], refs[3:6]
        if bwd:
            for r in refs[10:13]:
                r[...] = jnp.zeros_like(r)
        for e in range(2):
            lanes = slice(64 * e, 64 * e + 64)
            vals = [r[:, lanes] for r in o_refs] + [r[:, e:e + 1] for r in l_refs]
            if not bwd:
                refs[6][:, lanes] = _lse_merge(*vals).astype(refs[6].dtype)
            else:
                _, vjp = jax.vjp(_lse_merge, *vals)
                g = vjp(refs[6][:, lanes])
                for r, v in zip(refs[7:10], g[:3]):
                    r[:, lanes] = v
                for r, v in zip(refs[10:13], g[3:]):
                    r[:, e:e + 1] = v

    blk = pl.BlockSpec((tm, 128), lambda i, hp: (i, hp))
    lblk = pl.BlockSpec((None, tm, 128), lambda i, hp: (hp, i, 0))
    lsd = jax.ShapeDtypeStruct((2, T, 128), F32)
    if not bwd:
        return _pcall(body, name=name, grid=(T // tm, 2), in_specs=[blk] * 3 + [lblk] * 3, out_specs=blk,
                      out_shape=jax.ShapeDtypeStruct((T, 256), F32),
                      compiler_params=_cp(("parallel", "parallel")))(*outs, *lses)
    return _pcall(body, name=name, grid=(T // tm, 2), in_specs=[blk] * 3 + [lblk] * 3 + [blk],
                  out_specs=[blk] * 3 + [lblk] * 3, out_shape=[jax.ShapeDtypeStruct((T, 256), F32)] * 3 + [lsd] * 3,
                  compiler_params=_cp(("parallel", "parallel")))(*outs, *lses, dy)


def _gmerge(g0, g1, g2, gb, ya, yb, yc):
    return (jax.nn.sigmoid(g0 + gb[:, 0:D]) * ya + jax.nn.sigmoid(g1 + gb[:, D:2 * D]) * yb
            + jax.nn.sigmoid(g2 + gb[:, 2 * D:3 * D]) * yc)


def _gated_merge(gates, gb, ya, yb, yc, name, gcol=0):
    def fn(tv, fv):
        return [_gmerge(tv[0], tv[1], tv[2], fv[0], tv[3], tv[4], tv[5])], []
    return _rowwise(fn, [(gates, D, gcol), (gates, D, gcol + 1), (gates, D, gcol + 2), ya, yb, yc], [gb], [(D, _ACT)], [],
                    name=name)[0]


def _gated_merge_bwd(gates, gb, ya, yb, yc, dm, name, gcol=0):
    def fn(tv, fv):
        _, vjp = jax.vjp(_gmerge, tv[0], tv[1], tv[2], fv[0], tv[3], tv[4], tv[5])
        d0, d1, d2, dgb, da, db, dc = vjp(tv[6])
        return [[d0, d1, d2], da, db, dc], [dgb]
    return _rowwise(fn, [(gates, D, gcol), (gates, D, gcol + 1), (gates, D, gcol + 2), ya, yb, yc, dm], [gb],
                    [(3 * D, _ACT), (D, _ACT), (D, _ACT), (D, _ACT)], [(1, 3 * D)], name=name)


def _pool_affine(t1, pb, ps, dout, name):
    if dout is None:
        def fn(tv, fv):
            return [(tv[0] + fv[0]) * fv[1]], []
        return _rowwise(fn, [t1], [pb, ps], [(POOLW, _ACT)], [], name=name)[0]

    def fnb(tv, fv):
        t2, vjp = jax.vjp(lambda t, b, s: (t + b) * s, tv[0], fv[0], fv[1])
        dt, db, dsc = vjp(tv[1])
        return [dt, t2], [db, dsc]
    return _rowwise(fnb, [t1, dout], [pb, ps], [(POOLW, _ACT), (POOLW, _ACT)], [(1, POOLW), (1, POOLW)], name=name)


def _dt_softplus(dt_raw, dt_bias, ddt, name):
    f = lambda r, b: _softplus(r + b)
    if ddt is None:
        def fn(tv, fv):
            return [f(tv[0], fv[0])], []
        return _rowwise(fn, [dt_raw], [dt_bias], [(16, F32)], [], name=name, tm=1024)[0]

    def fnb(tv, fv):
        _, vjp = jax.vjp(f, tv[0], fv[0])
        dr, db = vjp(tv[1])
        return [dr], [db]
    return _rowwise(fnb, [dt_raw, ddt], [dt_bias], [(16, F32)], [(1, 16)], name=name, tm=1024)


def _adamw_math(wv, gv, mv, vv):
    c1 = 1.0 / (1.0 - ADAM_B1 ** ADAM_STEP)
    c2 = 1.0 / (1.0 - ADAM_B2 ** ADAM_STEP)
    mn = ADAM_B1 * mv + (1.0 - ADAM_B1) * gv
    vn = ADAM_B2 * vv + (1.0 - ADAM_B2) * (gv * gv)
    delta = -ADAM_LR * ((mn * c1) / (jnp.sqrt(vn * c2) + ADAM_EPS) + ADAM_WD * wv)
    return delta, mn, vn


def _adamw(w, g, m, v, name):
    R, C = w.shape
    tm = _pick(R, (256, 128, 64, 32, 16, 8))
    return _rowwise(lambda tv, fv: (list(_adamw_math(*tv)), []), [w, g, m, v], [], [(C, F32)] * 3, [], name=name, tm=tm)


def _adamw_layer(i, w, g, m, v, accs, name, dep=None):
    R, C = w.shape
    r = R // NL
    tm = _pick(r, (256, 128, 64, 32, 16, 8))
    nt = r // tm
    if accs is None:
        accs = [lax.empty((R, C), F32) for _ in range(4)]
    extra = [] if dep is None else [dep]

    def body(w_ref, g_ref, m_ref, v_ref, *rest):
        go_ref, do_ref, mo_ref, vo_ref = rest[-4:]
        gv = g_ref[...]
        delta, mn, vn = _adamw_math(w_ref[...], gv, m_ref[...], v_ref[...])
        go_ref[...] = gv
        do_ref[...] = delta
        mo_ref[...] = mn
        vo_ref[...] = vn

    slab = pl.BlockSpec((tm, C), lambda t: (i * nt + t, 0))
    anyspec = pl.BlockSpec(memory_space=pl.ANY)
    return _pcall(body, name=name, grid=(nt,),
                  in_specs=[slab, pl.BlockSpec((tm, C), lambda t: (t, 0)), slab, slab] + [anyspec] * (4 + len(extra)),
                  out_specs=[slab] * 4, out_shape=[jax.ShapeDtypeStruct((R, C), F32)] * 4,
                  input_output_aliases={4 + k: k for k in range(4)},
                  compiler_params=_cp(("parallel",)))(w, g, m, v, *accs, *extra)


def _ffn_fwd(x, xm, w13, w2, g, b, tag, dep=None):
    ha, hg, s = _swiglu_fwd(xm, w13, dep, name=f"{tag}_h")
    r, out, outm = _mm_res_ln(s, w2, x, g, b, 0.5, name=f"{tag}_y")
    return out, outm, dict(x=xm, ha=ha, hg=hg, r=r)


def _ffn_bwd(dout, sv, w13, w2, g, b, tag, dep=None):
    dskip, dy, dg, db = _ln_bwd(sv['r'], g, b, dout, 0.5, name=f"{tag}_lnb")
    ds = _mm(dy, w2, tb=True, dep=dep, name=f"{tag}_ds")
    dh, s = _swiglu_act_bwd(sv['ha'], sv['hg'], ds, name=f"{tag}_actb")
    dw2 = _mm(s, dy, ta=True, name=f"{tag}_dw2")
    dw13 = _mm(sv['x'], dh, ta=True, name=f"{tag}_dw13")
    dx = _mm(dh, w13, tb=True, add=dskip, name=f"{tag}_dx")
    return dx, dict(w13=dw13, w2=dw2, g=dg, b=db)


def _mixer_fwd(x1, x1m, W, bias_all, tag, dep=None):
    T = x1.shape[0]
    hcat = _mm(x1m, W['w_in_r'], dep=dep, name=f"{tag}_hcat")
    dt_raw = hcat[:, O_DT:O_DT + 16]
    pooled = _pool_mean(hcat, False, name=f"{tag}_pool", col0=O_U // 128)
    t1 = _mm(pooled, W['pool_wbd'], name=f"{tag}_pt1")
    t2 = _pool_affine(t1, W['pool_b'], W['pool_scale'], None, name=f"{tag}_paff")
    ya = _mm(t2, W['p_pool'], name=f"{tag}_ya")
    act = _conv_silu(hcat, W['conv_w'], W['conv_b'], name=f"{tag}_conv", col0=O_XBC // 128)
    dt = _dt_softplus(dt_raw, W['dt_bias'], None, name=f"{tag}_dt")
    dtg = dt.reshape(T, 4, 4).transpose(1, 0, 2)
    yscan, hs = _ssd_fwd(act, dtg, W['a_neg'], W['d_skip'], name=f"{tag}_ssd")
    ybn = _ssd_gate_norm(yscan, hcat, W['ssd_norm'], name=f"{tag}_gn", zcol=O_Z // 256)
    yb = _mm(ybn, W['p_ssd'], name=f"{tag}_yb")
    outs, lses = [], []
    for gi in range(len(ATTN_DILS)):
        o, l = _attn_fwd(hcat, bias_all, gi, name=f"{tag}_attn{gi}")
        outs.append(o)
        lses.append(l)
    ycp = _attn_merge(outs, lses, None, name=f"{tag}_amerge")
    yc = _mm(ycp, W['p_attn'], name=f"{tag}_yc")
    merged = _gated_merge(hcat, W['gate_b'], ya, yb, yc, name=f"{tag}_gm", gcol=O_G // D)
    r, out, outm = _mm_res_ln(merged, W['w_out'], x1, W['ln2_g'], W['ln2_b'], 1.0, name=f"{tag}_mix")
    sv = dict(x1=x1m, dt_raw=dt_raw, pooled=pooled, t1=t1, act=act, dtg=dtg,
              hs=hs, yscan=yscan, ybn=ybn, hcat=hcat, outs=outs, lses=lses, ycp=ycp, ya=ya, yb=yb, yc=yc,
              merged=merged, r=r)
    return out, outm, sv


def _mixer_bwd(dout, sv, W, bias_all, tag, dep=None):
    T = dout.shape[0]
    gr = {}
    dx1a, dr, gr['ln2_g'], gr['ln2_b'] = _ln_bwd(sv['r'], W['ln2_g'], W['ln2_b'], dout, 1.0, name=f"{tag}_lnb")
    dmerged = _mm(dr, W['w_out'], tb=True, dep=dep, name=f"{tag}_dmerged")
    gr['w_out'] = _mm(sv['merged'], dr, ta=True, name=f"{tag}_dwout")
    dgates, dya, dyb, dyc, gr['gate_b'] = _gated_merge_bwd(sv['hcat'], W['gate_b'], sv['ya'], sv['yb'], sv['yc'],
                                                           dmerged, name=f"{tag}_gmb", gcol=O_G // D)
    dycp = _mm(dyc, W['p_attn'], tb=True, name=f"{tag}_dycp")
    gr['p_attn'] = _mm(sv['ycp'], dyc, ta=True, name=f"{tag}_dpattn")
    dml = _attn_merge(sv['outs'], sv['lses'], dycp, name=f"{tag}_amergeb")
    dq, dk, dv, dbias = [], [], [], []
    for gi in range(len(ATTN_DILS)):
        a, b, c, d = _attn_bwd(sv['hcat'], bias_all, gi, dml[gi], dml[3 + gi], name=f"{tag}_attnb{gi}")
        dq.append(a)
        dk.append(b)
        dv.append(c)
        dbias.append(d)
    dbias = jnp.concatenate(dbias, axis=0)
    dybn = _mm(dyb, W['p_ssd'], tb=True, name=f"{tag}_dybn")
    gr['p_ssd'] = _mm(sv['ybn'], dyb, ta=True, name=f"{tag}_dpssd")
    dyscan, dz, gr['ssd_norm'] = _ssd_gate_norm_bwd(sv['yscan'], sv['hcat'], W['ssd_norm'], dybn, name=f"{tag}_gnb",
                                                    zcol=O_Z // 256)
    dxs, ddtg, dbm, dcm, dak = _ssd_bwd(sv['act'], sv['dtg'], W['a_neg'], W['d_skip'], sv['hs'], dyscan,
                                        name=f"{tag}_ssdb")
    gr['a_neg'], gr['d_skip'] = dak[:, 0, 0:4], dak[:, 1, 0:4]
    ddt = ddtg.transpose(1, 0, 2).reshape(T, 16)
    ddt_raw, gr['dt_bias'] = _dt_softplus(sv['dt_raw'], W['dt_bias'], ddt, name=f"{tag}_dtb")
    dact = jnp.concatenate([dxs, dbm, dcm], axis=1)
    dxbc, gr['conv_w'], gr['conv_b'] = _conv_silu_bwd(sv['hcat'], W['conv_w'], W['conv_b'], dact, name=f"{tag}_convb",
                                                      col0=O_XBC // 128)
    dt2 = _mm(dya, W['p_pool'], tb=True, name=f"{tag}_dt2")
    dt1, t2, gr['pool_b'], gr['pool_scale'] = _pool_affine(sv['t1'], W['pool_b'], W['pool_scale'], dt2, name=f"{tag}_paffb")
    gr['p_pool'] = _mm(t2, dya, ta=True, name=f"{tag}_dppool")
    dpooled = _mm(dt1, W['pool_wbd'], tb=True, name=f"{tag}_dpooled")
    gr['pool_wbd'] = _mm(sv['pooled'], dt1, ta=True, name=f"{tag}_dpoolw")
    du = _pool_mean(dpooled, True, name=f"{tag}_poolb")
    dhcat = jnp.concatenate([t.astype(_ACT) for t in [du, dz, dxbc] + dq + dk + dv + [dgates, ddt_raw]]
                            + [jnp.zeros((T, HC - O_DT - 16), _ACT)], axis=1)
    dx1 = _mm(dhcat, W['w_in_r'], tb=True, add=dx1a, name=f"{tag}_dx1")
    gr['w_in_r'] = _mm(sv['x1'], dhcat, ta=True, name=f"{tag}_dwin")
    return dx1, gr, dbias


def _prep_layer_weights(i, inp, G):
    W = {}
    for n in BIG:
        if n not in G:
            continue
        g = G[n]
        if n == 'w_in':
            W['w_in_r'] = jnp.concatenate(_nat_pieces(g, 0, 3840) + _nat_pieces(g, 3856, 9232) + _nat_pieces(g, 3840, 3856)
                                          + [jnp.zeros((D, HC - 9232), g.dtype)], axis=1)
        elif n in COL_SHARDED:
            W[n] = jnp.concatenate([g[j] for j in range(4)], axis=1)
        else:
            W[n] = g.reshape(4 * g.shape[1], g.shape[2])
    pw = inp['pool_w'][i].astype(_MXU)
    wbd = jnp.zeros((POOLW, POOLW), _MXU)
    for g in range(4):
        wbd = lax.dynamic_update_slice(wbd, pw[g], (g * POOL_GDIM, g * POOL_GDIM))
    W['pool_wbd'] = wbd
    W['pool_b'] = inp['pool_b'][i].reshape(1, POOLW)
    W['pool_scale'] = inp['pool_scale'][i].reshape(1, POOLW)
    if 'conv_w' in G:
        W['conv_w'] = jnp.concatenate([G['conv_w'][j] for j in range(4)], axis=1)
        W['gate_b'] = jnp.concatenate([G['gate_b'][j][b:b + 1] for b in range(3) for j in range(4)], axis=1)
    W['conv_b'] = inp['conv_b'][i].reshape(1, 2048)
    W['dt_bias'] = inp['dt_bias'][i].reshape(1, 16)
    W['a_neg'] = (-jnp.exp(inp['a_log'][i])).reshape(4, 1, 4)
    W['d_skip'] = inp['d_skip'][i].reshape(4, 1, 4)
    W['ssd_norm'] = inp['ssd_norm'][i].reshape(1, D)
    for n in ('ln1_g', 'ln1_b', 'ln2_g', 'ln2_b', 'ln3_g', 'ln3_b'):
        W[n] = inp[n][i].reshape(1, D)
    return W


GATHER_FIRST = ['ffn1_w13', 'ffn1_w2']
GATHER_REST = [n for n in BIG if n not in GATHER_FIRST] + ['gate_b', 'conv_w']


def _gather_start(inp, i, names):
    core = lax.axis_index("c")
    arrs = []
    for n in names:
        s = inp[n][i]
        if n in BIG:
            s = lax.dynamic_slice_in_dim(s, core * (s.shape[0] // 2), s.shape[0] // 2, axis=0).astype(BF16)
        arrs.append(s)
    state, token = _exchange_start(arrs, "chips", "gather", name="gather_start")
    return (names, state), token


def _gather_mid(handle, after):
    names, state = handle
    me = 2 * lax.axis_index("x") + lax.axis_index("y")
    own, outs = _exchange_wait(state, after, "chips", "gather", name="gather_wait")
    outs = [lax.dynamic_update_slice(o, a[None], (me, 0, 0)) for o, a in zip(outs, own)]
    big = [o for n, o in zip(names, outs) if n in BIG]
    state, token = _exchange_start(big, "cores", "gather", name="share_start")
    return (names, outs, state), token


def _gather_finish(handle, after):
    names, outs, state = handle
    core = lax.axis_index("c")
    mine, theirs = _exchange_wait(state, after, "cores", "gather", name="share_wait")
    G = {n: o for n, o in zip(names, outs) if n not in BIG}
    for n, a, b in zip([n for n in names if n in BIG], mine, theirs):
        G[n] = jnp.concatenate([jnp.where(core == 0, a, b), jnp.where(core == 0, b, a)], axis=1)
    return G


W_IN_SHARD = 2308


def _nat_pieces(g, lo, hi):
    out = []
    for j in range(4):
        s, e = max(lo, W_IN_SHARD * j), min(hi, W_IN_SHARD * (j + 1))
        if s < e:
            out.append(g[j][:, s - W_IN_SHARD * j:e - W_IN_SHARD * j])
    return out


def _reord_ranges(lo, hi):
    out = []
    for a, b, off in ((0, 3840, 0), (3840, 3856, O_DT - 3840), (3856, 9232, -16)):
        s, e = max(lo, a), min(hi, b)
        if s < e:
            out.append((s + off, e + off))
    return out


def _halves_of(n, g):
    if n == 'w_in':
        shards = [jnp.concatenate([g[:, a:b] for a, b in _reord_ranges(W_IN_SHARD * j, W_IN_SHARD * (j + 1))], axis=1)
                  for j in range(4)]
    elif n in COL_SHARDED:
        c = g.shape[1] // 4
        shards = [g[:, j * c:(j + 1) * c] for j in range(4)]
    else:
        r = g.shape[0] // 4
        shards = [g[j * r:(j + 1) * r] for j in range(4)]
    r2 = shards[0].shape[0] // 2
    return jnp.stack([jnp.concatenate([s[h * r2:(h + 1) * r2] for s in shards], axis=0) for h in range(2)])


def _reduce_a(grads):
    names = list(grads)
    halves = [_halves_of(n, grads[n]) for n in names]
    state, token = _exchange_start(halves, "cores", "scatter", name="rsc_start")
    return (names, state), token


def _reduce_b(handle, after):
    names, state = handle
    core = lax.axis_index("c").reshape(1)
    halves, got = _exchange_wait(state, after, "cores", "scatter", name="rsc_wait")
    chip = [_sum_own_recv(h, t, core, BF16, name="rs_sum2") for h, t in zip(halves, got)]
    chip = [t.reshape(4, t.shape[0] // 4, t.shape[1]) for t in chip]
    state, token = _exchange_start(chip, "chips", "scatter", name="rs_start")
    return (names, state), token


def _reduce_c(handle, after):
    names, state = handle
    chip_id = (2 * lax.axis_index("x") + lax.axis_index("y")).reshape(1)
    chip, got = _exchange_wait(state, after, "chips", "scatter", name="rs_wait")
    red = [_sum_own_recv(h, t, chip_id, F32, name="rs_sum4") for h, t in zip(chip, got)]
    other = _exchange(red, "cores", "gather", name="rs_share")
    out = {}
    for n, mine, theirs in zip(names, red, other):
        out[n] = jnp.where(lax.axis_index("c") == 0, jnp.concatenate([mine, theirs]), jnp.concatenate([theirs, mine]))
    return out


class _Comm:
    def __init__(self, inp):
        self.inp = inp

    def gather_start(self, i, names):
        return _gather_start(self.inp, i, names)

    gather_mid = staticmethod(_gather_mid)
    gather_finish = staticmethod(_gather_finish)

    def reduce_a(self, i, grads):
        return _reduce_a({n: grads[n] for n in BIG})

    reduce_b = staticmethod(_reduce_b)
    reduce_c = staticmethod(_reduce_c)


def _allreduce_small(vec, dep=None):
    for group in ("cores", "x", "y"):
        recv = _exchange([vec], group, "gather", name=f"ar_{group}", dep=dep if group == "cores" else None)[0]
        vec = _rowwise(lambda tv, fv: ([tv[0] + tv[1]], []), [vec, recv], [], [(128, F32)], [], name=f"ar_add_{group}")[0]
    return vec


def _pack(arrs):
    flat = jnp.concatenate([a.reshape(-1) for a in arrs])
    n = flat.shape[0]
    pad = (-n) % (256 * 128)
    flat = jnp.concatenate([flat, jnp.zeros((pad,), F32)])
    return flat.reshape(-1, 128)


def _unpack(p, shapes):
    flat = p.reshape(-1)
    out, off = [], 0
    for s in shapes:
        sz = int(np.prod(s))
        out.append(flat[off:off + sz].reshape(s))
        off += sz
    return out


def _forward_backward(inp, comm, bias_all):
    x = xm = inp['x'].reshape(-1, D)
    tgt = inp['loss_target'].reshape(-1, D)
    saved, Ws = [], []
    h_first, _ = comm.gather_start(0, GATHER_FIRST)
    h_rest, dep = comm.gather_start(0, GATHER_REST)
    h_first, tok = comm.gather_mid(h_first, x)
    G = comm.gather_finish(h_first, tok)
    for i in range(NL):
        W = _prep_layer_weights(i, inp, G)
        start_next = lambda: (comm.gather_start(i + 1, BIG + ['gate_b', 'conv_w']) if i + 1 < NL else (None, None))
        if i > 0:
            h_next, dep = start_next()
        x1, x1m, s1 = _ffn_fwd(x, xm, W['ffn1_w13'], W['ffn1_w2'], W['ln1_g'], W['ln1_b'], "f1", dep)
        if i == 0:
            h_rest, tok = comm.gather_mid(h_rest, x1m)
            W.update(_prep_layer_weights(i, inp, comm.gather_finish(h_rest, tok)))
            h_next, dep = start_next()
        x2, x2m, s2 = _mixer_fwd(x1, x1m, W, bias_all, "mx", dep if i == 0 else None)
        dep = None
        if h_next is not None:
            h_next, dep = comm.gather_mid(h_next, x2m)
        x, xm, s3 = _ffn_fwd(x2, x2m, W['ffn2_w13'], W['ffn2_w2'], W['ln3_g'], W['ln3_b'], "f2", dep)
        if h_next is not None:
            G = comm.gather_finish(h_next, xm)
        saved.append((s1, s2, s3))
        Ws.append(W)
    dy, lpart = _loss_fwd_bwd(x, tgt, name="loss")
    fins, reduced, dbiases = [None] * NL, [None] * NL, [None] * NL
    pend_a, pend_b, dep = None, None, None
    for i in reversed(range(NL)):
        W = Ws[i]
        s1, s2, s3 = saved[i]
        g = {}
        dx2, f = _ffn_bwd(dy, s3, W['ffn2_w13'], W['ffn2_w2'], W['ln3_g'], W['ln3_b'], "f2", dep)
        g['ffn2_w13'], g['ffn2_w2'], g['ln3_g'], g['ln3_b'] = f['w13'], f['w2'], f['g'], f['b']
        dep = None
        if pend_a is not None:
            handle, dep = comm.reduce_b(pend_a[1], dx2)
            pend_b = (pend_a[0], handle)
        dx1, gm, dbiases[i] = _mixer_bwd(dx2, s2, W, bias_all, "mx", dep)
        g.update(gm)
        dy, f = _ffn_bwd(dx1, s1, W['ffn1_w13'], W['ffn1_w2'], W['ln1_g'], W['ln1_b'], "f1")
        g['ffn1_w13'], g['ffn1_w2'], g['ln1_g'], g['ln1_b'] = f['w13'], f['w2'], f['g'], f['b']
        fins[i] = _finish_layer_grads(i, g, inp)
        if pend_b is not None:
            reduced[pend_b[0]] = comm.reduce_c(pend_b[1], dy)
            pend_b = None
        handle, dep = comm.reduce_a(i, fins[i])
        pend_a = (i, handle)
    return lpart, dy, fins, reduced, pend_a, dbiases, dep


def _finish_layer_grads(i, g, inp):
    out = {n: g[n] for n in BIG if n != 'w_in'}
    out['w_in'] = g['w_in_r']
    out['pool_w'] = jnp.stack([g['pool_wbd'][k * POOL_GDIM:(k + 1) * POOL_GDIM, k * POOL_GDIM:(k + 1) * POOL_GDIM] for k in range(4)])
    out['pool_b'] = g['pool_b'].reshape(4, POOL_GDIM)
    out['pool_scale'] = g['pool_scale'].reshape(POOLW)
    out['conv_w'] = g['conv_w']
    out['conv_b'] = g['conv_b'].reshape(2048)
    out['dt_bias'] = g['dt_bias'].reshape(16)
    out['a_log'] = (g['a_neg'].reshape(16)) * (-jnp.exp(inp['a_log'][i]))
    out['d_skip'] = g['d_skip'].reshape(16)
    out['ssd_norm'] = g['ssd_norm'].reshape(D)
    out['gate_b'] = g['gate_b'].reshape(3, D)
    for n in ('ln1_g', 'ln1_b', 'ln2_g', 'ln2_b', 'ln3_g', 'ln3_b'):
        out[n] = g[n].reshape(D)
    return out


def kernel(x, ffn1_w13, ffn1_w2, ln1_g, ln1_b, w_in, gate_b, pool_w, pool_b, pool_scale, conv_w, conv_b,
           dt_bias, a_log, d_skip, ssd_norm, rel_bias, p_pool, p_ssd, p_attn, w_out, ln2_g, ln2_b, ffn2_w13,
           ffn2_w2, ln3_g, ln3_b, loss_target, m_ffn1_w13, m_ffn1_w2, m_ln1_g, m_ln1_b, m_w_in, m_gate_b,
           m_pool_w, m_pool_b, m_pool_scale, m_conv_w, m_conv_b, m_dt_bias, m_a_log, m_d_skip, m_ssd_norm,
           m_rel_bias, m_p_pool, m_p_ssd, m_p_attn, m_w_out, m_ln2_g, m_ln2_b, m_ffn2_w13, m_ffn2_w2, m_ln3_g,
           m_ln3_b, v_ffn1_w13, v_ffn1_w2, v_ln1_g, v_ln1_b, v_w_in, v_gate_b, v_pool_w, v_pool_b,
           v_pool_scale, v_conv_w, v_conv_b, v_dt_bias, v_a_log, v_d_skip, v_ssd_norm, v_rel_bias, v_p_pool,
           v_p_ssd, v_p_attn, v_w_out, v_ln2_g, v_ln2_b, v_ffn2_w13, v_ffn2_w2, v_ln3_g, v_ln3_b):
    inp = dict(locals())
    maps = jnp.asarray(_bucket_maps())
    bias_all = _bias_build(rel_bias, maps, name="bias_build")
    comm = _Comm(inp)
    lpart, gx, fins, red, pending, dbiases, halves_started = _forward_backward(inp, comm, bias_all)
    loss = lax.psum(lpart[0, 0], ("x", "y", "c"))

    small_l = [n for n in SMALL if n != 'rel_bias']
    drel = _bias_reduce(jnp.stack(dbiases), maps, name="bias_reduce")[:, 0, :32].T
    small_arrs = [jnp.stack([fins[i][n] for i in range(NL)]) for n in small_l] + [drel]
    packed = _allreduce_small(_pack(small_arrs), dep=halves_started)
    handle_b, started = comm.reduce_b(pending[1], packed)
    gsmall = dict(zip(small_l + ['rel_bias'], _unpack(packed, [a.shape for a in small_arrs])))
    shard = 2 * lax.axis_index("x") + lax.axis_index("y")
    gsmall['gate_b'] = lax.dynamic_slice_in_dim(gsmall['gate_b'], shard * 256, 256, axis=2)
    gsmall['conv_w'] = lax.dynamic_slice_in_dim(gsmall['conv_w'], shard * 512, 512, axis=2)
    gout, delta, new_m, new_v = dict(gsmall), {}, {}, {}
    shapes = [inp[n].shape for n in SMALL]
    d, m, v = _adamw(_pack([inp[n] for n in SMALL]), _pack([gsmall[n] for n in SMALL]),
                     _pack([inp['m_' + n] for n in SMALL]), _pack([inp['v_' + n] for n in SMALL]), name="adamw_small")
    for n, dd, mm, vv in zip(SMALL, _unpack(d, shapes), _unpack(m, shapes), _unpack(v, shapes)):
        delta[n], new_m[n], new_v[n] = dd, mm, vv

    two_d = lambda a: a.reshape(a.shape[0] * a.shape[1], a.shape[2])
    accs = {n: None for n in BIG}

    def adamw_layer(i, dep=None):
        for n in BIG:
            accs[n] = _adamw_layer(i, two_d(inp[n]), red[i][n], two_d(inp['m_' + n]), two_d(inp['v_' + n]), accs[n],
                                   name="adamw_big", dep=dep)

    done = [i for i in range(NL) if i != pending[0]]
    for i in done:
        adamw_layer(i, dep=started)
    red[pending[0]] = comm.reduce_c(handle_b, [d] + ([accs[n][1] for n in BIG] if done else []))
    adamw_layer(pending[0])
    for n in BIG:
        gout[n], delta[n], new_m[n], new_v[n] = [a.reshape(inp[n].shape) for a in accs[n]]

    return (loss, gx.reshape(x.shape), *[gout[n] for n in WEIGHTS], *[delta[n] for n in WEIGHTS],
            *[new_m[n] for n in WEIGHTS], *[new_v[n] for n in WEIGHTS])
```

```python
import functools

import numpy as np
import jax
import jax.numpy as jnp
from jax import lax
from jax.experimental import pallas as pl
from jax.experimental.pallas import tpu as pltpu

F32 = jnp.float32
BF16 = jnp.bfloat16
_MXU = jnp.bfloat16
_ACT = jnp.bfloat16
_VMEM_LIMIT = 56 * 1024 * 1024

S = 2048
D = 1024
NL = 4
DFF = 2816
LN_EPS = 1e-5
SSD_EPS = 1e-5
ALPHA = (2.0 * NL) ** 0.25
POOLW = 768
POOL_WINDOWS = (2, 4, 8, 16)
POOL_GDIM = 192
CH = 128
ATTN_DILS = (1, 4, 16)
HC = 9728
O_U, O_Z, O_XBC, O_Q, O_K, O_V, O_G, O_DT = 0, 768, 1792, 3840, 4608, 5376, 6144, 9216

ADAM_LR, ADAM_B1, ADAM_B2, ADAM_EPS, ADAM_WD, ADAM_STEP = 0.001, 0.9, 0.999, 1e-08, 0.01, 10

WEIGHTS = ['ffn1_w13', 'ffn1_w2', 'ln1_g', 'ln1_b', 'w_in', 'gate_b', 'pool_w', 'pool_b', 'pool_scale', 'conv_w',
           'conv_b', 'dt_bias', 'a_log', 'd_skip', 'ssd_norm', 'rel_bias', 'p_pool', 'p_ssd', 'p_attn', 'w_out',
           'ln2_g', 'ln2_b', 'ffn2_w13', 'ffn2_w2', 'ln3_g', 'ln3_b']
BIG = ['ffn1_w13', 'ffn1_w2', 'w_in', 'p_pool', 'p_ssd', 'p_attn', 'w_out', 'ffn2_w13', 'ffn2_w2']
COL_SHARDED = {'ffn1_w13', 'ffn2_w13', 'w_in', 'p_pool', 'p_attn'}
SMALL = [n for n in WEIGHTS if n not in BIG]


def _pcall(body, **kw):
    return pl.pallas_call(body, **kw)


def _cp(sem=None):
    return pltpu.CompilerParams(dimension_semantics=sem, vmem_limit_bytes=_VMEM_LIMIT)


def _pick(n, cands):
    for c in cands:
        if n % c == 0:
            return c
    raise ValueError(f"no tile for {n}")


def _mm(a, b, *, ta=False, tb=False, add=None, out_dtype=F32, dep=None, name):
    if ta:
        K, M = a.shape
    else:
        M, K = a.shape
    if tb:
        N, K2 = b.shape
    else:
        K2, N = b.shape
    assert K == K2, (a.shape, b.shape, ta, tb)
    sa, sb, so = a.dtype.itemsize, b.dtype.itemsize, jnp.dtype(out_dtype).itemsize
    tm, tn, tk = _mm_tiles(M, N, K, sa, sb, so + (4 if add is not None else 0))
    nk = K // tk
    a_bytes, b_bytes = M * K * sa, K * N * sb
    j_outer = nk == 1 and (b_bytes + a_bytes * (N // tn) < a_bytes + b_bytes * (M // tm))
    ij = (lambda p, q: (q, p)) if j_outer else (lambda p, q: (p, q))

    def im(f):
        return lambda p, q, k: f(*ij(p, q), k)

    a_spec = pl.BlockSpec((tk, tm), im(lambda i, j, k: (k, i))) if ta else pl.BlockSpec((tm, tk), im(lambda i, j, k: (i, k)))
    b_spec = pl.BlockSpec((tn, tk), im(lambda i, j, k: (j, k))) if tb else pl.BlockSpec((tk, tn), im(lambda i, j, k: (k, j)))
    o_spec = pl.BlockSpec((tm, tn), im(lambda i, j, k: (i, j)))
    dims = (((0 if ta else 1,), (1 if tb else 0,)), ((), ()))
    has_add = add is not None

    n_in = 2 + int(has_add) + int(dep is not None)

    def body(*refs):
        a_ref, b_ref = refs[0], refs[1]
        add_ref = refs[2] if has_add else None
        o_ref = refs[n_in]
        part = lax.dot_general(a_ref[...].astype(_MXU), b_ref[...].astype(_MXU), dims, preferred_element_type=F32)

        def finish(r):
            if has_add:
                r = r + add_ref[...]
            o_ref[...] = r.astype(out_dtype)

        if nk == 1:
            finish(part)
        else:
            acc = refs[-1]
            k = pl.program_id(2)

            @pl.when(k == 0)
            def _():
                acc[...] = part

            @pl.when(k > 0)
            def _():
                acc[...] += part

            @pl.when(k == nk - 1)
            def _():
                finish(acc[...])

    in_specs = [a_spec, b_spec]
    args = [a, b]
    if has_add:
        in_specs.append(o_spec)
        args.append(add)
    if dep is not None:
        in_specs.append(pl.BlockSpec(memory_space=pl.ANY))
        args.append(dep)
    gm, gn = M // tm, N // tn
    return _pcall(
        body, name=name, grid=((gn, gm, nk) if j_outer else (gm, gn, nk)), in_specs=in_specs, out_specs=o_spec,
        out_shape=jax.ShapeDtypeStruct((M, N), out_dtype),
        scratch_shapes=([pltpu.VMEM((tm, tn), F32)] if nk > 1 else []),
        compiler_params=_cp(("parallel", "parallel", "arbitrary")),
    )(*args)


_MM_VMEM_BUDGET = 40 * 1024 * 1024


def _divisors128(n, cap):
    return [d for d in range(128, min(n, cap) + 1, 128) if n % d == 0][::-1]


_MM_CYC_PER_MMAC = 4.35
_MM_CYC_PER_ACC_VREG = 2.03
_MM_HBM_BYTES_PER_CYC = 1455.0
_MM_CYC_PER_STEP = 770.0


def _mm_tiles(M, N, K, sa, sb, so):
    best = None
    for tm in _divisors128(M, 1408):
        for tn in _divisors128(N, 2560):
            for tk in ([K] if K <= 4096 else []) + _divisors128(K, 2816):
                nk = K // tk
                need = 2 * (tm * tk * sa + tk * tn * sb + tm * tn * so) + (tm * tn * 4 if nk > 1 else 0)
                need += tm * tk * 2 + tk * tn * 2 + tm * tn * 4
                if need > _MM_VMEM_BUDGET:
                    continue
                gm, gn = M // tm, N // tn
                a_bytes, b_bytes = M * K * sa, K * N * sb
                hbm = min(b_bytes + a_bytes * gn, a_bytes + b_bytes * gm) if nk == 1 else a_bytes * gn + b_bytes * gm
                hbm += M * N * so
                work = _MM_CYC_PER_MMAC * M * N * K / 1e6 + _MM_CYC_PER_ACC_VREG * (M * N / 1024) * (nk if nk > 1 else 0.5)
                cost = max(work, hbm / _MM_HBM_BYTES_PER_CYC) + gm * gn * nk * _MM_CYC_PER_STEP
                if best is None or cost < best[0]:
                    best = (cost, (tm, tn, tk))
    assert best is not None, (M, N, K)
    return best[1]


def _store(ref, val):
    if isinstance(val, (list, tuple)):
        off = 0
        for p in val:
            w = p.shape[1]
            ref[:, off:off + w] = p.astype(ref.dtype)
            off += w
    else:
        ref[...] = val.astype(ref.dtype)


def _acc_store(ref, val, first):
    pieces = val if isinstance(val, (list, tuple)) else [val]
    off = 0
    for p in pieces:
        w = p.shape[1]

        @pl.when(first)
        def _(p=p, off=off, w=w):
            ref[:, off:off + w] = p

        @pl.when(jnp.logical_not(first))
        def _(p=p, off=off, w=w):
            ref[:, off:off + w] += p

        off += w


def _rowwise(fn, tiled, full, out_tiled, out_acc, *, name, tm=256):
    arrs, specs = [], []
    for t in tiled:
        arr, w, cb = t if isinstance(t, tuple) else (t, t.shape[1], 0)
        arrs.append(arr)
        specs.append(pl.BlockSpec((tm, w), functools.partial(lambda i, cb: (i, cb), cb=cb)))
    R = arrs[0].shape[0]
    assert R % tm == 0
    for f in full:
        arrs.append(f)
        specs.append(pl.BlockSpec(f.shape, functools.partial(lambda i, nd: (0,) * nd, nd=f.ndim)))
    nt, nf, no = len(tiled), len(full), len(out_tiled)

    def body(*refs):
        tv = [r[...] for r in refs[:nt]]
        fv = [r[...] for r in refs[nt:nt + nf]]
        ot, oa = fn(tv, fv)
        for r, v in zip(refs[nt + nf:nt + nf + no], ot):
            _store(r, v)
        first = pl.program_id(0) == 0
        for r, v in zip(refs[nt + nf + no:], oa):
            _acc_store(r, v, first)

    out_shape = [jax.ShapeDtypeStruct((R, c), dt) for c, dt in out_tiled]
    out_specs = [pl.BlockSpec((tm, c), lambda i: (i, 0)) for c, _ in out_tiled]
    for shp in out_acc:
        out_shape.append(jax.ShapeDtypeStruct(shp, F32))
        out_specs.append(pl.BlockSpec(shp, lambda i: (0, 0)))
    return _pcall(body, name=name, grid=(R // tm,), in_specs=specs, out_specs=out_specs, out_shape=out_shape,
                  compiler_params=_cp(("arbitrary",)))(*arrs)


def _group(group):
    x, y, c = lax.axis_index("x"), lax.axis_index("y"), lax.axis_index("c")
    if group == "chips":
        return 2 * x + y, [((x, 1 - y, c), 2 * x + 1 - y), ((1 - x, y, c), 2 * (1 - x) + y),
                           ((1 - x, 1 - y, c), 2 * (1 - x) + 1 - y)]
    if group == "cores":
        return c, [((x, y, 1 - c), 1 - c)]
    if group == "x":
        return x, [((1 - x, y, c), 1 - x)]
    return y, [((x, 1 - y, c), 1 - y)]


def _exchange(arrs, group, mode, name, dep=None):
    chips = group == "chips"
    k = len(arrs)
    npeer = 3 if chips else 1

    def body(*refs):
        nd = 0 if dep is None else 1
        ins, outs = refs[:k], refs[k + nd:2 * k + nd]
        send_sems, recv_sems = refs[2 * k + nd:]
        me, peers = _group(group)
        remote = []
        for i in range(k):
            for p, (dev, slot) in enumerate(peers):
                src = ins[i].at[slot] if mode == "scatter" else ins[i]
                if not chips:
                    dst = outs[i]
                else:
                    dst = outs[i].at[p] if mode == "scatter" else outs[i].at[me]
                cp = pltpu.make_async_remote_copy(src_ref=src, dst_ref=dst, send_sem=send_sems.at[i, p],
                                                  recv_sem=recv_sems.at[i, p], device_id=dev,
                                                  device_id_type=pl.DeviceIdType.MESH)
                cp.start()
                remote.append(cp)
        for cp in remote:
            cp.wait_recv()
        for cp in remote:
            cp.wait_send()

    def oshape(a):
        piece = a.shape[1:] if mode == "scatter" else a.shape
        if chips:
            piece = ((3,) if mode == "scatter" else (4,)) + piece
        return jax.ShapeDtypeStruct(piece, a.dtype)

    any_spec = pl.BlockSpec(memory_space=pl.ANY)
    extra = [] if dep is None else [dep]
    return _pcall(body, name=name, in_specs=[any_spec] * (k + len(extra)), out_specs=[any_spec] * k,
                  out_shape=[oshape(a) for a in arrs],
                  scratch_shapes=[pltpu.SemaphoreType.DMA((k, npeer)), pltpu.SemaphoreType.DMA((k, npeer))])(*arrs, *extra)


def _split_copies(ins, lands, send_sems, recv_sems, group, mode):
    chips = group == "chips"
    me, peers = _group(group)
    npeer = len(peers)
    out = []
    for i in range(len(ins)):
        for p, (dev, slot) in enumerate(peers):
            src = ins[i].at[slot] if mode == "scatter" else ins[i]
            if not chips:
                dst = lands[i]
            else:
                dst = lands[i].at[p] if mode == "scatter" else lands[i].at[me]
            out.append(pltpu.make_async_remote_copy(src_ref=src, dst_ref=dst, send_sem=send_sems.at[npeer * i + p],
                                                    recv_sem=recv_sems.at[npeer * i + p], device_id=dev,
                                                    device_id_type=pl.DeviceIdType.MESH))
    return out


def _exchange_start(arrs, group, mode, name):
    k = len(arrs)
    chips = group == "chips"
    nsem = (3 if chips else 1) * k
    hbm = pl.BlockSpec(memory_space=pltpu.HBM)
    sem = pl.BlockSpec(memory_space=pltpu.SEMAPHORE)

    def land_shape(a):
        piece = a.shape[1:] if mode == "scatter" else a.shape
        if chips:
            piece = ((3,) if mode == "scatter" else (4,)) + piece
        return piece

    def body(*refs):
        ins, lands = refs[:k], refs[k:2 * k]
        send_sems, recv_sems = refs[2 * k], refs[2 * k + 1]
        token = refs[-1]
        for cp in _split_copies(ins, lands, send_sems, recv_sems, group, mode):
            cp.start()
        token[...] = jnp.zeros_like(token)

    srcs = [pltpu.with_memory_space_constraint(a, pltpu.HBM) for a in arrs]
    lands = [pltpu.with_memory_space_constraint(lax.empty(land_shape(a), a.dtype), pltpu.HBM) for a in arrs]
    out_shape = ([pltpu.SemaphoreType.DMA((nsem,)), pltpu.SemaphoreType.DMA((nsem,))]
                 + [pltpu.HBM(a.shape, a.dtype) for a in arrs] + [pltpu.HBM(land_shape(a), a.dtype) for a in arrs]
                 + [jax.ShapeDtypeStruct((8, 128), F32)])
    outs = _pcall(body, name=name, in_specs=[hbm] * (2 * k),
                  out_specs=[sem, sem] + [hbm] * (2 * k) + [pl.BlockSpec(memory_space=pltpu.VMEM)], out_shape=out_shape,
                  input_output_aliases={i: 2 + i for i in range(2 * k)},
                  compiler_params=pltpu.CompilerParams(has_side_effects=pltpu.SideEffectType.DATAFLOW_SIDE_EFFECTING))(
                      *srcs, *lands)
    return (outs[0], outs[1], list(outs[2:2 + k]), list(outs[2 + k:2 + 2 * k])), outs[-1]


def _exchange_wait(state, after, group, mode, name):
    send_sems, recv_sems, srcs, lands = state
    k = len(srcs)
    after = list(after) if isinstance(after, (list, tuple)) else [after]
    hbm = pl.BlockSpec(memory_space=pltpu.HBM)
    sem = pl.BlockSpec(memory_space=pltpu.SEMAPHORE)

    def body(*refs):
        ins, lnd = refs[:k], refs[k:2 * k]
        send_sems, recv_sems = refs[2 * k], refs[2 * k + 1]
        for cp in _split_copies(ins, lnd, send_sems, recv_sems, group, mode):
            cp.wait_send()
            cp.wait_recv()

    outs = _pcall(body, name=name,
                  in_specs=[hbm] * (2 * k) + [sem, sem] + [pl.BlockSpec(memory_space=pl.ANY)] * len(after),
                  out_specs=[hbm] * (2 * k),
                  out_shape=[pltpu.HBM(a.shape, a.dtype) for a in srcs] + [pltpu.HBM(a.shape, a.dtype) for a in lands],
                  input_output_aliases={i: i for i in range(2 * k)},
                  compiler_params=pltpu.CompilerParams(has_side_effects=pltpu.SideEffectType.DATAFLOW_SIDE_EFFECTING))(
                      *srcs, *lands, send_sems, recv_sems, *after)
    return list(outs[:k]), list(outs[k:])


def _sum_own_recv(own, recv, me, out_dtype, name):
    n, R, C = own.shape
    nr = 1 if recv.ndim == 2 else recv.shape[0]
    tr = _pick(R, (256, 128, 64, 32, 16, 8))

    def body(me_ref, own_ref, *refs):
        o_ref = refs[-1]
        acc = own_ref[...].astype(F32)
        for r in refs[:-1]:
            acc = acc + r[...].astype(F32)
        o_ref[...] = acc.astype(out_dtype)

    specs = [pl.BlockSpec((None, tr, C), lambda i, me_ref: (me_ref[0], i, 0))]
    args = [own]
    if recv.ndim == 2:
        specs.append(pl.BlockSpec((tr, C), lambda i, me_ref: (i, 0)))
        args.append(recv)
    else:
        for p in range(nr):
            specs.append(pl.BlockSpec((None, tr, C), functools.partial(lambda i, me_ref, p: (p, i, 0), p=p)))
            args.append(recv)
    gs = pltpu.PrefetchScalarGridSpec(num_scalar_prefetch=1, grid=(R // tr,), in_specs=specs,
                                      out_specs=pl.BlockSpec((tr, C), lambda i, me_ref: (i, 0)))
    return _pcall(body, name=name, grid_spec=gs, out_shape=jax.ShapeDtypeStruct((R, C), out_dtype),
                  compiler_params=_cp(("parallel",)))(me, *args)


def _silu(x):
    return x * jax.nn.sigmoid(x)


def _ln(r, g, b):
    mu = jnp.mean(r, -1, keepdims=True)
    xc = r - mu
    var = jnp.mean(xc * xc, -1, keepdims=True)
    return xc * lax.rsqrt(var + LN_EPS) * g + b


def _softplus(x):
    return jnp.maximum(x, 0.0) + jnp.log1p(jnp.exp(-jnp.abs(x)))


RES_LN_TM = 512


def _mm_res_ln(a, w, x, g, b, res, name):
    T, K = a.shape
    tm = RES_LN_TM

    def body(a_ref, w_ref, x_ref, g_ref, b_ref, r_ref, o_ref, om_ref):
        y = jnp.dot(a_ref[...].astype(_MXU), w_ref[...].astype(_MXU), preferred_element_type=F32)
        r = ALPHA * x_ref[...] + res * y
        out = _ln(r, g_ref[...], b_ref[...])
        r_ref[...] = r
        o_ref[...] = out
        om_ref[...] = out.astype(om_ref.dtype)

    row = pl.BlockSpec((tm, D), lambda i: (i, 0))
    vec = pl.BlockSpec((1, D), lambda i: (0, 0))
    return _pcall(body, name=name, grid=(T // tm,),
                  in_specs=[pl.BlockSpec((tm, K), lambda i: (i, 0)), pl.BlockSpec((K, D), lambda i: (0, 0)), row, vec, vec],
                  out_specs=[row, row, row],
                  out_shape=[jax.ShapeDtypeStruct((T, D), F32), jax.ShapeDtypeStruct((T, D), F32),
                             jax.ShapeDtypeStruct((T, D), _ACT)],
                  compiler_params=_cp(("parallel",)))(a, w, x, g, b)


def _ln_bwd(r, g, b, dout, res, name):
    def fn(tv, fv):
        _, vjp = jax.vjp(_ln, tv[0], fv[0], fv[1])
        dr, dg, db = vjp(tv[1])
        return [ALPHA * dr, res * dr], [dg, db]
    return _rowwise(fn, [r, dout], [g, b], [(D, F32), (D, _ACT)], [(1, D), (1, D)], name=name)


SWIGLU_TM, SWIGLU_TN = 512, 1408


def _swiglu_fwd(x, w13, dep, name):
    T, K = x.shape
    tm, tn = SWIGLU_TM, SWIGLU_TN
    nj = DFF // tn
    has_dep = dep is not None

    def body(x_ref, wa_ref, wg_ref, *rest):
        a_ref, g_ref, s_ref = rest[-3:]
        xv = x_ref[...].astype(_MXU)
        a = jnp.dot(xv, wa_ref[...].astype(_MXU), preferred_element_type=F32)
        g = jnp.dot(xv, wg_ref[...].astype(_MXU), preferred_element_type=F32)
        a_ref[...] = a
        g_ref[...] = g
        s_ref[...] = (_silu(a) * g).astype(s_ref.dtype)

    out = pl.BlockSpec((tm, tn), lambda j, i: (i, j))
    in_specs = [pl.BlockSpec((tm, K), lambda j, i: (i, 0)), pl.BlockSpec((K, tn), lambda j, i: (0, j)),
                pl.BlockSpec((K, tn), lambda j, i: (0, nj + j))]
    args = [x, w13, w13]
    if has_dep:
        in_specs.append(pl.BlockSpec(memory_space=pl.ANY))
        args.append(dep)
    return _pcall(body, name=name, grid=(nj, T // tm), in_specs=in_specs, out_specs=[out, out, out],
                  out_shape=[jax.ShapeDtypeStruct((T, DFF), F32), jax.ShapeDtypeStruct((T, DFF), F32),
                             jax.ShapeDtypeStruct((T, DFF), _ACT)],
                  compiler_params=_cp(("parallel", "parallel")))(*args)


def _swiglu_act_bwd(a, g, ds, name):
    def fn(tv, fv):
        s, vjp = jax.vjp(lambda a, g: _silu(a) * g, tv[0], tv[1])
        da, dg = vjp(tv[2])
        return [[da, dg], s], []
    return _rowwise(fn, [a, g, ds], [], [(2 * DFF, _ACT), (DFF, _ACT)], [], name=name)


def _loss_fwd_bwd(y, tgt, name):
    def fn(tv, fv):
        e = tv[0] - tv[1]
        row = jnp.sum(e * e, axis=1, keepdims=True)
        tot = jnp.sum(row, axis=0, keepdims=True) * (0.5 / D)
        return [e * (1.0 / D)], [jnp.broadcast_to(tot, (1, 128))]
    return _rowwise(fn, [y, tgt], [], [(D, F32)], [(1, 128)], name=name)


def _shift_down(x, k, row):
    return jnp.where(row >= k, pltpu.roll(x, k, axis=0), 0.0)


def _shift_up(x, k, row):
    n = x.shape[0]
    return jnp.where(row < n - k, pltpu.roll(x, n - k, axis=0), 0.0)


def _pool_window_masks(j):
    lane = lax.broadcasted_iota(jnp.int32, (1, 128), 1) + j * 128
    grp = lane // POOL_GDIM
    return [grp == g for g in range(4)]


def _pool_mean(u, bwd, name, col0=0):
    T = u.shape[0]
    B = T // S

    def body(u_ref, o_ref):
        j = pl.program_id(1)
        x = u_ref[...]
        row = lax.broadcasted_iota(jnp.int32, (S, 1), 0)
        masks = _pool_window_masks(j)
        inv = [1.0 / jnp.minimum(row + 1, w).astype(F32) for w in POOL_WINDOWS]
        if not bwd:
            s2 = x + _shift_down(x, 1, row)
            s4 = s2 + _shift_down(s2, 2, row)
            s8 = s4 + _shift_down(s4, 4, row)
            s16 = s8 + _shift_down(s8, 8, row)
            mean = jnp.where(masks[0], s2 * inv[0], jnp.where(masks[1], s4 * inv[1],
                             jnp.where(masks[2], s8 * inv[2], s16 * inv[3])))
            o_ref[...] = (mean - x).astype(o_ref.dtype)
        else:
            g = [jnp.where(masks[i], x * inv[i], 0.0) for i in range(4)]
            t = g[3]
            t = t + _shift_up(t, 8, row) + g[2]
            t = t + _shift_up(t, 4, row) + g[1]
            t = t + _shift_up(t, 2, row) + g[0]
            t = t + _shift_up(t, 1, row)
            o_ref[...] = (t - x).astype(o_ref.dtype)

    spec = pl.BlockSpec((S, 128), lambda b, j: (b, j))
    return _pcall(body, name=name, grid=(B, POOLW // 128),
                  in_specs=[pl.BlockSpec((S, 128), lambda b, j: (b, j + col0))], out_specs=spec,
                  out_shape=jax.ShapeDtypeStruct((T, POOLW), _ACT), compiler_params=_cp(("parallel", "parallel")))(u)


def _conv_silu(xbc, w, b, name, col0=0):
    T, C = xbc.shape[0], w.shape[1]
    B = T // S

    def body(x_ref, w_ref, b_ref, o_ref):
        x = x_ref[...]
        row = lax.broadcasted_iota(jnp.int32, (S, 1), 0)
        c = b_ref[...] + w_ref[3:4, :] * x
        for s in range(1, 4):
            c = c + w_ref[3 - s:4 - s, :] * _shift_down(x, s, row)
        o_ref[...] = _silu(c)

    return _pcall(body, name=name, grid=(B, C // 128),
                  in_specs=[pl.BlockSpec((S, 128), lambda b, j: (b, j + col0)), pl.BlockSpec((4, 128), lambda b, j: (0, j)),
                            pl.BlockSpec((1, 128), lambda b, j: (0, j))],
                  out_specs=pl.BlockSpec((S, 128), lambda b, j: (b, j)),
                  out_shape=jax.ShapeDtypeStruct((T, C), F32), compiler_params=_cp(("parallel", "parallel")))(xbc, w, b)


def _conv_silu_bwd(xbc, w, b, dact, name, col0=0):
    T, C = xbc.shape[0], w.shape[1]
    B = T // S

    def body(x_ref, w_ref, b_ref, d_ref, dx_ref, dw_ref, db_ref):
        bi = pl.program_id(1)
        x = x_ref[...]
        row = lax.broadcasted_iota(jnp.int32, (S, 1), 0)
        xs = [x] + [_shift_down(x, s, row) for s in range(1, 4)]
        c = b_ref[...]
        for s in range(4):
            c = c + w_ref[3 - s:4 - s, :] * xs[s]
        _, vjp = jax.vjp(_silu, c)
        dc = vjp(d_ref[...])[0]
        dx = w_ref[3:4, :] * dc
        for s in range(1, 4):
            dx = dx + w_ref[3 - s:4 - s, :] * _shift_up(dc, s, row)
        dx_ref[...] = dx.astype(dx_ref.dtype)
        first = bi == 0
        for s in range(4):
            _acc_rows(dw_ref, 3 - s, jnp.sum(dc * xs[s], axis=0, keepdims=True), first)
        _acc_rows(db_ref, 0, jnp.sum(dc, axis=0, keepdims=True), first)

    blk = pl.BlockSpec((S, 128), lambda j, b: (b, j))
    return _pcall(body, name=name, grid=(C // 128, B),
                  in_specs=[pl.BlockSpec((S, 128), lambda j, b: (b, j + col0)), pl.BlockSpec((4, 128), lambda j, b: (0, j)),
                            pl.BlockSpec((1, 128), lambda j, b: (0, j)), blk],
                  out_specs=[blk, pl.BlockSpec((4, 128), lambda j, b: (0, j)), pl.BlockSpec((1, 128), lambda j, b: (0, j))],
                  out_shape=[jax.ShapeDtypeStruct((T, C), _ACT), jax.ShapeDtypeStruct((4, C), F32),
                             jax.ShapeDtypeStruct((1, C), F32)],
                  compiler_params=_cp(("parallel", "arbitrary")))(xbc, w, b, dact)


def _acc_rows(ref, r, val, first):
    @pl.when(first)
    def _():
        ref[r:r + 1, :] = val

    @pl.when(jnp.logical_not(first))
    def _():
        ref[r:r + 1, :] += val


def _tri_consts():
    i = lax.broadcasted_iota(jnp.int32, (CH, CH), 0)
    j = lax.broadcasted_iota(jnp.int32, (CH, CH), 1)
    return (i == j).astype(F32), (j <= i).astype(F32), (i <= j).astype(F32), i >= j


def _ssd_chunk(h, x, dt, Bm, Cm, a, dsk, consts):
    eye, tril, triu, lower = consts
    Bb = Bm.astype(_MXU)
    Cb = Cm.astype(_MXU)
    cb = lax.dot_general(Cb, Bb, (((1,), (1,)), ((), ())), preferred_element_type=F32)
    ys, hn = [], []
    for e in range(4):
        dtm = jnp.broadcast_to(dt[e], (CH, CH))
        adm = dtm * a[e]
        adt_row = jnp.sum(adm * eye, axis=0, keepdims=True)
        cs_col = jnp.sum(adt_row * tril, axis=1, keepdims=True)
        cs_row = jnp.sum(adm * triu, axis=0, keepdims=True)
        cs_last = jnp.sum(adt_row, axis=1, keepdims=True)
        csm = jnp.broadcast_to(cs_col, (CH, CH))
        decay = jnp.exp(jnp.where(lower, csm - cs_row, -jnp.inf))
        xb = (x[e] * dtm[:, 0:64]).astype(_MXU)
        y_diag = jnp.dot((cb * decay).astype(_MXU), xb, preferred_element_type=F32)
        bdec = (Bm * jnp.exp(cs_last - csm)).astype(_MXU)
        st = lax.dot_general(bdec, xb, (((0,), (0,)), ((), ())), preferred_element_type=F32)
        hn.append(h[e] * jnp.exp(cs_last) + st)
        y_off = jnp.exp(csm[:, 0:64]) * jnp.dot(Cb, h[e].astype(_MXU), preferred_element_type=F32)
        ys.append(y_diag + y_off + dsk[e] * x[e])
    return ys, hn


def _ssd_specs(order):
    def im(f):
        return lambda p, q: f(*order(p, q))
    xs = pl.BlockSpec((S, 256), im(lambda b, g: (b, g)))
    dt = pl.BlockSpec((None, S, 4), im(lambda b, g: (g, b, 0)))
    bc = pl.BlockSpec((S, 128), im(lambda b, g: (b, g)))
    hd = pl.BlockSpec((None, 1, 4), im(lambda b, g: (g, 0, 0)))
    hs = pl.BlockSpec((None, None, S // CH, 4, 128, 64), im(lambda b, g: (b, g, 0, 0, 0, 0)))
    bw = pl.BlockSpec((S, 128), im(lambda b, g: (b, 8 + g)))
    cw = pl.BlockSpec((S, 128), im(lambda b, g: (b, 12 + g)))
    return xs, dt, bc, hd, hs, bw, cw


def _ssd_fwd(act, dtg, a, dsk, name):
    xs = bm = cm = act
    T = xs.shape[0]
    B = T // S
    nc = S // CH

    def body(x_ref, dt_ref, b_ref, c_ref, a_ref, k_ref, y_ref, hs_ref, h_ref):
        consts = _tri_consts()
        h_ref[...] = jnp.zeros_like(h_ref)
        al = [a_ref[:, e:e + 1] for e in range(4)]
        kl = [k_ref[:, e:e + 1] for e in range(4)]

        def step(c, carry):
            r0 = pl.multiple_of(c * CH, CH)
            rows = pl.ds(r0, CH)
            h = [h_ref[e] for e in range(4)]
            for e in range(4):
                hs_ref[c, e] = h[e]
            x = [x_ref[rows, 64 * e:64 * e + 64] for e in range(4)]
            dt = [dt_ref[rows, e:e + 1] for e in range(4)]
            ys, hn = _ssd_chunk(h, x, dt, b_ref[rows, :], c_ref[rows, :], al, kl, consts)
            for e in range(4):
                y_ref[rows, 64 * e:64 * e + 64] = ys[e]
                h_ref[e] = hn[e]
            return carry

        lax.fori_loop(0, nc, step, 0)

    sx, sdt, sbc, shd, shs, sbw, scw = _ssd_specs(lambda b, g: (b, g))
    return _pcall(body, name=name, grid=(B, 4), in_specs=[sx, sdt, sbw, scw, shd, shd], out_specs=[sx, shs],
                  out_shape=[jax.ShapeDtypeStruct((T, 1024), F32), jax.ShapeDtypeStruct((B, 4, nc, 4, 128, 64), F32)],
                  scratch_shapes=[pltpu.VMEM((4, 128, 64), F32)],
                  compiler_params=_cp(("parallel", "parallel")))(xs, dtg, bm, cm, a, dsk)


def _lane_place(vals, width):
    lane = lax.broadcasted_iota(jnp.int32, (1, width), 1)
    out = jnp.zeros((1, width), F32)
    for e, v in enumerate(vals):
        out = out + jnp.where(lane == e, v, 0.0)
    return out


def _ssd_bwd(act, dtg, a, dsk, hs, dy, name):
    xs = bm = cm = act
    T = xs.shape[0]
    B = T // S
    nc = S // CH

    def body(x_ref, dt_ref, b_ref, c_ref, a_ref, k_ref, hs_ref, dy_ref,
             dx_ref, ddt_ref, db_ref, dc_ref, dak_ref, dh_ref, sc_ref):
        bi = pl.program_id(1)
        consts = _tri_consts()
        dh_ref[...] = jnp.zeros_like(dh_ref)
        sc_ref[...] = jnp.zeros_like(sc_ref)
        al = [a_ref[:, e:e + 1] for e in range(4)]
        kl = [k_ref[:, e:e + 1] for e in range(4)]

        def step(i, carry):
            c = nc - 1 - i
            r0 = pl.multiple_of(c * CH, CH)
            rows = pl.ds(r0, CH)
            h = [hs_ref[c, e] for e in range(4)]
            x = [x_ref[rows, 64 * e:64 * e + 64] for e in range(4)]
            dt = [dt_ref[rows, e:e + 1] for e in range(4)]
            f = functools.partial(_ssd_chunk, consts=consts)
            _, vjp = jax.vjp(f, h, x, dt, b_ref[rows, :], c_ref[rows, :], al, kl)
            dys = [dy_ref[rows, 64 * e:64 * e + 64] for e in range(4)]
            dhn = [dh_ref[e] for e in range(4)]
            dh, dx, ddt, dB, dC, da, dk = vjp((dys, dhn))
            for e in range(4):
                dh_ref[e] = dh[e]
                dx_ref[rows, 64 * e:64 * e + 64] = dx[e]
                ddt_ref[rows, e:e + 1] = ddt[e]
            db_ref[rows, :] = dB
            dc_ref[rows, :] = dC
            sc_ref[0:1, :] += _lane_place(da, 128)
            sc_ref[1:2, :] += _lane_place(dk, 128)
            return carry

        lax.fori_loop(0, nc, step, 0)
        first = bi == 0

        @pl.when(first)
        def _():
            dak_ref[...] = sc_ref[...]

        @pl.when(jnp.logical_not(first))
        def _():
            dak_ref[...] += sc_ref[...]

    sx, sdt, sbc, shd, shs, sbw, scw = _ssd_specs(lambda g, b: (b, g))
    return _pcall(body, name=name, grid=(4, B), in_specs=[sx, sdt, sbw, scw, shd, shd, shs, sx],
                  out_specs=[sx, sdt, sbc, sbc, pl.BlockSpec((None, 8, 128), lambda g, b: (g, 0, 0))],
                  out_shape=[jax.ShapeDtypeStruct((T, 1024), F32), jax.ShapeDtypeStruct((4, T, 4), F32),
                             jax.ShapeDtypeStruct((T, 512), F32), jax.ShapeDtypeStruct((T, 512), F32),
                             jax.ShapeDtypeStruct((4, 8, 128), F32)],
                  scratch_shapes=[pltpu.VMEM((4, 128, 64), F32), pltpu.VMEM((8, 128), F32)],
                  compiler_params=_cp(("parallel", "arbitrary")))(xs, dtg, bm, cm, a, dsk, hs, dy)


def _gate_norm(y, z, nw):
    t = y * _silu(z)
    return t * lax.rsqrt(jnp.mean(t * t, axis=-1, keepdims=True) + SSD_EPS) * nw


def _ssd_gate_norm(y, z, nw, name, zcol=0):
    def fn(tv, fv):
        return [[_gate_norm(tv[g], tv[4 + g], fv[0][:, 256 * g:256 * g + 256]) for g in range(4)]], []
    tiled = [(y, 256, g) for g in range(4)] + [(z, 256, zcol + g) for g in range(4)]
    return _rowwise(fn, tiled, [nw], [(1024, _ACT)], [], name=name)[0]


def _ssd_gate_norm_bwd(y, z, nw, dout, name, zcol=0):
    def fn(tv, fv):
        dys, dzs, dns = [], [], []
        for g in range(4):
            _, vjp = jax.vjp(_gate_norm, tv[g], tv[4 + g], fv[0][:, 256 * g:256 * g + 256])
            a, b, c = vjp(tv[8 + g])
            dys.append(a)
            dzs.append(b)
            dns.append(c)
        return [dys, dzs], [dns]
    tiled = [(y, 256, g) for g in range(4)] + [(z, 256, zcol + g) for g in range(4)] + [(dout, 256, g) for g in range(4)]
    return _rowwise(fn, tiled, [nw], [(1024, F32), (1024, _ACT)], [(1, 1024)], name=name)


def _t5_bucket_np(dist):
    dist = np.maximum(dist, 0)
    max_exact = 16
    large = max_exact + (np.log(np.maximum(dist, 1) / max_exact) / np.log(2048 / max_exact) * (32 - max_exact)).astype(np.int32)
    large = np.minimum(large, 31)
    return np.where(dist < max_exact, dist, large).astype(np.int32)


def _bucket_maps():
    qi = np.arange(128)[:, None]
    kj = np.arange(256)[None, :]
    return np.stack([_t5_bucket_np((qi - kj + 128) * dil) for dil in ATTN_DILS]).astype(np.int32)


def _bias_build(rel_bias, maps, name):
    def body(tab_ref, map_ref, o_ref):
        hh = pl.program_id(0)
        m = map_ref[...]
        acc = jnp.zeros((128, 256), F32)
        for b in range(32):
            acc = jnp.where(m == b, tab_ref[b, hh], acc)
        o_ref[...] = acc

    return _pcall(body, name=name, grid=(12,),
                  in_specs=[pl.BlockSpec(memory_space=pltpu.SMEM), pl.BlockSpec((None, 128, 256), lambda h: (h // 4, 0, 0))],
                  out_specs=pl.BlockSpec((None, 128, 256), lambda h: (h, 0, 0)),
                  out_shape=jax.ShapeDtypeStruct((12, 128, 256), F32), compiler_params=_cp(("parallel",)))(rel_bias, maps)


def _bias_reduce(dbias, maps, name):
    nl = dbias.shape[0]

    def body(d_ref, map_ref, o_ref):
        m = map_ref[...]
        d = d_ref[0]
        for i in range(1, nl):
            d = d + d_ref[i]
        lane = lax.broadcasted_iota(jnp.int32, (1, 128), 1)
        out = jnp.zeros((1, 128), F32)
        for b in range(32):
            s = jnp.sum(jnp.sum(jnp.where(m == b, d, 0.0), axis=1, keepdims=True), axis=0, keepdims=True)
            out = out + jnp.where(lane == b, s, 0.0)
        o_ref[...] = out

    return _pcall(body, name=name, grid=(12,),
                  in_specs=[pl.BlockSpec((nl, None, 128, 256), lambda h: (0, h, 0, 0)),
                            pl.BlockSpec((None, 128, 256), lambda h: (h // 4, 0, 0))],
                  out_specs=pl.BlockSpec((None, 1, 128), lambda h: (h, 0, 0)),
                  out_shape=jax.ShapeDtypeStruct((12, 1, 128), F32), compiler_params=_cp(("parallel",)))(dbias, maps)


def _attn_block(q, kb, vb, bias, mask):
    s = lax.dot_general(q.astype(_MXU), kb.astype(_MXU), (((1,), (1,)), ((), ())), preferred_element_type=F32) * 0.125 + bias
    s = jnp.where(mask, s, -jnp.inf)
    m = lax.stop_gradient(jnp.max(s, axis=-1, keepdims=True))
    p = jnp.exp(s - m)
    den = jnp.sum(p, axis=-1, keepdims=True)
    out = jnp.dot((p / den).astype(_MXU), vb.astype(_MXU), preferred_element_type=F32)
    return out, m + jnp.log(den)


ATTN_QB = 512


def _attn_masks(dil):
    qi = lax.broadcasted_iota(jnp.int32, (ATTN_QB, ATTN_QB + 128), 0)
    kj = lax.broadcasted_iota(jnp.int32, (ATTN_QB, ATTN_QB + 128), 1)
    band = (kj >= qi) & (kj <= qi + 128)
    if dil == 16:
        q2 = lax.broadcasted_iota(jnp.int32, (ATTN_QB, ATTN_QB), 0)
        k2 = lax.broadcasted_iota(jnp.int32, (ATTN_QB, ATTN_QB), 1)
        return ((q2 // 128) == (k2 // 128)) & (k2 <= q2), None
    return band[:, 128:], band


def _attn_wide_bias(b, dil):
    if dil == 16:
        return jnp.tile(b[:, 128:], (4, 4)), None
    z = jnp.zeros((128, 128), F32)
    band = jnp.concatenate([jnp.concatenate([z] * i + [b] + [z] * (3 - i), axis=1) for i in range(4)], axis=0)
    return band[:, 128:], band


def _fold_dbias(dbs, dil, band_form):
    def blk(i, j):
        return dbs[128 * i:128 * i + 128, 128 * j:128 * j + 128]
    if band_form:
        return sum(blk(i, i) for i in range(4)), sum(blk(i, i + 1) for i in range(4))
    cur = sum(blk(i, i) for i in range(4))
    if dil == 16:
        return None, cur
    return sum(blk(i, i - 1) for i in range(1, 4)), cur


def _attn_chunks(dil):
    out = []
    for n in range(S // ATTN_QB):
        if dil == 1 and n > 0:
            out.append((n * ATTN_QB, n * ATTN_QB - 128, ATTN_QB + 128, True))
        else:
            out.append((n * ATTN_QB, n * ATTN_QB, ATTN_QB, False))
    return out


def _qkv_specs(gi, order):
    def spec(base):
        col = (base + 256 * gi) // 128
        return pl.BlockSpec((S, 128), lambda p, q: (order(p, q)[0], col + order(p, q)[1]))
    return [spec(O_Q), spec(O_K), spec(O_V)]


def _residue_rows(r, dil):
    return pl.ds(r, S // dil, stride=dil)


def _attn_fwd(hcat, bias_all, gi, name):
    dil = ATTN_DILS[gi]
    T = hcat.shape[0]
    B, L = T // S, S // dil

    def body(q_ref, k_ref, v_ref, b_ref, o_ref, l_ref, *scr):
        mask_first, mask_band = _attn_masks(dil)
        if dil > 1:
            qs, ks, vs, os_, ls = scr
            for r in range(dil):
                rows, dst = _residue_rows(r, dil), pl.ds(r * L, L)
                qs[dst, :] = q_ref[rows, :]
                ks[dst, :] = k_ref[rows, :]
                vs[dst, :] = v_ref[rows, :]
        else:
            qs, ks, vs, os_, ls = q_ref, k_ref, v_ref, o_ref, l_ref
        ls[...] = jnp.zeros_like(ls)
        for e in range(2):
            lanes = slice(64 * e, 64 * e + 64)
            bias_first, bias_band = _attn_wide_bias(b_ref[e], dil)
            for q0, k0, kn, band_form in _attn_chunks(dil):
                cur, keys = pl.ds(q0, ATTN_QB), pl.ds(k0, kn)
                o, l = _attn_block(qs[cur, lanes], ks[keys, lanes], vs[keys, lanes],
                                   bias_band if band_form else bias_first, mask_band if band_form else mask_first)
                os_[cur, lanes] = o
                ls[cur, e:e + 1] = l
        if dil > 1:
            for r in range(dil):
                rows, src = _residue_rows(r, dil), pl.ds(r * L, L)
                o_ref[rows, :] = os_[src, :]
                l_ref[rows, :] = ls[src, :]

    scratch = [pltpu.VMEM((S, 128), F32)] * 5 if dil > 1 else []
    return _pcall(body, name=name, grid=(B, 2),
                  in_specs=_qkv_specs(gi, lambda b, hp: (b, hp))
                  + [pl.BlockSpec((2, 128, 256), lambda b, hp: (2 * gi + hp, 0, 0))],
                  out_specs=[pl.BlockSpec((S, 128), lambda b, hp: (b, hp)),
                             pl.BlockSpec((None, S, 128), lambda b, hp: (hp, b, 0))],
                  out_shape=[jax.ShapeDtypeStruct((T, 256), F32), jax.ShapeDtypeStruct((2, T, 128), F32)],
                  scratch_shapes=scratch,
                  compiler_params=_cp(("parallel", "parallel")))(hcat, hcat, hcat, bias_all)


def _attn_bwd(hcat, bias_all, gi, do, dl, name):
    dil = ATTN_DILS[gi]
    T = hcat.shape[0]
    B, L = T // S, S // dil

    def body(q_ref, k_ref, v_ref, b_ref, do_ref, dl_ref, dq_ref, dk_ref, dv_ref, db_ref, acc_ref, *scr):
        bi = pl.program_id(1)
        mask_first, mask_band = _attn_masks(dil)
        if dil > 1:
            qs, ks, vs, dos, dls, dqs, dks, dvs = scr
            for r in range(dil):
                rows, dst = _residue_rows(r, dil), pl.ds(r * L, L)
                qs[dst, :] = q_ref[rows, :]
                ks[dst, :] = k_ref[rows, :]
                vs[dst, :] = v_ref[rows, :]
                dos[dst, :] = do_ref[rows, :]
                dls[dst, :] = dl_ref[rows, :]
        else:
            qs, ks, vs, dos, dls, dqs, dks, dvs = q_ref, k_ref, v_ref, do_ref, dl_ref, dq_ref, dk_ref, dv_ref
        dks[...] = jnp.zeros_like(dks)
        dvs[...] = jnp.zeros_like(dvs)
        for e in range(2):
            lanes = slice(64 * e, 64 * e + 64)
            bias_first, bias_band = _attn_wide_bias(b_ref[e], dil)
            acc_ref[...] = jnp.zeros_like(acc_ref)
            for q0, k0, kn, band_form in _attn_chunks(dil):
                cur, keys = pl.ds(q0, ATTN_QB), pl.ds(k0, kn)
                f = functools.partial(_attn_block, mask=mask_band if band_form else mask_first)
                _, vjp = jax.vjp(f, qs[cur, lanes], ks[keys, lanes], vs[keys, lanes],
                                 bias_band if band_form else bias_first)
                dq, dkb, dvb, dbs = vjp((dos[cur, lanes], dls[cur, e:e + 1]))
                dqs[cur, lanes] = dq
                dks[keys, lanes] += dkb
                dvs[keys, lanes] += dvb
                prev, here = _fold_dbias(dbs, dil, band_form)
                if prev is not None:
                    acc_ref[:, 0:128] += prev
                acc_ref[:, 128:256] += here

            @pl.when(bi == 0)
            def _(e=e):
                db_ref[e] = acc_ref[...]

            @pl.when(bi > 0)
            def _(e=e):
                db_ref[e] += acc_ref[...]

        if dil > 1:
            for r in range(dil):
                rows, src = _residue_rows(r, dil), pl.ds(r * L, L)
                dq_ref[rows, :] = dqs[src, :]
                dk_ref[rows, :] = dks[src, :]
                dv_ref[rows, :] = dvs[src, :]

    order = lambda hp, b: (b, hp)
    blk = pl.BlockSpec((S, 128), lambda hp, b: (b, hp))
    lblk = pl.BlockSpec((None, S, 128), lambda hp, b: (hp, b, 0))
    sds = jax.ShapeDtypeStruct((T, 256), F32)
    scratch = [pltpu.VMEM((128, 256), F32)] + ([pltpu.VMEM((S, 128), F32)] * 8 if dil > 1 else [])
    return _pcall(body, name=name, grid=(2, B),
                  in_specs=_qkv_specs(gi, order) + [pl.BlockSpec((2, 128, 256), lambda hp, b: (2 * gi + hp, 0, 0)), blk, lblk],
                  out_specs=[blk, blk, blk, pl.BlockSpec((2, 128, 256), lambda hp, b: (hp, 0, 0))],
                  out_shape=[sds, sds, sds, jax.ShapeDtypeStruct((4, 128, 256), F32)],
                  scratch_shapes=scratch,
                  compiler_params=_cp(("parallel", "arbitrary")))(hcat, hcat, hcat, bias_all, do, dl)


def _lse_merge(o0, o1, o2, l0, l1, l2):
    m = lax.stop_gradient(jnp.maximum(jnp.maximum(l0, l1), l2))
    e0, e1, e2 = jnp.exp(l0 - m), jnp.exp(l1 - m), jnp.exp(l2 - m)
    den = e0 + e1 + e2
    return (e0 / den) * o0 + (e1 / den) * o1 + (e2 / den) * o2


def _attn_merge(outs, lses, dy, name):
    T = outs[0].shape[0]
    bwd = dy is not None
    tm = 512

    def body(*refs):
        o_refs, l_refs = refs[:3], refs[3:6]
        if bwd:
            for r in refs[10:13]:
                r[...] = jnp.zeros_like(r)
        for e in range(2):
            lanes = slice(64 * e, 64 * e + 64)
            vals = [r[:, lanes] for r in o_refs] + [r[:, e:e + 1] for r in l_refs]
            if not bwd:
                refs[6][:, lanes] = _lse_merge(*vals).astype(refs[6].dtype)
            else:
                _, vjp = jax.vjp(_lse_merge, *vals)
                g = vjp(refs[6][:, lanes])
                for r, v in zip(refs[7:10], g[:3]):
                    r[:, lanes] = v
                for r, v in zip(refs[10:13], g[3:]):
                    r[:, e:e + 1] = v

    blk = pl.BlockSpec((tm, 128), lambda i, hp: (i, hp))
    lblk = pl.BlockSpec((None, tm, 128), lambda i, hp: (hp, i, 0))
    lsd = jax.ShapeDtypeStruct((2, T, 128), F32)
    if not bwd:
        return _pcall(body, name=name, grid=(T // tm, 2), in_specs=[blk] * 3 + [lblk] * 3, out_specs=blk,
                      out_shape=jax.ShapeDtypeStruct((T, 256), F32),
                      compiler_params=_cp(("parallel", "parallel")))(*outs, *lses)
    return _pcall(body, name=name, grid=(T // tm, 2), in_specs=[blk] * 3 + [lblk] * 3 + [blk],
                  out_specs=[blk] * 3 + [lblk] * 3, out_shape=[jax.ShapeDtypeStruct((T, 256), F32)] * 3 + [lsd] * 3,
                  compiler_params=_cp(("parallel", "parallel")))(*outs, *lses, dy)


def _gmerge(g0, g1, g2, gb, ya, yb, yc):
    return (jax.nn.sigmoid(g0 + gb[:, 0:D]) * ya + jax.nn.sigmoid(g1 + gb[:, D:2 * D]) * yb
            + jax.nn.sigmoid(g2 + gb[:, 2 * D:3 * D]) * yc)


def _gated_merge(gates, gb, ya, yb, yc, name, gcol=0):
    def fn(tv, fv):
        return [_gmerge(tv[0], tv[1], tv[2], fv[0], tv[3], tv[4], tv[5])], []
    return _rowwise(fn, [(gates, D, gcol), (gates, D, gcol + 1), (gates, D, gcol + 2), ya, yb, yc], [gb], [(D, _ACT)], [],
                    name=name)[0]


def _gated_merge_bwd(gates, gb, ya, yb, yc, dm, name, gcol=0):
    def fn(tv, fv):
        _, vjp = jax.vjp(_gmerge, tv[0], tv[1], tv[2], fv[0], tv[3], tv[4], tv[5])
        d0, d1, d2, dgb, da, db, dc = vjp(tv[6])
        return [[d0, d1, d2], da, db, dc], [dgb]
    return _rowwise(fn, [(gates, D, gcol), (gates, D, gcol + 1), (gates, D, gcol + 2), ya, yb, yc, dm], [gb],
                    [(3 * D, _ACT), (D, _ACT), (D, _ACT), (D, _ACT)], [(1, 3 * D)], name=name)


def _pool_affine(t1, pb, ps, dout, name):
    if dout is None:
        def fn(tv, fv):
            return [(tv[0] + fv[0]) * fv[1]], []
        return _rowwise(fn, [t1], [pb, ps], [(POOLW, _ACT)], [], name=name)[0]

    def fnb(tv, fv):
        t2, vjp = jax.vjp(lambda t, b, s: (t + b) * s, tv[0], fv[0], fv[1])
        dt, db, dsc = vjp(tv[1])
        return [dt, t2], [db, dsc]
    return _rowwise(fnb, [t1, dout], [pb, ps], [(POOLW, _ACT), (POOLW, _ACT)], [(1, POOLW), (1, POOLW)], name=name)


def _dt_softplus(dt_raw, dt_bias, ddt, name):
    f = lambda r, b: _softplus(r + b)
    if ddt is None:
        def fn(tv, fv):
            return [f(tv[0], fv[0])], []
        return _rowwise(fn, [dt_raw], [dt_bias], [(16, F32)], [], name=name, tm=1024)[0]

    def fnb(tv, fv):
        _, vjp = jax.vjp(f, tv[0], fv[0])
        dr, db = vjp(tv[1])
        return [dr], [db]
    return _rowwise(fnb, [dt_raw, ddt], [dt_bias], [(16, F32)], [(1, 16)], name=name, tm=1024)


def _adamw_math(wv, gv, mv, vv):
    c1 = 1.0 / (1.0 - ADAM_B1 ** ADAM_STEP)
    c2 = 1.0 / (1.0 - ADAM_B2 ** ADAM_STEP)
    mn = ADAM_B1 * mv + (1.0 - ADAM_B1) * gv
    vn = ADAM_B2 * vv + (1.0 - ADAM_B2) * (gv * gv)
    delta = -ADAM_LR * ((mn * c1) / (jnp.sqrt(vn * c2) + ADAM_EPS) + ADAM_WD * wv)
    return delta, mn, vn


def _adamw(w, g, m, v, name):
    R, C = w.shape
    tm = _pick(R, (256, 128, 64, 32, 16, 8))
    return _rowwise(lambda tv, fv: (list(_adamw_math(*tv)), []), [w, g, m, v], [], [(C, F32)] * 3, [], name=name, tm=tm)


def _adamw_layer(i, w, g, m, v, accs, name, dep=None):
    R, C = w.shape
    r = R // NL
    tm = _pick(r, (256, 128, 64, 32, 16, 8))
    nt = r // tm
    if accs is None:
        accs = [lax.empty((R, C), F32) for _ in range(4)]
    extra = [] if dep is None else [dep]

    def body(w_ref, g_ref, m_ref, v_ref, *rest):
        go_ref, do_ref, mo_ref, vo_ref = rest[-4:]
        gv = g_ref[...]
        delta, mn, vn = _adamw_math(w_ref[...], gv, m_ref[...], v_ref[...])
        go_ref[...] = gv
        do_ref[...] = delta
        mo_ref[...] = mn
        vo_ref[...] = vn

    slab = pl.BlockSpec((tm, C), lambda t: (i * nt + t, 0))
    anyspec = pl.BlockSpec(memory_space=pl.ANY)
    return _pcall(body, name=name, grid=(nt,),
                  in_specs=[slab, pl.BlockSpec((tm, C), lambda t: (t, 0)), slab, slab] + [anyspec] * (4 + len(extra)),
                  out_specs=[slab] * 4, out_shape=[jax.ShapeDtypeStruct((R, C), F32)] * 4,
                  input_output_aliases={4 + k: k for k in range(4)},
                  compiler_params=_cp(("parallel",)))(w, g, m, v, *accs, *extra)


def _ffn_fwd(x, xm, w13, w2, g, b, tag, dep=None):
    ha, hg, s = _swiglu_fwd(xm, w13, dep, name=f"{tag}_h")
    r, out, outm = _mm_res_ln(s, w2, x, g, b, 0.5, name=f"{tag}_y")
    return out, outm, dict(x=xm, ha=ha, hg=hg, r=r)


def _ffn_bwd(dout, sv, w13, w2, g, b, tag, dep=None):
    dskip, dy, dg, db = _ln_bwd(sv['r'], g, b, dout, 0.5, name=f"{tag}_lnb")
    ds = _mm(dy, w2, tb=True, dep=dep, name=f"{tag}_ds")
    dh, s = _swiglu_act_bwd(sv['ha'], sv['hg'], ds, name=f"{tag}_actb")
    dw2 = _mm(s, dy, ta=True, name=f"{tag}_dw2")
    dw13 = _mm(sv['x'], dh, ta=True, name=f"{tag}_dw13")
    dx = _mm(dh, w13, tb=True, add=dskip, name=f"{tag}_dx")
    return dx, dict(w13=dw13, w2=dw2, g=dg, b=db)


def _mixer_fwd(x1, x1m, W, bias_all, tag, dep=None):
    T = x1.shape[0]
    hcat = _mm(x1m, W['w_in_r'], dep=dep, name=f"{tag}_hcat")
    dt_raw = hcat[:, O_DT:O_DT + 16]
    pooled = _pool_mean(hcat, False, name=f"{tag}_pool", col0=O_U // 128)
    t1 = _mm(pooled, W['pool_wbd'], name=f"{tag}_pt1")
    t2 = _pool_affine(t1, W['pool_b'], W['pool_scale'], None, name=f"{tag}_paff")
    ya = _mm(t2, W['p_pool'], name=f"{tag}_ya")
    act = _conv_silu(hcat, W['conv_w'], W['conv_b'], name=f"{tag}_conv", col0=O_XBC // 128)
    dt = _dt_softplus(dt_raw, W['dt_bias'], None, name=f"{tag}_dt")
    dtg = dt.reshape(T, 4, 4).transpose(1, 0, 2)
    yscan, hs = _ssd_fwd(act, dtg, W['a_neg'], W['d_skip'], name=f"{tag}_ssd")
    ybn = _ssd_gate_norm(yscan, hcat, W['ssd_norm'], name=f"{tag}_gn", zcol=O_Z // 256)
    yb = _mm(ybn, W['p_ssd'], name=f"{tag}_yb")
    outs, lses = [], []
    for gi in range(len(ATTN_DILS)):
        o, l = _attn_fwd(hcat, bias_all, gi, name=f"{tag}_attn{gi}")
        outs.append(o)
        lses.append(l)
    ycp = _attn_merge(outs, lses, None, name=f"{tag}_amerge")
    yc = _mm(ycp, W['p_attn'], name=f"{tag}_yc")
    merged = _gated_merge(hcat, W['gate_b'], ya, yb, yc, name=f"{tag}_gm", gcol=O_G // D)
    r, out, outm = _mm_res_ln(merged, W['w_out'], x1, W['ln2_g'], W['ln2_b'], 1.0, name=f"{tag}_mix")
    sv = dict(x1=x1m, dt_raw=dt_raw, pooled=pooled, t1=t1, act=act, dtg=dtg,
              hs=hs, yscan=yscan, ybn=ybn, hcat=hcat, outs=outs, lses=lses, ycp=ycp, ya=ya, yb=yb, yc=yc,
              merged=merged, r=r)
    return out, outm, sv


def _mixer_bwd(dout, sv, W, bias_all, tag, dep=None):
    T = dout.shape[0]
    gr = {}
    dx1a, dr, gr['ln2_g'], gr['ln2_b'] = _ln_bwd(sv['r'], W['ln2_g'], W['ln2_b'], dout, 1.0, name=f"{tag}_lnb")
    dmerged = _mm(dr, W['w_out'], tb=True, dep=dep, name=f"{tag}_dmerged")
    gr['w_out'] = _mm(sv['merged'], dr, ta=True, name=f"{tag}_dwout")
    dgates, dya, dyb, dyc, gr['gate_b'] = _gated_merge_bwd(sv['hcat'], W['gate_b'], sv['ya'], sv['yb'], sv['yc'],
                                                           dmerged, name=f"{tag}_gmb", gcol=O_G // D)
    dycp = _mm(dyc, W['p_attn'], tb=True, name=f"{tag}_dycp")
    gr['p_attn'] = _mm(sv['ycp'], dyc, ta=True, name=f"{tag}_dpattn")
    dml = _attn_merge(sv['outs'], sv['lses'], dycp, name=f"{tag}_amergeb")
    dq, dk, dv, dbias = [], [], [], []
    for gi in range(len(ATTN_DILS)):
        a, b, c, d = _attn_bwd(sv['hcat'], bias_all, gi, dml[gi], dml[3 + gi], name=f"{tag}_attnb{gi}")
        dq.append(a)
        dk.append(b)
        dv.append(c)
        dbias.append(d)
    dbias = jnp.concatenate(dbias, axis=0)
    dybn = _mm(dyb, W['p_ssd'], tb=True, name=f"{tag}_dybn")
    gr['p_ssd'] = _mm(sv['ybn'], dyb, ta=True, name=f"{tag}_dpssd")
    dyscan, dz, gr['ssd_norm'] = _ssd_gate_norm_bwd(sv['yscan'], sv['hcat'], W['ssd_norm'], dybn, name=f"{tag}_gnb",
                                                    zcol=O_Z // 256)
    dxs, ddtg, dbm, dcm, dak = _ssd_bwd(sv['act'], sv['dtg'], W['a_neg'], W['d_skip'], sv['hs'], dyscan,
                                        name=f"{tag}_ssdb")
    gr['a_neg'], gr['d_skip'] = dak[:, 0, 0:4], dak[:, 1, 0:4]
    ddt = ddtg.transpose(1, 0, 2).reshape(T, 16)
    ddt_raw, gr['dt_bias'] = _dt_softplus(sv['dt_raw'], W['dt_bias'], ddt, name=f"{tag}_dtb")
    dact = jnp.concatenate([dxs, dbm, dcm], axis=1)
    dxbc, gr['conv_w'], gr['conv_b'] = _conv_silu_bwd(sv['hcat'], W['conv_w'], W['conv_b'], dact, name=f"{tag}_convb",
                                                      col0=O_XBC // 128)
    dt2 = _mm(dya, W['p_pool'], tb=True, name=f"{tag}_dt2")
    dt1, t2, gr['pool_b'], gr['pool_scale'] = _pool_affine(sv['t1'], W['pool_b'], W['pool_scale'], dt2, name=f"{tag}_paffb")
    gr['p_pool'] = _mm(t2, dya, ta=True, name=f"{tag}_dppool")
    dpooled = _mm(dt1, W['pool_wbd'], tb=True, name=f"{tag}_dpooled")
    gr['pool_wbd'] = _mm(sv['pooled'], dt1, ta=True, name=f"{tag}_dpoolw")
    du = _pool_mean(dpooled, True, name=f"{tag}_poolb")
    dhcat = jnp.concatenate([t.astype(_ACT) for t in [du, dz, dxbc] + dq + dk + dv + [dgates, ddt_raw]]
                            + [jnp.zeros((T, HC - O_DT - 16), _ACT)], axis=1)
    dx1 = _mm(dhcat, W['w_in_r'], tb=True, add=dx1a, name=f"{tag}_dx1")
    gr['w_in_r'] = _mm(sv['x1'], dhcat, ta=True, name=f"{tag}_dwin")
    return dx1, gr, dbias


def _prep_layer_weights(i, inp, G):
    W = {}
    for n in BIG:
        if n not in G:
            continue
        g = G[n]
        if n == 'w_in':
            W['w_in_r'] = jnp.concatenate(_nat_pieces(g, 0, 3840) + _nat_pieces(g, 3856, 9232) + _nat_pieces(g, 3840, 3856)
                                          + [jnp.zeros((D, HC - 9232), g.dtype)], axis=1)
        elif n in COL_SHARDED:
            W[n] = jnp.concatenate([g[j] for j in range(4)], axis=1)
        else:
            W[n] = g.reshape(4 * g.shape[1], g.shape[2])
    pw = inp['pool_w'][i].astype(_MXU)
    wbd = jnp.zeros((POOLW, POOLW), _MXU)
    for g in range(4):
        wbd = lax.dynamic_update_slice(wbd, pw[g], (g * POOL_GDIM, g * POOL_GDIM))
    W['pool_wbd'] = wbd
    W['pool_b'] = inp['pool_b'][i].reshape(1, POOLW)
    W['pool_scale'] = inp['pool_scale'][i].reshape(1, POOLW)
    if 'conv_w' in G:
        W['conv_w'] = jnp.concatenate([G['conv_w'][j] for j in range(4)], axis=1)
        W['gate_b'] = jnp.concatenate([G['gate_b'][j][b:b + 1] for b in range(3) for j in range(4)], axis=1)
    W['conv_b'] = inp['conv_b'][i].reshape(1, 2048)
    W['dt_bias'] = inp['dt_bias'][i].reshape(1, 16)
    W['a_neg'] = (-jnp.exp(inp['a_log'][i])).reshape(4, 1, 4)
    W['d_skip'] = inp['d_skip'][i].reshape(4, 1, 4)
    W['ssd_norm'] = inp['ssd_norm'][i].reshape(1, D)
    for n in ('ln1_g', 'ln1_b', 'ln2_g', 'ln2_b', 'ln3_g', 'ln3_b'):
        W[n] = inp[n][i].reshape(1, D)
    return W


GATHER_FIRST = ['ffn1_w13', 'ffn1_w2']
GATHER_REST = [n for n in BIG if n not in GATHER_FIRST] + ['gate_b', 'conv_w']


def _gather_start(inp, i, names):
    core = lax.axis_index("c")
    arrs = []
    for n in names:
        s = inp[n][i]
        if n in BIG:
            s = lax.dynamic_slice_in_dim(s, core * (s.shape[0] // 2), s.shape[0] // 2, axis=0).astype(BF16)
        arrs.append(s)
    state, token = _exchange_start(arrs, "chips", "gather", name="gather_start")
    return (names, state), token


def _gather_mid(handle, after):
    names, state = handle
    me = 2 * lax.axis_index("x") + lax.axis_index("y")
    own, outs = _exchange_wait(state, after, "chips", "gather", name="gather_wait")
    outs = [lax.dynamic_update_slice(o, a[None], (me, 0, 0)) for o, a in zip(outs, own)]
    big = [o for n, o in zip(names, outs) if n in BIG]
    state, token = _exchange_start(big, "cores", "gather", name="share_start")
    return (names, outs, state), token


def _gather_finish(handle, after):
    names, outs, state = handle
    core = lax.axis_index("c")
    mine, theirs = _exchange_wait(state, after, "cores", "gather", name="share_wait")
    G = {n: o for n, o in zip(names, outs) if n not in BIG}
    for n, a, b in zip([n for n in names if n in BIG], mine, theirs):
        G[n] = jnp.concatenate([jnp.where(core == 0, a, b), jnp.where(core == 0, b, a)], axis=1)
    return G


W_IN_SHARD = 2308


def _nat_pieces(g, lo, hi):
    out = []
    for j in range(4):
        s, e = max(lo, W_IN_SHARD * j), min(hi, W_IN_SHARD * (j + 1))
        if s < e:
            out.append(g[j][:, s - W_IN_SHARD * j:e - W_IN_SHARD * j])
    return out


def _reord_ranges(lo, hi):
    out = []
    for a, b, off in ((0, 3840, 0), (3840, 3856, O_DT - 3840), (3856, 9232, -16)):
        s, e = max(lo, a), min(hi, b)
        if s < e:
            out.append((s + off, e + off))
    return out


def _halves_of(n, g):
    if n == 'w_in':
        shards = [jnp.concatenate([g[:, a:b] for a, b in _reord_ranges(W_IN_SHARD * j, W_IN_SHARD * (j + 1))], axis=1)
                  for j in range(4)]
    elif n in COL_SHARDED:
        c = g.shape[1] // 4
        shards = [g[:, j * c:(j + 1) * c] for j in range(4)]
    else:
        r = g.shape[0] // 4
        shards = [g[j * r:(j + 1) * r] for j in range(4)]
    r2 = shards[0].shape[0] // 2
    return jnp.stack([jnp.concatenate([s[h * r2:(h + 1) * r2] for s in shards], axis=0) for h in range(2)])


def _reduce_a(grads):
    names = list(grads)
    halves = [_halves_of(n, grads[n]) for n in names]
    state, token = _exchange_start(halves, "cores", "scatter", name="rsc_start")
    return (names, state), token


def _reduce_b(handle, after):
    names, state = handle
    core = lax.axis_index("c").reshape(1)
    halves, got = _exchange_wait(state, after, "cores", "scatter", name="rsc_wait")
    chip = [_sum_own_recv(h, t, core, BF16, name="rs_sum2") for h, t in zip(halves, got)]
    chip = [t.reshape(4, t.shape[0] // 4, t.shape[1]) for t in chip]
    state, token = _exchange_start(chip, "chips", "scatter", name="rs_start")
    return (names, state), token


def _reduce_c(handle, after):
    names, state = handle
    chip_id = (2 * lax.axis_index("x") + lax.axis_index("y")).reshape(1)
    chip, got = _exchange_wait(state, after, "chips", "scatter", name="rs_wait")
    red = [_sum_own_recv(h, t, chip_id, F32, name="rs_sum4") for h, t in zip(chip, got)]
    other = _exchange(red, "cores", "gather", name="rs_share")
    out = {}
    for n, mine, theirs in zip(names, red, other):
        out[n] = jnp.where(lax.axis_index("c") == 0, jnp.concatenate([mine, theirs]), jnp.concatenate([theirs, mine]))
    return out


class _Comm:
    def __init__(self, inp):
        self.inp = inp

    def gather_start(self, i, names):
        return _gather_start(self.inp, i, names)

    gather_mid = staticmethod(_gather_mid)
    gather_finish = staticmethod(_gather_finish)

    def reduce_a(self, i, grads):
        return _reduce_a({n: grads[n] for n in BIG})

    reduce_b = staticmethod(_reduce_b)
    reduce_c = staticmethod(_reduce_c)


def _allreduce_small(vec, dep=None):
    for group in ("cores", "x", "y"):
        recv = _exchange([vec], group, "gather", name=f"ar_{group}", dep=dep if group == "cores" else None)[0]
        vec = _rowwise(lambda tv, fv: ([tv[0] + tv[1]], []), [vec, recv], [], [(128, F32)], [], name=f"ar_add_{group}")[0]
    return vec


def _pack(arrs):
    flat = jnp.concatenate([a.reshape(-1) for a in arrs])
    n = flat.shape[0]
    pad = (-n) % (256 * 128)
    flat = jnp.concatenate([flat, jnp.zeros((pad,), F32)])
    return flat.reshape(-1, 128)


def _unpack(p, shapes):
    flat = p.reshape(-1)
    out, off = [], 0
    for s in shapes:
        sz = int(np.prod(s))
        out.append(flat[off:off + sz].reshape(s))
        off += sz
    return out


def _forward_backward(inp, comm, bias_all):
    x = xm = inp['x'].reshape(-1, D)
    tgt = inp['loss_target'].reshape(-1, D)
    saved, Ws = [], []
    h_first, _ = comm.gather_start(0, GATHER_FIRST)
    h_rest, dep = comm.gather_start(0, GATHER_REST)
    h_first, tok = comm.gather_mid(h_first, x)
    G = comm.gather_finish(h_first, tok)
    for i in range(NL):
        W = _prep_layer_weights(i, inp, G)
        start_next = lambda: (comm.gather_start(i + 1, BIG + ['gate_b', 'conv_w']) if i + 1 < NL else (None, None))
        if i > 0:
            h_next, dep = start_next()
        x1, x1m, s1 = _ffn_fwd(x, xm, W['ffn1_w13'], W['ffn1_w2'], W['ln1_g'], W['ln1_b'], "f1", dep)
        if i == 0:
            h_rest, tok = comm.gather_mid(h_rest, x1m)
            W.update(_prep_layer_weights(i, inp, comm.gather_finish(h_rest, tok)))
            h_next, dep = start_next()
        x2, x2m, s2 = _mixer_fwd(x1, x1m, W, bias_all, "mx", dep if i == 0 else None)
        dep = None
        if h_next is not None:
            h_next, dep = comm.gather_mid(h_next, x2m)
        x, xm, s3 = _ffn_fwd(x2, x2m, W['ffn2_w13'], W['ffn2_w2'], W['ln3_g'], W['ln3_b'], "f2", dep)
        if h_next is not None:
            G = comm.gather_finish(h_next, xm)
        saved.append((s1, s2, s3))
        Ws.append(W)
    dy, lpart = _loss_fwd_bwd(x, tgt, name="loss")
    fins, reduced, dbiases = [None] * NL, [None] * NL, [None] * NL
    pend_a, pend_b, dep = None, None, None
    for i in reversed(range(NL)):
        W = Ws[i]
        s1, s2, s3 = saved[i]
        g = {}
        dx2, f = _ffn_bwd(dy, s3, W['ffn2_w13'], W['ffn2_w2'], W['ln3_g'], W['ln3_b'], "f2", dep)
        g['ffn2_w13'], g['ffn2_w2'], g['ln3_g'], g['ln3_b'] = f['w13'], f['w2'], f['g'], f['b']
        dep = None
        if pend_a is not None:
            handle, dep = comm.reduce_b(pend_a[1], dx2)
            pend_b = (pend_a[0], handle)
        dx1, gm, dbiases[i] = _mixer_bwd(dx2, s2, W, bias_all, "mx", dep)
        g.update(gm)
        dy, f = _ffn_bwd(dx1, s1, W['ffn1_w13'], W['ffn1_w2'], W['ln1_g'], W['ln1_b'], "f1")
        g['ffn1_w13'], g['ffn1_w2'], g['ln1_g'], g['ln1_b'] = f['w13'], f['w2'], f['g'], f['b']
        fins[i] = _finish_layer_grads(i, g, inp)
        if pend_b is not None:
            reduced[pend_b[0]] = comm.reduce_c(pend_b[1], dy)
            pend_b = None
        handle, dep = comm.reduce_a(i, fins[i])
        pend_a = (i, handle)
    return lpart, dy, fins, reduced, pend_a, dbiases, dep


def _finish_layer_grads(i, g, inp):
    out = {n: g[n] for n in BIG if n != 'w_in'}
    out['w_in'] = g['w_in_r']
    out['pool_w'] = jnp.stack([g['pool_wbd'][k * POOL_GDIM:(k + 1) * POOL_GDIM, k * POOL_GDIM:(k + 1) * POOL_GDIM] for k in range(4)])
    out['pool_b'] = g['pool_b'].reshape(4, POOL_GDIM)
    out['pool_scale'] = g['pool_scale'].reshape(POOLW)
    out['conv_w'] = g['conv_w']
    out['conv_b'] = g['conv_b'].reshape(2048)
    out['dt_bias'] = g['dt_bias'].reshape(16)
    out['a_log'] = (g['a_neg'].reshape(16)) * (-jnp.exp(inp['a_log'][i]))
    out['d_skip'] = g['d_skip'].reshape(16)
    out['ssd_norm'] = g['ssd_norm'].reshape(D)
    out['gate_b'] = g['gate_b'].reshape(3, D)
    for n in ('ln1_g', 'ln1_b', 'ln2_g', 'ln2_b', 'ln3_g', 'ln3_b'):
        out[n] = g[n].reshape(D)
    return out


def kernel(x, ffn1_w13, ffn1_w2, ln1_g, ln1_b, w_in, gate_b, pool_w, pool_b, pool_scale, conv_w, conv_b,
           dt_bias, a_log, d_skip, ssd_norm, rel_bias, p_pool, p_ssd, p_attn, w_out, ln2_g, ln2_b, ffn2_w13,
           ffn2_w2, ln3_g, ln3_b, loss_target, m_ffn1_w13, m_ffn1_w2, m_ln1_g, m_ln1_b, m_w_in, m_gate_b,
           m_pool_w, m_pool_b, m_pool_scale, m_conv_w, m_conv_b, m_dt_bias, m_a_log, m_d_skip, m_ssd_norm,
           m_rel_bias, m_p_pool, m_p_ssd, m_p_attn, m_w_out, m_ln2_g, m_ln2_b, m_ffn2_w13, m_ffn2_w2, m_ln3_g,
           m_ln3_b, v_ffn1_w13, v_ffn1_w2, v_ln1_g, v_ln1_b, v_w_in, v_gate_b, v_pool_w, v_pool_b,
           v_pool_scale, v_conv_w, v_conv_b, v_dt_bias, v_a_log, v_d_skip, v_ssd_norm, v_rel_bias, v_p_pool,
           v_p_ssd, v_p_attn, v_w_out, v_ln2_g, v_ln2_b, v_ffn2_w13, v_ffn2_w2, v_ln3_g, v_ln3_b):
    inp = dict(locals())
    maps = jnp.asarray(_bucket_maps())
    bias_all = _bias_build(rel_bias, maps, name="bias_build")
    comm = _Comm(inp)
    lpart, gx, fins, red, pending, dbiases, halves_started = _forward_backward(inp, comm, bias_all)
    loss = lax.psum(lpart[0, 0], ("x", "y", "c"))

    small_l = [n for n in SMALL if n != 'rel_bias']
    drel = _bias_reduce(jnp.stack(dbiases), maps, name="bias_reduce")[:, 0, :32].T
    small_arrs = [jnp.stack([fins[i][n] for i in range(NL)]) for n in small_l] + [drel]
    packed = _allreduce_small(_pack(small_arrs), dep=halves_started)
    handle_b, started = comm.reduce_b(pending[1], packed)
    gsmall = dict(zip(small_l + ['rel_bias'], _unpack(packed, [a.shape for a in small_arrs])))
    shard = 2 * lax.axis_index("x") + lax.axis_index("y")
    gsmall['gate_b'] = lax.dynamic_slice_in_dim(gsmall['gate_b'], shard * 256, 256, axis=2)
    gsmall['conv_w'] = lax.dynamic_slice_in_dim(gsmall['conv_w'], shard * 512, 512, axis=2)
    gout, delta, new_m, new_v = dict(gsmall), {}, {}, {}
    shapes = [inp[n].shape for n in SMALL]
    d, m, v = _adamw(_pack([inp[n] for n in SMALL]), _pack([gsmall[n] for n in SMALL]),
                     _pack([inp['m_' + n] for n in SMALL]), _pack([inp['v_' + n] for n in SMALL]), name="adamw_small")
    for n, dd, mm, vv in zip(SMALL, _unpack(d, shapes), _unpack(m, shapes), _unpack(v, shapes)):
        delta[n], new_m[n], new_v[n] = dd, mm, vv

    two_d = lambda a: a.reshape(a.shape[0] * a.shape[1], a.shape[2])
    accs = {n: None for n in BIG}

    def adamw_layer(i, dep=None):
        for n in BIG:
            accs[n] = _adamw_layer(i, two_d(inp[n]), red[i][n], two_d(inp['m_' + n]), two_d(inp['v_' + n]), accs[n],
                                   name="adamw_big", dep=dep)

    done = [i for i in range(NL) if i != pending[0]]
    for i in done:
        adamw_layer(i, dep=started)
    red[pending[0]] = comm.reduce_c(handle_b, [d] + ([accs[n][1] for n in BIG] if done else []))
    adamw_layer(pending[0])
    for n in BIG:
        gout[n], delta[n], new_m[n], new_v[n] = [a.reshape(inp[n].shape) for a in accs[n]]

    return (loss, gx.reshape(x.shape), *[gout[n] for n in WEIGHTS], *[delta[n] for n in WEIGHTS],
            *[new_m[n] for n in WEIGHTS], *[new_v[n] for n in WEIGHTS])
```

```python
import functools

import numpy as np
import jax
import jax.numpy as jnp
from jax import lax
from jax.experimental import pallas as pl
from jax.experimental.pallas import tpu as pltpu

F32 = jnp.float32
BF16 = jnp.bfloat16
_MXU = jnp.bfloat16
_ACT = jnp.bfloat16
_VMEM_LIMIT = 56 * 1024 * 1024

S = 2048
D = 1024
NL = 4
DFF = 2816
LN_EPS = 1e-5
SSD_EPS = 1e-5
ALPHA = (2.0 * NL) ** 0.25
POOLW = 768
POOL_WINDOWS = (2, 4, 8, 16)
POOL_GDIM = 192
CH = 128
ATTN_DILS = (1, 4, 16)
HC = 9728
O_U, O_Z, O_XBC, O_Q, O_K, O_V, O_G, O_DT = 0, 768, 1792, 3840, 4608, 5376, 6144, 9216

ADAM_LR, ADAM_B1, ADAM_B2, ADAM_EPS, ADAM_WD, ADAM_STEP = 0.001, 0.9, 0.999, 1e-08, 0.01, 10

WEIGHTS = ['ffn1_w13', 'ffn1_w2', 'ln1_g', 'ln1_b', 'w_in', 'gate_b', 'pool_w', 'pool_b', 'pool_scale', 'conv_w',
           'conv_b', 'dt_bias', 'a_log', 'd_skip', 'ssd_norm', 'rel_bias', 'p_pool', 'p_ssd', 'p_attn', 'w_out',
           'ln2_g', 'ln2_b', 'ffn2_w13', 'ffn2_w2', 'ln3_g', 'ln3_b']
BIG = ['ffn1_w13', 'ffn1_w2', 'w_in', 'p_pool', 'p_ssd', 'p_attn', 'w_out', 'ffn2_w13', 'ffn2_w2']
COL_SHARDED = {'ffn1_w13', 'ffn2_w13', 'w_in', 'p_pool', 'p_attn'}
SMALL = [n for n in WEIGHTS if n not in BIG]


def _pcall(body, **kw):
    return pl.pallas_call(body, **kw)


def _cp(sem=None):
    return pltpu.CompilerParams(dimension_semantics=sem, vmem_limit_bytes=_VMEM_LIMIT)


def _pick(n, cands):
    for c in cands:
        if n % c == 0:
            return c
    raise ValueError(f"no tile for {n}")


def _mm(a, b, *, ta=False, tb=False, add=None, out_dtype=F32, dep=None, name):
    if ta:
        K, M = a.shape
    else:
        M, K = a.shape
    if tb:
        N, K2 = b.shape
    else:
        K2, N = b.shape
    assert K == K2, (a.shape, b.shape, ta, tb)
    sa, sb, so = a.dtype.itemsize, b.dtype.itemsize, jnp.dtype(out_dtype).itemsize
    tm, tn, tk = _mm_tiles(M, N, K, sa, sb, so + (4 if add is not None else 0))
    nk = K // tk
    a_bytes, b_bytes = M * K * sa, K * N * sb
    j_outer = nk == 1 and (b_bytes + a_bytes * (N // tn) < a_bytes + b_bytes * (M // tm))
    ij = (lambda p, q: (q, p)) if j_outer else (lambda p, q: (p, q))

    def im(f):
        return lambda p, q, k: f(*ij(p, q), k)

    a_spec = pl.BlockSpec((tk, tm), im(lambda i, j, k: (k, i))) if ta else pl.BlockSpec((tm, tk), im(lambda i, j, k: (i, k)))
    b_spec = pl.BlockSpec((tn, tk), im(lambda i, j, k: (j, k))) if tb else pl.BlockSpec((tk, tn), im(lambda i, j, k: (k, j)))
    o_spec = pl.BlockSpec((tm, tn), im(lambda i, j, k: (i, j)))
    dims = (((0 if ta else 1,), (1 if tb else 0,)), ((), ()))
    has_add = add is not None

    n_in = 2 + int(has_add) + int(dep is not None)

    def body(*refs):
        a_ref, b_ref = refs[0], refs[1]
        add_ref = refs[2] if has_add else None
        o_ref = refs[n_in]
        part = lax.dot_general(a_ref[...].astype(_MXU), b_ref[...].astype(_MXU), dims, preferred_element_type=F32)

        def finish(r):
            if has_add:
                r = r + add_ref[...]
            o_ref[...] = r.astype(out_dtype)

        if nk == 1:
            finish(part)
        else:
            acc = refs[-1]
            k = pl.program_id(2)

            @pl.when(k == 0)
            def _():
                acc[...] = part

            @pl.when(k > 0)
            def _():
                acc[...] += part

            @pl.when(k == nk - 1)
            def _():
                finish(acc[...])

    in_specs = [a_spec, b_spec]
    args = [a, b]
    if has_add:
        in_specs.append(o_spec)
        args.append(add)
    if dep is not None:
        in_specs.append(pl.BlockSpec(memory_space=pl.ANY))
        args.append(dep)
    gm, gn = M // tm, N // tn
    return _pcall(
        body, name=name, grid=((gn, gm, nk) if j_outer else (gm, gn, nk)), in_specs=in_specs, out_specs=o_spec,
        out_shape=jax.ShapeDtypeStruct((M, N), out_dtype),
        scratch_shapes=([pltpu.VMEM((tm, tn), F32)] if nk > 1 else []),
        compiler_params=_cp(("parallel", "parallel", "arbitrary")),
    )(*args)


_MM_VMEM_BUDGET = 40 * 1024 * 1024


def _divisors128(n, cap):
    return [d for d in range(128, min(n, cap) + 1, 128) if n % d == 0][::-1]


_MM_CYC_PER_MMAC = 4.35
_MM_CYC_PER_ACC_VREG = 2.03
_MM_HBM_BYTES_PER_CYC = 1455.0
_MM_CYC_PER_STEP = 770.0


def _mm_tiles(M, N, K, sa, sb, so):
    best = None
    for tm in _divisors128(M, 1408):
        for tn in _divisors128(N, 2560):
            for tk in ([K] if K <= 4096 else []) + _divisors128(K, 2816):
                nk = K // tk
                need = 2 * (tm * tk * sa + tk * tn * sb + tm * tn * so) + (tm * tn * 4 if nk > 1 else 0)
                need += tm * tk * 2 + tk * tn * 2 + tm * tn * 4
                if need > _MM_VMEM_BUDGET:
                    continue
                gm, gn = M // tm, N // tn
                a_bytes, b_bytes = M * K * sa, K * N * sb
                hbm = min(b_bytes + a_bytes * gn, a_bytes + b_bytes * gm) if nk == 1 else a_bytes * gn + b_bytes * gm
                hbm += M * N * so
                work = _MM_CYC_PER_MMAC * M * N * K / 1e6 + _MM_CYC_PER_ACC_VREG * (M * N / 1024) * (nk if nk > 1 else 0.5)
                cost = max(work, hbm / _MM_HBM_BYTES_PER_CYC) + gm * gn * nk * _MM_CYC_PER_STEP
                if best is None or cost < best[0]:
                    best = (cost, (tm, tn, tk))
    assert best is not None, (M, N, K)
    return best[1]


def _store(ref, val):
    if isinstance(val, (list, tuple)):
        off = 0
        for p in val:
            w = p.shape[1]
            ref[:, off:off + w] = p.astype(ref.dtype)
            off += w
    else:
        ref[...] = val.astype(ref.dtype)


def _acc_store(ref, val, first):
    pieces = val if isinstance(val, (list, tuple)) else [val]
    off = 0
    for p in pieces:
        w = p.shape[1]

        @pl.when(first)
        def _(p=p, off=off, w=w):
            ref[:, off:off + w] = p

        @pl.when(jnp.logical_not(first))
        def _(p=p, off=off, w=w):
            ref[:, off:off + w] += p

        off += w


def _rowwise(fn, tiled, full, out_tiled, out_acc, *, name, tm=256):
    arrs, specs = [], []
    for t in tiled:
        arr, w, cb = t if isinstance(t, tuple) else (t, t.shape[1], 0)
        arrs.append(arr)
        specs.append(pl.BlockSpec((tm, w), functools.partial(lambda i, cb: (i, cb), cb=cb)))
    R = arrs[0].shape[0]
    assert R % tm == 0
    for f in full:
        arrs.append(f)
        specs.append(pl.BlockSpec(f.shape, functools.partial(lambda i, nd: (0,) * nd, nd=f.ndim)))
    nt, nf, no = len(tiled), len(full), len(out_tiled)

    def body(*refs):
        tv = [r[...] for r in refs[:nt]]
        fv = [r[...] for r in refs[nt:nt + nf]]
        ot, oa = fn(tv, fv)
        for r, v in zip(refs[nt + nf:nt + nf + no], ot):
            _store(r, v)
        first = pl.program_id(0) == 0
        for r, v in zip(refs[nt + nf + no:], oa):
            _acc_store(r, v, first)

    out_shape = [jax.ShapeDtypeStruct((R, c), dt) for c, dt in out_tiled]
    out_specs = [pl.BlockSpec((tm, c), lambda i: (i, 0)) for c, _ in out_tiled]
    for shp in out_acc:
        out_shape.append(jax.ShapeDtypeStruct(shp, F32))
        out_specs.append(pl.BlockSpec(shp, lambda i: (0, 0)))
    return _pcall(body, name=name, grid=(R // tm,), in_specs=specs, out_specs=out_specs, out_shape=out_shape,
                  compiler_params=_cp(("arbitrary",)))(*arrs)


def _group(group):
    x, y, c = lax.axis_index("x"), lax.axis_index("y"), lax.axis_index("c")
    if group == "chips":
        return 2 * x + y, [((x, 1 - y, c), 2 * x + 1 - y), ((1 - x, y, c), 2 * (1 - x) + y),
                           ((1 - x, 1 - y, c), 2 * (1 - x) + 1 - y)]
    if group == "cores":
        return c, [((x, y, 1 - c), 1 - c)]
    if group == "x":
        return x, [((1 - x, y, c), 1 - x)]
    return y, [((x, 1 - y, c), 1 - y)]


def _exchange(arrs, group, mode, name, dep=None):
    chips = group == "chips"
    k = len(arrs)
    npeer = 3 if chips else 1

    def body(*refs):
        nd = 0 if dep is None else 1
        ins, outs = refs[:k], refs[k + nd:2 * k + nd]
        send_sems, recv_sems = refs[2 * k + nd:]
        me, peers = _group(group)
        remote = []
        for i in range(k):
            for p, (dev, slot) in enumerate(peers):
                src = ins[i].at[slot] if mode == "scatter" else ins[i]
                if not chips:
                    dst = outs[i]
                else:
                    dst = outs[i].at[p] if mode == "scatter" else outs[i].at[me]
                cp = pltpu.make_async_remote_copy(src_ref=src, dst_ref=dst, send_sem=send_sems.at[i, p],
                                                  recv_sem=recv_sems.at[i, p], device_id=dev,
                                                  device_id_type=pl.DeviceIdType.MESH)
                cp.start()
                remote.append(cp)
        for cp in remote:
            cp.wait_recv()
        for cp in remote:
            cp.wait_send()

    def oshape(a):
        piece = a.shape[1:] if mode == "scatter" else a.shape
        if chips:
            piece = ((3,) if mode == "scatter" else (4,)) + piece
        return jax.ShapeDtypeStruct(piece, a.dtype)

    any_spec = pl.BlockSpec(memory_space=pl.ANY)
    extra = [] if dep is None else [dep]
    return _pcall(body, name=name, in_specs=[any_spec] * (k + len(extra)), out_specs=[any_spec] * k,
                  out_shape=[oshape(a) for a in arrs],
                  scratch_shapes=[pltpu.SemaphoreType.DMA((k, npeer)), pltpu.SemaphoreType.DMA((k, npeer))])(*arrs, *extra)


def _split_copies(ins, lands, send_sems, recv_sems, group, mode):
    chips = group == "chips"
    me, peers = _group(group)
    npeer = len(peers)
    out = []
    for i in range(len(ins)):
        for p, (dev, slot) in enumerate(peers):
            src = ins[i].at[slot] if mode == "scatter" else ins[i]
            if not chips:
                dst = lands[i]
            else:
                dst = lands[i].at[p] if mode == "scatter" else lands[i].at[me]
            out.append(pltpu.make_async_remote_copy(src_ref=src, dst_ref=dst, send_sem=send_sems.at[npeer * i + p],
                                                    recv_sem=recv_sems.at[npeer * i + p], device_id=dev,
                                                    device_id_type=pl.DeviceIdType.MESH))
    return out


def _exchange_start(arrs, group, mode, name):
    k = len(arrs)
    chips = group == "chips"
    nsem = (3 if chips else 1) * k
    hbm = pl.BlockSpec(memory_space=pltpu.HBM)
    sem = pl.BlockSpec(memory_space=pltpu.SEMAPHORE)

    def land_shape(a):
        piece = a.shape[1:] if mode == "scatter" else a.shape
        if chips:
            piece = ((3,) if mode == "scatter" else (4,)) + piece
        return piece

    def body(*refs):
        ins, lands = refs[:k], refs[k:2 * k]
        send_sems, recv_sems = refs[2 * k], refs[2 * k + 1]
        token = refs[-1]
        for cp in _split_copies(ins, lands, send_sems, recv_sems, group, mode):
            cp.start()
        token[...] = jnp.zeros_like(token)

    srcs = [pltpu.with_memory_space_constraint(a, pltpu.HBM) for a in arrs]
    lands = [pltpu.with_memory_space_constraint(lax.empty(land_shape(a), a.dtype), pltpu.HBM) for a in arrs]
    out_shape = ([pltpu.SemaphoreType.DMA((nsem,)), pltpu.SemaphoreType.DMA((nsem,))]
                 + [pltpu.HBM(a.shape, a.dtype) for a in arrs] + [pltpu.HBM(land_shape(a), a.dtype) for a in arrs]
                 + [jax.ShapeDtypeStruct((8, 128), F32)])
    outs = _pcall(body, name=name, in_specs=[hbm] * (2 * k),
                  out_specs=[sem, sem] + [hbm] * (2 * k) + [pl.BlockSpec(memory_space=pltpu.VMEM)], out_shape=out_shape,
                  input_output_aliases={i: 2 + i for i in range(2 * k)},
                  compiler_params=pltpu.CompilerParams(has_side_effects=pltpu.SideEffectType.DATAFLOW_SIDE_EFFECTING))(
                      *srcs, *lands)
    return (outs[0], outs[1], list(outs[2:2 + k]), list(outs[2 + k:2 + 2 * k])), outs[-1]


def _exchange_wait(state, after, group, mode, name):
    send_sems, recv_sems, srcs, lands = state
    k = len(srcs)
    after = list(after) if isinstance(after, (list, tuple)) else [after]
    hbm = pl.BlockSpec(memory_space=pltpu.HBM)
    sem = pl.BlockSpec(memory_space=pltpu.SEMAPHORE)

    def body(*refs):
        ins, lnd = refs[:k], refs[k:2 * k]
        send_sems, recv_sems = refs[2 * k], refs[2 * k + 1]
        for cp in _split_copies(ins, lnd, send_sems, recv_sems, group, mode):
            cp.wait_send()
            cp.wait_recv()

    outs = _pcall(body, name=name,
                  in_specs=[hbm] * (2 * k) + [sem, sem] + [pl.BlockSpec(memory_space=pl.ANY)] * len(after),
                  out_specs=[hbm] * (2 * k),
                  out_shape=[pltpu.HBM(a.shape, a.dtype) for a in srcs] + [pltpu.HBM(a.shape, a.dtype) for a in lands],
                  input_output_aliases={i: i for i in range(2 * k)},
                  compiler_params=pltpu.CompilerParams(has_side_effects=pltpu.SideEffectType.DATAFLOW_SIDE_EFFECTING))(
                      *srcs, *lands, send_sems, recv_sems, *after)
    return list(outs[:k]), list(outs[k:])


def _sum_own_recv(own, recv, me, out_dtype, name):
    n, R, C = own.shape
    nr = 1 if recv.ndim == 2 else recv.shape[0]
    tr = _pick(R, (256, 128, 64, 32, 16, 8))

    def body(me_ref, own_ref, *refs):
        o_ref = refs[-1]
        acc = own_ref[...].astype(F32)
        for r in refs[:-1]:
            acc = acc + r[...].astype(F32)
        o_ref[...] = acc.astype(out_dtype)

    specs = [pl.BlockSpec((None, tr, C), lambda i, me_ref: (me_ref[0], i, 0))]
    args = [own]
    if recv.ndim == 2:
        specs.append(pl.BlockSpec((tr, C), lambda i, me_ref: (i, 0)))
        args.append(recv)
    else:
        for p in range(nr):
            specs.append(pl.BlockSpec((None, tr, C), functools.partial(lambda i, me_ref, p: (p, i, 0), p=p)))
            args.append(recv)
    gs = pltpu.PrefetchScalarGridSpec(num_scalar_prefetch=1, grid=(R // tr,), in_specs=specs,
                                      out_specs=pl.BlockSpec((tr, C), lambda i, me_ref: (i, 0)))
    return _pcall(body, name=name, grid_spec=gs, out_shape=jax.ShapeDtypeStruct((R, C), out_dtype),
                  compiler_params=_cp(("parallel",)))(me, *args)


def _silu(x):
    return x * jax.nn.sigmoid(x)


def _ln(r, g, b):
    mu = jnp.mean(r, -1, keepdims=True)
    xc = r - mu
    var = jnp.mean(xc * xc, -1, keepdims=True)
    return xc * lax.rsqrt(var + LN_EPS) * g + b


def _softplus(x):
    return jnp.maximum(x, 0.0) + jnp.log1p(jnp.exp(-jnp.abs(x)))


RES_LN_TM = 512


def _mm_res_ln(a, w, x, g, b, res, name):
    T, K = a.shape
    tm = RES_LN_TM

    def body(a_ref, w_ref, x_ref, g_ref, b_ref, r_ref, o_ref, om_ref):
        y = jnp.dot(a_ref[...].astype(_MXU), w_ref[...].astype(_MXU), preferred_element_type=F32)
        r = ALPHA * x_ref[...] + res * y
        out = _ln(r, g_ref[...], b_ref[...])
        r_ref[...] = r
        o_ref[...] = out
        om_ref[...] = out.astype(om_ref.dtype)

    row = pl.BlockSpec((tm, D), lambda i: (i, 0))
    vec = pl.BlockSpec((1, D), lambda i: (0, 0))
    return _pcall(body, name=name, grid=(T // tm,),
                  in_specs=[pl.BlockSpec((tm, K), lambda i: (i, 0)), pl.BlockSpec((K, D), lambda i: (0, 0)), row, vec, vec],
                  out_specs=[row, row, row],
                  out_shape=[jax.ShapeDtypeStruct((T, D), F32), jax.ShapeDtypeStruct((T, D), F32),
                             jax.ShapeDtypeStruct((T, D), _ACT)],
                  compiler_params=_cp(("parallel",)))(a, w, x, g, b)


def _ln_bwd(r, g, b, dout, res, name):
    def fn(tv, fv):
        _, vjp = jax.vjp(_ln, tv[0], fv[0], fv[1])
        dr, dg, db = vjp(tv[1])
        return [ALPHA * dr, res * dr], [dg, db]
    return _rowwise(fn, [r, dout], [g, b], [(D, F32), (D, _ACT)], [(1, D), (1, D)], name=name)


SWIGLU_TM, SWIGLU_TN = 512, 1408


def _swiglu_fwd(x, w13, dep, name):
    T, K = x.shape
    tm, tn = SWIGLU_TM, SWIGLU_TN
    nj = DFF // tn
    has_dep = dep is not None

    def body(x_ref, wa_ref, wg_ref, *rest):
        a_ref, g_ref, s_ref = rest[-3:]
        xv = x_ref[...].astype(_MXU)
        a = jnp.dot(xv, wa_ref[...].astype(_MXU), preferred_element_type=F32)
        g = jnp.dot(xv, wg_ref[...].astype(_MXU), preferred_element_type=F32)
        a_ref[...] = a
        g_ref[...] = g
        s_ref[...] = (_silu(a) * g).astype(s_ref.dtype)

    out = pl.BlockSpec((tm, tn), lambda j, i: (i, j))
    in_specs = [pl.BlockSpec((tm, K), lambda j, i: (i, 0)), pl.BlockSpec((K, tn), lambda j, i: (0, j)),
                pl.BlockSpec((K, tn), lambda j, i: (0, nj + j))]
    args = [x, w13, w13]
    if has_dep:
        in_specs.append(pl.BlockSpec(memory_space=pl.ANY))
        args.append(dep)
    return _pcall(body, name=name, grid=(nj, T // tm), in_specs=in_specs, out_specs=[out, out, out],
                  out_shape=[jax.ShapeDtypeStruct((T, DFF), F32), jax.ShapeDtypeStruct((T, DFF), F32),
                             jax.ShapeDtypeStruct((T, DFF), _ACT)],
                  compiler_params=_cp(("parallel", "parallel")))(*args)


def _swiglu_act_bwd(a, g, ds, name):
    def fn(tv, fv):
        s, vjp = jax.vjp(lambda a, g: _silu(a) * g, tv[0], tv[1])
        da, dg = vjp(tv[2])
        return [[da, dg], s], []
    return _rowwise(fn, [a, g, ds], [], [(2 * DFF, _ACT), (DFF, _ACT)], [], name=name)


def _loss_fwd_bwd(y, tgt, name):
    def fn(tv, fv):
        e = tv[0] - tv[1]
        row = jnp.sum(e * e, axis=1, keepdims=True)
        tot = jnp.sum(row, axis=0, keepdims=True) * (0.5 / D)
        return [e * (1.0 / D)], [jnp.broadcast_to(tot, (1, 128))]
    return _rowwise(fn, [y, tgt], [], [(D, F32)], [(1, 128)], name=name)


def _shift_down(x, k, row):
    return jnp.where(row >= k, pltpu.roll(x, k, axis=0), 0.0)


def _shift_up(x, k, row):
    n = x.shape[0]
    return jnp.where(row < n - k, pltpu.roll(x, n - k, axis=0), 0.0)


def _pool_window_masks(j):
    lane = lax.broadcasted_iota(jnp.int32, (1, 128), 1) + j * 128
    grp = lane // POOL_GDIM
    return [grp == g for g in range(4)]


def _pool_mean(u, bwd, name, col0=0):
    T = u.shape[0]
    B = T // S

    def body(u_ref, o_ref):
        j = pl.program_id(1)
        x = u_ref[...]
        row = lax.broadcasted_iota(jnp.int32, (S, 1), 0)
        masks = _pool_window_masks(j)
        inv = [1.0 / jnp.minimum(row + 1, w).astype(F32) for w in POOL_WINDOWS]
        if not bwd:
            s2 = x + _shift_down(x, 1, row)
            s4 = s2 + _shift_down(s2, 2, row)
            s8 = s4 + _shift_down(s4, 4, row)
            s16 = s8 + _shift_down(s8, 8, row)
            mean = jnp.where(masks[0], s2 * inv[0], jnp.where(masks[1], s4 * inv[1],
                             jnp.where(masks[2], s8 * inv[2], s16 * inv[3])))
            o_ref[...] = (mean - x).astype(o_ref.dtype)
        else:
            g = [jnp.where(masks[i], x * inv[i], 0.0) for i in range(4)]
            t = g[3]
            t = t + _shift_up(t, 8, row) + g[2]
            t = t + _shift_up(t, 4, row) + g[1]
            t = t + _shift_up(t, 2, row) + g[0]
            t = t + _shift_up(t, 1, row)
            o_ref[...] = (t - x).astype(o_ref.dtype)

    spec = pl.BlockSpec((S, 128), lambda b, j: (b, j))
    return _pcall(body, name=name, grid=(B, POOLW // 128),
                  in_specs=[pl.BlockSpec((S, 128), lambda b, j: (b, j + col0))], out_specs=spec,
                  out_shape=jax.ShapeDtypeStruct((T, POOLW), _ACT), compiler_params=_cp(("parallel", "parallel")))(u)


def _conv_silu(xbc, w, b, name, col0=0):
    T, C = xbc.shape[0], w.shape[1]
    B = T // S

    def body(x_ref, w_ref, b_ref, o_ref):
        x = x_ref[...]
        row = lax.broadcasted_iota(jnp.int32, (S, 1), 0)
        c = b_ref[...] + w_ref[3:4, :] * x
        for s in range(1, 4):
            c = c + w_ref[3 - s:4 - s, :] * _shift_down(x, s, row)
        o_ref[...] = _silu(c)

    return _pcall(body, name=name, grid=(B, C // 128),
                  in_specs=[pl.BlockSpec((S, 128), lambda b, j: (b, j + col0)), pl.BlockSpec((4, 128), lambda b, j: (0, j)),
                            pl.BlockSpec((1, 128), lambda b, j: (0, j))],
                  out_specs=pl.BlockSpec((S, 128), lambda b, j: (b, j)),
                  out_shape=jax.ShapeDtypeStruct((T, C), F32), compiler_params=_cp(("parallel", "parallel")))(xbc, w, b)


def _conv_silu_bwd(xbc, w, b, dact, name, col0=0):
    T, C = xbc.shape[0], w.shape[1]
    B = T // S

    def body(x_ref, w_ref, b_ref, d_ref, dx_ref, dw_ref, db_ref):
        bi = pl.program_id(1)
        x = x_ref[...]
        row = lax.broadcasted_iota(jnp.int32, (S, 1), 0)
        xs = [x] + [_shift_down(x, s, row) for s in range(1, 4)]
        c = b_ref[...]
        for s in range(4):
            c = c + w_ref[3 - s:4 - s, :] * xs[s]
        _, vjp = jax.vjp(_silu, c)
        dc = vjp(d_ref[...])[0]
        dx = w_ref[3:4, :] * dc
        for s in range(1, 4):
            dx = dx + w_ref[3 - s:4 - s, :] * _shift_up(dc, s, row)
        dx_ref[...] = dx.astype(dx_ref.dtype)
        first = bi == 0
        for s in range(4):
            _acc_rows(dw_ref, 3 - s, jnp.sum(dc * xs[s], axis=0, keepdims=True), first)
        _acc_rows(db_ref, 0, jnp.sum(dc, axis=0, keepdims=True), first)

    blk = pl.BlockSpec((S, 128), lambda j, b: (b, j))
    return _pcall(body, name=name, grid=(C // 128, B),
                  in_specs=[pl.BlockSpec((S, 128), lambda j, b: (b, j + col0)), pl.BlockSpec((4, 128), lambda j, b: (0, j)),
                            pl.BlockSpec((1, 128), lambda j, b: (0, j)), blk],
                  out_specs=[blk, pl.BlockSpec((4, 128), lambda j, b: (0, j)), pl.BlockSpec((1, 128), lambda j, b: (0, j))],
                  out_shape=[jax.ShapeDtypeStruct((T, C), _ACT), jax.ShapeDtypeStruct((4, C), F32),
                             jax.ShapeDtypeStruct((1, C), F32)],
                  compiler_params=_cp(("parallel", "arbitrary")))(xbc, w, b, dact)


def _acc_rows(ref, r, val, first):
    @pl.when(first)
    def _():
        ref[r:r + 1, :] = val

    @pl.when(jnp.logical_not(first))
    def _():
        ref[r:r + 1, :] += val


def _tri_consts():
    i = lax.broadcasted_iota(jnp.int32, (CH, CH), 0)
    j = lax.broadcasted_iota(jnp.int32, (CH, CH), 1)
    return (i == j).astype(F32), (j <= i).astype(F32), (i <= j).astype(F32), i >= j


def _ssd_chunk(h, x, dt, Bm, Cm, a, dsk, consts):
    eye, tril, triu, lower = consts
    first_head = lax.broadcasted_iota(jnp.int32, (1, 128), 1) < 64
    sel = lambda u, v: jnp.where(first_head, u, v)
    Bb = Bm.astype(_MXU)
    Cb = Cm.astype(_MXU)
    cb = lax.dot_general(Cb, Bb, (((1,), (1,)), ((), ())), preferred_element_type=F32)
    ys, hn = [], []
    for p in range(2):
        dtm, csm, cs_last, decay, bdec = [], [], [], [], []
        for e in (2 * p, 2 * p + 1):
            d = jnp.broadcast_to(dt[e], (CH, CH))
            adm = d * a[e]
            adt_row = jnp.sum(adm * eye, axis=0, keepdims=True)
            cs_col = jnp.sum(adt_row * tril, axis=1, keepdims=True)
            cs_row = jnp.sum(adm * triu, axis=0, keepdims=True)
            last = jnp.sum(adt_row, axis=1, keepdims=True)
            c = jnp.broadcast_to(cs_col, (CH, CH))
            dtm.append(d)
            csm.append(c)
            cs_last.append(last)
            decay.append((cb * jnp.exp(jnp.where(lower, c - cs_row, -jnp.inf))).astype(_MXU))
            bdec.append((Bm * jnp.exp(last - c)).astype(_MXU))
        xb = (x[p] * sel(dtm[0], dtm[1])).astype(_MXU)
        y_diag = sel(jnp.dot(decay[0], xb, preferred_element_type=F32), jnp.dot(decay[1], xb, preferred_element_type=F32))
        states = [lax.dot_general(b, xb, (((0,), (0,)), ((), ())), preferred_element_type=F32) for b in bdec]
        hn.append(h[p] * sel(jnp.exp(cs_last[0]), jnp.exp(cs_last[1])) + sel(states[0], states[1]))
        y_off = jnp.exp(sel(csm[0], csm[1])) * jnp.dot(Cb, h[p].astype(_MXU), preferred_element_type=F32)
        ys.append(y_diag + y_off + sel(dsk[2 * p], dsk[2 * p + 1]) * x[p])
    return ys, hn


def _ssd_specs(order):
    def im(f):
        return lambda p, q: f(*order(p, q))
    xs = pl.BlockSpec((S, 256), im(lambda b, g: (b, g)))
    dt = pl.BlockSpec((None, S, 4), im(lambda b, g: (g, b, 0)))
    bc = pl.BlockSpec((S, 128), im(lambda b, g: (b, g)))
    hd = pl.BlockSpec((None, 1, 4), im(lambda b, g: (g, 0, 0)))
    hs = pl.BlockSpec((None, None, S // CH, 2, 128, 128), im(lambda b, g: (b, g, 0, 0, 0, 0)))
    bw = pl.BlockSpec((S, 128), im(lambda b, g: (b, 8 + g)))
    cw = pl.BlockSpec((S, 128), im(lambda b, g: (b, 12 + g)))
    return xs, dt, bc, hd, hs, bw, cw


def _ssd_fwd(act, dtg, a, dsk, name):
    xs = bm = cm = act
    T = xs.shape[0]
    B = T // S
    nc = S // CH

    def body(x_ref, dt_ref, b_ref, c_ref, a_ref, k_ref, y_ref, hs_ref, h_ref):
        consts = _tri_consts()
        h_ref[...] = jnp.zeros_like(h_ref)
        al = [a_ref[:, e:e + 1] for e in range(4)]
        kl = [k_ref[:, e:e + 1] for e in range(4)]

        def step(c, carry):
            r0 = pl.multiple_of(c * CH, CH)
            rows = pl.ds(r0, CH)
            h = [h_ref[p] for p in range(2)]
            for p in range(2):
                hs_ref[c, p] = h[p]
            x = [x_ref[rows, 128 * p:128 * p + 128] for p in range(2)]
            dt = [dt_ref[rows, e:e + 1] for e in range(4)]
            ys, hn = _ssd_chunk(h, x, dt, b_ref[rows, :], c_ref[rows, :], al, kl, consts)
            for p in range(2):
                y_ref[rows, 128 * p:128 * p + 128] = ys[p]
                h_ref[p] = hn[p]
            return carry

        lax.fori_loop(0, nc, step, 0)

    sx, sdt, sbc, shd, shs, sbw, scw = _ssd_specs(lambda b, g: (b, g))
    return _pcall(body, name=name, grid=(B, 4), in_specs=[sx, sdt, sbw, scw, shd, shd], out_specs=[sx, shs],
                  out_shape=[jax.ShapeDtypeStruct((T, 1024), F32), jax.ShapeDtypeStruct((B, 4, nc, 2, 128, 128), F32)],
                  scratch_shapes=[pltpu.VMEM((2, 128, 128), F32)],
                  compiler_params=_cp(("parallel", "parallel")))(xs, dtg, bm, cm, a, dsk)


def _lane_place(vals, width):
    lane = lax.broadcasted_iota(jnp.int32, (1, width), 1)
    out = jnp.zeros((1, width), F32)
    for e, v in enumerate(vals):
        out = out + jnp.where(lane == e, v, 0.0)
    return out


def _ssd_bwd(act, dtg, a, dsk, hs, dy, name):
    xs = bm = cm = act
    T = xs.shape[0]
    B = T // S
    nc = S // CH

    def body(x_ref, dt_ref, b_ref, c_ref, a_ref, k_ref, hs_ref, dy_ref,
             dx_ref, ddt_ref, db_ref, dc_ref, dak_ref, dh_ref, sc_ref):
        bi = pl.program_id(1)
        consts = _tri_consts()
        dh_ref[...] = jnp.zeros_like(dh_ref)
        sc_ref[...] = jnp.zeros_like(sc_ref)
        al = [a_ref[:, e:e + 1] for e in range(4)]
        kl = [k_ref[:, e:e + 1] for e in range(4)]

        def step(i, carry):
            c = nc - 1 - i
            r0 = pl.multiple_of(c * CH, CH)
            rows = pl.ds(r0, CH)
            h = [hs_ref[c, p] for p in range(2)]
            x = [x_ref[rows, 128 * p:128 * p + 128] for p in range(2)]
            dt = [dt_ref[rows, e:e + 1] for e in range(4)]
            f = functools.partial(_ssd_chunk, consts=consts)
            _, vjp = jax.vjp(f, h, x, dt, b_ref[rows, :], c_ref[rows, :], al, kl)
            dys = [dy_ref[rows, 128 * p:128 * p + 128] for p in range(2)]
            dhn = [dh_ref[p] for p in range(2)]
            dh, dx, ddt, dB, dC, da, dk = vjp((dys, dhn))
            for p in range(2):
                dh_ref[p] = dh[p]
                dx_ref[rows, 128 * p:128 * p + 128] = dx[p]
            for e in range(4):
                ddt_ref[rows, e:e + 1] = ddt[e]
            db_ref[rows, :] = dB
            dc_ref[rows, :] = dC
            sc_ref[0:1, :] += _lane_place(da, 128)
            sc_ref[1:2, :] += _lane_place(dk, 128)
            return carry

        lax.fori_loop(0, nc, step, 0)
        first = bi == 0

        @pl.when(first)
        def _():
            dak_ref[...] = sc_ref[...]

        @pl.when(jnp.logical_not(first))
        def _():
            dak_ref[...] += sc_ref[...]

    sx, sdt, sbc, shd, shs, sbw, scw = _ssd_specs(lambda g, b: (b, g))
    return _pcall(body, name=name, grid=(4, B), in_specs=[sx, sdt, sbw, scw, shd, shd, shs, sx],
                  out_specs=[sx, sdt, sbc, sbc, pl.BlockSpec((None, 8, 128), lambda g, b: (g, 0, 0))],
                  out_shape=[jax.ShapeDtypeStruct((T, 1024), F32), jax.ShapeDtypeStruct((4, T, 4), F32),
                             jax.ShapeDtypeStruct((T, 512), F32), jax.ShapeDtypeStruct((T, 512), F32),
                             jax.ShapeDtypeStruct((4, 8, 128), F32)],
                  scratch_shapes=[pltpu.VMEM((2, 128, 128), F32), pltpu.VMEM((8, 128), F32)],
                  compiler_params=_cp(("parallel", "arbitrary")))(xs, dtg, bm, cm, a, dsk, hs, dy)


def _gate_norm(y, z, nw):
    t = y * _silu(z)
    return t * lax.rsqrt(jnp.mean(t * t, axis=-1, keepdims=True) + SSD_EPS) * nw


def _ssd_gate_norm(y, z, nw, name, zcol=0):
    def fn(tv, fv):
        return [[_gate_norm(tv[g], tv[4 + g], fv[0][:, 256 * g:256 * g + 256]) for g in range(4)]], []
    tiled = [(y, 256, g) for g in range(4)] + [(z, 256, zcol + g) for g in range(4)]
    return _rowwise(fn, tiled, [nw], [(1024, _ACT)], [], name=name)[0]


def _ssd_gate_norm_bwd(y, z, nw, dout, name, zcol=0):
    def fn(tv, fv):
        dys, dzs, dns = [], [], []
        for g in range(4):
            _, vjp = jax.vjp(_gate_norm, tv[g], tv[4 + g], fv[0][:, 256 * g:256 * g + 256])
            a, b, c = vjp(tv[8 + g])
            dys.append(a)
            dzs.append(b)
            dns.append(c)
        return [dys, dzs], [dns]
    tiled = [(y, 256, g) for g in range(4)] + [(z, 256, zcol + g) for g in range(4)] + [(dout, 256, g) for g in range(4)]
    return _rowwise(fn, tiled, [nw], [(1024, F32), (1024, _ACT)], [(1, 1024)], name=name)


def _t5_bucket_np(dist):
    dist = np.maximum(dist, 0)
    max_exact = 16
    large = max_exact + (np.log(np.maximum(dist, 1) / max_exact) / np.log(2048 / max_exact) * (32 - max_exact)).astype(np.int32)
    large = np.minimum(large, 31)
    return np.where(dist < max_exact, dist, large).astype(np.int32)


def _bucket_maps():
    qi = np.arange(128)[:, None]
    kj = np.arange(256)[None, :]
    return np.stack([_t5_bucket_np((qi - kj + 128) * dil) for dil in ATTN_DILS]).astype(np.int32)


def _bias_build(rel_bias, maps, name):
    def body(tab_ref, map_ref, o_ref):
        hh = pl.program_id(0)
        m = map_ref[...]
        acc = jnp.zeros((128, 256), F32)
        for b in range(32):
            acc = jnp.where(m == b, tab_ref[b, hh], acc)
        o_ref[...] = acc

    return _pcall(body, name=name, grid=(12,),
                  in_specs=[pl.BlockSpec(memory_space=pltpu.SMEM), pl.BlockSpec((None, 128, 256), lambda h: (h // 4, 0, 0))],
                  out_specs=pl.BlockSpec((None, 128, 256), lambda h: (h, 0, 0)),
                  out_shape=jax.ShapeDtypeStruct((12, 128, 256), F32), compiler_params=_cp(("parallel",)))(rel_bias, maps)


def _bias_reduce(dbias, maps, name):
    nl = dbias.shape[0]

    def body(d_ref, map_ref, o_ref):
        m = map_ref[...]
        d = d_ref[0]
        for i in range(1, nl):
            d = d + d_ref[i]
        lane = lax.broadcasted_iota(jnp.int32, (1, 128), 1)
        out = jnp.zeros((1, 128), F32)
        for b in range(32):
            s = jnp.sum(jnp.sum(jnp.where(m == b, d, 0.0), axis=1, keepdims=True), axis=0, keepdims=True)
            out = out + jnp.where(lane == b, s, 0.0)
        o_ref[...] = out

    return _pcall(body, name=name, grid=(12,),
                  in_specs=[pl.BlockSpec((nl, None, 128, 256), lambda h: (0, h, 0, 0)),
                            pl.BlockSpec((None, 128, 256), lambda h: (h // 4, 0, 0))],
                  out_specs=pl.BlockSpec((None, 1, 128), lambda h: (h, 0, 0)),
                  out_shape=jax.ShapeDtypeStruct((12, 1, 128), F32), compiler_params=_cp(("parallel",)))(dbias, maps)


def _attn_block(q, kb, vb, bias, mask):
    s = lax.dot_general(q.astype(_MXU), kb.astype(_MXU), (((1,), (1,)), ((), ())), preferred_element_type=F32) * 0.125 + bias
    s = jnp.where(mask, s, -jnp.inf)
    m = lax.stop_gradient(jnp.max(s, axis=-1, keepdims=True))
    p = jnp.exp(s - m)
    den = jnp.sum(p, axis=-1, keepdims=True)
    out = jnp.dot((p / den).astype(_MXU), vb.astype(_MXU), preferred_element_type=F32)
    return out, m + jnp.log(den)


ATTN_QB = 512


def _attn_masks(dil):
    qi = lax.broadcasted_iota(jnp.int32, (ATTN_QB, ATTN_QB + 128), 0)
    kj = lax.broadcasted_iota(jnp.int32, (ATTN_QB, ATTN_QB + 128), 1)
    band = (kj >= qi) & (kj <= qi + 128)
    if dil == 16:
        q2 = lax.broadcasted_iota(jnp.int32, (ATTN_QB, ATTN_QB), 0)
        k2 = lax.broadcasted_iota(jnp.int32, (ATTN_QB, ATTN_QB), 1)
        return ((q2 // 128) == (k2 // 128)) & (k2 <= q2), None
    return band[:, 128:], band


def _attn_wide_bias(b, dil):
    if dil == 16:
        return jnp.tile(b[:, 128:], (4, 4)), None
    z = jnp.zeros((128, 128), F32)
    band = jnp.concatenate([jnp.concatenate([z] * i + [b] + [z] * (3 - i), axis=1) for i in range(4)], axis=0)
    return band[:, 128:], band


def _fold_dbias(dbs, dil, band_form):
    def blk(i, j):
        return dbs[128 * i:128 * i + 128, 128 * j:128 * j + 128]
    if band_form:
        return sum(blk(i, i) for i in range(4)), sum(blk(i, i + 1) for i in range(4))
    cur = sum(blk(i, i) for i in range(4))
    if dil == 16:
        return None, cur
    return sum(blk(i, i - 1) for i in range(1, 4)), cur


def _attn_chunks(dil):
    out = []
    for n in range(S // ATTN_QB):
        if dil == 1 and n > 0:
            out.append((n * ATTN_QB, n * ATTN_QB - 128, ATTN_QB + 128, True))
        else:
            out.append((n * ATTN_QB, n * ATTN_QB, ATTN_QB, False))
    return out


def _qkv_specs(gi, order):
    def spec(base):
        col = (base + 256 * gi) // 128
        return pl.BlockSpec((S, 128), lambda p, q: (order(p, q)[0], col + order(p, q)[1]))
    return [spec(O_Q), spec(O_K), spec(O_V)]


def _residue_rows(r, dil):
    return pl.ds(r, S // dil, stride=dil)


def _attn_fwd(hcat, bias_all, gi, name):
    dil = ATTN_DILS[gi]
    T = hcat.shape[0]
    B, L = T // S, S // dil

    def body(q_ref, k_ref, v_ref, b_ref, o_ref, l_ref, *scr):
        mask_first, mask_band = _attn_masks(dil)
        if dil > 1:
            qs, ks, vs, os_, ls = scr
            for r in range(dil):
                rows, dst = _residue_rows(r, dil), pl.ds(r * L, L)
                qs[dst, :] = q_ref[rows, :]
                ks[dst, :] = k_ref[rows, :]
                vs[dst, :] = v_ref[rows, :]
        else:
            qs, ks, vs, os_, ls = q_ref, k_ref, v_ref, o_ref, l_ref
        ls[...] = jnp.zeros_like(ls)
        for e in range(2):
            lanes = slice(64 * e, 64 * e + 64)
            bias_first, bias_band = _attn_wide_bias(b_ref[e], dil)
            for q0, k0, kn, band_form in _attn_chunks(dil):
                cur, keys = pl.ds(q0, ATTN_QB), pl.ds(k0, kn)
                o, l = _attn_block(qs[cur, lanes], ks[keys, lanes], vs[keys, lanes],
                                   bias_band if band_form else bias_first, mask_band if band_form else mask_first)
                os_[cur, lanes] = o
                ls[cur, e:e + 1] = l
        if dil > 1:
            for r in range(dil):
                rows, src = _residue_rows(r, dil), pl.ds(r * L, L)
                o_ref[rows, :] = os_[src, :]
                l_ref[rows, :] = ls[src, :]

    scratch = [pltpu.VMEM((S, 128), F32)] * 5 if dil > 1 else []
    return _pcall(body, name=name, grid=(B, 2),
                  in_specs=_qkv_specs(gi, lambda b, hp: (b, hp))
                  + [pl.BlockSpec((2, 128, 256), lambda b, hp: (2 * gi + hp, 0, 0))],
                  out_specs=[pl.BlockSpec((S, 128), lambda b, hp: (b, hp)),
                             pl.BlockSpec((None, S, 128), lambda b, hp: (hp, b, 0))],
                  out_shape=[jax.ShapeDtypeStruct((T, 256), F32), jax.ShapeDtypeStruct((2, T, 128), F32)],
                  scratch_shapes=scratch,
                  compiler_params=_cp(("parallel", "parallel")))(hcat, hcat, hcat, bias_all)


def _attn_bwd(hcat, bias_all, gi, do, dl, name):
    dil = ATTN_DILS[gi]
    T = hcat.shape[0]
    B, L = T // S, S // dil

    def body(q_ref, k_ref, v_ref, b_ref, do_ref, dl_ref, dq_ref, dk_ref, dv_ref, db_ref, acc_ref, *scr):
        bi = pl.program_id(1)
        mask_first, mask_band = _attn_masks(dil)
        if dil > 1:
            qs, ks, vs, dos, dls, dqs, dks, dvs = scr
            for r in range(dil):
                rows, dst = _residue_rows(r, dil), pl.ds(r * L, L)
                qs[dst, :] = q_ref[rows, :]
                ks[dst, :] = k_ref[rows, :]
                vs[dst, :] = v_ref[rows, :]
                dos[dst, :] = do_ref[rows, :]
                dls[dst, :] = dl_ref[rows, :]
        else:
            qs, ks, vs, dos, dls, dqs, dks, dvs = q_ref, k_ref, v_ref, do_ref, dl_ref, dq_ref, dk_ref, dv_ref
        dks[...] = jnp.zeros_like(dks)
        dvs[...] = jnp.zeros_like(dvs)
        for e in range(2):
            lanes = slice(64 * e, 64 * e + 64)
            bias_first, bias_band = _attn_wide_bias(b_ref[e], dil)
            acc_ref[...] = jnp.zeros_like(acc_ref)
            for q0, k0, kn, band_form in _attn_chunks(dil):
                cur, keys = pl.ds(q0, ATTN_QB), pl.ds(k0, kn)
                f = functools.partial(_attn_block, mask=mask_band if band_form else mask_first)
                _, vjp = jax.vjp(f, qs[cur, lanes], ks[keys, lanes], vs[keys, lanes],
                                 bias_band if band_form else bias_first)
                dq, dkb, dvb, dbs = vjp((dos[cur, lanes], dls[cur, e:e + 1]))
                dqs[cur, lanes] = dq
                dks[keys, lanes] += dkb
                dvs[keys, lanes] += dvb
                prev, here = _fold_dbias(dbs, dil, band_form)
                if prev is not None:
                    acc_ref[:, 0:128] += prev
                acc_ref[:, 128:256] += here

            @pl.when(bi == 0)
            def _(e=e):
                db_ref[e] = acc_ref[...]

            @pl.when(bi > 0)
            def _(e=e):
                db_ref[e] += acc_ref[...]

        if dil > 1:
            for r in range(dil):
                rows, src = _residue_rows(r, dil), pl.ds(r * L, L)
                dq_ref[rows, :] = dqs[src, :]
                dk_ref[rows, :] = dks[src, :]
                dv_ref[rows, :] = dvs[src, :]

    order = lambda hp, b: (b, hp)
    blk = pl.BlockSpec((S, 128), lambda hp, b: (b, hp))
    lblk = pl.BlockSpec((None, S, 128), lambda hp, b: (hp, b, 0))
    sds = jax.ShapeDtypeStruct((T, 256), F32)
    scratch = [pltpu.VMEM((128, 256), F32)] + ([pltpu.VMEM((S, 128), F32)] * 8 if dil > 1 else [])
    return _pcall(body, name=name, grid=(2, B),
                  in_specs=_qkv_specs(gi, order) + [pl.BlockSpec((2, 128, 256), lambda hp, b: (2 * gi + hp, 0, 0)), blk, lblk],
                  out_specs=[blk, blk, blk, pl.BlockSpec((2, 128, 256), lambda hp, b: (hp, 0, 0))],
                  out_shape=[sds, sds, sds, jax.ShapeDtypeStruct((4, 128, 256), F32)],
                  scratch_shapes=scratch,
                  compiler_params=_cp(("parallel", "arbitrary")))(hcat, hcat, hcat, bias_all, do, dl)


def _lse_merge(o0, o1, o2, l0, l1, l2):
    m = lax.stop_gradient(jnp.maximum(jnp.maximum(l0, l1), l2))
    e0, e1, e2 = jnp.exp(l0 - m), jnp.exp(l1 - m), jnp.exp(l2 - m)
    den = e0 + e1 + e2
    return (e0 / den) * o0 + (e1 / den) * o1 + (e2 / den) * o2


def _attn_merge(outs, lses, dy, name):
    T = outs[0].shape[0]
    bwd = dy is not None
    tm = 512

    def body(*refs):
        o_refs, l_refs = refs[:3], refs[3:6]
        if bwd:
            for r in refs[10:13]:
                r[...] = jnp.zeros_like(r)
        for e in range(2):
            lanes = slice(64 * e, 64 * e + 64)
            vals = [r[:, lanes] for r in o_refs] + [r[:, e:e + 1] for r in l_refs]
            if not bwd:
                refs[6][:, lanes] = _lse_merge(*vals).astype(refs[6].dtype)
            else:
                _, vjp = jax.vjp(_lse_merge, *vals)
                g = vjp(refs[6][:, lanes])
                for r, v in zip(refs[7:10], g[:3]):
                    r[:, lanes] = v
                for r, v in zip(refs[10:13], g[3:]):
                    r[:, e:e + 1] = v

    blk = pl.BlockSpec((tm, 128), lambda i, hp: (i, hp))
    lblk = pl.BlockSpec((None, tm, 128), lambda i, hp: (hp, i, 0))
    lsd = jax.ShapeDtypeStruct((2, T, 128), F32)
    if not bwd:
        return _pcall(body, name=name, grid=(T // tm, 2), in_specs=[blk] * 3 + [lblk] * 3, out_specs=blk,
                      out_shape=jax.ShapeDtypeStruct((T, 256), F32),
                      compiler_params=_cp(("parallel", "parallel")))(*outs, *lses)
    return _pcall(body, name=name, grid=(T // tm, 2), in_specs=[blk] * 3 + [lblk] * 3 + [blk],
                  out_specs=[blk] * 3 + [lblk] * 3, out_shape=[jax.ShapeDtypeStruct((T, 256), F32)] * 3 + [lsd] * 3,
                  compiler_params=_cp(("parallel", "parallel")))(*outs, *lses, dy)


def _gmerge(g0, g1, g2, gb, ya, yb, yc):
    return (jax.nn.sigmoid(g0 + gb[:, 0:D]) * ya + jax.nn.sigmoid(g1 + gb[:, D:2 * D]) * yb
            + jax.nn.sigmoid(g2 + gb[:, 2 * D:3 * D]) * yc)


def _gated_merge(gates, gb, ya, yb, yc, name, gcol=0):
    def fn(tv, fv):
        return [_gmerge(tv[0], tv[1], tv[2], fv[0], tv[3], tv[4], tv[5])], []
    return _rowwise(fn, [(gates, D, gcol), (gates, D, gcol + 1), (gates, D, gcol + 2), ya, yb, yc], [gb], [(D, _ACT)], [],
                    name=name)[0]


def _gated_merge_bwd(gates, gb, ya, yb, yc, dm, name, gcol=0):
    def fn(tv, fv):
        _, vjp = jax.vjp(_gmerge, tv[0], tv[1], tv[2], fv[0], tv[3], tv[4], tv[5])
        d0, d1, d2, dgb, da, db, dc = vjp(tv[6])
        return [[d0, d1, d2], da, db, dc], [dgb]
    return _rowwise(fn, [(gates, D, gcol), (gates, D, gcol + 1), (gates, D, gcol + 2), ya, yb, yc, dm], [gb],
                    [(3 * D, _ACT), (D, _ACT), (D, _ACT), (D, _ACT)], [(1, 3 * D)], name=name)


def _pool_affine(t1, pb, ps, dout, name):
    if dout is None:
        def fn(tv, fv):
            return [(tv[0] + fv[0]) * fv[1]], []
        return _rowwise(fn, [t1], [pb, ps], [(POOLW, _ACT)], [], name=name)[0]

    def fnb(tv, fv):
        t2, vjp = jax.vjp(lambda t, b, s: (t + b) * s, tv[0], fv[0], fv[1])
        dt, db, dsc = vjp(tv[1])
        return [dt, t2], [db, dsc]
    return _rowwise(fnb, [t1, dout], [pb, ps], [(POOLW, _ACT), (POOLW, _ACT)], [(1, POOLW), (1, POOLW)], name=name)


def _dt_softplus(dt_raw, dt_bias, ddt, name):
    f = lambda r, b: _softplus(r + b)
    if ddt is None:
        def fn(tv, fv):
            return [f(tv[0], fv[0])], []
        return _rowwise(fn, [dt_raw], [dt_bias], [(16, F32)], [], name=name, tm=1024)[0]

    def fnb(tv, fv):
        _, vjp = jax.vjp(f, tv[0], fv[0])
        dr, db = vjp(tv[1])
        return [dr], [db]
    return _rowwise(fnb, [dt_raw, ddt], [dt_bias], [(16, F32)], [(1, 16)], name=name, tm=1024)


def _adamw_math(wv, gv, mv, vv):
    c1 = 1.0 / (1.0 - ADAM_B1 ** ADAM_STEP)
    c2 = 1.0 / (1.0 - ADAM_B2 ** ADAM_STEP)
    mn = ADAM_B1 * mv + (1.0 - ADAM_B1) * gv
    vn = ADAM_B2 * vv + (1.0 - ADAM_B2) * (gv * gv)
    delta = -ADAM_LR * ((mn * c1) / (jnp.sqrt(vn * c2) + ADAM_EPS) + ADAM_WD * wv)
    return delta, mn, vn


def _adamw(w, g, m, v, name):
    R, C = w.shape
    tm = _pick(R, (256, 128, 64, 32, 16, 8))
    return _rowwise(lambda tv, fv: (list(_adamw_math(*tv)), []), [w, g, m, v], [], [(C, F32)] * 3, [], name=name, tm=tm)


def _adamw_layer(i, w, g, m, v, accs, name, dep=None):
    R, C = w.shape
    r = R // NL
    tm = _pick(r, (256, 128, 64, 32, 16, 8))
    nt = r // tm
    if accs is None:
        accs = [lax.empty((R, C), F32) for _ in range(4)]
    extra = [] if dep is None else [dep]

    def body(w_ref, g_ref, m_ref, v_ref, *rest):
        go_ref, do_ref, mo_ref, vo_ref = rest[-4:]
        gv = g_ref[...]
        delta, mn, vn = _adamw_math(w_ref[...], gv, m_ref[...], v_ref[...])
        go_ref[...] = gv
        do_ref[...] = delta
        mo_ref[...] = mn
        vo_ref[...] = vn

    slab = pl.BlockSpec((tm, C), lambda t: (i * nt + t, 0))
    anyspec = pl.BlockSpec(memory_space=pl.ANY)
    return _pcall(body, name=name, grid=(nt,),
                  in_specs=[slab, pl.BlockSpec((tm, C), lambda t: (t, 0)), slab, slab] + [anyspec] * (4 + len(extra)),
                  out_specs=[slab] * 4, out_shape=[jax.ShapeDtypeStruct((R, C), F32)] * 4,
                  input_output_aliases={4 + k: k for k in range(4)},
                  compiler_params=_cp(("parallel",)))(w, g, m, v, *accs, *extra)


def _ffn_fwd(x, xm, w13, w2, g, b, tag, dep=None):
    ha, hg, s = _swiglu_fwd(xm, w13, dep, name=f"{tag}_h")
    r, out, outm = _mm_res_ln(s, w2, x, g, b, 0.5, name=f"{tag}_y")
    return out, outm, dict(x=xm, ha=ha, hg=hg, r=r)


def _ffn_bwd(dout, sv, w13, w2, g, b, tag, dep=None):
    dskip, dy, dg, db = _ln_bwd(sv['r'], g, b, dout, 0.5, name=f"{tag}_lnb")
    ds = _mm(dy, w2, tb=True, dep=dep, name=f"{tag}_ds")
    dh, s = _swiglu_act_bwd(sv['ha'], sv['hg'], ds, name=f"{tag}_actb")
    dw2 = _mm(s, dy, ta=True, name=f"{tag}_dw2")
    dw13 = _mm(sv['x'], dh, ta=True, name=f"{tag}_dw13")
    dx = _mm(dh, w13, tb=True, add=dskip, name=f"{tag}_dx")
    return dx, dict(w13=dw13, w2=dw2, g=dg, b=db)


def _mixer_fwd(x1, x1m, W, bias_all, tag, dep=None):
    T = x1.shape[0]
    hcat = _mm(x1m, W['w_in_r'], dep=dep, name=f"{tag}_hcat")
    dt_raw = hcat[:, O_DT:O_DT + 16]
    pooled = _pool_mean(hcat, False, name=f"{tag}_pool", col0=O_U // 128)
    t1 = _mm(pooled, W['pool_wbd'], name=f"{tag}_pt1")
    t2 = _pool_affine(t1, W['pool_b'], W['pool_scale'], None, name=f"{tag}_paff")
    ya = _mm(t2, W['p_pool'], name=f"{tag}_ya")
    act = _conv_silu(hcat, W['conv_w'], W['conv_b'], name=f"{tag}_conv", col0=O_XBC // 128)
    dt = _dt_softplus(dt_raw, W['dt_bias'], None, name=f"{tag}_dt")
    dtg = dt.reshape(T, 4, 4).transpose(1, 0, 2)
    yscan, hs = _ssd_fwd(act, dtg, W['a_neg'], W['d_skip'], name=f"{tag}_ssd")
    ybn = _ssd_gate_norm(yscan, hcat, W['ssd_norm'], name=f"{tag}_gn", zcol=O_Z // 256)
    yb = _mm(ybn, W['p_ssd'], name=f"{tag}_yb")
    outs, lses = [], []
    for gi in range(len(ATTN_DILS)):
        o, l = _attn_fwd(hcat, bias_all, gi, name=f"{tag}_attn{gi}")
        outs.append(o)
        lses.append(l)
    ycp = _attn_merge(outs, lses, None, name=f"{tag}_amerge")
    yc = _mm(ycp, W['p_attn'], name=f"{tag}_yc")
    merged = _gated_merge(hcat, W['gate_b'], ya, yb, yc, name=f"{tag}_gm", gcol=O_G // D)
    r, out, outm = _mm_res_ln(merged, W['w_out'], x1, W['ln2_g'], W['ln2_b'], 1.0, name=f"{tag}_mix")
    sv = dict(x1=x1m, dt_raw=dt_raw, pooled=pooled, t1=t1, act=act, dtg=dtg,
              hs=hs, yscan=yscan, ybn=ybn, hcat=hcat, outs=outs, lses=lses, ycp=ycp, ya=ya, yb=yb, yc=yc,
              merged=merged, r=r)
    return out, outm, sv


def _mixer_bwd(dout, sv, W, bias_all, tag, dep=None):
    T = dout.shape[0]
    gr = {}
    dx1a, dr, gr['ln2_g'], gr['ln2_b'] = _ln_bwd(sv['r'], W['ln2_g'], W['ln2_b'], dout, 1.0, name=f"{tag}_lnb")
    dmerged = _mm(dr, W['w_out'], tb=True, dep=dep, name=f"{tag}_dmerged")
    gr['w_out'] = _mm(sv['merged'], dr, ta=True, name=f"{tag}_dwout")
    dgates, dya, dyb, dyc, gr['gate_b'] = _gated_merge_bwd(sv['hcat'], W['gate_b'], sv['ya'], sv['yb'], sv['yc'],
                                                           dmerged, name=f"{tag}_gmb", gcol=O_G // D)
    dycp = _mm(dyc, W['p_attn'], tb=True, name=f"{tag}_dycp")
    gr['p_attn'] = _mm(sv['ycp'], dyc, ta=True, name=f"{tag}_dpattn")
    dml = _attn_merge(sv['outs'], sv['lses'], dycp, name=f"{tag}_amergeb")
    dq, dk, dv, dbias = [], [], [], []
    for gi in range(len(ATTN_DILS)):
        a, b, c, d = _attn_bwd(sv['hcat'], bias_all, gi, dml[gi], dml[3 + gi], name=f"{tag}_attnb{gi}")
        dq.append(a)
        dk.append(b)
        dv.append(c)
        dbias.append(d)
    dbias = jnp.concatenate(dbias, axis=0)
    dybn = _mm(dyb, W['p_ssd'], tb=True, name=f"{tag}_dybn")
    gr['p_ssd'] = _mm(sv['ybn'], dyb, ta=True, name=f"{tag}_dpssd")
    dyscan, dz, gr['ssd_norm'] = _ssd_gate_norm_bwd(sv['yscan'], sv['hcat'], W['ssd_norm'], dybn, name=f"{tag}_gnb",
                                                    zcol=O_Z // 256)
    dxs, ddtg, dbm, dcm, dak = _ssd_bwd(sv['act'], sv['dtg'], W['a_neg'], W['d_skip'], sv['hs'], dyscan,
                                        name=f"{tag}_ssdb")
    gr['a_neg'], gr['d_skip'] = dak[:, 0, 0:4], dak[:, 1, 0:4]
    ddt = ddtg.transpose(1, 0, 2).reshape(T, 16)
    ddt_raw, gr['dt_bias'] = _dt_softplus(sv['dt_raw'], W['dt_bias'], ddt, name=f"{tag}_dtb")
    dact = jnp.concatenate([dxs, dbm, dcm], axis=1)
    dxbc, gr['conv_w'], gr['conv_b'] = _conv_silu_bwd(sv['hcat'], W['conv_w'], W['conv_b'], dact, name=f"{tag}_convb",
                                                      col0=O_XBC // 128)
    dt2 = _mm(dya, W['p_pool'], tb=True, name=f"{tag}_dt2")
    dt1, t2, gr['pool_b'], gr['pool_scale'] = _pool_affine(sv['t1'], W['pool_b'], W['pool_scale'], dt2, name=f"{tag}_paffb")
    gr['p_pool'] = _mm(t2, dya, ta=True, name=f"{tag}_dppool")
    dpooled = _mm(dt1, W['pool_wbd'], tb=True, name=f"{tag}_dpooled")
    gr['pool_wbd'] = _mm(sv['pooled'], dt1, ta=True, name=f"{tag}_dpoolw")
    du = _pool_mean(dpooled, True, name=f"{tag}_poolb")
    dhcat = jnp.concatenate([t.astype(_ACT) for t in [du, dz, dxbc] + dq + dk + dv + [dgates, ddt_raw]]
                            + [jnp.zeros((T, HC - O_DT - 16), _ACT)], axis=1)
    dx1 = _mm(dhcat, W['w_in_r'], tb=True, add=dx1a, name=f"{tag}_dx1")
    gr['w_in_r'] = _mm(sv['x1'], dhcat, ta=True, name=f"{tag}_dwin")
    return dx1, gr, dbias


def _prep_layer_weights(i, inp, G):
    W = {}
    for n in BIG:
        if n not in G:
            continue
        g = G[n]
        if n == 'w_in':
            W['w_in_r'] = jnp.concatenate(_nat_pieces(g, 0, 3840) + _nat_pieces(g, 3856, 9232) + _nat_pieces(g, 3840, 3856)
                                          + [jnp.zeros((D, HC - 9232), g.dtype)], axis=1)
        elif n in COL_SHARDED:
            W[n] = jnp.concatenate([g[j] for j in range(4)], axis=1)
        else:
            W[n] = g.reshape(4 * g.shape[1], g.shape[2])
    pw = inp['pool_w'][i].astype(_MXU)
    wbd = jnp.zeros((POOLW, POOLW), _MXU)
    for g in range(4):
        wbd = lax.dynamic_update_slice(wbd, pw[g], (g * POOL_GDIM, g * POOL_GDIM))
    W['pool_wbd'] = wbd
    W['pool_b'] = inp['pool_b'][i].reshape(1, POOLW)
    W['pool_scale'] = inp['pool_scale'][i].reshape(1, POOLW)
    if 'conv_w' in G:
        W['conv_w'] = jnp.concatenate([G['conv_w'][j] for j in range(4)], axis=1)
        W['gate_b'] = jnp.concatenate([G['gate_b'][j][b:b + 1] for b in range(3) for j in range(4)], axis=1)
    W['conv_b'] = inp['conv_b'][i].reshape(1, 2048)
    W['dt_bias'] = inp['dt_bias'][i].reshape(1, 16)
    W['a_neg'] = (-jnp.exp(inp['a_log'][i])).reshape(4, 1, 4)
    W['d_skip'] = inp['d_skip'][i].reshape(4, 1, 4)
    W['ssd_norm'] = inp['ssd_norm'][i].reshape(1, D)
    for n in ('ln1_g', 'ln1_b', 'ln2_g', 'ln2_b', 'ln3_g', 'ln3_b'):
        W[n] = inp[n][i].reshape(1, D)
    return W


GATHER_FIRST = ['ffn1_w13', 'ffn1_w2']
GATHER_REST = [n for n in BIG if n not in GATHER_FIRST] + ['gate_b', 'conv_w']


def _gather_start(inp, i, names):
    core = lax.axis_index("c")
    arrs = []
    for n in names:
        s = inp[n][i]
        if n in BIG:
            s = lax.dynamic_slice_in_dim(s, core * (s.shape[0] // 2), s.shape[0] // 2, axis=0).astype(BF16)
        arrs.append(s)
    state, token = _exchange_start(arrs, "chips", "gather", name="gather_start")
    return (names, state), token


def _gather_mid(handle, after):
    names, state = handle
    me = 2 * lax.axis_index("x") + lax.axis_index("y")
    own, outs = _exchange_wait(state, after, "chips", "gather", name="gather_wait")
    outs = [lax.dynamic_update_slice(o, a[None], (me, 0, 0)) for o, a in zip(outs, own)]
    big = [o for n, o in zip(names, outs) if n in BIG]
    state, token = _exchange_start(big, "cores", "gather", name="share_start")
    return (names, outs, state), token


def _gather_finish(handle, after):
    names, outs, state = handle
    core = lax.axis_index("c")
    mine, theirs = _exchange_wait(state, after, "cores", "gather", name="share_wait")
    G = {n: o for n, o in zip(names, outs) if n not in BIG}
    for n, a, b in zip([n for n in names if n in BIG], mine, theirs):
        G[n] = jnp.concatenate([jnp.where(core == 0, a, b), jnp.where(core == 0, b, a)], axis=1)
    return G


W_IN_SHARD = 2308


def _nat_pieces(g, lo, hi):
    out = []
    for j in range(4):
        s, e = max(lo, W_IN_SHARD * j), min(hi, W_IN_SHARD * (j + 1))
        if s < e:
            out.append(g[j][:, s - W_IN_SHARD * j:e - W_IN_SHARD * j])
    return out


def _reord_ranges(lo, hi):
    out = []
    for a, b, off in ((0, 3840, 0), (3840, 3856, O_DT - 3840), (3856, 9232, -16)):
        s, e = max(lo, a), min(hi, b)
        if s < e:
            out.append((s + off, e + off))
    return out


def _halves_of(n, g):
    if n == 'w_in':
        shards = [jnp.concatenate([g[:, a:b] for a, b in _reord_ranges(W_IN_SHARD * j, W_IN_SHARD * (j + 1))], axis=1)
                  for j in range(4)]
    elif n in COL_SHARDED:
        c = g.shape[1] // 4
        shards = [g[:, j * c:(j + 1) * c] for j in range(4)]
    else:
        r = g.shape[0] // 4
        shards = [g[j * r:(j + 1) * r] for j in range(4)]
    r2 = shards[0].shape[0] // 2
    return jnp.stack([jnp.concatenate([s[h * r2:(h + 1) * r2] for s in shards], axis=0) for h in range(2)])


def _reduce_a(grads):
    names = list(grads)
    halves = [_halves_of(n, grads[n]) for n in names]
    state, token = _exchange_start(halves, "cores", "scatter", name="rsc_start")
    return (names, state), token


def _reduce_b(handle, after):
    names, state = handle
    core = lax.axis_index("c").reshape(1)
    halves, got = _exchange_wait(state, after, "cores", "scatter", name="rsc_wait")
    chip = [_sum_own_recv(h, t, core, BF16, name="rs_sum2") for h, t in zip(halves, got)]
    chip = [t.reshape(4, t.shape[0] // 4, t.shape[1]) for t in chip]
    state, token = _exchange_start(chip, "chips", "scatter", name="rs_start")
    return (names, state), token


def _reduce_c(handle, after):
    names, state = handle
    chip_id = (2 * lax.axis_index("x") + lax.axis_index("y")).reshape(1)
    chip, got = _exchange_wait(state, after, "chips", "scatter", name="rs_wait")
    red = [_sum_own_recv(h, t, chip_id, F32, name="rs_sum4") for h, t in zip(chip, got)]
    other = _exchange(red, "cores", "gather", name="rs_share")
    out = {}
    for n, mine, theirs in zip(names, red, other):
        out[n] = jnp.where(lax.axis_index("c") == 0, jnp.concatenate([mine, theirs]), jnp.concatenate([theirs, mine]))
    return out


class _Comm:
    def __init__(self, inp):
        self.inp = inp

    def gather_start(self, i, names):
        return _gather_start(self.inp, i, names)

    gather_mid = staticmethod(_gather_mid)
    gather_finish = staticmethod(_gather_finish)

    def reduce_a(self, i, grads):
        return _reduce_a({n: grads[n] for n in BIG})

    reduce_b = staticmethod(_reduce_b)
    reduce_c = staticmethod(_reduce_c)


def _allreduce_small(vec, dep=None):
    for group in ("cores", "x", "y"):
        recv = _exchange([vec], group, "gather", name=f"ar_{group}", dep=dep if group == "cores" else None)[0]
        vec = _rowwise(lambda tv, fv: ([tv[0] + tv[1]], []), [vec, recv], [], [(128, F32)], [], name=f"ar_add_{group}")[0]
    return vec


def _pack(arrs):
    flat = jnp.concatenate([a.reshape(-1) for a in arrs])
    n = flat.shape[0]
    pad = (-n) % (256 * 128)
    flat = jnp.concatenate([flat, jnp.zeros((pad,), F32)])
    return flat.reshape(-1, 128)


def _unpack(p, shapes):
    flat = p.reshape(-1)
    out, off = [], 0
    for s in shapes:
        sz = int(np.prod(s))
        out.append(flat[off:off + sz].reshape(s))
        off += sz
    return out


def _forward_backward(inp, comm, bias_all):
    x = xm = inp['x'].reshape(-1, D)
    tgt = inp['loss_target'].reshape(-1, D)
    saved, Ws = [], []
    h_first, _ = comm.gather_start(0, GATHER_FIRST)
    h_rest, dep = comm.gather_start(0, GATHER_REST)
    h_first, tok = comm.gather_mid(h_first, x)
    G = comm.gather_finish(h_first, tok)
    for i in range(NL):
        W = _prep_layer_weights(i, inp, G)
        start_next = lambda: (comm.gather_start(i + 1, BIG + ['gate_b', 'conv_w']) if i + 1 < NL else (None, None))
        if i > 0:
            h_next, dep = start_next()
        x1, x1m, s1 = _ffn_fwd(x, xm, W['ffn1_w13'], W['ffn1_w2'], W['ln1_g'], W['ln1_b'], "f1", dep)
        if i == 0:
            h_rest, tok = comm.gather_mid(h_rest, x1m)
            W.update(_prep_layer_weights(i, inp, comm.gather_finish(h_rest, tok)))
            h_next, dep = start_next()
        x2, x2m, s2 = _mixer_fwd(x1, x1m, W, bias_all, "mx", dep if i == 0 else None)
        dep = None
        if h_next is not None:
            h_next, dep = comm.gather_mid(h_next, x2m)
        x, xm, s3 = _ffn_fwd(x2, x2m, W['ffn2_w13'], W['ffn2_w2'], W['ln3_g'], W['ln3_b'], "f2", dep)
        if h_next is not None:
            G = comm.gather_finish(h_next, xm)
        saved.append((s1, s2, s3))
        Ws.append(W)
    dy, lpart = _loss_fwd_bwd(x, tgt, name="loss")
    fins, reduced, dbiases = [None] * NL, [None] * NL, [None] * NL
    pend_a, pend_b, dep = None, None, None
    for i in reversed(range(NL)):
        W = Ws[i]
        s1, s2, s3 = saved[i]
        g = {}
        dx2, f = _ffn_bwd(dy, s3, W['ffn2_w13'], W['ffn2_w2'], W['ln3_g'], W['ln3_b'], "f2", dep)
        g['ffn2_w13'], g['ffn2_w2'], g['ln3_g'], g['ln3_b'] = f['w13'], f['w2'], f['g'], f['b']
        dep = None
        if pend_a is not None:
            handle, dep = comm.reduce_b(pend_a[1], dx2)
            pend_b = (pend_a[0], handle)
        dx1, gm, dbiases[i] = _mixer_bwd(dx2, s2, W, bias_all, "mx", dep)
        g.update(gm)
        dy, f = _ffn_bwd(dx1, s1, W['ffn1_w13'], W['ffn1_w2'], W['ln1_g'], W['ln1_b'], "f1")
        g['ffn1_w13'], g['ffn1_w2'], g['ln1_g'], g['ln1_b'] = f['w13'], f['w2'], f['g'], f['b']
        fins[i] = _finish_layer_grads(i, g, inp)
        if pend_b is not None:
            reduced[pend_b[0]] = comm.reduce_c(pend_b[1], dy)
            pend_b = None
        handle, dep = comm.reduce_a(i, fins[i])
        pend_a = (i, handle)
    return lpart, dy, fins, reduced, pend_a, dbiases, dep


def _finish_layer_grads(i, g, inp):
    out = {n: g[n] for n in BIG if n != 'w_in'}
    out['w_in'] = g['w_in_r']
    out['pool_w'] = jnp.stack([g['pool_wbd'][k * POOL_GDIM:(k + 1) * POOL_GDIM, k * POOL_GDIM:(k + 1) * POOL_GDIM] for k in range(4)])
    out['pool_b'] = g['pool_b'].reshape(4, POOL_GDIM)
    out['pool_scale'] = g['pool_scale'].reshape(POOLW)
    out['conv_w'] = g['conv_w']
    out['conv_b'] = g['conv_b'].reshape(2048)
    out['dt_bias'] = g['dt_bias'].reshape(16)
    out['a_log'] = (g['a_neg'].reshape(16)) * (-jnp.exp(inp['a_log'][i]))
    out['d_skip'] = g['d_skip'].reshape(16)
    out['ssd_norm'] = g['ssd_norm'].reshape(D)
    out['gate_b'] = g['gate_b'].reshape(3, D)
    for n in ('ln1_g', 'ln1_b', 'ln2_g', 'ln2_b', 'ln3_g', 'ln3_b'):
        out[n] = g[n].reshape(D)
    return out


def kernel(x, ffn1_w13, ffn1_w2, ln1_g, ln1_b, w_in, gate_b, pool_w, pool_b, pool_scale, conv_w, conv_b,
           dt_bias, a_log, d_skip, ssd_norm, rel_bias, p_pool, p_ssd, p_attn, w_out, ln2_g, ln2_b, ffn2_w13,
           ffn2_w2, ln3_g, ln3_b, loss_target, m_ffn1_w13, m_ffn1_w2, m_ln1_g, m_ln1_b, m_w_in, m_gate_b,
           m_pool_w, m_pool_b, m_pool_scale, m_conv_w, m_conv_b, m_dt_bias, m_a_log, m_d_skip, m_ssd_norm,
           m_rel_bias, m_p_pool, m_p_ssd, m_p_attn, m_w_out, m_ln2_g, m_ln2_b, m_ffn2_w13, m_ffn2_w2, m_ln3_g,
           m_ln3_b, v_ffn1_w13, v_ffn1_w2, v_ln1_g, v_ln1_b, v_w_in, v_gate_b, v_pool_w, v_pool_b,
           v_pool_scale, v_conv_w, v_conv_b, v_dt_bias, v_a_log, v_d_skip, v_ssd_norm, v_rel_bias, v_p_pool,
           v_p_ssd, v_p_attn, v_w_out, v_ln2_g, v_ln2_b, v_ffn2_w13, v_ffn2_w2, v_ln3_g, v_ln3_b):
    inp = dict(locals())
    maps = jnp.asarray(_bucket_maps())
    bias_all = _bias_build(rel_bias, maps, name="bias_build")
    comm = _Comm(inp)
    lpart, gx, fins, red, pending, dbiases, halves_started = _forward_backward(inp, comm, bias_all)
    loss = lax.psum(lpart[0, 0], ("x", "y", "c"))

    small_l = [n for n in SMALL if n != 'rel_bias']
    drel = _bias_reduce(jnp.stack(dbiases), maps, name="bias_reduce")[:, 0, :32].T
    small_arrs = [jnp.stack([fins[i][n] for i in range(NL)]) for n in small_l] + [drel]
    packed = _allreduce_small(_pack(small_arrs), dep=halves_started)
    handle_b, started = comm.reduce_b(pending[1], packed)
    gsmall = dict(zip(small_l + ['rel_bias'], _unpack(packed, [a.shape for a in small_arrs])))
    shard = 2 * lax.axis_index("x") + lax.axis_index("y")
    gsmall['gate_b'] = lax.dynamic_slice_in_dim(gsmall['gate_b'], shard * 256, 256, axis=2)
    gsmall['conv_w'] = lax.dynamic_slice_in_dim(gsmall['conv_w'], shard * 512, 512, axis=2)
    gout, delta, new_m, new_v = dict(gsmall), {}, {}, {}
    shapes = [inp[n].shape for n in SMALL]
    d, m, v = _adamw(_pack([inp[n] for n in SMALL]), _pack([gsmall[n] for n in SMALL]),
                     _pack([inp['m_' + n] for n in SMALL]), _pack([inp['v_' + n] for n in SMALL]), name="adamw_small")
    for n, dd, mm, vv in zip(SMALL, _unpack(d, shapes), _unpack(m, shapes), _unpack(v, shapes)):
        delta[n], new_m[n], new_v[n] = dd, mm, vv

    two_d = lambda a: a.reshape(a.shape[0] * a.shape[1], a.shape[2])
    accs = {n: None for n in BIG}

    def adamw_layer(i, dep=None):
        for n in BIG:
            accs[n] = _adamw_layer(i, two_d(inp[n]), red[i][n], two_d(inp['m_' + n]), two_d(inp['v_' + n]), accs[n],
                                   name="adamw_big", dep=dep)

    done = [i for i in range(NL) if i != pending[0]]
    for i in done:
        adamw_layer(i, dep=started)
    red[pending[0]] = comm.reduce_c(handle_b, [d] + ([accs[n][1] for n in BIG] if done else []))
    adamw_layer(pending[0])
    for n in BIG:
        gout[n], delta[n], new_m[n], new_v[n] = [a.reshape(inp[n].shape) for a in accs[n]]

    return (loss, gx.reshape(x.shape), *[gout[n] for n in WEIGHTS], *[delta[n] for n in WEIGHTS],
            *[new_m[n] for n in WEIGHTS], *[new_v[n] for n in WEIGHTS])
```

```python
import functools

import numpy as np
import jax
import jax.numpy as jnp
from jax import lax
from jax.experimental import pallas as pl
from jax.experimental.pallas import tpu as pltpu

F32 = jnp.float32
BF16 = jnp.bfloat16
_MXU = jnp.bfloat16
_ACT = jnp.bfloat16
_VMEM_LIMIT = 56 * 1024 * 1024

S = 2048
D = 1024
NL = 4
DFF = 2816
LN_EPS = 1e-5
SSD_EPS = 1e-5
ALPHA = (2.0 * NL) ** 0.25
POOLW = 768
POOL_WINDOWS = (2, 4, 8, 16)
POOL_GDIM = 192
CH = 128
ATTN_DILS = (1, 4, 16)
HC = 9728
O_U, O_Z, O_XBC, O_Q, O_K, O_V, O_G, O_DT = 0, 768, 1792, 3840, 4608, 5376, 6144, 9216

ADAM_LR, ADAM_B1, ADAM_B2, ADAM_EPS, ADAM_WD, ADAM_STEP = 0.001, 0.9, 0.999, 1e-08, 0.01, 10

WEIGHTS = ['ffn1_w13', 'ffn1_w2', 'ln1_g', 'ln1_b', 'w_in', 'gate_b', 'pool_w', 'pool_b', 'pool_scale', 'conv_w',
           'conv_b', 'dt_bias', 'a_log', 'd_skip', 'ssd_norm', 'rel_bias', 'p_pool', 'p_ssd', 'p_attn', 'w_out',
           'ln2_g', 'ln2_b', 'ffn2_w13', 'ffn2_w2', 'ln3_g', 'ln3_b']
BIG = ['ffn1_w13', 'ffn1_w2', 'w_in', 'p_pool', 'p_ssd', 'p_attn', 'w_out', 'ffn2_w13', 'ffn2_w2']
COL_SHARDED = {'ffn1_w13', 'ffn2_w13', 'w_in', 'p_pool', 'p_attn'}
SMALL = [n for n in WEIGHTS if n not in BIG]


def _pcall(body, **kw):
    return pl.pallas_call(body, **kw)


def _cp(sem=None):
    return pltpu.CompilerParams(dimension_semantics=sem, vmem_limit_bytes=_VMEM_LIMIT)


def _pick(n, cands):
    for c in cands:
        if n % c == 0:
            return c
    raise ValueError(f"no tile for {n}")


def _mm(a, b, *, ta=False, tb=False, add=None, out_dtype=F32, dep=None, name):
    if ta:
        K, M = a.shape
    else:
        M, K = a.shape
    if tb:
        N, K2 = b.shape
    else:
        K2, N = b.shape
    assert K == K2, (a.shape, b.shape, ta, tb)
    sa, sb, so = a.dtype.itemsize, b.dtype.itemsize, jnp.dtype(out_dtype).itemsize
    tm, tn, tk = _mm_tiles(M, N, K, sa, sb, so + (4 if add is not None else 0))
    nk = K // tk
    a_bytes, b_bytes = M * K * sa, K * N * sb
    j_outer = nk == 1 and (b_bytes + a_bytes * (N // tn) < a_bytes + b_bytes * (M // tm))
    ij = (lambda p, q: (q, p)) if j_outer else (lambda p, q: (p, q))

    def im(f):
        return lambda p, q, k: f(*ij(p, q), k)

    a_spec = pl.BlockSpec((tk, tm), im(lambda i, j, k: (k, i))) if ta else pl.BlockSpec((tm, tk), im(lambda i, j, k: (i, k)))
    b_spec = pl.BlockSpec((tn, tk), im(lambda i, j, k: (j, k))) if tb else pl.BlockSpec((tk, tn), im(lambda i, j, k: (k, j)))
    o_spec = pl.BlockSpec((tm, tn), im(lambda i, j, k: (i, j)))
    dims = (((0 if ta else 1,), (1 if tb else 0,)), ((), ()))
    has_add = add is not None

    n_in = 2 + int(has_add) + int(dep is not None)

    def body(*refs):
        a_ref, b_ref = refs[0], refs[1]
        add_ref = refs[2] if has_add else None
        o_ref = refs[n_in]
        part = lax.dot_general(a_ref[...].astype(_MXU), b_ref[...].astype(_MXU), dims, preferred_element_type=F32)

        def finish(r):
            if has_add:
                r = r + add_ref[...]
            o_ref[...] = r.astype(out_dtype)

        if nk == 1:
            finish(part)
        else:
            acc = refs[-1]
            k = pl.program_id(2)

            @pl.when(k == 0)
            def _():
                acc[...] = part

            @pl.when(k > 0)
            def _():
                acc[...] += part

            @pl.when(k == nk - 1)
            def _():
                finish(acc[...])

    in_specs = [a_spec, b_spec]
    args = [a, b]
    if has_add:
        in_specs.append(o_spec)
        args.append(add)
    if dep is not None:
        in_specs.append(pl.BlockSpec(memory_space=pl.ANY))
        args.append(dep)
    gm, gn = M // tm, N // tn
    return _pcall(
        body, name=name, grid=((gn, gm, nk) if j_outer else (gm, gn, nk)), in_specs=in_specs, out_specs=o_spec,
        out_shape=jax.ShapeDtypeStruct((M, N), out_dtype),
        scratch_shapes=([pltpu.VMEM((tm, tn), F32)] if nk > 1 else []),
        compiler_params=_cp(("parallel", "parallel", "arbitrary")),
    )(*args)


_MM_VMEM_BUDGET = 40 * 1024 * 1024


def _divisors128(n, cap):
    return [d for d in range(128, min(n, cap) + 1, 128) if n % d == 0][::-1]


_MM_CYC_PER_MMAC = 4.35
_MM_CYC_PER_ACC_VREG = 2.03
_MM_HBM_BYTES_PER_CYC = 1455.0
_MM_CYC_PER_STEP = 770.0


def _mm_tiles(M, N, K, sa, sb, so):
    best = None
    for tm in _divisors128(M, 1408):
        for tn in _divisors128(N, 2560):
            for tk in ([K] if K <= 4096 else []) + _divisors128(K, 2816):
                nk = K // tk
                need = 2 * (tm * tk * sa + tk * tn * sb + tm * tn * so) + (tm * tn * 4 if nk > 1 else 0)
                need += tm * tk * 2 + tk * tn * 2 + tm * tn * 4
                if need > _MM_VMEM_BUDGET:
                    continue
                gm, gn = M // tm, N // tn
                a_bytes, b_bytes = M * K * sa, K * N * sb
                hbm = min(b_bytes + a_bytes * gn, a_bytes + b_bytes * gm) if nk == 1 else a_bytes * gn + b_bytes * gm
                hbm += M * N * so
                work = _MM_CYC_PER_MMAC * M * N * K / 1e6 + _MM_CYC_PER_ACC_VREG * (M * N / 1024) * (nk if nk > 1 else 0.5)
                cost = max(work, hbm / _MM_HBM_BYTES_PER_CYC) + gm * gn * nk * _MM_CYC_PER_STEP
                if best is None or cost < best[0]:
                    best = (cost, (tm, tn, tk))
    assert best is not None, (M, N, K)
    return best[1]


def _store(ref, val):
    if isinstance(val, (list, tuple)):
        off = 0
        for p in val:
            w = p.shape[1]
            ref[:, off:off + w] = p.astype(ref.dtype)
            off += w
    else:
        ref[...] = val.astype(ref.dtype)


def _acc_store(ref, val, first):
    pieces = val if isinstance(val, (list, tuple)) else [val]
    off = 0
    for p in pieces:
        w = p.shape[1]

        @pl.when(first)
        def _(p=p, off=off, w=w):
            ref[:, off:off + w] = p

        @pl.when(jnp.logical_not(first))
        def _(p=p, off=off, w=w):
            ref[:, off:off + w] += p

        off += w


def _rowwise(fn, tiled, full, out_tiled, out_acc, *, name, tm=256):
    arrs, specs = [], []
    for t in tiled:
        arr, w, cb = t if isinstance(t, tuple) else (t, t.shape[1], 0)
        arrs.append(arr)
        specs.append(pl.BlockSpec((tm, w), functools.partial(lambda i, cb: (i, cb), cb=cb)))
    R = arrs[0].shape[0]
    assert R % tm == 0
    for f in full:
        arrs.append(f)
        specs.append(pl.BlockSpec(f.shape, functools.partial(lambda i, nd: (0,) * nd, nd=f.ndim)))
    nt, nf, no = len(tiled), len(full), len(out_tiled)

    def body(*refs):
        tv = [r[...] for r in refs[:nt]]
        fv = [r[...] for r in refs[nt:nt + nf]]
        ot, oa = fn(tv, fv)
        for r, v in zip(refs[nt + nf:nt + nf + no], ot):
            _store(r, v)
        first = pl.program_id(0) == 0
        for r, v in zip(refs[nt + nf + no:], oa):
            _acc_store(r, v, first)

    out_shape = [jax.ShapeDtypeStruct((R, c), dt) for c, dt in out_tiled]
    out_specs = [pl.BlockSpec((tm, c), lambda i: (i, 0)) for c, _ in out_tiled]
    for shp in out_acc:
        out_shape.append(jax.ShapeDtypeStruct(shp, F32))
        out_specs.append(pl.BlockSpec(shp, lambda i: (0, 0)))
    return _pcall(body, name=name, grid=(R // tm,), in_specs=specs, out_specs=out_specs, out_shape=out_shape,
                  compiler_params=_cp(("arbitrary",)))(*arrs)


def _group(group):
    x, y, c = lax.axis_index("x"), lax.axis_index("y"), lax.axis_index("c")
    if group == "chips":
        return 2 * x + y, [((x, 1 - y, c), 2 * x + 1 - y), ((1 - x, y, c), 2 * (1 - x) + y),
                           ((1 - x, 1 - y, c), 2 * (1 - x) + 1 - y)]
    if group == "cores":
        return c, [((x, y, 1 - c), 1 - c)]
    if group == "x":
        return x, [((1 - x, y, c), 1 - x)]
    return y, [((x, 1 - y, c), 1 - y)]


def _exchange(arrs, group, mode, name, dep=None):
    chips = group == "chips"
    k = len(arrs)
    npeer = 3 if chips else 1

    def body(*refs):
        nd = 0 if dep is None else 1
        ins, outs = refs[:k], refs[k + nd:2 * k + nd]
        send_sems, recv_sems = refs[2 * k + nd:]
        me, peers = _group(group)
        remote = []
        for i in range(k):
            for p, (dev, slot) in enumerate(peers):
                src = ins[i].at[slot] if mode == "scatter" else ins[i]
                if not chips:
                    dst = outs[i]
                else:
                    dst = outs[i].at[p] if mode == "scatter" else outs[i].at[me]
                cp = pltpu.make_async_remote_copy(src_ref=src, dst_ref=dst, send_sem=send_sems.at[i, p],
                                                  recv_sem=recv_sems.at[i, p], device_id=dev,
                                                  device_id_type=pl.DeviceIdType.MESH)
                cp.start()
                remote.append(cp)
        for cp in remote:
            cp.wait_recv()
        for cp in remote:
            cp.wait_send()

    def oshape(a):
        piece = a.shape[1:] if mode == "scatter" else a.shape
        if chips:
            piece = ((3,) if mode == "scatter" else (4,)) + piece
        return jax.ShapeDtypeStruct(piece, a.dtype)

    any_spec = pl.BlockSpec(memory_space=pl.ANY)
    extra = [] if dep is None else [dep]
    return _pcall(body, name=name, in_specs=[any_spec] * (k + len(extra)), out_specs=[any_spec] * k,
                  out_shape=[oshape(a) for a in arrs],
                  scratch_shapes=[pltpu.SemaphoreType.DMA((k, npeer)), pltpu.SemaphoreType.DMA((k, npeer))])(*arrs, *extra)


def _split_copies(ins, lands, send_sems, recv_sems, group, mode):
    chips = group == "chips"
    me, peers = _group(group)
    npeer = len(peers)
    out = []
    for i in range(len(ins)):
        for p, (dev, slot) in enumerate(peers):
            src = ins[i].at[slot] if mode == "scatter" else ins[i]
            if not chips:
                dst = lands[i]
            else:
                dst = lands[i].at[p] if mode == "scatter" else lands[i].at[me]
            out.append(pltpu.make_async_remote_copy(src_ref=src, dst_ref=dst, send_sem=send_sems.at[npeer * i + p],
                                                    recv_sem=recv_sems.at[npeer * i + p], device_id=dev,
                                                    device_id_type=pl.DeviceIdType.MESH))
    return out


def _exchange_start(arrs, group, mode, name):
    k = len(arrs)
    chips = group == "chips"
    nsem = (3 if chips else 1) * k
    hbm = pl.BlockSpec(memory_space=pltpu.HBM)
    sem = pl.BlockSpec(memory_space=pltpu.SEMAPHORE)

    def land_shape(a):
        piece = a.shape[1:] if mode == "scatter" else a.shape
        if chips:
            piece = ((3,) if mode == "scatter" else (4,)) + piece
        return piece

    def body(*refs):
        ins, lands = refs[:k], refs[k:2 * k]
        send_sems, recv_sems = refs[2 * k], refs[2 * k + 1]
        token = refs[-1]
        for cp in _split_copies(ins, lands, send_sems, recv_sems, group, mode):
            cp.start()
        token[...] = jnp.zeros_like(token)

    srcs = [pltpu.with_memory_space_constraint(a, pltpu.HBM) for a in arrs]
    lands = [pltpu.with_memory_space_constraint(lax.empty(land_shape(a), a.dtype), pltpu.HBM) for a in arrs]
    out_shape = ([pltpu.SemaphoreType.DMA((nsem,)), pltpu.SemaphoreType.DMA((nsem,))]
                 + [pltpu.HBM(a.shape, a.dtype) for a in arrs] + [pltpu.HBM(land_shape(a), a.dtype) for a in arrs]
                 + [jax.ShapeDtypeStruct((8, 128), F32)])
    outs = _pcall(body, name=name, in_specs=[hbm] * (2 * k),
                  out_specs=[sem, sem] + [hbm] * (2 * k) + [pl.BlockSpec(memory_space=pltpu.VMEM)], out_shape=out_shape,
                  input_output_aliases={i: 2 + i for i in range(2 * k)},
                  compiler_params=pltpu.CompilerParams(has_side_effects=pltpu.SideEffectType.DATAFLOW_SIDE_EFFECTING))(
                      *srcs, *lands)
    return (outs[0], outs[1], list(outs[2:2 + k]), list(outs[2 + k:2 + 2 * k])), outs[-1]


def _exchange_wait(state, after, group, mode, name):
    send_sems, recv_sems, srcs, lands = state
    k = len(srcs)
    after = list(after) if isinstance(after, (list, tuple)) else [after]
    hbm = pl.BlockSpec(memory_space=pltpu.HBM)
    sem = pl.BlockSpec(memory_space=pltpu.SEMAPHORE)

    def body(*refs):
        ins, lnd = refs[:k], refs[k:2 * k]
        send_sems, recv_sems = refs[2 * k], refs[2 * k + 1]
        for cp in _split_copies(ins, lnd, send_sems, recv_sems, group, mode):
            cp.wait_send()
            cp.wait_recv()

    outs = _pcall(body, name=name,
                  in_specs=[hbm] * (2 * k) + [sem, sem] + [pl.BlockSpec(memory_space=pl.ANY)] * len(after),
                  out_specs=[hbm] * (2 * k),
                  out_shape=[pltpu.HBM(a.shape, a.dtype) for a in srcs] + [pltpu.HBM(a.shape, a.dtype) for a in lands],
                  input_output_aliases={i: i for i in range(2 * k)},
                  compiler_params=pltpu.CompilerParams(has_side_effects=pltpu.SideEffectType.DATAFLOW_SIDE_EFFECTING))(
                      *srcs, *lands, send_sems, recv_sems, *after)
    return list(outs[:k]), list(outs[k:])


def _sum_own_recv(own, recv, me, out_dtype, name):
    n, R, C = own.shape
    nr = 1 if recv.ndim == 2 else recv.shape[0]
    tr = _pick(R, (256, 128, 64, 32, 16, 8))

    def body(me_ref, own_ref, *refs):
        o_ref = refs[-1]
        acc = own_ref[...].astype(F32)
        for r in refs[:-1]:
            acc = acc + r[...].astype(F32)
        o_ref[...] = acc.astype(out_dtype)

    specs = [pl.BlockSpec((None, tr, C), lambda i, me_ref: (me_ref[0], i, 0))]
    args = [own]
    if recv.ndim == 2:
        specs.append(pl.BlockSpec((tr, C), lambda i, me_ref: (i, 0)))
        args.append(recv)
    else:
        for p in range(nr):
            specs.append(pl.BlockSpec((None, tr, C), functools.partial(lambda i, me_ref, p: (p, i, 0), p=p)))
            args.append(recv)
    gs = pltpu.PrefetchScalarGridSpec(num_scalar_prefetch=1, grid=(R // tr,), in_specs=specs,
                                      out_specs=pl.BlockSpec((tr, C), lambda i, me_ref: (i, 0)))
    return _pcall(body, name=name, grid_spec=gs, out_shape=jax.ShapeDtypeStruct((R, C), out_dtype),
                  compiler_params=_cp(("parallel",)))(me, *args)


def _silu(x):
    return x * jax.nn.sigmoid(x)


def _ln(r, g, b):
    mu = jnp.mean(r, -1, keepdims=True)
    xc = r - mu
    var = jnp.mean(xc * xc, -1, keepdims=True)
    return xc * lax.rsqrt(var + LN_EPS) * g + b


def _softplus(x):
    return jnp.maximum(x, 0.0) + jnp.log1p(jnp.exp(-jnp.abs(x)))


RES_LN_TM = 512


def _mm_res_ln(a, w, x, g, b, res, name):
    T, K = a.shape
    tm = RES_LN_TM

    def body(a_ref, w_ref, x_ref, g_ref, b_ref, r_ref, o_ref, om_ref):
        y = jnp.dot(a_ref[...].astype(_MXU), w_ref[...].astype(_MXU), preferred_element_type=F32)
        r = ALPHA * x_ref[...] + res * y
        out = _ln(r, g_ref[...], b_ref[...])
        r_ref[...] = r
        o_ref[...] = out
        om_ref[...] = out.astype(om_ref.dtype)

    row = pl.BlockSpec((tm, D), lambda i: (i, 0))
    vec = pl.BlockSpec((1, D), lambda i: (0, 0))
    return _pcall(body, name=name, grid=(T // tm,),
                  in_specs=[pl.BlockSpec((tm, K), lambda i: (i, 0)), pl.BlockSpec((K, D), lambda i: (0, 0)), row, vec, vec],
                  out_specs=[row, row, row],
                  out_shape=[jax.ShapeDtypeStruct((T, D), F32), jax.ShapeDtypeStruct((T, D), F32),
                             jax.ShapeDtypeStruct((T, D), _ACT)],
                  compiler_params=_cp(("parallel",)))(a, w, x, g, b)


def _ln_bwd(r, g, b, dout, res, name):
    def fn(tv, fv):
        _, vjp = jax.vjp(_ln, tv[0], fv[0], fv[1])
        dr, dg, db = vjp(tv[1])
        return [ALPHA * dr, res * dr], [dg, db]
    return _rowwise(fn, [r, dout], [g, b], [(D, F32), (D, _ACT)], [(1, D), (1, D)], name=name)


SWIGLU_TM, SWIGLU_TN = 512, 1408


def _swiglu_fwd(x, w13, dep, name):
    T, K = x.shape
    tm, tn = SWIGLU_TM, SWIGLU_TN
    nj = DFF // tn
    has_dep = dep is not None

    def body(x_ref, wa_ref, wg_ref, *rest):
        a_ref, g_ref, s_ref = rest[-3:]
        xv = x_ref[...].astype(_MXU)
        a = jnp.dot(xv, wa_ref[...].astype(_MXU), preferred_element_type=F32)
        g = jnp.dot(xv, wg_ref[...].astype(_MXU), preferred_element_type=F32)
        a_ref[...] = a
        g_ref[...] = g
        s_ref[...] = (_silu(a) * g).astype(s_ref.dtype)

    out = pl.BlockSpec((tm, tn), lambda j, i: (i, j))
    in_specs = [pl.BlockSpec((tm, K), lambda j, i: (i, 0)), pl.BlockSpec((K, tn), lambda j, i: (0, j)),
                pl.BlockSpec((K, tn), lambda j, i: (0, nj + j))]
    args = [x, w13, w13]
    if has_dep:
        in_specs.append(pl.BlockSpec(memory_space=pl.ANY))
        args.append(dep)
    return _pcall(body, name=name, grid=(nj, T // tm), in_specs=in_specs, out_specs=[out, out, out],
                  out_shape=[jax.ShapeDtypeStruct((T, DFF), F32), jax.ShapeDtypeStruct((T, DFF), F32),
                             jax.ShapeDtypeStruct((T, DFF), _ACT)],
                  compiler_params=_cp(("parallel", "parallel")))(*args)


def _swiglu_act_bwd(a, g, ds, name):
    def fn(tv, fv):
        s, vjp = jax.vjp(lambda a, g: _silu(a) * g, tv[0], tv[1])
        da, dg = vjp(tv[2])
        return [[da, dg], s], []
    return _rowwise(fn, [a, g, ds], [], [(2 * DFF, _ACT), (DFF, _ACT)], [], name=name)


def _loss_fwd_bwd(y, tgt, name):
    def fn(tv, fv):
        e = tv[0] - tv[1]
        row = jnp.sum(e * e, axis=1, keepdims=True)
        tot = jnp.sum(row, axis=0, keepdims=True) * (0.5 / D)
        return [e * (1.0 / D)], [jnp.broadcast_to(tot, (1, 128))]
    return _rowwise(fn, [y, tgt], [], [(D, F32)], [(1, 128)], name=name)


def _shift_down(x, k, row):
    return jnp.where(row >= k, pltpu.roll(x, k, axis=0), 0.0)


def _shift_up(x, k, row):
    n = x.shape[0]
    return jnp.where(row < n - k, pltpu.roll(x, n - k, axis=0), 0.0)


def _pool_window_masks(j):
    lane = lax.broadcasted_iota(jnp.int32, (1, 128), 1) + j * 128
    grp = lane // POOL_GDIM
    return [grp == g for g in range(4)]


def _pool_mean(u, bwd, name, col0=0):
    T = u.shape[0]
    B = T // S

    def body(u_ref, o_ref):
        j = pl.program_id(1)
        x = u_ref[...]
        row = lax.broadcasted_iota(jnp.int32, (S, 1), 0)
        masks = _pool_window_masks(j)
        inv = [1.0 / jnp.minimum(row + 1, w).astype(F32) for w in POOL_WINDOWS]
        if not bwd:
            s2 = x + _shift_down(x, 1, row)
            s4 = s2 + _shift_down(s2, 2, row)
            s8 = s4 + _shift_down(s4, 4, row)
            s16 = s8 + _shift_down(s8, 8, row)
            mean = jnp.where(masks[0], s2 * inv[0], jnp.where(masks[1], s4 * inv[1],
                             jnp.where(masks[2], s8 * inv[2], s16 * inv[3])))
            o_ref[...] = (mean - x).astype(o_ref.dtype)
        else:
            g = [jnp.where(masks[i], x * inv[i], 0.0) for i in range(4)]
            t = g[3]
            t = t + _shift_up(t, 8, row) + g[2]
            t = t + _shift_up(t, 4, row) + g[1]
            t = t + _shift_up(t, 2, row) + g[0]
            t = t + _shift_up(t, 1, row)
            o_ref[...] = (t - x).astype(o_ref.dtype)

    spec = pl.BlockSpec((S, 128), lambda b, j: (b, j))
    return _pcall(body, name=name, grid=(B, POOLW // 128),
                  in_specs=[pl.BlockSpec((S, 128), lambda b, j: (b, j + col0))], out_specs=spec,
                  out_shape=jax.ShapeDtypeStruct((T, POOLW), _ACT), compiler_params=_cp(("parallel", "parallel")))(u)


def _conv_silu(xbc, w, b, name, col0=0):
    T, C = xbc.shape[0], w.shape[1]
    B = T // S

    def body(x_ref, w_ref, b_ref, o_ref):
        x = x_ref[...]
        row = lax.broadcasted_iota(jnp.int32, (S, 1), 0)
        c = b_ref[...] + w_ref[3:4, :] * x
        for s in range(1, 4):
            c = c + w_ref[3 - s:4 - s, :] * _shift_down(x, s, row)
        o_ref[...] = _silu(c)

    return _pcall(body, name=name, grid=(B, C // 128),
                  in_specs=[pl.BlockSpec((S, 128), lambda b, j: (b, j + col0)), pl.BlockSpec((4, 128), lambda b, j: (0, j)),
                            pl.BlockSpec((1, 128), lambda b, j: (0, j))],
                  out_specs=pl.BlockSpec((S, 128), lambda b, j: (b, j)),
                  out_shape=jax.ShapeDtypeStruct((T, C), F32), compiler_params=_cp(("parallel", "parallel")))(xbc, w, b)


def _conv_silu_bwd(xbc, w, b, dact, name, col0=0):
    T, C = xbc.shape[0], w.shape[1]
    B = T // S

    def body(x_ref, w_ref, b_ref, d_ref, dx_ref, dw_ref, db_ref):
        bi = pl.program_id(1)
        x = x_ref[...]
        row = lax.broadcasted_iota(jnp.int32, (S, 1), 0)
        xs = [x] + [_shift_down(x, s, row) for s in range(1, 4)]
        c = b_ref[...]
        for s in range(4):
            c = c + w_ref[3 - s:4 - s, :] * xs[s]
        _, vjp = jax.vjp(_silu, c)
        dc = vjp(d_ref[...])[0]
        dx = w_ref[3:4, :] * dc
        for s in range(1, 4):
            dx = dx + w_ref[3 - s:4 - s, :] * _shift_up(dc, s, row)
        dx_ref[...] = dx.astype(dx_ref.dtype)
        first = bi == 0
        for s in range(4):
            _acc_rows(dw_ref, 3 - s, jnp.sum(dc * xs[s], axis=0, keepdims=True), first)
        _acc_rows(db_ref, 0, jnp.sum(dc, axis=0, keepdims=True), first)

    blk = pl.BlockSpec((S, 128), lambda j, b: (b, j))
    return _pcall(body, name=name, grid=(C // 128, B),
                  in_specs=[pl.BlockSpec((S, 128), lambda j, b: (b, j + col0)), pl.BlockSpec((4, 128), lambda j, b: (0, j)),
                            pl.BlockSpec((1, 128), lambda j, b: (0, j)), blk],
                  out_specs=[blk, pl.BlockSpec((4, 128), lambda j, b: (0, j)), pl.BlockSpec((1, 128), lambda j, b: (0, j))],
                  out_shape=[jax.ShapeDtypeStruct((T, C), _ACT), jax.ShapeDtypeStruct((4, C), F32),
                             jax.ShapeDtypeStruct((1, C), F32)],
                  compiler_params=_cp(("parallel", "arbitrary")))(xbc, w, b, dact)


def _acc_rows(ref, r, val, first):
    @pl.when(first)
    def _():
        ref[r:r + 1, :] = val

    @pl.when(jnp.logical_not(first))
    def _():
        ref[r:r + 1, :] += val


def _tri_consts():
    i = lax.broadcasted_iota(jnp.int32, (CH, CH), 0)
    j = lax.broadcasted_iota(jnp.int32, (CH, CH), 1)
    return (i == j).astype(F32), (j <= i).astype(F32), (i <= j).astype(F32), i >= j


def _ssd_chunk(h, x, dt, Bm, Cm, a, dsk, consts):
    eye, tril, triu, lower = consts
    first_head = lax.broadcasted_iota(jnp.int32, (1, 128), 1) < 64
    sel = lambda u, v: jnp.where(first_head, u, v)
    Bb = Bm.astype(_MXU)
    Cb = Cm.astype(_MXU)
    cb = lax.dot_general(Cb, Bb, (((1,), (1,)), ((), ())), preferred_element_type=F32)
    ys, hn = [], []
    for p in range(2):
        dtm, csm, cs_last, decay, bdec = [], [], [], [], []
        for e in (2 * p, 2 * p + 1):
            d = jnp.broadcast_to(dt[e], (CH, CH))
            adm = d * a[e]
            adt_row = jnp.sum(adm * eye, axis=0, keepdims=True)
            cs_col = jnp.sum(adt_row * tril, axis=1, keepdims=True)
            cs_row = jnp.sum(adm * triu, axis=0, keepdims=True)
            last = jnp.sum(adt_row, axis=1, keepdims=True)
            c = jnp.broadcast_to(cs_col, (CH, CH))
            dtm.append(d)
            csm.append(c)
            cs_last.append(last)
            decay.append((cb * jnp.exp(jnp.where(lower, c - cs_row, -jnp.inf))).astype(_MXU))
            bdec.append((Bm * jnp.exp(last - c)).astype(_MXU))
        xb = (x[p] * sel(dtm[0], dtm[1])).astype(_MXU)
        y_diag = sel(jnp.dot(decay[0], xb, preferred_element_type=F32), jnp.dot(decay[1], xb, preferred_element_type=F32))
        states = [lax.dot_general(b, xb, (((0,), (0,)), ((), ())), preferred_element_type=F32) for b in bdec]
        hn.append(h[p] * sel(jnp.exp(cs_last[0]), jnp.exp(cs_last[1])) + sel(states[0], states[1]))
        y_off = jnp.exp(sel(csm[0], csm[1])) * jnp.dot(Cb, h[p].astype(_MXU), preferred_element_type=F32)
        ys.append(y_diag + y_off + sel(dsk[2 * p], dsk[2 * p + 1]) * x[p])
    return ys, hn


def _ssd_specs(order):
    def im(f):
        return lambda p, q: f(*order(p, q))
    xs = pl.BlockSpec((S, 256), im(lambda b, g: (b, g)))
    dt = pl.BlockSpec((None, S, 4), im(lambda b, g: (g, b, 0)))
    bc = pl.BlockSpec((S, 128), im(lambda b, g: (b, g)))
    hd = pl.BlockSpec((None, 1, 4), im(lambda b, g: (g, 0, 0)))
    hs = pl.BlockSpec((None, None, S // CH, 2, 128, 128), im(lambda b, g: (b, g, 0, 0, 0, 0)))
    bw = pl.BlockSpec((S, 128), im(lambda b, g: (b, 8 + g)))
    cw = pl.BlockSpec((S, 128), im(lambda b, g: (b, 12 + g)))
    return xs, dt, bc, hd, hs, bw, cw


def _ssd_fwd(act, dtg, a, dsk, name):
    xs = bm = cm = act
    T = xs.shape[0]
    B = T // S
    nc = S // CH

    def body(x_ref, dt_ref, b_ref, c_ref, a_ref, k_ref, y_ref, hs_ref, h_ref):
        consts = _tri_consts()
        h_ref[...] = jnp.zeros_like(h_ref)
        al = [a_ref[:, e:e + 1] for e in range(4)]
        kl = [k_ref[:, e:e + 1] for e in range(4)]

        def step(c, carry):
            r0 = pl.multiple_of(c * CH, CH)
            rows = pl.ds(r0, CH)
            h = [h_ref[p] for p in range(2)]
            for p in range(2):
                hs_ref[c, p] = h[p]
            x = [x_ref[rows, 128 * p:128 * p + 128] for p in range(2)]
            dt = [dt_ref[rows, e:e + 1] for e in range(4)]
            ys, hn = _ssd_chunk(h, x, dt, b_ref[rows, :], c_ref[rows, :], al, kl, consts)
            for p in range(2):
                y_ref[rows, 128 * p:128 * p + 128] = ys[p]
                h_ref[p] = hn[p]
            return carry

        lax.fori_loop(0, nc, step, 0)

    sx, sdt, sbc, shd, shs, sbw, scw = _ssd_specs(lambda b, g: (b, g))
    return _pcall(body, name=name, grid=(B, 4), in_specs=[sx, sdt, sbw, scw, shd, shd], out_specs=[sx, shs],
                  out_shape=[jax.ShapeDtypeStruct((T, 1024), F32), jax.ShapeDtypeStruct((B, 4, nc, 2, 128, 128), F32)],
                  scratch_shapes=[pltpu.VMEM((2, 128, 128), F32)],
                  compiler_params=_cp(("parallel", "parallel")))(xs, dtg, bm, cm, a, dsk)


def _lane_place(vals, width):
    lane = lax.broadcasted_iota(jnp.int32, (1, width), 1)
    out = jnp.zeros((1, width), F32)
    for e, v in enumerate(vals):
        out = out + jnp.where(lane == e, v, 0.0)
    return out


def _ssd_bwd(act, dtg, a, dsk, hs, dy, name):
    xs = bm = cm = act
    T = xs.shape[0]
    B = T // S
    nc = S // CH

    def body(x_ref, dt_ref, b_ref, c_ref, a_ref, k_ref, hs_ref, dy_ref,
             dx_ref, ddt_ref, db_ref, dc_ref, dak_ref, dh_ref, sc_ref):
        bi = pl.program_id(1)
        consts = _tri_consts()
        dh_ref[...] = jnp.zeros_like(dh_ref)
        sc_ref[...] = jnp.zeros_like(sc_ref)
        al = [a_ref[:, e:e + 1] for e in range(4)]
        kl = [k_ref[:, e:e + 1] for e in range(4)]

        def step(i, carry):
            c = nc - 1 - i
            r0 = pl.multiple_of(c * CH, CH)
            rows = pl.ds(r0, CH)
            h = [hs_ref[c, p] for p in range(2)]
            x = [x_ref[rows, 128 * p:128 * p + 128] for p in range(2)]
            dt = [dt_ref[rows, e:e + 1] for e in range(4)]
            f = functools.partial(_ssd_chunk, consts=consts)
            _, vjp = jax.vjp(f, h, x, dt, b_ref[rows, :], c_ref[rows, :], al, kl)
            dys = [dy_ref[rows, 128 * p:128 * p + 128] for p in range(2)]
            dhn = [dh_ref[p] for p in range(2)]
            dh, dx, ddt, dB, dC, da, dk = vjp((dys, dhn))
            for p in range(2):
                dh_ref[p] = dh[p]
                dx_ref[rows, 128 * p:128 * p + 128] = dx[p]
            for e in range(4):
                ddt_ref[rows, e:e + 1] = ddt[e]
            db_ref[rows, :] = dB
            dc_ref[rows, :] = dC
            sc_ref[0:1, :] += _lane_place(da, 128)
            sc_ref[1:2, :] += _lane_place(dk, 128)
            return carry

        lax.fori_loop(0, nc, step, 0)
        first = bi == 0

        @pl.when(first)
        def _():
            dak_ref[...] = sc_ref[...]

        @pl.when(jnp.logical_not(first))
        def _():
            dak_ref[...] += sc_ref[...]

    sx, sdt, sbc, shd, shs, sbw, scw = _ssd_specs(lambda g, b: (b, g))
    return _pcall(body, name=name, grid=(4, B), in_specs=[sx, sdt, sbw, scw, shd, shd, shs, sx],
                  out_specs=[sx, sdt, sbc, sbc, pl.BlockSpec((None, 8, 128), lambda g, b: (g, 0, 0))],
                  out_shape=[jax.ShapeDtypeStruct((T, 1024), F32), jax.ShapeDtypeStruct((4, T, 4), F32),
                             jax.ShapeDtypeStruct((T, 512), F32), jax.ShapeDtypeStruct((T, 512), F32),
                             jax.ShapeDtypeStruct((4, 8, 128), F32)],
                  scratch_shapes=[pltpu.VMEM((2, 128, 128), F32), pltpu.VMEM((8, 128), F32)],
                  compiler_params=_cp(("parallel", "arbitrary")))(xs, dtg, bm, cm, a, dsk, hs, dy)


def _gate_norm(y, z, nw):
    t = y * _silu(z)
    return t * lax.rsqrt(jnp.mean(t * t, axis=-1, keepdims=True) + SSD_EPS) * nw


def _ssd_gate_norm(y, z, nw, name, zcol=0):
    def fn(tv, fv):
        return [[_gate_norm(tv[g], tv[4 + g], fv[0][:, 256 * g:256 * g + 256]) for g in range(4)]], []
    tiled = [(y, 256, g) for g in range(4)] + [(z, 256, zcol + g) for g in range(4)]
    return _rowwise(fn, tiled, [nw], [(1024, _ACT)], [], name=name)[0]


def _ssd_gate_norm_bwd(y, z, nw, dout, name, zcol=0):
    def fn(tv, fv):
        dys, dzs, dns = [], [], []
        for g in range(4):
            _, vjp = jax.vjp(_gate_norm, tv[g], tv[4 + g], fv[0][:, 256 * g:256 * g + 256])
            a, b, c = vjp(tv[8 + g])
            dys.append(a)
            dzs.append(b)
            dns.append(c)
        return [dys, dzs], [dns]
    tiled = [(y, 256, g) for g in range(4)] + [(z, 256, zcol + g) for g in range(4)] + [(dout, 256, g) for g in range(4)]
    return _rowwise(fn, tiled, [nw], [(1024, F32), (1024, _ACT)], [(1, 1024)], name=name)


def _t5_bucket_np(dist):
    dist = np.maximum(dist, 0)
    max_exact = 16
    large = max_exact + (np.log(np.maximum(dist, 1) / max_exact) / np.log(2048 / max_exact) * (32 - max_exact)).astype(np.int32)
    large = np.minimum(large, 31)
    return np.where(dist < max_exact, dist, large).astype(np.int32)


def _bucket_maps():
    qi = np.arange(128)[:, None]
    kj = np.arange(256)[None, :]
    return np.stack([_t5_bucket_np((qi - kj + 128) * dil) for dil in ATTN_DILS]).astype(np.int32)


def _bias_build(rel_bias, maps, name):
    def body(tab_ref, map_ref, o_ref):
        hh = pl.program_id(0)
        m = map_ref[...]
        acc = jnp.zeros((128, 256), F32)
        for b in range(32):
            acc = jnp.where(m == b, tab_ref[b, hh], acc)
        o_ref[...] = acc

    return _pcall(body, name=name, grid=(12,),
                  in_specs=[pl.BlockSpec(memory_space=pltpu.SMEM), pl.BlockSpec((None, 128, 256), lambda h: (h // 4, 0, 0))],
                  out_specs=pl.BlockSpec((None, 128, 256), lambda h: (h, 0, 0)),
                  out_shape=jax.ShapeDtypeStruct((12, 128, 256), F32), compiler_params=_cp(("parallel",)))(rel_bias, maps)


def _bias_reduce(dbias, maps, name):
    nl = dbias.shape[0]

    def body(d_ref, map_ref, o_ref):
        m = map_ref[...]
        d = d_ref[0]
        for i in range(1, nl):
            d = d + d_ref[i]
        lane = lax.broadcasted_iota(jnp.int32, (1, 128), 1)
        out = jnp.zeros((1, 128), F32)
        for b in range(32):
            s = jnp.sum(jnp.sum(jnp.where(m == b, d, 0.0), axis=1, keepdims=True), axis=0, keepdims=True)
            out = out + jnp.where(lane == b, s, 0.0)
        o_ref[...] = out

    return _pcall(body, name=name, grid=(12,),
                  in_specs=[pl.BlockSpec((nl, None, 128, 256), lambda h: (0, h, 0, 0)),
                            pl.BlockSpec((None, 128, 256), lambda h: (h // 4, 0, 0))],
                  out_specs=pl.BlockSpec((None, 1, 128), lambda h: (h, 0, 0)),
                  out_shape=jax.ShapeDtypeStruct((12, 1, 128), F32), compiler_params=_cp(("parallel",)))(dbias, maps)


def _attn_block(q, kb, vb, bias, mask):
    s = lax.dot_general(q.astype(_MXU), kb.astype(_MXU), (((1,), (1,)), ((), ())), preferred_element_type=F32) * 0.125 + bias
    s = jnp.where(mask, s, -jnp.inf)
    m = lax.stop_gradient(jnp.max(s, axis=-1, keepdims=True))
    p = jnp.exp(s - m)
    den = jnp.sum(p, axis=-1, keepdims=True)
    out = jnp.dot((p / den).astype(_MXU), vb.astype(_MXU), preferred_element_type=F32)
    return out, m + jnp.log(den)


ATTN_QB = 512


def _attn_masks(dil):
    qi = lax.broadcasted_iota(jnp.int32, (ATTN_QB, ATTN_QB + 128), 0)
    kj = lax.broadcasted_iota(jnp.int32, (ATTN_QB, ATTN_QB + 128), 1)
    band = (kj >= qi) & (kj <= qi + 128)
    if dil == 16:
        q2 = lax.broadcasted_iota(jnp.int32, (ATTN_QB, ATTN_QB), 0)
        k2 = lax.broadcasted_iota(jnp.int32, (ATTN_QB, ATTN_QB), 1)
        return ((q2 // 128) == (k2 // 128)) & (k2 <= q2), None
    return band[:, 128:], band


def _attn_wide_bias(b, dil):
    if dil == 16:
        return jnp.tile(b[:, 128:], (4, 4)), None
    z = jnp.zeros((128, 128), F32)
    band = jnp.concatenate([jnp.concatenate([z] * i + [b] + [z] * (3 - i), axis=1) for i in range(4)], axis=0)
    return band[:, 128:], band


def _fold_dbias(dbs, dil, band_form):
    def blk(i, j):
        return dbs[128 * i:128 * i + 128, 128 * j:128 * j + 128]
    if band_form:
        return sum(blk(i, i) for i in range(4)), sum(blk(i, i + 1) for i in range(4))
    cur = sum(blk(i, i) for i in range(4))
    if dil == 16:
        return None, cur
    return sum(blk(i, i - 1) for i in range(1, 4)), cur


def _attn_chunks(dil):
    out = []
    for n in range(S // ATTN_QB):
        if dil == 1 and n > 0:
            out.append((n * ATTN_QB, n * ATTN_QB - 128, ATTN_QB + 128, True))
        else:
            out.append((n * ATTN_QB, n * ATTN_QB, ATTN_QB, False))
    return out


def _qkv_specs(gi, order):
    def spec(base):
        col = (base + 256 * gi) // 128
        return pl.BlockSpec((S, 128), lambda p, q: (order(p, q)[0], col + order(p, q)[1]))
    return [spec(O_Q), spec(O_K), spec(O_V)]


def _residue_rows(r, dil):
    return pl.ds(r, S // dil, stride=dil)


def _attn_fwd(hcat, bias_all, gi, name):
    dil = ATTN_DILS[gi]
    T = hcat.shape[0]
    B, L = T // S, S // dil

    def body(q_ref, k_ref, v_ref, b_ref, o_ref, l_ref, *scr):
        mask_first, mask_band = _attn_masks(dil)
        if dil > 1:
            qs, ks, vs, os_, ls = scr
            for r in range(dil):
                rows, dst = _residue_rows(r, dil), pl.ds(r * L, L)
                qs[dst, :] = q_ref[rows, :]
                ks[dst, :] = k_ref[rows, :]
                vs[dst, :] = v_ref[rows, :]
        else:
            qs, ks, vs, os_, ls = q_ref, k_ref, v_ref, o_ref, l_ref
        ls[...] = jnp.zeros_like(ls)
        first_head = lax.broadcasted_iota(jnp.int32, (1, 128), 1) < 64
        biases = [_attn_wide_bias(b_ref[e], dil) for e in range(2)]
        for q0, k0, kn, band_form in _attn_chunks(dil):
            cur, keys = pl.ds(q0, ATTN_QB), pl.ds(k0, kn)
            qp, kp, vp = qs[cur, :], ks[keys, :], vs[keys, :]
            outs = []
            for e in range(2):
                o, l = _attn_block(jnp.where(first_head == (e == 0), qp, 0.0), kp, vp,
                                   biases[e][1] if band_form else biases[e][0], mask_band if band_form else mask_first)
                outs.append(o)
                ls[cur, e:e + 1] = l
            os_[cur, :] = jnp.where(first_head, outs[0], outs[1])
        if dil > 1:
            for r in range(dil):
                rows, src = _residue_rows(r, dil), pl.ds(r * L, L)
                o_ref[rows, :] = os_[src, :]
                l_ref[rows, :] = ls[src, :]

    scratch = [pltpu.VMEM((S, 128), F32)] * 5 if dil > 1 else []
    return _pcall(body, name=name, grid=(B, 2),
                  in_specs=_qkv_specs(gi, lambda b, hp: (b, hp))
                  + [pl.BlockSpec((2, 128, 256), lambda b, hp: (2 * gi + hp, 0, 0))],
                  out_specs=[pl.BlockSpec((S, 128), lambda b, hp: (b, hp)),
                             pl.BlockSpec((None, S, 128), lambda b, hp: (hp, b, 0))],
                  out_shape=[jax.ShapeDtypeStruct((T, 256), F32), jax.ShapeDtypeStruct((2, T, 128), F32)],
                  scratch_shapes=scratch,
                  compiler_params=_cp(("parallel", "parallel")))(hcat, hcat, hcat, bias_all)


def _attn_bwd(hcat, bias_all, gi, do, dl, name):
    dil = ATTN_DILS[gi]
    T = hcat.shape[0]
    B, L = T // S, S // dil

    def body(q_ref, k_ref, v_ref, b_ref, do_ref, dl_ref, dq_ref, dk_ref, dv_ref, db_ref, acc_ref, *scr):
        bi = pl.program_id(1)
        mask_first, mask_band = _attn_masks(dil)
        if dil > 1:
            qs, ks, vs, dos, dls, dqs, dks, dvs = scr
            for r in range(dil):
                rows, dst = _residue_rows(r, dil), pl.ds(r * L, L)
                qs[dst, :] = q_ref[rows, :]
                ks[dst, :] = k_ref[rows, :]
                vs[dst, :] = v_ref[rows, :]
                dos[dst, :] = do_ref[rows, :]
                dls[dst, :] = dl_ref[rows, :]
        else:
            qs, ks, vs, dos, dls, dqs, dks, dvs = q_ref, k_ref, v_ref, do_ref, dl_ref, dq_ref, dk_ref, dv_ref
        dks[...] = jnp.zeros_like(dks)
        dvs[...] = jnp.zeros_like(dvs)
        first_head = lax.broadcasted_iota(jnp.int32, (1, 128), 1) < 64
        for e in range(2):
            mine = first_head == (e == 0)
            bias_first, bias_band = _attn_wide_bias(b_ref[e], dil)
            acc_ref[...] = jnp.zeros_like(acc_ref)
            for q0, k0, kn, band_form in _attn_chunks(dil):
                cur, keys = pl.ds(q0, ATTN_QB), pl.ds(k0, kn)
                mask = mask_band if band_form else mask_first
                f = lambda qp, kp, vp, b: _attn_block(jnp.where(mine, qp, 0.0), kp, vp, b, mask)
                _, vjp = jax.vjp(f, qs[cur, :], ks[keys, :], vs[keys, :], bias_band if band_form else bias_first)
                dq, dkb, dvb, dbs = vjp((jnp.where(mine, dos[cur, :], 0.0), dls[cur, e:e + 1]))
                if e == 0:
                    dqs[cur, :] = dq
                else:
                    dqs[cur, :] += dq
                dks[keys, :] += dkb
                dvs[keys, :] += dvb
                prev, here = _fold_dbias(dbs, dil, band_form)
                if prev is not None:
                    acc_ref[:, 0:128] += prev
                acc_ref[:, 128:256] += here

            @pl.when(bi == 0)
            def _(e=e):
                db_ref[e] = acc_ref[...]

            @pl.when(bi > 0)
            def _(e=e):
                db_ref[e] += acc_ref[...]

        if dil > 1:
            for r in range(dil):
                rows, src = _residue_rows(r, dil), pl.ds(r * L, L)
                dq_ref[rows, :] = dqs[src, :]
                dk_ref[rows, :] = dks[src, :]
                dv_ref[rows, :] = dvs[src, :]

    order = lambda hp, b: (b, hp)
    blk = pl.BlockSpec((S, 128), lambda hp, b: (b, hp))
    lblk = pl.BlockSpec((None, S, 128), lambda hp, b: (hp, b, 0))
    sds = jax.ShapeDtypeStruct((T, 256), F32)
    scratch = [pltpu.VMEM((128, 256), F32)] + ([pltpu.VMEM((S, 128), F32)] * 8 if dil > 1 else [])
    return _pcall(body, name=name, grid=(2, B),
                  in_specs=_qkv_specs(gi, order) + [pl.BlockSpec((2, 128, 256), lambda hp, b: (2 * gi + hp, 0, 0)), blk, lblk],
                  out_specs=[blk, blk, blk, pl.BlockSpec((2, 128, 256), lambda hp, b: (hp, 0, 0))],
                  out_shape=[sds, sds, sds, jax.ShapeDtypeStruct((4, 128, 256), F32)],
                  scratch_shapes=scratch,
                  compiler_params=_cp(("parallel", "arbitrary")))(hcat, hcat, hcat, bias_all, do, dl)


def _lse_merge(o0, o1, o2, l0, l1, l2):
    m = lax.stop_gradient(jnp.maximum(jnp.maximum(l0, l1), l2))
    e0, e1, e2 = jnp.exp(l0 - m), jnp.exp(l1 - m), jnp.exp(l2 - m)
    den = e0 + e1 + e2
    return (e0 / den) * o0 + (e1 / den) * o1 + (e2 / den) * o2


def _attn_merge(outs, lses, dy, name):
    T = outs[0].shape[0]
    bwd = dy is not None
    tm = 512

    def body(*refs):
        o_refs, l_refs = refs[:3], refs[3:6]
        if bwd:
            for r in refs[10:13]:
                r[...] = jnp.zeros_like(r)
        for e in range(2):
            lanes = slice(64 * e, 64 * e + 64)
            vals = [r[:, lanes] for r in o_refs] + [r[:, e:e + 1] for r in l_refs]
            if not bwd:
                refs[6][:, lanes] = _lse_merge(*vals).astype(refs[6].dtype)
            else:
                _, vjp = jax.vjp(_lse_merge, *vals)
                g = vjp(refs[6][:, lanes])
                for r, v in zip(refs[7:10], g[:3]):
                    r[:, lanes] = v
                for r, v in zip(refs[10:13], g[3:]):
                    r[:, e:e + 1] = v

    blk = pl.BlockSpec((tm, 128), lambda i, hp: (i, hp))
    lblk = pl.BlockSpec((None, tm, 128), lambda i, hp: (hp, i, 0))
    lsd = jax.ShapeDtypeStruct((2, T, 128), F32)
    if not bwd:
        return _pcall(body, name=name, grid=(T // tm, 2), in_specs=[blk] * 3 + [lblk] * 3, out_specs=blk,
                      out_shape=jax.ShapeDtypeStruct((T, 256), F32),
                      compiler_params=_cp(("parallel", "parallel")))(*outs, *lses)
    return _pcall(body, name=name, grid=(T // tm, 2), in_specs=[blk] * 3 + [lblk] * 3 + [blk],
                  out_specs=[blk] * 3 + [lblk] * 3, out_shape=[jax.ShapeDtypeStruct((T, 256), F32)] * 3 + [lsd] * 3,
                  compiler_params=_cp(("parallel", "parallel")))(*outs, *lses, dy)


def _gmerge(g0, g1, g2, gb, ya, yb, yc):
    return (jax.nn.sigmoid(g0 + gb[:, 0:D]) * ya + jax.nn.sigmoid(g1 + gb[:, D:2 * D]) * yb
            + jax.nn.sigmoid(g2 + gb[:, 2 * D:3 * D]) * yc)


def _gated_merge(gates, gb, ya, yb, yc, name, gcol=0):
    def fn(tv, fv):
        return [_gmerge(tv[0], tv[1], tv[2], fv[0], tv[3], tv[4], tv[5])], []
    return _rowwise(fn, [(gates, D, gcol), (gates, D, gcol + 1), (gates, D, gcol + 2), ya, yb, yc], [gb], [(D, _ACT)], [],
                    name=name)[0]


def _gated_merge_bwd(gates, gb, ya, yb, yc, dm, name, gcol=0):
    def fn(tv, fv):
        _, vjp = jax.vjp(_gmerge, tv[0], tv[1], tv[2], fv[0], tv[3], tv[4], tv[5])
        d0, d1, d2, dgb, da, db, dc = vjp(tv[6])
        return [[d0, d1, d2], da, db, dc], [dgb]
    return _rowwise(fn, [(gates, D, gcol), (gates, D, gcol + 1), (gates, D, gcol + 2), ya, yb, yc, dm], [gb],
                    [(3 * D, _ACT), (D, _ACT), (D, _ACT), (D, _ACT)], [(1, 3 * D)], name=name)


def _pool_affine(t1, pb, ps, dout, name):
    if dout is None:
        def fn(tv, fv):
            return [(tv[0] + fv[0]) * fv[1]], []
        return _rowwise(fn, [t1], [pb, ps], [(POOLW, _ACT)], [], name=name)[0]

    def fnb(tv, fv):
        t2, vjp = jax.vjp(lambda t, b, s: (t + b) * s, tv[0], fv[0], fv[1])
        dt, db, dsc = vjp(tv[1])
        return [dt, t2], [db, dsc]
    return _rowwise(fnb, [t1, dout], [pb, ps], [(POOLW, _ACT), (POOLW, _ACT)], [(1, POOLW), (1, POOLW)], name=name)


def _dt_softplus(dt_raw, dt_bias, ddt, name):
    f = lambda r, b: _softplus(r + b)
    if ddt is None:
        def fn(tv, fv):
            return [f(tv[0], fv[0])], []
        return _rowwise(fn, [dt_raw], [dt_bias], [(16, F32)], [], name=name, tm=1024)[0]

    def fnb(tv, fv):
        _, vjp = jax.vjp(f, tv[0], fv[0])
        dr, db = vjp(tv[1])
        return [dr], [db]
    return _rowwise(fnb, [dt_raw, ddt], [dt_bias], [(16, F32)], [(1, 16)], name=name, tm=1024)


def _adamw_math(wv, gv, mv, vv):
    c1 = 1.0 / (1.0 - ADAM_B1 ** ADAM_STEP)
    c2 = 1.0 / (1.0 - ADAM_B2 ** ADAM_STEP)
    mn = ADAM_B1 * mv + (1.0 - ADAM_B1) * gv
    vn = ADAM_B2 * vv + (1.0 - ADAM_B2) * (gv * gv)
    delta = -ADAM_LR * ((mn * c1) / (jnp.sqrt(vn * c2) + ADAM_EPS) + ADAM_WD * wv)
    return delta, mn, vn


def _adamw(w, g, m, v, name):
    R, C = w.shape
    tm = _pick(R, (256, 128, 64, 32, 16, 8))
    return _rowwise(lambda tv, fv: (list(_adamw_math(*tv)), []), [w, g, m, v], [], [(C, F32)] * 3, [], name=name, tm=tm)


def _adamw_layer(i, w, g, m, v, accs, name, dep=None):
    R, C = w.shape
    r = R // NL
    tm = _pick(r, (256, 128, 64, 32, 16, 8))
    nt = r // tm
    if accs is None:
        accs = [lax.empty((R, C), F32) for _ in range(4)]
    extra = [] if dep is None else [dep]

    def body(w_ref, g_ref, m_ref, v_ref, *rest):
        go_ref, do_ref, mo_ref, vo_ref = rest[-4:]
        gv = g_ref[...]
        delta, mn, vn = _adamw_math(w_ref[...], gv, m_ref[...], v_ref[...])
        go_ref[...] = gv
        do_ref[...] = delta
        mo_ref[...] = mn
        vo_ref[...] = vn

    slab = pl.BlockSpec((tm, C), lambda t: (i * nt + t, 0))
    anyspec = pl.BlockSpec(memory_space=pl.ANY)
    return _pcall(body, name=name, grid=(nt,),
                  in_specs=[slab, pl.BlockSpec((tm, C), lambda t: (t, 0)), slab, slab] + [anyspec] * (4 + len(extra)),
                  out_specs=[slab] * 4, out_shape=[jax.ShapeDtypeStruct((R, C), F32)] * 4,
                  input_output_aliases={4 + k: k for k in range(4)},
                  compiler_params=_cp(("parallel",)))(w, g, m, v, *accs, *extra)


def _ffn_fwd(x, xm, w13, w2, g, b, tag, dep=None):
    ha, hg, s = _swiglu_fwd(xm, w13, dep, name=f"{tag}_h")
    r, out, outm = _mm_res_ln(s, w2, x, g, b, 0.5, name=f"{tag}_y")
    return out, outm, dict(x=xm, ha=ha, hg=hg, r=r)


def _ffn_bwd(dout, sv, w13, w2, g, b, tag, dep=None):
    dskip, dy, dg, db = _ln_bwd(sv['r'], g, b, dout, 0.5, name=f"{tag}_lnb")
    ds = _mm(dy, w2, tb=True, dep=dep, name=f"{tag}_ds")
    dh, s = _swiglu_act_bwd(sv['ha'], sv['hg'], ds, name=f"{tag}_actb")
    dw2 = _mm(s, dy, ta=True, name=f"{tag}_dw2")
    dw13 = _mm(sv['x'], dh, ta=True, name=f"{tag}_dw13")
    dx = _mm(dh, w13, tb=True, add=dskip, name=f"{tag}_dx")
    return dx, dict(w13=dw13, w2=dw2, g=dg, b=db)


def _mixer_fwd(x1, x1m, W, bias_all, tag, dep=None):
    T = x1.shape[0]
    hcat = _mm(x1m, W['w_in_r'], dep=dep, name=f"{tag}_hcat")
    dt_raw = hcat[:, O_DT:O_DT + 16]
    pooled = _pool_mean(hcat, False, name=f"{tag}_pool", col0=O_U // 128)
    t1 = _mm(pooled, W['pool_wbd'], name=f"{tag}_pt1")
    t2 = _pool_affine(t1, W['pool_b'], W['pool_scale'], None, name=f"{tag}_paff")
    ya = _mm(t2, W['p_pool'], name=f"{tag}_ya")
    act = _conv_silu(hcat, W['conv_w'], W['conv_b'], name=f"{tag}_conv", col0=O_XBC // 128)
    dt = _dt_softplus(dt_raw, W['dt_bias'], None, name=f"{tag}_dt")
    dtg = dt.reshape(T, 4, 4).transpose(1, 0, 2)
    yscan, hs = _ssd_fwd(act, dtg, W['a_neg'], W['d_skip'], name=f"{tag}_ssd")
    ybn = _ssd_gate_norm(yscan, hcat, W['ssd_norm'], name=f"{tag}_gn", zcol=O_Z // 256)
    yb = _mm(ybn, W['p_ssd'], name=f"{tag}_yb")
    outs, lses = [], []
    for gi in range(len(ATTN_DILS)):
        o, l = _attn_fwd(hcat, bias_all, gi, name=f"{tag}_attn{gi}")
        outs.append(o)
        lses.append(l)
    ycp = _attn_merge(outs, lses, None, name=f"{tag}_amerge")
    yc = _mm(ycp, W['p_attn'], name=f"{tag}_yc")
    merged = _gated_merge(hcat, W['gate_b'], ya, yb, yc, name=f"{tag}_gm", gcol=O_G // D)
    r, out, outm = _mm_res_ln(merged, W['w_out'], x1, W['ln2_g'], W['ln2_b'], 1.0, name=f"{tag}_mix")
    sv = dict(x1=x1m, dt_raw=dt_raw, pooled=pooled, t1=t1, act=act, dtg=dtg,
              hs=hs, yscan=yscan, ybn=ybn, hcat=hcat, outs=outs, lses=lses, ycp=ycp, ya=ya, yb=yb, yc=yc,
              merged=merged, r=r)
    return out, outm, sv


def _mixer_bwd(dout, sv, W, bias_all, tag, dep=None):
    T = dout.shape[0]
    gr = {}
    dx1a, dr, gr['ln2_g'], gr['ln2_b'] = _ln_bwd(sv['r'], W['ln2_g'], W['ln2_b'], dout, 1.0, name=f"{tag}_lnb")
    dmerged = _mm(dr, W['w_out'], tb=True, dep=dep, name=f"{tag}_dmerged")
    gr['w_out'] = _mm(sv['merged'], dr, ta=True, name=f"{tag}_dwout")
    dgates, dya, dyb, dyc, gr['gate_b'] = _gated_merge_bwd(sv['hcat'], W['gate_b'], sv['ya'], sv['yb'], sv['yc'],
                                                           dmerged, name=f"{tag}_gmb", gcol=O_G // D)
    dycp = _mm(dyc, W['p_attn'], tb=True, name=f"{tag}_dycp")
    gr['p_attn'] = _mm(sv['ycp'], dyc, ta=True, name=f"{tag}_dpattn")
    dml = _attn_merge(sv['outs'], sv['lses'], dycp, name=f"{tag}_amergeb")
    dq, dk, dv, dbias = [], [], [], []
    for gi in range(len(ATTN_DILS)):
        a, b, c, d = _attn_bwd(sv['hcat'], bias_all, gi, dml[gi], dml[3 + gi], name=f"{tag}_attnb{gi}")
        dq.append(a)
        dk.append(b)
        dv.append(c)
        dbias.append(d)
    dbias = jnp.concatenate(dbias, axis=0)
    dybn = _mm(dyb, W['p_ssd'], tb=True, name=f"{tag}_dybn")
    gr['p_ssd'] = _mm(sv['ybn'], dyb, ta=True, name=f"{tag}_dpssd")
    dyscan, dz, gr['ssd_norm'] = _ssd_gate_norm_bwd(sv['yscan'], sv['hcat'], W['ssd_norm'], dybn, name=f"{tag}_gnb",
                                                    zcol=O_Z // 256)
    dxs, ddtg, dbm, dcm, dak = _ssd_bwd(sv['act'], sv['dtg'], W['a_neg'], W['d_skip'], sv['hs'], dyscan,
                                        name=f"{tag}_ssdb")
    gr['a_neg'], gr['d_skip'] = dak[:, 0, 0:4], dak[:, 1, 0:4]
    ddt = ddtg.transpose(1, 0, 2).reshape(T, 16)
    ddt_raw, gr['dt_bias'] = _dt_softplus(sv['dt_raw'], W['dt_bias'], ddt, name=f"{tag}_dtb")
    dact = jnp.concatenate([dxs, dbm, dcm], axis=1)
    dxbc, gr['conv_w'], gr['conv_b'] = _conv_silu_bwd(sv['hcat'], W['conv_w'], W['conv_b'], dact, name=f"{tag}_convb",
                                                      col0=O_XBC // 128)
    dt2 = _mm(dya, W['p_pool'], tb=True, name=f"{tag}_dt2")
    dt1, t2, gr['pool_b'], gr['pool_scale'] = _pool_affine(sv['t1'], W['pool_b'], W['pool_scale'], dt2, name=f"{tag}_paffb")
    gr['p_pool'] = _mm(t2, dya, ta=True, name=f"{tag}_dppool")
    dpooled = _mm(dt1, W['pool_wbd'], tb=True, name=f"{tag}_dpooled")
    gr['pool_wbd'] = _mm(sv['pooled'], dt1, ta=True, name=f"{tag}_dpoolw")
    du = _pool_mean(dpooled, True, name=f"{tag}_poolb")
    dhcat = jnp.concatenate([t.astype(_ACT) for t in [du, dz, dxbc] + dq + dk + dv + [dgates, ddt_raw]]
                            + [jnp.zeros((T, HC - O_DT - 16), _ACT)], axis=1)
    dx1 = _mm(dhcat, W['w_in_r'], tb=True, add=dx1a, name=f"{tag}_dx1")
    gr['w_in_r'] = _mm(sv['x1'], dhcat, ta=True, name=f"{tag}_dwin")
    return dx1, gr, dbias


def _prep_layer_weights(i, inp, G):
    W = {}
    for n in BIG:
        if n not in G:
            continue
        g = G[n]
        if n == 'w_in':
            W['w_in_r'] = jnp.concatenate(_nat_pieces(g, 0, 3840) + _nat_pieces(g, 3856, 9232) + _nat_pieces(g, 3840, 3856)
                                          + [jnp.zeros((D, HC - 9232), g.dtype)], axis=1)
        elif n in COL_SHARDED:
            W[n] = jnp.concatenate([g[j] for j in range(4)], axis=1)
        else:
            W[n] = g.reshape(4 * g.shape[1], g.shape[2])
    pw = inp['pool_w'][i].astype(_MXU)
    wbd = jnp.zeros((POOLW, POOLW), _MXU)
    for g in range(4):
        wbd = lax.dynamic_update_slice(wbd, pw[g], (g * POOL_GDIM, g * POOL_GDIM))
    W['pool_wbd'] = wbd
    W['pool_b'] = inp['pool_b'][i].reshape(1, POOLW)
    W['pool_scale'] = inp['pool_scale'][i].reshape(1, POOLW)
    if 'conv_w' in G:
        W['conv_w'] = jnp.concatenate([G['conv_w'][j] for j in range(4)], axis=1)
        W['gate_b'] = jnp.concatenate([G['gate_b'][j][b:b + 1] for b in range(3) for j in range(4)], axis=1)
    W['conv_b'] = inp['conv_b'][i].reshape(1, 2048)
    W['dt_bias'] = inp['dt_bias'][i].reshape(1, 16)
    W['a_neg'] = (-jnp.exp(inp['a_log'][i])).reshape(4, 1, 4)
    W['d_skip'] = inp['d_skip'][i].reshape(4, 1, 4)
    W['ssd_norm'] = inp['ssd_norm'][i].reshape(1, D)
    for n in ('ln1_g', 'ln1_b', 'ln2_g', 'ln2_b', 'ln3_g', 'ln3_b'):
        W[n] = inp[n][i].reshape(1, D)
    return W


GATHER_FIRST = ['ffn1_w13', 'ffn1_w2']
GATHER_REST = [n for n in BIG if n not in GATHER_FIRST] + ['gate_b', 'conv_w']


def _gather_start(inp, i, names):
    core = lax.axis_index("c")
    arrs = []
    for n in names:
        s = inp[n][i]
        if n in BIG:
            s = lax.dynamic_slice_in_dim(s, core * (s.shape[0] // 2), s.shape[0] // 2, axis=0).astype(BF16)
        arrs.append(s)
    state, token = _exchange_start(arrs, "chips", "gather", name="gather_start")
    return (names, state), token


def _gather_mid(handle, after):
    names, state = handle
    me = 2 * lax.axis_index("x") + lax.axis_index("y")
    own, outs = _exchange_wait(state, after, "chips", "gather", name="gather_wait")
    outs = [lax.dynamic_update_slice(o, a[None], (me, 0, 0)) for o, a in zip(outs, own)]
    big = [o for n, o in zip(names, outs) if n in BIG]
    state, token = _exchange_start(big, "cores", "gather", name="share_start")
    return (names, outs, state), token


def _gather_finish(handle, after):
    names, outs, state = handle
    core = lax.axis_index("c")
    mine, theirs = _exchange_wait(state, after, "cores", "gather", name="share_wait")
    G = {n: o for n, o in zip(names, outs) if n not in BIG}
    for n, a, b in zip([n for n in names if n in BIG], mine, theirs):
        G[n] = jnp.concatenate([jnp.where(core == 0, a, b), jnp.where(core == 0, b, a)], axis=1)
    return G


W_IN_SHARD = 2308


def _nat_pieces(g, lo, hi):
    out = []
    for j in range(4):
        s, e = max(lo, W_IN_SHARD * j), min(hi, W_IN_SHARD * (j + 1))
        if s < e:
            out.append(g[j][:, s - W_IN_SHARD * j:e - W_IN_SHARD * j])
    return out


def _reord_ranges(lo, hi):
    out = []
    for a, b, off in ((0, 3840, 0), (3840, 3856, O_DT - 3840), (3856, 9232, -16)):
        s, e = max(lo, a), min(hi, b)
        if s < e:
            out.append((s + off, e + off))
    return out


def _halves_of(n, g):
    if n == 'w_in':
        shards = [jnp.concatenate([g[:, a:b] for a, b in _reord_ranges(W_IN_SHARD * j, W_IN_SHARD * (j + 1))], axis=1)
                  for j in range(4)]
    elif n in COL_SHARDED:
        c = g.shape[1] // 4
        shards = [g[:, j * c:(j + 1) * c] for j in range(4)]
    else:
        r = g.shape[0] // 4
        shards = [g[j * r:(j + 1) * r] for j in range(4)]
    r2 = shards[0].shape[0] // 2
    return jnp.stack([jnp.concatenate([s[h * r2:(h + 1) * r2] for s in shards], axis=0) for h in range(2)])


def _reduce_a(grads):
    names = list(grads)
    halves = [_halves_of(n, grads[n]) for n in names]
    state, token = _exchange_start(halves, "cores", "scatter", name="rsc_start")
    return (names, state), token


def _reduce_b(handle, after):
    names, state = handle
    core = lax.axis_index("c").reshape(1)
    halves, got = _exchange_wait(state, after, "cores", "scatter", name="rsc_wait")
    chip = [_sum_own_recv(h, t, core, BF16, name="rs_sum2") for h, t in zip(halves, got)]
    chip = [t.reshape(4, t.shape[0] // 4, t.shape[1]) for t in chip]
    state, token = _exchange_start(chip, "chips", "scatter", name="rs_start")
    return (names, state), token


def _reduce_c(handle, after):
    names, state = handle
    chip_id = (2 * lax.axis_index("x") + lax.axis_index("y")).reshape(1)
    chip, got = _exchange_wait(state, after, "chips", "scatter", name="rs_wait")
    red = [_sum_own_recv(h, t, chip_id, F32, name="rs_sum4") for h, t in zip(chip, got)]
    other = _exchange(red, "cores", "gather", name="rs_share")
    out = {}
    for n, mine, theirs in zip(names, red, other):
        out[n] = jnp.where(lax.axis_index("c") == 0, jnp.concatenate([mine, theirs]), jnp.concatenate([theirs, mine]))
    return out


class _Comm:
    def __init__(self, inp):
        self.inp = inp

    def gather_start(self, i, names):
        return _gather_start(self.inp, i, names)

    gather_mid = staticmethod(_gather_mid)
    gather_finish = staticmethod(_gather_finish)

    def reduce_a(self, i, grads):
        return _reduce_a({n: grads[n] for n in BIG})

    reduce_b = staticmethod(_reduce_b)
    reduce_c = staticmethod(_reduce_c)


def _allreduce_small(vec, dep=None):
    for group in ("cores", "x", "y"):
        recv = _exchange([vec], group, "gather", name=f"ar_{group}", dep=dep if group == "cores" else None)[0]
        vec = _rowwise(lambda tv, fv: ([tv[0] + tv[1]], []), [vec, recv], [], [(128, F32)], [], name=f"ar_add_{group}")[0]
    return vec


def _pack(arrs):
    flat = jnp.concatenate([a.reshape(-1) for a in arrs])
    n = flat.shape[0]
    pad = (-n) % (256 * 128)
    flat = jnp.concatenate([flat, jnp.zeros((pad,), F32)])
    return flat.reshape(-1, 128)


def _unpack(p, shapes):
    flat = p.reshape(-1)
    out, off = [], 0
    for s in shapes:
        sz = int(np.prod(s))
        out.append(flat[off:off + sz].reshape(s))
        off += sz
    return out


def _forward_backward(inp, comm, bias_all):
    x = xm = inp['x'].reshape(-1, D)
    tgt = inp['loss_target'].reshape(-1, D)
    saved, Ws = [], []
    h_first, _ = comm.gather_start(0, GATHER_FIRST)
    h_rest, dep = comm.gather_start(0, GATHER_REST)
    h_first, tok = comm.gather_mid(h_first, x)
    G = comm.gather_finish(h_first, tok)
    for i in range(NL):
        W = _prep_layer_weights(i, inp, G)
        start_next = lambda: (comm.gather_start(i + 1, BIG + ['gate_b', 'conv_w']) if i + 1 < NL else (None, None))
        if i > 0:
            h_next, dep = start_next()
        x1, x1m, s1 = _ffn_fwd(x, xm, W['ffn1_w13'], W['ffn1_w2'], W['ln1_g'], W['ln1_b'], "f1", dep)
        if i == 0:
            h_rest, tok = comm.gather_mid(h_rest, x1m)
            W.update(_prep_layer_weights(i, inp, comm.gather_finish(h_rest, tok)))
            h_next, dep = start_next()
        x2, x2m, s2 = _mixer_fwd(x1, x1m, W, bias_all, "mx", dep if i == 0 else None)
        dep = None
        if h_next is not None:
            h_next, dep = comm.gather_mid(h_next, x2m)
        x, xm, s3 = _ffn_fwd(x2, x2m, W['ffn2_w13'], W['ffn2_w2'], W['ln3_g'], W['ln3_b'], "f2", dep)
        if h_next is not None:
            G = comm.gather_finish(h_next, xm)
        saved.append((s1, s2, s3))
        Ws.append(W)
    dy, lpart = _loss_fwd_bwd(x, tgt, name="loss")
    fins, reduced, dbiases = [None] * NL, [None] * NL, [None] * NL
    pend_a, pend_b, dep = None, None, None
    for i in reversed(range(NL)):
        W = Ws[i]
        s1, s2, s3 = saved[i]
        g = {}
        dx2, f = _ffn_bwd(dy, s3, W['ffn2_w13'], W['ffn2_w2'], W['ln3_g'], W['ln3_b'], "f2", dep)
        g['ffn2_w13'], g['ffn2_w2'], g['ln3_g'], g['ln3_b'] = f['w13'], f['w2'], f['g'], f['b']
        dep = None
        if pend_a is not None:
            handle, dep = comm.reduce_b(pend_a[1], dx2)
            pend_b = (pend_a[0], handle)
        dx1, gm, dbiases[i] = _mixer_bwd(dx2, s2, W, bias_all, "mx", dep)
        g.update(gm)
        dy, f = _ffn_bwd(dx1, s1, W['ffn1_w13'], W['ffn1_w2'], W['ln1_g'], W['ln1_b'], "f1")
        g['ffn1_w13'], g['ffn1_w2'], g['ln1_g'], g['ln1_b'] = f['w13'], f['w2'], f['g'], f['b']
        fins[i] = _finish_layer_grads(i, g, inp)
        if pend_b is not None:
            reduced[pend_b[0]] = comm.reduce_c(pend_b[1], dy)
            pend_b = None
        handle, dep = comm.reduce_a(i, fins[i])
        pend_a = (i, handle)
    return lpart, dy, fins, reduced, pend_a, dbiases, dep


def _finish_layer_grads(i, g, inp):
    out = {n: g[n] for n in BIG if n != 'w_in'}
    out['w_in'] = g['w_in_r']
    out['pool_w'] = jnp.stack([g['pool_wbd'][k * POOL_GDIM:(k + 1) * POOL_GDIM, k * POOL_GDIM:(k + 1) * POOL_GDIM] for k in range(4)])
    out['pool_b'] = g['pool_b'].reshape(4, POOL_GDIM)
    out['pool_scale'] = g['pool_scale'].reshape(POOLW)
    out['conv_w'] = g['conv_w']
    out['conv_b'] = g['conv_b'].reshape(2048)
    out['dt_bias'] = g['dt_bias'].reshape(16)
    out['a_log'] = (g['a_neg'].reshape(16)) * (-jnp.exp(inp['a_log'][i]))
    out['d_skip'] = g['d_skip'].reshape(16)
    out['ssd_norm'] = g['ssd_norm'].reshape(D)
    out['gate_b'] = g['gate_b'].reshape(3, D)
    for n in ('ln1_g', 'ln1_b', 'ln2_g', 'ln2_b', 'ln3_g', 'ln3_b'):
        out[n] = g[n].reshape(D)
    return out


def kernel(x, ffn1_w13, ffn1_w2, ln1_g, ln1_b, w_in, gate_b, pool_w, pool_b, pool_scale, conv_w, conv_b,
           dt_bias, a_log, d_skip, ssd_norm, rel_bias, p_pool, p_ssd, p_attn, w_out, ln2_g, ln2_b, ffn2_w13,
           ffn2_w2, ln3_g, ln3_b, loss_target, m_ffn1_w13, m_ffn1_w2, m_ln1_g, m_ln1_b, m_w_in, m_gate_b,
           m_pool_w, m_pool_b, m_pool_scale, m_conv_w, m_conv_b, m_dt_bias, m_a_log, m_d_skip, m_ssd_norm,
           m_rel_bias, m_p_pool, m_p_ssd, m_p_attn, m_w_out, m_ln2_g, m_ln2_b, m_ffn2_w13, m_ffn2_w2, m_ln3_g,
           m_ln3_b, v_ffn1_w13, v_ffn1_w2, v_ln1_g, v_ln1_b, v_w_in, v_gate_b, v_pool_w, v_pool_b,
           v_pool_scale, v_conv_w, v_conv_b, v_dt_bias, v_a_log, v_d_skip, v_ssd_norm, v_rel_bias, v_p_pool,
           v_p_ssd, v_p_attn, v_w_out, v_ln2_g, v_ln2_b, v_ffn2_w13, v_ffn2_w2, v_ln3_g, v_ln3_b):
    inp = dict(locals())
    maps = jnp.asarray(_bucket_maps())
    bias_all = _bias_build(rel_bias, maps, name="bias_build")
    comm = _Comm(inp)
    lpart, gx, fins, red, pending, dbiases, halves_started = _forward_backward(inp, comm, bias_all)
    loss = lax.psum(lpart[0, 0], ("x", "y", "c"))

    small_l = [n for n in SMALL if n != 'rel_bias']
    drel = _bias_reduce(jnp.stack(dbiases), maps, name="bias_reduce")[:, 0, :32].T
    small_arrs = [jnp.stack([fins[i][n] for i in range(NL)]) for n in small_l] + [drel]
    packed = _allreduce_small(_pack(small_arrs), dep=halves_started)
    handle_b, started = comm.reduce_b(pending[1], packed)
    gsmall = dict(zip(small_l + ['rel_bias'], _unpack(packed, [a.shape for a in small_arrs])))
    shard = 2 * lax.axis_index("x") + lax.axis_index("y")
    gsmall['gate_b'] = lax.dynamic_slice_in_dim(gsmall['gate_b'], shard * 256, 256, axis=2)
    gsmall['conv_w'] = lax.dynamic_slice_in_dim(gsmall['conv_w'], shard * 512, 512, axis=2)
    gout, delta, new_m, new_v = dict(gsmall), {}, {}, {}
    shapes = [inp[n].shape for n in SMALL]
    d, m, v = _adamw(_pack([inp[n] for n in SMALL]), _pack([gsmall[n] for n in SMALL]),
                     _pack([inp['m_' + n] for n in SMALL]), _pack([inp['v_' + n] for n in SMALL]), name="adamw_small")
    for n, dd, mm, vv in zip(SMALL, _unpack(d, shapes), _unpack(m, shapes), _unpack(v, shapes)):
        delta[n], new_m[n], new_v[n] = dd, mm, vv

    two_d = lambda a: a.reshape(a.shape[0] * a.shape[1], a.shape[2])
    accs = {n: None for n in BIG}

    def adamw_layer(i, dep=None):
        for n in BIG:
            accs[n] = _adamw_layer(i, two_d(inp[n]), red[i][n], two_d(inp['m_' + n]), two_d(inp['v_' + n]), accs[n],
                                   name="adamw_big", dep=dep)

    done = [i for i in range(NL) if i != pending[0]]
    for i in done:
        adamw_layer(i, dep=started)
    red[pending[0]] = comm.reduce_c(handle_b, [d] + ([accs[n][1] for n in BIG] if done else []))
    adamw_layer(pending[0])
    for n in BIG:
        gout[n], delta[n], new_m[n], new_v[n] = [a.reshape(inp[n].shape) for a in accs[n]]

    return (loss, gx.reshape(x.shape), *[gout[n] for n in WEIGHTS], *[delta[n] for n in WEIGHTS],
            *[new_m[n] for n in WEIGHTS], *[new_v[n] for n in WEIGHTS])
```

```python
import functools

import numpy as np
import jax
import jax.numpy as jnp
from jax import lax
from jax.experimental import pallas as pl
from jax.experimental.pallas import tpu as pltpu

F32 = jnp.float32
BF16 = jnp.bfloat16
_MXU = jnp.bfloat16
_ACT = jnp.bfloat16
_VMEM_LIMIT = 56 * 1024 * 1024

S = 2048
D = 1024
NL = 4
DFF = 2816
LN_EPS = 1e-5
SSD_EPS = 1e-5
ALPHA = (2.0 * NL) ** 0.25
POOLW = 768
POOL_WINDOWS = (2, 4, 8, 16)
POOL_GDIM = 192
CH = 128
ATTN_DILS = (1, 4, 16)
HC = 9728
O_U, O_Z, O_XBC, O_Q, O_K, O_V, O_G, O_DT = 0, 768, 1792, 3840, 4608, 5376, 6144, 9216

ADAM_LR, ADAM_B1, ADAM_B2, ADAM_EPS, ADAM_WD, ADAM_STEP = 0.001, 0.9, 0.999, 1e-08, 0.01, 10

WEIGHTS = ['ffn1_w13', 'ffn1_w2', 'ln1_g', 'ln1_b', 'w_in', 'gate_b', 'pool_w', 'pool_b', 'pool_scale', 'conv_w',
           'conv_b', 'dt_bias', 'a_log', 'd_skip', 'ssd_norm', 'rel_bias', 'p_pool', 'p_ssd', 'p_attn', 'w_out',
           'ln2_g', 'ln2_b', 'ffn2_w13', 'ffn2_w2', 'ln3_g', 'ln3_b']
BIG = ['ffn1_w13', 'ffn1_w2', 'w_in', 'p_pool', 'p_ssd', 'p_attn', 'w_out', 'ffn2_w13', 'ffn2_w2']
COL_SHARDED = {'ffn1_w13', 'ffn2_w13', 'w_in', 'p_pool', 'p_attn'}
SMALL = [n for n in WEIGHTS if n not in BIG]


def _pcall(body, **kw):
    return pl.pallas_call(body, **kw)


def _cp(sem=None):
    return pltpu.CompilerParams(dimension_semantics=sem, vmem_limit_bytes=_VMEM_LIMIT)


def _pick(n, cands):
    for c in cands:
        if n % c == 0:
            return c
    raise ValueError(f"no tile for {n}")


def _mm(a, b, *, ta=False, tb=False, add=None, out_dtype=F32, dep=None, name):
    if ta:
        K, M = a.shape
    else:
        M, K = a.shape
    if tb:
        N, K2 = b.shape
    else:
        K2, N = b.shape
    assert K == K2, (a.shape, b.shape, ta, tb)
    sa, sb, so = a.dtype.itemsize, b.dtype.itemsize, jnp.dtype(out_dtype).itemsize
    tm, tn, tk = _mm_tiles(M, N, K, sa, sb, so + (4 if add is not None else 0))
    nk = K // tk
    a_bytes, b_bytes = M * K * sa, K * N * sb
    j_outer = nk == 1 and (b_bytes + a_bytes * (N // tn) < a_bytes + b_bytes * (M // tm))
    ij = (lambda p, q: (q, p)) if j_outer else (lambda p, q: (p, q))

    def im(f):
        return lambda p, q, k: f(*ij(p, q), k)

    a_spec = pl.BlockSpec((tk, tm), im(lambda i, j, k: (k, i))) if ta else pl.BlockSpec((tm, tk), im(lambda i, j, k: (i, k)))
    b_spec = pl.BlockSpec((tn, tk), im(lambda i, j, k: (j, k))) if tb else pl.BlockSpec((tk, tn), im(lambda i, j, k: (k, j)))
    o_spec = pl.BlockSpec((tm, tn), im(lambda i, j, k: (i, j)))
    dims = (((0 if ta else 1,), (1 if tb else 0,)), ((), ()))
    has_add = add is not None

    n_in = 2 + int(has_add) + int(dep is not None)

    def body(*refs):
        a_ref, b_ref = refs[0], refs[1]
        add_ref = refs[2] if has_add else None
        o_ref = refs[n_in]
        part = lax.dot_general(a_ref[...].astype(_MXU), b_ref[...].astype(_MXU), dims, preferred_element_type=F32)

        def finish(r):
            if has_add:
                r = r + add_ref[...]
            o_ref[...] = r.astype(out_dtype)

        if nk == 1:
            finish(part)
        else:
            acc = refs[-1]
            k = pl.program_id(2)

            @pl.when(k == 0)
            def _():
                acc[...] = part

            @pl.when(k > 0)
            def _():
                acc[...] += part

            @pl.when(k == nk - 1)
            def _():
                finish(acc[...])

    in_specs = [a_spec, b_spec]
    args = [a, b]
    if has_add:
        in_specs.append(o_spec)
        args.append(add)
    if dep is not None:
        in_specs.append(pl.BlockSpec(memory_space=pl.ANY))
        args.append(dep)
    gm, gn = M // tm, N // tn
    return _pcall(
        body, name=name, grid=((gn, gm, nk) if j_outer else (gm, gn, nk)), in_specs=in_specs, out_specs=o_spec,
        out_shape=jax.ShapeDtypeStruct((M, N), out_dtype),
        scratch_shapes=([pltpu.VMEM((tm, tn), F32)] if nk > 1 else []),
        compiler_params=_cp(("parallel", "parallel", "arbitrary")),
    )(*args)


_MM_VMEM_BUDGET = 40 * 1024 * 1024


def _divisors128(n, cap):
    return [d for d in range(128, min(n, cap) + 1, 128) if n % d == 0][::-1]


_MM_CYC_PER_MMAC = 4.35
_MM_CYC_PER_ACC_VREG = 2.03
_MM_HBM_BYTES_PER_CYC = 1455.0
_MM_CYC_PER_STEP = 770.0


def _mm_tiles(M, N, K, sa, sb, so):
    best = None
    for tm in _divisors128(M, 1408):
        for tn in _divisors128(N, 2560):
            for tk in ([K] if K <= 4096 else []) + _divisors128(K, 2816):
                nk = K // tk
                need = 2 * (tm * tk * sa + tk * tn * sb + tm * tn * so) + (tm * tn * 4 if nk > 1 else 0)
                need += tm * tk * 2 + tk * tn * 2 + tm * tn * 4
                if need > _MM_VMEM_BUDGET:
                    continue
                gm, gn = M // tm, N // tn
                a_bytes, b_bytes = M * K * sa, K * N * sb
                hbm = min(b_bytes + a_bytes * gn, a_bytes + b_bytes * gm) if nk == 1 else a_bytes * gn + b_bytes * gm
                hbm += M * N * so
                work = _MM_CYC_PER_MMAC * M * N * K / 1e6 + _MM_CYC_PER_ACC_VREG * (M * N / 1024) * (nk if nk > 1 else 0.5)
                cost = max(work, hbm / _MM_HBM_BYTES_PER_CYC) + gm * gn * nk * _MM_CYC_PER_STEP
                if best is None or cost < best[0]:
                    best = (cost, (tm, tn, tk))
    assert best is not None, (M, N, K)
    return best[1]


def _store(ref, val):
    if isinstance(val, (list, tuple)):
        off = 0
        for p in val:
            w = p.shape[1]
            ref[:, off:off + w] = p.astype(ref.dtype)
            off += w
    else:
        ref[...] = val.astype(ref.dtype)


def _acc_store(ref, val, first):
    pieces = val if isinstance(val, (list, tuple)) else [val]
    off = 0
    for p in pieces:
        w = p.shape[1]

        @pl.when(first)
        def _(p=p, off=off, w=w):
            ref[:, off:off + w] = p

        @pl.when(jnp.logical_not(first))
        def _(p=p, off=off, w=w):
            ref[:, off:off + w] += p

        off += w


def _rowwise(fn, tiled, full, out_tiled, out_acc, *, name, tm=256):
    arrs, specs = [], []
    for t in tiled:
        arr, w, cb = t if isinstance(t, tuple) else (t, t.shape[1], 0)
        arrs.append(arr)
        specs.append(pl.BlockSpec((tm, w), functools.partial(lambda i, cb: (i, cb), cb=cb)))
    R = arrs[0].shape[0]
    assert R % tm == 0
    for f in full:
        arrs.append(f)
        specs.append(pl.BlockSpec(f.shape, functools.partial(lambda i, nd: (0,) * nd, nd=f.ndim)))
    nt, nf, no = len(tiled), len(full), len(out_tiled)

    def body(*refs):
        tv = [r[...] for r in refs[:nt]]
        fv = [r[...] for r in refs[nt:nt + nf]]
        ot, oa = fn(tv, fv)
        for r, v in zip(refs[nt + nf:nt + nf + no], ot):
            _store(r, v)
        first = pl.program_id(0) == 0
        for r, v in zip(refs[nt + nf + no:], oa):
            _acc_store(r, v, first)

    out_shape = [jax.ShapeDtypeStruct((R, c), dt) for c, dt in out_tiled]
    out_specs = [pl.BlockSpec((tm, c), lambda i: (i, 0)) for c, _ in out_tiled]
    for shp in out_acc:
        out_shape.append(jax.ShapeDtypeStruct(shp, F32))
        out_specs.append(pl.BlockSpec(shp, lambda i: (0, 0)))
    return _pcall(body, name=name, grid=(R // tm,), in_specs=specs, out_specs=out_specs, out_shape=out_shape,
                  compiler_params=_cp(("arbitrary",)))(*arrs)


def _group(group):
    x, y, c = lax.axis_index("x"), lax.axis_index("y"), lax.axis_index("c")
    if group == "chips":
        return 2 * x + y, [((x, 1 - y, c), 2 * x + 1 - y), ((1 - x, y, c), 2 * (1 - x) + y),
                           ((1 - x, 1 - y, c), 2 * (1 - x) + 1 - y)]
    if group == "cores":
        return c, [((x, y, 1 - c), 1 - c)]
    if group == "x":
        return x, [((1 - x, y, c), 1 - x)]
    return y, [((x, 1 - y, c), 1 - y)]


def _exchange(arrs, group, mode, name, dep=None):
    chips = group == "chips"
    k = len(arrs)
    npeer = 3 if chips else 1

    def body(*refs):
        nd = 0 if dep is None else 1
        ins, outs = refs[:k], refs[k + nd:2 * k + nd]
        send_sems, recv_sems = refs[2 * k + nd:]
        me, peers = _group(group)
        remote = []
        for i in range(k):
            for p, (dev, slot) in enumerate(peers):
                src = ins[i].at[slot] if mode == "scatter" else ins[i]
                if not chips:
                    dst = outs[i]
                else:
                    dst = outs[i].at[p] if mode == "scatter" else outs[i].at[me]
                cp = pltpu.make_async_remote_copy(src_ref=src, dst_ref=dst, send_sem=send_sems.at[i, p],
                                                  recv_sem=recv_sems.at[i, p], device_id=dev,
                                                  device_id_type=pl.DeviceIdType.MESH)
                cp.start()
                remote.append(cp)
        for cp in remote:
            cp.wait_recv()
        for cp in remote:
            cp.wait_send()

    def oshape(a):
        piece = a.shape[1:] if mode == "scatter" else a.shape
        if chips:
            piece = ((3,) if mode == "scatter" else (4,)) + piece
        return jax.ShapeDtypeStruct(piece, a.dtype)

    any_spec = pl.BlockSpec(memory_space=pl.ANY)
    extra = [] if dep is None else [dep]
    return _pcall(body, name=name, in_specs=[any_spec] * (k + len(extra)), out_specs=[any_spec] * k,
                  out_shape=[oshape(a) for a in arrs],
                  scratch_shapes=[pltpu.SemaphoreType.DMA((k, npeer)), pltpu.SemaphoreType.DMA((k, npeer))])(*arrs, *extra)


def _split_copies(ins, lands, send_sems, recv_sems, group, mode):
    chips = group == "chips"
    me, peers = _group(group)
    npeer = len(peers)
    out = []
    for i in range(len(ins)):
        for p, (dev, slot) in enumerate(peers):
            src = ins[i].at[slot] if mode == "scatter" else ins[i]
            if not chips:
                dst = lands[i]
            else:
                dst = lands[i].at[p] if mode == "scatter" else lands[i].at[me]
            out.append(pltpu.make_async_remote_copy(src_ref=src, dst_ref=dst, send_sem=send_sems.at[npeer * i + p],
                                                    recv_sem=recv_sems.at[npeer * i + p], device_id=dev,
                                                    device_id_type=pl.DeviceIdType.MESH))
    return out


def _exchange_start(arrs, group, mode, name):
    k = len(arrs)
    chips = group == "chips"
    nsem = (3 if chips else 1) * k
    hbm = pl.BlockSpec(memory_space=pltpu.HBM)
    sem = pl.BlockSpec(memory_space=pltpu.SEMAPHORE)

    def land_shape(a):
        piece = a.shape[1:] if mode == "scatter" else a.shape
        if chips:
            piece = ((3,) if mode == "scatter" else (4,)) + piece
        return piece

    def body(*refs):
        ins, lands = refs[:k], refs[k:2 * k]
        send_sems, recv_sems = refs[2 * k], refs[2 * k + 1]
        token = refs[-1]
        for cp in _split_copies(ins, lands, send_sems, recv_sems, group, mode):
            cp.start()
        token[...] = jnp.zeros_like(token)

    srcs = [pltpu.with_memory_space_constraint(a, pltpu.HBM) for a in arrs]
    lands = [pltpu.with_memory_space_constraint(lax.empty(land_shape(a), a.dtype), pltpu.HBM) for a in arrs]
    out_shape = ([pltpu.SemaphoreType.DMA((nsem,)), pltpu.SemaphoreType.DMA((nsem,))]
                 + [pltpu.HBM(a.shape, a.dtype) for a in arrs] + [pltpu.HBM(land_shape(a), a.dtype) for a in arrs]
                 + [jax.ShapeDtypeStruct((8, 128), F32)])
    outs = _pcall(body, name=name, in_specs=[hbm] * (2 * k),
                  out_specs=[sem, sem] + [hbm] * (2 * k) + [pl.BlockSpec(memory_space=pltpu.VMEM)], out_shape=out_shape,
                  input_output_aliases={i: 2 + i for i in range(2 * k)},
                  compiler_params=pltpu.CompilerParams(has_side_effects=pltpu.SideEffectType.DATAFLOW_SIDE_EFFECTING))(
                      *srcs, *lands)
    return (outs[0], outs[1], list(outs[2:2 + k]), list(outs[2 + k:2 + 2 * k])), outs[-1]


def _exchange_wait(state, after, group, mode, name):
    send_sems, recv_sems, srcs, lands = state
    k = len(srcs)
    after = list(after) if isinstance(after, (list, tuple)) else [after]
    hbm = pl.BlockSpec(memory_space=pltpu.HBM)
    sem = pl.BlockSpec(memory_space=pltpu.SEMAPHORE)

    def body(*refs):
        ins, lnd = refs[:k], refs[k:2 * k]
        send_sems, recv_sems = refs[2 * k], refs[2 * k + 1]
        for cp in _split_copies(ins, lnd, send_sems, recv_sems, group, mode):
            cp.wait_send()
            cp.wait_recv()

    outs = _pcall(body, name=name,
                  in_specs=[hbm] * (2 * k) + [sem, sem] + [pl.BlockSpec(memory_space=pl.ANY)] * len(after),
                  out_specs=[hbm] * (2 * k),
                  out_shape=[pltpu.HBM(a.shape, a.dtype) for a in srcs] + [pltpu.HBM(a.shape, a.dtype) for a in lands],
                  input_output_aliases={i: i for i in range(2 * k)},
                  compiler_params=pltpu.CompilerParams(has_side_effects=pltpu.SideEffectType.DATAFLOW_SIDE_EFFECTING))(
                      *srcs, *lands, send_sems, recv_sems, *after)
    return list(outs[:k]), list(outs[k:])


def _sum_own_recv(own, recv, me, out_dtype, name):
    n, R, C = own.shape
    nr = 1 if recv.ndim == 2 else recv.shape[0]
    tr = _pick(R, (256, 128, 64, 32, 16, 8))

    def body(me_ref, own_ref, *refs):
        o_ref = refs[-1]
        acc = own_ref[...].astype(F32)
        for r in refs[:-1]:
            acc = acc + r[...].astype(F32)
        o_ref[...] = acc.astype(out_dtype)

    specs = [pl.BlockSpec((None, tr, C), lambda i, me_ref: (me_ref[0], i, 0))]
    args = [own]
    if recv.ndim == 2:
        specs.append(pl.BlockSpec((tr, C), lambda i, me_ref: (i, 0)))
        args.append(recv)
    else:
        for p in range(nr):
            specs.append(pl.BlockSpec((None, tr, C), functools.partial(lambda i, me_ref, p: (p, i, 0), p=p)))
            args.append(recv)
    gs = pltpu.PrefetchScalarGridSpec(num_scalar_prefetch=1, grid=(R // tr,), in_specs=specs,
                                      out_specs=pl.BlockSpec((tr, C), lambda i, me_ref: (i, 0)))
    return _pcall(body, name=name, grid_spec=gs, out_shape=jax.ShapeDtypeStruct((R, C), out_dtype),
                  compiler_params=_cp(("parallel",)))(me, *args)


def _silu(x):
    return x * jax.nn.sigmoid(x)


def _ln(r, g, b):
    mu = jnp.mean(r, -1, keepdims=True)
    xc = r - mu
    var = jnp.mean(xc * xc, -1, keepdims=True)
    return xc * lax.rsqrt(var + LN_EPS) * g + b


def _softplus(x):
    return jnp.maximum(x, 0.0) + jnp.log1p(jnp.exp(-jnp.abs(x)))


RES_LN_TM = 512


def _mm_res_ln(a, w, x, g, b, res, name):
    T, K = a.shape
    tm = RES_LN_TM

    def body(a_ref, w_ref, x_ref, g_ref, b_ref, r_ref, o_ref, om_ref):
        y = jnp.dot(a_ref[...].astype(_MXU), w_ref[...].astype(_MXU), preferred_element_type=F32)
        r = ALPHA * x_ref[...] + res * y
        out = _ln(r, g_ref[...], b_ref[...])
        r_ref[...] = r
        o_ref[...] = out
        om_ref[...] = out.astype(om_ref.dtype)

    row = pl.BlockSpec((tm, D), lambda i: (i, 0))
    vec = pl.BlockSpec((1, D), lambda i: (0, 0))
    return _pcall(body, name=name, grid=(T // tm,),
                  in_specs=[pl.BlockSpec((tm, K), lambda i: (i, 0)), pl.BlockSpec((K, D), lambda i: (0, 0)), row, vec, vec],
                  out_specs=[row, row, row],
                  out_shape=[jax.ShapeDtypeStruct((T, D), F32), jax.ShapeDtypeStruct((T, D), F32),
                             jax.ShapeDtypeStruct((T, D), _ACT)],
                  compiler_params=_cp(("parallel",)))(a, w, x, g, b)


def _ln_bwd(r, g, b, dout, res, name):
    def fn(tv, fv):
        _, vjp = jax.vjp(_ln, tv[0], fv[0], fv[1])
        dr, dg, db = vjp(tv[1])
        return [ALPHA * dr, res * dr], [dg, db]
    return _rowwise(fn, [r, dout], [g, b], [(D, F32), (D, _ACT)], [(1, D), (1, D)], name=name)


SWIGLU_TM, SWIGLU_TN = 512, 1408


def _swiglu_fwd(x, w13, dep, name):
    T, K = x.shape
    tm, tn = SWIGLU_TM, SWIGLU_TN
    nj = DFF // tn
    has_dep = dep is not None

    def body(x_ref, wa_ref, wg_ref, *rest):
        a_ref, g_ref, s_ref = rest[-3:]
        xv = x_ref[...].astype(_MXU)
        a = jnp.dot(xv, wa_ref[...].astype(_MXU), preferred_element_type=F32)
        g = jnp.dot(xv, wg_ref[...].astype(_MXU), preferred_element_type=F32)
        a_ref[...] = a
        g_ref[...] = g
        s_ref[...] = (_silu(a) * g).astype(s_ref.dtype)

    out = pl.BlockSpec((tm, tn), lambda j, i: (i, j))
    in_specs = [pl.BlockSpec((tm, K), lambda j, i: (i, 0)), pl.BlockSpec((K, tn), lambda j, i: (0, j)),
                pl.BlockSpec((K, tn), lambda j, i: (0, nj + j))]
    args = [x, w13, w13]
    if has_dep:
        in_specs.append(pl.BlockSpec(memory_space=pl.ANY))
        args.append(dep)
    return _pcall(body, name=name, grid=(nj, T // tm), in_specs=in_specs, out_specs=[out, out, out],
                  out_shape=[jax.ShapeDtypeStruct((T, DFF), F32), jax.ShapeDtypeStruct((T, DFF), F32),
                             jax.ShapeDtypeStruct((T, DFF), _ACT)],
                  compiler_params=_cp(("parallel", "parallel")))(*args)


def _swiglu_act_bwd(a, g, ds, name):
    def fn(tv, fv):
        av, gv, dsv = tv
        sg = jax.nn.sigmoid(av)
        sa = av * sg
        da = dsv * gv * (sg + sa * (1.0 - sg))
        return [[da, dsv * sa], sa * gv], []
    return _rowwise(fn, [a, g, ds], [], [(2 * DFF, _ACT), (DFF, _ACT)], [], name=name)


def _loss_fwd_bwd(y, tgt, name):
    def fn(tv, fv):
        e = tv[0] - tv[1]
        row = jnp.sum(e * e, axis=1, keepdims=True)
        tot = jnp.sum(row, axis=0, keepdims=True) * (0.5 / D)
        return [e * (1.0 / D)], [jnp.broadcast_to(tot, (1, 128))]
    return _rowwise(fn, [y, tgt], [], [(D, F32)], [(1, 128)], name=name)


def _shift_down(x, k, row):
    return jnp.where(row >= k, pltpu.roll(x, k, axis=0), 0.0)


def _shift_up(x, k, row):
    n = x.shape[0]
    return jnp.where(row < n - k, pltpu.roll(x, n - k, axis=0), 0.0)


def _pool_window_masks(j):
    lane = lax.broadcasted_iota(jnp.int32, (1, 128), 1) + j * 128
    grp = lane // POOL_GDIM
    return [grp == g for g in range(4)]


def _pool_mean(u, bwd, name, col0=0):
    T = u.shape[0]
    B = T // S

    def body(u_ref, o_ref):
        j = pl.program_id(1)
        x = u_ref[...]
        row = lax.broadcasted_iota(jnp.int32, (S, 1), 0)
        masks = _pool_window_masks(j)
        inv = [1.0 / jnp.minimum(row + 1, w).astype(F32) for w in POOL_WINDOWS]
        if not bwd:
            s2 = x + _shift_down(x, 1, row)
            s4 = s2 + _shift_down(s2, 2, row)
            s8 = s4 + _shift_down(s4, 4, row)
            s16 = s8 + _shift_down(s8, 8, row)
            mean = jnp.where(masks[0], s2 * inv[0], jnp.where(masks[1], s4 * inv[1],
                             jnp.where(masks[2], s8 * inv[2], s16 * inv[3])))
            o_ref[...] = (mean - x).astype(o_ref.dtype)
        else:
            g = [jnp.where(masks[i], x * inv[i], 0.0) for i in range(4)]
            t = g[3]
            t = t + _shift_up(t, 8, row) + g[2]
            t = t + _shift_up(t, 4, row) + g[1]
            t = t + _shift_up(t, 2, row) + g[0]
            t = t + _shift_up(t, 1, row)
            o_ref[...] = (t - x).astype(o_ref.dtype)

    spec = pl.BlockSpec((S, 128), lambda b, j: (b, j))
    return _pcall(body, name=name, grid=(B, POOLW // 128),
                  in_specs=[pl.BlockSpec((S, 128), lambda b, j: (b, j + col0))], out_specs=spec,
                  out_shape=jax.ShapeDtypeStruct((T, POOLW), _ACT), compiler_params=_cp(("parallel", "parallel")))(u)


def _conv_silu(xbc, w, b, name, col0=0):
    T, C = xbc.shape[0], w.shape[1]
    B = T // S

    def body(x_ref, w_ref, b_ref, o_ref):
        x = x_ref[...]
        row = lax.broadcasted_iota(jnp.int32, (S, 1), 0)
        c = b_ref[...] + w_ref[3:4, :] * x
        for s in range(1, 4):
            c = c + w_ref[3 - s:4 - s, :] * _shift_down(x, s, row)
        o_ref[...] = _silu(c)

    return _pcall(body, name=name, grid=(B, C // 128),
                  in_specs=[pl.BlockSpec((S, 128), lambda b, j: (b, j + col0)), pl.BlockSpec((4, 128), lambda b, j: (0, j)),
                            pl.BlockSpec((1, 128), lambda b, j: (0, j))],
                  out_specs=pl.BlockSpec((S, 128), lambda b, j: (b, j)),
                  out_shape=jax.ShapeDtypeStruct((T, C), F32), compiler_params=_cp(("parallel", "parallel")))(xbc, w, b)


def _conv_silu_bwd(xbc, w, b, dact, name, col0=0):
    T, C = xbc.shape[0], w.shape[1]
    B = T // S

    def body(x_ref, w_ref, b_ref, d_ref, dx_ref, dw_ref, db_ref):
        bi = pl.program_id(1)
        x = x_ref[...]
        row = lax.broadcasted_iota(jnp.int32, (S, 1), 0)
        xs = [x] + [_shift_down(x, s, row) for s in range(1, 4)]
        c = b_ref[...]
        for s in range(4):
            c = c + w_ref[3 - s:4 - s, :] * xs[s]
        _, vjp = jax.vjp(_silu, c)
        dc = vjp(d_ref[...])[0]
        dx = w_ref[3:4, :] * dc
        for s in range(1, 4):
            dx = dx + w_ref[3 - s:4 - s, :] * _shift_up(dc, s, row)
        dx_ref[...] = dx.astype(dx_ref.dtype)
        first = bi == 0
        for s in range(4):
            _acc_rows(dw_ref, 3 - s, jnp.sum(dc * xs[s], axis=0, keepdims=True), first)
        _acc_rows(db_ref, 0, jnp.sum(dc, axis=0, keepdims=True), first)

    blk = pl.BlockSpec((S, 128), lambda j, b: (b, j))
    return _pcall(body, name=name, grid=(C // 128, B),
                  in_specs=[pl.BlockSpec((S, 128), lambda j, b: (b, j + col0)), pl.BlockSpec((4, 128), lambda j, b: (0, j)),
                            pl.BlockSpec((1, 128), lambda j, b: (0, j)), blk],
                  out_specs=[blk, pl.BlockSpec((4, 128), lambda j, b: (0, j)), pl.BlockSpec((1, 128), lambda j, b: (0, j))],
                  out_shape=[jax.ShapeDtypeStruct((T, C), _ACT), jax.ShapeDtypeStruct((4, C), F32),
                             jax.ShapeDtypeStruct((1, C), F32)],
                  compiler_params=_cp(("parallel", "arbitrary")))(xbc, w, b, dact)


def _acc_rows(ref, r, val, first):
    @pl.when(first)
    def _():
        ref[r:r + 1, :] = val

    @pl.when(jnp.logical_not(first))
    def _():
        ref[r:r + 1, :] += val


def _tri_consts():
    i = lax.broadcasted_iota(jnp.int32, (CH, CH), 0)
    j = lax.broadcasted_iota(jnp.int32, (CH, CH), 1)
    return (i == j).astype(F32), (j <= i).astype(F32), (i <= j).astype(F32), i >= j


def _ssd_chunk(h, x, dt, Bm, Cm, a, dsk, consts):
    eye, tril, triu, lower = consts
    first_head = lax.broadcasted_iota(jnp.int32, (1, 128), 1) < 64
    sel = lambda u, v: jnp.where(first_head, u, v)
    Bb = Bm.astype(_MXU)
    Cb = Cm.astype(_MXU)
    cb = lax.dot_general(Cb, Bb, (((1,), (1,)), ((), ())), preferred_element_type=F32)
    ys, hn = [], []
    for p in range(2):
        dtm, csm, cs_last, decay, bdec = [], [], [], [], []
        for e in (2 * p, 2 * p + 1):
            d = jnp.broadcast_to(dt[e], (CH, CH))
            adm = d * a[e]
            adt_row = jnp.sum(adm * eye, axis=0, keepdims=True)
            cs_col = jnp.sum(adt_row * tril, axis=1, keepdims=True)
            cs_row = jnp.sum(adm * triu, axis=0, keepdims=True)
            last = jnp.sum(adt_row, axis=1, keepdims=True)
            c = jnp.broadcast_to(cs_col, (CH, CH))
            dtm.append(d)
            csm.append(c)
            cs_last.append(last)
            decay.append((cb * jnp.exp(jnp.where(lower, c - cs_row, -jnp.inf))).astype(_MXU))
            bdec.append((Bm * jnp.exp(last - c)).astype(_MXU))
        xb = (x[p] * sel(dtm[0], dtm[1])).astype(_MXU)
        y_diag = sel(jnp.dot(decay[0], xb, preferred_element_type=F32), jnp.dot(decay[1], xb, preferred_element_type=F32))
        states = [lax.dot_general(b, xb, (((0,), (0,)), ((), ())), preferred_element_type=F32) for b in bdec]
        hn.append(h[p] * sel(jnp.exp(cs_last[0]), jnp.exp(cs_last[1])) + sel(states[0], states[1]))
        y_off = jnp.exp(sel(csm[0], csm[1])) * jnp.dot(Cb, h[p].astype(_MXU), preferred_element_type=F32)
        ys.append(y_diag + y_off + sel(dsk[2 * p], dsk[2 * p + 1]) * x[p])
    return ys, hn


def _ssd_specs(order):
    def im(f):
        return lambda p, q: f(*order(p, q))
    xs = pl.BlockSpec((S, 256), im(lambda b, g: (b, g)))
    dt = pl.BlockSpec((None, S, 4), im(lambda b, g: (g, b, 0)))
    bc = pl.BlockSpec((S, 128), im(lambda b, g: (b, g)))
    hd = pl.BlockSpec((None, 1, 4), im(lambda b, g: (g, 0, 0)))
    hs = pl.BlockSpec((None, None, S // CH, 2, 128, 128), im(lambda b, g: (b, g, 0, 0, 0, 0)))
    bw = pl.BlockSpec((S, 128), im(lambda b, g: (b, 8 + g)))
    cw = pl.BlockSpec((S, 128), im(lambda b, g: (b, 12 + g)))
    return xs, dt, bc, hd, hs, bw, cw


def _ssd_fwd(act, dtg, a, dsk, name):
    xs = bm = cm = act
    T = xs.shape[0]
    B = T // S
    nc = S // CH

    def body(x_ref, dt_ref, b_ref, c_ref, a_ref, k_ref, y_ref, hs_ref, h_ref):
        consts = _tri_consts()
        h_ref[...] = jnp.zeros_like(h_ref)
        al = [a_ref[:, e:e + 1] for e in range(4)]
        kl = [k_ref[:, e:e + 1] for e in range(4)]

        def step(c, carry):
            r0 = pl.multiple_of(c * CH, CH)
            rows = pl.ds(r0, CH)
            h = [h_ref[p] for p in range(2)]
            for p in range(2):
                hs_ref[c, p] = h[p]
            x = [x_ref[rows, 128 * p:128 * p + 128] for p in range(2)]
            dt = [dt_ref[rows, e:e + 1] for e in range(4)]
            ys, hn = _ssd_chunk(h, x, dt, b_ref[rows, :], c_ref[rows, :], al, kl, consts)
            for p in range(2):
                y_ref[rows, 128 * p:128 * p + 128] = ys[p]
                h_ref[p] = hn[p]
            return carry

        lax.fori_loop(0, nc, step, 0)

    sx, sdt, sbc, shd, shs, sbw, scw = _ssd_specs(lambda b, g: (b, g))
    return _pcall(body, name=name, grid=(B, 4), in_specs=[sx, sdt, sbw, scw, shd, shd], out_specs=[sx, shs],
                  out_shape=[jax.ShapeDtypeStruct((T, 1024), F32), jax.ShapeDtypeStruct((B, 4, nc, 2, 128, 128), F32)],
                  scratch_shapes=[pltpu.VMEM((2, 128, 128), F32)],
                  compiler_params=_cp(("parallel", "parallel")))(xs, dtg, bm, cm, a, dsk)


def _lane_place(vals, width):
    lane = lax.broadcasted_iota(jnp.int32, (1, width), 1)
    out = jnp.zeros((1, width), F32)
    for e, v in enumerate(vals):
        out = out + jnp.where(lane == e, v, 0.0)
    return out


def _ssd_bwd(act, dtg, a, dsk, hs, dy, name):
    xs = bm = cm = act
    T = xs.shape[0]
    B = T // S
    nc = S // CH

    def body(x_ref, dt_ref, b_ref, c_ref, a_ref, k_ref, hs_ref, dy_ref,
             dx_ref, ddt_ref, db_ref, dc_ref, dak_ref, dh_ref, sc_ref):
        bi = pl.program_id(1)
        consts = _tri_consts()
        dh_ref[...] = jnp.zeros_like(dh_ref)
        sc_ref[...] = jnp.zeros_like(sc_ref)
        al = [a_ref[:, e:e + 1] for e in range(4)]
        kl = [k_ref[:, e:e + 1] for e in range(4)]

        def step(i, carry):
            c = nc - 1 - i
            r0 = pl.multiple_of(c * CH, CH)
            rows = pl.ds(r0, CH)
            h = [hs_ref[c, p] for p in range(2)]
            x = [x_ref[rows, 128 * p:128 * p + 128] for p in range(2)]
            dt = [dt_ref[rows, e:e + 1] for e in range(4)]
            f = functools.partial(_ssd_chunk, consts=consts)
            _, vjp = jax.vjp(f, h, x, dt, b_ref[rows, :], c_ref[rows, :], al, kl)
            dys = [dy_ref[rows, 128 * p:128 * p + 128] for p in range(2)]
            dhn = [dh_ref[p] for p in range(2)]
            dh, dx, ddt, dB, dC, da, dk = vjp((dys, dhn))
            for p in range(2):
                dh_ref[p] = dh[p]
                dx_ref[rows, 128 * p:128 * p + 128] = dx[p]
            for e in range(4):
                ddt_ref[rows, e:e + 1] = ddt[e]
            db_ref[rows, :] = dB
            dc_ref[rows, :] = dC
            sc_ref[0:1, :] += _lane_place(da, 128)
            sc_ref[1:2, :] += _lane_place(dk, 128)
            return carry

        lax.fori_loop(0, nc, step, 0)
        first = bi == 0

        @pl.when(first)
        def _():
            dak_ref[...] = sc_ref[...]

        @pl.when(jnp.logical_not(first))
        def _():
            dak_ref[...] += sc_ref[...]

    sx, sdt, sbc, shd, shs, sbw, scw = _ssd_specs(lambda g, b: (b, g))
    return _pcall(body, name=name, grid=(4, B), in_specs=[sx, sdt, sbw, scw, shd, shd, shs, sx],
                  out_specs=[sx, sdt, sbc, sbc, pl.BlockSpec((None, 8, 128), lambda g, b: (g, 0, 0))],
                  out_shape=[jax.ShapeDtypeStruct((T, 1024), F32), jax.ShapeDtypeStruct((4, T, 4), F32),
                             jax.ShapeDtypeStruct((T, 512), F32), jax.ShapeDtypeStruct((T, 512), F32),
                             jax.ShapeDtypeStruct((4, 8, 128), F32)],
                  scratch_shapes=[pltpu.VMEM((2, 128, 128), F32), pltpu.VMEM((8, 128), F32)],
                  compiler_params=_cp(("parallel", "arbitrary")))(xs, dtg, bm, cm, a, dsk, hs, dy)


def _gate_norm(y, z, nw):
    t = y * _silu(z)
    return t * lax.rsqrt(jnp.mean(t * t, axis=-1, keepdims=True) + SSD_EPS) * nw


def _ssd_gate_norm(y, z, nw, name, zcol=0):
    def fn(tv, fv):
        return [[_gate_norm(tv[g], tv[4 + g], fv[0][:, 256 * g:256 * g + 256]) for g in range(4)]], []
    tiled = [(y, 256, g) for g in range(4)] + [(z, 256, zcol + g) for g in range(4)]
    return _rowwise(fn, tiled, [nw], [(1024, _ACT)], [], name=name)[0]


def _ssd_gate_norm_bwd(y, z, nw, dout, name, zcol=0):
    def fn(tv, fv):
        dys, dzs, dns = [], [], []
        for g in range(4):
            _, vjp = jax.vjp(_gate_norm, tv[g], tv[4 + g], fv[0][:, 256 * g:256 * g + 256])
            a, b, c = vjp(tv[8 + g])
            dys.append(a)
            dzs.append(b)
            dns.append(c)
        return [dys, dzs], [dns]
    tiled = [(y, 256, g) for g in range(4)] + [(z, 256, zcol + g) for g in range(4)] + [(dout, 256, g) for g in range(4)]
    return _rowwise(fn, tiled, [nw], [(1024, F32), (1024, _ACT)], [(1, 1024)], name=name)


def _t5_bucket_np(dist):
    dist = np.maximum(dist, 0)
    max_exact = 16
    large = max_exact + (np.log(np.maximum(dist, 1) / max_exact) / np.log(2048 / max_exact) * (32 - max_exact)).astype(np.int32)
    large = np.minimum(large, 31)
    return np.where(dist < max_exact, dist, large).astype(np.int32)


def _bucket_maps():
    qi = np.arange(128)[:, None]
    kj = np.arange(256)[None, :]
    return np.stack([_t5_bucket_np((qi - kj + 128) * dil) for dil in ATTN_DILS]).astype(np.int32)


def _bias_build(rel_bias, maps, name):
    def body(tab_ref, map_ref, o_ref):
        hh = pl.program_id(0)
        m = map_ref[...]
        acc = jnp.zeros((128, 256), F32)
        for b in range(32):
            acc = jnp.where(m == b, tab_ref[b, hh], acc)
        o_ref[...] = acc

    return _pcall(body, name=name, grid=(12,),
                  in_specs=[pl.BlockSpec(memory_space=pltpu.SMEM), pl.BlockSpec((None, 128, 256), lambda h: (h // 4, 0, 0))],
                  out_specs=pl.BlockSpec((None, 128, 256), lambda h: (h, 0, 0)),
                  out_shape=jax.ShapeDtypeStruct((12, 128, 256), F32), compiler_params=_cp(("parallel",)))(rel_bias, maps)


def _bias_reduce(dbias, maps, name):
    nl = dbias.shape[0]

    def body(d_ref, map_ref, o_ref):
        m = map_ref[...]
        d = d_ref[0]
        for i in range(1, nl):
            d = d + d_ref[i]
        lane = lax.broadcasted_iota(jnp.int32, (1, 128), 1)
        out = jnp.zeros((1, 128), F32)
        for b in range(32):
            s = jnp.sum(jnp.sum(jnp.where(m == b, d, 0.0), axis=1, keepdims=True), axis=0, keepdims=True)
            out = out + jnp.where(lane == b, s, 0.0)
        o_ref[...] = out

    return _pcall(body, name=name, grid=(12,),
                  in_specs=[pl.BlockSpec((nl, None, 128, 256), lambda h: (0, h, 0, 0)),
                            pl.BlockSpec((None, 128, 256), lambda h: (h // 4, 0, 0))],
                  out_specs=pl.BlockSpec((None, 1, 128), lambda h: (h, 0, 0)),
                  out_shape=jax.ShapeDtypeStruct((12, 1, 128), F32), compiler_params=_cp(("parallel",)))(dbias, maps)


def _attn_block(q, kb, vb, bias, mask):
    s = lax.dot_general(q.astype(_MXU), kb.astype(_MXU), (((1,), (1,)), ((), ())), preferred_element_type=F32) * 0.125 + bias
    s = jnp.where(mask, s, -jnp.inf)
    m = lax.stop_gradient(jnp.max(s, axis=-1, keepdims=True))
    p = jnp.exp(s - m)
    den = jnp.sum(p, axis=-1, keepdims=True)
    out = jnp.dot((p * jnp.broadcast_to(1.0 / den, p.shape)).astype(_MXU), vb.astype(_MXU), preferred_element_type=F32)
    return out, m + jnp.log(den)


ATTN_QB = 512


def _attn_masks(dil):
    qi = lax.broadcasted_iota(jnp.int32, (ATTN_QB, ATTN_QB + 128), 0)
    kj = lax.broadcasted_iota(jnp.int32, (ATTN_QB, ATTN_QB + 128), 1)
    band = (kj >= qi) & (kj <= qi + 128)
    if dil == 16:
        q2 = lax.broadcasted_iota(jnp.int32, (ATTN_QB, ATTN_QB), 0)
        k2 = lax.broadcasted_iota(jnp.int32, (ATTN_QB, ATTN_QB), 1)
        return ((q2 // 128) == (k2 // 128)) & (k2 <= q2), None
    return band[:, 128:], band


def _attn_wide_bias(b, dil):
    if dil == 16:
        return jnp.tile(b[:, 128:], (4, 4)), None
    z = jnp.zeros((128, 128), F32)
    band = jnp.concatenate([jnp.concatenate([z] * i + [b] + [z] * (3 - i), axis=1) for i in range(4)], axis=0)
    return band[:, 128:], band


def _fold_dbias(dbs, dil, band_form):
    def blk(i, j):
        return dbs[128 * i:128 * i + 128, 128 * j:128 * j + 128]
    if band_form:
        return sum(blk(i, i) for i in range(4)), sum(blk(i, i + 1) for i in range(4))
    cur = sum(blk(i, i) for i in range(4))
    if dil == 16:
        return None, cur
    return sum(blk(i, i - 1) for i in range(1, 4)), cur


def _attn_chunks(dil):
    out = []
    for n in range(S // ATTN_QB):
        if dil == 1 and n > 0:
            out.append((n * ATTN_QB, n * ATTN_QB - 128, ATTN_QB + 128, True))
        else:
            out.append((n * ATTN_QB, n * ATTN_QB, ATTN_QB, False))
    return out


def _qkv_specs(gi, order):
    def spec(base):
        col = (base + 256 * gi) // 128
        return pl.BlockSpec((S, 128), lambda p, q: (order(p, q)[0], col + order(p, q)[1]))
    return [spec(O_Q), spec(O_K), spec(O_V)]


def _residue_rows(r, dil):
    return pl.ds(r, S // dil, stride=dil)


def _attn_fwd(hcat, bias_all, gi, name):
    dil = ATTN_DILS[gi]
    T = hcat.shape[0]
    B, L = T // S, S // dil

    def body(q_ref, k_ref, v_ref, b_ref, o_ref, l_ref, *scr):
        mask_first, mask_band = _attn_masks(dil)
        if dil > 1:
            qs, ks, vs, os_, ls = scr
            for r in range(dil):
                rows, dst = _residue_rows(r, dil), pl.ds(r * L, L)
                qs[dst, :] = q_ref[rows, :]
                ks[dst, :] = k_ref[rows, :]
                vs[dst, :] = v_ref[rows, :]
        else:
            qs, ks, vs, os_, ls = q_ref, k_ref, v_ref, o_ref, l_ref
        ls[...] = jnp.zeros_like(ls)
        first_head = lax.broadcasted_iota(jnp.int32, (1, 128), 1) < 64
        biases = [_attn_wide_bias(b_ref[e], dil) for e in range(2)]
        for q0, k0, kn, band_form in _attn_chunks(dil):
            cur, keys = pl.ds(q0, ATTN_QB), pl.ds(k0, kn)
            qp, kp, vp = qs[cur, :], ks[keys, :], vs[keys, :]
            outs = []
            for e in range(2):
                o, l = _attn_block(jnp.where(first_head == (e == 0), qp, 0.0), kp, vp,
                                   biases[e][1] if band_form else biases[e][0], mask_band if band_form else mask_first)
                outs.append(o)
                ls[cur, e:e + 1] = l
            os_[cur, :] = jnp.where(first_head, outs[0], outs[1])
        if dil > 1:
            for r in range(dil):
                rows, src = _residue_rows(r, dil), pl.ds(r * L, L)
                o_ref[rows, :] = os_[src, :]
                l_ref[rows, :] = ls[src, :]

    scratch = [pltpu.VMEM((S, 128), F32)] * 5 if dil > 1 else []
    return _pcall(body, name=name, grid=(B, 2),
                  in_specs=_qkv_specs(gi, lambda b, hp: (b, hp))
                  + [pl.BlockSpec((2, 128, 256), lambda b, hp: (2 * gi + hp, 0, 0))],
                  out_specs=[pl.BlockSpec((S, 128), lambda b, hp: (b, hp)),
                             pl.BlockSpec((None, S, 128), lambda b, hp: (hp, b, 0))],
                  out_shape=[jax.ShapeDtypeStruct((T, 256), F32), jax.ShapeDtypeStruct((2, T, 128), F32)],
                  scratch_shapes=scratch,
                  compiler_params=_cp(("parallel", "parallel")))(hcat, hcat, hcat, bias_all)


def _attn_bwd(hcat, bias_all, gi, do, dl, name):
    dil = ATTN_DILS[gi]
    T = hcat.shape[0]
    B, L = T // S, S // dil

    def body(q_ref, k_ref, v_ref, b_ref, do_ref, dl_ref, dq_ref, dk_ref, dv_ref, db_ref, acc_ref, *scr):
        bi = pl.program_id(1)
        mask_first, mask_band = _attn_masks(dil)
        if dil > 1:
            qs, ks, vs, dos, dls, dqs, dks, dvs = scr
            for r in range(dil):
                rows, dst = _residue_rows(r, dil), pl.ds(r * L, L)
                qs[dst, :] = q_ref[rows, :]
                ks[dst, :] = k_ref[rows, :]
                vs[dst, :] = v_ref[rows, :]
                dos[dst, :] = do_ref[rows, :]
                dls[dst, :] = dl_ref[rows, :]
        else:
            qs, ks, vs, dos, dls, dqs, dks, dvs = q_ref, k_ref, v_ref, do_ref, dl_ref, dq_ref, dk_ref, dv_ref
        dks[...] = jnp.zeros_like(dks)
        dvs[...] = jnp.zeros_like(dvs)
        first_head = lax.broadcasted_iota(jnp.int32, (1, 128), 1) < 64
        for e in range(2):
            mine = first_head == (e == 0)
            bias_first, bias_band = _attn_wide_bias(b_ref[e], dil)
            acc_ref[...] = jnp.zeros_like(acc_ref)
            for q0, k0, kn, band_form in _attn_chunks(dil):
                cur, keys = pl.ds(q0, ATTN_QB), pl.ds(k0, kn)
                mask = mask_band if band_form else mask_first
                f = lambda qp, kp, vp, b: _attn_block(jnp.where(mine, qp, 0.0), kp, vp, b, mask)
                _, vjp = jax.vjp(f, qs[cur, :], ks[keys, :], vs[keys, :], bias_band if band_form else bias_first)
                dq, dkb, dvb, dbs = vjp((jnp.where(mine, dos[cur, :], 0.0), dls[cur, e:e + 1]))
                if e == 0:
                    dqs[cur, :] = dq
                else:
                    dqs[cur, :] += dq
                dks[keys, :] += dkb
                dvs[keys, :] += dvb
                prev, here = _fold_dbias(dbs, dil, band_form)
                if prev is not None:
                    acc_ref[:, 0:128] += prev
                acc_ref[:, 128:256] += here

            @pl.when(bi == 0)
            def _(e=e):
                db_ref[e] = acc_ref[...]

            @pl.when(bi > 0)
            def _(e=e):
                db_ref[e] += acc_ref[...]

        if dil > 1:
            for r in range(dil):
                rows, src = _residue_rows(r, dil), pl.ds(r * L, L)
                dq_ref[rows, :] = dqs[src, :]
                dk_ref[rows, :] = dks[src, :]
                dv_ref[rows, :] = dvs[src, :]

    order = lambda hp, b: (b, hp)
    blk = pl.BlockSpec((S, 128), lambda hp, b: (b, hp))
    lblk = pl.BlockSpec((None, S, 128), lambda hp, b: (hp, b, 0))
    sds = jax.ShapeDtypeStruct((T, 256), F32)
    scratch = [pltpu.VMEM((128, 256), F32)] + ([pltpu.VMEM((S, 128), F32)] * 8 if dil > 1 else [])
    return _pcall(body, name=name, grid=(2, B),
                  in_specs=_qkv_specs(gi, order) + [pl.BlockSpec((2, 128, 256), lambda hp, b: (2 * gi + hp, 0, 0)), blk, lblk],
                  out_specs=[blk, blk, blk, pl.BlockSpec((2, 128, 256), lambda hp, b: (hp, 0, 0))],
                  out_shape=[sds, sds, sds, jax.ShapeDtypeStruct((4, 128, 256), F32)],
                  scratch_shapes=scratch,
                  compiler_params=_cp(("parallel", "arbitrary")))(hcat, hcat, hcat, bias_all, do, dl)


def _lse_merge(o0, o1, o2, l0, l1, l2):
    m = lax.stop_gradient(jnp.maximum(jnp.maximum(l0, l1), l2))
    e0, e1, e2 = jnp.exp(l0 - m), jnp.exp(l1 - m), jnp.exp(l2 - m)
    den = e0 + e1 + e2
    return (e0 / den) * o0 + (e1 / den) * o1 + (e2 / den) * o2


def _attn_merge(outs, lses, dy, name):
    T = outs[0].shape[0]
    bwd = dy is not None
    tm = 512

    def body(*refs):
        o_refs, l_refs = refs[:3], refs[3:6]
        if bwd:
            for r in refs[10:13]:
                r[...] = jnp.zeros_like(r)
        for e in range(2):
            lanes = slice(64 * e, 64 * e + 64)
            vals = [r[:, lanes] for r in o_refs] + [r[:, e:e + 1] for r in l_refs]
            if not bwd:
                refs[6][:, lanes] = _lse_merge(*vals).astype(refs[6].dtype)
            else:
                _, vjp = jax.vjp(_lse_merge, *vals)
                g = vjp(refs[6][:, lanes])
                for r, v in zip(refs[7:10], g[:3]):
                    r[:, lanes] = v
                for r, v in zip(refs[10:13], g[3:]):
                    r[:, e:e + 1] = v

    blk = pl.BlockSpec((tm, 128), lambda i, hp: (i, hp))
    lblk = pl.BlockSpec((None, tm, 128), lambda i, hp: (hp, i, 0))
    lsd = jax.ShapeDtypeStruct((2, T, 128), F32)
    if not bwd:
        return _pcall(body, name=name, grid=(T // tm, 2), in_specs=[blk] * 3 + [lblk] * 3, out_specs=blk,
                      out_shape=jax.ShapeDtypeStruct((T, 256), F32),
                      compiler_params=_cp(("parallel", "parallel")))(*outs, *lses)
    return _pcall(body, name=name, grid=(T // tm, 2), in_specs=[blk] * 3 + [lblk] * 3 + [blk],
                  out_specs=[blk] * 3 + [lblk] * 3, out_shape=[jax.ShapeDtypeStruct((T, 256), F32)] * 3 + [lsd] * 3,
                  compiler_params=_cp(("parallel", "parallel")))(*outs, *lses, dy)


def _gmerge(g0, g1, g2, gb, ya, yb, yc):
    return (jax.nn.sigmoid(g0 + gb[:, 0:D]) * ya + jax.nn.sigmoid(g1 + gb[:, D:2 * D]) * yb
            + jax.nn.sigmoid(g2 + gb[:, 2 * D:3 * D]) * yc)


def _gated_merge(gates, gb, ya, yb, yc, name, gcol=0):
    def fn(tv, fv):
        return [_gmerge(tv[0], tv[1], tv[2], fv[0], tv[3], tv[4], tv[5])], []
    return _rowwise(fn, [(gates, D, gcol), (gates, D, gcol + 1), (gates, D, gcol + 2), ya, yb, yc], [gb], [(D, _ACT)], [],
                    name=name)[0]


def _gated_merge_bwd(gates, gb, ya, yb, yc, dm, name, gcol=0):
    def fn(tv, fv):
        _, vjp = jax.vjp(_gmerge, tv[0], tv[1], tv[2], fv[0], tv[3], tv[4], tv[5])
        d0, d1, d2, dgb, da, db, dc = vjp(tv[6])
        return [[d0, d1, d2], da, db, dc], [dgb]
    return _rowwise(fn, [(gates, D, gcol), (gates, D, gcol + 1), (gates, D, gcol + 2), ya, yb, yc, dm], [gb],
                    [(3 * D, _ACT), (D, _ACT), (D, _ACT), (D, _ACT)], [(1, 3 * D)], name=name)


def _pool_affine(t1, pb, ps, dout, name):
    if dout is None:
        def fn(tv, fv):
            return [(tv[0] + fv[0]) * fv[1]], []
        return _rowwise(fn, [t1], [pb, ps], [(POOLW, _ACT)], [], name=name)[0]

    def fnb(tv, fv):
        t2, vjp = jax.vjp(lambda t, b, s: (t + b) * s, tv[0], fv[0], fv[1])
        dt, db, dsc = vjp(tv[1])
        return [dt, t2], [db, dsc]
    return _rowwise(fnb, [t1, dout], [pb, ps], [(POOLW, _ACT), (POOLW, _ACT)], [(1, POOLW), (1, POOLW)], name=name)


def _dt_softplus(dt_raw, dt_bias, ddt, name):
    f = lambda r, b: _softplus(r + b)
    if ddt is None:
        def fn(tv, fv):
            return [f(tv[0], fv[0])], []
        return _rowwise(fn, [dt_raw], [dt_bias], [(16, F32)], [], name=name, tm=1024)[0]

    def fnb(tv, fv):
        _, vjp = jax.vjp(f, tv[0], fv[0])
        dr, db = vjp(tv[1])
        return [dr], [db]
    return _rowwise(fnb, [dt_raw, ddt], [dt_bias], [(16, F32)], [(1, 16)], name=name, tm=1024)


def _adamw_math(wv, gv, mv, vv):
    c1 = 1.0 / (1.0 - ADAM_B1 ** ADAM_STEP)
    c2 = 1.0 / (1.0 - ADAM_B2 ** ADAM_STEP)
    mn = ADAM_B1 * mv + (1.0 - ADAM_B1) * gv
    vn = ADAM_B2 * vv + (1.0 - ADAM_B2) * (gv * gv)
    delta = -ADAM_LR * ((mn * c1) / (jnp.sqrt(vn * c2) + ADAM_EPS) + ADAM_WD * wv)
    return delta, mn, vn


def _adamw(w, g, m, v, name):
    R, C = w.shape
    tm = _pick(R, (256, 128, 64, 32, 16, 8))
    return _rowwise(lambda tv, fv: (list(_adamw_math(*tv)), []), [w, g, m, v], [], [(C, F32)] * 3, [], name=name, tm=tm)


def _adamw_layer(i, w, g, m, v, accs, name, dep=None):
    R, C = w.shape
    r = R // NL
    tm = _pick(r, (256, 128, 64, 32, 16, 8))
    nt = r // tm
    if accs is None:
        accs = [lax.empty((R, C), F32) for _ in range(4)]
    extra = [] if dep is None else [dep]

    def body(w_ref, g_ref, m_ref, v_ref, *rest):
        go_ref, do_ref, mo_ref, vo_ref = rest[-4:]
        gv = g_ref[...]
        delta, mn, vn = _adamw_math(w_ref[...], gv, m_ref[...], v_ref[...])
        go_ref[...] = gv
        do_ref[...] = delta
        mo_ref[...] = mn
        vo_ref[...] = vn

    slab = pl.BlockSpec((tm, C), lambda t: (i * nt + t, 0))
    anyspec = pl.BlockSpec(memory_space=pl.ANY)
    return _pcall(body, name=name, grid=(nt,),
                  in_specs=[slab, pl.BlockSpec((tm, C), lambda t: (t, 0)), slab, slab] + [anyspec] * (4 + len(extra)),
                  out_specs=[slab] * 4, out_shape=[jax.ShapeDtypeStruct((R, C), F32)] * 4,
                  input_output_aliases={4 + k: k for k in range(4)},
                  compiler_params=_cp(("parallel",)))(w, g, m, v, *accs, *extra)


def _ffn_fwd(x, xm, w13, w2, g, b, tag, dep=None):
    ha, hg, s = _swiglu_fwd(xm, w13, dep, name=f"{tag}_h")
    r, out, outm = _mm_res_ln(s, w2, x, g, b, 0.5, name=f"{tag}_y")
    return out, outm, dict(x=xm, ha=ha, hg=hg, r=r)


def _ffn_bwd(dout, sv, w13, w2, g, b, tag, dep=None):
    dskip, dy, dg, db = _ln_bwd(sv['r'], g, b, dout, 0.5, name=f"{tag}_lnb")
    ds = _mm(dy, w2, tb=True, dep=dep, name=f"{tag}_ds")
    dh, s = _swiglu_act_bwd(sv['ha'], sv['hg'], ds, name=f"{tag}_actb")
    dw2 = _mm(s, dy, ta=True, name=f"{tag}_dw2")
    dw13 = _mm(sv['x'], dh, ta=True, name=f"{tag}_dw13")
    dx = _mm(dh, w13, tb=True, add=dskip, name=f"{tag}_dx")
    return dx, dict(w13=dw13, w2=dw2, g=dg, b=db)


def _mixer_fwd(x1, x1m, W, bias_all, tag, dep=None):
    T = x1.shape[0]
    hcat = _mm(x1m, W['w_in_r'], dep=dep, name=f"{tag}_hcat")
    dt_raw = hcat[:, O_DT:O_DT + 16]
    pooled = _pool_mean(hcat, False, name=f"{tag}_pool", col0=O_U // 128)
    t1 = _mm(pooled, W['pool_wbd'], name=f"{tag}_pt1")
    t2 = _pool_affine(t1, W['pool_b'], W['pool_scale'], None, name=f"{tag}_paff")
    ya = _mm(t2, W['p_pool'], name=f"{tag}_ya")
    act = _conv_silu(hcat, W['conv_w'], W['conv_b'], name=f"{tag}_conv", col0=O_XBC // 128)
    dt = _dt_softplus(dt_raw, W['dt_bias'], None, name=f"{tag}_dt")
    dtg = dt.reshape(T, 4, 4).transpose(1, 0, 2)
    yscan, hs = _ssd_fwd(act, dtg, W['a_neg'], W['d_skip'], name=f"{tag}_ssd")
    ybn = _ssd_gate_norm(yscan, hcat, W['ssd_norm'], name=f"{tag}_gn", zcol=O_Z // 256)
    yb = _mm(ybn, W['p_ssd'], name=f"{tag}_yb")
    outs, lses = [], []
    for gi in range(len(ATTN_DILS)):
        o, l = _attn_fwd(hcat, bias_all, gi, name=f"{tag}_attn{gi}")
        outs.append(o)
        lses.append(l)
    ycp = _attn_merge(outs, lses, None, name=f"{tag}_amerge")
    yc = _mm(ycp, W['p_attn'], name=f"{tag}_yc")
    merged = _gated_merge(hcat, W['gate_b'], ya, yb, yc, name=f"{tag}_gm", gcol=O_G // D)
    r, out, outm = _mm_res_ln(merged, W['w_out'], x1, W['ln2_g'], W['ln2_b'], 1.0, name=f"{tag}_mix")
    sv = dict(x1=x1m, dt_raw=dt_raw, pooled=pooled, t1=t1, act=act, dtg=dtg,
              hs=hs, yscan=yscan, ybn=ybn, hcat=hcat, outs=outs, lses=lses, ycp=ycp, ya=ya, yb=yb, yc=yc,
              merged=merged, r=r)
    return out, outm, sv


def _mixer_bwd(dout, sv, W, bias_all, tag, dep=None):
    T = dout.shape[0]
    gr = {}
    dx1a, dr, gr['ln2_g'], gr['ln2_b'] = _ln_bwd(sv['r'], W['ln2_g'], W['ln2_b'], dout, 1.0, name=f"{tag}_lnb")
    dmerged = _mm(dr, W['w_out'], tb=True, dep=dep, name=f"{tag}_dmerged")
    gr['w_out'] = _mm(sv['merged'], dr, ta=True, name=f"{tag}_dwout")
    dgates, dya, dyb, dyc, gr['gate_b'] = _gated_merge_bwd(sv['hcat'], W['gate_b'], sv['ya'], sv['yb'], sv['yc'],
                                                           dmerged, name=f"{tag}_gmb", gcol=O_G // D)
    dycp = _mm(dyc, W['p_attn'], tb=True, name=f"{tag}_dycp")
    gr['p_attn'] = _mm(sv['ycp'], dyc, ta=True, name=f"{tag}_dpattn")
    dml = _attn_merge(sv['outs'], sv['lses'], dycp, name=f"{tag}_amergeb")
    dq, dk, dv, dbias = [], [], [], []
    for gi in range(len(ATTN_DILS)):
        a, b, c, d = _attn_bwd(sv['hcat'], bias_all, gi, dml[gi], dml[3 + gi], name=f"{tag}_attnb{gi}")
        dq.append(a)
        dk.append(b)
        dv.append(c)
        dbias.append(d)
    dbias = jnp.concatenate(dbias, axis=0)
    dybn = _mm(dyb, W['p_ssd'], tb=True, name=f"{tag}_dybn")
    gr['p_ssd'] = _mm(sv['ybn'], dyb, ta=True, name=f"{tag}_dpssd")
    dyscan, dz, gr['ssd_norm'] = _ssd_gate_norm_bwd(sv['yscan'], sv['hcat'], W['ssd_norm'], dybn, name=f"{tag}_gnb",
                                                    zcol=O_Z // 256)
    dxs, ddtg, dbm, dcm, dak = _ssd_bwd(sv['act'], sv['dtg'], W['a_neg'], W['d_skip'], sv['hs'], dyscan,
                                        name=f"{tag}_ssdb")
    gr['a_neg'], gr['d_skip'] = dak[:, 0, 0:4], dak[:, 1, 0:4]
    ddt = ddtg.transpose(1, 0, 2).reshape(T, 16)
    ddt_raw, gr['dt_bias'] = _dt_softplus(sv['dt_raw'], W['dt_bias'], ddt, name=f"{tag}_dtb")
    dact = jnp.concatenate([dxs, dbm, dcm], axis=1)
    dxbc, gr['conv_w'], gr['conv_b'] = _conv_silu_bwd(sv['hcat'], W['conv_w'], W['conv_b'], dact, name=f"{tag}_convb",
                                                      col0=O_XBC // 128)
    dt2 = _mm(dya, W['p_pool'], tb=True, name=f"{tag}_dt2")
    dt1, t2, gr['pool_b'], gr['pool_scale'] = _pool_affine(sv['t1'], W['pool_b'], W['pool_scale'], dt2, name=f"{tag}_paffb")
    gr['p_pool'] = _mm(t2, dya, ta=True, name=f"{tag}_dppool")
    dpooled = _mm(dt1, W['pool_wbd'], tb=True, name=f"{tag}_dpooled")
    gr['pool_wbd'] = _mm(sv['pooled'], dt1, ta=True, name=f"{tag}_dpoolw")
    du = _pool_mean(dpooled, True, name=f"{tag}_poolb")
    dhcat = jnp.concatenate([t.astype(_ACT) for t in [du, dz, dxbc] + dq + dk + dv + [dgates, ddt_raw]]
                            + [jnp.zeros((T, HC - O_DT - 16), _ACT)], axis=1)
    dx1 = _mm(dhcat, W['w_in_r'], tb=True, add=dx1a, name=f"{tag}_dx1")
    gr['w_in_r'] = _mm(sv['x1'], dhcat, ta=True, name=f"{tag}_dwin")
    return dx1, gr, dbias


def _prep_layer_weights(i, inp, G):
    W = {}
    for n in BIG:
        if n not in G:
            continue
        g = G[n]
        if n == 'w_in':
            W['w_in_r'] = jnp.concatenate(_nat_pieces(g, 0, 3840) + _nat_pieces(g, 3856, 9232) + _nat_pieces(g, 3840, 3856)
                                          + [jnp.zeros((D, HC - 9232), g.dtype)], axis=1)
        elif n in COL_SHARDED:
            W[n] = jnp.concatenate([g[j] for j in range(4)], axis=1)
        else:
            W[n] = g.reshape(4 * g.shape[1], g.shape[2])
    pw = inp['pool_w'][i].astype(_MXU)
    wbd = jnp.zeros((POOLW, POOLW), _MXU)
    for g in range(4):
        wbd = lax.dynamic_update_slice(wbd, pw[g], (g * POOL_GDIM, g * POOL_GDIM))
    W['pool_wbd'] = wbd
    W['pool_b'] = inp['pool_b'][i].reshape(1, POOLW)
    W['pool_scale'] = inp['pool_scale'][i].reshape(1, POOLW)
    if 'conv_w' in G:
        W['conv_w'] = jnp.concatenate([G['conv_w'][j] for j in range(4)], axis=1)
        W['gate_b'] = jnp.concatenate([G['gate_b'][j][b:b + 1] for b in range(3) for j in range(4)], axis=1)
    W['conv_b'] = inp['conv_b'][i].reshape(1, 2048)
    W['dt_bias'] = inp['dt_bias'][i].reshape(1, 16)
    W['a_neg'] = (-jnp.exp(inp['a_log'][i])).reshape(4, 1, 4)
    W['d_skip'] = inp['d_skip'][i].reshape(4, 1, 4)
    W['ssd_norm'] = inp['ssd_norm'][i].reshape(1, D)
    for n in ('ln1_g', 'ln1_b', 'ln2_g', 'ln2_b', 'ln3_g', 'ln3_b'):
        W[n] = inp[n][i].reshape(1, D)
    return W


GATHER_FIRST = ['ffn1_w13', 'ffn1_w2']
GATHER_REST = [n for n in BIG if n not in GATHER_FIRST] + ['gate_b', 'conv_w']


def _gather_start(inp, i, names):
    core = lax.axis_index("c")
    arrs = []
    for n in names:
        s = inp[n][i]
        if n in BIG:
            s = lax.dynamic_slice_in_dim(s, core * (s.shape[0] // 2), s.shape[0] // 2, axis=0).astype(BF16)
        arrs.append(s)
    state, token = _exchange_start(arrs, "chips", "gather", name="gather_start")
    return (names, state), token


def _gather_mid(handle, after):
    names, state = handle
    me = 2 * lax.axis_index("x") + lax.axis_index("y")
    own, outs = _exchange_wait(state, after, "chips", "gather", name="gather_wait")
    outs = [lax.dynamic_update_slice(o, a[None], (me, 0, 0)) for o, a in zip(outs, own)]
    big = [o for n, o in zip(names, outs) if n in BIG]
    state, token = _exchange_start(big, "cores", "gather", name="share_start")
    return (names, outs, state), token


def _gather_finish(handle, after):
    names, outs, state = handle
    core = lax.axis_index("c")
    mine, theirs = _exchange_wait(state, after, "cores", "gather", name="share_wait")
    G = {n: o for n, o in zip(names, outs) if n not in BIG}
    for n, a, b in zip([n for n in names if n in BIG], mine, theirs):
        G[n] = jnp.concatenate([jnp.where(core == 0, a, b), jnp.where(core == 0, b, a)], axis=1)
    return G


W_IN_SHARD = 2308


def _nat_pieces(g, lo, hi):
    out = []
    for j in range(4):
        s, e = max(lo, W_IN_SHARD * j), min(hi, W_IN_SHARD * (j + 1))
        if s < e:
            out.append(g[j][:, s - W_IN_SHARD * j:e - W_IN_SHARD * j])
    return out


def _reord_ranges(lo, hi):
    out = []
    for a, b, off in ((0, 3840, 0), (3840, 3856, O_DT - 3840), (3856, 9232, -16)):
        s, e = max(lo, a), min(hi, b)
        if s < e:
            out.append((s + off, e + off))
    return out


def _halves_of(n, g):
    if n == 'w_in':
        shards = [jnp.concatenate([g[:, a:b] for a, b in _reord_ranges(W_IN_SHARD * j, W_IN_SHARD * (j + 1))], axis=1)
                  for j in range(4)]
    elif n in COL_SHARDED:
        c = g.shape[1] // 4
        shards = [g[:, j * c:(j + 1) * c] for j in range(4)]
    else:
        r = g.shape[0] // 4
        shards = [g[j * r:(j + 1) * r] for j in range(4)]
    r2 = shards[0].shape[0] // 2
    return jnp.stack([jnp.concatenate([s[h * r2:(h + 1) * r2] for s in shards], axis=0) for h in range(2)])


def _reduce_a(grads):
    names = list(grads)
    halves = [_halves_of(n, grads[n]) for n in names]
    state, token = _exchange_start(halves, "cores", "scatter", name="rsc_start")
    return (names, state), token


def _reduce_b(handle, after):
    names, state = handle
    core = lax.axis_index("c").reshape(1)
    halves, got = _exchange_wait(state, after, "cores", "scatter", name="rsc_wait")
    chip = [_sum_own_recv(h, t, core, BF16, name="rs_sum2") for h, t in zip(halves, got)]
    chip = [t.reshape(4, t.shape[0] // 4, t.shape[1]) for t in chip]
    state, token = _exchange_start(chip, "chips", "scatter", name="rs_start")
    return (names, state), token


def _reduce_c(handle, after):
    names, state = handle
    chip_id = (2 * lax.axis_index("x") + lax.axis_index("y")).reshape(1)
    chip, got = _exchange_wait(state, after, "chips", "scatter", name="rs_wait")
    red = [_sum_own_recv(h, t, chip_id, F32, name="rs_sum4") for h, t in zip(chip, got)]
    other = _exchange(red, "cores", "gather", name="rs_share")
    out = {}
    for n, mine, theirs in zip(names, red, other):
        out[n] = jnp.where(lax.axis_index("c") == 0, jnp.concatenate([mine, theirs]), jnp.concatenate([theirs, mine]))
    return out


class _Comm:
    def __init__(self, inp):
        self.inp = inp

    def gather_start(self, i, names):
        return _gather_start(self.inp, i, names)

    gather_mid = staticmethod(_gather_mid)
    gather_finish = staticmethod(_gather_finish)

    def reduce_a(self, i, grads):
        return _reduce_a({n: grads[n] for n in BIG})

    reduce_b = staticmethod(_reduce_b)
    reduce_c = staticmethod(_reduce_c)


def _allreduce_small(vec, dep=None):
    for group in ("cores", "x", "y"):
        recv = _exchange([vec], group, "gather", name=f"ar_{group}", dep=dep if group == "cores" else None)[0]
        vec = _rowwise(lambda tv, fv: ([tv[0] + tv[1]], []), [vec, recv], [], [(128, F32)], [], name=f"ar_add_{group}")[0]
    return vec


def _pack(arrs):
    flat = jnp.concatenate([a.reshape(-1) for a in arrs])
    n = flat.shape[0]
    pad = (-n) % (256 * 128)
    flat = jnp.concatenate([flat, jnp.zeros((pad,), F32)])
    return flat.reshape(-1, 128)


def _unpack(p, shapes):
    flat = p.reshape(-1)
    out, off = [], 0
    for s in shapes:
        sz = int(np.prod(s))
        out.append(flat[off:off + sz].reshape(s))
        off += sz
    return out


def _forward_backward(inp, comm, bias_all):
    x = xm = inp['x'].reshape(-1, D)
    tgt = inp['loss_target'].reshape(-1, D)
    saved, Ws = [], []
    h_first, _ = comm.gather_start(0, GATHER_FIRST)
    h_rest, dep = comm.gather_start(0, GATHER_REST)
    h_first, tok = comm.gather_mid(h_first, x)
    G = comm.gather_finish(h_first, tok)
    for i in range(NL):
        W = _prep_layer_weights(i, inp, G)
        start_next = lambda: (comm.gather_start(i + 1, BIG + ['gate_b', 'conv_w']) if i + 1 < NL else (None, None))
        if i > 0:
            h_next, dep = start_next()
        x1, x1m, s1 = _ffn_fwd(x, xm, W['ffn1_w13'], W['ffn1_w2'], W['ln1_g'], W['ln1_b'], "f1", dep)
        if i == 0:
            h_rest, tok = comm.gather_mid(h_rest, x1m)
            W.update(_prep_layer_weights(i, inp, comm.gather_finish(h_rest, tok)))
            h_next, dep = start_next()
        x2, x2m, s2 = _mixer_fwd(x1, x1m, W, bias_all, "mx", dep if i == 0 else None)
        dep = None
        if h_next is not None:
            h_next, dep = comm.gather_mid(h_next, x2m)
        x, xm, s3 = _ffn_fwd(x2, x2m, W['ffn2_w13'], W['ffn2_w2'], W['ln3_g'], W['ln3_b'], "f2", dep)
        if h_next is not None:
            G = comm.gather_finish(h_next, xm)
        saved.append((s1, s2, s3))
        Ws.append(W)
    dy, lpart = _loss_fwd_bwd(x, tgt, name="loss")
    fins, reduced, dbiases = [None] * NL, [None] * NL, [None] * NL
    pend_a, pend_b, dep = None, None, None
    for i in reversed(range(NL)):
        W = Ws[i]
        s1, s2, s3 = saved[i]
        g = {}
        dx2, f = _ffn_bwd(dy, s3, W['ffn2_w13'], W['ffn2_w2'], W['ln3_g'], W['ln3_b'], "f2", dep)
        g['ffn2_w13'], g['ffn2_w2'], g['ln3_g'], g['ln3_b'] = f['w13'], f['w2'], f['g'], f['b']
        dep = None
        if pend_a is not None:
            handle, dep = comm.reduce_b(pend_a[1], dx2)
            pend_b = (pend_a[0], handle)
        dx1, gm, dbiases[i] = _mixer_bwd(dx2, s2, W, bias_all, "mx", dep)
        g.update(gm)
        dy, f = _ffn_bwd(dx1, s1, W['ffn1_w13'], W['ffn1_w2'], W['ln1_g'], W['ln1_b'], "f1")
        g['ffn1_w13'], g['ffn1_w2'], g['ln1_g'], g['ln1_b'] = f['w13'], f['w2'], f['g'], f['b']
        fins[i] = _finish_layer_grads(i, g, inp)
        if pend_b is not None:
            reduced[pend_b[0]] = comm.reduce_c(pend_b[1], dy)
            pend_b = None
        handle, dep = comm.reduce_a(i, fins[i])
        pend_a = (i, handle)
    return lpart, dy, fins, reduced, pend_a, dbiases, dep


def _finish_layer_grads(i, g, inp):
    out = {n: g[n] for n in BIG if n != 'w_in'}
    out['w_in'] = g['w_in_r']
    out['pool_w'] = jnp.stack([g['pool_wbd'][k * POOL_GDIM:(k + 1) * POOL_GDIM, k * POOL_GDIM:(k + 1) * POOL_GDIM] for k in range(4)])
    out['pool_b'] = g['pool_b'].reshape(4, POOL_GDIM)
    out['pool_scale'] = g['pool_scale'].reshape(POOLW)
    out['conv_w'] = g['conv_w']
    out['conv_b'] = g['conv_b'].reshape(2048)
    out['dt_bias'] = g['dt_bias'].reshape(16)
    out['a_log'] = (g['a_neg'].reshape(16)) * (-jnp.exp(inp['a_log'][i]))
    out['d_skip'] = g['d_skip'].reshape(16)
    out['ssd_norm'] = g['ssd_norm'].reshape(D)
    out['gate_b'] = g['gate_b'].reshape(3, D)
    for n in ('ln1_g', 'ln1_b', 'ln2_g', 'ln2_b', 'ln3_g', 'ln3_b'):
        out[n] = g[n].reshape(D)
    return out


def kernel(x, ffn1_w13, ffn1_w2, ln1_g, ln1_b, w_in, gate_b, pool_w, pool_b, pool_scale, conv_w, conv_b,
           dt_bias, a_log, d_skip, ssd_norm, rel_bias, p_pool, p_ssd, p_attn, w_out, ln2_g, ln2_b, ffn2_w13,
           ffn2_w2, ln3_g, ln3_b, loss_target, m_ffn1_w13, m_ffn1_w2, m_ln1_g, m_ln1_b, m_w_in, m_gate_b,
           m_pool_w, m_pool_b, m_pool_scale, m_conv_w, m_conv_b, m_dt_bias, m_a_log, m_d_skip, m_ssd_norm,
           m_rel_bias, m_p_pool, m_p_ssd, m_p_attn, m_w_out, m_ln2_g, m_ln2_b, m_ffn2_w13, m_ffn2_w2, m_ln3_g,
           m_ln3_b, v_ffn1_w13, v_ffn1_w2, v_ln1_g, v_ln1_b, v_w_in, v_gate_b, v_pool_w, v_pool_b,
           v_pool_scale, v_conv_w, v_conv_b, v_dt_bias, v_a_log, v_d_skip, v_ssd_norm, v_rel_bias, v_p_pool,
           v_p_ssd, v_p_attn, v_w_out, v_ln2_g, v_ln2_b, v_ffn2_w13, v_ffn2_w2, v_ln3_g, v_ln3_b):
    inp = dict(locals())
    maps = jnp.asarray(_bucket_maps())
    bias_all = _bias_build(rel_bias, maps, name="bias_build")
    comm = _Comm(inp)
    lpart, gx, fins, red, pending, dbiases, halves_started = _forward_backward(inp, comm, bias_all)
    loss = lax.psum(lpart[0, 0], ("x", "y", "c"))

    small_l = [n for n in SMALL if n != 'rel_bias']
    drel = _bias_reduce(jnp.stack(dbiases), maps, name="bias_reduce")[:, 0, :32].T
    small_arrs = [jnp.stack([fins[i][n] for i in range(NL)]) for n in small_l] + [drel]
    packed = _allreduce_small(_pack(small_arrs), dep=halves_started)
    handle_b, started = comm.reduce_b(pending[1], packed)
    gsmall = dict(zip(small_l + ['rel_bias'], _unpack(packed, [a.shape for a in small_arrs])))
    shard = 2 * lax.axis_index("x") + lax.axis_index("y")
    gsmall['gate_b'] = lax.dynamic_slice_in_dim(gsmall['gate_b'], shard * 256, 256, axis=2)
    gsmall['conv_w'] = lax.dynamic_slice_in_dim(gsmall['conv_w'], shard * 512, 512, axis=2)
    gout, delta, new_m, new_v = dict(gsmall), {}, {}, {}
    shapes = [inp[n].shape for n in SMALL]
    d, m, v = _adamw(_pack([inp[n] for n in SMALL]), _pack([gsmall[n] for n in SMALL]),
                     _pack([inp['m_' + n] for n in SMALL]), _pack([inp['v_' + n] for n in SMALL]), name="adamw_small")
    for n, dd, mm, vv in zip(SMALL, _unpack(d, shapes), _unpack(m, shapes), _unpack(v, shapes)):
        delta[n], new_m[n], new_v[n] = dd, mm, vv

    two_d = lambda a: a.reshape(a.shape[0] * a.shape[1], a.shape[2])
    accs = {n: None for n in BIG}

    def adamw_layer(i, dep=None):
        for n in BIG:
            accs[n] = _adamw_layer(i, two_d(inp[n]), red[i][n], two_d(inp['m_' + n]), two_d(inp['v_' + n]), accs[n],
                                   name="adamw_big", dep=dep)

    done = [i for i in range(NL) if i != pending[0]]
    for i in done:
        adamw_layer(i, dep=started)
    red[pending[0]] = comm.reduce_c(handle_b, [d] + ([accs[n][1] for n in BIG] if done else []))
    adamw_layer(pending[0])
    for n in BIG:
        gout[n], delta[n], new_m[n], new_v[n] = [a.reshape(inp[n].shape) for a in accs[n]]

    return (loss, gx.reshape(x.shape), *[gout[n] for n in WEIGHTS], *[delta[n] for n in WEIGHTS],
            *[new_m[n] for n in WEIGHTS], *[new_v[n] for n in WEIGHTS])
```
